```python
import math
import jax, jax.numpy as jnp
from jax import lax
import numpy as np

D_MODEL = 2048
BATCH = 2
SEQ = 4096
DEPTH = 4
DEC_BATCH = 8
DEC_SEQ = 8
PAST_LEN = 16384
PAGE_SIZE = 128

N_MIXERS = 3
N_DIFF = (DEPTH + 2) // 3
N_SSM = (DEPTH + 1) // 3
N_NSA = DEPTH // 3

N_HEADS = 16
N_KV = 4
HPG = N_HEADS // N_KV
Q_BLOCK = 128
DIFF_DH = D_MODEL // N_HEADS // 2
DIFF_DV = 2 * DIFF_DH
DIFF_IN = N_HEADS * 2 * DIFF_DH + N_KV * 2 * DIFF_DH + N_KV * DIFF_DV
N_BUCKETS = 32
MAX_EXACT = N_BUCKETS // 2
MAX_DIST = 128
D_INNER = 2 * D_MODEL
SSM_HEADDIM = 64
SSM_HEADS = D_INNER // SSM_HEADDIM
SSM_GROUPS = 8
SSM_HPG = SSM_HEADS // SSM_GROUPS
D_STATE = 128
CONV_W = 4
CONV_DIM = D_INNER + 2 * SSM_GROUPS * D_STATE
SSM_IN = D_INNER + CONV_DIM + SSM_HEADS
SSM_CHUNK = 128
NSA_DK = D_MODEL // N_HEADS
NSA_DV = NSA_DK
CMP_STRIDE = 16
CMP_BLOCK = 2 * CMP_STRIDE
CMP_HIDDEN = NSA_DK
SLC_BLOCK = 64
SLC_RATIO = SLC_BLOCK // CMP_STRIDE
N_SELECT = 16
WINDOW = 512
NSA_IN = N_HEADS * NSA_DK + 6 * N_KV * NSA_DK + 3 * N_HEADS
N_MEM = 256
X_HEADS = 4
X_DH = 128
D_FF = -(-8 * D_MODEL // (3 * 256)) * 256

NEG_INF = -1e30
FORCE_SCORE = 1e4

kernel_name = 'hybrid_diffattn_ssd_nsa_decoder_step'


def rmsnorm(x, g, eps=1e-6):
    xf = x.astype(jnp.float32)
    y = xf * lax.rsqrt(jnp.mean(xf * xf, axis=-1, keepdims=True) + eps)
    return y.astype(x.dtype) * g


def masked_softmax(logits, mask):
    l = jnp.where(mask, logits.astype(jnp.float32), NEG_INF)
    m = jnp.max(l, axis=-1, keepdims=True)
    e = jnp.where(mask, jnp.exp(l - m), 0.0)
    return e / jnp.maximum(jnp.sum(e, axis=-1, keepdims=True), 1e-30)


def rel_bucket(dist):
    n = jnp.maximum(dist, 0)
    nf = jnp.maximum(n, 1).astype(jnp.float32)
    large = MAX_EXACT + (jnp.log(nf / MAX_EXACT) / math.log(MAX_DIST / MAX_EXACT)
                         * (N_BUCKETS - MAX_EXACT)).astype(jnp.int32)
    return jnp.where(n < MAX_EXACT, n, jnp.minimum(large, N_BUCKETS - 1))


def rel_bias(table, q_pos, k_pos):
    return jnp.moveaxis(table[rel_bucket(q_pos[:, None] - k_pos[None, :])], -1, 0).astype(jnp.float32)


def blockwise(fn, q_args, q_pos):
    t = q_pos.shape[0]
    if t <= Q_BLOCK:
        return fn(q_args, q_pos)
    nb = t // Q_BLOCK

    def split(a):
        return jnp.moveaxis(a.reshape(a.shape[0], nb, Q_BLOCK, *a.shape[2:]), 1, 0)

    out = lax.map(lambda blk: fn(blk[0], blk[1]),
                  (tuple(split(a) for a in q_args), q_pos.reshape(nb, Q_BLOCK)))
    out = jnp.moveaxis(out, 0, 1)
    return out.reshape(out.shape[0], t, *out.shape[3:])


def paged_rows(cache, layer, page_table):
    rows = cache[layer, page_table]
    return rows.reshape(rows.shape[0], rows.shape[1] * rows.shape[2], *rows.shape[3:])


def diff_project(h, w_in, q_norm, k_norm):
    b, t, _ = h.shape
    qkv = h @ w_in
    nq, nk = N_HEADS * 2 * DIFF_DH, N_KV * 2 * DIFF_DH
    q = rmsnorm(qkv[..., :nq].reshape(b, t, N_HEADS, 2, DIFF_DH), q_norm)
    k = rmsnorm(qkv[..., nq:nq + nk].reshape(b, t, N_KV, 2, DIFF_DH), k_norm)
    v = qkv[..., nq + nk:].reshape(b, t, N_KV, DIFF_DV)
    return q, k.reshape(b, t, N_KV, 2 * DIFF_DH), v


def diff_core(q, k, v, q_pos, k_pos, table, lam, lam_init, sub_norm):
    b, tq = q.shape[:2]
    tk = k.shape[1]
    qg = q.reshape(b, tq, N_KV, HPG, 2, DIFF_DH)
    kg = k.reshape(b, tk, N_KV, 2, DIFF_DH)
    logits = jnp.einsum('bqgrmd,bkgmd->bmgrqk', qg, kg).astype(jnp.float32) * DIFF_DH ** -0.5
    logits = logits + rel_bias(table, q_pos, k_pos).reshape(N_KV, HPG, tq, tk)
    p = masked_softmax(logits, k_pos[None, :] <= q_pos[:, None])
    a = p[:, 0] - lam * p[:, 1]
    o = jnp.einsum('bgrqk,bkgd->bqgrd', a.astype(v.dtype), v).reshape(b, tq, N_HEADS, DIFF_DV)
    o = rmsnorm(o, sub_norm) * (1.0 - lam_init)
    return o.reshape(b, tq, N_HEADS * DIFF_DV)


def diff_attention(h, q_pos, k_past, v_past, w_in, q_norm, k_norm, lam_p, sub_norm, w_out,
                   table, lam_init):
    q, k, v = diff_project(h, w_in, q_norm, k_norm)
    if k_past is None:
        k_all, v_all, k_pos = k, v, q_pos
    else:
        k_all = jnp.concatenate([k_past.astype(k.dtype), k], axis=1)
        v_all = jnp.concatenate([v_past.astype(v.dtype), v], axis=1)
        k_pos = jnp.arange(k_all.shape[1], dtype=jnp.int32)
    lf = lam_p.astype(jnp.float32)
    lam = jnp.exp(jnp.sum(lf[0] * lf[1])) - jnp.exp(jnp.sum(lf[2] * lf[3])) + lam_init
    o = blockwise(lambda qa, qp: diff_core(qa[0], k_all, v_all, qp, k_pos, table, lam, lam_init, sub_norm),
                  (q,), q_pos)
    return o @ w_out, k, v


def ssd_scan(x, dt, a, bm, cm, h0, chunk):
    b, t = x.shape[:2]
    nc = t // chunk

    def to_chunks(z):
        return jnp.moveaxis(z.reshape(b, nc, chunk, *z.shape[2:]), 1, 0)

    xs = (to_chunks(x.astype(jnp.float32)), to_chunks(dt),
          to_chunks(bm.astype(jnp.float32)), to_chunks(cm.astype(jnp.float32)))
    causal = jnp.tril(jnp.ones((chunk, chunk), bool))[None, :, :, None, None]

    def step(h, inp):
        xc, dtc, bc, cc = inp
        cs = jnp.cumsum(dtc * a, axis=1)
        seg = cs[:, :, None] - cs[:, None, :]
        decay = jnp.where(causal, jnp.exp(jnp.where(causal, seg, 0.0)), 0.0)
        cb = jnp.einsum('blgn,bsgn->blsg', cc, bc)
        m = cb[..., None] * decay * dtc[:, None]
        y_diag = jnp.einsum('blsgr,bsgrp->blgrp', m, xc)
        y_off = jnp.einsum('blgn,bgrpn->blgrp', cc, h) * jnp.exp(cs)[..., None]
        w = jnp.exp(cs[:, -1:] - cs) * dtc
        h_new = (jnp.exp(cs[:, -1])[..., None, None] * h
                 + jnp.einsum('bsgrp,bsgn->bgrpn', xc * w[..., None], bc))
        return h_new, y_diag + y_off

    h_last, ys = lax.scan(step, h0.astype(jnp.float32), xs)
    y = jnp.moveaxis(ys, 0, 1).reshape(x.shape)
    return y.astype(x.dtype), h_last


def ssd_mixer(h, conv_buf, ssm_state, w_in, conv_w, conv_b, dt_bias, a_log, d_skip, norm_g, w_out):
    b, t, _ = h.shape
    zxbcdt = h @ w_in
    z = zxbcdt[..., :D_INNER]
    xbc = zxbcdt[..., D_INNER:D_INNER + CONV_DIM]
    dt_raw = zxbcdt[..., D_INNER + CONV_DIM:]
    xpad = jnp.concatenate([conv_buf.astype(xbc.dtype), xbc], axis=1)
    conv = conv_b + sum(xpad[:, k:k + t] * conv_w[k] for k in range(CONV_W))
    xbc = jax.nn.silu(conv)
    ng = SSM_GROUPS * D_STATE
    xh = xbc[..., :D_INNER].reshape(b, t, SSM_GROUPS, SSM_HPG, SSM_HEADDIM)
    bm = xbc[..., D_INNER:D_INNER + ng].reshape(b, t, SSM_GROUPS, D_STATE)
    cm = xbc[..., D_INNER + ng:].reshape(b, t, SSM_GROUPS, D_STATE)
    dt = jax.nn.softplus(dt_raw.astype(jnp.float32) + dt_bias.astype(jnp.float32))
    dt = dt.reshape(b, t, SSM_GROUPS, SSM_HPG)
    a = -jnp.exp(a_log.astype(jnp.float32)).reshape(SSM_GROUPS, SSM_HPG)
    h0 = ssm_state.reshape(b, SSM_GROUPS, SSM_HPG, SSM_HEADDIM, D_STATE)
    y, h_last = ssd_scan(xh, dt, a, bm, cm, h0, math.gcd(t, SSM_CHUNK))
    y = y + d_skip.reshape(SSM_GROUPS, SSM_HPG, 1) * xh
    y = y.reshape(b, t, D_INNER) * jax.nn.silu(z)
    y = rmsnorm(y.reshape(b, t, SSM_GROUPS, D_INNER // SSM_GROUPS),
                norm_g.reshape(SSM_GROUPS, D_INNER // SSM_GROUPS)).reshape(b, t, D_INNER)
    return y @ w_out, xpad[:, t:], h_last.reshape(b, SSM_HEADS, SSM_HEADDIM, D_STATE)


def nsa_project(h, w_in, q_norm, k_norm):
    b, t, _ = h.shape
    out = h @ w_in
    nq, nkv = N_HEADS * NSA_DK, 6 * N_KV * NSA_DK
    q = rmsnorm(out[..., :nq].reshape(b, t, N_HEADS, NSA_DK), q_norm)
    kv = out[..., nq:nq + nkv].reshape(b, t, 6, N_KV, NSA_DK)
    gates = jax.nn.sigmoid(out[..., nq + nkv:].astype(jnp.float32)).reshape(b, t, 3, N_HEADS)
    k_slc = rmsnorm(kv[:, :, 2], k_norm[1])
    k_win = rmsnorm(kv[:, :, 4], k_norm[2])
    return q, gates, kv[:, :, 0], kv[:, :, 1], k_slc, kv[:, :, 3], k_win, kv[:, :, 5]


def compress(rows, pe, w1, w2):
    b, s = rows.shape[:2]
    n_sub = s // CMP_STRIDE
    sub = rows[:, :n_sub * CMP_STRIDE].reshape(b, n_sub, CMP_STRIDE, N_KV, NSA_DK)
    pe = pe.reshape(2, CMP_STRIDE, NSA_DK)
    w1 = w1.reshape(2, CMP_STRIDE, NSA_DK, CMP_HIDDEN)
    first = jnp.einsum('bnlgd,lde->bnge', sub[:, :-1] + pe[0][:, None, :], w1[0])
    second = jnp.einsum('bnlgd,lde->bnge', sub[:, 1:] + pe[1][:, None, :], w1[1])
    return jax.nn.silu(first + second) @ w2


def to_blocks(rows):
    b, s = rows.shape[:2]
    n = -(-s // SLC_BLOCK)
    rows = jnp.pad(rows, ((0, 0), (0, n * SLC_BLOCK - s), (0, 0), (0, 0)))
    return jnp.transpose(rows.reshape(b, n, SLC_BLOCK, N_KV, NSA_DK), (0, 3, 1, 2, 4))


def nsa_core(q, gates, q_pos, kc, vc, c_pos, ks_blk, vs_blk, kw, vw, w_pos, table):
    b, tq = q.shape[:2]
    qg = q.reshape(b, tq, N_KV, HPG, NSA_DK)
    scale = NSA_DK ** -0.5
    lc = jnp.einsum('bqgrd,bngd->bgrqn', qg, kc).astype(jnp.float32) * scale
    lc = lc + rel_bias(table, q_pos, c_pos).reshape(N_KV, HPG, tq, -1)
    pc = masked_softmax(lc, c_pos[None, :] <= q_pos[:, None])
    o_cmp = jnp.einsum('bgrqn,bngd->bqgrd', pc.astype(vc.dtype), vc)
    n_cmp, n_slc = kc.shape[1], ks_blk.shape[2]
    imp = jnp.pad(pc.sum(axis=2), ((0, 0), (0, 0), (0, 0), (1, SLC_RATIO * n_slc - n_cmp)))
    idx = SLC_RATIO * jnp.arange(n_slc)[:, None] + jnp.arange(SLC_RATIO + 1)[None, :]
    w_imp = jnp.array([1.0] + [2.0] * (SLC_RATIO - 1) + [1.0], jnp.float32)
    s_slc = jnp.einsum('bgqjm,m->bgqj', imp[..., idx], w_imp)
    blk = jnp.arange(n_slc)
    qb = (q_pos // SLC_BLOCK)[:, None]
    forced = (blk == 0) | (blk == qb) | (blk == qb - 1)
    score = jnp.where(forced, FORCE_SCORE, jnp.where(blk * SLC_BLOCK <= q_pos[:, None], s_slc, -1.0))
    _, sel = lax.top_k(score, min(N_SELECT, n_slc))
    n_sel = sel.shape[-1]
    bi = jnp.arange(b)[:, None, None, None]
    gi = jnp.arange(N_KV)[None, :, None, None]
    kg = ks_blk[bi, gi, sel].reshape(b, N_KV, tq, n_sel * SLC_BLOCK, NSA_DK)
    vg = vs_blk[bi, gi, sel].reshape(b, N_KV, tq, n_sel * SLC_BLOCK, NSA_DV)
    pos_s = (sel[..., None] * SLC_BLOCK + jnp.arange(SLC_BLOCK)).reshape(b, N_KV, tq, n_sel * SLC_BLOCK)
    dist = q_pos[:, None] - pos_s
    tbl = table.reshape(N_BUCKETS, N_KV, HPG)
    ls = jnp.einsum('bqgrd,bgqnd->bgrqn', qg, kg).astype(jnp.float32) * scale
    ls = ls + jnp.moveaxis(tbl[rel_bucket(dist), gi], -1, 2).astype(jnp.float32)
    ps = masked_softmax(ls, (dist >= 0)[:, :, None])
    o_slc = jnp.einsum('bgrqn,bgqnd->bqgrd', ps.astype(vg.dtype), vg)
    lw = jnp.einsum('bqgrd,bkgd->bgrqk', qg, kw).astype(jnp.float32) * scale
    lw = lw + rel_bias(table, q_pos, w_pos).reshape(N_KV, HPG, tq, -1)
    dw = q_pos[:, None] - w_pos[None, :]
    pw = masked_softmax(lw, (dw >= 0) & (dw < WINDOW) & (w_pos[None, :] >= 0))
    o_win = jnp.einsum('bgrqk,bkgd->bqgrd', pw.astype(vw.dtype), vw)
    g = gates.reshape(b, tq, 3, N_KV, HPG, 1).astype(o_cmp.dtype)
    o = g[:, :, 0] * o_cmp + g[:, :, 1] * o_slc + g[:, :, 2] * o_win
    return o.reshape(b, tq, N_HEADS * NSA_DV)


def nsa_attention(h, q_pos, past, w_in, q_norm, k_norm, cmp_pe, cmp_w1, cmp_w2, w_out, table):
    q, gates, kcr, vcr, ks, vs, kw, vw = nsa_project(h, w_in, q_norm, k_norm)
    if past is None:
        kcr_all, vcr_all, ks_all, vs_all = kcr, vcr, ks, vs
        kw_src = jnp.pad(kw, ((0, 0), (WINDOW, 0), (0, 0), (0, 0)))
        vw_src = jnp.pad(vw, ((0, 0), (WINDOW, 0), (0, 0), (0, 0)))
    else:
        pk_c, pv_c, pk_s, pv_s, buf_k, buf_v = past
        kcr_all = jnp.concatenate([pk_c.astype(kcr.dtype), kcr], axis=1)
        vcr_all = jnp.concatenate([pv_c.astype(vcr.dtype), vcr], axis=1)
        ks_all = jnp.concatenate([pk_s.astype(ks.dtype), ks], axis=1)
        vs_all = jnp.concatenate([pv_s.astype(vs.dtype), vs], axis=1)
        kw_src = jnp.concatenate([buf_k.astype(kw.dtype), kw], axis=1)
        vw_src = jnp.concatenate([buf_v.astype(vw.dtype), vw], axis=1)
    kc = rmsnorm(compress(kcr_all, cmp_pe[0], cmp_w1[0], cmp_w2[0]), k_norm[0])
    vc = compress(vcr_all, cmp_pe[1], cmp_w1[1], cmp_w2[1])
    c_pos = jnp.arange(kc.shape[1], dtype=jnp.int32) * CMP_STRIDE + CMP_BLOCK - 1
    ks_blk, vs_blk = to_blocks(ks_all), to_blocks(vs_all)

    def block_fn(qa, qp):
        if past is None:
            tb = qp.shape[0]
            start = qp[0]
            kwb = lax.dynamic_slice_in_dim(kw_src, start, WINDOW + tb, axis=1)
            vwb = lax.dynamic_slice_in_dim(vw_src, start, WINDOW + tb, axis=1)
            w_pos = start - WINDOW + jnp.arange(WINDOW + tb, dtype=jnp.int32)
        else:
            kwb, vwb = kw_src, vw_src
            w_pos = q_pos[0] - WINDOW + jnp.arange(kw_src.shape[1], dtype=jnp.int32)
        return nsa_core(qa[0], qa[1], qp, kc, vc, c_pos, ks_blk, vs_blk, kwb, vwb, w_pos, table)

    o = blockwise(block_fn, (q, gates), q_pos)
    return o @ w_out, (kcr, vcr, ks, vs, kw_src[:, -WINDOW:], vw_src[:, -WINDOW:])


def mem_kv(mem, g_mem, w_k, w_v, k_norm):
    b, n, _ = mem.shape
    m = rmsnorm(mem, g_mem)
    k = rmsnorm((m @ w_k).reshape(b, n, X_HEADS, X_DH), k_norm)
    v = (m @ w_v).reshape(b, n, X_HEADS, X_DH)
    return k, v


def mem_attention(h, k, v, w_q, q_norm, w_o):
    b, t, _ = h.shape
    q = rmsnorm((h @ w_q).reshape(b, t, X_HEADS, X_DH), q_norm)
    logits = jnp.einsum('bqhd,bkhd->bhqk', q, k.astype(q.dtype)).astype(jnp.float32) * X_DH ** -0.5
    p = jax.nn.softmax(logits, axis=-1)
    o = jnp.einsum('bhqk,bkhd->bqhd', p.astype(h.dtype), v.astype(h.dtype))
    return o.reshape(b, t, X_HEADS * X_DH) @ w_o


def swiglu(h, w1, w3, w2):
    return (jax.nn.silu(h @ w1) * (h @ w3)) @ w2


def stack0(arrs):
    return jnp.stack(arrs, axis=0)


def setup_inputs(seed: int = 0) -> dict:
    keys = list(jax.random.split(jax.random.key(seed), 80))

    def nxt():
        return keys.pop()

    def nrm(shape, scale=1.0):
        return jax.random.normal(nxt(), shape, jnp.float32) * scale

    def gain(shape):
        return 1.0 + nrm(shape, 0.05)

    n_pages = PAST_LEN // PAGE_SIZE
    n_used = DEC_BATCH * n_pages
    n_phys = (5 * n_used + 3) // 4
    page_table = jax.random.permutation(nxt(), n_phys)[:n_used].reshape(DEC_BATCH, n_pages).astype(jnp.int32)
    dt = jnp.exp(jax.random.uniform(nxt(), (N_SSM, SSM_HEADS), jnp.float32, math.log(1e-3), math.log(1e-1)))
    a_init = jax.random.uniform(nxt(), (N_SSM, SSM_HEADS), jnp.float32, 1.0, 16.0)
    return {
        'x_prompt': nrm((BATCH, SEQ, D_MODEL)),
        'x_sample': nrm((DEC_BATCH, DEC_SEQ, D_MODEL)),
        'cache_diff_k': nrm((N_DIFF, n_phys, PAGE_SIZE, N_KV, 2 * DIFF_DH)),
        'cache_diff_v': nrm((N_DIFF, n_phys, PAGE_SIZE, N_KV, DIFF_DV)),
        'state_ssm': nrm((N_SSM, DEC_BATCH, SSM_HEADS, SSM_HEADDIM, D_STATE), 0.1),
        'state_conv': nrm((N_SSM, DEC_BATCH, CONV_W - 1, CONV_DIM)),
        'cache_nsa_cmp_k': nrm((N_NSA, n_phys, PAGE_SIZE, N_KV, NSA_DK)),
        'cache_nsa_cmp_v': nrm((N_NSA, n_phys, PAGE_SIZE, N_KV, NSA_DV)),
        'cache_nsa_slc_k': nrm((N_NSA, n_phys, PAGE_SIZE, N_KV, NSA_DK)),
        'cache_nsa_slc_v': nrm((N_NSA, n_phys, PAGE_SIZE, N_KV, NSA_DV)),
        'cache_nsa_win_k': nrm((N_NSA, DEC_BATCH, WINDOW, N_KV, NSA_DK)),
        'cache_nsa_win_v': nrm((N_NSA, DEC_BATCH, WINDOW, N_KV, NSA_DV)),
        'cache_mem_k': nrm((DEPTH, DEC_BATCH, N_MEM, X_HEADS, X_DH)),
        'cache_mem_v': nrm((DEPTH, DEC_BATCH, N_MEM, X_HEADS, X_DH)),
        'page_table': page_table,
        'mem_prompt': nrm((BATCH, N_MEM, D_MODEL)),
        'rel_bias_table': nrm((N_BUCKETS, N_HEADS), 0.3),
        'norm_mix': gain((DEPTH, D_MODEL)),
        'norm_xattn': gain((DEPTH, D_MODEL)),
        'norm_mem': gain((DEPTH, D_MODEL)),
        'norm_ffn': gain((DEPTH, D_MODEL)),
        'diff_w_in': nrm((N_DIFF, D_MODEL, DIFF_IN), D_MODEL ** -0.5),
        'diff_q_norm': gain((N_DIFF, DIFF_DH)),
        'diff_k_norm': gain((N_DIFF, DIFF_DH)),
        'diff_lambda': nrm((N_DIFF, 4, DIFF_DH), 0.1),
        'diff_sub_norm': gain((N_DIFF, DIFF_DV)),
        'diff_w_out': nrm((N_DIFF, N_HEADS * DIFF_DV, D_MODEL), (N_HEADS * DIFF_DV) ** -0.5),
        'ssm_w_in': nrm((N_SSM, D_MODEL, SSM_IN), D_MODEL ** -0.5),
        'ssm_conv_w': nrm((N_SSM, CONV_W, CONV_DIM), CONV_W ** -0.5),
        'ssm_conv_b': nrm((N_SSM, CONV_DIM), 0.02),
        'ssm_dt_bias': dt + jnp.log(-jnp.expm1(-dt)),
        'ssm_a_log': jnp.log(a_init),
        'ssm_d': gain((N_SSM, SSM_HEADS)),
        'ssm_norm': gain((N_SSM, D_INNER)),
        'ssm_w_out': nrm((N_SSM, D_INNER, D_MODEL), D_INNER ** -0.5),
        'nsa_w_in': nrm((N_NSA, D_MODEL, NSA_IN), D_MODEL ** -0.5),
        'nsa_q_norm': gain((N_NSA, NSA_DK)),
        'nsa_k_norm': gain((N_NSA, 3, NSA_DK)),
        'nsa_cmp_pe': nrm((N_NSA, 2, CMP_BLOCK, NSA_DK), 0.1),
        'nsa_cmp_w1': nrm((N_NSA, 2, CMP_BLOCK * NSA_DK, CMP_HIDDEN), (CMP_BLOCK * NSA_DK) ** -0.5),
        'nsa_cmp_w2': nrm((N_NSA, 2, CMP_HIDDEN, NSA_DK), CMP_HIDDEN ** -0.5),
        'nsa_w_out': nrm((N_NSA, N_HEADS * NSA_DV, D_MODEL), (N_HEADS * NSA_DV) ** -0.5),
        'xattn_w_q': nrm((DEPTH, D_MODEL, X_HEADS * X_DH), D_MODEL ** -0.5),
        'xattn_w_k': nrm((DEPTH, D_MODEL, X_HEADS * X_DH), D_MODEL ** -0.5),
        'xattn_w_v': nrm((DEPTH, D_MODEL, X_HEADS * X_DH), D_MODEL ** -0.5),
        'xattn_q_norm': gain((DEPTH, X_DH)),
        'xattn_k_norm': gain((DEPTH, X_DH)),
        'xattn_w_o': nrm((DEPTH, X_HEADS * X_DH, D_MODEL), (X_HEADS * X_DH) ** -0.5),
        'ffn_w1': nrm((DEPTH, D_MODEL, D_FF), D_MODEL ** -0.5),
        'ffn_w3': nrm((DEPTH, D_MODEL, D_FF), D_MODEL ** -0.5),
        'ffn_w2': nrm((DEPTH, D_FF, D_MODEL), D_FF ** -0.5),
    }


def reference(x_prompt, x_sample, cache_diff_k, cache_diff_v, state_ssm, state_conv,
              cache_nsa_cmp_k, cache_nsa_cmp_v, cache_nsa_slc_k, cache_nsa_slc_v,
              cache_nsa_win_k, cache_nsa_win_v, cache_mem_k, cache_mem_v, page_table, mem_prompt,
              rel_bias_table, norm_mix, norm_xattn, norm_mem, norm_ffn,
              diff_w_in, diff_q_norm, diff_k_norm, diff_lambda, diff_sub_norm, diff_w_out,
              ssm_w_in, ssm_conv_w, ssm_conv_b, ssm_dt_bias, ssm_a_log, ssm_d, ssm_norm, ssm_w_out,
              nsa_w_in, nsa_q_norm, nsa_k_norm, nsa_cmp_pe, nsa_cmp_w1, nsa_cmp_w2, nsa_w_out,
              xattn_w_q, xattn_w_k, xattn_w_v, xattn_q_norm, xattn_k_norm, xattn_w_o,
              ffn_w1, ffn_w3, ffn_w2):
    pos_p = jnp.arange(SEQ, dtype=jnp.int32)
    pos_s = PAST_LEN + jnp.arange(DEC_SEQ, dtype=jnp.int32)
    xp, xs = x_prompt, x_sample
    dkp, dvp, dks, dvs = [], [], [], []
    ssp, cvp, sss, cvs = [], [], [], []
    nsp = [[] for _ in range(6)]
    nss = [[] for _ in range(6)]
    mkp, mvp = [], []
    for i in range(DEPTH):
        kind, j = i % N_MIXERS, i // N_MIXERS
        hp, hs = rmsnorm(xp, norm_mix[i]), rmsnorm(xs, norm_mix[i])
        if kind == 0:
            lam_init = 0.8 - 0.6 * math.exp(-0.3 * i)
            w = (diff_w_in[j], diff_q_norm[j], diff_k_norm[j], diff_lambda[j], diff_sub_norm[j],
                 diff_w_out[j], rel_bias_table, lam_init)
            yp, k_new, v_new = diff_attention(hp, pos_p, None, None, *w)
            dkp.append(k_new)
            dvp.append(v_new)
            ys, k_new, v_new = diff_attention(hs, pos_s, paged_rows(cache_diff_k, j, page_table),
                                              paged_rows(cache_diff_v, j, page_table), *w)
            dks.append(k_new)
            dvs.append(v_new)
        elif kind == 1:
            w = (ssm_w_in[j], ssm_conv_w[j], ssm_conv_b[j], ssm_dt_bias[j], ssm_a_log[j], ssm_d[j],
                 ssm_norm[j], ssm_w_out[j])
            yp, cb, st = ssd_mixer(hp, jnp.zeros((BATCH, CONV_W - 1, CONV_DIM), xp.dtype),
                                   jnp.zeros((BATCH, SSM_HEADS, SSM_HEADDIM, D_STATE), jnp.float32), *w)
            cvp.append(cb)
            ssp.append(st)
            ys, cb, st = ssd_mixer(hs, state_conv[j], state_ssm[j], *w)
            cvs.append(cb)
            sss.append(st)
        else:
            w = (nsa_w_in[j], nsa_q_norm[j], nsa_k_norm[j], nsa_cmp_pe[j], nsa_cmp_w1[j], nsa_cmp_w2[j],
                 nsa_w_out[j], rel_bias_table)
            yp, st = nsa_attention(hp, pos_p, None, *w)
            for a_idx in range(6):
                nsp[a_idx].append(st[a_idx])
            past = (paged_rows(cache_nsa_cmp_k, j, page_table), paged_rows(cache_nsa_cmp_v, j, page_table),
                    paged_rows(cache_nsa_slc_k, j, page_table), paged_rows(cache_nsa_slc_v, j, page_table),
                    cache_nsa_win_k[j], cache_nsa_win_v[j])
            ys, st = nsa_attention(hs, pos_s, past, *w)
            for a_idx in range(6):
                nss[a_idx].append(st[a_idx])
        xp, xs = xp + yp, xs + ys
        mk, mv = mem_kv(mem_prompt, norm_mem[i], xattn_w_k[i], xattn_w_v[i], xattn_k_norm[i])
        mkp.append(mk)
        mvp.append(mv)
        xp = xp + mem_attention(rmsnorm(xp, norm_xattn[i]), mk, mv, xattn_w_q[i], xattn_q_norm[i], xattn_w_o[i])
        xs = xs + mem_attention(rmsnorm(xs, norm_xattn[i]), cache_mem_k[i], cache_mem_v[i],
                                xattn_w_q[i], xattn_q_norm[i], xattn_w_o[i])
        xp = xp + swiglu(rmsnorm(xp, norm_ffn[i]), ffn_w1[i], ffn_w3[i], ffn_w2[i])
        xs = xs + swiglu(rmsnorm(xs, norm_ffn[i]), ffn_w1[i], ffn_w3[i], ffn_w2[i])
    return (xp, xs,
            stack0(dkp), stack0(dvp), stack0(dks), stack0(dvs),
            stack0(ssp), stack0(cvp), stack0(sss), stack0(cvs),
            stack0(nsp[0]), stack0(nsp[1]), stack0(nsp[2]), stack0(nsp[3]), stack0(nsp[4]), stack0(nsp[5]),
            stack0(nss[0]), stack0(nss[1]), stack0(nss[2]), stack0(nss[3]), stack0(nss[4]), stack0(nss[5]),
            stack0(mkp), stack0(mvp))
```

```python
import functools
import math

import numpy as np
import jax
import jax.numpy as jnp
from jax import lax
from jax.experimental import pallas as pl
from jax.experimental.pallas import tpu as pltpu

F32 = jnp.float32
BF16 = jnp.bfloat16

D_MODEL = 2048
DEPTH = 4
PAGE = 128
N_HEADS = 16
N_KV = 4
HPG = N_HEADS // N_KV
DIFF_DH = 64
HEAD_W = 128
N_BUCKETS = 32
MAX_EXACT = 16
MAX_DIST = 128
D_INNER = 2 * D_MODEL
SSM_HEADDIM = 64
SSM_HEADS = D_INNER // SSM_HEADDIM
SSM_GROUPS = 8
SSM_HPG = SSM_HEADS // SSM_GROUPS
D_STATE = 128
CONV_W = 4
CONV_DIM = D_INNER + 2 * SSM_GROUPS * D_STATE
SSM_CHUNK = 128
CMP_STRIDE = 16
SLC_BLOCK = 64
SLC_RATIO = SLC_BLOCK // CMP_STRIDE
N_SELECT = 16
WINDOW = 512
N_MEM = 256
X_HEADS = 4
X_DH = 128
EPS = 1e-6
NEG_INF = -1e30
FORCE_SCORE = 1e4

LANE = 128
VMEM_LIMIT = 56 * 1024 * 1024


def _params(sem):
    return pltpu.CompilerParams(dimension_semantics=sem, vmem_limit_bytes=VMEM_LIMIT)


def _rms(x, gain):
    return x * lax.rsqrt(jnp.mean(x * x, axis=-1, keepdims=True) + EPS) * gain


def _dot(a, b):
    return jnp.dot(a, b, preferred_element_type=F32)


def _dot_t(a, b):
    return lax.dot_general(a, b, (((1,), (1,)), ((), ())), preferred_element_type=F32)


def _split3(x):
    hi = x.astype(BF16)
    r1 = x - hi.astype(F32)
    mid = r1.astype(BF16)
    lo = (r1 - mid.astype(F32)).astype(BF16)
    return hi, mid, lo


def _dot_exact_rhs(x, m_bf16):
    hi, mid, lo = _split3(x)
    return _dot(hi, m_bf16) + _dot(mid, m_bf16) + _dot(lo, m_bf16)


def _dense_kernel(*refs, norm, residual):
    refs = list(refs)
    x_ref = refs.pop(0)
    g_ref = refs.pop(0) if norm else None
    w_ref = refs.pop(0)
    r_ref = refs.pop(0) if residual else None
    o_ref, xb_ref = refs

    @pl.when(pl.program_id(1) == 0)
    def _():
        x = x_ref[...]
        if norm:
            x = _rms(x, g_ref[...])
        xb_ref[...] = x.astype(BF16)

    y = _dot(xb_ref[...], w_ref[...])
    if residual:
        y = y + r_ref[...]
    o_ref[...] = y


def dense(x, w, gain=None, res=None):
    m, k = x.shape
    n = w.shape[1]
    tm = m if m <= 512 else (1024 if k <= D_MODEL else 512)
    tn = 512 if n % 512 == 0 else LANE
    assert m % tm == 0 and n % tn == 0
    norm, residual = gain is not None, res is not None
    in_specs = [pl.BlockSpec((tm, k), lambda i, j: (i, 0))]
    args = [x]
    if norm:
        in_specs.append(pl.BlockSpec((1, k), lambda i, j: (0, 0)))
        args.append(gain.reshape(1, k))
    in_specs.append(pl.BlockSpec((k, tn), lambda i, j: (0, j)))
    args.append(w)
    if residual:
        in_specs.append(pl.BlockSpec((tm, tn), lambda i, j: (i, j)))
        args.append(res)
    return pl.pallas_call(
        functools.partial(_dense_kernel, norm=norm, residual=residual),
        grid=(m // tm, n // tn),
        in_specs=in_specs,
        out_specs=pl.BlockSpec((tm, tn), lambda i, j: (i, j)),
        out_shape=jax.ShapeDtypeStruct((m, n), F32),
        scratch_shapes=[pltpu.VMEM((tm, k), BF16)],
        compiler_params=_params(("parallel", "arbitrary")),
        name="dense",
    )(*args)


def _ffn_kernel(x_ref, g_ref, w1_ref, w3_ref, w2_ref, o_ref, xb_ref):
    @pl.when(pl.program_id(1) == 0)
    def _():
        x = x_ref[...]
        xb_ref[...] = _rms(x, g_ref[...]).astype(BF16)
        o_ref[...] = x

    xb = xb_ref[...]
    h1 = _dot(xb, w1_ref[...])
    h3 = _dot(xb, w3_ref[...])
    a = (h1 * jax.nn.sigmoid(h1) * h3).astype(BF16)
    o_ref[...] += _dot(a, w2_ref[...])


def ffn(x, gain, w1, w3, w2):
    m, d = x.shape
    f = w1.shape[1]
    tm = min(m, 512)
    tf = 512
    assert m % tm == 0 and f % tf == 0
    return pl.pallas_call(
        _ffn_kernel,
        grid=(m // tm, f // tf),
        in_specs=[
            pl.BlockSpec((tm, d), lambda i, j: (i, 0)),
            pl.BlockSpec((1, d), lambda i, j: (0, 0)),
            pl.BlockSpec((d, tf), lambda i, j: (0, j)),
            pl.BlockSpec((d, tf), lambda i, j: (0, j)),
            pl.BlockSpec((tf, d), lambda i, j: (j, 0)),
        ],
        out_specs=pl.BlockSpec((tm, d), lambda i, j: (i, 0)),
        out_shape=jax.ShapeDtypeStruct((m, d), F32),
        scratch_shapes=[pltpu.VMEM((tm, d), BF16)],
        compiler_params=_params(("parallel", "arbitrary")),
        name="ffn",
    )(x, gain.reshape(1, d), w1, w3, w2)


def _mem_kv_kernel(mem_ref, g_ref, wk_ref, wv_ref, kn_ref, k_ref, v_ref):
    m = _rms(mem_ref[0], g_ref[0]).astype(BF16)
    k = _dot(m, wk_ref[0])
    v_ref[0, 0] = _dot(m, wv_ref[0])
    for h in range(X_HEADS):
        sl = slice(h * X_DH, (h + 1) * X_DH)
        k_ref[0, 0, :, sl] = _rms(k[:, sl], kn_ref[0])


def mem_kv(mem, g_mem, wk, wv, k_norm):
    b = mem.shape[0]
    nl = wk.shape[0]
    hw = X_HEADS * X_DH
    shape = jax.ShapeDtypeStruct((nl, b, N_MEM, hw), F32)
    return pl.pallas_call(
        _mem_kv_kernel,
        grid=(nl, b),
        in_specs=[
            pl.BlockSpec((1, N_MEM, D_MODEL), lambda l, i: (i, 0, 0)),
            pl.BlockSpec((1, 1, D_MODEL), lambda l, i: (l, 0, 0)),
            pl.BlockSpec((1, D_MODEL, hw), lambda l, i: (l, 0, 0)),
            pl.BlockSpec((1, D_MODEL, hw), lambda l, i: (l, 0, 0)),
            pl.BlockSpec((1, 1, X_DH), lambda l, i: (l, 0, 0)),
        ],
        out_specs=[pl.BlockSpec((1, 1, N_MEM, hw), lambda l, i: (l, i, 0, 0))] * 2,
        out_shape=[shape, shape],
        compiler_params=_params(("parallel", "parallel")),
        name="mem_kv",
    )(mem, g_mem.reshape(nl, 1, D_MODEL), wk, wv, k_norm.reshape(nl, 1, X_DH))


def _xattn_kernel(x_ref, g_ref, wq_ref, qn_ref, k_ref, v_ref, wo_ref, o_ref):
    x = x_ref[0]
    q = _dot(_rms(x, g_ref[...]).astype(BF16), wq_ref[...])
    outs = []
    for h in range(X_HEADS):
        sl = slice(h * X_DH, (h + 1) * X_DH)
        qh = _rms(q[:, sl], qn_ref[...]).astype(BF16)
        s = _dot_t(qh, k_ref[0, :, sl].astype(BF16)) * (X_DH ** -0.5)
        e = jnp.exp(s - jnp.max(s, axis=-1, keepdims=True))
        p = e / jnp.sum(e, axis=-1, keepdims=True)
        outs.append(_dot(p.astype(BF16), v_ref[0, :, sl].astype(BF16)))
    o = jnp.concatenate(outs, axis=1).astype(BF16)
    o_ref[0] = x + _dot(o, wo_ref[...])


def xattn(x, gain, wq, q_norm, k, v, wo):
    b, t, d = x.shape
    hw = X_HEADS * X_DH
    tm = min(t, 512)
    return pl.pallas_call(
        _xattn_kernel,
        grid=(b, t // tm),
        in_specs=[
            pl.BlockSpec((1, tm, d), lambda i, j: (i, j, 0)),
            pl.BlockSpec((1, d), lambda i, j: (0, 0)),
            pl.BlockSpec((d, hw), lambda i, j: (0, 0)),
            pl.BlockSpec((1, X_DH), lambda i, j: (0, 0)),
            pl.BlockSpec((1, N_MEM, hw), lambda i, j: (i, 0, 0)),
            pl.BlockSpec((1, N_MEM, hw), lambda i, j: (i, 0, 0)),
            pl.BlockSpec((hw, d), lambda i, j: (0, 0)),
        ],
        out_specs=pl.BlockSpec((1, tm, d), lambda i, j: (i, j, 0)),
        out_shape=jax.ShapeDtypeStruct((b, t, d), F32),
        compiler_params=_params(("parallel", "parallel")),
        name="xattn",
    )(x, gain.reshape(1, d), wq, q_norm.reshape(1, X_DH), k, v, wo)


def _bucket_np(dist):
    n = np.maximum(dist, 0)
    nf = np.maximum(n, 1).astype(np.float64)
    large = MAX_EXACT + (np.log(nf / MAX_EXACT) / math.log(MAX_DIST / MAX_EXACT)
                         * (N_BUCKETS - MAX_EXACT)).astype(np.int64)
    b = np.where(n < MAX_EXACT, n, np.minimum(large, N_BUCKETS - 1))
    return np.where(dist < 0, -1, b).astype(np.int32)


def _bias_kernel(bkt_ref, tab_ref, o_ref):
    h = pl.program_id(1)
    b = bkt_ref[0]
    acc = jnp.full(b.shape, NEG_INF, F32)
    for k in range(N_BUCKETS):
        acc = jnp.where(b == k, tab_ref[k, h], acc)
    o_ref[0] = acc


def bias_tiles(buckets, table):
    nt, r, w = buckets.shape
    return pl.pallas_call(
        _bias_kernel,
        grid=(nt, N_HEADS),
        in_specs=[
            pl.BlockSpec((1, r, w), lambda t, h: (t, 0, 0)),
            pl.BlockSpec(memory_space=pltpu.SMEM),
        ],
        out_specs=pl.BlockSpec((1, r, w), lambda t, h: (t, h, 0)),
        out_shape=jax.ShapeDtypeStruct((nt, N_HEADS * r, w), F32),
        compiler_params=_params(("parallel", "parallel")),
        name="bias_tiles",
    )(jnp.asarray(buckets), table)


def _prompt_attn_buckets(tq):
    i = np.arange(tq)[:, None]
    j = np.arange(tq)[None, :]
    far = np.full((tq, tq), N_BUCKETS - 1, np.int32)
    return np.stack([
        _bucket_np(i - j),
        _bucket_np(i - j + tq),
        far,
        np.where(j > i, far, -1),
    ]).astype(np.int32)


def _norm64(blk, gain, lo):
    sq = blk * blk
    s_lo = jnp.sum(jnp.where(lo, sq, 0.0), axis=-1, keepdims=True)
    s_hi = jnp.sum(jnp.where(lo, 0.0, sq), axis=-1, keepdims=True)
    ms = jnp.where(lo, s_lo, s_hi) * (1.0 / DIFF_DH)
    return blk * lax.rsqrt(ms + EPS) * gain


def _diff_post_kernel(x_ref, qg_ref, kg_ref, q_ref, k_ref, v_ref):
    tm = x_ref.shape[0]
    lo = lax.broadcasted_iota(jnp.int32, (tm, LANE), 1) < DIFF_DH
    nq = N_HEADS * HEAD_W
    nk = N_KV * HEAD_W
    for c in range(N_HEADS):
        sl = slice(c * LANE, (c + 1) * LANE)
        q_ref[:, sl] = _norm64(x_ref[:, sl], qg_ref[...], lo)
    for c in range(N_KV):
        sl = slice(c * LANE, (c + 1) * LANE)
        k_ref[:, sl] = _norm64(x_ref[:, nq + c * LANE: nq + (c + 1) * LANE], kg_ref[...], lo)
    v_ref[...] = x_ref[:, nq + nk:]


def diff_post(qkv, q_norm, k_norm):
    m, n = qkv.shape
    tm = min(m, 256)
    nq, nk = N_HEADS * HEAD_W, N_KV * HEAD_W
    return pl.pallas_call(
        _diff_post_kernel,
        grid=(m // tm,),
        in_specs=[
            pl.BlockSpec((tm, n), lambda i: (i, 0)),
            pl.BlockSpec((1, LANE), lambda i: (0, 0)),
            pl.BlockSpec((1, LANE), lambda i: (0, 0)),
        ],
        out_specs=[
            pl.BlockSpec((tm, nq), lambda i: (i, 0)),
            pl.BlockSpec((tm, nk), lambda i: (i, 0)),
            pl.BlockSpec((tm, nk), lambda i: (i, 0)),
        ],
        out_shape=[jax.ShapeDtypeStruct((m, nq), F32), jax.ShapeDtypeStruct((m, nk), F32),
                   jax.ShapeDtypeStruct((m, nk), F32)],
        compiler_params=_params(("parallel",)),
        name="diff_post",
    )(qkv, jnp.tile(q_norm, 2).reshape(1, LANE), jnp.tile(k_norm, 2).reshape(1, LANE))


def _online_update(s, v, m_ref, l_ref, acc_ref):
    m_old = m_ref[...]
    m_new = jnp.maximum(m_old, jnp.max(s, axis=-1, keepdims=True))
    alpha = jnp.exp(m_old - m_new)
    p = jnp.exp(s - m_new)
    l_ref[...] = alpha * l_ref[...] + jnp.sum(p, axis=-1, keepdims=True)
    acc_ref[...] = alpha * acc_ref[...] + _dot(p.astype(BF16), v)
    m_ref[...] = m_new


def _diff_lambda(lam_ref, lam_init):
    lf = lam_ref[...]
    s01 = jnp.sum(lf[0:1] * lf[1:2], axis=-1, keepdims=True)
    s23 = jnp.sum(lf[2:3] * lf[3:4], axis=-1, keepdims=True)
    return jnp.exp(s01) - jnp.exp(s23) + lam_init


def _split_maps(qs):
    lo = lax.broadcasted_iota(jnp.int32, qs.shape, 1) < DIFF_DH
    return jnp.concatenate([jnp.where(lo, qs, 0.0), jnp.where(lo, 0.0, qs)], axis=0)


def _diff_finish(acc, l, lam, sub_norm, lam_init):
    r = acc.shape[0] // 2
    o = acc / l
    a = o[:r] - lam * o[r:]
    return _rms(a, sub_norm) * (1.0 - lam_init)


def _diff_flash_kernel(lam_ref, q_ref, k_ref, v_ref, bias_ref, sn_ref, o_ref,
                       qs_ref, m_ref, l_ref, acc_ref, *, tq, lam_init):
    qi = pl.program_id(2)
    qb = q_ref[0]
    qs = jnp.concatenate([qb[:, r * HEAD_W:(r + 1) * HEAD_W] for r in range(HPG)], axis=0)
    qs_ref[...] = _split_maps(qs * (DIFF_DH ** -0.5)).astype(BF16)
    m_ref[...] = jnp.full(m_ref.shape, NEG_INF, F32)
    l_ref[...] = jnp.zeros(l_ref.shape, F32)
    acc_ref[...] = jnp.zeros(acc_ref.shape, F32)

    def body(kt, c):
        rows = pl.ds(pl.multiple_of(kt * tq, tq), tq)
        kb = k_ref[0, rows, :].astype(BF16)
        vb = v_ref[0, rows, :].astype(BF16)
        b = bias_ref[jnp.minimum(qi - kt, 2)]
        s = _dot_t(qs_ref[...], kb) + jnp.concatenate([b, b], axis=0)
        _online_update(s, vb, m_ref, l_ref, acc_ref)
        return c

    lax.fori_loop(0, qi + 1, body, 0)
    a = _diff_finish(acc_ref[...], l_ref[...], _diff_lambda(lam_ref, lam_init), sn_ref[...], lam_init)
    for r in range(HPG):
        o_ref[0, :, r * HEAD_W:(r + 1) * HEAD_W] = a[r * tq:(r + 1) * tq]


def diff_flash(q, k, v, bias, lam_p, sub_norm, lam_init, tq=128):
    b, t, _ = q.shape
    gw = HPG * HEAD_W
    rows = 2 * HPG * tq
    return pl.pallas_call(
        functools.partial(_diff_flash_kernel, tq=tq, lam_init=lam_init),
        grid=(b, N_KV, t // tq),
        in_specs=[
            pl.BlockSpec((4, DIFF_DH), lambda i, g, j: (0, 0)),
            pl.BlockSpec((1, tq, gw), lambda i, g, j: (i, j, g)),
            pl.BlockSpec((1, t, HEAD_W), lambda i, g, j: (i, 0, g)),
            pl.BlockSpec((1, t, HEAD_W), lambda i, g, j: (i, 0, g)),
            pl.BlockSpec((4, HPG * tq, tq), lambda i, g, j: (0, g, 0)),
            pl.BlockSpec((1, HEAD_W), lambda i, g, j: (0, 0)),
        ],
        out_specs=pl.BlockSpec((1, tq, gw), lambda i, g, j: (i, j, g)),
        out_shape=jax.ShapeDtypeStruct(q.shape, F32),
        scratch_shapes=[
            pltpu.VMEM((rows, HEAD_W), BF16),
            pltpu.VMEM((rows, 1), F32),
            pltpu.VMEM((rows, 1), F32),
            pltpu.VMEM((rows, HEAD_W), F32),
        ],
        compiler_params=_params(("parallel", "parallel", "arbitrary")),
        name="diff_flash",
    )(lam_p, q, k, v, bias, sub_norm.reshape(1, HEAD_W))


PAGES_PER_STEP = 4


def _decode_buckets(t_new, past_len):
    i = np.arange(t_new)[:, None]
    j = np.arange(PAGE)[None, :]
    far = np.full((t_new, PAGE), N_BUCKETS - 1, np.int32)
    new = np.where(j < t_new, _bucket_np(i - j), -1)
    return np.stack([far, _bucket_np(PAGE + i - j), new]).astype(np.int32)


def _stack_group_heads(q_ref, g):
    return jnp.concatenate(
        [q_ref[0, :, (g * HPG + r) * HEAD_W:(g * HPG + r + 1) * HEAD_W] for r in range(HPG)], axis=0)


def _page_bias(bias_ref, g, rows, first_page, n_pages):
    tiles = []
    for u in range(PAGES_PER_STEP):
        typ = jnp.where(first_page + u == n_pages - 1, 1, 0)
        tiles.append(bias_ref[typ, g * rows:(g + 1) * rows, :])
    return jnp.concatenate(tiles, axis=1)


def _diff_decode_kernel(pt_ref, lam_ref, q_ref, kn_ref, vn_ref, *rest, n_pages, lam_init):
    del pt_ref
    kp = rest[:PAGES_PER_STEP]
    vp = rest[PAGES_PER_STEP:2 * PAGES_PER_STEP]
    bias_ref, sn_ref, o_ref, qx_ref, m_ref, l_ref, acc_ref = rest[2 * PAGES_PER_STEP:]
    s = pl.program_id(1)
    t_new = q_ref.shape[1]
    rows = HPG * t_new

    @pl.when(s == 0)
    def _():
        m_ref[...] = jnp.full(m_ref.shape, NEG_INF, F32)
        l_ref[...] = jnp.zeros(l_ref.shape, F32)
        acc_ref[...] = jnp.zeros(acc_ref.shape, F32)
        for g in range(N_KV):
            sl = slice(g * HEAD_W, (g + 1) * HEAD_W)
            qx_ref[g] = _split_maps(_stack_group_heads(q_ref, g) * (DIFF_DH ** -0.5)).astype(BF16)
            b = bias_ref[2, g * rows:(g + 1) * rows, :]
            sc = _dot_t(qx_ref[g], kn_ref[0, :, sl].astype(BF16)) + jnp.concatenate([b, b], axis=0)
            _online_update(sc, vn_ref[0, :, sl].astype(BF16), m_ref.at[g], l_ref.at[g], acc_ref.at[g])

    for g in range(N_KV):
        sl = slice(g * HEAD_W, (g + 1) * HEAD_W)
        kc = jnp.concatenate([r[0, 0, :, g, :] for r in kp], axis=0).astype(BF16)
        vc = jnp.concatenate([r[0, 0, :, g, :] for r in vp], axis=0).astype(BF16)
        b = _page_bias(bias_ref, g, rows, s * PAGES_PER_STEP, n_pages)
        sc = _dot_t(qx_ref[g], kc) + jnp.concatenate([b, b], axis=0)
        _online_update(sc, vc, m_ref.at[g], l_ref.at[g], acc_ref.at[g])

    @pl.when(s == pl.num_programs(1) - 1)
    def _():
        lam = _diff_lambda(lam_ref, lam_init)
        for g in range(N_KV):
            a = _diff_finish(acc_ref[g], l_ref[g], lam, sn_ref[...], lam_init)
            for r in range(HPG):
                h = g * HPG + r
                o_ref[0, :, h * HEAD_W:(h + 1) * HEAD_W] = a[r * t_new:(r + 1) * t_new]


def _page_specs(layer, n):
    def spec(u):
        return pl.BlockSpec((1, 1, PAGE, N_KV, HEAD_W),
                            lambda i, s, pt: (layer, pt[i, s * PAGES_PER_STEP + u], 0, 0, 0))
    return [spec(u) for u in range(PAGES_PER_STEP)] * n


def diff_decode(q, k_new, v_new, cache_k, cache_v, layer, page_table, bias, lam_p, sub_norm, lam_init):
    b, t_new, _ = q.shape
    n_pages = page_table.shape[1]
    kvw = N_KV * HEAD_W
    rows = 2 * HPG * t_new
    fixed = lambda *shape: pl.BlockSpec(shape, lambda i, s, pt: (0,) * len(shape))
    per_b = lambda *shape: pl.BlockSpec((1,) + shape, lambda i, s, pt: (i,) + (0,) * len(shape))
    grid_spec = pltpu.PrefetchScalarGridSpec(
        num_scalar_prefetch=1,
        grid=(b, n_pages // PAGES_PER_STEP),
        in_specs=[fixed(4, DIFF_DH), per_b(t_new, N_HEADS * HEAD_W), per_b(PAGE, kvw), per_b(PAGE, kvw)]
        + _page_specs(layer, 2)
        + [fixed(*bias.shape), fixed(1, HEAD_W)],
        out_specs=per_b(t_new, N_HEADS * HEAD_W),
        scratch_shapes=[
            pltpu.VMEM((N_KV, rows, HEAD_W), BF16),
            pltpu.VMEM((N_KV, rows, 1), F32),
            pltpu.VMEM((N_KV, rows, 1), F32),
            pltpu.VMEM((N_KV, rows, HEAD_W), F32),
        ],
    )
    return pl.pallas_call(
        functools.partial(_diff_decode_kernel, n_pages=n_pages, lam_init=lam_init),
        grid_spec=grid_spec,
        out_shape=jax.ShapeDtypeStruct(q.shape, F32),
        compiler_params=_params(("parallel", "arbitrary")),
        name="diff_decode",
    )(page_table, lam_p, q, k_new, v_new, *([cache_k] * PAGES_PER_STEP), *([cache_v] * PAGES_PER_STEP),
      bias, sub_norm.reshape(1, HEAD_W))


def _pad_rows(a, n):
    return jnp.pad(a, ((0, 0), (0, n - a.shape[1]), (0, 0)))


def diff_layer(xp, xs, gain, w_in, q_norm, k_norm, lam_p, sub_norm, w_out, cache_k, cache_v, layer,
               page_table, bias_p, bias_s, lam_init):
    outs = []
    for x, paged in ((xp, False), (xs, True)):
        b, t, d = x.shape
        x2 = x.reshape(b * t, d)
        q, k, v = diff_post(dense(x2, w_in, gain=gain), q_norm, k_norm)
        q3, k3, v3 = (a.reshape(b, t, -1) for a in (q, k, v))
        if paged:
            o = diff_decode(q3, _pad_rows(k3, PAGE), _pad_rows(v3, PAGE), cache_k, cache_v, layer,
                            page_table, bias_s, lam_p, sub_norm, lam_init)
        else:
            o = diff_flash(q3, k3, v3, bias_p, lam_p, sub_norm, lam_init)
        y = dense(o.reshape(b * t, -1), w_out, res=x2).reshape(b, t, d)
        outs.append((y, k3.reshape(b, t, N_KV, HEAD_W), v3.reshape(b, t, N_KV, HEAD_W)))
    (yp, kp, vp), (ys, ks, vs) = outs
    return yp, ys, kp, vp, ks, vs


SSM_GW = SSM_HPG * SSM_HEADDIM
SSM_BC = 2 * SSM_GROUPS * D_STATE
CONV_PAD = 8


def _conv_silu(buf_ref, w_ref, b_ref, n):
    acc = b_ref[...]
    for k in range(CONV_W):
        acc = acc + buf_ref[pl.ds(CONV_PAD - (CONV_W - 1) + k, n), :] * w_ref[k:k + 1, :]
    return acc * jax.nn.sigmoid(acc)


def _ssd_kernel(z_ref, x_ref, bc_ref, dt_ref, cbx_ref, cbbc_ref, wx_ref, wbc_ref, bx_ref, bbc_ref,
                dtb_ref, alog_ref, dsk_ref, ng_ref, e_ref, tri_ref, h0_ref,
                y_ref, hout_ref, ht_ref, xbuf_ref, bcbuf_ref, xa_ref, bca_ref, *, t_valid):
    c = pl.program_id(1)
    n = x_ref.shape[1]

    @pl.when(c == 0)
    def _():
        for g in range(SSM_GROUPS):
            ht_ref[g] = h0_ref[0, g].T
        xbuf_ref[0:CONV_PAD] = cbx_ref[0]
        bcbuf_ref[0:CONV_PAD] = cbbc_ref[0]

    xbuf_ref[CONV_PAD:CONV_PAD + n] = x_ref[0]
    bcbuf_ref[CONV_PAD:CONV_PAD + n] = bc_ref[0]
    xa_ref[...] = _conv_silu(xbuf_ref, wx_ref, bx_ref, n)
    bca_ref[...] = _conv_silu(bcbuf_ref, wbc_ref, bbc_ref, n)
    xbuf_ref[0:CONV_PAD] = xbuf_ref[n:n + CONV_PAD]
    bcbuf_ref[0:CONV_PAD] = bcbuf_ref[n:n + CONV_PAD]

    row = lax.broadcasted_iota(jnp.int32, (n, LANE), 0) + c * n
    dtr = dt_ref[0] + dtb_ref[...]
    dt = jnp.maximum(dtr, 0.0) + jnp.log1p(jnp.exp(-jnp.abs(dtr)))
    dt = jnp.where(row < t_valid, dt, 0.0)
    dta = dt * (-jnp.exp(alog_ref[...]))
    hi, mid, lo = _split3(dta)
    tri = tri_ref[...]
    cs = _dot(tri, hi) + _dot(tri, mid) + _dot(tri, lo)
    cs_last = cs[n - 1:n, :]
    cs_t = cs.T
    dt_t = dt.T
    stacked = jnp.concatenate(
        [jnp.exp(cs), jnp.exp(cs_last - cs) * dt, jnp.broadcast_to(jnp.exp(cs_last), (8, LANE))], axis=0)
    ex = _dot_exact_rhs(stacked, e_ref[...])
    causal = (lax.broadcasted_iota(jnp.int32, (n, n), 0) >= lax.broadcasted_iota(jnp.int32, (n, n), 1))

    for g in range(SSM_GROUPS):
        gs = slice(g * SSM_GW, (g + 1) * SSM_GW)
        bm = bca_ref[:, g * D_STATE:(g + 1) * D_STATE]
        cm = bca_ref[:, (SSM_GROUPS + g) * D_STATE:(SSM_GROUPS + g + 1) * D_STATE].astype(BF16)
        cb = _dot_t(cm, bm.astype(BF16))
        xg = xa_ref[:, gs]
        ys = []
        for r in range(SSM_HPG):
            h = g * SSM_HPG + r
            seg = cs[:, h:h + 1] - cs_t[h:h + 1, :]
            dec = jnp.where(causal, jnp.exp(jnp.where(causal, seg, 0.0)), 0.0)
            mm = (cb * dec * dt_t[h:h + 1, :]).astype(BF16)
            ys.append(_dot(mm, xg[:, r * SSM_HEADDIM:(r + 1) * SSM_HEADDIM].astype(BF16)))
        ht = ht_ref[g]
        y = jnp.concatenate(ys, axis=1) + _dot(cm, ht.astype(BF16)) * ex[0:n, gs]
        y = y + dsk_ref[:, gs] * xg
        zg = z_ref[0, :, gs]
        y = y * (zg * jax.nn.sigmoid(zg))
        y_ref[0, :, gs] = _rms(y, ng_ref[:, gs])
        xw = (xg * ex[n:2 * n, gs]).astype(BF16)
        ht_ref[g] = ht * ex[2 * n:2 * n + 1, gs] + _dot(bm.T.astype(BF16), xw)

    @pl.when(c == pl.num_programs(1) - 1)
    def _():
        for g in range(SSM_GROUPS):
            hout_ref[0, g] = ht_ref[g].T


def _head_expand_matrix():
    e = np.zeros((LANE, D_INNER), np.float32)
    for h in range(SSM_HEADS):
        e[h, h * SSM_HEADDIM:(h + 1) * SSM_HEADDIM] = 1.0
    return e


def ssd_core(zx, dt_raw, conv_buf, h0, conv_w, conv_b, dt_bias, a_log, d_skip, norm_g, t_valid):
    b, t, _ = zx.shape
    n = SSM_CHUNK
    pad_h = LANE - SSM_HEADS
    cb = jnp.pad(conv_buf, ((0, 0), (CONV_PAD - (CONV_W - 1), 0), (0, 0)))
    tri = jnp.asarray(np.tril(np.ones((n, n), np.float32)), BF16)
    e = jnp.asarray(_head_expand_matrix(), BF16)
    d_exp = jnp.repeat(d_skip, SSM_HEADDIM).reshape(1, D_INNER)
    fixed = lambda *shape: pl.BlockSpec(shape, lambda i, c: (0,) * len(shape))
    per_b = lambda *shape: pl.BlockSpec((1,) + shape, lambda i, c: (i,) + (0,) * len(shape))
    y, h_last = pl.pallas_call(
        functools.partial(_ssd_kernel, t_valid=t_valid),
        grid=(b, t // n),
        in_specs=[
            pl.BlockSpec((1, n, D_INNER), lambda i, c: (i, c, 0)),
            pl.BlockSpec((1, n, D_INNER), lambda i, c: (i, c, 1)),
            pl.BlockSpec((1, n, SSM_BC), lambda i, c: (i, c, 2 * D_INNER // SSM_BC)),
            pl.BlockSpec((1, n, LANE), lambda i, c: (i, c, 0)),
            per_b(CONV_PAD, D_INNER), per_b(CONV_PAD, SSM_BC),
            fixed(CONV_W, D_INNER), fixed(CONV_W, SSM_BC), fixed(1, D_INNER), fixed(1, SSM_BC),
            fixed(1, LANE), fixed(1, LANE), fixed(1, D_INNER), fixed(1, D_INNER),
            fixed(LANE, D_INNER), fixed(n, n),
            per_b(SSM_GROUPS, SSM_GW, D_STATE),
        ],
        out_specs=[
            pl.BlockSpec((1, n, D_INNER), lambda i, c: (i, c, 0)),
            per_b(SSM_GROUPS, SSM_GW, D_STATE),
        ],
        out_shape=[jax.ShapeDtypeStruct((b, t, D_INNER), F32),
                   jax.ShapeDtypeStruct((b, SSM_GROUPS, SSM_GW, D_STATE), F32)],
        scratch_shapes=[
            pltpu.VMEM((SSM_GROUPS, D_STATE, SSM_GW), F32),
            pltpu.VMEM((n + CONV_PAD, D_INNER), F32),
            pltpu.VMEM((n + CONV_PAD, SSM_BC), F32),
            pltpu.VMEM((n, D_INNER), F32),
            pltpu.VMEM((n, SSM_BC), F32),
        ],
        compiler_params=_params(("parallel", "arbitrary")),
        name="ssd_core",
    )(zx, zx, zx, dt_raw, cb[:, :, :D_INNER], cb[:, :, D_INNER:],
      conv_w[:, :D_INNER], conv_w[:, D_INNER:], conv_b[:D_INNER].reshape(1, -1), conv_b[D_INNER:].reshape(1, -1),
      jnp.pad(dt_bias, (0, pad_h)).reshape(1, LANE), jnp.pad(a_log, (0, pad_h)).reshape(1, LANE),
      d_exp, norm_g.reshape(1, D_INNER), e, tri,
      h0.reshape(b, SSM_GROUPS, SSM_GW, D_STATE))
    return y, h_last.reshape(b, SSM_HEADS, SSM_HEADDIM, D_STATE)


def ssd_layer(xp, xs, gain, w_zx, w_dt, conv_w, conv_b, dt_bias, a_log, d_skip, norm_g, w_out,
              state_conv, state_ssm):
    outs = []
    for x, conv_buf, h0 in ((xp, None, None), (xs, state_conv, state_ssm)):
        b, t, d = x.shape
        x2 = x.reshape(b * t, d)
        if conv_buf is None:
            conv_buf = jnp.zeros((b, CONV_W - 1, CONV_DIM), F32)
            h0 = jnp.zeros((b, SSM_HEADS, SSM_HEADDIM, D_STATE), F32)
        zx = dense(x2, w_zx, gain=gain).reshape(b, t, -1)
        dt_raw = dense(x2, w_dt, gain=gain).reshape(b, t, LANE)
        tp = -(-t // SSM_CHUNK) * SSM_CHUNK
        y, h_last = ssd_core(_pad_rows(zx, tp), _pad_rows(dt_raw, tp), conv_buf, h0, conv_w, conv_b,
                             dt_bias, a_log, d_skip, norm_g, t)
        y = y[:, :t].reshape(b * t, D_INNER)
        xpad = jnp.concatenate([conv_buf, zx[:, :, D_INNER:]], axis=1)
        outs.append((dense(y, w_out, res=x2).reshape(b, t, d), h_last, xpad[:, t:]))
    (yp, hp, cp), (ys, hs, cs) = outs
    return yp, ys, hp, cp, hs, cs


NSA_KV = N_KV * HEAD_W
NSA_SCALE = HEAD_W ** -0.5
CMP_PAIRS = CMP_STRIDE // 2
CMP_PAGES = 8


def _nsa_post_kernel(x_ref, qg_ref, kg_ref, q_ref, kcr_ref, vcr_ref, ks_ref, vs_ref, kw_ref, vw_ref):
    nq = N_HEADS * HEAD_W
    for c in range(N_HEADS):
        sl = slice(c * LANE, (c + 1) * LANE)
        q_ref[:, sl] = _rms(x_ref[:, sl], qg_ref[...])
    outs = (kcr_ref, vcr_ref, ks_ref, vs_ref, kw_ref, vw_ref)
    gains = (None, None, 1, None, 2, None)
    for a, (o_ref, gi) in enumerate(zip(outs, gains)):
        for c in range(N_KV):
            col = nq + a * NSA_KV + c * LANE
            blk = x_ref[:, col:col + LANE]
            o_ref[:, c * LANE:(c + 1) * LANE] = blk if gi is None else _rms(blk, kg_ref[gi:gi + 1, :])


def nsa_post(proj, q_norm, k_norm):
    m, n = proj.shape
    tm = min(m, 256)
    nq = N_HEADS * HEAD_W
    kv_spec = pl.BlockSpec((tm, NSA_KV), lambda i: (i, 0))
    kv_shape = jax.ShapeDtypeStruct((m, NSA_KV), F32)
    return pl.pallas_call(
        _nsa_post_kernel,
        grid=(m // tm,),
        in_specs=[
            pl.BlockSpec((tm, n), lambda i: (i, 0)),
            pl.BlockSpec((1, LANE), lambda i: (0, 0)),
            pl.BlockSpec((3, LANE), lambda i: (0, 0)),
        ],
        out_specs=[pl.BlockSpec((tm, nq), lambda i: (i, 0))] + [kv_spec] * 6,
        out_shape=[jax.ShapeDtypeStruct((m, nq), F32)] + [kv_shape] * 6,
        compiler_params=_params(("parallel",)),
        name="nsa_post",
    )(proj, q_norm.reshape(1, LANE), k_norm)


def _cmp_uv_kernel(pt_ref, *refs):
    del pt_ref
    pages = refs[:CMP_PAGES]
    wab_ref, o_ref = refs[CMP_PAGES:]
    subs = PAGE // CMP_STRIDE
    for g in range(N_KV):
        sl = slice(g * HEAD_W, (g + 1) * HEAD_W)
        acc = jnp.zeros((CMP_PAGES * subs, 2 * HEAD_W), F32)
        for lp in range(CMP_PAIRS):
            halves = []
            for li in range(2):
                rows = pl.ds(2 * lp + li, subs, stride=CMP_STRIDE)
                halves.append(jnp.concatenate([p[0, 0, rows, g, :] for p in pages], axis=0))
            acc = acc + _dot(jnp.concatenate(halves, axis=1).astype(BF16), wab_ref[lp])
        o_ref[0, :, g * 2 * HEAD_W:(g + 1) * 2 * HEAD_W] = acc


def cmp_uv(rows, layer, page_table, wab):
    b, n_pages = page_table.shape
    subs = PAGE // CMP_STRIDE

    def spec(u):
        return pl.BlockSpec((1, 1, PAGE, N_KV, HEAD_W),
                            lambda i, c, pt: (layer, pt[i, c * CMP_PAGES + u], 0, 0, 0))

    grid_spec = pltpu.PrefetchScalarGridSpec(
        num_scalar_prefetch=1,
        grid=(b, n_pages // CMP_PAGES),
        in_specs=[spec(u) for u in range(CMP_PAGES)]
        + [pl.BlockSpec(wab.shape, lambda i, c, pt: (0, 0, 0))],
        out_specs=pl.BlockSpec((1, CMP_PAGES * subs, 2 * NSA_KV), lambda i, c, pt: (i, c, 0)),
    )
    return pl.pallas_call(
        _cmp_uv_kernel,
        grid_spec=grid_spec,
        out_shape=jax.ShapeDtypeStruct((b, n_pages * subs, 2 * NSA_KV), F32),
        compiler_params=_params(("parallel", "arbitrary")),
        name="cmp_uv",
    )(page_table, *([rows] * CMP_PAGES), wab)


def _cmp_finish_kernel(uv_ref, pe_ref, w1_ref, w2_ref, kn_ref, o_ref, *, norm):
    n = uv_ref.shape[1]
    c = _dot(jnp.broadcast_to(pe_ref[...], (8, pe_ref.shape[1])).astype(BF16), w1_ref[...])[0:1]
    for g in range(N_KV):
        u = uv_ref[0, :, g * 2 * HEAD_W:g * 2 * HEAD_W + HEAD_W]
        v = uv_ref[0, :, g * 2 * HEAD_W + HEAD_W:(g + 1) * 2 * HEAD_W]
        pre = u + pltpu.roll(v, n - 1, axis=0) + c
        out = _dot((pre * jax.nn.sigmoid(pre)).astype(BF16), w2_ref[...])
        if norm:
            out = _rms(out, kn_ref[...])
        o_ref[0, :, g * HEAD_W:(g + 1) * HEAD_W] = out


def cmp_finish(uv, pe, w1, w2, k_norm):
    b, n, _ = uv.shape
    norm = k_norm is not None
    kn = (k_norm if norm else jnp.ones((HEAD_W,), F32)).reshape(1, HEAD_W)
    return pl.pallas_call(
        functools.partial(_cmp_finish_kernel, norm=norm),
        grid=(b,),
        in_specs=[
            pl.BlockSpec((1, n, 2 * NSA_KV), lambda i: (i, 0, 0)),
            pl.BlockSpec((1, pe.size), lambda i: (0, 0)),
            pl.BlockSpec(w1.shape, lambda i: (0, 0)),
            pl.BlockSpec(w2.shape, lambda i: (0, 0)),
            pl.BlockSpec((1, HEAD_W), lambda i: (0, 0)),
        ],
        out_specs=pl.BlockSpec((1, n, NSA_KV), lambda i: (i, 0, 0)),
        out_shape=jax.ShapeDtypeStruct((b, n, NSA_KV), F32),
        compiler_params=_params(("parallel",)),
        name="cmp_finish",
    )(uv, pe.reshape(1, -1), w1, w2, kn)


def compress(rows, layer, page_table, pe, w1, w2, k_norm):
    w1r = w1.reshape(2, CMP_STRIDE, HEAD_W, HEAD_W)
    wab = jnp.transpose(w1r, (1, 2, 0, 3)).reshape(CMP_PAIRS, 2 * HEAD_W, 2 * HEAD_W).astype(BF16)
    uv = cmp_uv(rows, layer, page_table, wab)
    return cmp_finish(uv, pe, w1.astype(BF16), w2.astype(BF16), k_norm)


def _importance_matrix(n_cmp, n_rows, n_slc, n_cols):
    w = np.zeros((n_rows, n_cols), np.float32)
    w_imp = [1.0] + [2.0] * (SLC_RATIO - 1) + [1.0]
    for s in range(n_slc):
        for m, wm in enumerate(w_imp):
            j = SLC_RATIO * s + m - 1
            if 0 <= j < n_cmp:
                w[j, s] += wm
    return w


def _select_blocks(s_slc, q_pos0, n_slc):
    t, w = s_slc.shape
    blk = lax.broadcasted_iota(jnp.int32, (t, w), 1)
    qpos = q_pos0 + lax.broadcasted_iota(jnp.int32, (t, w), 0)
    qb = qpos // SLC_BLOCK
    forced = (blk == 0) | (blk == qb) | (blk == qb - 1)
    score = jnp.where(forced, FORCE_SCORE, jnp.where(blk * SLC_BLOCK <= qpos, s_slc, -1.0))
    score = jnp.where(blk < n_slc, score, -2.0)
    cnt = jnp.zeros((t, w), F32)
    for sp in range(n_slc):
        col = score[:, sp:sp + 1]
        tie = jnp.where(blk > sp, 1.0, 0.0)
        cnt = cnt + jnp.where(col > score, 1.0, jnp.where(col == score, tie, 0.0))
    return jnp.where(cnt < N_SELECT, 1.0, 0.0)


def _masked_softmax(s, valid):
    m = jnp.max(s, axis=-1, keepdims=True)
    e = jnp.where(valid, jnp.exp(s - m), 0.0)
    return e / jnp.maximum(jnp.sum(e, axis=-1, keepdims=True), 1e-30)


def _reset_flash(m_ref, l_ref, acc_ref):
    m_ref[...] = jnp.full(m_ref.shape, NEG_INF, F32)
    l_ref[...] = jnp.zeros(l_ref.shape, F32)
    acc_ref[...] = jnp.zeros(acc_ref.shape, F32)


def _block_mask(sel, first_block, n_keys, reps):
    w = sel.shape[1]
    er = lax.broadcasted_iota(jnp.int32, (w, n_keys), 0)
    ec = lax.broadcasted_iota(jnp.int32, (w, n_keys), 1)
    e = jnp.where(er == first_block + ec // SLC_BLOCK, 1.0, 0.0).astype(BF16)
    mt = _dot(sel, e)
    return jnp.concatenate([mt] * reps, axis=0)


def _nsa_prompt_kernel(q_ref, gt_ref, kc_ref, vc_ref, ks_ref, vs_ref, kw_ref, vw_ref, bias_ref, u_ref, wimp_ref,
                       o_ref, qs_ref, sel_ref, m_ref, l_ref, acc_ref, *, tq, n_slc):
    qi = pl.program_id(2)
    qs_ref[...] = jnp.concatenate(
        [q_ref[0, :, r * HEAD_W:(r + 1) * HEAD_W] for r in range(HPG)], axis=0).astype(BF16)

    n_pad = kc_ref.shape[1]
    uw = u_ref.shape[2]
    shift = lax.rem(uw - (n_pad - (tq // CMP_STRIDE) * (qi + 1)), uw)
    bias_c = pltpu.roll(u_ref[0], shift, axis=1)[:, :n_pad]
    s = _dot_t(qs_ref[...], kc_ref[0].astype(BF16)) * NSA_SCALE + bias_c
    pc = _masked_softmax(s, bias_c > 0.5 * NEG_INF)
    o_cmp = _dot(pc.astype(BF16), vc_ref[0].astype(BF16))
    psum = pc[0:tq] + pc[tq:2 * tq] + pc[2 * tq:3 * tq] + pc[3 * tq:4 * tq]
    s_slc = _dot_exact_rhs(psum, wimp_ref[...])
    sel_ref[...] = _select_blocks(s_slc, qi * tq, n_slc).astype(BF16)

    _reset_flash(m_ref, l_ref, acc_ref)

    def slc_body(kt, c):
        rows = pl.ds(pl.multiple_of(kt * tq, tq), tq)
        mask = _block_mask(sel_ref[...], kt * (tq // SLC_BLOCK), tq, HPG)
        sc = _dot_t(qs_ref[...], ks_ref[0, rows, :].astype(BF16)) * NSA_SCALE + bias_ref[jnp.minimum(qi - kt, 2)]
        sc = jnp.where(mask > 0.5, sc, NEG_INF)
        _online_update(sc, vs_ref[0, rows, :].astype(BF16), m_ref, l_ref, acc_ref)
        return c

    lax.fori_loop(0, qi + 1, slc_body, 0)
    o_slc = acc_ref[...] / l_ref[...]

    _reset_flash(m_ref, l_ref, acc_ref)
    nw = WINDOW // tq

    def win_body(kt, c):
        rows = pl.ds(pl.multiple_of(kt * tq, tq), tq)
        t = qi - kt
        typ = jnp.where(t == nw, 3, jnp.minimum(t, 2))
        sc = _dot_t(qs_ref[...], kw_ref[0, rows, :].astype(BF16)) * NSA_SCALE + bias_ref[typ]
        _online_update(sc, vw_ref[0, rows, :].astype(BF16), m_ref, l_ref, acc_ref)
        return c

    lax.fori_loop(jnp.maximum(qi - nw, 0), qi + 1, win_body, 0)
    o_win = acc_ref[...] / l_ref[...]

    sig = jax.nn.sigmoid(gt_ref[0])
    for r in range(HPG):
        rs = slice(r * tq, (r + 1) * tq)
        o_ref[0, :, r * HEAD_W:(r + 1) * HEAD_W] = (
            sig[:, r:r + 1] * o_cmp[rs] + sig[:, HPG + r:HPG + r + 1] * o_slc[rs]
            + sig[:, 2 * HPG + r:2 * HPG + r + 1] * o_win[rs])


def _prompt_cmp_buckets(tq, n_pad):
    i = np.arange(tq)[:, None]
    jp = np.arange(2 * n_pad)[None, :] - (n_pad - tq // CMP_STRIDE)
    return _bucket_np(i - CMP_STRIDE * jp - (2 * CMP_STRIDE - 1))[None].astype(np.int32)


def nsa_prompt(q, gates, kc, vc, ks, vs, kw, vw, bias, u_bias, tq=128):
    b, t, _ = q.shape
    gw = HPG * HEAD_W
    n_pad = kc.shape[1]
    n_slc = t // SLC_BLOCK
    wimp = jnp.asarray(_importance_matrix(n_pad - 1, n_pad, n_slc, LANE), BF16)
    rows = HPG * tq
    seq = lambda n: pl.BlockSpec((1, n, HEAD_W), lambda i, g, j: (i, 0, g))
    return pl.pallas_call(
        functools.partial(_nsa_prompt_kernel, tq=tq, n_slc=n_slc),
        grid=(b, N_KV, t // tq),
        in_specs=[
            pl.BlockSpec((1, tq, gw), lambda i, g, j: (i, j, g)),
            pl.BlockSpec((1, tq, LANE), lambda i, g, j: (i, j, g)),
            seq(n_pad), seq(n_pad), seq(t), seq(t), seq(t), seq(t),
            pl.BlockSpec((4, rows, tq), lambda i, g, j: (0, g, 0)),
            pl.BlockSpec((1, rows, 2 * n_pad), lambda i, g, j: (0, g, 0)),
            pl.BlockSpec(wimp.shape, lambda i, g, j: (0, 0)),
        ],
        out_specs=pl.BlockSpec((1, tq, gw), lambda i, g, j: (i, j, g)),
        out_shape=jax.ShapeDtypeStruct(q.shape, F32),
        scratch_shapes=[
            pltpu.VMEM((rows, HEAD_W), BF16),
            pltpu.VMEM((tq, LANE), BF16),
            pltpu.VMEM((rows, 1), F32),
            pltpu.VMEM((rows, 1), F32),
            pltpu.VMEM((rows, HEAD_W), F32),
        ],
        compiler_params=_params(("parallel", "parallel", "arbitrary")),
        name="nsa_prompt",
    )(q, gates, kc, vc, ks, vs, kw, vw, bias, u_bias, wimp)


def _decode_cmp_buckets(t_new, past_len):
    n_pad = past_len // CMP_STRIDE
    i = np.arange(t_new)[:, None]
    j = n_pad - LANE + np.arange(LANE)[None, :]
    last = _bucket_np(past_len + i - CMP_STRIDE * j - (2 * CMP_STRIDE - 1))
    last = np.where(j < n_pad - 1, last, -1)
    return np.stack([np.full((t_new, LANE), N_BUCKETS - 1, np.int32), last]).astype(np.int32)


def _decode_win_buckets(t_new, n_tiles):
    i = np.arange(t_new)[:, None]
    idx = np.arange(n_tiles * LANE)[None, :]
    dw = WINDOW + i - idx
    ok = (dw >= 0) & (dw < WINDOW) & (idx < WINDOW + t_new)
    b = np.where(ok, _bucket_np(dw), -1)
    return np.stack([b[:, k * LANE:(k + 1) * LANE] for k in range(n_tiles)]).astype(np.int32)


def _nsa_decode_kernel(pt_ref, q_ref, gt_ref, kc_ref, vc_ref, kn_ref, vn_ref, kw_ref, vw_ref, *rest,
                       n_pages, past_len):
    del pt_ref
    kp = rest[:PAGES_PER_STEP]
    vp = rest[PAGES_PER_STEP:2 * PAGES_PER_STEP]
    (bias_ref, bias_c_ref, bias_w_ref, wimp_ref, o_ref,
     qx_ref, sel_ref, ocmp_ref, owin_ref, m_ref, l_ref, acc_ref) = rest[2 * PAGES_PER_STEP:]
    s = pl.program_id(1)
    t_new = q_ref.shape[1]
    rows = HPG * t_new
    n_slc = -(-(past_len + t_new) // SLC_BLOCK)

    @pl.when(s == 0)
    def _():
        _reset_flash(m_ref, l_ref, acc_ref)
        n_pad = kc_ref.shape[1]
        for g in range(N_KV):
            sl = slice(g * HEAD_W, (g + 1) * HEAD_W)
            gr = slice(g * rows, (g + 1) * rows)
            qx_ref[g] = _stack_group_heads(q_ref, g).astype(BF16)
            qx = qx_ref[g]
            bias_c = jnp.concatenate([bias_c_ref[0, gr, :]] * (n_pad // LANE - 1) + [bias_c_ref[1, gr, :]], axis=1)
            sc = _dot_t(qx, kc_ref[0, :, sl].astype(BF16)) * NSA_SCALE + bias_c
            pc = _masked_softmax(sc, bias_c > 0.5 * NEG_INF)
            ocmp_ref[g] = _dot(pc.astype(BF16), vc_ref[0, :, sl].astype(BF16))
            psum = sum(pc[r * t_new:(r + 1) * t_new] for r in range(1, HPG)) + pc[0:t_new]
            s_slc = _dot_exact_rhs(psum, wimp_ref[...])
            sel_ref[g] = _select_blocks(s_slc, past_len, n_slc)
            n_wt = bias_w_ref.shape[0]
            bias_w = jnp.concatenate([bias_w_ref[k, gr, :] for k in range(n_wt)], axis=1)
            sw = _dot_t(qx, kw_ref[0, :, sl].astype(BF16)) * NSA_SCALE + bias_w
            pw = _masked_softmax(sw, bias_w > 0.5 * NEG_INF)
            owin_ref[g] = _dot(pw.astype(BF16), vw_ref[0, :, sl].astype(BF16))
            sn = _dot_t(qx, kn_ref[0, :, sl].astype(BF16)) * NSA_SCALE + bias_ref[2, gr, :]
            _online_update(sn, vn_ref[0, :, sl].astype(BF16), m_ref.at[g], l_ref.at[g], acc_ref.at[g])

    for g in range(N_KV):
        sl = slice(g * HEAD_W, (g + 1) * HEAD_W)
        kc = jnp.concatenate([r[0, 0, :, g, :] for r in kp], axis=0).astype(BF16)
        vc = jnp.concatenate([r[0, 0, :, g, :] for r in vp], axis=0).astype(BF16)
        b = _page_bias(bias_ref, g, rows, s * PAGES_PER_STEP, n_pages)
        first_block = s * (PAGES_PER_STEP * PAGE // SLC_BLOCK)
        mask = _block_mask(sel_ref[g].astype(BF16), first_block, PAGES_PER_STEP * PAGE, HPG)
        sc = _dot_t(qx_ref[g], kc) * NSA_SCALE + b
        sc = jnp.where(mask > 0.5, sc, NEG_INF)
        _online_update(sc, vc, m_ref.at[g], l_ref.at[g], acc_ref.at[g])

    @pl.when(s == pl.num_programs(1) - 1)
    def _():
        sig = jax.nn.sigmoid(gt_ref[0])
        for g in range(N_KV):
            o_slc = acc_ref[g] / l_ref[g]
            o_cmp = ocmp_ref[g]
            o_win = owin_ref[g]
            for r in range(HPG):
                h = g * HPG + r
                rs = slice(r * t_new, (r + 1) * t_new)
                c0 = g * LANE + r
                o_ref[0, :, h * HEAD_W:(h + 1) * HEAD_W] = (
                    sig[:, c0:c0 + 1] * o_cmp[rs] + sig[:, c0 + HPG:c0 + HPG + 1] * o_slc[rs]
                    + sig[:, c0 + 2 * HPG:c0 + 2 * HPG + 1] * o_win[rs])


def nsa_decode(q, gates, kc, vc, k_new, v_new, kw_src, vw_src, cache_k, cache_v, layer, page_table,
               bias, bias_c, bias_w, past_len):
    b, t_new, _ = q.shape
    n_pages = page_table.shape[1]
    rows = HPG * t_new
    n_pad = kc.shape[1]
    n_slc = -(-(past_len + t_new) // SLC_BLOCK)
    n_cols = -(-n_slc // LANE) * LANE
    wimp = jnp.asarray(_importance_matrix(n_pad - 1, n_pad, n_slc, n_cols), BF16)
    fixed = lambda *shape: pl.BlockSpec(shape, lambda i, s, pt: (0,) * len(shape))
    per_b = lambda *shape: pl.BlockSpec((1,) + shape, lambda i, s, pt: (i,) + (0,) * len(shape))
    grid_spec = pltpu.PrefetchScalarGridSpec(
        num_scalar_prefetch=1,
        grid=(b, n_pages // PAGES_PER_STEP),
        in_specs=[per_b(t_new, N_HEADS * HEAD_W), per_b(t_new, N_KV * LANE),
                  per_b(n_pad, NSA_KV), per_b(n_pad, NSA_KV), per_b(PAGE, NSA_KV), per_b(PAGE, NSA_KV),
                  per_b(kw_src.shape[1], NSA_KV), per_b(kw_src.shape[1], NSA_KV)]
        + _page_specs(layer, 2)
        + [fixed(*bias.shape), fixed(*bias_c.shape), fixed(*bias_w.shape), fixed(*wimp.shape)],
        out_specs=per_b(t_new, N_HEADS * HEAD_W),
        scratch_shapes=[
            pltpu.VMEM((N_KV, rows, HEAD_W), BF16),
            pltpu.VMEM((N_KV, t_new, n_cols), F32),
            pltpu.VMEM((N_KV, rows, HEAD_W), F32),
            pltpu.VMEM((N_KV, rows, HEAD_W), F32),
            pltpu.VMEM((N_KV, rows, 1), F32),
            pltpu.VMEM((N_KV, rows, 1), F32),
            pltpu.VMEM((N_KV, rows, HEAD_W), F32),
        ],
    )
    return pl.pallas_call(
        functools.partial(_nsa_decode_kernel, n_pages=n_pages, past_len=past_len),
        grid_spec=grid_spec,
        out_shape=jax.ShapeDtypeStruct(q.shape, F32),
        compiler_params=_params(("parallel", "arbitrary")),
        name="nsa_decode",
    )(page_table, q, gates, kc, vc, k_new, v_new, kw_src, vw_src,
      *([cache_k] * PAGES_PER_STEP), *([cache_v] * PAGES_PER_STEP), bias, bias_c, bias_w, wimp)


def _gate_weights(w_g):
    d = w_g.shape[0]
    w = jnp.transpose(w_g.reshape(d, 3, N_KV, HPG), (0, 2, 1, 3)).reshape(d, N_KV, 3 * HPG)
    return jnp.pad(w, ((0, 0), (0, 0), (0, LANE - 3 * HPG))).reshape(d, N_KV * LANE)


def nsa_layer(xp, xs, gain, w_qkv, w_g, q_norm, k_norm, pe, w1, w2, w_out, caches, layer, page_table,
              bias_p, u_bias, bias_s, bias_sc, bias_sw, past_len):
    cmp_k, cmp_v, slc_k, slc_v, win_k, win_v = caches
    outs = []
    for x, paged in ((xp, False), (xs, True)):
        b, t, d = x.shape
        x2 = x.reshape(b * t, d)
        proj = dense(x2, w_qkv, gain=gain)
        gates = dense(x2, w_g, gain=gain).reshape(b, t, -1)
        q, kcr, vcr, ks, vs, kw, vw = (a.reshape(b, t, -1) for a in nsa_post(proj, q_norm, k_norm))
        if paged:
            kc = compress(cmp_k, layer, page_table, pe[0], w1[0], w2[0], k_norm[0])
            vc = compress(cmp_v, layer, page_table, pe[1], w1[1], w2[1], None)
            n_wt = bias_sw.shape[0]
            kw_src = jnp.concatenate([win_k[layer], kw], axis=1)
            vw_src = jnp.concatenate([win_v[layer], vw], axis=1)
            o = nsa_decode(q, gates, kc, vc, _pad_rows(ks, PAGE), _pad_rows(vs, PAGE),
                           _pad_rows(kw_src, n_wt * LANE), _pad_rows(vw_src, n_wt * LANE),
                           slc_k, slc_v, layer, page_table, bias_s, bias_sc, bias_sw, past_len)
            kw_out, vw_out = kw_src[:, -WINDOW:], vw_src[:, -WINDOW:]
        else:
            n_pg = t // PAGE
            ident = jnp.arange(b * n_pg, dtype=jnp.int32).reshape(b, n_pg)
            as_pages = lambda a: a.reshape(1, b * n_pg, PAGE, N_KV, HEAD_W)
            kc = compress(as_pages(kcr), 0, ident, pe[0], w1[0], w2[0], k_norm[0])
            vc = compress(as_pages(vcr), 0, ident, pe[1], w1[1], w2[1], None)
            o = nsa_prompt(q, gates, kc, vc, ks, vs, kw, vw, bias_p, u_bias)
            kw_out, vw_out = kw[:, -WINDOW:], vw[:, -WINDOW:]
        y = dense(o.reshape(b * t, -1), w_out, res=x2).reshape(b, t, d)
        st = tuple(a.reshape(b, -1, N_KV, HEAD_W) for a in (kcr, vcr, ks, vs, kw_out, vw_out))
        outs.append((y, st))
    (yp, stp), (ys, sts) = outs
    return yp, ys, stp, sts


def kernel(x_prompt, x_sample, cache_diff_k, cache_diff_v, state_ssm, state_conv, cache_nsa_cmp_k, cache_nsa_cmp_v, cache_nsa_slc_k, cache_nsa_slc_v, cache_nsa_win_k, cache_nsa_win_v, cache_mem_k, cache_mem_v, page_table, mem_prompt, rel_bias_table, norm_mix, norm_xattn, norm_mem, norm_ffn, diff_w_in, diff_q_norm, diff_k_norm, diff_lambda, diff_sub_norm, diff_w_out, ssm_w_in, ssm_conv_w, ssm_conv_b, ssm_dt_bias, ssm_a_log, ssm_d, ssm_norm, ssm_w_out, nsa_w_in, nsa_q_norm, nsa_k_norm, nsa_cmp_pe, nsa_cmp_w1, nsa_cmp_w2, nsa_w_out, xattn_w_q, xattn_w_k, xattn_w_v, xattn_q_norm, xattn_k_norm, xattn_w_o, ffn_w1, ffn_w3, ffn_w2):
    xp, xs = x_prompt, x_sample
    bp, t, d = xp.shape
    bs, t_new, _ = xs.shape
    past_len = page_table.shape[1] * PAGE
    depth = norm_mix.shape[0]
    bf = lambda a: a.astype(BF16)

    tq = 128
    bias_p = bias_tiles(_prompt_attn_buckets(tq), rel_bias_table)
    u_bias = bias_tiles(_prompt_cmp_buckets(tq, t // CMP_STRIDE), rel_bias_table)
    bias_s = bias_tiles(_decode_buckets(t_new, past_len), rel_bias_table)
    bias_sc = bias_tiles(_decode_cmp_buckets(t_new, past_len), rel_bias_table)
    n_wt = -(-(WINDOW + t_new) // LANE)
    bias_sw = bias_tiles(_decode_win_buckets(t_new, n_wt), rel_bias_table)

    mem_k, mem_v = mem_kv(mem_prompt, norm_mem, bf(xattn_w_k), bf(xattn_w_v), xattn_k_norm)
    xw = X_HEADS * X_DH
    win_k = cache_nsa_win_k.reshape(*cache_nsa_win_k.shape[:3], NSA_KV)
    win_v = cache_nsa_win_v.reshape(*cache_nsa_win_v.shape[:3], NSA_KV)

    dkp, dvp, dks, dvs = [], [], [], []
    ssp, cvp, sss, cvs = [], [], [], []
    nsp, nss = [], []
    for i in range(depth):
        kind, j = i % 3, i // 3
        if kind == 0:
            lam_init = 0.8 - 0.6 * math.exp(-0.3 * i)
            xp, xs, kp_, vp_, ks_, vs_ = diff_layer(
                xp, xs, norm_mix[i], bf(diff_w_in[j]), diff_q_norm[j], diff_k_norm[j], diff_lambda[j],
                diff_sub_norm[j], bf(diff_w_out[j]), cache_diff_k, cache_diff_v, j, page_table,
                bias_p, bias_s, lam_init)
            dkp.append(kp_)
            dvp.append(vp_)
            dks.append(ks_)
            dvs.append(vs_)
        elif kind == 1:
            nzx = D_INNER + CONV_DIM
            w_zx = bf(ssm_w_in[j][:, :nzx])
            w_dt = bf(jnp.pad(ssm_w_in[j][:, nzx:], ((0, 0), (0, LANE - SSM_HEADS))))
            xp, xs, hp_, cp_, hs_, cs_ = ssd_layer(
                xp, xs, norm_mix[i], w_zx, w_dt, ssm_conv_w[j], ssm_conv_b[j], ssm_dt_bias[j], ssm_a_log[j],
                ssm_d[j], ssm_norm[j], bf(ssm_w_out[j]), state_conv[j], state_ssm[j])
            ssp.append(hp_)
            cvp.append(cp_)
            sss.append(hs_)
            cvs.append(cs_)
        else:
            nqkv = N_HEADS * HEAD_W + 6 * NSA_KV
            w_qkv = bf(nsa_w_in[j][:, :nqkv])
            w_g = bf(_gate_weights(nsa_w_in[j][:, nqkv:]))
            caches = (cache_nsa_cmp_k, cache_nsa_cmp_v, cache_nsa_slc_k, cache_nsa_slc_v, win_k, win_v)
            xp, xs, stp, sts = nsa_layer(
                xp, xs, norm_mix[i], w_qkv, w_g, nsa_q_norm[j], nsa_k_norm[j], nsa_cmp_pe[j], nsa_cmp_w1[j],
                nsa_cmp_w2[j], bf(nsa_w_out[j]), caches, j, page_table,
                bias_p, u_bias, bias_s, bias_sc, bias_sw, past_len)
            nsp.append(stp)
            nss.append(sts)
        wq, wo = bf(xattn_w_q[i]), bf(xattn_w_o[i])
        xp = xattn(xp, norm_xattn[i], wq, xattn_q_norm[i], mem_k[i], mem_v[i], wo)
        xs = xattn(xs, norm_xattn[i], wq, xattn_q_norm[i], cache_mem_k[i].reshape(bs, N_MEM, xw),
                   cache_mem_v[i].reshape(bs, N_MEM, xw), wo)
        w1, w3, w2 = bf(ffn_w1[i]), bf(ffn_w3[i]), bf(ffn_w2[i])
        xp = ffn(xp.reshape(bp * t, d), norm_ffn[i], w1, w3, w2).reshape(bp, t, d)
        xs = ffn(xs.reshape(bs * t_new, d), norm_ffn[i], w1, w3, w2).reshape(bs, t_new, d)

    st = lambda xs_: jnp.stack(xs_, axis=0)
    nsp_t = [st([s[a] for s in nsp]) for a in range(6)]
    nss_t = [st([s[a] for s in nss]) for a in range(6)]
    mem_shape = (depth, bp, N_MEM, X_HEADS, X_DH)
    return (xp, xs,
            st(dkp), st(dvp), st(dks), st(dvs),
            st(ssp), st(cvp), st(sss), st(cvs),
            *nsp_t, *nss_t,
            mem_k.reshape(mem_shape), mem_v.reshape(mem_shape))
```

```python
import functools
import math

import numpy as np
import jax
import jax.numpy as jnp
from jax import lax
from jax.experimental import pallas as pl
from jax.experimental.pallas import tpu as pltpu

F32 = jnp.float32
BF16 = jnp.bfloat16

D_MODEL = 2048
DEPTH = 4
PAGE = 128
N_HEADS = 16
N_KV = 4
HPG = N_HEADS // N_KV
DIFF_DH = 64
HEAD_W = 128
N_BUCKETS = 32
MAX_EXACT = 16
MAX_DIST = 128
D_INNER = 2 * D_MODEL
SSM_HEADDIM = 64
SSM_HEADS = D_INNER // SSM_HEADDIM
SSM_GROUPS = 8
SSM_HPG = SSM_HEADS // SSM_GROUPS
D_STATE = 128
CONV_W = 4
CONV_DIM = D_INNER + 2 * SSM_GROUPS * D_STATE
SSM_CHUNK = 128
CMP_STRIDE = 16
SLC_BLOCK = 64
SLC_RATIO = SLC_BLOCK // CMP_STRIDE
N_SELECT = 16
WINDOW = 512
N_MEM = 256
X_HEADS = 4
X_DH = 128
EPS = 1e-6
NEG_INF = -1e30
FORCE_SCORE = 1e4

LANE = 128
VMEM_LIMIT = 56 * 1024 * 1024


def _params(sem):
    return pltpu.CompilerParams(dimension_semantics=sem, vmem_limit_bytes=VMEM_LIMIT)


def _rms(x, gain):
    return x * lax.rsqrt(jnp.mean(x * x, axis=-1, keepdims=True) + EPS) * gain


def _dot(a, b):
    return jnp.dot(a, b, preferred_element_type=F32)


def _dot_t(a, b):
    return lax.dot_general(a, b, (((1,), (1,)), ((), ())), preferred_element_type=F32)


def _split3(x):
    hi = x.astype(BF16)
    r1 = x - hi.astype(F32)
    mid = r1.astype(BF16)
    lo = (r1 - mid.astype(F32)).astype(BF16)
    return hi, mid, lo


def _dot_exact_rhs(x, m_bf16):
    hi, mid, lo = _split3(x)
    return _dot(hi, m_bf16) + _dot(mid, m_bf16) + _dot(lo, m_bf16)


def _dense_kernel(*refs, norm, residual):
    refs = list(refs)
    x_ref = refs.pop(0)
    g_ref = refs.pop(0) if norm else None
    w_ref = refs.pop(0)
    r_ref = refs.pop(0) if residual else None
    o_ref, xb_ref = refs

    @pl.when(pl.program_id(1) == 0)
    def _():
        x = x_ref[...]
        if norm:
            x = _rms(x, g_ref[...])
        xb_ref[...] = x.astype(BF16)

    y = _dot(xb_ref[...], w_ref[0].astype(BF16))
    if residual:
        y = y + r_ref[...]
    o_ref[...] = y


def dense(x, w, layer, gain=None, res=None, n=None):
    m, k = x.shape
    n = w.shape[2] if n is None else n
    tm = m if m <= 512 else (1024 if k <= D_MODEL else 512)
    tn = LANE if n % 256 else (512 if k <= D_MODEL and n % 512 == 0 else 256)
    assert m % tm == 0 and n % tn == 0
    norm, residual = gain is not None, res is not None
    in_specs = [pl.BlockSpec((tm, k), lambda i, j: (i, 0))]
    args = [x]
    if norm:
        in_specs.append(pl.BlockSpec((1, k), lambda i, j: (0, 0)))
        args.append(gain.reshape(1, k))
    in_specs.append(pl.BlockSpec((1, k, tn), lambda i, j: (layer, 0, j)))
    args.append(w)
    if residual:
        in_specs.append(pl.BlockSpec((tm, tn), lambda i, j: (i, j)))
        args.append(res)
    return pl.pallas_call(
        functools.partial(_dense_kernel, norm=norm, residual=residual),
        grid=(m // tm, n // tn),
        in_specs=in_specs,
        out_specs=pl.BlockSpec((tm, tn), lambda i, j: (i, j)),
        out_shape=jax.ShapeDtypeStruct((m, n), F32),
        scratch_shapes=[pltpu.VMEM((tm, k), BF16)],
        compiler_params=_params(("parallel", "arbitrary")),
        name="dense",
    )(*args)


def _ffn_kernel(x_ref, g_ref, w1_ref, w3_ref, w2_ref, o_ref, xb_ref):
    @pl.when(pl.program_id(1) == 0)
    def _():
        x = x_ref[...]
        xb_ref[...] = _rms(x, g_ref[...]).astype(BF16)
        o_ref[...] = x

    xb = xb_ref[...]
    h1 = _dot(xb, w1_ref[0].astype(BF16))
    h3 = _dot(xb, w3_ref[0].astype(BF16))
    a = (h1 * jax.nn.sigmoid(h1) * h3).astype(BF16)
    o_ref[...] += _dot(a, w2_ref[0].astype(BF16))


def ffn(x, gain, w1, w3, w2, layer):
    m, d = x.shape
    f = w1.shape[2]
    tm = min(m, 1024)
    tf = 256
    assert m % tm == 0 and f % tf == 0
    return pl.pallas_call(
        _ffn_kernel,
        grid=(m // tm, f // tf),
        in_specs=[
            pl.BlockSpec((tm, d), lambda i, j: (i, 0), pipeline_mode=pl.Buffered(1)),
            pl.BlockSpec((1, d), lambda i, j: (0, 0)),
            pl.BlockSpec((1, d, tf), lambda i, j: (layer, 0, j)),
            pl.BlockSpec((1, d, tf), lambda i, j: (layer, 0, j)),
            pl.BlockSpec((1, tf, d), lambda i, j: (layer, j, 0)),
        ],
        out_specs=pl.BlockSpec((tm, d), lambda i, j: (i, 0)),
        out_shape=jax.ShapeDtypeStruct((m, d), F32),
        scratch_shapes=[pltpu.VMEM((tm, d), BF16)],
        compiler_params=_params(("parallel", "arbitrary")),
        name="ffn",
    )(x, gain.reshape(1, d), w1, w3, w2)


def _mem_kv_kernel(mem_ref, g_ref, wk_ref, wv_ref, kn_ref, k_ref, v_ref):
    m = _rms(mem_ref[0], g_ref[0]).astype(BF16)
    k = _dot(m, wk_ref[0].astype(BF16))
    v_ref[0, 0] = _dot(m, wv_ref[0].astype(BF16))
    for h in range(X_HEADS):
        sl = slice(h * X_DH, (h + 1) * X_DH)
        k_ref[0, 0, :, sl] = _rms(k[:, sl], kn_ref[0])


def mem_kv(mem, g_mem, wk, wv, k_norm):
    b = mem.shape[0]
    nl = wk.shape[0]
    hw = X_HEADS * X_DH
    shape = jax.ShapeDtypeStruct((nl, b, N_MEM, hw), F32)
    return pl.pallas_call(
        _mem_kv_kernel,
        grid=(nl, b),
        in_specs=[
            pl.BlockSpec((1, N_MEM, D_MODEL), lambda l, i: (i, 0, 0)),
            pl.BlockSpec((1, 1, D_MODEL), lambda l, i: (l, 0, 0)),
            pl.BlockSpec((1, D_MODEL, hw), lambda l, i: (l, 0, 0)),
            pl.BlockSpec((1, D_MODEL, hw), lambda l, i: (l, 0, 0)),
            pl.BlockSpec((1, 1, X_DH), lambda l, i: (l, 0, 0)),
        ],
        out_specs=[pl.BlockSpec((1, 1, N_MEM, hw), lambda l, i: (l, i, 0, 0))] * 2,
        out_shape=[shape, shape],
        compiler_params=_params(("parallel", "parallel")),
        name="mem_kv",
    )(mem, g_mem.reshape(nl, 1, D_MODEL), wk, wv, k_norm.reshape(nl, 1, X_DH))


def _xattn_kernel(x_ref, g_ref, wq_ref, qn_ref, k_ref, v_ref, wo_ref, o_ref):
    x = x_ref[0]
    q = _dot(_rms(x, g_ref[...]).astype(BF16), wq_ref[0].astype(BF16))
    outs = []
    for h in range(X_HEADS):
        sl = slice(h * X_DH, (h + 1) * X_DH)
        qh = _rms(q[:, sl], qn_ref[...]).astype(BF16)
        s = _dot_t(qh, k_ref[0, :, sl].astype(BF16)) * (X_DH ** -0.5)
        e = jnp.exp(s - jnp.max(s, axis=-1, keepdims=True))
        p = e / jnp.sum(e, axis=-1, keepdims=True)
        outs.append(_dot(p.astype(BF16), v_ref[0, :, sl].astype(BF16)))
    o = jnp.concatenate(outs, axis=1).astype(BF16)
    o_ref[0] = x + _dot(o, wo_ref[0].astype(BF16))


def xattn(x, gain, wq, q_norm, k, v, wo, layer):
    b, t, d = x.shape
    hw = X_HEADS * X_DH
    tm = min(t, 512)
    return pl.pallas_call(
        _xattn_kernel,
        grid=(b, t // tm),
        in_specs=[
            pl.BlockSpec((1, tm, d), lambda i, j: (i, j, 0)),
            pl.BlockSpec((1, d), lambda i, j: (0, 0)),
            pl.BlockSpec((1, d, hw), lambda i, j: (layer, 0, 0)),
            pl.BlockSpec((1, X_DH), lambda i, j: (0, 0)),
            pl.BlockSpec((1, N_MEM, hw), lambda i, j: (i, 0, 0)),
            pl.BlockSpec((1, N_MEM, hw), lambda i, j: (i, 0, 0)),
            pl.BlockSpec((1, hw, d), lambda i, j: (layer, 0, 0)),
        ],
        out_specs=pl.BlockSpec((1, tm, d), lambda i, j: (i, j, 0)),
        out_shape=jax.ShapeDtypeStruct((b, t, d), F32),
        compiler_params=_params(("parallel", "parallel")),
        name="xattn",
    )(x, gain.reshape(1, d), wq, q_norm.reshape(1, X_DH), k, v, wo)


def _bucket_np(dist):
    n = np.maximum(dist, 0)
    nf = np.maximum(n, 1).astype(np.float64)
    large = MAX_EXACT + (np.log(nf / MAX_EXACT) / math.log(MAX_DIST / MAX_EXACT)
                         * (N_BUCKETS - MAX_EXACT)).astype(np.int64)
    b = np.where(n < MAX_EXACT, n, np.minimum(large, N_BUCKETS - 1))
    return np.where(dist < 0, -1, b).astype(np.int32)


def _bias_kernel(bkt_ref, tab_ref, o_ref):
    h = pl.program_id(1)
    b = bkt_ref[0]
    acc = jnp.full(b.shape, NEG_INF, F32)
    for k in range(N_BUCKETS):
        acc = jnp.where(b == k, tab_ref[k, h], acc)
    o_ref[0] = acc


def bias_tiles(buckets, table, heads_on_lanes=False):
    nt, r, w = buckets.shape
    if heads_on_lanes:
        out_spec = pl.BlockSpec((1, r, w), lambda t, h: (t, 0, h))
        out_shape = (nt, r, N_HEADS * w)
    else:
        out_spec = pl.BlockSpec((1, r, w), lambda t, h: (t, h, 0))
        out_shape = (nt, N_HEADS * r, w)
    return pl.pallas_call(
        _bias_kernel,
        grid=(nt, N_HEADS),
        in_specs=[
            pl.BlockSpec((1, r, w), lambda t, h: (t, 0, 0)),
            pl.BlockSpec(memory_space=pltpu.SMEM),
        ],
        out_specs=out_spec,
        out_shape=jax.ShapeDtypeStruct(out_shape, F32),
        compiler_params=_params(("parallel", "parallel")),
        name="bias_tiles",
    )(jnp.asarray(buckets), table)


def _prompt_attn_buckets(tq):
    assert WINDOW % tq == 0 and tq >= MAX_DIST
    j = np.arange(tq)[:, None]
    i = np.arange(tq)[None, :]
    far = np.full((tq, tq), N_BUCKETS - 1, np.int32)
    return np.stack([
        _bucket_np(i - j),
        _bucket_np(i - j + tq),
        far,
        np.where(j > i, far, -1),
    ]).astype(np.int32)


def _norm64(blk, gain, lo):
    sq = blk * blk
    s_lo = jnp.sum(jnp.where(lo, sq, 0.0), axis=-1, keepdims=True)
    s_hi = jnp.sum(jnp.where(lo, 0.0, sq), axis=-1, keepdims=True)
    ms = jnp.where(lo, s_lo, s_hi) * (1.0 / DIFF_DH)
    return blk * lax.rsqrt(ms + EPS) * gain


def _diff_post_kernel(x_ref, qg_ref, kg_ref, q_ref, k_ref, v_ref):
    tm = x_ref.shape[0]
    lo = lax.broadcasted_iota(jnp.int32, (tm, LANE), 1) < DIFF_DH
    nq = N_HEADS * HEAD_W
    nk = N_KV * HEAD_W
    for c in range(N_HEADS):
        sl = slice(c * LANE, (c + 1) * LANE)
        q_ref[:, sl] = _norm64(x_ref[:, sl], qg_ref[...], lo)
    for c in range(N_KV):
        sl = slice(c * LANE, (c + 1) * LANE)
        k_ref[:, sl] = _norm64(x_ref[:, nq + c * LANE: nq + (c + 1) * LANE], kg_ref[...], lo)
    v_ref[...] = x_ref[:, nq + nk:]


def diff_post(qkv, q_norm, k_norm):
    m, n = qkv.shape
    tm = min(m, 256)
    nq, nk = N_HEADS * HEAD_W, N_KV * HEAD_W
    return pl.pallas_call(
        _diff_post_kernel,
        grid=(m // tm,),
        in_specs=[
            pl.BlockSpec((tm, n), lambda i: (i, 0)),
            pl.BlockSpec((1, LANE), lambda i: (0, 0)),
            pl.BlockSpec((1, LANE), lambda i: (0, 0)),
        ],
        out_specs=[
            pl.BlockSpec((tm, nq), lambda i: (i, 0)),
            pl.BlockSpec((tm, nk), lambda i: (i, 0)),
            pl.BlockSpec((tm, nk), lambda i: (i, 0)),
        ],
        out_shape=[jax.ShapeDtypeStruct((m, nq), F32), jax.ShapeDtypeStruct((m, nk), F32),
                   jax.ShapeDtypeStruct((m, nk), F32)],
        compiler_params=_params(("parallel",)),
        name="diff_post",
    )(qkv, jnp.tile(q_norm, 2).reshape(1, LANE), jnp.tile(k_norm, 2).reshape(1, LANE))


def _online_update(s, v, m_ref, l_ref, acc_ref):
    m_old = m_ref[...]
    m_new = jnp.maximum(m_old, jnp.max(s, axis=-1, keepdims=True))
    alpha = jnp.exp(m_old - m_new)
    p = jnp.exp(s - m_new)
    l_ref[...] = alpha * l_ref[...] + jnp.sum(p, axis=-1, keepdims=True)
    acc_ref[...] = alpha * acc_ref[...] + _dot(p.astype(BF16), v)
    m_ref[...] = m_new


def _diff_lambda(lam_ref, lam_init):
    lf = lam_ref[...]
    s01 = jnp.sum(lf[0:1] * lf[1:2], axis=-1, keepdims=True)
    s23 = jnp.sum(lf[2:3] * lf[3:4], axis=-1, keepdims=True)
    return jnp.exp(s01) - jnp.exp(s23) + lam_init


def _split_maps(qs):
    lo = lax.broadcasted_iota(jnp.int32, qs.shape, 1) < DIFF_DH
    return jnp.concatenate([jnp.where(lo, qs, 0.0), jnp.where(lo, 0.0, qs)], axis=0)


def _diff_finish_rows(o, lam, sub_norm, lam_init):
    r = o.shape[0] // 2
    a = o[:r] - lam * o[r:]
    return _rms(a, sub_norm) * (1.0 - lam_init)


def _diff_finish(acc, l, lam, sub_norm, lam_init):
    return _diff_finish_rows(acc / l, lam, sub_norm, lam_init)


PROMPT_TQ = 256


def _flash_step_t(s, vt, m_ref, l_ref, acc_ref):
    m_old = m_ref[...]
    m_new = jnp.maximum(m_old, jnp.max(s, axis=0, keepdims=True))
    alpha = jnp.exp(m_old - m_new)
    p = jnp.exp(s - m_new)
    l_ref[...] = alpha * l_ref[...] + jnp.sum(p, axis=0, keepdims=True)
    acc_ref[...] = alpha * acc_ref[...] + _dot(vt, p.astype(BF16))
    m_ref[...] = m_new


def _stage_kv(k_ref, v_ref, kb_ref, vt_ref, chunk):
    for c in range(k_ref.shape[1] // chunk):
        rows = slice(c * chunk, (c + 1) * chunk)
        kb_ref[rows, :] = k_ref[0, rows, :].astype(BF16)
        vt_ref[:, rows] = v_ref[0, rows, :].T.astype(BF16)


def _diff_flash_kernel(lam_ref, q_ref, k_ref, v_ref, bias_ref, sn_ref, o_ref,
                       qs_ref, kb_ref, vt_ref, m_ref, l_ref, acc_ref, *, tq, lam_init):
    qi = pl.program_id(2)

    @pl.when(qi == 0)
    def _():
        _stage_kv(k_ref, v_ref, kb_ref, vt_ref, tq)

    qb = q_ref[0]
    qs = jnp.concatenate([qb[:, r * HEAD_W:(r + 1) * HEAD_W] for r in range(HPG)], axis=0)
    qs_ref[...] = _split_maps(qs * (DIFF_DH ** -0.5)).astype(BF16)
    _reset_flash(m_ref, l_ref, acc_ref)

    def body(kt, c):
        rows = pl.ds(pl.multiple_of(kt * tq, tq), tq)
        b = bias_ref[jnp.minimum(qi - kt, 2)]
        s = _dot_t(kb_ref[rows, :], qs_ref[...]) + jnp.concatenate([b, b], axis=1)
        _flash_step_t(s, vt_ref[:, rows], m_ref, l_ref, acc_ref)
        return c

    lax.fori_loop(0, qi + 1, body, 0)
    o = (acc_ref[...] / l_ref[...]).T
    a = _diff_finish_rows(o, _diff_lambda(lam_ref, lam_init), sn_ref[...], lam_init)
    for r in range(HPG):
        o_ref[0, :, r * HEAD_W:(r + 1) * HEAD_W] = a[r * tq:(r + 1) * tq]


def diff_flash(q, k, v, bias, lam_p, sub_norm, lam_init, tq=PROMPT_TQ):
    b, t, _ = q.shape
    gw = HPG * HEAD_W
    rows = 2 * HPG * tq
    return pl.pallas_call(
        functools.partial(_diff_flash_kernel, tq=tq, lam_init=lam_init),
        grid=(b, N_KV, t // tq),
        in_specs=[
            pl.BlockSpec((4, DIFF_DH), lambda i, g, j: (0, 0)),
            pl.BlockSpec((1, tq, gw), lambda i, g, j: (i, j, g)),
            pl.BlockSpec((1, t, HEAD_W), lambda i, g, j: (i, 0, g)),
            pl.BlockSpec((1, t, HEAD_W), lambda i, g, j: (i, 0, g)),
            pl.BlockSpec((4, tq, HPG * tq), lambda i, g, j: (0, 0, g)),
            pl.BlockSpec((1, HEAD_W), lambda i, g, j: (0, 0)),
        ],
        out_specs=pl.BlockSpec((1, tq, gw), lambda i, g, j: (i, j, g)),
        out_shape=jax.ShapeDtypeStruct(q.shape, F32),
        scratch_shapes=[
            pltpu.VMEM((rows, HEAD_W), BF16),
            pltpu.VMEM((t, HEAD_W), BF16),
            pltpu.VMEM((HEAD_W, t), BF16),
            pltpu.VMEM((1, rows), F32),
            pltpu.VMEM((1, rows), F32),
            pltpu.VMEM((HEAD_W, rows), F32),
        ],
        compiler_params=_params(("parallel", "parallel", "arbitrary")),
        name="diff_flash",
    )(lam_p, q, k, v, bias, sub_norm.reshape(1, HEAD_W))


PAGES_PER_STEP = 4


def _decode_buckets(t_new, past_len):
    i = np.arange(t_new)[:, None]
    j = np.arange(PAGE)[None, :]
    far = np.full((t_new, PAGE), N_BUCKETS - 1, np.int32)
    new = np.where(j < t_new, _bucket_np(i - j), -1)
    return np.stack([far, _bucket_np(PAGE + i - j), new]).astype(np.int32)


def _stack_group_heads(q_ref, g):
    return jnp.concatenate(
        [q_ref[0, :, (g * HPG + r) * HEAD_W:(g * HPG + r + 1) * HEAD_W] for r in range(HPG)], axis=0)


def _page_bias(bias_ref, g, rows, first_page, n_pages):
    tiles = []
    for u in range(PAGES_PER_STEP):
        typ = jnp.where(first_page + u == n_pages - 1, 1, 0)
        tiles.append(bias_ref[typ, g * rows:(g + 1) * rows, :])
    return jnp.concatenate(tiles, axis=1)


def _diff_decode_kernel(pt_ref, lam_ref, q_ref, kn_ref, vn_ref, *rest, n_pages, lam_init):
    del pt_ref
    kp = rest[:PAGES_PER_STEP]
    vp = rest[PAGES_PER_STEP:2 * PAGES_PER_STEP]
    bias_ref, sn_ref, o_ref, qx_ref, m_ref, l_ref, acc_ref = rest[2 * PAGES_PER_STEP:]
    s = pl.program_id(1)
    t_new = q_ref.shape[1]
    rows = HPG * t_new

    @pl.when(s == 0)
    def _():
        m_ref[...] = jnp.full(m_ref.shape, NEG_INF, F32)
        l_ref[...] = jnp.zeros(l_ref.shape, F32)
        acc_ref[...] = jnp.zeros(acc_ref.shape, F32)
        for g in range(N_KV):
            sl = slice(g * HEAD_W, (g + 1) * HEAD_W)
            qx_ref[g] = _split_maps(_stack_group_heads(q_ref, g) * (DIFF_DH ** -0.5)).astype(BF16)
            b = bias_ref[2, g * rows:(g + 1) * rows, :]
            sc = _dot_t(qx_ref[g], kn_ref[0, :, sl].astype(BF16)) + jnp.concatenate([b, b], axis=0)
            _online_update(sc, vn_ref[0, :, sl].astype(BF16), m_ref.at[g], l_ref.at[g], acc_ref.at[g])

    for g in range(N_KV):
        sl = slice(g * HEAD_W, (g + 1) * HEAD_W)
        kc = jnp.concatenate([r[0, 0, :, g, :] for r in kp], axis=0).astype(BF16)
        vc = jnp.concatenate([r[0, 0, :, g, :] for r in vp], axis=0).astype(BF16)
        b = _page_bias(bias_ref, g, rows, s * PAGES_PER_STEP, n_pages)
        sc = _dot_t(qx_ref[g], kc) + jnp.concatenate([b, b], axis=0)
        _online_update(sc, vc, m_ref.at[g], l_ref.at[g], acc_ref.at[g])

    @pl.when(s == pl.num_programs(1) - 1)
    def _():
        lam = _diff_lambda(lam_ref, lam_init)
        for g in range(N_KV):
            a = _diff_finish(acc_ref[g], l_ref[g], lam, sn_ref[...], lam_init)
            for r in range(HPG):
                h = g * HPG + r
                o_ref[0, :, h * HEAD_W:(h + 1) * HEAD_W] = a[r * t_new:(r + 1) * t_new]


def _page_specs(layer, n):
    def spec(u):
        return pl.BlockSpec((1, 1, PAGE, N_KV, HEAD_W),
                            lambda i, s, pt: (layer, pt[i, s * PAGES_PER_STEP + u], 0, 0, 0))
    return [spec(u) for u in range(PAGES_PER_STEP)] * n


def diff_decode(q, k_new, v_new, cache_k, cache_v, layer, page_table, bias, lam_p, sub_norm, lam_init):
    b, t_new, _ = q.shape
    n_pages = page_table.shape[1]
    kvw = N_KV * HEAD_W
    rows = 2 * HPG * t_new
    fixed = lambda *shape: pl.BlockSpec(shape, lambda i, s, pt: (0,) * len(shape))
    per_b = lambda *shape: pl.BlockSpec((1,) + shape, lambda i, s, pt: (i,) + (0,) * len(shape))
    grid_spec = pltpu.PrefetchScalarGridSpec(
        num_scalar_prefetch=1,
        grid=(b, n_pages // PAGES_PER_STEP),
        in_specs=[fixed(4, DIFF_DH), per_b(t_new, N_HEADS * HEAD_W), per_b(PAGE, kvw), per_b(PAGE, kvw)]
        + _page_specs(layer, 2)
        + [fixed(*bias.shape), fixed(1, HEAD_W)],
        out_specs=per_b(t_new, N_HEADS * HEAD_W),
        scratch_shapes=[
            pltpu.VMEM((N_KV, rows, HEAD_W), BF16),
            pltpu.VMEM((N_KV, rows, 1), F32),
            pltpu.VMEM((N_KV, rows, 1), F32),
            pltpu.VMEM((N_KV, rows, HEAD_W), F32),
        ],
    )
    return pl.pallas_call(
        functools.partial(_diff_decode_kernel, n_pages=n_pages, lam_init=lam_init),
        grid_spec=grid_spec,
        out_shape=jax.ShapeDtypeStruct(q.shape, F32),
        compiler_params=_params(("parallel", "arbitrary")),
        name="diff_decode",
    )(page_table, lam_p, q, k_new, v_new, *([cache_k] * PAGES_PER_STEP), *([cache_v] * PAGES_PER_STEP),
      bias, sub_norm.reshape(1, HEAD_W))


def _pad_rows(a, n):
    return jnp.pad(a, ((0, 0), (0, n - a.shape[1]), (0, 0)))


def diff_layer(xp, xs, gain, w_in, q_norm, k_norm, lam_p, sub_norm, w_out, cache_k, cache_v, layer,
               page_table, bias_p, bias_s, lam_init):
    outs = []
    for x, paged in ((xp, False), (xs, True)):
        b, t, d = x.shape
        x2 = x.reshape(b * t, d)
        q, k, v = diff_post(dense(x2, w_in, layer, gain=gain), q_norm, k_norm)
        q3, k3, v3 = (a.reshape(b, t, -1) for a in (q, k, v))
        if paged:
            o = diff_decode(q3, _pad_rows(k3, PAGE), _pad_rows(v3, PAGE), cache_k, cache_v, layer,
                            page_table, bias_s, lam_p, sub_norm, lam_init)
        else:
            o = diff_flash(q3, k3, v3, bias_p, lam_p, sub_norm, lam_init)
        y = dense(o.reshape(b * t, -1), w_out, layer, res=x2).reshape(b, t, d)
        outs.append((y, k3.reshape(b, t, N_KV, HEAD_W), v3.reshape(b, t, N_KV, HEAD_W)))
    (yp, kp, vp), (ys, ks, vs) = outs
    return yp, ys, kp, vp, ks, vs


SSM_GW = SSM_HPG * SSM_HEADDIM
SSM_BC = 2 * SSM_GROUPS * D_STATE
CONV_PAD = 8


def _conv_silu(buf_ref, w_ref, b_ref, n):
    acc = b_ref[...]
    for k in range(CONV_W):
        acc = acc + buf_ref[pl.ds(CONV_PAD - (CONV_W - 1) + k, n), :] * w_ref[k:k + 1, :]
    return acc * jax.nn.sigmoid(acc)


def _ssd_kernel(z_ref, x_ref, bc_ref, dt_ref, cbx_ref, cbbc_ref, wx_ref, wbc_ref, bx_ref, bbc_ref,
                dtb_ref, alog_ref, dsk_ref, ng_ref, e_ref, tri_ref, h0_ref,
                y_ref, hout_ref, ht_ref, xbuf_ref, bcbuf_ref, xa_ref, bca_ref, *, t_valid):
    c = pl.program_id(1)
    n = x_ref.shape[1]

    @pl.when(c == 0)
    def _():
        for g in range(SSM_GROUPS):
            ht_ref[g] = h0_ref[0, g].T
        xbuf_ref[0:CONV_PAD] = cbx_ref[0]
        bcbuf_ref[0:CONV_PAD] = cbbc_ref[0]

    xbuf_ref[CONV_PAD:CONV_PAD + n] = x_ref[0]
    bcbuf_ref[CONV_PAD:CONV_PAD + n] = bc_ref[0]
    xa_ref[...] = _conv_silu(xbuf_ref, wx_ref, bx_ref, n)
    bca_ref[...] = _conv_silu(bcbuf_ref, wbc_ref, bbc_ref, n)
    xbuf_ref[0:CONV_PAD] = xbuf_ref[n:n + CONV_PAD]
    bcbuf_ref[0:CONV_PAD] = bcbuf_ref[n:n + CONV_PAD]

    row = lax.broadcasted_iota(jnp.int32, (n, LANE), 0) + c * n
    dtr = dt_ref[0] + dtb_ref[...]
    dt = jnp.maximum(dtr, 0.0) + jnp.log1p(jnp.exp(-jnp.abs(dtr)))
    dt = jnp.where(row < t_valid, dt, 0.0)
    dta = dt * (-jnp.exp(alog_ref[...]))
    hi, mid, lo = _split3(dta)
    tri = tri_ref[...]
    cs = _dot(tri, hi) + _dot(tri, mid) + _dot(tri, lo)
    cs_last = cs[n - 1:n, :]
    cs_t = cs.T
    dt_t = dt.T
    stacked = jnp.concatenate(
        [jnp.exp(cs), jnp.exp(cs_last - cs) * dt, jnp.broadcast_to(jnp.exp(cs_last), (8, LANE))], axis=0)
    ex = _dot_exact_rhs(stacked, e_ref[...])
    causal = (lax.broadcasted_iota(jnp.int32, (n, n), 0) >= lax.broadcasted_iota(jnp.int32, (n, n), 1))

    for g in range(SSM_GROUPS):
        gs = slice(g * SSM_GW, (g + 1) * SSM_GW)
        bm = bca_ref[:, g * D_STATE:(g + 1) * D_STATE]
        cm = bca_ref[:, (SSM_GROUPS + g) * D_STATE:(SSM_GROUPS + g + 1) * D_STATE].astype(BF16)
        cb = _dot_t(cm, bm.astype(BF16))
        xg = xa_ref[:, gs]
        ys = []
        for r in range(SSM_HPG):
            h = g * SSM_HPG + r
            seg = cs[:, h:h + 1] - cs_t[h:h + 1, :]
            dec = jnp.where(causal, jnp.exp(jnp.where(causal, seg, 0.0)), 0.0)
            mm = (cb * dec * dt_t[h:h + 1, :]).astype(BF16)
            ys.append(_dot(mm, xg[:, r * SSM_HEADDIM:(r + 1) * SSM_HEADDIM].astype(BF16)))
        ht = ht_ref[g]
        y = jnp.concatenate(ys, axis=1) + _dot(cm, ht.astype(BF16)) * ex[0:n, gs]
        y = y + dsk_ref[:, gs] * xg
        zg = z_ref[0, :, gs]
        y = y * (zg * jax.nn.sigmoid(zg))
        y_ref[0, :, gs] = _rms(y, ng_ref[:, gs])
        xw = (xg * ex[n:2 * n, gs]).astype(BF16)
        ht_ref[g] = ht * ex[2 * n:2 * n + 1, gs] + _dot(bm.T.astype(BF16), xw)

    @pl.when(c == pl.num_programs(1) - 1)
    def _():
        for g in range(SSM_GROUPS):
            hout_ref[0, g] = ht_ref[g].T


def _head_expand_matrix():
    e = np.zeros((LANE, D_INNER), np.float32)
    for h in range(SSM_HEADS):
        e[h, h * SSM_HEADDIM:(h + 1) * SSM_HEADDIM] = 1.0
    return e


def ssd_core(zx, dt_raw, conv_buf, h0, conv_w, conv_b, dt_bias, a_log, d_skip, norm_g, t_valid):
    b, t, _ = zx.shape
    n = SSM_CHUNK
    pad_h = LANE - SSM_HEADS
    cb = jnp.pad(conv_buf, ((0, 0), (CONV_PAD - (CONV_W - 1), 0), (0, 0)))
    tri = jnp.asarray(np.tril(np.ones((n, n), np.float32)), BF16)
    e = jnp.asarray(_head_expand_matrix(), BF16)
    d_exp = jnp.repeat(d_skip, SSM_HEADDIM).reshape(1, D_INNER)
    fixed = lambda *shape: pl.BlockSpec(shape, lambda i, c: (0,) * len(shape))
    per_b = lambda *shape: pl.BlockSpec((1,) + shape, lambda i, c: (i,) + (0,) * len(shape))
    y, h_last = pl.pallas_call(
        functools.partial(_ssd_kernel, t_valid=t_valid),
        grid=(b, t // n),
        in_specs=[
            pl.BlockSpec((1, n, D_INNER), lambda i, c: (i, c, 0)),
            pl.BlockSpec((1, n, D_INNER), lambda i, c: (i, c, 1)),
            pl.BlockSpec((1, n, SSM_BC), lambda i, c: (i, c, 2 * D_INNER // SSM_BC)),
            pl.BlockSpec((1, n, LANE), lambda i, c: (i, c, 0)),
            per_b(CONV_PAD, D_INNER), per_b(CONV_PAD, SSM_BC),
            fixed(CONV_W, D_INNER), fixed(CONV_W, SSM_BC), fixed(1, D_INNER), fixed(1, SSM_BC),
            fixed(1, LANE), fixed(1, LANE), fixed(1, D_INNER), fixed(1, D_INNER),
            fixed(LANE, D_INNER), fixed(n, n),
            per_b(SSM_GROUPS, SSM_GW, D_STATE),
        ],
        out_specs=[
            pl.BlockSpec((1, n, D_INNER), lambda i, c: (i, c, 0)),
            per_b(SSM_GROUPS, SSM_GW, D_STATE),
        ],
        out_shape=[jax.ShapeDtypeStruct((b, t, D_INNER), F32),
                   jax.ShapeDtypeStruct((b, SSM_GROUPS, SSM_GW, D_STATE), F32)],
        scratch_shapes=[
            pltpu.VMEM((SSM_GROUPS, D_STATE, SSM_GW), F32),
            pltpu.VMEM((n + CONV_PAD, D_INNER), F32),
            pltpu.VMEM((n + CONV_PAD, SSM_BC), F32),
            pltpu.VMEM((n, D_INNER), F32),
            pltpu.VMEM((n, SSM_BC), F32),
        ],
        compiler_params=_params(("parallel", "arbitrary")),
        name="ssd_core",
    )(zx, zx, zx, dt_raw, cb[:, :, :D_INNER], cb[:, :, D_INNER:],
      conv_w[:, :D_INNER], conv_w[:, D_INNER:], conv_b[:D_INNER].reshape(1, -1), conv_b[D_INNER:].reshape(1, -1),
      jnp.pad(dt_bias, (0, pad_h)).reshape(1, LANE), jnp.pad(a_log, (0, pad_h)).reshape(1, LANE),
      d_exp, norm_g.reshape(1, D_INNER), e, tri,
      h0.reshape(b, SSM_GROUPS, SSM_GW, D_STATE))
    return y, h_last.reshape(b, SSM_HEADS, SSM_HEADDIM, D_STATE)


def ssd_layer(xp, xs, gain, w_in, layer, conv_w, conv_b, dt_bias, a_log, d_skip, norm_g, w_out,
              state_conv, state_ssm):
    nzx = D_INNER + CONV_DIM
    w_dt = jnp.pad(w_in[layer, :, nzx:], ((0, 0), (0, LANE - SSM_HEADS)))[None]
    outs = []
    for x, conv_buf, h0 in ((xp, None, None), (xs, state_conv, state_ssm)):
        b, t, d = x.shape
        x2 = x.reshape(b * t, d)
        if conv_buf is None:
            conv_buf = jnp.zeros((b, CONV_W - 1, CONV_DIM), F32)
            h0 = jnp.zeros((b, SSM_HEADS, SSM_HEADDIM, D_STATE), F32)
        zx = dense(x2, w_in, layer, gain=gain, n=nzx).reshape(b, t, -1)
        dt_raw = dense(x2, w_dt, 0, gain=gain).reshape(b, t, LANE)
        tp = -(-t // SSM_CHUNK) * SSM_CHUNK
        y, h_last = ssd_core(_pad_rows(zx, tp), _pad_rows(dt_raw, tp), conv_buf, h0, conv_w, conv_b,
                             dt_bias, a_log, d_skip, norm_g, t)
        y = y[:, :t].reshape(b * t, D_INNER)
        xpad = jnp.concatenate([conv_buf, zx[:, :, D_INNER:]], axis=1)
        outs.append((dense(y, w_out, layer, res=x2).reshape(b, t, d), h_last, xpad[:, t:]))
    (yp, hp, cp), (ys, hs, cs) = outs
    return yp, ys, hp, cp, hs, cs


NSA_KV = N_KV * HEAD_W
NSA_SCALE = HEAD_W ** -0.5
CMP_PAIRS = CMP_STRIDE // 2
CMP_PAGES = 8


def _nsa_post_kernel(x_ref, qg_ref, kg_ref, q_ref, kcr_ref, vcr_ref, ks_ref, vs_ref, kw_ref, vw_ref):
    nq = N_HEADS * HEAD_W
    for c in range(N_HEADS):
        sl = slice(c * LANE, (c + 1) * LANE)
        q_ref[:, sl] = _rms(x_ref[:, sl], qg_ref[...])
    outs = (kcr_ref, vcr_ref, ks_ref, vs_ref, kw_ref, vw_ref)
    gains = (None, None, 1, None, 2, None)
    for a, (o_ref, gi) in enumerate(zip(outs, gains)):
        for c in range(N_KV):
            col = nq + a * NSA_KV + c * LANE
            blk = x_ref[:, col:col + LANE]
            o_ref[:, c * LANE:(c + 1) * LANE] = blk if gi is None else _rms(blk, kg_ref[gi:gi + 1, :])


def nsa_post(proj, q_norm, k_norm):
    m, n = proj.shape
    tm = min(m, 256)
    nq = N_HEADS * HEAD_W
    kv_spec = pl.BlockSpec((tm, NSA_KV), lambda i: (i, 0))
    kv_shape = jax.ShapeDtypeStruct((m, NSA_KV), F32)
    return pl.pallas_call(
        _nsa_post_kernel,
        grid=(m // tm,),
        in_specs=[
            pl.BlockSpec((tm, n), lambda i: (i, 0)),
            pl.BlockSpec((1, LANE), lambda i: (0, 0)),
            pl.BlockSpec((3, LANE), lambda i: (0, 0)),
        ],
        out_specs=[pl.BlockSpec((tm, nq), lambda i: (i, 0))] + [kv_spec] * 6,
        out_shape=[jax.ShapeDtypeStruct((m, nq), F32)] + [kv_shape] * 6,
        compiler_params=_params(("parallel",)),
        name="nsa_post",
    )(proj, q_norm.reshape(1, LANE), k_norm)


def _cmp_uv_kernel(pt_ref, *refs):
    del pt_ref
    pages = refs[:CMP_PAGES]
    wab_ref, o_ref = refs[CMP_PAGES:]
    subs = PAGE // CMP_STRIDE
    for g in range(N_KV):
        sl = slice(g * HEAD_W, (g + 1) * HEAD_W)
        acc = jnp.zeros((CMP_PAGES * subs, 2 * HEAD_W), F32)
        for lp in range(CMP_PAIRS):
            halves = []
            for li in range(2):
                rows = pl.ds(2 * lp + li, subs, stride=CMP_STRIDE)
                halves.append(jnp.concatenate([p[0, 0, rows, g, :] for p in pages], axis=0))
            acc = acc + _dot(jnp.concatenate(halves, axis=1).astype(BF16), wab_ref[lp])
        o_ref[0, :, g * 2 * HEAD_W:(g + 1) * 2 * HEAD_W] = acc


def cmp_uv(rows, layer, page_table, wab):
    b, n_pages = page_table.shape
    subs = PAGE // CMP_STRIDE

    def spec(u):
        return pl.BlockSpec((1, 1, PAGE, N_KV, HEAD_W),
                            lambda i, c, pt: (layer, pt[i, c * CMP_PAGES + u], 0, 0, 0))

    grid_spec = pltpu.PrefetchScalarGridSpec(
        num_scalar_prefetch=1,
        grid=(b, n_pages // CMP_PAGES),
        in_specs=[spec(u) for u in range(CMP_PAGES)]
        + [pl.BlockSpec(wab.shape, lambda i, c, pt: (0, 0, 0))],
        out_specs=pl.BlockSpec((1, CMP_PAGES * subs, 2 * NSA_KV), lambda i, c, pt: (i, c, 0)),
    )
    return pl.pallas_call(
        _cmp_uv_kernel,
        grid_spec=grid_spec,
        out_shape=jax.ShapeDtypeStruct((b, n_pages * subs, 2 * NSA_KV), F32),
        compiler_params=_params(("parallel", "arbitrary")),
        name="cmp_uv",
    )(page_table, *([rows] * CMP_PAGES), wab)


def _cmp_finish_kernel(uv_ref, pe_ref, w1_ref, w2_ref, kn_ref, o_ref, *, norm):
    n = uv_ref.shape[1]
    c = _dot(jnp.broadcast_to(pe_ref[...], (8, pe_ref.shape[1])).astype(BF16), w1_ref[...])[0:1]
    for g in range(N_KV):
        u = uv_ref[0, :, g * 2 * HEAD_W:g * 2 * HEAD_W + HEAD_W]
        v = uv_ref[0, :, g * 2 * HEAD_W + HEAD_W:(g + 1) * 2 * HEAD_W]
        pre = u + pltpu.roll(v, n - 1, axis=0) + c
        out = _dot((pre * jax.nn.sigmoid(pre)).astype(BF16), w2_ref[...])
        if norm:
            out = _rms(out, kn_ref[...])
        o_ref[0, :, g * HEAD_W:(g + 1) * HEAD_W] = out


def cmp_finish(uv, pe, w1, w2, k_norm):
    b, n, _ = uv.shape
    norm = k_norm is not None
    kn = (k_norm if norm else jnp.ones((HEAD_W,), F32)).reshape(1, HEAD_W)
    return pl.pallas_call(
        functools.partial(_cmp_finish_kernel, norm=norm),
        grid=(b,),
        in_specs=[
            pl.BlockSpec((1, n, 2 * NSA_KV), lambda i: (i, 0, 0)),
            pl.BlockSpec((1, pe.size), lambda i: (0, 0)),
            pl.BlockSpec(w1.shape, lambda i: (0, 0)),
            pl.BlockSpec(w2.shape, lambda i: (0, 0)),
            pl.BlockSpec((1, HEAD_W), lambda i: (0, 0)),
        ],
        out_specs=pl.BlockSpec((1, n, NSA_KV), lambda i: (i, 0, 0)),
        out_shape=jax.ShapeDtypeStruct((b, n, NSA_KV), F32),
        compiler_params=_params(("parallel",)),
        name="cmp_finish",
    )(uv, pe.reshape(1, -1), w1, w2, kn)


def compress(rows, layer, page_table, pe, w1, w2, k_norm):
    w1r = w1.reshape(2, CMP_STRIDE, HEAD_W, HEAD_W)
    wab = jnp.transpose(w1r, (1, 2, 0, 3)).reshape(CMP_PAIRS, 2 * HEAD_W, 2 * HEAD_W).astype(BF16)
    uv = cmp_uv(rows, layer, page_table, wab)
    return cmp_finish(uv, pe, w1.astype(BF16), w2.astype(BF16), k_norm)


def _importance_matrix(n_cmp, n_rows, n_slc, n_cols):
    w = np.zeros((n_rows, n_cols), np.float32)
    w_imp = [1.0] + [2.0] * (SLC_RATIO - 1) + [1.0]
    for s in range(n_slc):
        for m, wm in enumerate(w_imp):
            j = SLC_RATIO * s + m - 1
            if 0 <= j < n_cmp:
                w[j, s] += wm
    return w


def _select_blocks(s_slc, q_pos0, n_slc):
    t, w = s_slc.shape
    blk = lax.broadcasted_iota(jnp.int32, (t, w), 1)
    qpos = q_pos0 + lax.broadcasted_iota(jnp.int32, (t, w), 0)
    qb = qpos // SLC_BLOCK
    forced = (blk == 0) | (blk == qb) | (blk == qb - 1)
    score = jnp.where(forced, FORCE_SCORE, jnp.where(blk * SLC_BLOCK <= qpos, s_slc, -1.0))
    score = jnp.where(blk < n_slc, score, -2.0)
    cnt = jnp.zeros((t, w), F32)
    for sp in range(n_slc):
        col = score[:, sp:sp + 1]
        tie = jnp.where(blk > sp, 1.0, 0.0)
        cnt = cnt + jnp.where(col > score, 1.0, jnp.where(col == score, tie, 0.0))
    return jnp.where(cnt < N_SELECT, 1.0, 0.0)


def _masked_softmax(s, valid, axis=-1):
    m = jnp.max(s, axis=axis, keepdims=True)
    e = jnp.where(valid, jnp.exp(s - m), 0.0)
    return e / jnp.maximum(jnp.sum(e, axis=axis, keepdims=True), 1e-30)


def _reset_flash(m_ref, l_ref, acc_ref):
    m_ref[...] = jnp.full(m_ref.shape, NEG_INF, F32)
    l_ref[...] = jnp.zeros(l_ref.shape, F32)
    acc_ref[...] = jnp.zeros(acc_ref.shape, F32)


def _block_mask(sel, first_block, n_keys, reps):
    w = sel.shape[1]
    er = lax.broadcasted_iota(jnp.int32, (w, n_keys), 0)
    ec = lax.broadcasted_iota(jnp.int32, (w, n_keys), 1)
    e = jnp.where(er == first_block + ec // SLC_BLOCK, 1.0, 0.0).astype(BF16)
    mt = _dot(sel, e)
    return jnp.concatenate([mt] * reps, axis=0)


def _select_blocks_t(s_slc, q_pos0, n_slc):
    w, t = s_slc.shape
    n8 = -(-n_slc // 8) * 8
    blk = lax.broadcasted_iota(jnp.int32, (n8, t), 0)
    qpos = q_pos0 + lax.broadcasted_iota(jnp.int32, (n8, t), 1)
    qb = qpos // SLC_BLOCK
    forced = (blk == 0) | (blk == qb) | (blk == qb - 1)
    score = jnp.where(forced, FORCE_SCORE, jnp.where(blk * SLC_BLOCK <= qpos, s_slc[:n8], -1.0))
    score = jnp.where(blk < n_slc, score, -2.0)
    cnt = jnp.zeros((n8, t), F32)
    for sp in range(n_slc):
        row = score[sp:sp + 1, :]
        tie = jnp.where(blk > sp, 1.0, 0.0)
        cnt = cnt + jnp.where(row > score, 1.0, jnp.where(row == score, tie, 0.0))
    sel = jnp.where(cnt < N_SELECT, 1.0, 0.0)
    if n8 < w:
        sel = jnp.concatenate([sel, jnp.zeros((w - n8, t), F32)], axis=0)
    return sel


def _nsa_prompt_kernel(q_ref, gt_ref, kc_ref, vc_ref, ks_ref, vs_ref, kw_ref, vw_ref, bias_ref, u_ref, wimp_ref,
                       o_ref, qs_ref, ksb_ref, vst_ref, kwb_ref, vwt_ref, sel_ref, m_ref, l_ref, acc_ref,
                       *, tq, n_slc):
    qi = pl.program_id(2)

    @pl.when(qi == 0)
    def _():
        _stage_kv(ks_ref, vs_ref, ksb_ref, vst_ref, tq)
        _stage_kv(kw_ref, vw_ref, kwb_ref, vwt_ref, tq)

    qs_ref[...] = jnp.concatenate(
        [q_ref[0, :, r * HEAD_W:(r + 1) * HEAD_W] for r in range(HPG)], axis=0).astype(BF16)

    n_pad = kc_ref.shape[1]
    start = pl.multiple_of(n_pad - (tq // CMP_STRIDE) * (qi + 1), 8)
    bias_c = u_ref[0, pl.ds(start, n_pad), :]
    s = _dot_t(kc_ref[0].astype(BF16), qs_ref[...]) * NSA_SCALE + bias_c
    pc = _masked_softmax(s, bias_c > 0.5 * NEG_INF, axis=0)
    o_cmp = _dot(vc_ref[0].T.astype(BF16), pc.astype(BF16))
    psum = pc[:, 0:tq] + pc[:, tq:2 * tq] + pc[:, 2 * tq:3 * tq] + pc[:, 3 * tq:4 * tq]
    hi, mid, lo = _split3(psum)
    wimp = wimp_ref[...]
    s_slc = _dot(wimp, hi) + _dot(wimp, mid) + _dot(wimp, lo)
    sel_ref[...] = _select_blocks_t(s_slc, qi * tq, n_slc).astype(BF16)

    _reset_flash(m_ref, l_ref, acc_ref)

    def slc_body(kt, c):
        rows = pl.ds(pl.multiple_of(kt * tq, tq), tq)
        key = lax.broadcasted_iota(jnp.int32, (tq, LANE), 0)
        blk = lax.broadcasted_iota(jnp.int32, (tq, LANE), 1)
        onehot = jnp.where(blk == kt * (tq // SLC_BLOCK) + key // SLC_BLOCK, 1.0, 0.0).astype(BF16)
        mt = _dot(onehot, sel_ref[...])
        mask = jnp.concatenate([mt] * HPG, axis=1)
        sc = _dot_t(ksb_ref[rows, :], qs_ref[...]) * NSA_SCALE + bias_ref[jnp.minimum(qi - kt, 2)]
        sc = jnp.where(mask > 0.5, sc, NEG_INF)
        _flash_step_t(sc, vst_ref[:, rows], m_ref, l_ref, acc_ref)
        return c

    lax.fori_loop(0, qi + 1, slc_body, 0)
    o_slc = acc_ref[...] / l_ref[...]

    _reset_flash(m_ref, l_ref, acc_ref)
    nw = WINDOW // tq

    def win_body(kt, c):
        rows = pl.ds(pl.multiple_of(kt * tq, tq), tq)
        t = qi - kt
        typ = jnp.where(t == nw, 3, jnp.minimum(t, 2))
        sc = _dot_t(kwb_ref[rows, :], qs_ref[...]) * NSA_SCALE + bias_ref[typ]
        _flash_step_t(sc, vwt_ref[:, rows], m_ref, l_ref, acc_ref)
        return c

    lax.fori_loop(jnp.maximum(qi - nw, 0), qi + 1, win_body, 0)
    o_win = acc_ref[...] / l_ref[...]

    sig = jax.nn.sigmoid(gt_ref[0]).T
    for r in range(HPG):
        cs = slice(r * tq, (r + 1) * tq)
        o_t = (sig[r:r + 1, :] * o_cmp[:, cs] + sig[HPG + r:HPG + r + 1, :] * o_slc[:, cs]
               + sig[2 * HPG + r:2 * HPG + r + 1, :] * o_win[:, cs])
        o_ref[0, :, r * HEAD_W:(r + 1) * HEAD_W] = o_t.T


def _prompt_cmp_buckets(tq, n_pad):
    jp = np.arange(2 * n_pad)[:, None] - (n_pad - tq // CMP_STRIDE)
    i = np.arange(tq)[None, :]
    return _bucket_np(i - CMP_STRIDE * jp - (2 * CMP_STRIDE - 1))[None].astype(np.int32)


def nsa_prompt(q, gates, kc, vc, ks, vs, kw, vw, bias, u_bias, tq=PROMPT_TQ):
    b, t, _ = q.shape
    gw = HPG * HEAD_W
    n_pad = kc.shape[1]
    n_slc = t // SLC_BLOCK
    wimp = jnp.asarray(_importance_matrix(n_pad - 1, n_pad, n_slc, LANE).T, BF16)
    rows = HPG * tq
    seq = lambda n: pl.BlockSpec((1, n, HEAD_W), lambda i, g, j: (i, 0, g))
    return pl.pallas_call(
        functools.partial(_nsa_prompt_kernel, tq=tq, n_slc=n_slc),
        grid=(b, N_KV, t // tq),
        in_specs=[
            pl.BlockSpec((1, tq, gw), lambda i, g, j: (i, j, g)),
            pl.BlockSpec((1, tq, LANE), lambda i, g, j: (i, j, g)),
            seq(n_pad), seq(n_pad), seq(t), seq(t), seq(t), seq(t),
            pl.BlockSpec((4, tq, rows), lambda i, g, j: (0, 0, g)),
            pl.BlockSpec((1, 2 * n_pad, rows), lambda i, g, j: (0, 0, g)),
            pl.BlockSpec(wimp.shape, lambda i, g, j: (0, 0)),
        ],
        out_specs=pl.BlockSpec((1, tq, gw), lambda i, g, j: (i, j, g)),
        out_shape=jax.ShapeDtypeStruct(q.shape, F32),
        scratch_shapes=[
            pltpu.VMEM((rows, HEAD_W), BF16),
            pltpu.VMEM((t, HEAD_W), BF16),
            pltpu.VMEM((HEAD_W, t), BF16),
            pltpu.VMEM((t, HEAD_W), BF16),
            pltpu.VMEM((HEAD_W, t), BF16),
            pltpu.VMEM((LANE, tq), BF16),
            pltpu.VMEM((1, rows), F32),
            pltpu.VMEM((1, rows), F32),
            pltpu.VMEM((HEAD_W, rows), F32),
        ],
        compiler_params=_params(("parallel", "parallel", "arbitrary")),
        name="nsa_prompt",
    )(q, gates, kc, vc, ks, vs, kw, vw, bias, u_bias, wimp)


def _decode_cmp_buckets(t_new, past_len):
    n_pad = past_len // CMP_STRIDE
    i = np.arange(t_new)[:, None]
    j = n_pad - LANE + np.arange(LANE)[None, :]
    last = _bucket_np(past_len + i - CMP_STRIDE * j - (2 * CMP_STRIDE - 1))
    last = np.where(j < n_pad - 1, last, -1)
    return np.stack([np.full((t_new, LANE), N_BUCKETS - 1, np.int32), last]).astype(np.int32)


def _decode_win_buckets(t_new, n_tiles):
    i = np.arange(t_new)[:, None]
    idx = np.arange(n_tiles * LANE)[None, :]
    dw = WINDOW + i - idx
    ok = (dw >= 0) & (dw < WINDOW) & (idx < WINDOW + t_new)
    b = np.where(ok, _bucket_np(dw), -1)
    return np.stack([b[:, k * LANE:(k + 1) * LANE] for k in range(n_tiles)]).astype(np.int32)


def _nsa_decode_kernel(pt_ref, q_ref, gt_ref, kc_ref, vc_ref, kn_ref, vn_ref, kw_ref, vw_ref, *rest,
                       n_pages, past_len):
    del pt_ref
    kp = rest[:PAGES_PER_STEP]
    vp = rest[PAGES_PER_STEP:2 * PAGES_PER_STEP]
    (bias_ref, bias_c_ref, bias_w_ref, wimp_ref, o_ref,
     qx_ref, sel_ref, ocmp_ref, owin_ref, m_ref, l_ref, acc_ref) = rest[2 * PAGES_PER_STEP:]
    s = pl.program_id(1)
    t_new = q_ref.shape[1]
    rows = HPG * t_new
    n_slc = -(-(past_len + t_new) // SLC_BLOCK)

    @pl.when(s == 0)
    def _():
        _reset_flash(m_ref, l_ref, acc_ref)
        n_pad = kc_ref.shape[1]
        for g in range(N_KV):
            sl = slice(g * HEAD_W, (g + 1) * HEAD_W)
            gr = slice(g * rows, (g + 1) * rows)
            qx_ref[g] = _stack_group_heads(q_ref, g).astype(BF16)
            qx = qx_ref[g]
            bias_c = jnp.concatenate([bias_c_ref[0, gr, :]] * (n_pad // LANE - 1) + [bias_c_ref[1, gr, :]], axis=1)
            sc = _dot_t(qx, kc_ref[0, :, sl].astype(BF16)) * NSA_SCALE + bias_c
            pc = _masked_softmax(sc, bias_c > 0.5 * NEG_INF)
            ocmp_ref[g] = _dot(pc.astype(BF16), vc_ref[0, :, sl].astype(BF16))
            psum = sum(pc[r * t_new:(r + 1) * t_new] for r in range(1, HPG)) + pc[0:t_new]
            s_slc = _dot_exact_rhs(psum, wimp_ref[...])
            sel_ref[g] = _select_blocks(s_slc, past_len, n_slc)
            n_wt = bias_w_ref.shape[0]
            bias_w = jnp.concatenate([bias_w_ref[k, gr, :] for k in range(n_wt)], axis=1)
            sw = _dot_t(qx, kw_ref[0, :, sl].astype(BF16)) * NSA_SCALE + bias_w
            pw = _masked_softmax(sw, bias_w > 0.5 * NEG_INF)
            owin_ref[g] = _dot(pw.astype(BF16), vw_ref[0, :, sl].astype(BF16))
            sn = _dot_t(qx, kn_ref[0, :, sl].astype(BF16)) * NSA_SCALE + bias_ref[2, gr, :]
            _online_update(sn, vn_ref[0, :, sl].astype(BF16), m_ref.at[g], l_ref.at[g], acc_ref.at[g])

    for g in range(N_KV):
        sl = slice(g * HEAD_W, (g + 1) * HEAD_W)
        kc = jnp.concatenate([r[0, 0, :, g, :] for r in kp], axis=0).astype(BF16)
        vc = jnp.concatenate([r[0, 0, :, g, :] for r in vp], axis=0).astype(BF16)
        b = _page_bias(bias_ref, g, rows, s * PAGES_PER_STEP, n_pages)
        first_block = s * (PAGES_PER_STEP * PAGE // SLC_BLOCK)
        mask = _block_mask(sel_ref[g].astype(BF16), first_block, PAGES_PER_STEP * PAGE, HPG)
        sc = _dot_t(qx_ref[g], kc) * NSA_SCALE + b
        sc = jnp.where(mask > 0.5, sc, NEG_INF)
        _online_update(sc, vc, m_ref.at[g], l_ref.at[g], acc_ref.at[g])

    @pl.when(s == pl.num_programs(1) - 1)
    def _():
        sig = jax.nn.sigmoid(gt_ref[0])
        for g in range(N_KV):
            o_slc = acc_ref[g] / l_ref[g]
            o_cmp = ocmp_ref[g]
            o_win = owin_ref[g]
            for r in range(HPG):
                h = g * HPG + r
                rs = slice(r * t_new, (r + 1) * t_new)
                c0 = g * LANE + r
                o_ref[0, :, h * HEAD_W:(h + 1) * HEAD_W] = (
                    sig[:, c0:c0 + 1] * o_cmp[rs] + sig[:, c0 + HPG:c0 + HPG + 1] * o_slc[rs]
                    + sig[:, c0 + 2 * HPG:c0 + 2 * HPG + 1] * o_win[rs])


def nsa_decode(q, gates, kc, vc, k_new, v_new, kw_src, vw_src, cache_k, cache_v, layer, page_table,
               bias, bias_c, bias_w, past_len):
    b, t_new, _ = q.shape
    n_pages = page_table.shape[1]
    rows = HPG * t_new
    n_pad = kc.shape[1]
    n_slc = -(-(past_len + t_new) // SLC_BLOCK)
    n_cols = -(-n_slc // LANE) * LANE
    wimp = jnp.asarray(_importance_matrix(n_pad - 1, n_pad, n_slc, n_cols), BF16)
    fixed = lambda *shape: pl.BlockSpec(shape, lambda i, s, pt: (0,) * len(shape))
    per_b = lambda *shape: pl.BlockSpec((1,) + shape, lambda i, s, pt: (i,) + (0,) * len(shape))
    grid_spec = pltpu.PrefetchScalarGridSpec(
        num_scalar_prefetch=1,
        grid=(b, n_pages // PAGES_PER_STEP),
        in_specs=[per_b(t_new, N_HEADS * HEAD_W), per_b(t_new, N_KV * LANE),
                  per_b(n_pad, NSA_KV), per_b(n_pad, NSA_KV), per_b(PAGE, NSA_KV), per_b(PAGE, NSA_KV),
                  per_b(kw_src.shape[1], NSA_KV), per_b(kw_src.shape[1], NSA_KV)]
        + _page_specs(layer, 2)
        + [fixed(*bias.shape), fixed(*bias_c.shape), fixed(*bias_w.shape), fixed(*wimp.shape)],
        out_specs=per_b(t_new, N_HEADS * HEAD_W),
        scratch_shapes=[
            pltpu.VMEM((N_KV, rows, HEAD_W), BF16),
            pltpu.VMEM((N_KV, t_new, n_cols), F32),
            pltpu.VMEM((N_KV, rows, HEAD_W), F32),
            pltpu.VMEM((N_KV, rows, HEAD_W), F32),
            pltpu.VMEM((N_KV, rows, 1), F32),
            pltpu.VMEM((N_KV, rows, 1), F32),
            pltpu.VMEM((N_KV, rows, HEAD_W), F32),
        ],
    )
    return pl.pallas_call(
        functools.partial(_nsa_decode_kernel, n_pages=n_pages, past_len=past_len),
        grid_spec=grid_spec,
        out_shape=jax.ShapeDtypeStruct(q.shape, F32),
        compiler_params=_params(("parallel", "arbitrary")),
        name="nsa_decode",
    )(page_table, q, gates, kc, vc, k_new, v_new, kw_src, vw_src,
      *([cache_k] * PAGES_PER_STEP), *([cache_v] * PAGES_PER_STEP), bias, bias_c, bias_w, wimp)


def _gate_weights(w_g):
    d = w_g.shape[0]
    w = jnp.transpose(w_g.reshape(d, 3, N_KV, HPG), (0, 2, 1, 3)).reshape(d, N_KV, 3 * HPG)
    return jnp.pad(w, ((0, 0), (0, 0), (0, LANE - 3 * HPG))).reshape(d, N_KV * LANE)


def nsa_layer(xp, xs, gain, w_in, q_norm, k_norm, pe, w1, w2, w_out, caches, layer, page_table,
              bias_p, u_bias, bias_s, bias_sc, bias_sw, past_len):
    cmp_k, cmp_v, slc_k, slc_v, win_k, win_v = caches
    nqkv = N_HEADS * HEAD_W + 6 * NSA_KV
    w_g = _gate_weights(w_in[layer, :, nqkv:])[None]
    outs = []
    for x, paged in ((xp, False), (xs, True)):
        b, t, d = x.shape
        x2 = x.reshape(b * t, d)
        proj = dense(x2, w_in, layer, gain=gain, n=nqkv)
        gates = dense(x2, w_g, 0, gain=gain).reshape(b, t, -1)
        q, kcr, vcr, ks, vs, kw, vw = (a.reshape(b, t, -1) for a in nsa_post(proj, q_norm, k_norm))
        if paged:
            kc = compress(cmp_k, layer, page_table, pe[0], w1[0], w2[0], k_norm[0])
            vc = compress(cmp_v, layer, page_table, pe[1], w1[1], w2[1], None)
            n_wt = bias_sw.shape[0]
            kw_src = jnp.concatenate([win_k[layer], kw], axis=1)
            vw_src = jnp.concatenate([win_v[layer], vw], axis=1)
            o = nsa_decode(q, gates, kc, vc, _pad_rows(ks, PAGE), _pad_rows(vs, PAGE),
                           _pad_rows(kw_src, n_wt * LANE), _pad_rows(vw_src, n_wt * LANE),
                           slc_k, slc_v, layer, page_table, bias_s, bias_sc, bias_sw, past_len)
            kw_out, vw_out = kw_src[:, -WINDOW:], vw_src[:, -WINDOW:]
        else:
            n_pg = t // PAGE
            ident = jnp.arange(b * n_pg, dtype=jnp.int32).reshape(b, n_pg)
            as_pages = lambda a: a.reshape(1, b * n_pg, PAGE, N_KV, HEAD_W)
            kc = compress(as_pages(kcr), 0, ident, pe[0], w1[0], w2[0], k_norm[0])
            vc = compress(as_pages(vcr), 0, ident, pe[1], w1[1], w2[1], None)
            o = nsa_prompt(q, gates, kc, vc, ks, vs, kw, vw, bias_p, u_bias)
            kw_out, vw_out = kw[:, -WINDOW:], vw[:, -WINDOW:]
        y = dense(o.reshape(b * t, -1), w_out, layer, res=x2).reshape(b, t, d)
        st = tuple(a.reshape(b, -1, N_KV, HEAD_W) for a in (kcr, vcr, ks, vs, kw_out, vw_out))
        outs.append((y, st))
    (yp, stp), (ys, sts) = outs
    return yp, ys, stp, sts


def kernel(x_prompt, x_sample, cache_diff_k, cache_diff_v, state_ssm, state_conv, cache_nsa_cmp_k, cache_nsa_cmp_v, cache_nsa_slc_k, cache_nsa_slc_v, cache_nsa_win_k, cache_nsa_win_v, cache_mem_k, cache_mem_v, page_table, mem_prompt, rel_bias_table, norm_mix, norm_xattn, norm_mem, norm_ffn, diff_w_in, diff_q_norm, diff_k_norm, diff_lambda, diff_sub_norm, diff_w_out, ssm_w_in, ssm_conv_w, ssm_conv_b, ssm_dt_bias, ssm_a_log, ssm_d, ssm_norm, ssm_w_out, nsa_w_in, nsa_q_norm, nsa_k_norm, nsa_cmp_pe, nsa_cmp_w1, nsa_cmp_w2, nsa_w_out, xattn_w_q, xattn_w_k, xattn_w_v, xattn_q_norm, xattn_k_norm, xattn_w_o, ffn_w1, ffn_w3, ffn_w2):
    xp, xs = x_prompt, x_sample
    bp, t, d = xp.shape
    bs, t_new, _ = xs.shape
    past_len = page_table.shape[1] * PAGE
    depth = norm_mix.shape[0]

    bias_p = bias_tiles(_prompt_attn_buckets(PROMPT_TQ), rel_bias_table, heads_on_lanes=True)
    u_bias = bias_tiles(_prompt_cmp_buckets(PROMPT_TQ, t // CMP_STRIDE), rel_bias_table, heads_on_lanes=True)
    bias_s = bias_tiles(_decode_buckets(t_new, past_len), rel_bias_table)
    bias_sc = bias_tiles(_decode_cmp_buckets(t_new, past_len), rel_bias_table)
    n_wt = -(-(WINDOW + t_new) // LANE)
    bias_sw = bias_tiles(_decode_win_buckets(t_new, n_wt), rel_bias_table)

    mem_k, mem_v = mem_kv(mem_prompt, norm_mem, xattn_w_k, xattn_w_v, xattn_k_norm)
    xw = X_HEADS * X_DH
    win_k = cache_nsa_win_k.reshape(*cache_nsa_win_k.shape[:3], NSA_KV)
    win_v = cache_nsa_win_v.reshape(*cache_nsa_win_v.shape[:3], NSA_KV)

    dkp, dvp, dks, dvs = [], [], [], []
    ssp, cvp, sss, cvs = [], [], [], []
    nsp, nss = [], []
    for i in range(depth):
        kind, j = i % 3, i // 3
        if kind == 0:
            lam_init = 0.8 - 0.6 * math.exp(-0.3 * i)
            xp, xs, kp_, vp_, ks_, vs_ = diff_layer(
                xp, xs, norm_mix[i], diff_w_in, diff_q_norm[j], diff_k_norm[j], diff_lambda[j],
                diff_sub_norm[j], diff_w_out, cache_diff_k, cache_diff_v, j, page_table,
                bias_p, bias_s, lam_init)
            dkp.append(kp_)
            dvp.append(vp_)
            dks.append(ks_)
            dvs.append(vs_)
        elif kind == 1:
            xp, xs, hp_, cp_, hs_, cs_ = ssd_layer(
                xp, xs, norm_mix[i], ssm_w_in, j, ssm_conv_w[j], ssm_conv_b[j], ssm_dt_bias[j], ssm_a_log[j],
                ssm_d[j], ssm_norm[j], ssm_w_out, state_conv[j], state_ssm[j])
            ssp.append(hp_)
            cvp.append(cp_)
            sss.append(hs_)
            cvs.append(cs_)
        else:
            caches = (cache_nsa_cmp_k, cache_nsa_cmp_v, cache_nsa_slc_k, cache_nsa_slc_v, win_k, win_v)
            xp, xs, stp, sts = nsa_layer(
                xp, xs, norm_mix[i], nsa_w_in, nsa_q_norm[j], nsa_k_norm[j], nsa_cmp_pe[j], nsa_cmp_w1[j],
                nsa_cmp_w2[j], nsa_w_out, caches, j, page_table,
                bias_p, u_bias, bias_s, bias_sc, bias_sw, past_len)
            nsp.append(stp)
            nss.append(sts)
        xp = xattn(xp, norm_xattn[i], xattn_w_q, xattn_q_norm[i], mem_k[i], mem_v[i], xattn_w_o, i)
        xs = xattn(xs, norm_xattn[i], xattn_w_q, xattn_q_norm[i], cache_mem_k[i].reshape(bs, N_MEM, xw),
                   cache_mem_v[i].reshape(bs, N_MEM, xw), xattn_w_o, i)
        xp = ffn(xp.reshape(bp * t, d), norm_ffn[i], ffn_w1, ffn_w3, ffn_w2, i).reshape(bp, t, d)
        xs = ffn(xs.reshape(bs * t_new, d), norm_ffn[i], ffn_w1, ffn_w3, ffn_w2, i).reshape(bs, t_new, d)

    st = lambda xs_: jnp.stack(xs_, axis=0)
    nsp_t = [st([s[a] for s in nsp]) for a in range(6)]
    nss_t = [st([s[a] for s in nss]) for a in range(6)]
    mem_shape = (depth, bp, N_MEM, X_HEADS, X_DH)
    return (xp, xs,
            st(dkp), st(dvp), st(dks), st(dvs),
            st(ssp), st(cvp), st(sss), st(cvs),
            *nsp_t, *nss_t,
            mem_k.reshape(mem_shape), mem_v.reshape(mem_shape))
```

```python
import functools
import math

import numpy as np
import jax
import jax.numpy as jnp
from jax import lax
from jax.experimental import pallas as pl
from jax.experimental.pallas import tpu as pltpu

F32 = jnp.float32
BF16 = jnp.bfloat16

D_MODEL = 2048
DEPTH = 4
PAGE = 128
N_HEADS = 16
N_KV = 4
HPG = N_HEADS // N_KV
DIFF_DH = 64
HEAD_W = 128
N_BUCKETS = 32
MAX_EXACT = 16
MAX_DIST = 128
D_INNER = 2 * D_MODEL
SSM_HEADDIM = 64
SSM_HEADS = D_INNER // SSM_HEADDIM
SSM_GROUPS = 8
SSM_HPG = SSM_HEADS // SSM_GROUPS
D_STATE = 128
CONV_W = 4
CONV_DIM = D_INNER + 2 * SSM_GROUPS * D_STATE
SSM_CHUNK = 128
CMP_STRIDE = 16
SLC_BLOCK = 64
SLC_RATIO = SLC_BLOCK // CMP_STRIDE
N_SELECT = 16
WINDOW = 512
N_MEM = 256
X_HEADS = 4
X_DH = 128
EPS = 1e-6
NEG_INF = -1e30
FORCE_SCORE = 1e4

LANE = 128
VMEM_LIMIT = 56 * 1024 * 1024


def _params(sem):
    return pltpu.CompilerParams(dimension_semantics=sem, vmem_limit_bytes=VMEM_LIMIT)


def _rms(x, gain):
    return x * lax.rsqrt(jnp.mean(x * x, axis=-1, keepdims=True) + EPS) * gain


def _dot(a, b):
    return jnp.dot(a, b, preferred_element_type=F32)


def _dot_t(a, b):
    return lax.dot_general(a, b, (((1,), (1,)), ((), ())), preferred_element_type=F32)


def _split3(x):
    hi = x.astype(BF16)
    r1 = x - hi.astype(F32)
    mid = r1.astype(BF16)
    lo = (r1 - mid.astype(F32)).astype(BF16)
    return hi, mid, lo


def _dot_exact_rhs(x, m_bf16):
    hi, mid, lo = _split3(x)
    return _dot(hi, m_bf16) + _dot(mid, m_bf16) + _dot(lo, m_bf16)


def _dense_kernel(*refs, norm, residual):
    refs = list(refs)
    x_ref = refs.pop(0)
    g_ref = refs.pop(0) if norm else None
    w_ref = refs.pop(0)
    r_ref = refs.pop(0) if residual else None
    o_ref, xb_ref = refs

    @pl.when(pl.program_id(1) == 0)
    def _():
        x = x_ref[...]
        if norm:
            x = _rms(x, g_ref[...])
        xb_ref[...] = x.astype(BF16)

    y = _dot(xb_ref[...], w_ref[0].astype(BF16))
    if residual:
        y = y + r_ref[...]
    o_ref[...] = y


def dense(x, w, layer, gain=None, res=None, n=None):
    m, k = x.shape
    n = w.shape[2] if n is None else n
    tm = m if m <= 512 else (1024 if k <= D_MODEL else 512)
    tn = LANE if n % 256 else (512 if k <= D_MODEL and n % 512 == 0 else 256)
    assert m % tm == 0 and n % tn == 0
    norm, residual = gain is not None, res is not None
    in_specs = [pl.BlockSpec((tm, k), lambda i, j: (i, 0))]
    args = [x]
    if norm:
        in_specs.append(pl.BlockSpec((1, k), lambda i, j: (0, 0)))
        args.append(gain.reshape(1, k))
    in_specs.append(pl.BlockSpec((1, k, tn), lambda i, j: (layer, 0, j)))
    args.append(w)
    if residual:
        in_specs.append(pl.BlockSpec((tm, tn), lambda i, j: (i, j)))
        args.append(res)
    return pl.pallas_call(
        functools.partial(_dense_kernel, norm=norm, residual=residual),
        grid=(m // tm, n // tn),
        in_specs=in_specs,
        out_specs=pl.BlockSpec((tm, tn), lambda i, j: (i, j)),
        out_shape=jax.ShapeDtypeStruct((m, n), F32),
        scratch_shapes=[pltpu.VMEM((tm, k), BF16)],
        compiler_params=_params(("parallel", "arbitrary")),
        name="dense",
    )(*args)


def _ffn_kernel(x_ref, g_ref, w1_ref, w3_ref, w2_ref, o_ref, xb_ref):
    @pl.when(pl.program_id(1) == 0)
    def _():
        x = x_ref[...]
        xb_ref[...] = _rms(x, g_ref[...]).astype(BF16)
        o_ref[...] = x

    xb = xb_ref[...]
    h1 = _dot(xb, w1_ref[0].astype(BF16))
    h3 = _dot(xb, w3_ref[0].astype(BF16))
    a = (h1 * jax.nn.sigmoid(h1) * h3).astype(BF16)
    o_ref[...] += _dot(a, w2_ref[0].astype(BF16))


def ffn(x, gain, w1, w3, w2, layer):
    m, d = x.shape
    f = w1.shape[2]
    tm = min(m, 1024)
    tf = 256
    assert m % tm == 0 and f % tf == 0
    return pl.pallas_call(
        _ffn_kernel,
        grid=(m // tm, f // tf),
        in_specs=[
            pl.BlockSpec((tm, d), lambda i, j: (i, 0), pipeline_mode=pl.Buffered(1)),
            pl.BlockSpec((1, d), lambda i, j: (0, 0)),
            pl.BlockSpec((1, d, tf), lambda i, j: (layer, 0, j)),
            pl.BlockSpec((1, d, tf), lambda i, j: (layer, 0, j)),
            pl.BlockSpec((1, tf, d), lambda i, j: (layer, j, 0)),
        ],
        out_specs=pl.BlockSpec((tm, d), lambda i, j: (i, 0)),
        out_shape=jax.ShapeDtypeStruct((m, d), F32),
        scratch_shapes=[pltpu.VMEM((tm, d), BF16)],
        compiler_params=_params(("parallel", "arbitrary")),
        name="ffn",
    )(x, gain.reshape(1, d), w1, w3, w2)


def _mem_kv_kernel(mem_ref, g_ref, wk_ref, wv_ref, kn_ref, k_ref, v_ref):
    m = _rms(mem_ref[0], g_ref[0]).astype(BF16)
    k = _dot(m, wk_ref[0].astype(BF16))
    v_ref[0, 0] = _dot(m, wv_ref[0].astype(BF16))
    for h in range(X_HEADS):
        sl = slice(h * X_DH, (h + 1) * X_DH)
        k_ref[0, 0, :, sl] = _rms(k[:, sl], kn_ref[0])


def mem_kv(mem, g_mem, wk, wv, k_norm):
    b = mem.shape[0]
    nl = wk.shape[0]
    hw = X_HEADS * X_DH
    shape = jax.ShapeDtypeStruct((nl, b, N_MEM, hw), F32)
    return pl.pallas_call(
        _mem_kv_kernel,
        grid=(nl, b),
        in_specs=[
            pl.BlockSpec((1, N_MEM, D_MODEL), lambda l, i: (i, 0, 0)),
            pl.BlockSpec((1, 1, D_MODEL), lambda l, i: (l, 0, 0)),
            pl.BlockSpec((1, D_MODEL, hw), lambda l, i: (l, 0, 0)),
            pl.BlockSpec((1, D_MODEL, hw), lambda l, i: (l, 0, 0)),
            pl.BlockSpec((1, 1, X_DH), lambda l, i: (l, 0, 0)),
        ],
        out_specs=[pl.BlockSpec((1, 1, N_MEM, hw), lambda l, i: (l, i, 0, 0))] * 2,
        out_shape=[shape, shape],
        compiler_params=_params(("parallel", "parallel")),
        name="mem_kv",
    )(mem, g_mem.reshape(nl, 1, D_MODEL), wk, wv, k_norm.reshape(nl, 1, X_DH))


def _xattn_kernel(x_ref, g_ref, wq_ref, qn_ref, k_ref, v_ref, wo_ref, o_ref):
    x = x_ref[0]
    q = _dot(_rms(x, g_ref[...]).astype(BF16), wq_ref[0].astype(BF16))
    outs = []
    for h in range(X_HEADS):
        sl = slice(h * X_DH, (h + 1) * X_DH)
        qh = _rms(q[:, sl], qn_ref[...]).astype(BF16)
        s = _dot_t(qh, k_ref[0, :, sl].astype(BF16)) * (X_DH ** -0.5)
        e = jnp.exp(s - jnp.max(s, axis=-1, keepdims=True))
        p = e / jnp.sum(e, axis=-1, keepdims=True)
        outs.append(_dot(p.astype(BF16), v_ref[0, :, sl].astype(BF16)))
    o = jnp.concatenate(outs, axis=1).astype(BF16)
    o_ref[0] = x + _dot(o, wo_ref[0].astype(BF16))


def xattn(x, gain, wq, q_norm, k, v, wo, layer):
    b, t, d = x.shape
    hw = X_HEADS * X_DH
    tm = min(t, 512)
    return pl.pallas_call(
        _xattn_kernel,
        grid=(b, t // tm),
        in_specs=[
            pl.BlockSpec((1, tm, d), lambda i, j: (i, j, 0)),
            pl.BlockSpec((1, d), lambda i, j: (0, 0)),
            pl.BlockSpec((1, d, hw), lambda i, j: (layer, 0, 0)),
            pl.BlockSpec((1, X_DH), lambda i, j: (0, 0)),
            pl.BlockSpec((1, N_MEM, hw), lambda i, j: (i, 0, 0)),
            pl.BlockSpec((1, N_MEM, hw), lambda i, j: (i, 0, 0)),
            pl.BlockSpec((1, hw, d), lambda i, j: (layer, 0, 0)),
        ],
        out_specs=pl.BlockSpec((1, tm, d), lambda i, j: (i, j, 0)),
        out_shape=jax.ShapeDtypeStruct((b, t, d), F32),
        compiler_params=_params(("parallel", "parallel")),
        name="xattn",
    )(x, gain.reshape(1, d), wq, q_norm.reshape(1, X_DH), k, v, wo)


def _bucket_np(dist):
    n = np.maximum(dist, 0)
    nf = np.maximum(n, 1).astype(np.float64)
    large = MAX_EXACT + (np.log(nf / MAX_EXACT) / math.log(MAX_DIST / MAX_EXACT)
                         * (N_BUCKETS - MAX_EXACT)).astype(np.int64)
    b = np.where(n < MAX_EXACT, n, np.minimum(large, N_BUCKETS - 1))
    return np.where(dist < 0, -1, b).astype(np.int32)


def _bias_kernel(bkt_ref, tab_ref, o_ref, *, scale):
    h = pl.program_id(1)
    b = bkt_ref[0]
    acc = jnp.full(b.shape, NEG_INF, F32)
    for k in range(N_BUCKETS):
        acc = jnp.where(b == k, tab_ref[k, h] * scale, acc)
    o_ref[0] = acc


def _bias_packed_kernel(idx_ref, tab_ref, o_ref, *, scale):
    idx = idx_ref[0]
    acc = jnp.full(idx.shape, NEG_INF, F32)
    for k in range(N_BUCKETS):
        for h in range(N_HEADS):
            acc = jnp.where(idx == k * N_HEADS + h, tab_ref[k, h] * scale, acc)
    o_ref[0] = acc


def bias_tiles_indexed(idx, table, scale=1.0):
    nt, r, w = idx.shape
    return pl.pallas_call(
        functools.partial(_bias_packed_kernel, scale=scale),
        grid=(nt,),
        in_specs=[
            pl.BlockSpec((1, r, w), lambda t: (t, 0, 0)),
            pl.BlockSpec(memory_space=pltpu.SMEM),
        ],
        out_specs=pl.BlockSpec((1, r, w), lambda t: (t, 0, 0)),
        out_shape=jax.ShapeDtypeStruct((nt, r, w), F32),
        compiler_params=_params(("parallel",)),
        name="bias_tiles_indexed",
    )(jnp.asarray(idx.astype(np.int32)), table)


def bias_tiles(buckets, table, heads_on_lanes=False, scale=1.0):
    nt, r, w = buckets.shape
    if heads_on_lanes:
        out_spec = pl.BlockSpec((1, r, w), lambda t, h: (t, 0, h))
        out_shape = (nt, r, N_HEADS * w)
    else:
        out_spec = pl.BlockSpec((1, r, w), lambda t, h: (t, h, 0))
        out_shape = (nt, N_HEADS * r, w)
    return pl.pallas_call(
        functools.partial(_bias_kernel, scale=scale),
        grid=(nt, N_HEADS),
        in_specs=[
            pl.BlockSpec((1, r, w), lambda t, h: (t, 0, 0)),
            pl.BlockSpec(memory_space=pltpu.SMEM),
        ],
        out_specs=out_spec,
        out_shape=jax.ShapeDtypeStruct(out_shape, F32),
        compiler_params=_params(("parallel", "parallel")),
        name="bias_tiles",
    )(jnp.asarray(buckets), table)


def _prompt_attn_buckets(tq):
    assert WINDOW % tq == 0 and tq >= MAX_DIST
    j = np.arange(tq)[:, None]
    i = np.arange(tq)[None, :]
    far = np.full((tq, tq), N_BUCKETS - 1, np.int32)
    return np.stack([
        _bucket_np(i - j),
        _bucket_np(i - j + tq),
        far,
        np.where(j > i, far, -1),
    ]).astype(np.int32)


def _norm64(blk, gain, lo):
    sq = blk * blk
    s_lo = jnp.sum(jnp.where(lo, sq, 0.0), axis=-1, keepdims=True)
    s_hi = jnp.sum(jnp.where(lo, 0.0, sq), axis=-1, keepdims=True)
    ms = jnp.where(lo, s_lo, s_hi) * (1.0 / DIFF_DH)
    return blk * lax.rsqrt(ms + EPS) * gain


def _diff_post_kernel(x_ref, qg_ref, kg_ref, q_ref, k_ref, v_ref):
    tm = x_ref.shape[0]
    lo = lax.broadcasted_iota(jnp.int32, (tm, LANE), 1) < DIFF_DH
    nq = N_HEADS * HEAD_W
    nk = N_KV * HEAD_W
    for c in range(N_HEADS):
        sl = slice(c * LANE, (c + 1) * LANE)
        q_ref[:, sl] = _norm64(x_ref[:, sl], qg_ref[...], lo)
    for c in range(N_KV):
        sl = slice(c * LANE, (c + 1) * LANE)
        k_ref[:, sl] = _norm64(x_ref[:, nq + c * LANE: nq + (c + 1) * LANE], kg_ref[...], lo)
    v_ref[...] = x_ref[:, nq + nk:]


def diff_post(qkv, q_norm, k_norm):
    m, n = qkv.shape
    tm = min(m, 256)
    nq, nk = N_HEADS * HEAD_W, N_KV * HEAD_W
    return pl.pallas_call(
        _diff_post_kernel,
        grid=(m // tm,),
        in_specs=[
            pl.BlockSpec((tm, n), lambda i: (i, 0)),
            pl.BlockSpec((1, LANE), lambda i: (0, 0)),
            pl.BlockSpec((1, LANE), lambda i: (0, 0)),
        ],
        out_specs=[
            pl.BlockSpec((tm, nq), lambda i: (i, 0)),
            pl.BlockSpec((tm, nk), lambda i: (i, 0)),
            pl.BlockSpec((tm, nk), lambda i: (i, 0)),
        ],
        out_shape=[jax.ShapeDtypeStruct((m, nq), F32), jax.ShapeDtypeStruct((m, nk), F32),
                   jax.ShapeDtypeStruct((m, nk), F32)],
        compiler_params=_params(("parallel",)),
        name="diff_post",
    )(qkv, jnp.tile(q_norm, 2).reshape(1, LANE), jnp.tile(k_norm, 2).reshape(1, LANE))


def _online_update(s, v, m_ref, l_ref, acc_ref):
    m_old = m_ref[...]
    m_new = jnp.maximum(m_old, jnp.max(s, axis=-1, keepdims=True))
    alpha = jnp.exp(m_old - m_new)
    p = jnp.exp(s - m_new)
    l_ref[...] = alpha * l_ref[...] + jnp.sum(p, axis=-1, keepdims=True)
    acc_ref[...] = alpha * acc_ref[...] + _dot(p.astype(BF16), v)
    m_ref[...] = m_new


def _diff_lambda(lam_ref, lam_init):
    lf = lam_ref[...]
    s01 = jnp.sum(lf[0:1] * lf[1:2], axis=-1, keepdims=True)
    s23 = jnp.sum(lf[2:3] * lf[3:4], axis=-1, keepdims=True)
    return jnp.exp(s01) - jnp.exp(s23) + lam_init


def _split_maps(qs):
    lo = lax.broadcasted_iota(jnp.int32, qs.shape, 1) < DIFF_DH
    return jnp.concatenate([jnp.where(lo, qs, 0.0), jnp.where(lo, 0.0, qs)], axis=0)


def _diff_finish_rows(o, lam, sub_norm, lam_init):
    r = o.shape[0] // 2
    a = o[:r] - lam * o[r:]
    return _rms(a, sub_norm) * (1.0 - lam_init)


def _diff_finish(acc, l, lam, sub_norm, lam_init):
    return _diff_finish_rows(acc / l, lam, sub_norm, lam_init)


PROMPT_TQ = 256


LOG2E = math.log2(math.e)
FLASH_CHUNKS = 4


def _flash_tile_t(k_tile, vt_tile, qs_ref, m_ref, l_ref, acc_ref, scale=None, bias=None, far_bias=None,
                  mask=None, n_chunks=FLASH_CHUNKS):
    chunk = qs_ref.shape[0] // n_chunks
    logits = [_dot_t(k_tile, qs_ref[c * chunk:(c + 1) * chunk, :]) for c in range(n_chunks)]
    for c in range(n_chunks):
        cols = slice(c * chunk, (c + 1) * chunk)
        s = logits[c]
        if scale is not None:
            s = s * scale
        if bias is not None:
            s = s + bias(cols)
        if mask is not None:
            s = jnp.where(mask[:, cols] > 0.5, s, NEG_INF)
        m_old = m_ref[:, cols]
        s_max = jnp.max(s, axis=0, keepdims=True)
        if far_bias is None:
            m_new = jnp.maximum(m_old, s_max)
            p = jnp.exp2(s - m_new)
        else:
            fb = far_bias[:, cols]
            m_new = jnp.maximum(m_old, s_max + fb)
            p = jnp.exp2(s - (m_new - fb))
        alpha = jnp.exp2(m_old - m_new)
        l_ref[:, cols] = alpha * l_ref[:, cols] + jnp.sum(p, axis=0, keepdims=True)
        acc_ref[:, cols] = alpha * acc_ref[:, cols] + _dot(vt_tile, p.astype(BF16))
        m_ref[:, cols] = m_new


def _stage_kv(k_ref, v_ref, kb_ref, vt_ref, chunk):
    for c in range(k_ref.shape[1] // chunk):
        rows = slice(c * chunk, (c + 1) * chunk)
        kb_ref[rows, :] = k_ref[0, rows, :].astype(BF16)
        vt_ref[:, rows] = v_ref[0, rows, :].T.astype(BF16)


def _diff_flash_kernel(lam_ref, q_ref, k_ref, v_ref, bias_ref, sn_ref, o_ref,
                       qs_ref, kb_ref, vt_ref, m_ref, l_ref, acc_ref, *, tq, lam_init):
    qi = pl.program_id(2)

    @pl.when(qi == 0)
    def _():
        _stage_kv(k_ref, v_ref, kb_ref, vt_ref, tq)

    qb = q_ref[0]
    qs = jnp.concatenate([qb[:, r * HEAD_W:(r + 1) * HEAD_W] for r in range(HPG)], axis=0)
    qs_ref[...] = _split_maps(qs * (DIFF_DH ** -0.5 * LOG2E)).astype(BF16)
    _reset_flash(m_ref, l_ref, acc_ref)
    far = bias_ref[2, 0:1, :]
    far = jnp.concatenate([far, far], axis=1)

    def far_body(kt, c):
        rows = pl.ds(pl.multiple_of(kt * tq, tq), tq)
        _flash_tile_t(kb_ref[rows, :], vt_ref[:, rows], qs_ref, m_ref, l_ref, acc_ref, far_bias=far)
        return c

    def near_body(kt, c):
        rows = pl.ds(pl.multiple_of(kt * tq, tq), tq)
        b = bias_ref[qi - kt]
        _flash_tile_t(kb_ref[rows, :], vt_ref[:, rows], qs_ref, m_ref, l_ref, acc_ref,
                      bias=lambda cols: jnp.concatenate([b, b], axis=1), n_chunks=1)
        return c

    n_far = jnp.maximum(qi - 1, 0)
    lax.fori_loop(0, n_far, far_body, 0)
    lax.fori_loop(n_far, qi + 1, near_body, 0)
    o = (acc_ref[...] / l_ref[...]).T
    a = _diff_finish_rows(o, _diff_lambda(lam_ref, lam_init), sn_ref[...], lam_init)
    for r in range(HPG):
        o_ref[0, :, r * HEAD_W:(r + 1) * HEAD_W] = a[r * tq:(r + 1) * tq]


def diff_flash(q, k, v, bias, lam_p, sub_norm, lam_init, tq=PROMPT_TQ):
    b, t, _ = q.shape
    gw = HPG * HEAD_W
    rows = 2 * HPG * tq
    return pl.pallas_call(
        functools.partial(_diff_flash_kernel, tq=tq, lam_init=lam_init),
        grid=(b, N_KV, t // tq),
        in_specs=[
            pl.BlockSpec((4, DIFF_DH), lambda i, g, j: (0, 0)),
            pl.BlockSpec((1, tq, gw), lambda i, g, j: (i, j, g)),
            pl.BlockSpec((1, t, HEAD_W), lambda i, g, j: (i, 0, g)),
            pl.BlockSpec((1, t, HEAD_W), lambda i, g, j: (i, 0, g)),
            pl.BlockSpec((4, tq, HPG * tq), lambda i, g, j: (0, 0, g)),
            pl.BlockSpec((1, HEAD_W), lambda i, g, j: (0, 0)),
        ],
        out_specs=pl.BlockSpec((1, tq, gw), lambda i, g, j: (i, j, g)),
        out_shape=jax.ShapeDtypeStruct(q.shape, F32),
        scratch_shapes=[
            pltpu.VMEM((rows, HEAD_W), BF16),
            pltpu.VMEM((t, HEAD_W), BF16),
            pltpu.VMEM((HEAD_W, t), BF16),
            pltpu.VMEM((1, rows), F32),
            pltpu.VMEM((1, rows), F32),
            pltpu.VMEM((HEAD_W, rows), F32),
        ],
        compiler_params=_params(("parallel", "parallel", "arbitrary")),
        name="diff_flash",
    )(lam_p, q, k, v, bias, sub_norm.reshape(1, HEAD_W))


PAGES_PER_STEP = 4


def _decode_page_index(t_new, past_len):
    j = np.arange(PAGE)[:, None]
    i = np.arange(t_new)[None, :]
    far = np.full((PAGE, t_new), N_BUCKETS - 1, np.int32)
    new = np.where(j < t_new, _bucket_np(i - j), -1)
    g = np.arange(N_KV)[None, :, None, None]
    h = np.arange(N_HEADS)[None, None, :, None]
    tiles = []
    for bkt in (far, _bucket_np(PAGE + i - j), new):
        b = bkt[:, None, None, :]
        idx = np.where((b >= 0) & (h // HPG == g), b * N_HEADS + h, -1)
        tiles.append(idx.reshape(PAGE * N_KV, N_HEADS * t_new))
    return np.stack(tiles).astype(np.int32)


def _stack_group_heads(q_ref, g):
    return jnp.concatenate(
        [q_ref[0, :, (g * HPG + r) * HEAD_W:(g * HPG + r + 1) * HEAD_W] for r in range(HPG)], axis=0)


def _query_rows(q_ref, maps, scale):
    pieces = []
    for m in range(maps):
        for h in range(N_HEADS):
            qh = q_ref[0, :, h * HEAD_W:(h + 1) * HEAD_W] * scale
            if maps == 2:
                lo = lax.broadcasted_iota(jnp.int32, qh.shape, 1) < DIFF_DH
                qh = jnp.where(lo, qh, 0.0) if m == 0 else jnp.where(lo, 0.0, qh)
            pieces.append(qh)
    return jnp.concatenate(pieces, axis=0).astype(BF16)


def _page_bias_t(bias_ref, first_page, n_pages, reps):
    tiles = [bias_ref[jnp.where(first_page + u == n_pages - 1, 1, 0)] for u in range(PAGES_PER_STEP)]
    b = jnp.concatenate(tiles, axis=0)
    return jnp.concatenate([b] * reps, axis=1)


def _decode_step_t(k_rows, v_rows, q_ref, bias, m_ref, l_ref, acc_ref, keep=None):
    s = _dot_t(k_rows.astype(BF16), q_ref[...]) + bias
    if keep is not None:
        s = jnp.where(keep > 0.5, s, NEG_INF)
    m_old = m_ref[...]
    m_new = jnp.maximum(m_old, jnp.max(s, axis=0, keepdims=True))
    alpha = jnp.exp(m_old - m_new)
    p = jnp.exp(s - m_new)
    l_ref[...] = alpha * l_ref[...] + jnp.sum(p, axis=0, keepdims=True)
    acc_ref[...] = alpha * acc_ref[...] + _dot(v_rows.T.astype(BF16), p.astype(BF16))
    m_ref[...] = m_new


def _page_rows(refs):
    return jnp.concatenate([r[0, 0] for r in refs], axis=0)


def _diff_decode_kernel(pt_ref, lam_ref, q_ref, kn_ref, vn_ref, *rest, n_pages, lam_init):
    del pt_ref
    kp = rest[:PAGES_PER_STEP]
    vp = rest[PAGES_PER_STEP:2 * PAGES_PER_STEP]
    bias_ref, sn_ref, o_ref, qr_ref, m_ref, l_ref, acc_ref = rest[2 * PAGES_PER_STEP:]
    s = pl.program_id(1)
    t_new = q_ref.shape[1]

    @pl.when(s == 0)
    def _():
        _reset_flash(m_ref, l_ref, acc_ref)
        qr_ref[...] = _query_rows(q_ref, 2, DIFF_DH ** -0.5)
        b = bias_ref[2]
        _decode_step_t(kn_ref[0], vn_ref[0], qr_ref, jnp.concatenate([b, b], axis=1), m_ref, l_ref, acc_ref)

    _decode_step_t(_page_rows(kp), _page_rows(vp), qr_ref,
                   _page_bias_t(bias_ref, s * PAGES_PER_STEP, n_pages, 2), m_ref, l_ref, acc_ref)

    @pl.when(s == pl.num_programs(1) - 1)
    def _():
        o = (acc_ref[...] / l_ref[...]).T
        a = _diff_finish_rows(o, _diff_lambda(lam_ref, lam_init), sn_ref[...], lam_init)
        for h in range(N_HEADS):
            o_ref[0, :, h * HEAD_W:(h + 1) * HEAD_W] = a[h * t_new:(h + 1) * t_new]


PAGE_ROWS = PAGE * N_KV


def _as_page_rows(cache):
    return cache.reshape(*cache.shape[:2], PAGE_ROWS, HEAD_W)


def _page_specs(layer, n):
    def spec(u):
        return pl.BlockSpec((1, 1, PAGE_ROWS, HEAD_W),
                            lambda i, s, pt: (layer, pt[i, s * PAGES_PER_STEP + u], 0, 0))
    return [spec(u) for u in range(PAGES_PER_STEP)] * n


def diff_decode(q, k_new, v_new, cache_k, cache_v, layer, page_table, bias, lam_p, sub_norm, lam_init):
    b, t_new, _ = q.shape
    n_pages = page_table.shape[1]
    lanes = 2 * N_HEADS * t_new
    assert N_HEADS * t_new == LANE
    fixed = lambda *shape: pl.BlockSpec(shape, lambda i, s, pt: (0,) * len(shape))
    per_b = lambda *shape: pl.BlockSpec((1,) + shape, lambda i, s, pt: (i,) + (0,) * len(shape))
    grid_spec = pltpu.PrefetchScalarGridSpec(
        num_scalar_prefetch=1,
        grid=(b, n_pages // PAGES_PER_STEP),
        in_specs=[fixed(4, DIFF_DH), per_b(t_new, N_HEADS * HEAD_W), per_b(PAGE_ROWS, HEAD_W),
                  per_b(PAGE_ROWS, HEAD_W)]
        + _page_specs(layer, 2)
        + [fixed(*bias.shape), fixed(1, HEAD_W)],
        out_specs=per_b(t_new, N_HEADS * HEAD_W),
        scratch_shapes=[
            pltpu.VMEM((lanes, HEAD_W), BF16),
            pltpu.VMEM((1, lanes), F32),
            pltpu.VMEM((1, lanes), F32),
            pltpu.VMEM((HEAD_W, lanes), F32),
        ],
    )
    return pl.pallas_call(
        functools.partial(_diff_decode_kernel, n_pages=n_pages, lam_init=lam_init),
        grid_spec=grid_spec,
        out_shape=jax.ShapeDtypeStruct(q.shape, F32),
        compiler_params=_params(("parallel", "arbitrary")),
        name="diff_decode",
    )(page_table, lam_p, q, k_new, v_new, *([cache_k] * PAGES_PER_STEP), *([cache_v] * PAGES_PER_STEP),
      bias, sub_norm.reshape(1, HEAD_W))


def _pad_rows(a, n):
    return jnp.pad(a, ((0, 0), (0, n - a.shape[1]), (0, 0)))


def _new_page_rows(a):
    return _pad_rows(a, PAGE).reshape(a.shape[0], PAGE_ROWS, HEAD_W)


def diff_layer(xp, xs, gain, w_in, q_norm, k_norm, lam_p, sub_norm, w_out, cache_k, cache_v, layer,
               page_table, bias_p, bias_s, lam_init):
    outs = []
    for x, paged in ((xp, False), (xs, True)):
        b, t, d = x.shape
        x2 = x.reshape(b * t, d)
        q, k, v = diff_post(dense(x2, w_in, layer, gain=gain), q_norm, k_norm)
        q3, k3, v3 = (a.reshape(b, t, -1) for a in (q, k, v))
        if paged:
            o = diff_decode(q3, _new_page_rows(k3), _new_page_rows(v3), _as_page_rows(cache_k),
                            _as_page_rows(cache_v), layer, page_table, bias_s, lam_p, sub_norm, lam_init)
        else:
            o = diff_flash(q3, k3, v3, bias_p, lam_p, sub_norm, lam_init)
        y = dense(o.reshape(b * t, -1), w_out, layer, res=x2).reshape(b, t, d)
        outs.append((y, k3.reshape(b, t, N_KV, HEAD_W), v3.reshape(b, t, N_KV, HEAD_W)))
    (yp, kp, vp), (ys, ks, vs) = outs
    return yp, ys, kp, vp, ks, vs


SSM_GW = SSM_HPG * SSM_HEADDIM
SSM_BC = 2 * SSM_GROUPS * D_STATE
CONV_PAD = 8


def _conv_silu(buf_ref, w_ref, b_ref, n):
    acc = b_ref[...]
    for k in range(CONV_W):
        acc = acc + buf_ref[pl.ds(CONV_PAD - (CONV_W - 1) + k, n), :] * w_ref[k:k + 1, :]
    return acc * jax.nn.sigmoid(acc)


def _ssd_kernel(z_ref, x_ref, bc_ref, dt_ref, cbx_ref, cbbc_ref, wx_ref, wbc_ref, bx_ref, bbc_ref,
                dtb_ref, alog_ref, dsk_ref, ng_ref, e_ref, tri_ref, h0_ref,
                y_ref, hout_ref, ht_ref, xbuf_ref, bcbuf_ref, xa_ref, bca_ref, *, t_valid):
    c = pl.program_id(1)
    n = x_ref.shape[1]

    @pl.when(c == 0)
    def _():
        for g in range(SSM_GROUPS):
            ht_ref[g] = h0_ref[0, g].T
        xbuf_ref[0:CONV_PAD] = cbx_ref[0]
        bcbuf_ref[0:CONV_PAD] = cbbc_ref[0]

    xbuf_ref[CONV_PAD:CONV_PAD + n] = x_ref[0]
    bcbuf_ref[CONV_PAD:CONV_PAD + n] = bc_ref[0]
    xa_ref[...] = _conv_silu(xbuf_ref, wx_ref, bx_ref, n)
    bca_ref[...] = _conv_silu(bcbuf_ref, wbc_ref, bbc_ref, n)
    xbuf_ref[0:CONV_PAD] = xbuf_ref[n:n + CONV_PAD]
    bcbuf_ref[0:CONV_PAD] = bcbuf_ref[n:n + CONV_PAD]

    row = lax.broadcasted_iota(jnp.int32, (n, LANE), 0) + c * n
    dtr = dt_ref[0] + dtb_ref[...]
    dt = jnp.maximum(dtr, 0.0) + jnp.log1p(jnp.exp(-jnp.abs(dtr)))
    dt = jnp.where(row < t_valid, dt, 0.0)
    dta = dt * (-jnp.exp(alog_ref[...]))
    hi, mid, lo = _split3(dta)
    tri = tri_ref[...]
    cs = _dot(tri, hi) + _dot(tri, mid) + _dot(tri, lo)
    cs_last = cs[n - 1:n, :]
    cs_t = cs.T
    dt_t = dt.T
    stacked = jnp.concatenate(
        [jnp.exp(cs), jnp.exp(cs_last - cs) * dt, jnp.broadcast_to(jnp.exp(cs_last), (8, LANE))], axis=0)
    ex = _dot_exact_rhs(stacked, e_ref[...])
    causal = (lax.broadcasted_iota(jnp.int32, (n, n), 0) >= lax.broadcasted_iota(jnp.int32, (n, n), 1))

    for g in range(SSM_GROUPS):
        gs = slice(g * SSM_GW, (g + 1) * SSM_GW)
        bm = bca_ref[:, g * D_STATE:(g + 1) * D_STATE]
        cm = bca_ref[:, (SSM_GROUPS + g) * D_STATE:(SSM_GROUPS + g + 1) * D_STATE].astype(BF16)
        cb = _dot_t(cm, bm.astype(BF16))
        xg = xa_ref[:, gs]
        ys = []
        for r in range(SSM_HPG):
            h = g * SSM_HPG + r
            seg = cs[:, h:h + 1] - cs_t[h:h + 1, :]
            dec = jnp.where(causal, jnp.exp(jnp.where(causal, seg, 0.0)), 0.0)
            mm = (cb * dec * dt_t[h:h + 1, :]).astype(BF16)
            ys.append(_dot(mm, xg[:, r * SSM_HEADDIM:(r + 1) * SSM_HEADDIM].astype(BF16)))
        ht = ht_ref[g]
        y = jnp.concatenate(ys, axis=1) + _dot(cm, ht.astype(BF16)) * ex[0:n, gs]
        y = y + dsk_ref[:, gs] * xg
        zg = z_ref[0, :, gs]
        y = y * (zg * jax.nn.sigmoid(zg))
        y_ref[0, :, gs] = _rms(y, ng_ref[:, gs])
        xw = (xg * ex[n:2 * n, gs]).astype(BF16)
        ht_ref[g] = ht * ex[2 * n:2 * n + 1, gs] + _dot(bm.T.astype(BF16), xw)

    @pl.when(c == pl.num_programs(1) - 1)
    def _():
        for g in range(SSM_GROUPS):
            hout_ref[0, g] = ht_ref[g].T


def _head_expand_matrix():
    e = np.zeros((LANE, D_INNER), np.float32)
    for h in range(SSM_HEADS):
        e[h, h * SSM_HEADDIM:(h + 1) * SSM_HEADDIM] = 1.0
    return e


def ssd_core(zx, dt_raw, conv_buf, h0, conv_w, conv_b, dt_bias, a_log, d_skip, norm_g, t_valid):
    b, t, _ = zx.shape
    n = SSM_CHUNK
    pad_h = LANE - SSM_HEADS
    cb = jnp.pad(conv_buf, ((0, 0), (CONV_PAD - (CONV_W - 1), 0), (0, 0)))
    tri = jnp.asarray(np.tril(np.ones((n, n), np.float32)), BF16)
    e = jnp.asarray(_head_expand_matrix(), BF16)
    d_exp = jnp.repeat(d_skip, SSM_HEADDIM).reshape(1, D_INNER)
    fixed = lambda *shape: pl.BlockSpec(shape, lambda i, c: (0,) * len(shape))
    per_b = lambda *shape: pl.BlockSpec((1,) + shape, lambda i, c: (i,) + (0,) * len(shape))
    y, h_last = pl.pallas_call(
        functools.partial(_ssd_kernel, t_valid=t_valid),
        grid=(b, t // n),
        in_specs=[
            pl.BlockSpec((1, n, D_INNER), lambda i, c: (i, c, 0)),
            pl.BlockSpec((1, n, D_INNER), lambda i, c: (i, c, 1)),
            pl.BlockSpec((1, n, SSM_BC), lambda i, c: (i, c, 2 * D_INNER // SSM_BC)),
            pl.BlockSpec((1, n, LANE), lambda i, c: (i, c, 0)),
            per_b(CONV_PAD, D_INNER), per_b(CONV_PAD, SSM_BC),
            fixed(CONV_W, D_INNER), fixed(CONV_W, SSM_BC), fixed(1, D_INNER), fixed(1, SSM_BC),
            fixed(1, LANE), fixed(1, LANE), fixed(1, D_INNER), fixed(1, D_INNER),
            fixed(LANE, D_INNER), fixed(n, n),
            per_b(SSM_GROUPS, SSM_GW, D_STATE),
        ],
        out_specs=[
            pl.BlockSpec((1, n, D_INNER), lambda i, c: (i, c, 0)),
            per_b(SSM_GROUPS, SSM_GW, D_STATE),
        ],
        out_shape=[jax.ShapeDtypeStruct((b, t, D_INNER), F32),
                   jax.ShapeDtypeStruct((b, SSM_GROUPS, SSM_GW, D_STATE), F32)],
        scratch_shapes=[
            pltpu.VMEM((SSM_GROUPS, D_STATE, SSM_GW), F32),
            pltpu.VMEM((n + CONV_PAD, D_INNER), F32),
            pltpu.VMEM((n + CONV_PAD, SSM_BC), F32),
            pltpu.VMEM((n, D_INNER), F32),
            pltpu.VMEM((n, SSM_BC), F32),
        ],
        compiler_params=_params(("parallel", "arbitrary")),
        name="ssd_core",
    )(zx, zx, zx, dt_raw, cb[:, :, :D_INNER], cb[:, :, D_INNER:],
      conv_w[:, :D_INNER], conv_w[:, D_INNER:], conv_b[:D_INNER].reshape(1, -1), conv_b[D_INNER:].reshape(1, -1),
      jnp.pad(dt_bias, (0, pad_h)).reshape(1, LANE), jnp.pad(a_log, (0, pad_h)).reshape(1, LANE),
      d_exp, norm_g.reshape(1, D_INNER), e, tri,
      h0.reshape(b, SSM_GROUPS, SSM_GW, D_STATE))
    return y, h_last.reshape(b, SSM_HEADS, SSM_HEADDIM, D_STATE)


def ssd_layer(xp, xs, gain, w_in, layer, conv_w, conv_b, dt_bias, a_log, d_skip, norm_g, w_out,
              state_conv, state_ssm):
    nzx = D_INNER + CONV_DIM
    w_dt = jnp.pad(w_in[layer, :, nzx:], ((0, 0), (0, LANE - SSM_HEADS)))[None]
    outs = []
    for x, conv_buf, h0 in ((xp, None, None), (xs, state_conv, state_ssm)):
        b, t, d = x.shape
        x2 = x.reshape(b * t, d)
        if conv_buf is None:
            conv_buf = jnp.zeros((b, CONV_W - 1, CONV_DIM), F32)
            h0 = jnp.zeros((b, SSM_HEADS, SSM_HEADDIM, D_STATE), F32)
        zx = dense(x2, w_in, layer, gain=gain, n=nzx).reshape(b, t, -1)
        dt_raw = dense(x2, w_dt, 0, gain=gain).reshape(b, t, LANE)
        tp = -(-t // SSM_CHUNK) * SSM_CHUNK
        y, h_last = ssd_core(_pad_rows(zx, tp), _pad_rows(dt_raw, tp), conv_buf, h0, conv_w, conv_b,
                             dt_bias, a_log, d_skip, norm_g, t)
        y = y[:, :t].reshape(b * t, D_INNER)
        xpad = jnp.concatenate([conv_buf, zx[:, :, D_INNER:]], axis=1)
        outs.append((dense(y, w_out, layer, res=x2).reshape(b, t, d), h_last, xpad[:, t:]))
    (yp, hp, cp), (ys, hs, cs) = outs
    return yp, ys, hp, cp, hs, cs


NSA_KV = N_KV * HEAD_W
NSA_SCALE = HEAD_W ** -0.5
CMP_PAIRS = CMP_STRIDE // 2
CMP_PAGES = 8


def _nsa_post_kernel(x_ref, qg_ref, kg_ref, q_ref, kcr_ref, vcr_ref, ks_ref, vs_ref, kw_ref, vw_ref):
    nq = N_HEADS * HEAD_W
    for c in range(N_HEADS):
        sl = slice(c * LANE, (c + 1) * LANE)
        q_ref[:, sl] = _rms(x_ref[:, sl], qg_ref[...])
    outs = (kcr_ref, vcr_ref, ks_ref, vs_ref, kw_ref, vw_ref)
    gains = (None, None, 1, None, 2, None)
    for a, (o_ref, gi) in enumerate(zip(outs, gains)):
        for c in range(N_KV):
            col = nq + a * NSA_KV + c * LANE
            blk = x_ref[:, col:col + LANE]
            o_ref[:, c * LANE:(c + 1) * LANE] = blk if gi is None else _rms(blk, kg_ref[gi:gi + 1, :])


def nsa_post(proj, q_norm, k_norm):
    m, n = proj.shape
    tm = min(m, 256)
    nq = N_HEADS * HEAD_W
    kv_spec = pl.BlockSpec((tm, NSA_KV), lambda i: (i, 0))
    kv_shape = jax.ShapeDtypeStruct((m, NSA_KV), F32)
    return pl.pallas_call(
        _nsa_post_kernel,
        grid=(m // tm,),
        in_specs=[
            pl.BlockSpec((tm, n), lambda i: (i, 0)),
            pl.BlockSpec((1, LANE), lambda i: (0, 0)),
            pl.BlockSpec((3, LANE), lambda i: (0, 0)),
        ],
        out_specs=[pl.BlockSpec((tm, nq), lambda i: (i, 0))] + [kv_spec] * 6,
        out_shape=[jax.ShapeDtypeStruct((m, nq), F32)] + [kv_shape] * 6,
        compiler_params=_params(("parallel",)),
        name="nsa_post",
    )(proj, q_norm.reshape(1, LANE), k_norm)


def _cmp_uv_kernel(pt_ref, *refs):
    del pt_ref
    pages = refs[:CMP_PAGES]
    wab_ref, o_ref = refs[CMP_PAGES:]
    subs = PAGE // CMP_STRIDE
    for g in range(N_KV):
        acc = jnp.zeros((CMP_PAGES * subs, 2 * HEAD_W), F32)
        for lp in range(CMP_PAIRS):
            halves = []
            for li in range(2):
                rows = pl.ds((2 * lp + li) * N_KV + g, subs, stride=CMP_STRIDE * N_KV)
                halves.append(jnp.concatenate([p[0, 0, rows, :] for p in pages], axis=0))
            acc = acc + _dot(jnp.concatenate(halves, axis=1).astype(BF16), wab_ref[lp])
        o_ref[0, :, g * 2 * HEAD_W:(g + 1) * 2 * HEAD_W] = acc


def cmp_uv(rows, layer, page_table, wab):
    b, n_pages = page_table.shape
    subs = PAGE // CMP_STRIDE

    def spec(u):
        return pl.BlockSpec((1, 1, PAGE_ROWS, HEAD_W),
                            lambda i, c, pt: (layer, pt[i, c * CMP_PAGES + u], 0, 0))

    grid_spec = pltpu.PrefetchScalarGridSpec(
        num_scalar_prefetch=1,
        grid=(b, n_pages // CMP_PAGES),
        in_specs=[spec(u) for u in range(CMP_PAGES)]
        + [pl.BlockSpec(wab.shape, lambda i, c, pt: (0, 0, 0))],
        out_specs=pl.BlockSpec((1, CMP_PAGES * subs, 2 * NSA_KV), lambda i, c, pt: (i, c, 0)),
    )
    return pl.pallas_call(
        _cmp_uv_kernel,
        grid_spec=grid_spec,
        out_shape=jax.ShapeDtypeStruct((b, n_pages * subs, 2 * NSA_KV), F32),
        compiler_params=_params(("parallel", "arbitrary")),
        name="cmp_uv",
    )(page_table, *([rows] * CMP_PAGES), wab)


def _cmp_finish_kernel(uv_ref, pe_ref, w1_ref, w2_ref, kn_ref, o_ref, *, norm):
    n = uv_ref.shape[1]
    c = _dot(jnp.broadcast_to(pe_ref[...], (8, pe_ref.shape[1])).astype(BF16), w1_ref[...])[0:1]
    for g in range(N_KV):
        u = uv_ref[0, :, g * 2 * HEAD_W:g * 2 * HEAD_W + HEAD_W]
        v = uv_ref[0, :, g * 2 * HEAD_W + HEAD_W:(g + 1) * 2 * HEAD_W]
        pre = u + pltpu.roll(v, n - 1, axis=0) + c
        out = _dot((pre * jax.nn.sigmoid(pre)).astype(BF16), w2_ref[...])
        if norm:
            out = _rms(out, kn_ref[...])
        o_ref[0, :, g * HEAD_W:(g + 1) * HEAD_W] = out


def cmp_finish(uv, pe, w1, w2, k_norm):
    b, n, _ = uv.shape
    norm = k_norm is not None
    kn = (k_norm if norm else jnp.ones((HEAD_W,), F32)).reshape(1, HEAD_W)
    return pl.pallas_call(
        functools.partial(_cmp_finish_kernel, norm=norm),
        grid=(b,),
        in_specs=[
            pl.BlockSpec((1, n, 2 * NSA_KV), lambda i: (i, 0, 0)),
            pl.BlockSpec((1, pe.size), lambda i: (0, 0)),
            pl.BlockSpec(w1.shape, lambda i: (0, 0)),
            pl.BlockSpec(w2.shape, lambda i: (0, 0)),
            pl.BlockSpec((1, HEAD_W), lambda i: (0, 0)),
        ],
        out_specs=pl.BlockSpec((1, n, NSA_KV), lambda i: (i, 0, 0)),
        out_shape=jax.ShapeDtypeStruct((b, n, NSA_KV), F32),
        compiler_params=_params(("parallel",)),
        name="cmp_finish",
    )(uv, pe.reshape(1, -1), w1, w2, kn)


def compress(rows, layer, page_table, pe, w1, w2, k_norm):
    w1r = w1.reshape(2, CMP_STRIDE, HEAD_W, HEAD_W)
    wab = jnp.transpose(w1r, (1, 2, 0, 3)).reshape(CMP_PAIRS, 2 * HEAD_W, 2 * HEAD_W).astype(BF16)
    uv = cmp_uv(rows, layer, page_table, wab)
    return cmp_finish(uv, pe, w1.astype(BF16), w2.astype(BF16), k_norm)


def _importance_matrix(n_cmp, n_rows, n_slc, n_cols):
    w = np.zeros((n_rows, n_cols), np.float32)
    w_imp = [1.0] + [2.0] * (SLC_RATIO - 1) + [1.0]
    for s in range(n_slc):
        for m, wm in enumerate(w_imp):
            j = SLC_RATIO * s + m - 1
            if 0 <= j < n_cmp:
                w[j, s] += wm
    return w


def _select_blocks(s_slc, q_pos0, n_slc):
    t, w = s_slc.shape
    blk = lax.broadcasted_iota(jnp.int32, (t, w), 1)
    qpos = q_pos0 + lax.broadcasted_iota(jnp.int32, (t, w), 0)
    qb = qpos // SLC_BLOCK
    forced = (blk == 0) | (blk == qb) | (blk == qb - 1)
    score = jnp.where(forced, FORCE_SCORE, jnp.where(blk * SLC_BLOCK <= qpos, s_slc, -1.0))
    score = jnp.where(blk < n_slc, score, -2.0)
    cnt = jnp.zeros((t, w), F32)
    for sp in range(n_slc):
        col = score[:, sp:sp + 1]
        tie = jnp.where(blk > sp, 1.0, 0.0)
        cnt = cnt + jnp.where(col > score, 1.0, jnp.where(col == score, tie, 0.0))
    return jnp.where(cnt < N_SELECT, 1.0, 0.0)


def _masked_softmax(s, valid, axis=-1, base2=False):
    m = jnp.max(s, axis=axis, keepdims=True)
    e = jnp.where(valid, (jnp.exp2 if base2 else jnp.exp)(s - m), 0.0)
    return e / jnp.maximum(jnp.sum(e, axis=axis, keepdims=True), 1e-30)


def _reset_flash(m_ref, l_ref, acc_ref):
    m_ref[...] = jnp.full(m_ref.shape, NEG_INF, F32)
    l_ref[...] = jnp.zeros(l_ref.shape, F32)
    acc_ref[...] = jnp.zeros(acc_ref.shape, F32)


def _block_mask(sel, first_block, n_keys, reps):
    w = sel.shape[1]
    er = lax.broadcasted_iota(jnp.int32, (w, n_keys), 0)
    ec = lax.broadcasted_iota(jnp.int32, (w, n_keys), 1)
    e = jnp.where(er == first_block + ec // SLC_BLOCK, 1.0, 0.0).astype(BF16)
    mt = _dot(sel, e)
    return jnp.concatenate([mt] * reps, axis=0)


def _select_blocks_t(s_slc, q_pos0, n_slc):
    w, t = s_slc.shape
    n8 = -(-n_slc // 8) * 8
    blk = lax.broadcasted_iota(jnp.int32, (n8, t), 0)
    qpos = q_pos0 + lax.broadcasted_iota(jnp.int32, (n8, t), 1)
    qb = qpos // SLC_BLOCK
    forced = (blk == 0) | (blk == qb) | (blk == qb - 1)
    score = jnp.where(forced, FORCE_SCORE, jnp.where(blk * SLC_BLOCK <= qpos, s_slc[:n8], -1.0))
    score = jnp.where(blk < n_slc, score, -2.0)
    cnt = jnp.zeros((n8, t), F32)
    for sp in range(n_slc):
        row = score[sp:sp + 1, :]
        tie = jnp.where(blk > sp, 1.0, 0.0)
        cnt = cnt + jnp.where(row > score, 1.0, jnp.where(row == score, tie, 0.0))
    sel = jnp.where(cnt < N_SELECT, 1.0, 0.0)
    if n8 < w:
        sel = jnp.concatenate([sel, jnp.zeros((w - n8, t), F32)], axis=0)
    return sel


def _nsa_prompt_kernel(q_ref, gt_ref, kc_ref, vc_ref, ks_ref, vs_ref, kw_ref, vw_ref, bias_ref, u_ref, wimp_ref,
                       o_ref, qs_ref, ksb_ref, vst_ref, kwb_ref, vwt_ref, sel_ref, m_ref, l_ref, acc_ref,
                       *, tq, n_slc):
    qi = pl.program_id(2)

    @pl.when(qi == 0)
    def _():
        _stage_kv(ks_ref, vs_ref, ksb_ref, vst_ref, tq)
        _stage_kv(kw_ref, vw_ref, kwb_ref, vwt_ref, tq)

    qs_ref[...] = jnp.concatenate(
        [q_ref[0, :, r * HEAD_W:(r + 1) * HEAD_W] for r in range(HPG)], axis=0).astype(BF16)

    n_pad = kc_ref.shape[1]
    start = pl.multiple_of(n_pad - (tq // CMP_STRIDE) * (qi + 1), 8)
    scale = NSA_SCALE * LOG2E
    bias_c = u_ref[0, pl.ds(start, n_pad), :]
    s = _dot_t(kc_ref[0].astype(BF16), qs_ref[...]) * scale + bias_c
    pc = _masked_softmax(s, bias_c > 0.5 * NEG_INF, axis=0, base2=True)
    o_cmp = _dot(vc_ref[0].T.astype(BF16), pc.astype(BF16))
    psum = pc[:, 0:tq] + pc[:, tq:2 * tq] + pc[:, 2 * tq:3 * tq] + pc[:, 3 * tq:4 * tq]
    hi, mid, lo = _split3(psum)
    wimp = wimp_ref[...]
    s_slc = _dot(wimp, hi) + _dot(wimp, mid) + _dot(wimp, lo)
    sel_ref[...] = _select_blocks_t(s_slc, qi * tq, n_slc).astype(BF16)

    _reset_flash(m_ref, l_ref, acc_ref)

    far = bias_ref[2, 0:1, :]

    def key_mask(kt):
        key = lax.broadcasted_iota(jnp.int32, (tq, LANE), 0)
        blk = lax.broadcasted_iota(jnp.int32, (tq, LANE), 1)
        onehot = jnp.where(blk == kt * (tq // SLC_BLOCK) + key // SLC_BLOCK, 1.0, 0.0).astype(BF16)
        return jnp.concatenate([_dot(onehot, sel_ref[...])] * HPG, axis=1)

    def slc_far_body(kt, c):
        rows = pl.ds(pl.multiple_of(kt * tq, tq), tq)
        _flash_tile_t(ksb_ref[rows, :], vst_ref[:, rows], qs_ref, m_ref, l_ref, acc_ref, scale=scale,
                      far_bias=far, mask=key_mask(kt))
        return c

    def slc_near_body(kt, c):
        rows = pl.ds(pl.multiple_of(kt * tq, tq), tq)
        typ = qi - kt
        _flash_tile_t(ksb_ref[rows, :], vst_ref[:, rows], qs_ref, m_ref, l_ref, acc_ref, scale=scale,
                      bias=lambda cols: bias_ref[typ, :, cols], mask=key_mask(kt))
        return c

    n_far = jnp.maximum(qi - 1, 0)
    lax.fori_loop(0, n_far, slc_far_body, 0)
    lax.fori_loop(n_far, qi + 1, slc_near_body, 0)
    o_slc = acc_ref[...] / l_ref[...]

    _reset_flash(m_ref, l_ref, acc_ref)
    nw = WINDOW // tq

    def win_body(kt, c):
        rows = pl.ds(pl.multiple_of(kt * tq, tq), tq)
        t = qi - kt
        typ = jnp.where(t == nw, 3, jnp.minimum(t, 2))
        _flash_tile_t(kwb_ref[rows, :], vwt_ref[:, rows], qs_ref, m_ref, l_ref, acc_ref, scale=scale,
                      bias=lambda cols: bias_ref[typ, :, cols])
        return c

    lax.fori_loop(jnp.maximum(qi - nw, 0), qi + 1, win_body, 0)
    o_win = acc_ref[...] / l_ref[...]

    sig = jax.nn.sigmoid(gt_ref[0]).T
    for r in range(HPG):
        cs = slice(r * tq, (r + 1) * tq)
        o_t = (sig[r:r + 1, :] * o_cmp[:, cs] + sig[HPG + r:HPG + r + 1, :] * o_slc[:, cs]
               + sig[2 * HPG + r:2 * HPG + r + 1, :] * o_win[:, cs])
        o_ref[0, :, r * HEAD_W:(r + 1) * HEAD_W] = o_t.T


def _prompt_cmp_buckets(tq, n_pad):
    jp = np.arange(2 * n_pad)[:, None] - (n_pad - tq // CMP_STRIDE)
    i = np.arange(tq)[None, :]
    return _bucket_np(i - CMP_STRIDE * jp - (2 * CMP_STRIDE - 1))[None].astype(np.int32)


def nsa_prompt(q, gates, kc, vc, ks, vs, kw, vw, bias, u_bias, tq=PROMPT_TQ):
    b, t, _ = q.shape
    gw = HPG * HEAD_W
    n_pad = kc.shape[1]
    n_slc = t // SLC_BLOCK
    wimp = jnp.asarray(_importance_matrix(n_pad - 1, n_pad, n_slc, LANE).T, BF16)
    rows = HPG * tq
    seq = lambda n: pl.BlockSpec((1, n, HEAD_W), lambda i, g, j: (i, 0, g))
    return pl.pallas_call(
        functools.partial(_nsa_prompt_kernel, tq=tq, n_slc=n_slc),
        grid=(b, N_KV, t // tq),
        in_specs=[
            pl.BlockSpec((1, tq, gw), lambda i, g, j: (i, j, g)),
            pl.BlockSpec((1, tq, LANE), lambda i, g, j: (i, j, g)),
            seq(n_pad), seq(n_pad), seq(t), seq(t), seq(t), seq(t),
            pl.BlockSpec((4, tq, rows), lambda i, g, j: (0, 0, g)),
            pl.BlockSpec((1, 2 * n_pad, rows), lambda i, g, j: (0, 0, g)),
            pl.BlockSpec(wimp.shape, lambda i, g, j: (0, 0)),
        ],
        out_specs=pl.BlockSpec((1, tq, gw), lambda i, g, j: (i, j, g)),
        out_shape=jax.ShapeDtypeStruct(q.shape, F32),
        scratch_shapes=[
            pltpu.VMEM((rows, HEAD_W), BF16),
            pltpu.VMEM((t, HEAD_W), BF16),
            pltpu.VMEM((HEAD_W, t), BF16),
            pltpu.VMEM((t, HEAD_W), BF16),
            pltpu.VMEM((HEAD_W, t), BF16),
            pltpu.VMEM((LANE, tq), BF16),
            pltpu.VMEM((1, rows), F32),
            pltpu.VMEM((1, rows), F32),
            pltpu.VMEM((HEAD_W, rows), F32),
        ],
        compiler_params=_params(("parallel", "parallel", "arbitrary")),
        name="nsa_prompt",
    )(q, gates, kc, vc, ks, vs, kw, vw, bias, u_bias, wimp)


def _decode_cmp_buckets(t_new, past_len):
    n_pad = past_len // CMP_STRIDE
    i = np.arange(t_new)[:, None]
    j = n_pad - LANE + np.arange(LANE)[None, :]
    last = _bucket_np(past_len + i - CMP_STRIDE * j - (2 * CMP_STRIDE - 1))
    last = np.where(j < n_pad - 1, last, -1)
    return np.stack([np.full((t_new, LANE), N_BUCKETS - 1, np.int32), last]).astype(np.int32)


def _decode_win_buckets(t_new, n_tiles):
    i = np.arange(t_new)[:, None]
    idx = np.arange(n_tiles * LANE)[None, :]
    dw = WINDOW + i - idx
    ok = (dw >= 0) & (dw < WINDOW) & (idx < WINDOW + t_new)
    b = np.where(ok, _bucket_np(dw), -1)
    return np.stack([b[:, k * LANE:(k + 1) * LANE] for k in range(n_tiles)]).astype(np.int32)


def _nsa_decode_kernel(pt_ref, q_ref, gt_ref, kc_ref, vc_ref, kn_ref, vn_ref, kw_ref, vw_ref, *rest,
                       n_pages, past_len):
    del pt_ref
    kp = rest[:PAGES_PER_STEP]
    vp = rest[PAGES_PER_STEP:2 * PAGES_PER_STEP]
    (bias_ref, bias_c_ref, bias_w_ref, wimp_ref, o_ref,
     qr_ref, selt_ref, ocmp_ref, owin_ref, m_ref, l_ref, acc_ref) = rest[2 * PAGES_PER_STEP:]
    s = pl.program_id(1)
    t_new = q_ref.shape[1]
    rows = HPG * t_new
    n_slc = -(-(past_len + t_new) // SLC_BLOCK)

    @pl.when(s == 0)
    def _():
        _reset_flash(m_ref, l_ref, acc_ref)
        qr_ref[...] = _query_rows(q_ref, 1, NSA_SCALE)
        n_pad = kc_ref.shape[1]
        sel_t = jnp.zeros(selt_ref.shape, F32)
        for g in range(N_KV):
            sl = slice(g * HEAD_W, (g + 1) * HEAD_W)
            gr = slice(g * rows, (g + 1) * rows)
            qx = _stack_group_heads(q_ref, g).astype(BF16)
            bias_c = jnp.concatenate([bias_c_ref[0, gr, :]] * (n_pad // LANE - 1) + [bias_c_ref[1, gr, :]], axis=1)
            sc = _dot_t(qx, kc_ref[0, :, sl].astype(BF16)) * NSA_SCALE + bias_c
            pc = _masked_softmax(sc, bias_c > 0.5 * NEG_INF)
            ocmp_ref[g] = _dot(pc.astype(BF16), vc_ref[0, :, sl].astype(BF16))
            psum = sum(pc[r * t_new:(r + 1) * t_new] for r in range(1, HPG)) + pc[0:t_new]
            s_slc = _dot_exact_rhs(psum, wimp_ref[...])
            sel = _select_blocks(s_slc, past_len, n_slc)
            sel_pad = jnp.concatenate([sel, jnp.zeros((LANE - t_new, sel.shape[1]), F32)], axis=0).T
            tok = lax.broadcasted_iota(jnp.int32, (LANE, LANE), 0)
            lane = lax.broadcasted_iota(jnp.int32, (LANE, LANE), 1)
            spread = jnp.where((lane % t_new == tok) & (lane // rows == g), 1.0, 0.0).astype(BF16)
            sel_t = sel_t + _dot(sel_pad.astype(BF16), spread)
            n_wt = bias_w_ref.shape[0]
            bias_w = jnp.concatenate([bias_w_ref[k, gr, :] for k in range(n_wt)], axis=1)
            sw = _dot_t(qx, kw_ref[0, :, sl].astype(BF16)) * NSA_SCALE + bias_w
            pw = _masked_softmax(sw, bias_w > 0.5 * NEG_INF)
            owin_ref[g] = _dot(pw.astype(BF16), vw_ref[0, :, sl].astype(BF16))
        selt_ref[...] = sel_t
        _decode_step_t(kn_ref[0], vn_ref[0], qr_ref, bias_ref[2], m_ref, l_ref, acc_ref)

    block_rows = SLC_BLOCK * N_KV
    first_block = s * (PAGES_PER_STEP * PAGE // SLC_BLOCK)
    keep = jnp.concatenate(
        [jnp.broadcast_to(selt_ref[pl.ds(first_block + b, 1), :], (block_rows, selt_ref.shape[1]))
         for b in range(PAGES_PER_STEP * PAGE // SLC_BLOCK)], axis=0)
    _decode_step_t(_page_rows(kp), _page_rows(vp), qr_ref,
                   _page_bias_t(bias_ref, s * PAGES_PER_STEP, n_pages, 1), m_ref, l_ref, acc_ref, keep=keep)

    @pl.when(s == pl.num_programs(1) - 1)
    def _():
        sig = jax.nn.sigmoid(gt_ref[0])
        o_slc_all = (acc_ref[...] / l_ref[...]).T
        for g in range(N_KV):
            o_cmp = ocmp_ref[g]
            o_win = owin_ref[g]
            for r in range(HPG):
                h = g * HPG + r
                rs = slice(r * t_new, (r + 1) * t_new)
                c0 = g * LANE + r
                o_ref[0, :, h * HEAD_W:(h + 1) * HEAD_W] = (
                    sig[:, c0:c0 + 1] * o_cmp[rs]
                    + sig[:, c0 + HPG:c0 + HPG + 1] * o_slc_all[h * t_new:(h + 1) * t_new]
                    + sig[:, c0 + 2 * HPG:c0 + 2 * HPG + 1] * o_win[rs])


def nsa_decode(q, gates, kc, vc, k_new, v_new, kw_src, vw_src, cache_k, cache_v, layer, page_table,
               bias, bias_c, bias_w, past_len):
    b, t_new, _ = q.shape
    n_pages = page_table.shape[1]
    rows = HPG * t_new
    lanes = N_HEADS * t_new
    assert lanes == LANE
    n_pad = kc.shape[1]
    n_slc = -(-(past_len + t_new) // SLC_BLOCK)
    n_cols = -(-n_slc // LANE) * LANE
    wimp = jnp.asarray(_importance_matrix(n_pad - 1, n_pad, n_slc, n_cols), BF16)
    fixed = lambda *shape: pl.BlockSpec(shape, lambda i, s, pt: (0,) * len(shape))
    per_b = lambda *shape: pl.BlockSpec((1,) + shape, lambda i, s, pt: (i,) + (0,) * len(shape))
    grid_spec = pltpu.PrefetchScalarGridSpec(
        num_scalar_prefetch=1,
        grid=(b, n_pages // PAGES_PER_STEP),
        in_specs=[per_b(t_new, N_HEADS * HEAD_W), per_b(t_new, N_KV * LANE),
                  per_b(n_pad, NSA_KV), per_b(n_pad, NSA_KV), per_b(PAGE_ROWS, HEAD_W), per_b(PAGE_ROWS, HEAD_W),
                  per_b(kw_src.shape[1], NSA_KV), per_b(kw_src.shape[1], NSA_KV)]
        + _page_specs(layer, 2)
        + [fixed(*bias.shape), fixed(*bias_c.shape), fixed(*bias_w.shape), fixed(*wimp.shape)],
        out_specs=per_b(t_new, N_HEADS * HEAD_W),
        scratch_shapes=[
            pltpu.VMEM((lanes, HEAD_W), BF16),
            pltpu.VMEM((n_cols, lanes), F32),
            pltpu.VMEM((N_KV, rows, HEAD_W), F32),
            pltpu.VMEM((N_KV, rows, HEAD_W), F32),
            pltpu.VMEM((1, lanes), F32),
            pltpu.VMEM((1, lanes), F32),
            pltpu.VMEM((HEAD_W, lanes), F32),
        ],
    )
    return pl.pallas_call(
        functools.partial(_nsa_decode_kernel, n_pages=n_pages, past_len=past_len),
        grid_spec=grid_spec,
        out_shape=jax.ShapeDtypeStruct(q.shape, F32),
        compiler_params=_params(("parallel", "arbitrary")),
        name="nsa_decode",
    )(page_table, q, gates, kc, vc, k_new, v_new, kw_src, vw_src,
      *([cache_k] * PAGES_PER_STEP), *([cache_v] * PAGES_PER_STEP), bias, bias_c, bias_w, wimp)


def _gate_weights(w_g):
    d = w_g.shape[0]
    w = jnp.transpose(w_g.reshape(d, 3, N_KV, HPG), (0, 2, 1, 3)).reshape(d, N_KV, 3 * HPG)
    return jnp.pad(w, ((0, 0), (0, 0), (0, LANE - 3 * HPG))).reshape(d, N_KV * LANE)


def nsa_layer(xp, xs, gain, w_in, q_norm, k_norm, pe, w1, w2, w_out, caches, layer, page_table,
              bias_p, u_bias, bias_s, bias_sc, bias_sw, past_len):
    cmp_k, cmp_v, slc_k, slc_v, win_k, win_v = caches
    nqkv = N_HEADS * HEAD_W + 6 * NSA_KV
    w_g = _gate_weights(w_in[layer, :, nqkv:])[None]
    outs = []
    for x, paged in ((xp, False), (xs, True)):
        b, t, d = x.shape
        x2 = x.reshape(b * t, d)
        proj = dense(x2, w_in, layer, gain=gain, n=nqkv)
        gates = dense(x2, w_g, 0, gain=gain).reshape(b, t, -1)
        q, kcr, vcr, ks, vs, kw, vw = (a.reshape(b, t, -1) for a in nsa_post(proj, q_norm, k_norm))
        if paged:
            kc = compress(_as_page_rows(cmp_k), layer, page_table, pe[0], w1[0], w2[0], k_norm[0])
            vc = compress(_as_page_rows(cmp_v), layer, page_table, pe[1], w1[1], w2[1], None)
            n_wt = bias_sw.shape[0]
            kw_src = jnp.concatenate([win_k[layer], kw], axis=1)
            vw_src = jnp.concatenate([win_v[layer], vw], axis=1)
            o = nsa_decode(q, gates, kc, vc, _new_page_rows(ks), _new_page_rows(vs),
                           _pad_rows(kw_src, n_wt * LANE), _pad_rows(vw_src, n_wt * LANE),
                           _as_page_rows(slc_k), _as_page_rows(slc_v), layer, page_table,
                           bias_s, bias_sc, bias_sw, past_len)
            kw_out, vw_out = kw_src[:, -WINDOW:], vw_src[:, -WINDOW:]
        else:
            n_pg = t // PAGE
            ident = jnp.arange(b * n_pg, dtype=jnp.int32).reshape(b, n_pg)
            as_pages = lambda a: a.reshape(1, b * n_pg, PAGE_ROWS, HEAD_W)
            kc = compress(as_pages(kcr), 0, ident, pe[0], w1[0], w2[0], k_norm[0])
            vc = compress(as_pages(vcr), 0, ident, pe[1], w1[1], w2[1], None)
            o = nsa_prompt(q, gates, kc, vc, ks, vs, kw, vw, bias_p, u_bias)
            kw_out, vw_out = kw[:, -WINDOW:], vw[:, -WINDOW:]
        y = dense(o.reshape(b * t, -1), w_out, layer, res=x2).reshape(b, t, d)
        st = tuple(a.reshape(b, -1, N_KV, HEAD_W) for a in (kcr, vcr, ks, vs, kw_out, vw_out))
        outs.append((y, st))
    (yp, stp), (ys, sts) = outs
    return yp, ys, stp, sts


def kernel(x_prompt, x_sample, cache_diff_k, cache_diff_v, state_ssm, state_conv, cache_nsa_cmp_k, cache_nsa_cmp_v, cache_nsa_slc_k, cache_nsa_slc_v, cache_nsa_win_k, cache_nsa_win_v, cache_mem_k, cache_mem_v, page_table, mem_prompt, rel_bias_table, norm_mix, norm_xattn, norm_mem, norm_ffn, diff_w_in, diff_q_norm, diff_k_norm, diff_lambda, diff_sub_norm, diff_w_out, ssm_w_in, ssm_conv_w, ssm_conv_b, ssm_dt_bias, ssm_a_log, ssm_d, ssm_norm, ssm_w_out, nsa_w_in, nsa_q_norm, nsa_k_norm, nsa_cmp_pe, nsa_cmp_w1, nsa_cmp_w2, nsa_w_out, xattn_w_q, xattn_w_k, xattn_w_v, xattn_q_norm, xattn_k_norm, xattn_w_o, ffn_w1, ffn_w3, ffn_w2):
    xp, xs = x_prompt, x_sample
    bp, t, d = xp.shape
    bs, t_new, _ = xs.shape
    past_len = page_table.shape[1] * PAGE
    depth = norm_mix.shape[0]

    bias_p = bias_tiles(_prompt_attn_buckets(PROMPT_TQ), rel_bias_table, heads_on_lanes=True, scale=LOG2E)
    u_bias = bias_tiles(_prompt_cmp_buckets(PROMPT_TQ, t // CMP_STRIDE), rel_bias_table, heads_on_lanes=True,
                        scale=LOG2E)
    bias_s = bias_tiles_indexed(_decode_page_index(t_new, past_len), rel_bias_table)
    bias_sc = bias_tiles(_decode_cmp_buckets(t_new, past_len), rel_bias_table)
    n_wt = -(-(WINDOW + t_new) // LANE)
    bias_sw = bias_tiles(_decode_win_buckets(t_new, n_wt), rel_bias_table)

    mem_k, mem_v = mem_kv(mem_prompt, norm_mem, xattn_w_k, xattn_w_v, xattn_k_norm)
    xw = X_HEADS * X_DH
    win_k = cache_nsa_win_k.reshape(*cache_nsa_win_k.shape[:3], NSA_KV)
    win_v = cache_nsa_win_v.reshape(*cache_nsa_win_v.shape[:3], NSA_KV)

    dkp, dvp, dks, dvs = [], [], [], []
    ssp, cvp, sss, cvs = [], [], [], []
    nsp, nss = [], []
    for i in range(depth):
        kind, j = i % 3, i // 3
        if kind == 0:
            lam_init = 0.8 - 0.6 * math.exp(-0.3 * i)
            xp, xs, kp_, vp_, ks_, vs_ = diff_layer(
                xp, xs, norm_mix[i], diff_w_in, diff_q_norm[j], diff_k_norm[j], diff_lambda[j],
                diff_sub_norm[j], diff_w_out, cache_diff_k, cache_diff_v, j, page_table,
                bias_p, bias_s, lam_init)
            dkp.append(kp_)
            dvp.append(vp_)
            dks.append(ks_)
            dvs.append(vs_)
        elif kind == 1:
            xp, xs, hp_, cp_, hs_, cs_ = ssd_layer(
                xp, xs, norm_mix[i], ssm_w_in, j, ssm_conv_w[j], ssm_conv_b[j], ssm_dt_bias[j], ssm_a_log[j],
                ssm_d[j], ssm_norm[j], ssm_w_out, state_conv[j], state_ssm[j])
            ssp.append(hp_)
            cvp.append(cp_)
            sss.append(hs_)
            cvs.append(cs_)
        else:
            caches = (cache_nsa_cmp_k, cache_nsa_cmp_v, cache_nsa_slc_k, cache_nsa_slc_v, win_k, win_v)
            xp, xs, stp, sts = nsa_layer(
                xp, xs, norm_mix[i], nsa_w_in, nsa_q_norm[j], nsa_k_norm[j], nsa_cmp_pe[j], nsa_cmp_w1[j],
                nsa_cmp_w2[j], nsa_w_out, caches, j, page_table,
                bias_p, u_bias, bias_s, bias_sc, bias_sw, past_len)
            nsp.append(stp)
            nss.append(sts)
        xp = xattn(xp, norm_xattn[i], xattn_w_q, xattn_q_norm[i], mem_k[i], mem_v[i], xattn_w_o, i)
        xs = xattn(xs, norm_xattn[i], xattn_w_q, xattn_q_norm[i], cache_mem_k[i].reshape(bs, N_MEM, xw),
                   cache_mem_v[i].reshape(bs, N_MEM, xw), xattn_w_o, i)
        xp = ffn(xp.reshape(bp * t, d), norm_ffn[i], ffn_w1, ffn_w3, ffn_w2, i).reshape(bp, t, d)
        xs = ffn(xs.reshape(bs * t_new, d), norm_ffn[i], ffn_w1, ffn_w3, ffn_w2, i).reshape(bs, t_new, d)

    st = lambda xs_: jnp.stack(xs_, axis=0)
    nsp_t = [st([s[a] for s in nsp]) for a in range(6)]
    nss_t = [st([s[a] for s in nss]) for a in range(6)]
    mem_shape = (depth, bp, N_MEM, X_HEADS, X_DH)
    return (xp, xs,
            st(dkp), st(dvp), st(dks), st(dvs),
            st(ssp), st(cvp), st(sss), st(cvs),
            *nsp_t, *nss_t,
            mem_k.reshape(mem_shape), mem_v.reshape(mem_shape))
```

```python
import functools
import math

import numpy as np
import jax
import jax.numpy as jnp
from jax import lax
from jax.experimental import pallas as pl
from jax.experimental.pallas import tpu as pltpu

F32 = jnp.float32
BF16 = jnp.bfloat16

D_MODEL = 2048
DEPTH = 4
PAGE = 128
N_HEADS = 16
N_KV = 4
HPG = N_HEADS // N_KV
DIFF_DH = 64
HEAD_W = 128
N_BUCKETS = 32
MAX_EXACT = 16
MAX_DIST = 128
D_INNER = 2 * D_MODEL
SSM_HEADDIM = 64
SSM_HEADS = D_INNER // SSM_HEADDIM
SSM_GROUPS = 8
SSM_HPG = SSM_HEADS // SSM_GROUPS
D_STATE = 128
CONV_W = 4
CONV_DIM = D_INNER + 2 * SSM_GROUPS * D_STATE
SSM_CHUNK = 128
CMP_STRIDE = 16
SLC_BLOCK = 64
SLC_RATIO = SLC_BLOCK // CMP_STRIDE
N_SELECT = 16
WINDOW = 512
N_MEM = 256
X_HEADS = 4
X_DH = 128
EPS = 1e-6
NEG_INF = -1e30
FORCE_SCORE = 1e4

LANE = 128
VMEM_LIMIT = 56 * 1024 * 1024


def _params(sem):
    return pltpu.CompilerParams(dimension_semantics=sem, vmem_limit_bytes=VMEM_LIMIT)


def _rms(x, gain):
    return x * lax.rsqrt(jnp.mean(x * x, axis=-1, keepdims=True) + EPS) * gain


def _dot(a, b):
    return jnp.dot(a, b, preferred_element_type=F32)


def _dot_t(a, b):
    return lax.dot_general(a, b, (((1,), (1,)), ((), ())), preferred_element_type=F32)


def _split3(x):
    hi = x.astype(BF16)
    r1 = x - hi.astype(F32)
    mid = r1.astype(BF16)
    lo = (r1 - mid.astype(F32)).astype(BF16)
    return hi, mid, lo


def _dot_exact_rhs(x, m_bf16):
    hi, mid, lo = _split3(x)
    return _dot(hi, m_bf16) + _dot(mid, m_bf16) + _dot(lo, m_bf16)


def _dense_kernel(*refs, norm, residual):
    refs = list(refs)
    x_ref = refs.pop(0)
    g_ref = refs.pop(0) if norm else None
    w_ref = refs.pop(0)
    r_ref = refs.pop(0) if residual else None
    o_ref, xb_ref = refs

    @pl.when(pl.program_id(1) == 0)
    def _():
        x = x_ref[...]
        if norm:
            x = _rms(x, g_ref[...])
        xb_ref[...] = x.astype(BF16)

    y = _dot(xb_ref[...], w_ref[0].astype(BF16))
    if residual:
        y = y + r_ref[...]
    o_ref[...] = y


def dense(x, w, layer, gain=None, res=None, n=None):
    m, k = x.shape
    n = w.shape[2] if n is None else n
    tm = m if m <= 512 else (1024 if k <= D_MODEL else 512)
    tn = LANE if n % 256 else (512 if n % 512 == 0 else 256)
    assert m % tm == 0 and n % tn == 0
    norm, residual = gain is not None, res is not None
    in_specs = [pl.BlockSpec((tm, k), lambda i, j: (i, 0))]
    args = [x]
    if norm:
        in_specs.append(pl.BlockSpec((1, k), lambda i, j: (0, 0)))
        args.append(gain.reshape(1, k))
    in_specs.append(pl.BlockSpec((1, k, tn), lambda i, j: (layer, 0, j)))
    args.append(w)
    if residual:
        in_specs.append(pl.BlockSpec((tm, tn), lambda i, j: (i, j)))
        args.append(res)
    return pl.pallas_call(
        functools.partial(_dense_kernel, norm=norm, residual=residual),
        grid=(m // tm, n // tn),
        in_specs=in_specs,
        out_specs=pl.BlockSpec((tm, tn), lambda i, j: (i, j)),
        out_shape=jax.ShapeDtypeStruct((m, n), F32),
        scratch_shapes=[pltpu.VMEM((tm, k), BF16)],
        compiler_params=_params(("parallel", "arbitrary")),
        name="dense",
    )(*args)


def _ffn_kernel(x_ref, g_ref, w1_ref, w3_ref, w2_ref, o_ref, xb_ref):
    @pl.when(pl.program_id(1) == 0)
    def _():
        x = x_ref[...]
        xb_ref[...] = _rms(x, g_ref[...]).astype(BF16)
        o_ref[...] = x

    xb = xb_ref[...]
    h1 = _dot(xb, w1_ref[0].astype(BF16))
    h3 = _dot(xb, w3_ref[0].astype(BF16))
    a = (h1 * jax.nn.sigmoid(h1) * h3).astype(BF16)
    o_ref[...] += _dot(a, w2_ref[0].astype(BF16))


def ffn(x, gain, w1, w3, w2, layer):
    m, d = x.shape
    f = w1.shape[2]
    tm = min(m, 1024)
    tf = 256
    assert m % tm == 0 and f % tf == 0
    return pl.pallas_call(
        _ffn_kernel,
        grid=(m // tm, f // tf),
        in_specs=[
            pl.BlockSpec((tm, d), lambda i, j: (i, 0), pipeline_mode=pl.Buffered(1)),
            pl.BlockSpec((1, d), lambda i, j: (0, 0)),
            pl.BlockSpec((1, d, tf), lambda i, j: (layer, 0, j)),
            pl.BlockSpec((1, d, tf), lambda i, j: (layer, 0, j)),
            pl.BlockSpec((1, tf, d), lambda i, j: (layer, j, 0)),
        ],
        out_specs=pl.BlockSpec((tm, d), lambda i, j: (i, 0)),
        out_shape=jax.ShapeDtypeStruct((m, d), F32),
        scratch_shapes=[pltpu.VMEM((tm, d), BF16)],
        compiler_params=_params(("parallel", "arbitrary")),
        name="ffn",
    )(x, gain.reshape(1, d), w1, w3, w2)


def _mem_kv_kernel(mem_ref, g_ref, wk_ref, wv_ref, kn_ref, k_ref, v_ref):
    m = _rms(mem_ref[0], g_ref[0]).astype(BF16)
    k = _dot(m, wk_ref[0].astype(BF16))
    v_ref[0, 0] = _dot(m, wv_ref[0].astype(BF16))
    for h in range(X_HEADS):
        sl = slice(h * X_DH, (h + 1) * X_DH)
        k_ref[0, 0, :, sl] = _rms(k[:, sl], kn_ref[0])


def mem_kv(mem, g_mem, wk, wv, k_norm):
    b = mem.shape[0]
    nl = wk.shape[0]
    hw = X_HEADS * X_DH
    shape = jax.ShapeDtypeStruct((nl, b, N_MEM, hw), F32)
    return pl.pallas_call(
        _mem_kv_kernel,
        grid=(nl, b),
        in_specs=[
            pl.BlockSpec((1, N_MEM, D_MODEL), lambda l, i: (i, 0, 0)),
            pl.BlockSpec((1, 1, D_MODEL), lambda l, i: (l, 0, 0)),
            pl.BlockSpec((1, D_MODEL, hw), lambda l, i: (l, 0, 0)),
            pl.BlockSpec((1, D_MODEL, hw), lambda l, i: (l, 0, 0)),
            pl.BlockSpec((1, 1, X_DH), lambda l, i: (l, 0, 0)),
        ],
        out_specs=[pl.BlockSpec((1, 1, N_MEM, hw), lambda l, i: (l, i, 0, 0))] * 2,
        out_shape=[shape, shape],
        compiler_params=_params(("parallel", "parallel")),
        name="mem_kv",
    )(mem, g_mem.reshape(nl, 1, D_MODEL), wk, wv, k_norm.reshape(nl, 1, X_DH))


def _xattn_kernel(x_ref, g_ref, wq_ref, qn_ref, k_ref, v_ref, wo_ref, o_ref):
    x = x_ref[0]
    q = _dot(_rms(x, g_ref[...]).astype(BF16), wq_ref[0].astype(BF16))
    outs = []
    for h in range(X_HEADS):
        sl = slice(h * X_DH, (h + 1) * X_DH)
        qh = _rms(q[:, sl], qn_ref[...]).astype(BF16)
        s = _dot_t(qh, k_ref[0, :, sl].astype(BF16)) * (X_DH ** -0.5)
        e = jnp.exp(s - jnp.max(s, axis=-1, keepdims=True))
        p = e / jnp.sum(e, axis=-1, keepdims=True)
        outs.append(_dot(p.astype(BF16), v_ref[0, :, sl].astype(BF16)))
    o = jnp.concatenate(outs, axis=1).astype(BF16)
    o_ref[0] = x + _dot(o, wo_ref[0].astype(BF16))


def xattn(x, gain, wq, q_norm, k, v, wo, layer):
    b, t, d = x.shape
    hw = X_HEADS * X_DH
    tm = min(t, 512)
    return pl.pallas_call(
        _xattn_kernel,
        grid=(b, t // tm),
        in_specs=[
            pl.BlockSpec((1, tm, d), lambda i, j: (i, j, 0)),
            pl.BlockSpec((1, d), lambda i, j: (0, 0)),
            pl.BlockSpec((1, d, hw), lambda i, j: (layer, 0, 0)),
            pl.BlockSpec((1, X_DH), lambda i, j: (0, 0)),
            pl.BlockSpec((1, N_MEM, hw), lambda i, j: (i, 0, 0)),
            pl.BlockSpec((1, N_MEM, hw), lambda i, j: (i, 0, 0)),
            pl.BlockSpec((1, hw, d), lambda i, j: (layer, 0, 0)),
        ],
        out_specs=pl.BlockSpec((1, tm, d), lambda i, j: (i, j, 0)),
        out_shape=jax.ShapeDtypeStruct((b, t, d), F32),
        compiler_params=_params(("parallel", "parallel")),
        name="xattn",
    )(x, gain.reshape(1, d), wq, q_norm.reshape(1, X_DH), k, v, wo)


def _bucket_np(dist):
    n = np.maximum(dist, 0)
    nf = np.maximum(n, 1).astype(np.float64)
    large = MAX_EXACT + (np.log(nf / MAX_EXACT) / math.log(MAX_DIST / MAX_EXACT)
                         * (N_BUCKETS - MAX_EXACT)).astype(np.int64)
    b = np.where(n < MAX_EXACT, n, np.minimum(large, N_BUCKETS - 1))
    return np.where(dist < 0, -1, b).astype(np.int32)


def _bias_kernel(bkt_ref, tab_ref, o_ref, *, scale):
    h = pl.program_id(1)
    b = bkt_ref[0]
    acc = jnp.full(b.shape, NEG_INF, F32)
    for k in range(N_BUCKETS):
        acc = jnp.where(b == k, tab_ref[k, h] * scale, acc)
    o_ref[0] = acc


def _bias_packed_kernel(idx_ref, tab_ref, o_ref, *, scale):
    idx = idx_ref[0]
    acc = jnp.full(idx.shape, NEG_INF, F32)
    for k in range(N_BUCKETS):
        for h in range(N_HEADS):
            acc = jnp.where(idx == k * N_HEADS + h, tab_ref[k, h] * scale, acc)
    o_ref[0] = acc


def bias_tiles_indexed(idx, table, scale=1.0):
    nt, r, w = idx.shape
    return pl.pallas_call(
        functools.partial(_bias_packed_kernel, scale=scale),
        grid=(nt,),
        in_specs=[
            pl.BlockSpec((1, r, w), lambda t: (t, 0, 0)),
            pl.BlockSpec(memory_space=pltpu.SMEM),
        ],
        out_specs=pl.BlockSpec((1, r, w), lambda t: (t, 0, 0)),
        out_shape=jax.ShapeDtypeStruct((nt, r, w), F32),
        compiler_params=_params(("parallel",)),
        name="bias_tiles_indexed",
    )(jnp.asarray(idx.astype(np.int32)), table)


def bias_tiles(buckets, table, heads_on_lanes=False, scale=1.0):
    nt, r, w = buckets.shape
    if heads_on_lanes:
        out_spec = pl.BlockSpec((1, r, w), lambda t, h: (t, 0, h))
        out_shape = (nt, r, N_HEADS * w)
    else:
        out_spec = pl.BlockSpec((1, r, w), lambda t, h: (t, h, 0))
        out_shape = (nt, N_HEADS * r, w)
    return pl.pallas_call(
        functools.partial(_bias_kernel, scale=scale),
        grid=(nt, N_HEADS),
        in_specs=[
            pl.BlockSpec((1, r, w), lambda t, h: (t, 0, 0)),
            pl.BlockSpec(memory_space=pltpu.SMEM),
        ],
        out_specs=out_spec,
        out_shape=jax.ShapeDtypeStruct(out_shape, F32),
        compiler_params=_params(("parallel", "parallel")),
        name="bias_tiles",
    )(jnp.asarray(buckets), table)


def _prompt_attn_buckets(tq):
    assert WINDOW % tq == 0 and tq >= MAX_DIST
    j = np.arange(tq)[:, None]
    i = np.arange(tq)[None, :]
    far = np.full((tq, tq), N_BUCKETS - 1, np.int32)
    return np.stack([
        _bucket_np(i - j),
        _bucket_np(i - j + tq),
        far,
        np.where(j > i, far, -1),
    ]).astype(np.int32)


def _norm64(blk, gain, lo):
    sq = blk * blk
    s_lo = jnp.sum(jnp.where(lo, sq, 0.0), axis=-1, keepdims=True)
    s_hi = jnp.sum(jnp.where(lo, 0.0, sq), axis=-1, keepdims=True)
    ms = jnp.where(lo, s_lo, s_hi) * (1.0 / DIFF_DH)
    return blk * lax.rsqrt(ms + EPS) * gain


def _diff_post_kernel(x_ref, qg_ref, kg_ref, q_ref, k_ref, v_ref):
    tm = x_ref.shape[0]
    lo = lax.broadcasted_iota(jnp.int32, (tm, LANE), 1) < DIFF_DH
    nq = N_HEADS * HEAD_W
    nk = N_KV * HEAD_W
    for c in range(N_HEADS):
        sl = slice(c * LANE, (c + 1) * LANE)
        q_ref[:, sl] = _norm64(x_ref[:, sl], qg_ref[...], lo)
    for c in range(N_KV):
        sl = slice(c * LANE, (c + 1) * LANE)
        k_ref[:, sl] = _norm64(x_ref[:, nq + c * LANE: nq + (c + 1) * LANE], kg_ref[...], lo)
    v_ref[...] = x_ref[:, nq + nk:]


def diff_post(qkv, q_norm, k_norm):
    m, n = qkv.shape
    tm = min(m, 256)
    nq, nk = N_HEADS * HEAD_W, N_KV * HEAD_W
    return pl.pallas_call(
        _diff_post_kernel,
        grid=(m // tm,),
        in_specs=[
            pl.BlockSpec((tm, n), lambda i: (i, 0)),
            pl.BlockSpec((1, LANE), lambda i: (0, 0)),
            pl.BlockSpec((1, LANE), lambda i: (0, 0)),
        ],
        out_specs=[
            pl.BlockSpec((tm, nq), lambda i: (i, 0)),
            pl.BlockSpec((tm, nk), lambda i: (i, 0)),
            pl.BlockSpec((tm, nk), lambda i: (i, 0)),
        ],
        out_shape=[jax.ShapeDtypeStruct((m, nq), F32), jax.ShapeDtypeStruct((m, nk), F32),
                   jax.ShapeDtypeStruct((m, nk), F32)],
        compiler_params=_params(("parallel",)),
        name="diff_post",
    )(qkv, jnp.tile(q_norm, 2).reshape(1, LANE), jnp.tile(k_norm, 2).reshape(1, LANE))


def _online_update(s, v, m_ref, l_ref, acc_ref):
    m_old = m_ref[...]
    m_new = jnp.maximum(m_old, jnp.max(s, axis=-1, keepdims=True))
    alpha = jnp.exp(m_old - m_new)
    p = jnp.exp(s - m_new)
    l_ref[...] = alpha * l_ref[...] + jnp.sum(p, axis=-1, keepdims=True)
    acc_ref[...] = alpha * acc_ref[...] + _dot(p.astype(BF16), v)
    m_ref[...] = m_new


def _diff_lambda(lam_ref, lam_init):
    lf = lam_ref[...]
    s01 = jnp.sum(lf[0:1] * lf[1:2], axis=-1, keepdims=True)
    s23 = jnp.sum(lf[2:3] * lf[3:4], axis=-1, keepdims=True)
    return jnp.exp(s01) - jnp.exp(s23) + lam_init


def _split_maps(qs):
    lo = lax.broadcasted_iota(jnp.int32, qs.shape, 1) < DIFF_DH
    return jnp.concatenate([jnp.where(lo, qs, 0.0), jnp.where(lo, 0.0, qs)], axis=0)


def _diff_finish_rows(o, lam, sub_norm, lam_init):
    r = o.shape[0] // 2
    a = o[:r] - lam * o[r:]
    return _rms(a, sub_norm) * (1.0 - lam_init)


def _diff_finish(acc, l, lam, sub_norm, lam_init):
    return _diff_finish_rows(acc / l, lam, sub_norm, lam_init)


PROMPT_TQ = 256


LOG2E = math.log2(math.e)
FLASH_CHUNKS = 4
FAR_TILES = 2


def _flash_tile_t(k_tile, vt_tile, qs_ref, m_ref, l_ref, acc_ref, scale=None, bias=None, far_bias=None,
                  mask=None, n_chunks=FLASH_CHUNKS):
    chunk = qs_ref.shape[0] // n_chunks
    logits = [_dot_t(k_tile, qs_ref[c * chunk:(c + 1) * chunk, :]) for c in range(n_chunks)]
    for c in range(n_chunks):
        cols = slice(c * chunk, (c + 1) * chunk)
        s = logits[c]
        if scale is not None:
            s = s * scale
        if bias is not None:
            s = s + bias(cols)
        if mask is not None:
            s = jnp.where(mask[:, cols] > 0.5, s, NEG_INF)
        m_old = m_ref[:, cols]
        s_max = jnp.max(s, axis=0, keepdims=True)
        if far_bias is None:
            m_new = jnp.maximum(m_old, s_max)
            p = jnp.exp2(s - m_new)
        else:
            fb = far_bias[:, cols]
            m_new = jnp.maximum(m_old, s_max + fb)
            p = jnp.exp2(s - (m_new - fb))
        alpha = jnp.exp2(m_old - m_new)
        l_ref[:, cols] = alpha * l_ref[:, cols] + jnp.sum(p, axis=0, keepdims=True)
        acc_ref[:, cols] = alpha * acc_ref[:, cols] + _dot(vt_tile, p.astype(BF16))
        m_ref[:, cols] = m_new


def _stage_kv(k_ref, v_ref, kb_ref, vt_ref, chunk):
    for c in range(k_ref.shape[1] // chunk):
        rows = slice(c * chunk, (c + 1) * chunk)
        kb_ref[rows, :] = k_ref[0, rows, :].astype(BF16)
        vt_ref[:, rows] = v_ref[0, rows, :].T.astype(BF16)


def _diff_flash_kernel(lam_ref, q_ref, k_ref, v_ref, bias_ref, sn_ref, o_ref,
                       qs_ref, kb_ref, vt_ref, m_ref, l_ref, acc_ref, *, tq, lam_init):
    qi = pl.program_id(2)

    @pl.when(qi == 0)
    def _():
        _stage_kv(k_ref, v_ref, kb_ref, vt_ref, tq)

    qb = q_ref[0]
    qs = jnp.concatenate([qb[:, r * HEAD_W:(r + 1) * HEAD_W] for r in range(HPG)], axis=0)
    qs_ref[...] = _split_maps(qs * (DIFF_DH ** -0.5 * LOG2E)).astype(BF16)
    _reset_flash(m_ref, l_ref, acc_ref)
    far = bias_ref[2, 0:1, :]
    far = jnp.concatenate([far, far], axis=1)

    def far_body(width):
        def body(kt, c):
            rows = pl.ds(pl.multiple_of(kt * width, width), width)
            _flash_tile_t(kb_ref[rows, :], vt_ref[:, rows], qs_ref, m_ref, l_ref, acc_ref, far_bias=far)
            return c
        return body

    def near_body(kt, c):
        rows = pl.ds(pl.multiple_of(kt * tq, tq), tq)
        b = bias_ref[qi - kt]
        _flash_tile_t(kb_ref[rows, :], vt_ref[:, rows], qs_ref, m_ref, l_ref, acc_ref,
                      bias=lambda cols: jnp.concatenate([b, b], axis=1), n_chunks=1)
        return c

    n_far = jnp.maximum(qi - 1, 0)
    lax.fori_loop(0, n_far // FAR_TILES, far_body(FAR_TILES * tq), 0)
    lax.fori_loop((n_far // FAR_TILES) * FAR_TILES, n_far, far_body(tq), 0)
    lax.fori_loop(n_far, qi + 1, near_body, 0)
    o = (acc_ref[...] / l_ref[...]).T
    a = _diff_finish_rows(o, _diff_lambda(lam_ref, lam_init), sn_ref[...], lam_init)
    for r in range(HPG):
        o_ref[0, :, r * HEAD_W:(r + 1) * HEAD_W] = a[r * tq:(r + 1) * tq]


def diff_flash(q, k, v, bias, lam_p, sub_norm, lam_init, tq=PROMPT_TQ):
    b, t, _ = q.shape
    gw = HPG * HEAD_W
    rows = 2 * HPG * tq
    return pl.pallas_call(
        functools.partial(_diff_flash_kernel, tq=tq, lam_init=lam_init),
        grid=(b, N_KV, t // tq),
        in_specs=[
            pl.BlockSpec((4, DIFF_DH), lambda i, g, j: (0, 0)),
            pl.BlockSpec((1, tq, gw), lambda i, g, j: (i, j, g)),
            pl.BlockSpec((1, t, HEAD_W), lambda i, g, j: (i, 0, g)),
            pl.BlockSpec((1, t, HEAD_W), lambda i, g, j: (i, 0, g)),
            pl.BlockSpec((4, tq, HPG * tq), lambda i, g, j: (0, 0, g)),
            pl.BlockSpec((1, HEAD_W), lambda i, g, j: (0, 0)),
        ],
        out_specs=pl.BlockSpec((1, tq, gw), lambda i, g, j: (i, j, g)),
        out_shape=jax.ShapeDtypeStruct(q.shape, F32),
        scratch_shapes=[
            pltpu.VMEM((rows, HEAD_W), BF16),
            pltpu.VMEM((t, HEAD_W), BF16),
            pltpu.VMEM((HEAD_W, t), BF16),
            pltpu.VMEM((1, rows), F32),
            pltpu.VMEM((1, rows), F32),
            pltpu.VMEM((HEAD_W, rows), F32),
        ],
        compiler_params=_params(("parallel", "parallel", "arbitrary")),
        name="diff_flash",
    )(lam_p, q, k, v, bias, sub_norm.reshape(1, HEAD_W))


PAGES_PER_STEP = 4


def _decode_page_index(t_new, past_len):
    j = np.arange(PAGE)[:, None]
    i = np.arange(t_new)[None, :]
    far = np.full((PAGE, t_new), N_BUCKETS - 1, np.int32)
    new = np.where(j < t_new, _bucket_np(i - j), -1)
    g = np.arange(N_KV)[None, :, None, None]
    h = np.arange(N_HEADS)[None, None, :, None]
    tiles = []
    for bkt in (far, _bucket_np(PAGE + i - j), new):
        b = bkt[:, None, None, :]
        idx = np.where((b >= 0) & (h // HPG == g), b * N_HEADS + h, -1)
        tiles.append(idx.reshape(PAGE * N_KV, N_HEADS * t_new))
    return np.stack(tiles).astype(np.int32)


def _stack_group_heads(q_ref, g):
    return jnp.concatenate(
        [q_ref[0, :, (g * HPG + r) * HEAD_W:(g * HPG + r + 1) * HEAD_W] for r in range(HPG)], axis=0)


def _query_rows(q_ref, maps, scale):
    pieces = []
    for m in range(maps):
        for h in range(N_HEADS):
            qh = q_ref[0, :, h * HEAD_W:(h + 1) * HEAD_W] * scale
            if maps == 2:
                lo = lax.broadcasted_iota(jnp.int32, qh.shape, 1) < DIFF_DH
                qh = jnp.where(lo, qh, 0.0) if m == 0 else jnp.where(lo, 0.0, qh)
            pieces.append(qh)
    return jnp.concatenate(pieces, axis=0).astype(BF16)


def _page_bias_t(bias_ref, first_page, n_pages, reps):
    tiles = [bias_ref[jnp.where(first_page + u == n_pages - 1, 1, 0)] for u in range(PAGES_PER_STEP)]
    b = jnp.concatenate(tiles, axis=0)
    return jnp.concatenate([b] * reps, axis=1)


def _decode_step_t(k_rows, v_rows, q_ref, bias, m_ref, l_ref, acc_ref, keep=None):
    s = _dot_t(k_rows.astype(BF16), q_ref[...]) + bias
    if keep is not None:
        s = jnp.where(keep > 0.5, s, NEG_INF)
    m_old = m_ref[...]
    m_new = jnp.maximum(m_old, jnp.max(s, axis=0, keepdims=True))
    alpha = jnp.exp(m_old - m_new)
    p = jnp.exp(s - m_new)
    l_ref[...] = alpha * l_ref[...] + jnp.sum(p, axis=0, keepdims=True)
    acc_ref[...] = alpha * acc_ref[...] + _dot(v_rows.T.astype(BF16), p.astype(BF16))
    m_ref[...] = m_new


def _page_rows(refs):
    return jnp.concatenate([r[0, 0] for r in refs], axis=0)


def _diff_decode_kernel(pt_ref, lam_ref, q_ref, kn_ref, vn_ref, *rest, n_pages, lam_init):
    del pt_ref
    kp = rest[:PAGES_PER_STEP]
    vp = rest[PAGES_PER_STEP:2 * PAGES_PER_STEP]
    bias_ref, sn_ref, o_ref, qr_ref, m_ref, l_ref, acc_ref = rest[2 * PAGES_PER_STEP:]
    s = pl.program_id(1)
    t_new = q_ref.shape[1]

    @pl.when(s == 0)
    def _():
        _reset_flash(m_ref, l_ref, acc_ref)
        qr_ref[...] = _query_rows(q_ref, 2, DIFF_DH ** -0.5)
        b = bias_ref[2]
        _decode_step_t(kn_ref[0], vn_ref[0], qr_ref, jnp.concatenate([b, b], axis=1), m_ref, l_ref, acc_ref)

    _decode_step_t(_page_rows(kp), _page_rows(vp), qr_ref,
                   _page_bias_t(bias_ref, s * PAGES_PER_STEP, n_pages, 2), m_ref, l_ref, acc_ref)

    @pl.when(s == pl.num_programs(1) - 1)
    def _():
        o = (acc_ref[...] / l_ref[...]).T
        a = _diff_finish_rows(o, _diff_lambda(lam_ref, lam_init), sn_ref[...], lam_init)
        for h in range(N_HEADS):
            o_ref[0, :, h * HEAD_W:(h + 1) * HEAD_W] = a[h * t_new:(h + 1) * t_new]


PAGE_ROWS = PAGE * N_KV


def _as_page_rows(cache):
    return cache.reshape(*cache.shape[:2], PAGE_ROWS, HEAD_W)


def _page_specs(layer, n):
    def spec(u):
        return pl.BlockSpec((1, 1, PAGE_ROWS, HEAD_W),
                            lambda i, s, pt: (layer, pt[i, s * PAGES_PER_STEP + u], 0, 0))
    return [spec(u) for u in range(PAGES_PER_STEP)] * n


def diff_decode(q, k_new, v_new, cache_k, cache_v, layer, page_table, bias, lam_p, sub_norm, lam_init):
    b, t_new, _ = q.shape
    n_pages = page_table.shape[1]
    lanes = 2 * N_HEADS * t_new
    assert N_HEADS * t_new == LANE
    fixed = lambda *shape: pl.BlockSpec(shape, lambda i, s, pt: (0,) * len(shape))
    per_b = lambda *shape: pl.BlockSpec((1,) + shape, lambda i, s, pt: (i,) + (0,) * len(shape))
    grid_spec = pltpu.PrefetchScalarGridSpec(
        num_scalar_prefetch=1,
        grid=(b, n_pages // PAGES_PER_STEP),
        in_specs=[fixed(4, DIFF_DH), per_b(t_new, N_HEADS * HEAD_W), per_b(PAGE_ROWS, HEAD_W),
                  per_b(PAGE_ROWS, HEAD_W)]
        + _page_specs(layer, 2)
        + [fixed(*bias.shape), fixed(1, HEAD_W)],
        out_specs=per_b(t_new, N_HEADS * HEAD_W),
        scratch_shapes=[
            pltpu.VMEM((lanes, HEAD_W), BF16),
            pltpu.VMEM((1, lanes), F32),
            pltpu.VMEM((1, lanes), F32),
            pltpu.VMEM((HEAD_W, lanes), F32),
        ],
    )
    return pl.pallas_call(
        functools.partial(_diff_decode_kernel, n_pages=n_pages, lam_init=lam_init),
        grid_spec=grid_spec,
        out_shape=jax.ShapeDtypeStruct(q.shape, F32),
        compiler_params=_params(("parallel", "arbitrary")),
        name="diff_decode",
    )(page_table, lam_p, q, k_new, v_new, *([cache_k] * PAGES_PER_STEP), *([cache_v] * PAGES_PER_STEP),
      bias, sub_norm.reshape(1, HEAD_W))


def _pad_rows(a, n):
    return jnp.pad(a, ((0, 0), (0, n - a.shape[1]), (0, 0)))


def _new_page_rows(a):
    return _pad_rows(a, PAGE).reshape(a.shape[0], PAGE_ROWS, HEAD_W)


def diff_layer(xp, xs, gain, w_in, q_norm, k_norm, lam_p, sub_norm, w_out, cache_k, cache_v, layer,
               page_table, bias_p, bias_s, lam_init):
    outs = []
    for x, paged in ((xp, False), (xs, True)):
        b, t, d = x.shape
        x2 = x.reshape(b * t, d)
        q, k, v = diff_post(dense(x2, w_in, layer, gain=gain), q_norm, k_norm)
        q3, k3, v3 = (a.reshape(b, t, -1) for a in (q, k, v))
        if paged:
            o = diff_decode(q3, _new_page_rows(k3), _new_page_rows(v3), _as_page_rows(cache_k),
                            _as_page_rows(cache_v), layer, page_table, bias_s, lam_p, sub_norm, lam_init)
        else:
            o = diff_flash(q3, k3, v3, bias_p, lam_p, sub_norm, lam_init)
        y = dense(o.reshape(b * t, -1), w_out, layer, res=x2).reshape(b, t, d)
        outs.append((y, k3.reshape(b, t, N_KV, HEAD_W), v3.reshape(b, t, N_KV, HEAD_W)))
    (yp, kp, vp), (ys, ks, vs) = outs
    return yp, ys, kp, vp, ks, vs


SSM_GW = SSM_HPG * SSM_HEADDIM
SSM_BC = 2 * SSM_GROUPS * D_STATE
CONV_PAD = 8


def _conv_silu(buf_ref, w_ref, b_ref, n):
    acc = b_ref[...]
    for k in range(CONV_W):
        acc = acc + buf_ref[pl.ds(CONV_PAD - (CONV_W - 1) + k, n), :] * w_ref[k:k + 1, :]
    return acc * jax.nn.sigmoid(acc)


def _ssd_kernel(z_ref, x_ref, bc_ref, dt_ref, cbx_ref, cbbc_ref, wx_ref, wbc_ref, bx_ref, bbc_ref,
                dtb_ref, alog_ref, dsk_ref, ng_ref, e_ref, tri_ref, h0_ref,
                y_ref, hout_ref, ht_ref, xbuf_ref, bcbuf_ref, xa_ref, bca_ref, *, t_valid):
    c = pl.program_id(1)
    n = x_ref.shape[1]

    @pl.when(c == 0)
    def _():
        for g in range(SSM_GROUPS):
            ht_ref[g] = h0_ref[0, g].T
        xbuf_ref[0:CONV_PAD] = cbx_ref[0]
        bcbuf_ref[0:CONV_PAD] = cbbc_ref[0]

    xbuf_ref[CONV_PAD:CONV_PAD + n] = x_ref[0]
    bcbuf_ref[CONV_PAD:CONV_PAD + n] = bc_ref[0]
    xa_ref[...] = _conv_silu(xbuf_ref, wx_ref, bx_ref, n)
    bca_ref[...] = _conv_silu(bcbuf_ref, wbc_ref, bbc_ref, n)
    xbuf_ref[0:CONV_PAD] = xbuf_ref[n:n + CONV_PAD]
    bcbuf_ref[0:CONV_PAD] = bcbuf_ref[n:n + CONV_PAD]

    row = lax.broadcasted_iota(jnp.int32, (n, LANE), 0) + c * n
    dtr = dt_ref[0] + dtb_ref[...]
    dt = jnp.maximum(dtr, 0.0) + jnp.log1p(jnp.exp(-jnp.abs(dtr)))
    dt = jnp.where(row < t_valid, dt, 0.0)
    dta = dt * (-jnp.exp(alog_ref[...]))
    hi, mid, lo = _split3(dta)
    tri = tri_ref[...]
    cs = _dot(tri, hi) + _dot(tri, mid) + _dot(tri, lo)
    cs_last = cs[n - 1:n, :]
    cs_t = cs.T
    dt_t = dt.T
    stacked = jnp.concatenate(
        [jnp.exp(cs), jnp.exp(cs_last - cs) * dt, jnp.broadcast_to(jnp.exp(cs_last), (8, LANE))], axis=0)
    ex = _dot_exact_rhs(stacked, e_ref[...])
    causal = (lax.broadcasted_iota(jnp.int32, (n, n), 0) >= lax.broadcasted_iota(jnp.int32, (n, n), 1))

    for g in range(SSM_GROUPS):
        gs = slice(g * SSM_GW, (g + 1) * SSM_GW)
        bm = bca_ref[:, g * D_STATE:(g + 1) * D_STATE]
        cm = bca_ref[:, (SSM_GROUPS + g) * D_STATE:(SSM_GROUPS + g + 1) * D_STATE].astype(BF16)
        cb = _dot_t(cm, bm.astype(BF16))
        xg = xa_ref[:, gs]
        ys = []
        for r in range(SSM_HPG):
            h = g * SSM_HPG + r
            seg = cs[:, h:h + 1] - cs_t[h:h + 1, :]
            dec = jnp.where(causal, jnp.exp(jnp.where(causal, seg, 0.0)), 0.0)
            mm = (cb * dec * dt_t[h:h + 1, :]).astype(BF16)
            ys.append(_dot(mm, xg[:, r * SSM_HEADDIM:(r + 1) * SSM_HEADDIM].astype(BF16)))
        ht = ht_ref[g]
        y = jnp.concatenate(ys, axis=1) + _dot(cm, ht.astype(BF16)) * ex[0:n, gs]
        y = y + dsk_ref[:, gs] * xg
        zg = z_ref[0, :, gs]
        y = y * (zg * jax.nn.sigmoid(zg))
        y_ref[0, :, gs] = _rms(y, ng_ref[:, gs])
        xw = (xg * ex[n:2 * n, gs]).astype(BF16)
        ht_ref[g] = ht * ex[2 * n:2 * n + 1, gs] + _dot(bm.T.astype(BF16), xw)

    @pl.when(c == pl.num_programs(1) - 1)
    def _():
        for g in range(SSM_GROUPS):
            hout_ref[0, g] = ht_ref[g].T


def _head_expand_matrix():
    e = np.zeros((LANE, D_INNER), np.float32)
    for h in range(SSM_HEADS):
        e[h, h * SSM_HEADDIM:(h + 1) * SSM_HEADDIM] = 1.0
    return e


def ssd_core(zx, dt_raw, conv_buf, h0, conv_w, conv_b, dt_bias, a_log, d_skip, norm_g, t_valid):
    b, t, _ = zx.shape
    n = SSM_CHUNK
    pad_h = LANE - SSM_HEADS
    cb = jnp.pad(conv_buf, ((0, 0), (CONV_PAD - (CONV_W - 1), 0), (0, 0)))
    tri = jnp.asarray(np.tril(np.ones((n, n), np.float32)), BF16)
    e = jnp.asarray(_head_expand_matrix(), BF16)
    d_exp = jnp.repeat(d_skip, SSM_HEADDIM).reshape(1, D_INNER)
    fixed = lambda *shape: pl.BlockSpec(shape, lambda i, c: (0,) * len(shape))
    per_b = lambda *shape: pl.BlockSpec((1,) + shape, lambda i, c: (i,) + (0,) * len(shape))
    y, h_last = pl.pallas_call(
        functools.partial(_ssd_kernel, t_valid=t_valid),
        grid=(b, t // n),
        in_specs=[
            pl.BlockSpec((1, n, D_INNER), lambda i, c: (i, c, 0)),
            pl.BlockSpec((1, n, D_INNER), lambda i, c: (i, c, 1)),
            pl.BlockSpec((1, n, SSM_BC), lambda i, c: (i, c, 2 * D_INNER // SSM_BC)),
            pl.BlockSpec((1, n, LANE), lambda i, c: (i, c, 0)),
            per_b(CONV_PAD, D_INNER), per_b(CONV_PAD, SSM_BC),
            fixed(CONV_W, D_INNER), fixed(CONV_W, SSM_BC), fixed(1, D_INNER), fixed(1, SSM_BC),
            fixed(1, LANE), fixed(1, LANE), fixed(1, D_INNER), fixed(1, D_INNER),
            fixed(LANE, D_INNER), fixed(n, n),
            per_b(SSM_GROUPS, SSM_GW, D_STATE),
        ],
        out_specs=[
            pl.BlockSpec((1, n, D_INNER), lambda i, c: (i, c, 0)),
            per_b(SSM_GROUPS, SSM_GW, D_STATE),
        ],
        out_shape=[jax.ShapeDtypeStruct((b, t, D_INNER), F32),
                   jax.ShapeDtypeStruct((b, SSM_GROUPS, SSM_GW, D_STATE), F32)],
        scratch_shapes=[
            pltpu.VMEM((SSM_GROUPS, D_STATE, SSM_GW), F32),
            pltpu.VMEM((n + CONV_PAD, D_INNER), F32),
            pltpu.VMEM((n + CONV_PAD, SSM_BC), F32),
            pltpu.VMEM((n, D_INNER), F32),
            pltpu.VMEM((n, SSM_BC), F32),
        ],
        compiler_params=_params(("parallel", "arbitrary")),
        name="ssd_core",
    )(zx, zx, zx, dt_raw, cb[:, :, :D_INNER], cb[:, :, D_INNER:],
      conv_w[:, :D_INNER], conv_w[:, D_INNER:], conv_b[:D_INNER].reshape(1, -1), conv_b[D_INNER:].reshape(1, -1),
      jnp.pad(dt_bias, (0, pad_h)).reshape(1, LANE), jnp.pad(a_log, (0, pad_h)).reshape(1, LANE),
      d_exp, norm_g.reshape(1, D_INNER), e, tri,
      h0.reshape(b, SSM_GROUPS, SSM_GW, D_STATE))
    return y, h_last.reshape(b, SSM_HEADS, SSM_HEADDIM, D_STATE)


def ssd_layer(xp, xs, gain, w_in, layer, conv_w, conv_b, dt_bias, a_log, d_skip, norm_g, w_out,
              state_conv, state_ssm):
    nzx = D_INNER + CONV_DIM
    w_dt = jnp.pad(w_in[layer, :, nzx:], ((0, 0), (0, LANE - SSM_HEADS)))[None]
    outs = []
    for x, conv_buf, h0 in ((xp, None, None), (xs, state_conv, state_ssm)):
        b, t, d = x.shape
        x2 = x.reshape(b * t, d)
        if conv_buf is None:
            conv_buf = jnp.zeros((b, CONV_W - 1, CONV_DIM), F32)
            h0 = jnp.zeros((b, SSM_HEADS, SSM_HEADDIM, D_STATE), F32)
        zx = dense(x2, w_in, layer, gain=gain, n=nzx).reshape(b, t, -1)
        dt_raw = dense(x2, w_dt, 0, gain=gain).reshape(b, t, LANE)
        tp = -(-t // SSM_CHUNK) * SSM_CHUNK
        y, h_last = ssd_core(_pad_rows(zx, tp), _pad_rows(dt_raw, tp), conv_buf, h0, conv_w, conv_b,
                             dt_bias, a_log, d_skip, norm_g, t)
        y = y[:, :t].reshape(b * t, D_INNER)
        tail = min(t, CONV_W - 1)
        conv_out = jnp.concatenate([conv_buf[:, tail:], zx[:, t - tail:, D_INNER:]], axis=1)
        outs.append((dense(y, w_out, layer, res=x2).reshape(b, t, d), h_last, conv_out))
    (yp, hp, cp), (ys, hs, cs) = outs
    return yp, ys, hp, cp, hs, cs


NSA_KV = N_KV * HEAD_W
NSA_SCALE = HEAD_W ** -0.5
CMP_PAIRS = CMP_STRIDE // 2
CMP_PAGES = 16


def _nsa_post_kernel(x_ref, qg_ref, kg_ref, q_ref, kcr_ref, vcr_ref, ks_ref, vs_ref, kw_ref, vw_ref):
    nq = N_HEADS * HEAD_W
    for c in range(N_HEADS):
        sl = slice(c * LANE, (c + 1) * LANE)
        q_ref[:, sl] = _rms(x_ref[:, sl], qg_ref[...])
    outs = (kcr_ref, vcr_ref, ks_ref, vs_ref, kw_ref, vw_ref)
    gains = (None, None, 1, None, 2, None)
    for a, (o_ref, gi) in enumerate(zip(outs, gains)):
        for c in range(N_KV):
            col = nq + a * NSA_KV + c * LANE
            blk = x_ref[:, col:col + LANE]
            o_ref[:, c * LANE:(c + 1) * LANE] = blk if gi is None else _rms(blk, kg_ref[gi:gi + 1, :])


def nsa_post(proj, q_norm, k_norm):
    m, n = proj.shape
    tm = min(m, 256)
    nq = N_HEADS * HEAD_W
    kv_spec = pl.BlockSpec((tm, NSA_KV), lambda i: (i, 0))
    kv_shape = jax.ShapeDtypeStruct((m, NSA_KV), F32)
    return pl.pallas_call(
        _nsa_post_kernel,
        grid=(m // tm,),
        in_specs=[
            pl.BlockSpec((tm, n), lambda i: (i, 0)),
            pl.BlockSpec((1, LANE), lambda i: (0, 0)),
            pl.BlockSpec((3, LANE), lambda i: (0, 0)),
        ],
        out_specs=[pl.BlockSpec((tm, nq), lambda i: (i, 0))] + [kv_spec] * 6,
        out_shape=[jax.ShapeDtypeStruct((m, nq), F32)] + [kv_shape] * 6,
        compiler_params=_params(("parallel",)),
        name="nsa_post",
    )(proj, q_norm.reshape(1, LANE), k_norm)


def _cmp_uv_kernel(pt_ref, *refs):
    del pt_ref
    pages = refs[:CMP_PAGES]
    wab_ref, o_ref = refs[CMP_PAGES:]
    subs = PAGE // CMP_STRIDE
    for g in range(N_KV):
        acc = jnp.zeros((CMP_PAGES * subs, 2 * HEAD_W), F32)
        for lp in range(CMP_PAIRS):
            halves = []
            for li in range(2):
                rows = pl.ds((2 * lp + li) * N_KV + g, subs, stride=CMP_STRIDE * N_KV)
                halves.append(jnp.concatenate([p[0, 0, rows, :] for p in pages], axis=0))
            acc = acc + _dot(jnp.concatenate(halves, axis=1).astype(BF16), wab_ref[lp])
        o_ref[0, :, g * 2 * HEAD_W:(g + 1) * 2 * HEAD_W] = acc


def cmp_uv(rows, layer, page_table, wab):
    b, n_pages = page_table.shape
    subs = PAGE // CMP_STRIDE

    def spec(u):
        return pl.BlockSpec((1, 1, PAGE_ROWS, HEAD_W),
                            lambda i, c, pt: (layer, pt[i, c * CMP_PAGES + u], 0, 0))

    grid_spec = pltpu.PrefetchScalarGridSpec(
        num_scalar_prefetch=1,
        grid=(b, n_pages // CMP_PAGES),
        in_specs=[spec(u) for u in range(CMP_PAGES)]
        + [pl.BlockSpec(wab.shape, lambda i, c, pt: (0, 0, 0))],
        out_specs=pl.BlockSpec((1, CMP_PAGES * subs, 2 * NSA_KV), lambda i, c, pt: (i, c, 0)),
    )
    return pl.pallas_call(
        _cmp_uv_kernel,
        grid_spec=grid_spec,
        out_shape=jax.ShapeDtypeStruct((b, n_pages * subs, 2 * NSA_KV), F32),
        compiler_params=_params(("parallel", "arbitrary")),
        name="cmp_uv",
    )(page_table, *([rows] * CMP_PAGES), wab)


def _cmp_finish_kernel(uv_ref, pe_ref, w1_ref, w2_ref, kn_ref, o_ref, *, norm):
    n = uv_ref.shape[1]
    c = _dot(jnp.broadcast_to(pe_ref[...], (8, pe_ref.shape[1])).astype(BF16), w1_ref[...])[0:1]
    for g in range(N_KV):
        u = uv_ref[0, :, g * 2 * HEAD_W:g * 2 * HEAD_W + HEAD_W]
        v = uv_ref[0, :, g * 2 * HEAD_W + HEAD_W:(g + 1) * 2 * HEAD_W]
        pre = u + pltpu.roll(v, n - 1, axis=0) + c
        out = _dot((pre * jax.nn.sigmoid(pre)).astype(BF16), w2_ref[...])
        if norm:
            out = _rms(out, kn_ref[...])
        o_ref[0, :, g * HEAD_W:(g + 1) * HEAD_W] = out


def cmp_finish(uv, pe, w1, w2, k_norm):
    b, n, _ = uv.shape
    norm = k_norm is not None
    kn = (k_norm if norm else jnp.ones((HEAD_W,), F32)).reshape(1, HEAD_W)
    return pl.pallas_call(
        functools.partial(_cmp_finish_kernel, norm=norm),
        grid=(b,),
        in_specs=[
            pl.BlockSpec((1, n, 2 * NSA_KV), lambda i: (i, 0, 0)),
            pl.BlockSpec((1, pe.size), lambda i: (0, 0)),
            pl.BlockSpec(w1.shape, lambda i: (0, 0)),
            pl.BlockSpec(w2.shape, lambda i: (0, 0)),
            pl.BlockSpec((1, HEAD_W), lambda i: (0, 0)),
        ],
        out_specs=pl.BlockSpec((1, n, NSA_KV), lambda i: (i, 0, 0)),
        out_shape=jax.ShapeDtypeStruct((b, n, NSA_KV), F32),
        compiler_params=_params(("parallel",)),
        name="cmp_finish",
    )(uv, pe.reshape(1, -1), w1, w2, kn)


def compress(rows, layer, page_table, pe, w1, w2, k_norm):
    w1r = w1.reshape(2, CMP_STRIDE, HEAD_W, HEAD_W)
    wab = jnp.transpose(w1r, (1, 2, 0, 3)).reshape(CMP_PAIRS, 2 * HEAD_W, 2 * HEAD_W).astype(BF16)
    uv = cmp_uv(rows, layer, page_table, wab)
    return cmp_finish(uv, pe, w1.astype(BF16), w2.astype(BF16), k_norm)


def _importance_matrix(n_cmp, n_rows, n_slc, n_cols):
    w = np.zeros((n_rows, n_cols), np.float32)
    w_imp = [1.0] + [2.0] * (SLC_RATIO - 1) + [1.0]
    for s in range(n_slc):
        for m, wm in enumerate(w_imp):
            j = SLC_RATIO * s + m - 1
            if 0 <= j < n_cmp:
                w[j, s] += wm
    return w


def _select_blocks(s_slc, q_pos0, n_slc):
    t, w = s_slc.shape
    blk = lax.broadcasted_iota(jnp.int32, (t, w), 1)
    qpos = q_pos0 + lax.broadcasted_iota(jnp.int32, (t, w), 0)
    qb = qpos // SLC_BLOCK
    forced = (blk == 0) | (blk == qb) | (blk == qb - 1)
    score = jnp.where(forced, FORCE_SCORE, jnp.where(blk * SLC_BLOCK <= qpos, s_slc, -1.0))
    score = jnp.where(blk < n_slc, score, -2.0)
    cnt = jnp.zeros((t, w), F32)
    for sp in range(n_slc):
        col = score[:, sp:sp + 1]
        tie = jnp.where(blk > sp, 1.0, 0.0)
        cnt = cnt + jnp.where(col > score, 1.0, jnp.where(col == score, tie, 0.0))
    return jnp.where(cnt < N_SELECT, 1.0, 0.0)


def _masked_softmax(s, valid, axis=-1, base2=False):
    m = jnp.max(s, axis=axis, keepdims=True)
    e = jnp.where(valid, (jnp.exp2 if base2 else jnp.exp)(s - m), 0.0)
    return e / jnp.maximum(jnp.sum(e, axis=axis, keepdims=True), 1e-30)


def _reset_flash(m_ref, l_ref, acc_ref):
    m_ref[...] = jnp.full(m_ref.shape, NEG_INF, F32)
    l_ref[...] = jnp.zeros(l_ref.shape, F32)
    acc_ref[...] = jnp.zeros(acc_ref.shape, F32)


def _block_mask(sel, first_block, n_keys, reps):
    w = sel.shape[1]
    er = lax.broadcasted_iota(jnp.int32, (w, n_keys), 0)
    ec = lax.broadcasted_iota(jnp.int32, (w, n_keys), 1)
    e = jnp.where(er == first_block + ec // SLC_BLOCK, 1.0, 0.0).astype(BF16)
    mt = _dot(sel, e)
    return jnp.concatenate([mt] * reps, axis=0)


def _select_blocks_t(s_slc, q_pos0, n_slc):
    w, t = s_slc.shape
    n8 = -(-n_slc // 8) * 8
    blk = lax.broadcasted_iota(jnp.int32, (n8, t), 0)
    qpos = q_pos0 + lax.broadcasted_iota(jnp.int32, (n8, t), 1)
    qb = qpos // SLC_BLOCK
    forced = (blk == 0) | (blk == qb) | (blk == qb - 1)
    score = jnp.where(forced, FORCE_SCORE, jnp.where(blk * SLC_BLOCK <= qpos, s_slc[:n8], -1.0))
    score = jnp.where(blk < n_slc, score, -2.0)
    cnt = jnp.zeros((n8, t), F32)
    for sp in range(n_slc):
        row = score[sp:sp + 1, :]
        tie = jnp.where(blk > sp, 1.0, 0.0)
        cnt = cnt + jnp.where(row > score, 1.0, jnp.where(row == score, tie, 0.0))
    sel = jnp.where(cnt < N_SELECT, 1.0, 0.0)
    if n8 < w:
        sel = jnp.concatenate([sel, jnp.zeros((w - n8, t), F32)], axis=0)
    return sel


def _nsa_prompt_kernel(q_ref, gt_ref, kc_ref, vc_ref, ks_ref, vs_ref, kw_ref, vw_ref, bias_ref, u_ref, wimp_ref,
                       o_ref, qs_ref, ksb_ref, vst_ref, kwb_ref, vwt_ref, sel_ref, m_ref, l_ref, acc_ref,
                       *, tq, n_slc):
    qi = pl.program_id(2)

    @pl.when(qi == 0)
    def _():
        _stage_kv(ks_ref, vs_ref, ksb_ref, vst_ref, tq)
        _stage_kv(kw_ref, vw_ref, kwb_ref, vwt_ref, tq)

    qs_ref[...] = jnp.concatenate(
        [q_ref[0, :, r * HEAD_W:(r + 1) * HEAD_W] for r in range(HPG)], axis=0).astype(BF16)

    n_pad = kc_ref.shape[1]
    start = pl.multiple_of(n_pad - (tq // CMP_STRIDE) * (qi + 1), 8)
    scale = NSA_SCALE * LOG2E
    bias_c = u_ref[0, pl.ds(start, n_pad), :]
    s = _dot_t(kc_ref[0].astype(BF16), qs_ref[...]) * scale + bias_c
    pc = _masked_softmax(s, bias_c > 0.5 * NEG_INF, axis=0, base2=True)
    o_cmp = _dot(vc_ref[0].T.astype(BF16), pc.astype(BF16))
    psum = pc[:, 0:tq] + pc[:, tq:2 * tq] + pc[:, 2 * tq:3 * tq] + pc[:, 3 * tq:4 * tq]
    hi, mid, lo = _split3(psum)
    wimp = wimp_ref[...]
    s_slc = _dot(wimp, hi) + _dot(wimp, mid) + _dot(wimp, lo)
    sel_ref[...] = _select_blocks_t(s_slc, qi * tq, n_slc).astype(BF16)

    _reset_flash(m_ref, l_ref, acc_ref)

    far = bias_ref[2, 0:1, :]

    def key_mask(kt, width):
        key = lax.broadcasted_iota(jnp.int32, (width, LANE), 0)
        blk = lax.broadcasted_iota(jnp.int32, (width, LANE), 1)
        onehot = jnp.where(blk == kt * (width // SLC_BLOCK) + key // SLC_BLOCK, 1.0, 0.0).astype(BF16)
        return jnp.concatenate([_dot(onehot, sel_ref[...])] * HPG, axis=1)

    def slc_far_body(width):
        def body(kt, c):
            rows = pl.ds(pl.multiple_of(kt * width, width), width)
            _flash_tile_t(ksb_ref[rows, :], vst_ref[:, rows], qs_ref, m_ref, l_ref, acc_ref, scale=scale,
                          far_bias=far, mask=key_mask(kt, width))
            return c
        return body

    def slc_near_body(kt, c):
        rows = pl.ds(pl.multiple_of(kt * tq, tq), tq)
        typ = qi - kt
        _flash_tile_t(ksb_ref[rows, :], vst_ref[:, rows], qs_ref, m_ref, l_ref, acc_ref, scale=scale,
                      bias=lambda cols: bias_ref[typ, :, cols], mask=key_mask(kt, tq))
        return c

    n_far = jnp.maximum(qi - 1, 0)
    lax.fori_loop(0, n_far // FAR_TILES, slc_far_body(FAR_TILES * tq), 0)
    lax.fori_loop((n_far // FAR_TILES) * FAR_TILES, n_far, slc_far_body(tq), 0)
    lax.fori_loop(n_far, qi + 1, slc_near_body, 0)
    o_slc = acc_ref[...] / l_ref[...]

    _reset_flash(m_ref, l_ref, acc_ref)
    nw = WINDOW // tq

    def win_body(kt, c):
        rows = pl.ds(pl.multiple_of(kt * tq, tq), tq)
        t = qi - kt
        typ = jnp.where(t == nw, 3, jnp.minimum(t, 2))
        _flash_tile_t(kwb_ref[rows, :], vwt_ref[:, rows], qs_ref, m_ref, l_ref, acc_ref, scale=scale,
                      bias=lambda cols: bias_ref[typ, :, cols])
        return c

    lax.fori_loop(jnp.maximum(qi - nw, 0), qi + 1, win_body, 0)
    o_win = acc_ref[...] / l_ref[...]

    sig = jax.nn.sigmoid(gt_ref[0]).T
    for r in range(HPG):
        cs = slice(r * tq, (r + 1) * tq)
        o_t = (sig[r:r + 1, :] * o_cmp[:, cs] + sig[HPG + r:HPG + r + 1, :] * o_slc[:, cs]
               + sig[2 * HPG + r:2 * HPG + r + 1, :] * o_win[:, cs])
        o_ref[0, :, r * HEAD_W:(r + 1) * HEAD_W] = o_t.T


def _prompt_cmp_buckets(tq, n_pad):
    jp = np.arange(2 * n_pad)[:, None] - (n_pad - tq // CMP_STRIDE)
    i = np.arange(tq)[None, :]
    return _bucket_np(i - CMP_STRIDE * jp - (2 * CMP_STRIDE - 1))[None].astype(np.int32)


def nsa_prompt(q, gates, kc, vc, ks, vs, kw, vw, bias, u_bias, tq=PROMPT_TQ):
    b, t, _ = q.shape
    gw = HPG * HEAD_W
    n_pad = kc.shape[1]
    n_slc = t // SLC_BLOCK
    wimp = jnp.asarray(_importance_matrix(n_pad - 1, n_pad, n_slc, LANE).T, BF16)
    rows = HPG * tq
    seq = lambda n: pl.BlockSpec((1, n, HEAD_W), lambda i, g, j: (i, 0, g))
    return pl.pallas_call(
        functools.partial(_nsa_prompt_kernel, tq=tq, n_slc=n_slc),
        grid=(b, N_KV, t // tq),
        in_specs=[
            pl.BlockSpec((1, tq, gw), lambda i, g, j: (i, j, g)),
            pl.BlockSpec((1, tq, LANE), lambda i, g, j: (i, j, g)),
            seq(n_pad), seq(n_pad), seq(t), seq(t), seq(t), seq(t),
            pl.BlockSpec((4, tq, rows), lambda i, g, j: (0, 0, g)),
            pl.BlockSpec((1, 2 * n_pad, rows), lambda i, g, j: (0, 0, g)),
            pl.BlockSpec(wimp.shape, lambda i, g, j: (0, 0)),
        ],
        out_specs=pl.BlockSpec((1, tq, gw), lambda i, g, j: (i, j, g)),
        out_shape=jax.ShapeDtypeStruct(q.shape, F32),
        scratch_shapes=[
            pltpu.VMEM((rows, HEAD_W), BF16),
            pltpu.VMEM((t, HEAD_W), BF16),
            pltpu.VMEM((HEAD_W, t), BF16),
            pltpu.VMEM((t, HEAD_W), BF16),
            pltpu.VMEM((HEAD_W, t), BF16),
            pltpu.VMEM((LANE, tq), BF16),
            pltpu.VMEM((1, rows), F32),
            pltpu.VMEM((1, rows), F32),
            pltpu.VMEM((HEAD_W, rows), F32),
        ],
        compiler_params=_params(("parallel", "parallel", "arbitrary")),
        name="nsa_prompt",
    )(q, gates, kc, vc, ks, vs, kw, vw, bias, u_bias, wimp)


def _decode_cmp_buckets(t_new, past_len):
    n_pad = past_len // CMP_STRIDE
    i = np.arange(t_new)[:, None]
    j = n_pad - LANE + np.arange(LANE)[None, :]
    last = _bucket_np(past_len + i - CMP_STRIDE * j - (2 * CMP_STRIDE - 1))
    last = np.where(j < n_pad - 1, last, -1)
    return np.stack([np.full((t_new, LANE), N_BUCKETS - 1, np.int32), last]).astype(np.int32)


def _decode_win_buckets(t_new, n_tiles):
    i = np.arange(t_new)[:, None]
    idx = np.arange(n_tiles * LANE)[None, :]
    dw = WINDOW + i - idx
    ok = (dw >= 0) & (dw < WINDOW) & (idx < WINDOW + t_new)
    b = np.where(ok, _bucket_np(dw), -1)
    return np.stack([b[:, k * LANE:(k + 1) * LANE] for k in range(n_tiles)]).astype(np.int32)


def _nsa_decode_kernel(pt_ref, q_ref, gt_ref, kc_ref, vc_ref, kn_ref, vn_ref, kw_ref, vw_ref, *rest,
                       n_pages, past_len):
    del pt_ref
    kp = rest[:PAGES_PER_STEP]
    vp = rest[PAGES_PER_STEP:2 * PAGES_PER_STEP]
    (bias_ref, bias_c_ref, bias_w_ref, wimp_ref, o_ref,
     qr_ref, selt_ref, ocmp_ref, owin_ref, m_ref, l_ref, acc_ref) = rest[2 * PAGES_PER_STEP:]
    s = pl.program_id(1)
    t_new = q_ref.shape[1]
    rows = HPG * t_new
    n_slc = -(-(past_len + t_new) // SLC_BLOCK)

    @pl.when(s == 0)
    def _():
        _reset_flash(m_ref, l_ref, acc_ref)
        qr_ref[...] = _query_rows(q_ref, 1, NSA_SCALE)
        n_pad = kc_ref.shape[1]
        sel_t = jnp.zeros(selt_ref.shape, F32)
        for g in range(N_KV):
            sl = slice(g * HEAD_W, (g + 1) * HEAD_W)
            gr = slice(g * rows, (g + 1) * rows)
            qx = _stack_group_heads(q_ref, g).astype(BF16)
            bias_c = jnp.concatenate([bias_c_ref[0, gr, :]] * (n_pad // LANE - 1) + [bias_c_ref[1, gr, :]], axis=1)
            sc = _dot_t(qx, kc_ref[0, :, sl].astype(BF16)) * NSA_SCALE + bias_c
            pc = _masked_softmax(sc, bias_c > 0.5 * NEG_INF)
            ocmp_ref[g] = _dot(pc.astype(BF16), vc_ref[0, :, sl].astype(BF16))
            psum = sum(pc[r * t_new:(r + 1) * t_new] for r in range(1, HPG)) + pc[0:t_new]
            s_slc = _dot_exact_rhs(psum, wimp_ref[...])
            sel = _select_blocks(s_slc, past_len, n_slc)
            sel_pad = jnp.concatenate([sel, jnp.zeros((LANE - t_new, sel.shape[1]), F32)], axis=0).T
            tok = lax.broadcasted_iota(jnp.int32, (LANE, LANE), 0)
            lane = lax.broadcasted_iota(jnp.int32, (LANE, LANE), 1)
            spread = jnp.where((lane % t_new == tok) & (lane // rows == g), 1.0, 0.0).astype(BF16)
            sel_t = sel_t + _dot(sel_pad.astype(BF16), spread)
            n_wt = bias_w_ref.shape[0]
            bias_w = jnp.concatenate([bias_w_ref[k, gr, :] for k in range(n_wt)], axis=1)
            sw = _dot_t(qx, kw_ref[0, :, sl].astype(BF16)) * NSA_SCALE + bias_w
            pw = _masked_softmax(sw, bias_w > 0.5 * NEG_INF)
            owin_ref[g] = _dot(pw.astype(BF16), vw_ref[0, :, sl].astype(BF16))
        selt_ref[...] = sel_t
        _decode_step_t(kn_ref[0], vn_ref[0], qr_ref, bias_ref[2], m_ref, l_ref, acc_ref)

    block_rows = SLC_BLOCK * N_KV
    first_block = s * (PAGES_PER_STEP * PAGE // SLC_BLOCK)
    keep = jnp.concatenate(
        [jnp.broadcast_to(selt_ref[pl.ds(first_block + b, 1), :], (block_rows, selt_ref.shape[1]))
         for b in range(PAGES_PER_STEP * PAGE // SLC_BLOCK)], axis=0)
    _decode_step_t(_page_rows(kp), _page_rows(vp), qr_ref,
                   _page_bias_t(bias_ref, s * PAGES_PER_STEP, n_pages, 1), m_ref, l_ref, acc_ref, keep=keep)

    @pl.when(s == pl.num_programs(1) - 1)
    def _():
        sig = jax.nn.sigmoid(gt_ref[0])
        o_slc_all = (acc_ref[...] / l_ref[...]).T
        for g in range(N_KV):
            o_cmp = ocmp_ref[g]
            o_win = owin_ref[g]
            for r in range(HPG):
                h = g * HPG + r
                rs = slice(r * t_new, (r + 1) * t_new)
                c0 = g * LANE + r
                o_ref[0, :, h * HEAD_W:(h + 1) * HEAD_W] = (
                    sig[:, c0:c0 + 1] * o_cmp[rs]
                    + sig[:, c0 + HPG:c0 + HPG + 1] * o_slc_all[h * t_new:(h + 1) * t_new]
                    + sig[:, c0 + 2 * HPG:c0 + 2 * HPG + 1] * o_win[rs])


def nsa_decode(q, gates, kc, vc, k_new, v_new, kw_src, vw_src, cache_k, cache_v, layer, page_table,
               bias, bias_c, bias_w, past_len):
    b, t_new, _ = q.shape
    n_pages = page_table.shape[1]
    rows = HPG * t_new
    lanes = N_HEADS * t_new
    assert lanes == LANE
    n_pad = kc.shape[1]
    n_slc = -(-(past_len + t_new) // SLC_BLOCK)
    n_cols = -(-n_slc // LANE) * LANE
    wimp = jnp.asarray(_importance_matrix(n_pad - 1, n_pad, n_slc, n_cols), BF16)
    fixed = lambda *shape: pl.BlockSpec(shape, lambda i, s, pt: (0,) * len(shape))
    per_b = lambda *shape: pl.BlockSpec((1,) + shape, lambda i, s, pt: (i,) + (0,) * len(shape))
    grid_spec = pltpu.PrefetchScalarGridSpec(
        num_scalar_prefetch=1,
        grid=(b, n_pages // PAGES_PER_STEP),
        in_specs=[per_b(t_new, N_HEADS * HEAD_W), per_b(t_new, N_KV * LANE),
                  per_b(n_pad, NSA_KV), per_b(n_pad, NSA_KV), per_b(PAGE_ROWS, HEAD_W), per_b(PAGE_ROWS, HEAD_W),
                  per_b(kw_src.shape[1], NSA_KV), per_b(kw_src.shape[1], NSA_KV)]
        + _page_specs(layer, 2)
        + [fixed(*bias.shape), fixed(*bias_c.shape), fixed(*bias_w.shape), fixed(*wimp.shape)],
        out_specs=per_b(t_new, N_HEADS * HEAD_W),
        scratch_shapes=[
            pltpu.VMEM((lanes, HEAD_W), BF16),
            pltpu.VMEM((n_cols, lanes), F32),
            pltpu.VMEM((N_KV, rows, HEAD_W), F32),
            pltpu.VMEM((N_KV, rows, HEAD_W), F32),
            pltpu.VMEM((1, lanes), F32),
            pltpu.VMEM((1, lanes), F32),
            pltpu.VMEM((HEAD_W, lanes), F32),
        ],
    )
    return pl.pallas_call(
        functools.partial(_nsa_decode_kernel, n_pages=n_pages, past_len=past_len),
        grid_spec=grid_spec,
        out_shape=jax.ShapeDtypeStruct(q.shape, F32),
        compiler_params=_params(("parallel", "arbitrary")),
        name="nsa_decode",
    )(page_table, q, gates, kc, vc, k_new, v_new, kw_src, vw_src,
      *([cache_k] * PAGES_PER_STEP), *([cache_v] * PAGES_PER_STEP), bias, bias_c, bias_w, wimp)


def _gate_weights(w_g):
    d = w_g.shape[0]
    w = jnp.transpose(w_g.reshape(d, 3, N_KV, HPG), (0, 2, 1, 3)).reshape(d, N_KV, 3 * HPG)
    return jnp.pad(w, ((0, 0), (0, 0), (0, LANE - 3 * HPG))).reshape(d, N_KV * LANE)


def nsa_layer(xp, xs, gain, w_in, q_norm, k_norm, pe, w1, w2, w_out, caches, layer, page_table,
              bias_p, u_bias, bias_s, bias_sc, bias_sw, past_len):
    cmp_k, cmp_v, slc_k, slc_v, win_k, win_v = caches
    nqkv = N_HEADS * HEAD_W + 6 * NSA_KV
    w_g = _gate_weights(w_in[layer, :, nqkv:])[None]
    outs = []
    for x, paged in ((xp, False), (xs, True)):
        b, t, d = x.shape
        x2 = x.reshape(b * t, d)
        proj = dense(x2, w_in, layer, gain=gain, n=nqkv)
        gates = dense(x2, w_g, 0, gain=gain).reshape(b, t, -1)
        q, kcr, vcr, ks, vs, kw, vw = (a.reshape(b, t, -1) for a in nsa_post(proj, q_norm, k_norm))
        if paged:
            kc = compress(_as_page_rows(cmp_k), layer, page_table, pe[0], w1[0], w2[0], k_norm[0])
            vc = compress(_as_page_rows(cmp_v), layer, page_table, pe[1], w1[1], w2[1], None)
            n_wt = bias_sw.shape[0]
            kw_src = jnp.concatenate([win_k[layer], kw], axis=1)
            vw_src = jnp.concatenate([win_v[layer], vw], axis=1)
            o = nsa_decode(q, gates, kc, vc, _new_page_rows(ks), _new_page_rows(vs),
                           _pad_rows(kw_src, n_wt * LANE), _pad_rows(vw_src, n_wt * LANE),
                           _as_page_rows(slc_k), _as_page_rows(slc_v), layer, page_table,
                           bias_s, bias_sc, bias_sw, past_len)
            kw_out, vw_out = kw_src[:, -WINDOW:], vw_src[:, -WINDOW:]
        else:
            n_pg = t // PAGE
            ident = jnp.arange(b * n_pg, dtype=jnp.int32).reshape(b, n_pg)
            as_pages = lambda a: a.reshape(1, b * n_pg, PAGE_ROWS, HEAD_W)
            kc = compress(as_pages(kcr), 0, ident, pe[0], w1[0], w2[0], k_norm[0])
            vc = compress(as_pages(vcr), 0, ident, pe[1], w1[1], w2[1], None)
            o = nsa_prompt(q, gates, kc, vc, ks, vs, kw, vw, bias_p, u_bias)
            kw_out, vw_out = kw[:, -WINDOW:], vw[:, -WINDOW:]
        y = dense(o.reshape(b * t, -1), w_out, layer, res=x2).reshape(b, t, d)
        st = tuple(a.reshape(b, -1, N_KV, HEAD_W) for a in (kcr, vcr, ks, vs, kw_out, vw_out))
        outs.append((y, st))
    (yp, stp), (ys, sts) = outs
    return yp, ys, stp, sts


def kernel(x_prompt, x_sample, cache_diff_k, cache_diff_v, state_ssm, state_conv, cache_nsa_cmp_k, cache_nsa_cmp_v, cache_nsa_slc_k, cache_nsa_slc_v, cache_nsa_win_k, cache_nsa_win_v, cache_mem_k, cache_mem_v, page_table, mem_prompt, rel_bias_table, norm_mix, norm_xattn, norm_mem, norm_ffn, diff_w_in, diff_q_norm, diff_k_norm, diff_lambda, diff_sub_norm, diff_w_out, ssm_w_in, ssm_conv_w, ssm_conv_b, ssm_dt_bias, ssm_a_log, ssm_d, ssm_norm, ssm_w_out, nsa_w_in, nsa_q_norm, nsa_k_norm, nsa_cmp_pe, nsa_cmp_w1, nsa_cmp_w2, nsa_w_out, xattn_w_q, xattn_w_k, xattn_w_v, xattn_q_norm, xattn_k_norm, xattn_w_o, ffn_w1, ffn_w3, ffn_w2):
    xp, xs = x_prompt, x_sample
    bp, t, d = xp.shape
    bs, t_new, _ = xs.shape
    past_len = page_table.shape[1] * PAGE
    depth = norm_mix.shape[0]

    bias_p = bias_tiles(_prompt_attn_buckets(PROMPT_TQ), rel_bias_table, heads_on_lanes=True, scale=LOG2E)
    u_bias = bias_tiles(_prompt_cmp_buckets(PROMPT_TQ, t // CMP_STRIDE), rel_bias_table, heads_on_lanes=True,
                        scale=LOG2E)
    bias_s = bias_tiles_indexed(_decode_page_index(t_new, past_len), rel_bias_table)
    bias_sc = bias_tiles(_decode_cmp_buckets(t_new, past_len), rel_bias_table)
    n_wt = -(-(WINDOW + t_new) // LANE)
    bias_sw = bias_tiles(_decode_win_buckets(t_new, n_wt), rel_bias_table)

    mem_k, mem_v = mem_kv(mem_prompt, norm_mem, xattn_w_k, xattn_w_v, xattn_k_norm)
    xw = X_HEADS * X_DH
    win_k = cache_nsa_win_k.reshape(*cache_nsa_win_k.shape[:3], NSA_KV)
    win_v = cache_nsa_win_v.reshape(*cache_nsa_win_v.shape[:3], NSA_KV)

    dkp, dvp, dks, dvs = [], [], [], []
    ssp, cvp, sss, cvs = [], [], [], []
    nsp, nss = [], []
    for i in range(depth):
        kind, j = i % 3, i // 3
        if kind == 0:
            lam_init = 0.8 - 0.6 * math.exp(-0.3 * i)
            xp, xs, kp_, vp_, ks_, vs_ = diff_layer(
                xp, xs, norm_mix[i], diff_w_in, diff_q_norm[j], diff_k_norm[j], diff_lambda[j],
                diff_sub_norm[j], diff_w_out, cache_diff_k, cache_diff_v, j, page_table,
                bias_p, bias_s, lam_init)
            dkp.append(kp_)
            dvp.append(vp_)
            dks.append(ks_)
            dvs.append(vs_)
        elif kind == 1:
            xp, xs, hp_, cp_, hs_, cs_ = ssd_layer(
                xp, xs, norm_mix[i], ssm_w_in, j, ssm_conv_w[j], ssm_conv_b[j], ssm_dt_bias[j], ssm_a_log[j],
                ssm_d[j], ssm_norm[j], ssm_w_out, state_conv[j], state_ssm[j])
            ssp.append(hp_)
            cvp.append(cp_)
            sss.append(hs_)
            cvs.append(cs_)
        else:
            caches = (cache_nsa_cmp_k, cache_nsa_cmp_v, cache_nsa_slc_k, cache_nsa_slc_v, win_k, win_v)
            xp, xs, stp, sts = nsa_layer(
                xp, xs, norm_mix[i], nsa_w_in, nsa_q_norm[j], nsa_k_norm[j], nsa_cmp_pe[j], nsa_cmp_w1[j],
                nsa_cmp_w2[j], nsa_w_out, caches, j, page_table,
                bias_p, u_bias, bias_s, bias_sc, bias_sw, past_len)
            nsp.append(stp)
            nss.append(sts)
        xp = xattn(xp, norm_xattn[i], xattn_w_q, xattn_q_norm[i], mem_k[i], mem_v[i], xattn_w_o, i)
        xs = xattn(xs, norm_xattn[i], xattn_w_q, xattn_q_norm[i], cache_mem_k[i].reshape(bs, N_MEM, xw),
                   cache_mem_v[i].reshape(bs, N_MEM, xw), xattn_w_o, i)
        xp = ffn(xp.reshape(bp * t, d), norm_ffn[i], ffn_w1, ffn_w3, ffn_w2, i).reshape(bp, t, d)
        xs = ffn(xs.reshape(bs * t_new, d), norm_ffn[i], ffn_w1, ffn_w3, ffn_w2, i).reshape(bs, t_new, d)

    st = lambda xs_: jnp.stack(xs_, axis=0)
    nsp_t = [st([s[a] for s in nsp]) for a in range(6)]
    nss_t = [st([s[a] for s in nss]) for a in range(6)]
    mem_shape = (depth, bp, N_MEM, X_HEADS, X_DH)
    return (xp, xs,
            st(dkp), st(dvp), st(dks), st(dvs),
            st(ssp), st(cvp), st(sss), st(cvs),
            *nsp_t, *nss_t,
            mem_k.reshape(mem_shape), mem_v.reshape(mem_shape))
```

```python
import functools
import math

import numpy as np
import jax
import jax.numpy as jnp
from jax import lax
from jax.experimental import pallas as pl
from jax.experimental.pallas import tpu as pltpu

F32 = jnp.float32
BF16 = jnp.bfloat16

D_MODEL = 2048
DEPTH = 4
PAGE = 128
N_HEADS = 16
N_KV = 4
HPG = N_HEADS // N_KV
DIFF_DH = 64
HEAD_W = 128
N_BUCKETS = 32
MAX_EXACT = 16
MAX_DIST = 128
D_INNER = 2 * D_MODEL
SSM_HEADDIM = 64
SSM_HEADS = D_INNER // SSM_HEADDIM
SSM_GROUPS = 8
SSM_HPG = SSM_HEADS // SSM_GROUPS
D_STATE = 128
CONV_W = 4
CONV_DIM = D_INNER + 2 * SSM_GROUPS * D_STATE
SSM_CHUNK = 128
CMP_STRIDE = 16
SLC_BLOCK = 64
SLC_RATIO = SLC_BLOCK // CMP_STRIDE
N_SELECT = 16
WINDOW = 512
N_MEM = 256
X_HEADS = 4
X_DH = 128
EPS = 1e-6
NEG_INF = -1e30
FORCE_SCORE = 1e4

LANE = 128
VMEM_LIMIT = 56 * 1024 * 1024


def _params(sem):
    return pltpu.CompilerParams(dimension_semantics=sem, vmem_limit_bytes=VMEM_LIMIT)


def _rms(x, gain):
    return x * lax.rsqrt(jnp.mean(x * x, axis=-1, keepdims=True) + EPS) * gain


def _dot(a, b):
    return jnp.dot(a, b, preferred_element_type=F32)


def _dot_t(a, b):
    return lax.dot_general(a, b, (((1,), (1,)), ((), ())), preferred_element_type=F32)


def _split3(x):
    hi = x.astype(BF16)
    r1 = x - hi.astype(F32)
    mid = r1.astype(BF16)
    lo = (r1 - mid.astype(F32)).astype(BF16)
    return hi, mid, lo


def _dot_exact_rhs(x, m_bf16):
    hi, mid, lo = _split3(x)
    return _dot(hi, m_bf16) + _dot(mid, m_bf16) + _dot(lo, m_bf16)


def _norm64(blk, gain, lo):
    sq = blk * blk
    s_lo = jnp.sum(jnp.where(lo, sq, 0.0), axis=-1, keepdims=True)
    s_hi = jnp.sum(jnp.where(lo, 0.0, sq), axis=-1, keepdims=True)
    ms = jnp.where(lo, s_lo, s_hi) * (1.0 / DIFF_DH)
    return blk * lax.rsqrt(ms + EPS) * gain


def _head_norm_tile(y, gain, seg):
    lo = lax.broadcasted_iota(jnp.int32, (y.shape[0], LANE), 1) < DIFF_DH
    blocks = []
    for c in range(y.shape[1] // LANE):
        sl = slice(c * LANE, (c + 1) * LANE)
        blocks.append(_norm64(y[:, sl], gain[:, sl], lo) if seg == DIFF_DH else _rms(y[:, sl], gain[:, sl]))
    return jnp.concatenate(blocks, axis=1)


def _dense_kernel(*refs, norm, residual, w_t, head_seg, norm_tiles):
    refs = list(refs)
    x_ref = refs.pop(0)
    g_ref = refs.pop(0) if norm else None
    w_ref = refs.pop(0)
    r_ref = refs.pop(0) if residual else None
    hg_ref = refs.pop(0) if head_seg else None
    o_ref, xb_ref = refs
    j = pl.program_id(1)

    @pl.when(j == 0)
    def _():
        x = x_ref[...]
        if norm:
            x = _rms(x, g_ref[...])
        xb_ref[...] = x.astype(BF16)

    w = w_ref[0].astype(BF16)
    y = _dot_t(xb_ref[...], w) if w_t else _dot(xb_ref[...], w)
    if residual:
        y = y + r_ref[...]
    if not head_seg:
        o_ref[...] = y
        return
    is_norm = functools.reduce(jnp.logical_or, [j == t for t in norm_tiles])

    @pl.when(is_norm)
    def _():
        o_ref[...] = _head_norm_tile(y, hg_ref[0], head_seg)

    @pl.when(jnp.logical_not(is_norm))
    def _():
        o_ref[...] = y


def dense(x, w, layer, gain=None, res=None, n=None, w_t=False, head_norm=None):
    m, k = x.shape
    n = w.shape[1 if w_t else 2] if n is None else n
    tm = m if m <= 512 else (2048 if k <= D_MODEL and m % 2048 == 0 else 512)
    tn = LANE if n % 256 else (512 if n % 512 == 0 else 256)
    assert m % tm == 0 and n % tn == 0
    norm, residual = gain is not None, res is not None
    in_specs = [pl.BlockSpec((tm, k), lambda i, j: (i, 0), pipeline_mode=pl.Buffered(1))]
    args = [x]
    if norm:
        in_specs.append(pl.BlockSpec((1, k), lambda i, j: (0, 0)))
        args.append(gain.reshape(1, k))
    if w_t:
        in_specs.append(pl.BlockSpec((1, tn, k), lambda i, j: (layer, j, 0)))
    else:
        in_specs.append(pl.BlockSpec((1, k, tn), lambda i, j: (layer, 0, j)))
    args.append(w)
    if residual:
        in_specs.append(pl.BlockSpec((tm, tn), lambda i, j: (i, j)))
        args.append(res)
    head_seg, norm_tiles = 0, ()
    if head_norm is not None:
        head_seg, tile_gains = head_norm
        norm_tiles = tuple(sorted(tile_gains))
        ones = jnp.ones((tn,), F32)
        gains = jnp.stack([tile_gains.get(t, ones) for t in range(n // tn)]).reshape(n // tn, 1, tn)
        in_specs.append(pl.BlockSpec((1, 1, tn), lambda i, j: (j, 0, 0)))
        args.append(gains)
    return pl.pallas_call(
        functools.partial(_dense_kernel, norm=norm, residual=residual, w_t=w_t, head_seg=head_seg,
                          norm_tiles=norm_tiles),
        grid=(m // tm, n // tn),
        in_specs=in_specs,
        out_specs=pl.BlockSpec((tm, tn), lambda i, j: (i, j)),
        out_shape=jax.ShapeDtypeStruct((m, n), F32),
        scratch_shapes=[pltpu.VMEM((tm, k), BF16)],
        compiler_params=_params(("parallel", "arbitrary")),
        name="dense",
    )(*args)


def _ffn_kernel(x_ref, g_ref, w1_ref, w3_ref, w2_ref, o_ref, xb_ref):
    @pl.when(pl.program_id(1) == 0)
    def _():
        x = x_ref[...]
        xb_ref[...] = _rms(x, g_ref[...]).astype(BF16)
        o_ref[...] = x

    xb = xb_ref[...]
    h1 = _dot(xb, w1_ref[0].astype(BF16))
    h3 = _dot(xb, w3_ref[0].astype(BF16))
    a = (h1 * jax.nn.sigmoid(h1) * h3).astype(BF16)
    o_ref[...] += _dot(a, w2_ref[0].astype(BF16))


def ffn(x, gain, w1, w3, w2, layer):
    m, d = x.shape
    f = w1.shape[2]
    tm = min(m, 1024)
    tf = 256
    assert m % tm == 0 and f % tf == 0
    return pl.pallas_call(
        _ffn_kernel,
        grid=(m // tm, f // tf),
        in_specs=[
            pl.BlockSpec((tm, d), lambda i, j: (i, 0), pipeline_mode=pl.Buffered(1)),
            pl.BlockSpec((1, d), lambda i, j: (0, 0)),
            pl.BlockSpec((1, d, tf), lambda i, j: (layer, 0, j)),
            pl.BlockSpec((1, d, tf), lambda i, j: (layer, 0, j)),
            pl.BlockSpec((1, tf, d), lambda i, j: (layer, j, 0)),
        ],
        out_specs=pl.BlockSpec((tm, d), lambda i, j: (i, 0)),
        out_shape=jax.ShapeDtypeStruct((m, d), F32),
        scratch_shapes=[pltpu.VMEM((tm, d), BF16)],
        compiler_params=_params(("parallel", "arbitrary")),
        name="ffn",
    )(x, gain.reshape(1, d), w1, w3, w2)


def _mem_kv_kernel(mem_ref, g_ref, wk_ref, wv_ref, kn_ref, k_ref, v_ref):
    m = _rms(mem_ref[0], g_ref[0]).astype(BF16)
    k = _dot(m, wk_ref[0].astype(BF16))
    v_ref[0, 0] = _dot(m, wv_ref[0].astype(BF16))
    for h in range(X_HEADS):
        sl = slice(h * X_DH, (h + 1) * X_DH)
        k_ref[0, 0, :, sl] = _rms(k[:, sl], kn_ref[0])


def mem_kv(mem, g_mem, wk, wv, k_norm):
    b = mem.shape[0]
    nl = wk.shape[0]
    hw = X_HEADS * X_DH
    shape = jax.ShapeDtypeStruct((nl, b, N_MEM, hw), F32)
    return pl.pallas_call(
        _mem_kv_kernel,
        grid=(nl, b),
        in_specs=[
            pl.BlockSpec((1, N_MEM, D_MODEL), lambda l, i: (i, 0, 0)),
            pl.BlockSpec((1, 1, D_MODEL), lambda l, i: (l, 0, 0)),
            pl.BlockSpec((1, D_MODEL, hw), lambda l, i: (l, 0, 0)),
            pl.BlockSpec((1, D_MODEL, hw), lambda l, i: (l, 0, 0)),
            pl.BlockSpec((1, 1, X_DH), lambda l, i: (l, 0, 0)),
        ],
        out_specs=[pl.BlockSpec((1, 1, N_MEM, hw), lambda l, i: (l, i, 0, 0))] * 2,
        out_shape=[shape, shape],
        compiler_params=_params(("parallel", "parallel")),
        name="mem_kv",
    )(mem, g_mem.reshape(nl, 1, D_MODEL), wk, wv, k_norm.reshape(nl, 1, X_DH))


def _xattn_kernel(x_ref, g_ref, wq_ref, qn_ref, k_ref, v_ref, wo_ref, o_ref):
    x = x_ref[0]
    q = _dot(_rms(x, g_ref[...]).astype(BF16), wq_ref[0].astype(BF16))
    outs = []
    for h in range(X_HEADS):
        sl = slice(h * X_DH, (h + 1) * X_DH)
        qh = _rms(q[:, sl], qn_ref[...]).astype(BF16)
        s = _dot_t(qh, k_ref[0, :, sl].astype(BF16)) * (X_DH ** -0.5)
        e = jnp.exp(s - jnp.max(s, axis=-1, keepdims=True))
        p = e / jnp.sum(e, axis=-1, keepdims=True)
        outs.append(_dot(p.astype(BF16), v_ref[0, :, sl].astype(BF16)))
    o = jnp.concatenate(outs, axis=1).astype(BF16)
    o_ref[0] = x + _dot(o, wo_ref[0].astype(BF16))


def xattn(x, gain, wq, q_norm, k, v, wo, layer):
    b, t, d = x.shape
    hw = X_HEADS * X_DH
    tm = min(t, 512)
    return pl.pallas_call(
        _xattn_kernel,
        grid=(b, t // tm),
        in_specs=[
            pl.BlockSpec((1, tm, d), lambda i, j: (i, j, 0)),
            pl.BlockSpec((1, d), lambda i, j: (0, 0)),
            pl.BlockSpec((1, d, hw), lambda i, j: (layer, 0, 0)),
            pl.BlockSpec((1, X_DH), lambda i, j: (0, 0)),
            pl.BlockSpec((1, N_MEM, hw), lambda i, j: (i, 0, 0)),
            pl.BlockSpec((1, N_MEM, hw), lambda i, j: (i, 0, 0)),
            pl.BlockSpec((1, hw, d), lambda i, j: (layer, 0, 0)),
        ],
        out_specs=pl.BlockSpec((1, tm, d), lambda i, j: (i, j, 0)),
        out_shape=jax.ShapeDtypeStruct((b, t, d), F32),
        compiler_params=_params(("parallel", "parallel")),
        name="xattn",
    )(x, gain.reshape(1, d), wq, q_norm.reshape(1, X_DH), k, v, wo)


def _bucket_np(dist):
    n = np.maximum(dist, 0)
    nf = np.maximum(n, 1).astype(np.float64)
    large = MAX_EXACT + (np.log(nf / MAX_EXACT) / math.log(MAX_DIST / MAX_EXACT)
                         * (N_BUCKETS - MAX_EXACT)).astype(np.int64)
    b = np.where(n < MAX_EXACT, n, np.minimum(large, N_BUCKETS - 1))
    return np.where(dist < 0, -1, b).astype(np.int32)


def _bias_kernel(bkt_ref, tab_ref, o_ref, *, scale):
    h = pl.program_id(1)
    b = bkt_ref[0]
    acc = jnp.full(b.shape, NEG_INF, F32)
    for k in range(N_BUCKETS):
        acc = jnp.where(b == k, tab_ref[k, h] * scale, acc)
    o_ref[0] = acc


def _bias_packed_kernel(idx_ref, tab_ref, o_ref, *, scale):
    idx = idx_ref[0]
    acc = jnp.full(idx.shape, NEG_INF, F32)
    for k in range(N_BUCKETS):
        for h in range(N_HEADS):
            acc = jnp.where(idx == k * N_HEADS + h, tab_ref[k, h] * scale, acc)
    o_ref[0] = acc


def bias_tiles_indexed(idx, table, scale=1.0):
    nt, r, w = idx.shape
    return pl.pallas_call(
        functools.partial(_bias_packed_kernel, scale=scale),
        grid=(nt,),
        in_specs=[
            pl.BlockSpec((1, r, w), lambda t: (t, 0, 0)),
            pl.BlockSpec(memory_space=pltpu.SMEM),
        ],
        out_specs=pl.BlockSpec((1, r, w), lambda t: (t, 0, 0)),
        out_shape=jax.ShapeDtypeStruct((nt, r, w), F32),
        compiler_params=_params(("parallel",)),
        name="bias_tiles_indexed",
    )(jnp.asarray(idx.astype(np.int32)), table)


def bias_tiles(buckets, table, heads_on_lanes=False, scale=1.0):
    nt, r, w = buckets.shape
    if heads_on_lanes:
        out_spec = pl.BlockSpec((1, r, w), lambda t, h: (t, 0, h))
        out_shape = (nt, r, N_HEADS * w)
    else:
        out_spec = pl.BlockSpec((1, r, w), lambda t, h: (t, h, 0))
        out_shape = (nt, N_HEADS * r, w)
    return pl.pallas_call(
        functools.partial(_bias_kernel, scale=scale),
        grid=(nt, N_HEADS),
        in_specs=[
            pl.BlockSpec((1, r, w), lambda t, h: (t, 0, 0)),
            pl.BlockSpec(memory_space=pltpu.SMEM),
        ],
        out_specs=out_spec,
        out_shape=jax.ShapeDtypeStruct(out_shape, F32),
        compiler_params=_params(("parallel", "parallel")),
        name="bias_tiles",
    )(jnp.asarray(buckets), table)


def _prompt_attn_buckets(tq):
    assert WINDOW % tq == 0 and tq >= MAX_DIST
    j = np.arange(tq)[:, None]
    i = np.arange(tq)[None, :]
    far = np.full((tq, tq), N_BUCKETS - 1, np.int32)
    return np.stack([
        _bucket_np(i - j),
        _bucket_np(i - j + tq),
        far,
        np.where(j > i, far, -1),
    ]).astype(np.int32)


def _diff_lambda(lam_ref, lam_init):
    lf = lam_ref[...]
    s01 = jnp.sum(lf[0:1] * lf[1:2], axis=-1, keepdims=True)
    s23 = jnp.sum(lf[2:3] * lf[3:4], axis=-1, keepdims=True)
    return jnp.exp(s01) - jnp.exp(s23) + lam_init


def _split_maps(qs):
    lo = lax.broadcasted_iota(jnp.int32, qs.shape, 1) < DIFF_DH
    return jnp.concatenate([jnp.where(lo, qs, 0.0), jnp.where(lo, 0.0, qs)], axis=0)


def _diff_finish_rows(o, lam, sub_norm, lam_init):
    r = o.shape[0] // 2
    a = o[:r] - lam * o[r:]
    return _rms(a, sub_norm) * (1.0 - lam_init)


PROMPT_TQ = 256


LOG2E = math.log2(math.e)
FLASH_CHUNKS = 4
FAR_TILES = 2


def _flash_tile_t(k_tile, vt_tile, qs_ref, m_ref, l_ref, acc_ref, scale=None, bias=None, far_bias=None,
                  mask=None, n_chunks=FLASH_CHUNKS):
    chunk = qs_ref.shape[0] // n_chunks
    logits = [_dot_t(k_tile, qs_ref[c * chunk:(c + 1) * chunk, :]) for c in range(n_chunks)]
    for c in range(n_chunks):
        cols = slice(c * chunk, (c + 1) * chunk)
        s = logits[c]
        if scale is not None:
            s = s * scale
        if bias is not None:
            s = s + bias(cols)
        if mask is not None:
            s = jnp.where(mask[:, cols] > 0.5, s, NEG_INF)
        m_old = m_ref[:, cols]
        s_max = jnp.max(s, axis=0, keepdims=True)
        if far_bias is None:
            m_new = jnp.maximum(m_old, s_max)
            p = jnp.exp2(s - m_new)
        else:
            fb = far_bias[:, cols]
            m_new = jnp.maximum(m_old, s_max + fb)
            p = jnp.exp2(s - (m_new - fb))
        alpha = jnp.exp2(m_old - m_new)
        l_ref[:, cols] = alpha * l_ref[:, cols] + jnp.sum(p, axis=0, keepdims=True)
        acc_ref[:, cols] = alpha * acc_ref[:, cols] + _dot(vt_tile, p.astype(BF16))
        m_ref[:, cols] = m_new


def _stage_kv(k_ref, v_ref, kb_ref, vt_ref, chunk):
    for c in range(k_ref.shape[1] // chunk):
        rows = slice(c * chunk, (c + 1) * chunk)
        kb_ref[rows, :] = k_ref[0, rows, :].astype(BF16)
        vt_ref[:, rows] = v_ref[0, rows, :].T.astype(BF16)


def _diff_flash_kernel(lam_ref, q_ref, k_ref, v_ref, bias_ref, sn_ref, o_ref,
                       qs_ref, kb_ref, vt_ref, m_ref, l_ref, acc_ref, *, tq, lam_init):
    qi = pl.program_id(2)

    @pl.when(qi == 0)
    def _():
        _stage_kv(k_ref, v_ref, kb_ref, vt_ref, tq)

    qb = q_ref[0]
    qs = jnp.concatenate([qb[:, r * HEAD_W:(r + 1) * HEAD_W] for r in range(HPG)], axis=0)
    qs_ref[...] = _split_maps(qs * (DIFF_DH ** -0.5 * LOG2E)).astype(BF16)
    _reset_flash(m_ref, l_ref, acc_ref)
    far = bias_ref[2, 0:1, :]
    far = jnp.concatenate([far, far], axis=1)

    def far_body(width):
        def body(kt, c):
            rows = pl.ds(pl.multiple_of(kt * width, width), width)
            _flash_tile_t(kb_ref[rows, :], vt_ref[:, rows], qs_ref, m_ref, l_ref, acc_ref, far_bias=far)
            return c
        return body

    def near_body(kt, c):
        rows = pl.ds(pl.multiple_of(kt * tq, tq), tq)
        b = bias_ref[qi - kt]
        _flash_tile_t(kb_ref[rows, :], vt_ref[:, rows], qs_ref, m_ref, l_ref, acc_ref,
                      bias=lambda cols: jnp.concatenate([b, b], axis=1), n_chunks=1)
        return c

    n_far = jnp.maximum(qi - 1, 0)
    lax.fori_loop(0, n_far // FAR_TILES, far_body(FAR_TILES * tq), 0)
    lax.fori_loop((n_far // FAR_TILES) * FAR_TILES, n_far, far_body(tq), 0)
    lax.fori_loop(n_far, qi + 1, near_body, 0)
    o = (acc_ref[...] / l_ref[...]).T
    a = _diff_finish_rows(o, _diff_lambda(lam_ref, lam_init), sn_ref[...], lam_init)
    for r in range(HPG):
        o_ref[0, :, r * HEAD_W:(r + 1) * HEAD_W] = a[r * tq:(r + 1) * tq]


def diff_flash(qkv, bias, lam_p, sub_norm, lam_init, tq=PROMPT_TQ):
    b, t, _ = qkv.shape
    gw = HPG * HEAD_W
    rows = 2 * HPG * tq
    k0 = N_HEADS
    return pl.pallas_call(
        functools.partial(_diff_flash_kernel, tq=tq, lam_init=lam_init),
        grid=(b, N_KV, t // tq),
        in_specs=[
            pl.BlockSpec((4, DIFF_DH), lambda i, g, j: (0, 0)),
            pl.BlockSpec((1, tq, gw), lambda i, g, j: (i, j, g)),
            pl.BlockSpec((1, t, HEAD_W), lambda i, g, j: (i, 0, k0 + g)),
            pl.BlockSpec((1, t, HEAD_W), lambda i, g, j: (i, 0, k0 + N_KV + g)),
            pl.BlockSpec((4, tq, HPG * tq), lambda i, g, j: (0, 0, g)),
            pl.BlockSpec((1, HEAD_W), lambda i, g, j: (0, 0)),
        ],
        out_specs=pl.BlockSpec((1, tq, gw), lambda i, g, j: (i, j, g)),
        out_shape=jax.ShapeDtypeStruct((b, t, N_HEADS * HEAD_W), F32),
        scratch_shapes=[
            pltpu.VMEM((rows, HEAD_W), BF16),
            pltpu.VMEM((t, HEAD_W), BF16),
            pltpu.VMEM((HEAD_W, t), BF16),
            pltpu.VMEM((1, rows), F32),
            pltpu.VMEM((1, rows), F32),
            pltpu.VMEM((HEAD_W, rows), F32),
        ],
        compiler_params=_params(("parallel", "parallel", "arbitrary")),
        name="diff_flash",
    )(lam_p, qkv, qkv, qkv, bias, sub_norm.reshape(1, HEAD_W))


PAGES_PER_STEP = 4


def _decode_page_index(t_new, past_len):
    j = np.arange(PAGE)[:, None]
    i = np.arange(t_new)[None, :]
    far = np.full((PAGE, t_new), N_BUCKETS - 1, np.int32)
    new = np.where(j < t_new, _bucket_np(i - j), -1)
    g = np.arange(N_KV)[None, :, None, None]
    h = np.arange(N_HEADS)[None, None, :, None]
    tiles = []
    for bkt in (far, _bucket_np(PAGE + i - j), new):
        b = bkt[:, None, None, :]
        idx = np.where((b >= 0) & (h // HPG == g), b * N_HEADS + h, -1)
        tiles.append(idx.reshape(PAGE * N_KV, N_HEADS * t_new))
    return np.stack(tiles).astype(np.int32)


def _stack_group_heads(q_ref, g):
    return jnp.concatenate(
        [q_ref[0, :, (g * HPG + r) * HEAD_W:(g * HPG + r + 1) * HEAD_W] for r in range(HPG)], axis=0)


def _query_rows(q_ref, maps, scale):
    pieces = []
    for m in range(maps):
        for h in range(N_HEADS):
            qh = q_ref[0, :, h * HEAD_W:(h + 1) * HEAD_W] * scale
            if maps == 2:
                lo = lax.broadcasted_iota(jnp.int32, qh.shape, 1) < DIFF_DH
                qh = jnp.where(lo, qh, 0.0) if m == 0 else jnp.where(lo, 0.0, qh)
            pieces.append(qh)
    return jnp.concatenate(pieces, axis=0).astype(BF16)


def _page_bias_t(bias_ref, first_page, n_pages, reps):
    tiles = [bias_ref[jnp.where(first_page + u == n_pages - 1, 1, 0)] for u in range(PAGES_PER_STEP)]
    b = jnp.concatenate(tiles, axis=0)
    return jnp.concatenate([b] * reps, axis=1)


def _decode_step_t(k_rows, v_rows, q_ref, bias, m_ref, l_ref, acc_ref, keep=None):
    s = _dot_t(k_rows.astype(BF16), q_ref[...]) + bias
    if keep is not None:
        s = jnp.where(keep > 0.5, s, NEG_INF)
    m_old = m_ref[...]
    m_new = jnp.maximum(m_old, jnp.max(s, axis=0, keepdims=True))
    alpha = jnp.exp(m_old - m_new)
    p = jnp.exp(s - m_new)
    l_ref[...] = alpha * l_ref[...] + jnp.sum(p, axis=0, keepdims=True)
    acc_ref[...] = alpha * acc_ref[...] + _dot(v_rows.T.astype(BF16), p.astype(BF16))
    m_ref[...] = m_new


def _page_rows(refs):
    return jnp.concatenate([r[0, 0] for r in refs], axis=0)


def _diff_decode_kernel(pt_ref, lam_ref, q_ref, kn_ref, vn_ref, *rest, n_pages, lam_init):
    del pt_ref
    kp = rest[:PAGES_PER_STEP]
    vp = rest[PAGES_PER_STEP:2 * PAGES_PER_STEP]
    bias_ref, sn_ref, o_ref, qr_ref, m_ref, l_ref, acc_ref = rest[2 * PAGES_PER_STEP:]
    s = pl.program_id(1)
    t_new = q_ref.shape[1]

    @pl.when(s == 0)
    def _():
        _reset_flash(m_ref, l_ref, acc_ref)
        qr_ref[...] = _query_rows(q_ref, 2, DIFF_DH ** -0.5)
        b = bias_ref[2]
        _decode_step_t(kn_ref[0], vn_ref[0], qr_ref, jnp.concatenate([b, b], axis=1), m_ref, l_ref, acc_ref)

    _decode_step_t(_page_rows(kp), _page_rows(vp), qr_ref,
                   _page_bias_t(bias_ref, s * PAGES_PER_STEP, n_pages, 2), m_ref, l_ref, acc_ref)

    @pl.when(s == pl.num_programs(1) - 1)
    def _():
        o = (acc_ref[...] / l_ref[...]).T
        a = _diff_finish_rows(o, _diff_lambda(lam_ref, lam_init), sn_ref[...], lam_init)
        for h in range(N_HEADS):
            o_ref[0, :, h * HEAD_W:(h + 1) * HEAD_W] = a[h * t_new:(h + 1) * t_new]


PAGE_ROWS = PAGE * N_KV


def _as_page_rows(cache):
    return cache.reshape(*cache.shape[:2], PAGE_ROWS, HEAD_W)


def _page_specs(layer, n):
    def spec(u):
        return pl.BlockSpec((1, 1, PAGE_ROWS, HEAD_W),
                            lambda i, s, pt: (layer, pt[i, s * PAGES_PER_STEP + u], 0, 0))
    return [spec(u) for u in range(PAGES_PER_STEP)] * n


def diff_decode(q, k_new, v_new, cache_k, cache_v, layer, page_table, bias, lam_p, sub_norm, lam_init):
    b, t_new, _ = q.shape
    n_pages = page_table.shape[1]
    lanes = 2 * N_HEADS * t_new
    assert N_HEADS * t_new == LANE
    fixed = lambda *shape: pl.BlockSpec(shape, lambda i, s, pt: (0,) * len(shape))
    per_b = lambda *shape: pl.BlockSpec((1,) + shape, lambda i, s, pt: (i,) + (0,) * len(shape))
    grid_spec = pltpu.PrefetchScalarGridSpec(
        num_scalar_prefetch=1,
        grid=(b, n_pages // PAGES_PER_STEP),
        in_specs=[fixed(4, DIFF_DH), per_b(t_new, N_HEADS * HEAD_W), per_b(PAGE_ROWS, HEAD_W),
                  per_b(PAGE_ROWS, HEAD_W)]
        + _page_specs(layer, 2)
        + [fixed(*bias.shape), fixed(1, HEAD_W)],
        out_specs=per_b(t_new, N_HEADS * HEAD_W),
        scratch_shapes=[
            pltpu.VMEM((lanes, HEAD_W), BF16),
            pltpu.VMEM((1, lanes), F32),
            pltpu.VMEM((1, lanes), F32),
            pltpu.VMEM((HEAD_W, lanes), F32),
        ],
    )
    return pl.pallas_call(
        functools.partial(_diff_decode_kernel, n_pages=n_pages, lam_init=lam_init),
        grid_spec=grid_spec,
        out_shape=jax.ShapeDtypeStruct(q.shape[:2] + (N_HEADS * HEAD_W,), F32),
        compiler_params=_params(("parallel", "arbitrary")),
        name="diff_decode",
    )(page_table, lam_p, q, k_new, v_new, *([cache_k] * PAGES_PER_STEP), *([cache_v] * PAGES_PER_STEP),
      bias, sub_norm.reshape(1, HEAD_W))


def _pad_rows(a, n):
    return jnp.pad(a, ((0, 0), (0, n - a.shape[1]), (0, 0)))


def _new_page_rows(a):
    return _pad_rows(a, PAGE).reshape(a.shape[0], PAGE_ROWS, HEAD_W)


def diff_layer(xp, xs, gain, w_in, q_norm, k_norm, lam_p, sub_norm, w_out, cache_k, cache_v, layer,
               page_table, bias_p, bias_s, lam_init):
    nq, nk = N_HEADS * HEAD_W, N_KV * HEAD_W
    tile = 4 * LANE
    q_gain, k_gain = jnp.tile(q_norm, tile // DIFF_DH), jnp.tile(k_norm, tile // DIFF_DH)
    head_norm = (DIFF_DH, {**{c: q_gain for c in range(nq // tile)}, nq // tile: k_gain})
    outs = []
    for x, paged in ((xp, False), (xs, True)):
        b, t, d = x.shape
        x2 = x.reshape(b * t, d)
        qkv = dense(x2, w_in, layer, gain=gain, head_norm=head_norm).reshape(b, t, -1)
        k3, v3 = qkv[:, :, nq:nq + nk], qkv[:, :, nq + nk:]
        if paged:
            o = diff_decode(qkv, _new_page_rows(k3), _new_page_rows(v3), _as_page_rows(cache_k),
                            _as_page_rows(cache_v), layer, page_table, bias_s, lam_p, sub_norm, lam_init)
        else:
            o = diff_flash(qkv, bias_p, lam_p, sub_norm, lam_init)
        y = dense(o.reshape(b * t, -1), w_out, layer, res=x2).reshape(b, t, d)
        outs.append((y, k3.reshape(b, t, N_KV, HEAD_W), v3.reshape(b, t, N_KV, HEAD_W)))
    (yp, kp, vp), (ys, ks, vs) = outs
    return yp, ys, kp, vp, ks, vs


SSM_GW = SSM_HPG * SSM_HEADDIM
SSM_BC = 2 * SSM_GROUPS * D_STATE
CONV_PAD = 8


def _conv_silu(buf_ref, w_ref, b_ref, n):
    acc = b_ref[...]
    for k in range(CONV_W):
        acc = acc + buf_ref[pl.ds(CONV_PAD - (CONV_W - 1) + k, n), :] * w_ref[k:k + 1, :]
    return acc * jax.nn.sigmoid(acc)


def _ssd_kernel(z_ref, x_ref, bc_ref, dt_ref, cbx_ref, cbbc_ref, wx_ref, wbc_ref, bx_ref, bbc_ref,
                dtb_ref, alog_ref, dsk_ref, ng_ref, e_ref, tri_ref, h0_ref,
                y_ref, hout_ref, ht_ref, xbuf_ref, bcbuf_ref, xa_ref, bca_ref, *, t_valid):
    c = pl.program_id(1)
    n = x_ref.shape[1]

    @pl.when(c == 0)
    def _():
        for g in range(SSM_GROUPS):
            ht_ref[g] = h0_ref[0, g].T
        xbuf_ref[0:CONV_PAD] = cbx_ref[0]
        bcbuf_ref[0:CONV_PAD] = cbbc_ref[0]

    xbuf_ref[CONV_PAD:CONV_PAD + n] = x_ref[0]
    bcbuf_ref[CONV_PAD:CONV_PAD + n] = bc_ref[0]
    xa_ref[...] = _conv_silu(xbuf_ref, wx_ref, bx_ref, n)
    bca_ref[...] = _conv_silu(bcbuf_ref, wbc_ref, bbc_ref, n)
    xbuf_ref[0:CONV_PAD] = xbuf_ref[n:n + CONV_PAD]
    bcbuf_ref[0:CONV_PAD] = bcbuf_ref[n:n + CONV_PAD]

    row = lax.broadcasted_iota(jnp.int32, (n, LANE), 0) + c * n
    dtr = dt_ref[0] + dtb_ref[...]
    dt = jnp.maximum(dtr, 0.0) + jnp.log1p(jnp.exp(-jnp.abs(dtr)))
    dt = jnp.where(row < t_valid, dt, 0.0)
    dta = dt * (-jnp.exp(alog_ref[...]))
    hi, mid, lo = _split3(dta)
    tri = tri_ref[...]
    cs = _dot(tri, hi) + _dot(tri, mid) + _dot(tri, lo)
    cs_last = cs[n - 1:n, :]
    cs_t = cs.T
    dt_t = dt.T
    stacked = jnp.concatenate(
        [jnp.exp(cs), jnp.exp(cs_last - cs) * dt, jnp.broadcast_to(jnp.exp(cs_last), (8, LANE))], axis=0)
    ex = _dot_exact_rhs(stacked, e_ref[...])
    causal = (lax.broadcasted_iota(jnp.int32, (n, n), 0) >= lax.broadcasted_iota(jnp.int32, (n, n), 1))

    for g in range(SSM_GROUPS):
        gs = slice(g * SSM_GW, (g + 1) * SSM_GW)
        bm = bca_ref[:, g * D_STATE:(g + 1) * D_STATE]
        cm = bca_ref[:, (SSM_GROUPS + g) * D_STATE:(SSM_GROUPS + g + 1) * D_STATE].astype(BF16)
        cb = _dot_t(cm, bm.astype(BF16))
        xg = xa_ref[:, gs]
        ys = []
        for r in range(SSM_HPG):
            h = g * SSM_HPG + r
            seg = cs[:, h:h + 1] - cs_t[h:h + 1, :]
            dec = jnp.where(causal, jnp.exp(jnp.where(causal, seg, 0.0)), 0.0)
            mm = (cb * dec * dt_t[h:h + 1, :]).astype(BF16)
            ys.append(_dot(mm, xg[:, r * SSM_HEADDIM:(r + 1) * SSM_HEADDIM].astype(BF16)))
        ht = ht_ref[g]
        y = jnp.concatenate(ys, axis=1) + _dot(cm, ht.astype(BF16)) * ex[0:n, gs]
        y = y + dsk_ref[:, gs] * xg
        zg = z_ref[0, :, gs]
        y = y * (zg * jax.nn.sigmoid(zg))
        y_ref[0, :, gs] = _rms(y, ng_ref[:, gs])
        xw = (xg * ex[n:2 * n, gs]).astype(BF16)
        ht_ref[g] = ht * ex[2 * n:2 * n + 1, gs] + _dot(bm.T.astype(BF16), xw)

    @pl.when(c == pl.num_programs(1) - 1)
    def _():
        for g in range(SSM_GROUPS):
            hout_ref[0, g] = ht_ref[g].T


def _head_expand_matrix():
    e = np.zeros((LANE, D_INNER), np.float32)
    for h in range(SSM_HEADS):
        e[h, h * SSM_HEADDIM:(h + 1) * SSM_HEADDIM] = 1.0
    return e


def ssd_core(zx, dt_raw, conv_buf, h0, conv_w, conv_b, dt_bias, a_log, d_skip, norm_g, t_valid):
    b, t, _ = zx.shape
    n = SSM_CHUNK
    pad_h = LANE - SSM_HEADS
    cb = jnp.pad(conv_buf, ((0, 0), (CONV_PAD - (CONV_W - 1), 0), (0, 0)))
    tri = jnp.asarray(np.tril(np.ones((n, n), np.float32)), BF16)
    e = jnp.asarray(_head_expand_matrix(), BF16)
    d_exp = jnp.repeat(d_skip, SSM_HEADDIM).reshape(1, D_INNER)
    fixed = lambda *shape: pl.BlockSpec(shape, lambda i, c: (0,) * len(shape))
    per_b = lambda *shape: pl.BlockSpec((1,) + shape, lambda i, c: (i,) + (0,) * len(shape))
    y, h_last = pl.pallas_call(
        functools.partial(_ssd_kernel, t_valid=t_valid),
        grid=(b, t // n),
        in_specs=[
            pl.BlockSpec((1, n, D_INNER), lambda i, c: (i, c, 0)),
            pl.BlockSpec((1, n, D_INNER), lambda i, c: (i, c, 1)),
            pl.BlockSpec((1, n, SSM_BC), lambda i, c: (i, c, 2 * D_INNER // SSM_BC)),
            pl.BlockSpec((1, n, LANE), lambda i, c: (i, c, 0)),
            per_b(CONV_PAD, D_INNER), per_b(CONV_PAD, SSM_BC),
            fixed(CONV_W, D_INNER), fixed(CONV_W, SSM_BC), fixed(1, D_INNER), fixed(1, SSM_BC),
            fixed(1, LANE), fixed(1, LANE), fixed(1, D_INNER), fixed(1, D_INNER),
            fixed(LANE, D_INNER), fixed(n, n),
            per_b(SSM_GROUPS, SSM_GW, D_STATE),
        ],
        out_specs=[
            pl.BlockSpec((1, n, D_INNER), lambda i, c: (i, c, 0)),
            per_b(SSM_GROUPS, SSM_GW, D_STATE),
        ],
        out_shape=[jax.ShapeDtypeStruct((b, t, D_INNER), F32),
                   jax.ShapeDtypeStruct((b, SSM_GROUPS, SSM_GW, D_STATE), F32)],
        scratch_shapes=[
            pltpu.VMEM((SSM_GROUPS, D_STATE, SSM_GW), F32),
            pltpu.VMEM((n + CONV_PAD, D_INNER), F32),
            pltpu.VMEM((n + CONV_PAD, SSM_BC), F32),
            pltpu.VMEM((n, D_INNER), F32),
            pltpu.VMEM((n, SSM_BC), F32),
        ],
        compiler_params=_params(("parallel", "arbitrary")),
        name="ssd_core",
    )(zx, zx, zx, dt_raw, cb[:, :, :D_INNER], cb[:, :, D_INNER:],
      conv_w[:, :D_INNER], conv_w[:, D_INNER:], conv_b[:D_INNER].reshape(1, -1), conv_b[D_INNER:].reshape(1, -1),
      jnp.pad(dt_bias, (0, pad_h)).reshape(1, LANE), jnp.pad(a_log, (0, pad_h)).reshape(1, LANE),
      d_exp, norm_g.reshape(1, D_INNER), e, tri,
      h0.reshape(b, SSM_GROUPS, SSM_GW, D_STATE))
    return y, h_last.reshape(b, SSM_HEADS, SSM_HEADDIM, D_STATE)


def ssd_layer(xp, xs, gain, w_in, layer, conv_w, conv_b, dt_bias, a_log, d_skip, norm_g, w_out,
              state_conv, state_ssm):
    nzx = D_INNER + CONV_DIM
    w_dt = jnp.pad(w_in[layer, :, nzx:], ((0, 0), (0, LANE - SSM_HEADS)))[None]
    w_in_t = jnp.swapaxes(w_in, 1, 2)
    outs = []
    for x, conv_buf, h0 in ((xp, None, None), (xs, state_conv, state_ssm)):
        b, t, d = x.shape
        x2 = x.reshape(b * t, d)
        if conv_buf is None:
            conv_buf = jnp.zeros((b, CONV_W - 1, CONV_DIM), F32)
            h0 = jnp.zeros((b, SSM_HEADS, SSM_HEADDIM, D_STATE), F32)
        zx = dense(x2, w_in_t, layer, gain=gain, n=nzx, w_t=True).reshape(b, t, -1)
        dt_raw = dense(x2, w_dt, 0, gain=gain).reshape(b, t, LANE)
        tp = -(-t // SSM_CHUNK) * SSM_CHUNK
        y, h_last = ssd_core(_pad_rows(zx, tp), _pad_rows(dt_raw, tp), conv_buf, h0, conv_w, conv_b,
                             dt_bias, a_log, d_skip, norm_g, t)
        y = y[:, :t].reshape(b * t, D_INNER)
        tail = min(t, CONV_W - 1)
        conv_out = jnp.concatenate([conv_buf[:, tail:], zx[:, t - tail:, D_INNER:]], axis=1)
        outs.append((dense(y, w_out, layer, res=x2).reshape(b, t, d), h_last, conv_out))
    (yp, hp, cp), (ys, hs, cs) = outs
    return yp, ys, hp, cp, hs, cs


NSA_KV = N_KV * HEAD_W
NSA_SCALE = HEAD_W ** -0.5
CMP_PAIRS = CMP_STRIDE // 2
CMP_PAGES = 16


def _cmp_uv_kernel(pt_ref, *refs):
    del pt_ref
    pages = refs[:CMP_PAGES]
    wab_ref, o_ref = refs[CMP_PAGES:]
    subs = PAGE // CMP_STRIDE
    for g in range(N_KV):
        acc = jnp.zeros((CMP_PAGES * subs, 2 * HEAD_W), F32)
        for lp in range(CMP_PAIRS):
            halves = []
            for li in range(2):
                rows = pl.ds((2 * lp + li) * N_KV + g, subs, stride=CMP_STRIDE * N_KV)
                halves.append(jnp.concatenate([p[0, 0, rows, :] for p in pages], axis=0))
            acc = acc + _dot(jnp.concatenate(halves, axis=1).astype(BF16), wab_ref[lp])
        o_ref[0, :, g * 2 * HEAD_W:(g + 1) * 2 * HEAD_W] = acc


def cmp_uv(rows, layer, page_table, wab):
    b, n_pages = page_table.shape
    subs = PAGE // CMP_STRIDE

    def spec(u):
        return pl.BlockSpec((1, 1, PAGE_ROWS, HEAD_W),
                            lambda i, c, pt: (layer, pt[i, c * CMP_PAGES + u], 0, 0))

    grid_spec = pltpu.PrefetchScalarGridSpec(
        num_scalar_prefetch=1,
        grid=(b, n_pages // CMP_PAGES),
        in_specs=[spec(u) for u in range(CMP_PAGES)]
        + [pl.BlockSpec(wab.shape, lambda i, c, pt: (0, 0, 0))],
        out_specs=pl.BlockSpec((1, CMP_PAGES * subs, 2 * NSA_KV), lambda i, c, pt: (i, c, 0)),
    )
    return pl.pallas_call(
        _cmp_uv_kernel,
        grid_spec=grid_spec,
        out_shape=jax.ShapeDtypeStruct((b, n_pages * subs, 2 * NSA_KV), F32),
        compiler_params=_params(("parallel", "arbitrary")),
        name="cmp_uv",
    )(page_table, *([rows] * CMP_PAGES), wab)


def _cmp_finish_kernel(uv_ref, pe_ref, w1_ref, w2_ref, kn_ref, o_ref, *, norm):
    n = uv_ref.shape[1]
    c = _dot(jnp.broadcast_to(pe_ref[...], (8, pe_ref.shape[1])).astype(BF16), w1_ref[...])[0:1]
    for g in range(N_KV):
        u = uv_ref[0, :, g * 2 * HEAD_W:g * 2 * HEAD_W + HEAD_W]
        v = uv_ref[0, :, g * 2 * HEAD_W + HEAD_W:(g + 1) * 2 * HEAD_W]
        pre = u + pltpu.roll(v, n - 1, axis=0) + c
        out = _dot((pre * jax.nn.sigmoid(pre)).astype(BF16), w2_ref[...])
        if norm:
            out = _rms(out, kn_ref[...])
        o_ref[0, :, g * HEAD_W:(g + 1) * HEAD_W] = out


def cmp_finish(uv, pe, w1, w2, k_norm):
    b, n, _ = uv.shape
    norm = k_norm is not None
    kn = (k_norm if norm else jnp.ones((HEAD_W,), F32)).reshape(1, HEAD_W)
    return pl.pallas_call(
        functools.partial(_cmp_finish_kernel, norm=norm),
        grid=(b,),
        in_specs=[
            pl.BlockSpec((1, n, 2 * NSA_KV), lambda i: (i, 0, 0)),
            pl.BlockSpec((1, pe.size), lambda i: (0, 0)),
            pl.BlockSpec(w1.shape, lambda i: (0, 0)),
            pl.BlockSpec(w2.shape, lambda i: (0, 0)),
            pl.BlockSpec((1, HEAD_W), lambda i: (0, 0)),
        ],
        out_specs=pl.BlockSpec((1, n, NSA_KV), lambda i: (i, 0, 0)),
        out_shape=jax.ShapeDtypeStruct((b, n, NSA_KV), F32),
        compiler_params=_params(("parallel",)),
        name="cmp_finish",
    )(uv, pe.reshape(1, -1), w1, w2, kn)


def compress(rows, layer, page_table, pe, w1, w2, k_norm):
    w1r = w1.reshape(2, CMP_STRIDE, HEAD_W, HEAD_W)
    wab = jnp.transpose(w1r, (1, 2, 0, 3)).reshape(CMP_PAIRS, 2 * HEAD_W, 2 * HEAD_W).astype(BF16)
    uv = cmp_uv(rows, layer, page_table, wab)
    return cmp_finish(uv, pe, w1.astype(BF16), w2.astype(BF16), k_norm)


def _importance_matrix(n_cmp, n_rows, n_slc, n_cols):
    w = np.zeros((n_rows, n_cols), np.float32)
    w_imp = [1.0] + [2.0] * (SLC_RATIO - 1) + [1.0]
    for s in range(n_slc):
        for m, wm in enumerate(w_imp):
            j = SLC_RATIO * s + m - 1
            if 0 <= j < n_cmp:
                w[j, s] += wm
    return w


def _select_blocks(s_slc, q_pos0, n_slc):
    t, w = s_slc.shape
    blk = lax.broadcasted_iota(jnp.int32, (t, w), 1)
    qpos = q_pos0 + lax.broadcasted_iota(jnp.int32, (t, w), 0)
    qb = qpos // SLC_BLOCK
    forced = (blk == 0) | (blk == qb) | (blk == qb - 1)
    score = jnp.where(forced, FORCE_SCORE, jnp.where(blk * SLC_BLOCK <= qpos, s_slc, -1.0))
    score = jnp.where(blk < n_slc, score, -2.0)
    cnt = jnp.zeros((t, w), F32)
    for sp in range(n_slc):
        col = score[:, sp:sp + 1]
        tie = jnp.where(blk > sp, 1.0, 0.0)
        cnt = cnt + jnp.where(col > score, 1.0, jnp.where(col == score, tie, 0.0))
    return jnp.where(cnt < N_SELECT, 1.0, 0.0)


def _masked_softmax(s, valid, axis=-1, base2=False):
    m = jnp.max(s, axis=axis, keepdims=True)
    e = jnp.where(valid, (jnp.exp2 if base2 else jnp.exp)(s - m), 0.0)
    return e / jnp.maximum(jnp.sum(e, axis=axis, keepdims=True), 1e-30)


def _reset_flash(m_ref, l_ref, acc_ref):
    m_ref[...] = jnp.full(m_ref.shape, NEG_INF, F32)
    l_ref[...] = jnp.zeros(l_ref.shape, F32)
    acc_ref[...] = jnp.zeros(acc_ref.shape, F32)


def _select_blocks_t(s_slc, q_pos0, n_slc):
    w, t = s_slc.shape
    n8 = -(-n_slc // 8) * 8
    blk = lax.broadcasted_iota(jnp.int32, (n8, t), 0)
    qpos = q_pos0 + lax.broadcasted_iota(jnp.int32, (n8, t), 1)
    qb = qpos // SLC_BLOCK
    forced = (blk == 0) | (blk == qb) | (blk == qb - 1)
    score = jnp.where(forced, FORCE_SCORE, jnp.where(blk * SLC_BLOCK <= qpos, s_slc[:n8], -1.0))
    score = jnp.where(blk < n_slc, score, -2.0)
    cnt = jnp.zeros((n8, t), F32)
    for sp in range(n_slc):
        row = score[sp:sp + 1, :]
        tie = jnp.where(blk > sp, 1.0, 0.0)
        cnt = cnt + jnp.where(row > score, 1.0, jnp.where(row == score, tie, 0.0))
    sel = jnp.where(cnt < N_SELECT, 1.0, 0.0)
    if n8 < w:
        sel = jnp.concatenate([sel, jnp.zeros((w - n8, t), F32)], axis=0)
    return sel


def _nsa_prompt_kernel(q_ref, gt_ref, kc_ref, vc_ref, ks_ref, vs_ref, kw_ref, vw_ref, bias_ref, u_ref, wimp_ref,
                       o_ref, qs_ref, ksb_ref, vst_ref, kwb_ref, vwt_ref, sel_ref, m_ref, l_ref, acc_ref,
                       *, tq, n_slc):
    qi = pl.program_id(2)

    @pl.when(qi == 0)
    def _():
        _stage_kv(ks_ref, vs_ref, ksb_ref, vst_ref, tq)
        _stage_kv(kw_ref, vw_ref, kwb_ref, vwt_ref, tq)

    qs_ref[...] = jnp.concatenate(
        [q_ref[0, :, r * HEAD_W:(r + 1) * HEAD_W] for r in range(HPG)], axis=0).astype(BF16)

    n_pad = kc_ref.shape[1]
    start = pl.multiple_of(n_pad - (tq // CMP_STRIDE) * (qi + 1), 8)
    scale = NSA_SCALE * LOG2E
    bias_c = u_ref[0, pl.ds(start, n_pad), :]
    s = _dot_t(kc_ref[0].astype(BF16), qs_ref[...]) * scale + bias_c
    pc = _masked_softmax(s, bias_c > 0.5 * NEG_INF, axis=0, base2=True)
    o_cmp = _dot(vc_ref[0].T.astype(BF16), pc.astype(BF16))
    psum = pc[:, 0:tq] + pc[:, tq:2 * tq] + pc[:, 2 * tq:3 * tq] + pc[:, 3 * tq:4 * tq]
    hi, mid, lo = _split3(psum)
    wimp = wimp_ref[...]
    s_slc = _dot(wimp, hi) + _dot(wimp, mid) + _dot(wimp, lo)
    sel_ref[...] = _select_blocks_t(s_slc, qi * tq, n_slc).astype(BF16)

    _reset_flash(m_ref, l_ref, acc_ref)

    far = bias_ref[2, 0:1, :]

    def key_mask(kt, width):
        key = lax.broadcasted_iota(jnp.int32, (width, LANE), 0)
        blk = lax.broadcasted_iota(jnp.int32, (width, LANE), 1)
        onehot = jnp.where(blk == kt * (width // SLC_BLOCK) + key // SLC_BLOCK, 1.0, 0.0).astype(BF16)
        return jnp.concatenate([_dot(onehot, sel_ref[...])] * HPG, axis=1)

    def slc_far_body(width):
        def body(kt, c):
            rows = pl.ds(pl.multiple_of(kt * width, width), width)
            _flash_tile_t(ksb_ref[rows, :], vst_ref[:, rows], qs_ref, m_ref, l_ref, acc_ref, scale=scale,
                          far_bias=far, mask=key_mask(kt, width))
            return c
        return body

    def slc_near_body(kt, c):
        rows = pl.ds(pl.multiple_of(kt * tq, tq), tq)
        typ = qi - kt
        _flash_tile_t(ksb_ref[rows, :], vst_ref[:, rows], qs_ref, m_ref, l_ref, acc_ref, scale=scale,
                      bias=lambda cols: bias_ref[typ, :, cols], mask=key_mask(kt, tq))
        return c

    n_far = jnp.maximum(qi - 1, 0)
    lax.fori_loop(0, n_far // FAR_TILES, slc_far_body(FAR_TILES * tq), 0)
    lax.fori_loop((n_far // FAR_TILES) * FAR_TILES, n_far, slc_far_body(tq), 0)
    lax.fori_loop(n_far, qi + 1, slc_near_body, 0)
    o_slc = acc_ref[...] / l_ref[...]

    _reset_flash(m_ref, l_ref, acc_ref)
    nw = WINDOW // tq

    def win_body(kt, c):
        rows = pl.ds(pl.multiple_of(kt * tq, tq), tq)
        t = qi - kt
        typ = jnp.where(t == nw, 3, jnp.minimum(t, 2))
        _flash_tile_t(kwb_ref[rows, :], vwt_ref[:, rows], qs_ref, m_ref, l_ref, acc_ref, scale=scale,
                      bias=lambda cols: bias_ref[typ, :, cols])
        return c

    lax.fori_loop(jnp.maximum(qi - nw, 0), qi + 1, win_body, 0)
    o_win = acc_ref[...] / l_ref[...]

    sig = jax.nn.sigmoid(gt_ref[0]).T
    for r in range(HPG):
        cs = slice(r * tq, (r + 1) * tq)
        o_t = (sig[r:r + 1, :] * o_cmp[:, cs] + sig[HPG + r:HPG + r + 1, :] * o_slc[:, cs]
               + sig[2 * HPG + r:2 * HPG + r + 1, :] * o_win[:, cs])
        o_ref[0, :, r * HEAD_W:(r + 1) * HEAD_W] = o_t.T


def _prompt_cmp_buckets(tq, n_pad):
    jp = np.arange(2 * n_pad)[:, None] - (n_pad - tq // CMP_STRIDE)
    i = np.arange(tq)[None, :]
    return _bucket_np(i - CMP_STRIDE * jp - (2 * CMP_STRIDE - 1))[None].astype(np.int32)


def nsa_prompt(proj, gates, kc, vc, bias, u_bias, tq=PROMPT_TQ):
    b, t, _ = proj.shape
    gw = HPG * HEAD_W
    n_pad = kc.shape[1]
    n_slc = t // SLC_BLOCK
    wimp = jnp.asarray(_importance_matrix(n_pad - 1, n_pad, n_slc, LANE).T, BF16)
    rows = HPG * tq
    seq = lambda n: pl.BlockSpec((1, n, HEAD_W), lambda i, g, j: (i, 0, g))
    kv = lambda a: pl.BlockSpec((1, t, HEAD_W), lambda i, g, j: (i, 0, N_HEADS + a * N_KV + g))
    return pl.pallas_call(
        functools.partial(_nsa_prompt_kernel, tq=tq, n_slc=n_slc),
        grid=(b, N_KV, t // tq),
        in_specs=[
            pl.BlockSpec((1, tq, gw), lambda i, g, j: (i, j, g)),
            pl.BlockSpec((1, tq, LANE), lambda i, g, j: (i, j, g)),
            seq(n_pad), seq(n_pad), kv(2), kv(3), kv(4), kv(5),
            pl.BlockSpec((4, tq, rows), lambda i, g, j: (0, 0, g)),
            pl.BlockSpec((1, 2 * n_pad, rows), lambda i, g, j: (0, 0, g)),
            pl.BlockSpec(wimp.shape, lambda i, g, j: (0, 0)),
        ],
        out_specs=pl.BlockSpec((1, tq, gw), lambda i, g, j: (i, j, g)),
        out_shape=jax.ShapeDtypeStruct((b, t, N_HEADS * HEAD_W), F32),
        scratch_shapes=[
            pltpu.VMEM((rows, HEAD_W), BF16),
            pltpu.VMEM((t, HEAD_W), BF16),
            pltpu.VMEM((HEAD_W, t), BF16),
            pltpu.VMEM((t, HEAD_W), BF16),
            pltpu.VMEM((HEAD_W, t), BF16),
            pltpu.VMEM((LANE, tq), BF16),
            pltpu.VMEM((1, rows), F32),
            pltpu.VMEM((1, rows), F32),
            pltpu.VMEM((HEAD_W, rows), F32),
        ],
        compiler_params=_params(("parallel", "parallel", "arbitrary")),
        name="nsa_prompt",
    )(proj, gates, kc, vc, proj, proj, proj, proj, bias, u_bias, wimp)


def _decode_cmp_buckets(t_new, past_len):
    n_pad = past_len // CMP_STRIDE
    i = np.arange(t_new)[:, None]
    j = n_pad - LANE + np.arange(LANE)[None, :]
    last = _bucket_np(past_len + i - CMP_STRIDE * j - (2 * CMP_STRIDE - 1))
    last = np.where(j < n_pad - 1, last, -1)
    return np.stack([np.full((t_new, LANE), N_BUCKETS - 1, np.int32), last]).astype(np.int32)


def _decode_win_buckets(t_new, n_tiles):
    i = np.arange(t_new)[:, None]
    idx = np.arange(n_tiles * LANE)[None, :]
    dw = WINDOW + i - idx
    ok = (dw >= 0) & (dw < WINDOW) & (idx < WINDOW + t_new)
    b = np.where(ok, _bucket_np(dw), -1)
    return np.stack([b[:, k * LANE:(k + 1) * LANE] for k in range(n_tiles)]).astype(np.int32)


def _nsa_decode_kernel(pt_ref, q_ref, gt_ref, kc_ref, vc_ref, kn_ref, vn_ref, kw_ref, vw_ref, *rest,
                       n_pages, past_len):
    del pt_ref
    kp = rest[:PAGES_PER_STEP]
    vp = rest[PAGES_PER_STEP:2 * PAGES_PER_STEP]
    (bias_ref, bias_c_ref, bias_w_ref, wimp_ref, o_ref,
     qr_ref, selt_ref, ocmp_ref, owin_ref, m_ref, l_ref, acc_ref) = rest[2 * PAGES_PER_STEP:]
    s = pl.program_id(1)
    t_new = q_ref.shape[1]
    rows = HPG * t_new
    n_slc = -(-(past_len + t_new) // SLC_BLOCK)

    @pl.when(s == 0)
    def _():
        _reset_flash(m_ref, l_ref, acc_ref)
        qr_ref[...] = _query_rows(q_ref, 1, NSA_SCALE)
        n_pad = kc_ref.shape[1]
        sel_t = jnp.zeros(selt_ref.shape, F32)
        for g in range(N_KV):
            sl = slice(g * HEAD_W, (g + 1) * HEAD_W)
            gr = slice(g * rows, (g + 1) * rows)
            qx = _stack_group_heads(q_ref, g).astype(BF16)
            bias_c = jnp.concatenate([bias_c_ref[0, gr, :]] * (n_pad // LANE - 1) + [bias_c_ref[1, gr, :]], axis=1)
            sc = _dot_t(qx, kc_ref[0, :, sl].astype(BF16)) * NSA_SCALE + bias_c
            pc = _masked_softmax(sc, bias_c > 0.5 * NEG_INF)
            ocmp_ref[g] = _dot(pc.astype(BF16), vc_ref[0, :, sl].astype(BF16))
            psum = sum(pc[r * t_new:(r + 1) * t_new] for r in range(1, HPG)) + pc[0:t_new]
            s_slc = _dot_exact_rhs(psum, wimp_ref[...])
            sel = _select_blocks(s_slc, past_len, n_slc)
            sel_pad = jnp.concatenate([sel, jnp.zeros((LANE - t_new, sel.shape[1]), F32)], axis=0).T
            tok = lax.broadcasted_iota(jnp.int32, (LANE, LANE), 0)
            lane = lax.broadcasted_iota(jnp.int32, (LANE, LANE), 1)
            spread = jnp.where((lane % t_new == tok) & (lane // rows == g), 1.0, 0.0).astype(BF16)
            sel_t = sel_t + _dot(sel_pad.astype(BF16), spread)
            n_wt = bias_w_ref.shape[0]
            bias_w = jnp.concatenate([bias_w_ref[k, gr, :] for k in range(n_wt)], axis=1)
            sw = _dot_t(qx, kw_ref[0, :, sl].astype(BF16)) * NSA_SCALE + bias_w
            pw = _masked_softmax(sw, bias_w > 0.5 * NEG_INF)
            owin_ref[g] = _dot(pw.astype(BF16), vw_ref[0, :, sl].astype(BF16))
        selt_ref[...] = sel_t
        _decode_step_t(kn_ref[0], vn_ref[0], qr_ref, bias_ref[2], m_ref, l_ref, acc_ref)

    block_rows = SLC_BLOCK * N_KV
    first_block = s * (PAGES_PER_STEP * PAGE // SLC_BLOCK)
    keep = jnp.concatenate(
        [jnp.broadcast_to(selt_ref[pl.ds(first_block + b, 1), :], (block_rows, selt_ref.shape[1]))
         for b in range(PAGES_PER_STEP * PAGE // SLC_BLOCK)], axis=0)
    _decode_step_t(_page_rows(kp), _page_rows(vp), qr_ref,
                   _page_bias_t(bias_ref, s * PAGES_PER_STEP, n_pages, 1), m_ref, l_ref, acc_ref, keep=keep)

    @pl.when(s == pl.num_programs(1) - 1)
    def _():
        sig = jax.nn.sigmoid(gt_ref[0])
        o_slc_all = (acc_ref[...] / l_ref[...]).T
        for g in range(N_KV):
            o_cmp = ocmp_ref[g]
            o_win = owin_ref[g]
            for r in range(HPG):
                h = g * HPG + r
                rs = slice(r * t_new, (r + 1) * t_new)
                c0 = g * LANE + r
                o_ref[0, :, h * HEAD_W:(h + 1) * HEAD_W] = (
                    sig[:, c0:c0 + 1] * o_cmp[rs]
                    + sig[:, c0 + HPG:c0 + HPG + 1] * o_slc_all[h * t_new:(h + 1) * t_new]
                    + sig[:, c0 + 2 * HPG:c0 + 2 * HPG + 1] * o_win[rs])


def nsa_decode(q, gates, kc, vc, k_new, v_new, kw_src, vw_src, cache_k, cache_v, layer, page_table,
               bias, bias_c, bias_w, past_len):
    b, t_new, _ = q.shape
    n_pages = page_table.shape[1]
    rows = HPG * t_new
    lanes = N_HEADS * t_new
    assert lanes == LANE
    n_pad = kc.shape[1]
    n_slc = -(-(past_len + t_new) // SLC_BLOCK)
    n_cols = -(-n_slc // LANE) * LANE
    wimp = jnp.asarray(_importance_matrix(n_pad - 1, n_pad, n_slc, n_cols), BF16)
    fixed = lambda *shape: pl.BlockSpec(shape, lambda i, s, pt: (0,) * len(shape))
    per_b = lambda *shape: pl.BlockSpec((1,) + shape, lambda i, s, pt: (i,) + (0,) * len(shape))
    grid_spec = pltpu.PrefetchScalarGridSpec(
        num_scalar_prefetch=1,
        grid=(b, n_pages // PAGES_PER_STEP),
        in_specs=[per_b(t_new, N_HEADS * HEAD_W), per_b(t_new, N_KV * LANE),
                  per_b(n_pad, NSA_KV), per_b(n_pad, NSA_KV), per_b(PAGE_ROWS, HEAD_W), per_b(PAGE_ROWS, HEAD_W),
                  per_b(kw_src.shape[1], NSA_KV), per_b(kw_src.shape[1], NSA_KV)]
        + _page_specs(layer, 2)
        + [fixed(*bias.shape), fixed(*bias_c.shape), fixed(*bias_w.shape), fixed(*wimp.shape)],
        out_specs=per_b(t_new, N_HEADS * HEAD_W),
        scratch_shapes=[
            pltpu.VMEM((lanes, HEAD_W), BF16),
            pltpu.VMEM((n_cols, lanes), F32),
            pltpu.VMEM((N_KV, rows, HEAD_W), F32),
            pltpu.VMEM((N_KV, rows, HEAD_W), F32),
            pltpu.VMEM((1, lanes), F32),
            pltpu.VMEM((1, lanes), F32),
            pltpu.VMEM((HEAD_W, lanes), F32),
        ],
    )
    return pl.pallas_call(
        functools.partial(_nsa_decode_kernel, n_pages=n_pages, past_len=past_len),
        grid_spec=grid_spec,
        out_shape=jax.ShapeDtypeStruct(q.shape[:2] + (N_HEADS * HEAD_W,), F32),
        compiler_params=_params(("parallel", "arbitrary")),
        name="nsa_decode",
    )(page_table, q, gates, kc, vc, k_new, v_new, kw_src, vw_src,
      *([cache_k] * PAGES_PER_STEP), *([cache_v] * PAGES_PER_STEP), bias, bias_c, bias_w, wimp)


def _gate_weights(w_g):
    d = w_g.shape[0]
    w = jnp.transpose(w_g.reshape(d, 3, N_KV, HPG), (0, 2, 1, 3)).reshape(d, N_KV, 3 * HPG)
    return jnp.pad(w, ((0, 0), (0, 0), (0, LANE - 3 * HPG))).reshape(d, N_KV * LANE)


def nsa_layer(xp, xs, gain, w_in, q_norm, k_norm, pe, w1, w2, w_out, caches, layer, page_table,
              bias_p, u_bias, bias_s, bias_sc, bias_sw, past_len):
    cmp_k, cmp_v, slc_k, slc_v, win_k, win_v = caches
    nqkv = N_HEADS * HEAD_W + 6 * NSA_KV
    w_g = _gate_weights(w_in[layer, :, nqkv:])[None]
    w_in_t = jnp.swapaxes(w_in, 1, 2)
    nq = N_HEADS * HEAD_W
    tile = 4 * LANE
    q_gain = jnp.tile(q_norm, tile // HEAD_W)
    head_norm = (HEAD_W, {**{c: q_gain for c in range(nq // tile)},
                          nq // tile + 2: jnp.tile(k_norm[1], tile // HEAD_W),
                          nq // tile + 4: jnp.tile(k_norm[2], tile // HEAD_W)})
    outs = []
    for x, paged in ((xp, False), (xs, True)):
        b, t, d = x.shape
        x2 = x.reshape(b * t, d)
        proj = dense(x2, w_in_t, layer, gain=gain, n=nqkv, w_t=True, head_norm=head_norm).reshape(b, t, -1)
        gates = dense(x2, w_g, 0, gain=gain).reshape(b, t, -1)
        kcr, vcr, ks, vs, kw, vw = (proj[:, :, nq + a * NSA_KV:nq + (a + 1) * NSA_KV] for a in range(6))
        q = proj
        if paged:
            kc = compress(_as_page_rows(cmp_k), layer, page_table, pe[0], w1[0], w2[0], k_norm[0])
            vc = compress(_as_page_rows(cmp_v), layer, page_table, pe[1], w1[1], w2[1], None)
            n_wt = bias_sw.shape[0]
            kw_src = jnp.concatenate([win_k[layer], kw], axis=1)
            vw_src = jnp.concatenate([win_v[layer], vw], axis=1)
            o = nsa_decode(q, gates, kc, vc, _new_page_rows(ks), _new_page_rows(vs),
                           _pad_rows(kw_src, n_wt * LANE), _pad_rows(vw_src, n_wt * LANE),
                           _as_page_rows(slc_k), _as_page_rows(slc_v), layer, page_table,
                           bias_s, bias_sc, bias_sw, past_len)
            kw_out, vw_out = kw_src[:, -WINDOW:], vw_src[:, -WINDOW:]
        else:
            n_pg = t // PAGE
            ident = jnp.arange(b * n_pg, dtype=jnp.int32).reshape(b, n_pg)
            as_pages = lambda a: a.reshape(1, b * n_pg, PAGE_ROWS, HEAD_W)
            kc = compress(as_pages(kcr), 0, ident, pe[0], w1[0], w2[0], k_norm[0])
            vc = compress(as_pages(vcr), 0, ident, pe[1], w1[1], w2[1], None)
            o = nsa_prompt(proj, gates, kc, vc, bias_p, u_bias)
            kw_out, vw_out = kw[:, -WINDOW:], vw[:, -WINDOW:]
        y = dense(o.reshape(b * t, -1), w_out, layer, res=x2).reshape(b, t, d)
        st = tuple(a.reshape(b, -1, N_KV, HEAD_W) for a in (kcr, vcr, ks, vs, kw_out, vw_out))
        outs.append((y, st))
    (yp, stp), (ys, sts) = outs
    return yp, ys, stp, sts


def kernel(x_prompt, x_sample, cache_diff_k, cache_diff_v, state_ssm, state_conv, cache_nsa_cmp_k, cache_nsa_cmp_v, cache_nsa_slc_k, cache_nsa_slc_v, cache_nsa_win_k, cache_nsa_win_v, cache_mem_k, cache_mem_v, page_table, mem_prompt, rel_bias_table, norm_mix, norm_xattn, norm_mem, norm_ffn, diff_w_in, diff_q_norm, diff_k_norm, diff_lambda, diff_sub_norm, diff_w_out, ssm_w_in, ssm_conv_w, ssm_conv_b, ssm_dt_bias, ssm_a_log, ssm_d, ssm_norm, ssm_w_out, nsa_w_in, nsa_q_norm, nsa_k_norm, nsa_cmp_pe, nsa_cmp_w1, nsa_cmp_w2, nsa_w_out, xattn_w_q, xattn_w_k, xattn_w_v, xattn_q_norm, xattn_k_norm, xattn_w_o, ffn_w1, ffn_w3, ffn_w2):
    xp, xs = x_prompt, x_sample
    bp, t, d = xp.shape
    bs, t_new, _ = xs.shape
    past_len = page_table.shape[1] * PAGE
    depth = norm_mix.shape[0]

    bias_p = bias_tiles(_prompt_attn_buckets(PROMPT_TQ), rel_bias_table, heads_on_lanes=True, scale=LOG2E)
    u_bias = bias_tiles(_prompt_cmp_buckets(PROMPT_TQ, t // CMP_STRIDE), rel_bias_table, heads_on_lanes=True,
                        scale=LOG2E)
    bias_s = bias_tiles_indexed(_decode_page_index(t_new, past_len), rel_bias_table)
    bias_sc = bias_tiles(_decode_cmp_buckets(t_new, past_len), rel_bias_table)
    n_wt = -(-(WINDOW + t_new) // LANE)
    bias_sw = bias_tiles(_decode_win_buckets(t_new, n_wt), rel_bias_table)

    mem_k, mem_v = mem_kv(mem_prompt, norm_mem, xattn_w_k, xattn_w_v, xattn_k_norm)
    xw = X_HEADS * X_DH
    win_k = cache_nsa_win_k.reshape(*cache_nsa_win_k.shape[:3], NSA_KV)
    win_v = cache_nsa_win_v.reshape(*cache_nsa_win_v.shape[:3], NSA_KV)

    dkp, dvp, dks, dvs = [], [], [], []
    ssp, cvp, sss, cvs = [], [], [], []
    nsp, nss = [], []
    for i in range(depth):
        kind, j = i % 3, i // 3
        if kind == 0:
            lam_init = 0.8 - 0.6 * math.exp(-0.3 * i)
            xp, xs, kp_, vp_, ks_, vs_ = diff_layer(
                xp, xs, norm_mix[i], diff_w_in, diff_q_norm[j], diff_k_norm[j], diff_lambda[j],
                diff_sub_norm[j], diff_w_out, cache_diff_k, cache_diff_v, j, page_table,
                bias_p, bias_s, lam_init)
            dkp.append(kp_)
            dvp.append(vp_)
            dks.append(ks_)
            dvs.append(vs_)
        elif kind == 1:
            xp, xs, hp_, cp_, hs_, cs_ = ssd_layer(
                xp, xs, norm_mix[i], ssm_w_in, j, ssm_conv_w[j], ssm_conv_b[j], ssm_dt_bias[j], ssm_a_log[j],
                ssm_d[j], ssm_norm[j], ssm_w_out, state_conv[j], state_ssm[j])
            ssp.append(hp_)
            cvp.append(cp_)
            sss.append(hs_)
            cvs.append(cs_)
        else:
            caches = (cache_nsa_cmp_k, cache_nsa_cmp_v, cache_nsa_slc_k, cache_nsa_slc_v, win_k, win_v)
            xp, xs, stp, sts = nsa_layer(
                xp, xs, norm_mix[i], nsa_w_in, nsa_q_norm[j], nsa_k_norm[j], nsa_cmp_pe[j], nsa_cmp_w1[j],
                nsa_cmp_w2[j], nsa_w_out, caches, j, page_table,
                bias_p, u_bias, bias_s, bias_sc, bias_sw, past_len)
            nsp.append(stp)
            nss.append(sts)
        xp = xattn(xp, norm_xattn[i], xattn_w_q, xattn_q_norm[i], mem_k[i], mem_v[i], xattn_w_o, i)
        xs = xattn(xs, norm_xattn[i], xattn_w_q, xattn_q_norm[i], cache_mem_k[i].reshape(bs, N_MEM, xw),
                   cache_mem_v[i].reshape(bs, N_MEM, xw), xattn_w_o, i)
        xp = ffn(xp.reshape(bp * t, d), norm_ffn[i], ffn_w1, ffn_w3, ffn_w2, i).reshape(bp, t, d)
        xs = ffn(xs.reshape(bs * t_new, d), norm_ffn[i], ffn_w1, ffn_w3, ffn_w2, i).reshape(bs, t_new, d)

    st = lambda xs_: jnp.stack(xs_, axis=0)
    nsp_t = [st([s[a] for s in nsp]) for a in range(6)]
    nss_t = [st([s[a] for s in nss]) for a in range(6)]
    mem_shape = (depth, bp, N_MEM, X_HEADS, X_DH)
    return (xp, xs,
            st(dkp), st(dvp), st(dks), st(dvs),
            st(ssp), st(cvp), st(sss), st(cvs),
            *nsp_t, *nss_t,
            mem_k.reshape(mem_shape), mem_v.reshape(mem_shape))
```

```python
import functools
import math

import numpy as np
import jax
import jax.numpy as jnp
from jax import lax
from jax.experimental import pallas as pl
from jax.experimental.pallas import tpu as pltpu

F32 = jnp.float32
BF16 = jnp.bfloat16

D_MODEL = 2048
DEPTH = 4
PAGE = 128
N_HEADS = 16
N_KV = 4
HPG = N_HEADS // N_KV
DIFF_DH = 64
HEAD_W = 128
N_BUCKETS = 32
MAX_EXACT = 16
MAX_DIST = 128
D_INNER = 2 * D_MODEL
SSM_HEADDIM = 64
SSM_HEADS = D_INNER // SSM_HEADDIM
SSM_GROUPS = 8
SSM_HPG = SSM_HEADS // SSM_GROUPS
D_STATE = 128
CONV_W = 4
CONV_DIM = D_INNER + 2 * SSM_GROUPS * D_STATE
SSM_CHUNK = 128
CMP_STRIDE = 16
SLC_BLOCK = 64
SLC_RATIO = SLC_BLOCK // CMP_STRIDE
N_SELECT = 16
WINDOW = 512
N_MEM = 256
X_HEADS = 4
X_DH = 128
EPS = 1e-6
NEG_INF = -1e30
FORCE_SCORE = 1e4

LANE = 128
VMEM_LIMIT = 56 * 1024 * 1024
DENSE_TALL_ROWS = 2048


def _params(sem):
    return pltpu.CompilerParams(dimension_semantics=sem, vmem_limit_bytes=VMEM_LIMIT)


def _rms(x, gain):
    return x * lax.rsqrt(jnp.mean(x * x, axis=-1, keepdims=True) + EPS) * gain


def _dot(a, b):
    return jnp.dot(a, b, preferred_element_type=F32)


def _dot_t(a, b):
    return lax.dot_general(a, b, (((1,), (1,)), ((), ())), preferred_element_type=F32)


def _split3(x):
    hi = x.astype(BF16)
    r1 = x - hi.astype(F32)
    mid = r1.astype(BF16)
    lo = (r1 - mid.astype(F32)).astype(BF16)
    return hi, mid, lo


def _dot_exact_rhs(x, m_bf16):
    hi, mid, lo = _split3(x)
    return _dot(hi, m_bf16) + _dot(mid, m_bf16) + _dot(lo, m_bf16)


def _norm64(blk, gain, lo):
    sq = blk * blk
    s_lo = jnp.sum(jnp.where(lo, sq, 0.0), axis=-1, keepdims=True)
    s_hi = jnp.sum(jnp.where(lo, 0.0, sq), axis=-1, keepdims=True)
    ms = jnp.where(lo, s_lo, s_hi) * (1.0 / DIFF_DH)
    return blk * lax.rsqrt(ms + EPS) * gain


def _head_norm_tile(y, gain, seg):
    lo = lax.broadcasted_iota(jnp.int32, (y.shape[0], LANE), 1) < DIFF_DH
    blocks = []
    for c in range(y.shape[1] // LANE):
        sl = slice(c * LANE, (c + 1) * LANE)
        blocks.append(_norm64(y[:, sl], gain[:, sl], lo) if seg == DIFF_DH else _rms(y[:, sl], gain[:, sl]))
    return jnp.concatenate(blocks, axis=1)


def _dense_kernel(*refs, norm, residual, w_t, head_seg, norm_tiles):
    refs = list(refs)
    x_ref = refs.pop(0)
    g_ref = refs.pop(0) if norm else None
    w_ref = refs.pop(0)
    r_ref = refs.pop(0) if residual else None
    hg_ref = refs.pop(0) if head_seg else None
    o_ref, xb_ref = refs
    j = pl.program_id(1)

    @pl.when(j == 0)
    def _():
        x = x_ref[...]
        if norm:
            x = _rms(x, g_ref[...])
        xb_ref[...] = x.astype(BF16)

    w = w_ref[0].astype(BF16)
    y = _dot_t(xb_ref[...], w) if w_t else _dot(xb_ref[...], w)
    if residual:
        y = y + r_ref[...]
    if not head_seg:
        o_ref[...] = y
        return
    is_norm = functools.reduce(jnp.logical_or, [j == t for t in norm_tiles])

    @pl.when(is_norm)
    def _():
        o_ref[...] = _head_norm_tile(y, hg_ref[0], head_seg)

    @pl.when(jnp.logical_not(is_norm))
    def _():
        o_ref[...] = y


def dense(x, w, layer, gain=None, res=None, n=None, w_t=False, head_norm=None):
    m, k = x.shape
    n = w.shape[1 if w_t else 2] if n is None else n
    tm = m if m <= 512 else (DENSE_TALL_ROWS if k <= D_MODEL and m % DENSE_TALL_ROWS == 0 else 512)
    tn = LANE if n % 256 else (512 if n % 512 == 0 else 256)
    assert m % tm == 0 and n % tn == 0
    norm, residual = gain is not None, res is not None
    x_mode = dict(pipeline_mode=pl.Buffered(1)) if tm == DENSE_TALL_ROWS else {}
    in_specs = [pl.BlockSpec((tm, k), lambda i, j: (i, 0), **x_mode)]
    args = [x]
    if norm:
        in_specs.append(pl.BlockSpec((1, k), lambda i, j: (0, 0)))
        args.append(gain.reshape(1, k))
    if w_t:
        in_specs.append(pl.BlockSpec((1, tn, k), lambda i, j: (layer, j, 0)))
    else:
        in_specs.append(pl.BlockSpec((1, k, tn), lambda i, j: (layer, 0, j)))
    args.append(w)
    if residual:
        in_specs.append(pl.BlockSpec((tm, tn), lambda i, j: (i, j)))
        args.append(res)
    head_seg, norm_tiles = 0, ()
    if head_norm is not None:
        head_seg, tile_gains = head_norm
        norm_tiles = tuple(sorted(tile_gains))
        ones = jnp.ones((tn,), F32)
        gains = jnp.stack([tile_gains.get(t, ones) for t in range(n // tn)]).reshape(n // tn, 1, tn)
        in_specs.append(pl.BlockSpec((1, 1, tn), lambda i, j: (j, 0, 0)))
        args.append(gains)
    return pl.pallas_call(
        functools.partial(_dense_kernel, norm=norm, residual=residual, w_t=w_t, head_seg=head_seg,
                          norm_tiles=norm_tiles),
        grid=(m // tm, n // tn),
        in_specs=in_specs,
        out_specs=pl.BlockSpec((tm, tn), lambda i, j: (i, j)),
        out_shape=jax.ShapeDtypeStruct((m, n), F32),
        scratch_shapes=[pltpu.VMEM((tm, k), BF16)],
        compiler_params=_params(("parallel", "arbitrary")),
        name="dense",
    )(*args)


def _ffn_kernel(x_ref, g_ref, w1_ref, w3_ref, w2_ref, o_ref, xb_ref):
    @pl.when(pl.program_id(1) == 0)
    def _():
        x = x_ref[...]
        xb_ref[...] = _rms(x, g_ref[...]).astype(BF16)
        o_ref[...] = x

    xb = xb_ref[...]
    h1 = _dot(xb, w1_ref[0].astype(BF16))
    h3 = _dot(xb, w3_ref[0].astype(BF16))
    a = (h1 * jax.nn.sigmoid(h1) * h3).astype(BF16)
    o_ref[...] += _dot(a, w2_ref[0].astype(BF16))


def ffn(x, gain, w1, w3, w2, layer):
    m, d = x.shape
    f = w1.shape[2]
    tm = min(m, 1024)
    tf = 256
    assert m % tm == 0 and f % tf == 0
    return pl.pallas_call(
        _ffn_kernel,
        grid=(m // tm, f // tf),
        in_specs=[
            pl.BlockSpec((tm, d), lambda i, j: (i, 0), pipeline_mode=pl.Buffered(1)),
            pl.BlockSpec((1, d), lambda i, j: (0, 0)),
            pl.BlockSpec((1, d, tf), lambda i, j: (layer, 0, j)),
            pl.BlockSpec((1, d, tf), lambda i, j: (layer, 0, j)),
            pl.BlockSpec((1, tf, d), lambda i, j: (layer, j, 0)),
        ],
        out_specs=pl.BlockSpec((tm, d), lambda i, j: (i, 0)),
        out_shape=jax.ShapeDtypeStruct((m, d), F32),
        scratch_shapes=[pltpu.VMEM((tm, d), BF16)],
        compiler_params=_params(("parallel", "arbitrary")),
        name="ffn",
    )(x, gain.reshape(1, d), w1, w3, w2)


def _mem_kv_kernel(mem_ref, g_ref, wk_ref, wv_ref, kn_ref, k_ref, v_ref):
    m = _rms(mem_ref[0], g_ref[0]).astype(BF16)
    k = _dot(m, wk_ref[0].astype(BF16))
    v_ref[0, 0] = _dot(m, wv_ref[0].astype(BF16))
    for h in range(X_HEADS):
        sl = slice(h * X_DH, (h + 1) * X_DH)
        k_ref[0, 0, :, sl] = _rms(k[:, sl], kn_ref[0])


def mem_kv(mem, g_mem, wk, wv, k_norm):
    b = mem.shape[0]
    nl = wk.shape[0]
    hw = X_HEADS * X_DH
    shape = jax.ShapeDtypeStruct((nl, b, N_MEM, hw), F32)
    return pl.pallas_call(
        _mem_kv_kernel,
        grid=(nl, b),
        in_specs=[
            pl.BlockSpec((1, N_MEM, D_MODEL), lambda l, i: (i, 0, 0)),
            pl.BlockSpec((1, 1, D_MODEL), lambda l, i: (l, 0, 0)),
            pl.BlockSpec((1, D_MODEL, hw), lambda l, i: (l, 0, 0)),
            pl.BlockSpec((1, D_MODEL, hw), lambda l, i: (l, 0, 0)),
            pl.BlockSpec((1, 1, X_DH), lambda l, i: (l, 0, 0)),
        ],
        out_specs=[pl.BlockSpec((1, 1, N_MEM, hw), lambda l, i: (l, i, 0, 0))] * 2,
        out_shape=[shape, shape],
        compiler_params=_params(("parallel", "parallel")),
        name="mem_kv",
    )(mem, g_mem.reshape(nl, 1, D_MODEL), wk, wv, k_norm.reshape(nl, 1, X_DH))


def _xattn_kernel(x_ref, g_ref, wq_ref, qn_ref, k_ref, v_ref, wo_ref, o_ref):
    x = x_ref[0]
    q = _dot(_rms(x, g_ref[...]).astype(BF16), wq_ref[0].astype(BF16))
    outs = []
    for h in range(X_HEADS):
        sl = slice(h * X_DH, (h + 1) * X_DH)
        qh = _rms(q[:, sl], qn_ref[...]).astype(BF16)
        s = _dot_t(qh, k_ref[0, :, sl].astype(BF16)) * (X_DH ** -0.5)
        e = jnp.exp(s - jnp.max(s, axis=-1, keepdims=True))
        p = e / jnp.sum(e, axis=-1, keepdims=True)
        outs.append(_dot(p.astype(BF16), v_ref[0, :, sl].astype(BF16)))
    o = jnp.concatenate(outs, axis=1).astype(BF16)
    o_ref[0] = x + _dot(o, wo_ref[0].astype(BF16))


def xattn(x, gain, wq, q_norm, k, v, wo, layer):
    b, t, d = x.shape
    hw = X_HEADS * X_DH
    tm = min(t, 512)
    return pl.pallas_call(
        _xattn_kernel,
        grid=(b, t // tm),
        in_specs=[
            pl.BlockSpec((1, tm, d), lambda i, j: (i, j, 0)),
            pl.BlockSpec((1, d), lambda i, j: (0, 0)),
            pl.BlockSpec((1, d, hw), lambda i, j: (layer, 0, 0)),
            pl.BlockSpec((1, X_DH), lambda i, j: (0, 0)),
            pl.BlockSpec((1, N_MEM, hw), lambda i, j: (i, 0, 0)),
            pl.BlockSpec((1, N_MEM, hw), lambda i, j: (i, 0, 0)),
            pl.BlockSpec((1, hw, d), lambda i, j: (layer, 0, 0)),
        ],
        out_specs=pl.BlockSpec((1, tm, d), lambda i, j: (i, j, 0)),
        out_shape=jax.ShapeDtypeStruct((b, t, d), F32),
        compiler_params=_params(("parallel", "parallel")),
        name="xattn",
    )(x, gain.reshape(1, d), wq, q_norm.reshape(1, X_DH), k, v, wo)


def _bucket_np(dist):
    n = np.maximum(dist, 0)
    nf = np.maximum(n, 1).astype(np.float64)
    large = MAX_EXACT + (np.log(nf / MAX_EXACT) / math.log(MAX_DIST / MAX_EXACT)
                         * (N_BUCKETS - MAX_EXACT)).astype(np.int64)
    b = np.where(n < MAX_EXACT, n, np.minimum(large, N_BUCKETS - 1))
    return np.where(dist < 0, -1, b).astype(np.int32)


def _bias_kernel(bkt_ref, tab_ref, o_ref, *, scale):
    h = pl.program_id(1)
    b = bkt_ref[0]
    acc = jnp.full(b.shape, NEG_INF, F32)
    for k in range(N_BUCKETS):
        acc = jnp.where(b == k, tab_ref[k, h] * scale, acc)
    o_ref[0] = acc


def _bias_packed_kernel(idx_ref, tab_ref, o_ref, *, scale):
    idx = idx_ref[0]
    acc = jnp.full(idx.shape, NEG_INF, F32)
    for k in range(N_BUCKETS):
        for h in range(N_HEADS):
            acc = jnp.where(idx == k * N_HEADS + h, tab_ref[k, h] * scale, acc)
    o_ref[0] = acc


def bias_tiles_indexed(idx, table, scale=1.0):
    nt, r, w = idx.shape
    return pl.pallas_call(
        functools.partial(_bias_packed_kernel, scale=scale),
        grid=(nt,),
        in_specs=[
            pl.BlockSpec((1, r, w), lambda t: (t, 0, 0)),
            pl.BlockSpec(memory_space=pltpu.SMEM),
        ],
        out_specs=pl.BlockSpec((1, r, w), lambda t: (t, 0, 0)),
        out_shape=jax.ShapeDtypeStruct((nt, r, w), F32),
        compiler_params=_params(("parallel",)),
        name="bias_tiles_indexed",
    )(jnp.asarray(idx.astype(np.int32)), table)


def bias_tiles(buckets, table, heads_on_lanes=False, scale=1.0):
    nt, r, w = buckets.shape
    if heads_on_lanes:
        out_spec = pl.BlockSpec((1, r, w), lambda t, h: (t, 0, h))
        out_shape = (nt, r, N_HEADS * w)
    else:
        out_spec = pl.BlockSpec((1, r, w), lambda t, h: (t, h, 0))
        out_shape = (nt, N_HEADS * r, w)
    return pl.pallas_call(
        functools.partial(_bias_kernel, scale=scale),
        grid=(nt, N_HEADS),
        in_specs=[
            pl.BlockSpec((1, r, w), lambda t, h: (t, 0, 0)),
            pl.BlockSpec(memory_space=pltpu.SMEM),
        ],
        out_specs=out_spec,
        out_shape=jax.ShapeDtypeStruct(out_shape, F32),
        compiler_params=_params(("parallel", "parallel")),
        name="bias_tiles",
    )(jnp.asarray(buckets), table)


def _prompt_attn_buckets(tq):
    assert WINDOW % tq == 0 and tq >= MAX_DIST
    j = np.arange(tq)[:, None]
    i = np.arange(tq)[None, :]
    far = np.full((tq, tq), N_BUCKETS - 1, np.int32)
    return np.stack([
        _bucket_np(i - j),
        _bucket_np(i - j + tq),
        far,
        np.where(j > i, far, -1),
    ]).astype(np.int32)


def _diff_lambda(lam_ref, lam_init):
    lf = lam_ref[...]
    s01 = jnp.sum(lf[0:1] * lf[1:2], axis=-1, keepdims=True)
    s23 = jnp.sum(lf[2:3] * lf[3:4], axis=-1, keepdims=True)
    return jnp.exp(s01) - jnp.exp(s23) + lam_init


def _split_maps(qs):
    lo = lax.broadcasted_iota(jnp.int32, qs.shape, 1) < DIFF_DH
    return jnp.concatenate([jnp.where(lo, qs, 0.0), jnp.where(lo, 0.0, qs)], axis=0)


def _diff_finish_rows(o, lam, sub_norm, lam_init):
    r = o.shape[0] // 2
    a = o[:r] - lam * o[r:]
    return _rms(a, sub_norm) * (1.0 - lam_init)


PROMPT_TQ = 256


LOG2E = math.log2(math.e)
FLASH_CHUNKS = 4
FAR_TILES = 2


def _flash_tile_t(k_tile, vt_tile, qs_ref, m_ref, l_ref, acc_ref, scale=None, bias=None, far_bias=None,
                  mask=None, n_chunks=FLASH_CHUNKS):
    chunk = qs_ref.shape[0] // n_chunks
    logits = [_dot_t(k_tile, qs_ref[c * chunk:(c + 1) * chunk, :]) for c in range(n_chunks)]
    for c in range(n_chunks):
        cols = slice(c * chunk, (c + 1) * chunk)
        s = logits[c]
        if scale is not None:
            s = s * scale
        if bias is not None:
            s = s + bias(cols)
        if mask is not None:
            s = jnp.where(mask[:, cols] > 0.5, s, NEG_INF)
        m_old = m_ref[:, cols]
        s_max = jnp.max(s, axis=0, keepdims=True)
        if far_bias is None:
            m_new = jnp.maximum(m_old, s_max)
            p = jnp.exp2(s - m_new)
        else:
            fb = far_bias[:, cols]
            m_new = jnp.maximum(m_old, s_max + fb)
            p = jnp.exp2(s - (m_new - fb))
        alpha = jnp.exp2(m_old - m_new)
        l_ref[:, cols] = alpha * l_ref[:, cols] + jnp.sum(p, axis=0, keepdims=True)
        acc_ref[:, cols] = alpha * acc_ref[:, cols] + _dot(vt_tile, p.astype(BF16))
        m_ref[:, cols] = m_new


def _stage_kv(k_ref, v_ref, kb_ref, vt_ref, chunk):
    for c in range(k_ref.shape[1] // chunk):
        rows = slice(c * chunk, (c + 1) * chunk)
        kb_ref[rows, :] = k_ref[0, rows, :].astype(BF16)
        vt_ref[:, rows] = v_ref[0, rows, :].T.astype(BF16)


def _diff_flash_kernel(lam_ref, q_ref, k_ref, v_ref, bias_ref, sn_ref, o_ref,
                       qs_ref, kb_ref, vt_ref, m_ref, l_ref, acc_ref, *, tq, lam_init):
    qi = pl.program_id(2)

    @pl.when(qi == 0)
    def _():
        _stage_kv(k_ref, v_ref, kb_ref, vt_ref, tq)

    qb = q_ref[0]
    qs = jnp.concatenate([qb[:, r * HEAD_W:(r + 1) * HEAD_W] for r in range(HPG)], axis=0)
    qs_ref[...] = _split_maps(qs * (DIFF_DH ** -0.5 * LOG2E)).astype(BF16)
    _reset_flash(m_ref, l_ref, acc_ref)
    far = bias_ref[2, 0:1, :]
    far = jnp.concatenate([far, far], axis=1)

    def far_body(width):
        def body(kt, c):
            rows = pl.ds(pl.multiple_of(kt * width, width), width)
            _flash_tile_t(kb_ref[rows, :], vt_ref[:, rows], qs_ref, m_ref, l_ref, acc_ref, far_bias=far)
            return c
        return body

    def near_body(kt, c):
        rows = pl.ds(pl.multiple_of(kt * tq, tq), tq)
        b = bias_ref[qi - kt]
        _flash_tile_t(kb_ref[rows, :], vt_ref[:, rows], qs_ref, m_ref, l_ref, acc_ref,
                      bias=lambda cols: jnp.concatenate([b, b], axis=1), n_chunks=1)
        return c

    n_far = jnp.maximum(qi - 1, 0)
    lax.fori_loop(0, n_far // FAR_TILES, far_body(FAR_TILES * tq), 0)
    lax.fori_loop((n_far // FAR_TILES) * FAR_TILES, n_far, far_body(tq), 0)
    lax.fori_loop(n_far, qi + 1, near_body, 0)
    o = (acc_ref[...] / l_ref[...]).T
    a = _diff_finish_rows(o, _diff_lambda(lam_ref, lam_init), sn_ref[...], lam_init)
    for r in range(HPG):
        o_ref[0, :, r * HEAD_W:(r + 1) * HEAD_W] = a[r * tq:(r + 1) * tq]


def diff_flash(qkv, bias, lam_p, sub_norm, lam_init, tq=PROMPT_TQ):
    b, t, _ = qkv.shape
    gw = HPG * HEAD_W
    rows = 2 * HPG * tq
    k0 = N_HEADS
    return pl.pallas_call(
        functools.partial(_diff_flash_kernel, tq=tq, lam_init=lam_init),
        grid=(b, N_KV, t // tq),
        in_specs=[
            pl.BlockSpec((4, DIFF_DH), lambda i, g, j: (0, 0)),
            pl.BlockSpec((1, tq, gw), lambda i, g, j: (i, j, g)),
            pl.BlockSpec((1, t, HEAD_W), lambda i, g, j: (i, 0, k0 + g)),
            pl.BlockSpec((1, t, HEAD_W), lambda i, g, j: (i, 0, k0 + N_KV + g)),
            pl.BlockSpec((4, tq, HPG * tq), lambda i, g, j: (0, 0, g)),
            pl.BlockSpec((1, HEAD_W), lambda i, g, j: (0, 0)),
        ],
        out_specs=pl.BlockSpec((1, tq, gw), lambda i, g, j: (i, j, g)),
        out_shape=jax.ShapeDtypeStruct((b, t, N_HEADS * HEAD_W), F32),
        scratch_shapes=[
            pltpu.VMEM((rows, HEAD_W), BF16),
            pltpu.VMEM((t, HEAD_W), BF16),
            pltpu.VMEM((HEAD_W, t), BF16),
            pltpu.VMEM((1, rows), F32),
            pltpu.VMEM((1, rows), F32),
            pltpu.VMEM((HEAD_W, rows), F32),
        ],
        compiler_params=_params(("parallel", "parallel", "arbitrary")),
        name="diff_flash",
    )(lam_p, qkv, qkv, qkv, bias, sub_norm.reshape(1, HEAD_W))


PAGES_PER_STEP = 4


def _decode_page_index(t_new, past_len):
    j = np.arange(PAGE)[:, None]
    i = np.arange(t_new)[None, :]
    far = np.full((PAGE, t_new), N_BUCKETS - 1, np.int32)
    new = np.where(j < t_new, _bucket_np(i - j), -1)
    g = np.arange(N_KV)[None, :, None, None]
    h = np.arange(N_HEADS)[None, None, :, None]
    tiles = []
    for bkt in (far, _bucket_np(PAGE + i - j), new):
        b = bkt[:, None, None, :]
        idx = np.where((b >= 0) & (h // HPG == g), b * N_HEADS + h, -1)
        tiles.append(idx.reshape(PAGE * N_KV, N_HEADS * t_new))
    return np.stack(tiles).astype(np.int32)


def _stack_group_heads(q_ref, g):
    return jnp.concatenate(
        [q_ref[0, :, (g * HPG + r) * HEAD_W:(g * HPG + r + 1) * HEAD_W] for r in range(HPG)], axis=0)


def _query_rows(q_ref, maps, scale):
    pieces = []
    for m in range(maps):
        for h in range(N_HEADS):
            qh = q_ref[0, :, h * HEAD_W:(h + 1) * HEAD_W] * scale
            if maps == 2:
                lo = lax.broadcasted_iota(jnp.int32, qh.shape, 1) < DIFF_DH
                qh = jnp.where(lo, qh, 0.0) if m == 0 else jnp.where(lo, 0.0, qh)
            pieces.append(qh)
    return jnp.concatenate(pieces, axis=0).astype(BF16)


def _page_bias_t(bias_ref, first_page, n_pages, reps):
    tiles = [bias_ref[jnp.where(first_page + u == n_pages - 1, 1, 0)] for u in range(PAGES_PER_STEP)]
    b = jnp.concatenate(tiles, axis=0)
    return jnp.concatenate([b] * reps, axis=1)


def _decode_step_t(k_rows, v_rows, q_ref, bias, m_ref, l_ref, acc_ref, keep=None):
    s = _dot_t(k_rows.astype(BF16), q_ref[...]) + bias
    if keep is not None:
        s = jnp.where(keep > 0.5, s, NEG_INF)
    m_old = m_ref[...]
    m_new = jnp.maximum(m_old, jnp.max(s, axis=0, keepdims=True))
    alpha = jnp.exp(m_old - m_new)
    p = jnp.exp(s - m_new)
    l_ref[...] = alpha * l_ref[...] + jnp.sum(p, axis=0, keepdims=True)
    acc_ref[...] = alpha * acc_ref[...] + _dot(v_rows.T.astype(BF16), p.astype(BF16))
    m_ref[...] = m_new


def _page_rows(refs):
    return jnp.concatenate([r[0, 0] for r in refs], axis=0)


def _diff_decode_kernel(pt_ref, lam_ref, q_ref, kn_ref, vn_ref, *rest, n_pages, lam_init):
    del pt_ref
    kp = rest[:PAGES_PER_STEP]
    vp = rest[PAGES_PER_STEP:2 * PAGES_PER_STEP]
    bias_ref, sn_ref, o_ref, qr_ref, m_ref, l_ref, acc_ref = rest[2 * PAGES_PER_STEP:]
    s = pl.program_id(1)
    t_new = q_ref.shape[1]

    @pl.when(s == 0)
    def _():
        _reset_flash(m_ref, l_ref, acc_ref)
        qr_ref[...] = _query_rows(q_ref, 2, DIFF_DH ** -0.5)
        b = bias_ref[2]
        _decode_step_t(kn_ref[0], vn_ref[0], qr_ref, jnp.concatenate([b, b], axis=1), m_ref, l_ref, acc_ref)

    _decode_step_t(_page_rows(kp), _page_rows(vp), qr_ref,
                   _page_bias_t(bias_ref, s * PAGES_PER_STEP, n_pages, 2), m_ref, l_ref, acc_ref)

    @pl.when(s == pl.num_programs(1) - 1)
    def _():
        o = (acc_ref[...] / l_ref[...]).T
        a = _diff_finish_rows(o, _diff_lambda(lam_ref, lam_init), sn_ref[...], lam_init)
        for h in range(N_HEADS):
            o_ref[0, :, h * HEAD_W:(h + 1) * HEAD_W] = a[h * t_new:(h + 1) * t_new]


PAGE_ROWS = PAGE * N_KV


def _as_page_rows(cache):
    return cache.reshape(*cache.shape[:2], PAGE_ROWS, HEAD_W)


def _page_specs(layer, n):
    def spec(u):
        return pl.BlockSpec((1, 1, PAGE_ROWS, HEAD_W),
                            lambda i, s, pt: (layer, pt[i, s * PAGES_PER_STEP + u], 0, 0))
    return [spec(u) for u in range(PAGES_PER_STEP)] * n


def diff_decode(q, k_new, v_new, cache_k, cache_v, layer, page_table, bias, lam_p, sub_norm, lam_init):
    b, t_new, _ = q.shape
    n_pages = page_table.shape[1]
    lanes = 2 * N_HEADS * t_new
    assert N_HEADS * t_new == LANE
    fixed = lambda *shape: pl.BlockSpec(shape, lambda i, s, pt: (0,) * len(shape))
    per_b = lambda *shape: pl.BlockSpec((1,) + shape, lambda i, s, pt: (i,) + (0,) * len(shape))
    grid_spec = pltpu.PrefetchScalarGridSpec(
        num_scalar_prefetch=1,
        grid=(b, n_pages // PAGES_PER_STEP),
        in_specs=[fixed(4, DIFF_DH), per_b(t_new, N_HEADS * HEAD_W), per_b(PAGE_ROWS, HEAD_W),
                  per_b(PAGE_ROWS, HEAD_W)]
        + _page_specs(layer, 2)
        + [fixed(*bias.shape), fixed(1, HEAD_W)],
        out_specs=per_b(t_new, N_HEADS * HEAD_W),
        scratch_shapes=[
            pltpu.VMEM((lanes, HEAD_W), BF16),
            pltpu.VMEM((1, lanes), F32),
            pltpu.VMEM((1, lanes), F32),
            pltpu.VMEM((HEAD_W, lanes), F32),
        ],
    )
    return pl.pallas_call(
        functools.partial(_diff_decode_kernel, n_pages=n_pages, lam_init=lam_init),
        grid_spec=grid_spec,
        out_shape=jax.ShapeDtypeStruct(q.shape[:2] + (N_HEADS * HEAD_W,), F32),
        compiler_params=_params(("parallel", "arbitrary")),
        name="diff_decode",
    )(page_table, lam_p, q, k_new, v_new, *([cache_k] * PAGES_PER_STEP), *([cache_v] * PAGES_PER_STEP),
      bias, sub_norm.reshape(1, HEAD_W))


def _pad_rows(a, n):
    return jnp.pad(a, ((0, 0), (0, n - a.shape[1]), (0, 0)))


def _new_page_rows(a):
    return _pad_rows(a, PAGE).reshape(a.shape[0], PAGE_ROWS, HEAD_W)


def diff_layer(xp, xs, gain, w_in, q_norm, k_norm, lam_p, sub_norm, w_out, cache_k, cache_v, layer,
               page_table, bias_p, bias_s, lam_init):
    nq, nk = N_HEADS * HEAD_W, N_KV * HEAD_W
    tile = 4 * LANE
    q_gain, k_gain = jnp.tile(q_norm, tile // DIFF_DH), jnp.tile(k_norm, tile // DIFF_DH)
    head_norm = (DIFF_DH, {**{c: q_gain for c in range(nq // tile)}, nq // tile: k_gain})
    outs = []
    for x, paged in ((xp, False), (xs, True)):
        b, t, d = x.shape
        x2 = x.reshape(b * t, d)
        qkv = dense(x2, w_in, layer, gain=gain, head_norm=head_norm).reshape(b, t, -1)
        k3, v3 = qkv[:, :, nq:nq + nk], qkv[:, :, nq + nk:]
        if paged:
            o = diff_decode(qkv, _new_page_rows(k3), _new_page_rows(v3), _as_page_rows(cache_k),
                            _as_page_rows(cache_v), layer, page_table, bias_s, lam_p, sub_norm, lam_init)
        else:
            o = diff_flash(qkv, bias_p, lam_p, sub_norm, lam_init)
        y = dense(o.reshape(b * t, -1), w_out, layer, res=x2).reshape(b, t, d)
        outs.append((y, k3.reshape(b, t, N_KV, HEAD_W), v3.reshape(b, t, N_KV, HEAD_W)))
    (yp, kp, vp), (ys, ks, vs) = outs
    return yp, ys, kp, vp, ks, vs


SSM_GW = SSM_HPG * SSM_HEADDIM
SSM_BC = 2 * SSM_GROUPS * D_STATE
CONV_PAD = 8


def _conv_silu(buf_ref, w_ref, b_ref, n):
    acc = b_ref[...]
    for k in range(CONV_W):
        acc = acc + buf_ref[pl.ds(CONV_PAD - (CONV_W - 1) + k, n), :] * w_ref[k:k + 1, :]
    return acc * jax.nn.sigmoid(acc)


def _ssd_kernel(z_ref, x_ref, bc_ref, dt_ref, cbx_ref, cbbc_ref, wx_ref, wbc_ref, bx_ref, bbc_ref,
                dtb_ref, alog_ref, dsk_ref, ng_ref, e_ref, tri_ref, h0_ref,
                y_ref, hout_ref, ht_ref, xbuf_ref, bcbuf_ref, xa_ref, bca_ref, *, t_valid):
    c = pl.program_id(1)
    n = x_ref.shape[1]

    @pl.when(c == 0)
    def _():
        for g in range(SSM_GROUPS):
            ht_ref[g] = h0_ref[0, g].T
        xbuf_ref[0:CONV_PAD] = cbx_ref[0]
        bcbuf_ref[0:CONV_PAD] = cbbc_ref[0]

    xbuf_ref[CONV_PAD:CONV_PAD + n] = x_ref[0]
    bcbuf_ref[CONV_PAD:CONV_PAD + n] = bc_ref[0]
    xa_ref[...] = _conv_silu(xbuf_ref, wx_ref, bx_ref, n)
    bca_ref[...] = _conv_silu(bcbuf_ref, wbc_ref, bbc_ref, n)
    xbuf_ref[0:CONV_PAD] = xbuf_ref[n:n + CONV_PAD]
    bcbuf_ref[0:CONV_PAD] = bcbuf_ref[n:n + CONV_PAD]

    row = lax.broadcasted_iota(jnp.int32, (n, LANE), 0) + c * n
    dtr = dt_ref[0] + dtb_ref[...]
    dt = jnp.maximum(dtr, 0.0) + jnp.log1p(jnp.exp(-jnp.abs(dtr)))
    dt = jnp.where(row < t_valid, dt, 0.0)
    dta = dt * (-jnp.exp(alog_ref[...]))
    hi, mid, lo = _split3(dta)
    tri = tri_ref[...]
    cs = _dot(tri, hi) + _dot(tri, mid) + _dot(tri, lo)
    cs_last = cs[n - 1:n, :]
    cs_t = cs.T
    dt_t = dt.T
    stacked = jnp.concatenate(
        [jnp.exp(cs), jnp.exp(cs_last - cs) * dt, jnp.broadcast_to(jnp.exp(cs_last), (8, LANE))], axis=0)
    ex = _dot_exact_rhs(stacked, e_ref[...])
    causal = (lax.broadcasted_iota(jnp.int32, (n, n), 0) >= lax.broadcasted_iota(jnp.int32, (n, n), 1))

    for g in range(SSM_GROUPS):
        gs = slice(g * SSM_GW, (g + 1) * SSM_GW)
        bm = bca_ref[:, g * D_STATE:(g + 1) * D_STATE]
        cm = bca_ref[:, (SSM_GROUPS + g) * D_STATE:(SSM_GROUPS + g + 1) * D_STATE].astype(BF16)
        cb = _dot_t(cm, bm.astype(BF16))
        xg = xa_ref[:, gs]
        ys = []
        for r in range(SSM_HPG):
            h = g * SSM_HPG + r
            seg = cs[:, h:h + 1] - cs_t[h:h + 1, :]
            dec = jnp.where(causal, jnp.exp(jnp.where(causal, seg, 0.0)), 0.0)
            mm = (cb * dec * dt_t[h:h + 1, :]).astype(BF16)
            ys.append(_dot(mm, xg[:, r * SSM_HEADDIM:(r + 1) * SSM_HEADDIM].astype(BF16)))
        ht = ht_ref[g]
        y = jnp.concatenate(ys, axis=1) + _dot(cm, ht.astype(BF16)) * ex[0:n, gs]
        y = y + dsk_ref[:, gs] * xg
        zg = z_ref[0, :, gs]
        y = y * (zg * jax.nn.sigmoid(zg))
        y_ref[0, :, gs] = _rms(y, ng_ref[:, gs])
        xw = (xg * ex[n:2 * n, gs]).astype(BF16)
        ht_ref[g] = ht * ex[2 * n:2 * n + 1, gs] + _dot(bm.T.astype(BF16), xw)

    @pl.when(c == pl.num_programs(1) - 1)
    def _():
        for g in range(SSM_GROUPS):
            hout_ref[0, g] = ht_ref[g].T


def _head_expand_matrix():
    e = np.zeros((LANE, D_INNER), np.float32)
    for h in range(SSM_HEADS):
        e[h, h * SSM_HEADDIM:(h + 1) * SSM_HEADDIM] = 1.0
    return e


def ssd_core(zx, dt_raw, conv_buf, h0, conv_w, conv_b, dt_bias, a_log, d_skip, norm_g, t_valid):
    b, t, _ = zx.shape
    n = SSM_CHUNK
    pad_h = LANE - SSM_HEADS
    cb = jnp.pad(conv_buf, ((0, 0), (CONV_PAD - (CONV_W - 1), 0), (0, 0)))
    tri = jnp.asarray(np.tril(np.ones((n, n), np.float32)), BF16)
    e = jnp.asarray(_head_expand_matrix(), BF16)
    d_exp = jnp.repeat(d_skip, SSM_HEADDIM).reshape(1, D_INNER)
    fixed = lambda *shape: pl.BlockSpec(shape, lambda i, c: (0,) * len(shape))
    per_b = lambda *shape: pl.BlockSpec((1,) + shape, lambda i, c: (i,) + (0,) * len(shape))
    y, h_last = pl.pallas_call(
        functools.partial(_ssd_kernel, t_valid=t_valid),
        grid=(b, t // n),
        in_specs=[
            pl.BlockSpec((1, n, D_INNER), lambda i, c: (i, c, 0)),
            pl.BlockSpec((1, n, D_INNER), lambda i, c: (i, c, 1)),
            pl.BlockSpec((1, n, SSM_BC), lambda i, c: (i, c, 2 * D_INNER // SSM_BC)),
            pl.BlockSpec((1, n, LANE), lambda i, c: (i, c, 0)),
            per_b(CONV_PAD, D_INNER), per_b(CONV_PAD, SSM_BC),
            fixed(CONV_W, D_INNER), fixed(CONV_W, SSM_BC), fixed(1, D_INNER), fixed(1, SSM_BC),
            fixed(1, LANE), fixed(1, LANE), fixed(1, D_INNER), fixed(1, D_INNER),
            fixed(LANE, D_INNER), fixed(n, n),
            per_b(SSM_GROUPS, SSM_GW, D_STATE),
        ],
        out_specs=[
            pl.BlockSpec((1, n, D_INNER), lambda i, c: (i, c, 0)),
            per_b(SSM_GROUPS, SSM_GW, D_STATE),
        ],
        out_shape=[jax.ShapeDtypeStruct((b, t, D_INNER), F32),
                   jax.ShapeDtypeStruct((b, SSM_GROUPS, SSM_GW, D_STATE), F32)],
        scratch_shapes=[
            pltpu.VMEM((SSM_GROUPS, D_STATE, SSM_GW), F32),
            pltpu.VMEM((n + CONV_PAD, D_INNER), F32),
            pltpu.VMEM((n + CONV_PAD, SSM_BC), F32),
            pltpu.VMEM((n, D_INNER), F32),
            pltpu.VMEM((n, SSM_BC), F32),
        ],
        compiler_params=_params(("parallel", "arbitrary")),
        name="ssd_core",
    )(zx, zx, zx, dt_raw, cb[:, :, :D_INNER], cb[:, :, D_INNER:],
      conv_w[:, :D_INNER], conv_w[:, D_INNER:], conv_b[:D_INNER].reshape(1, -1), conv_b[D_INNER:].reshape(1, -1),
      jnp.pad(dt_bias, (0, pad_h)).reshape(1, LANE), jnp.pad(a_log, (0, pad_h)).reshape(1, LANE),
      d_exp, norm_g.reshape(1, D_INNER), e, tri,
      h0.reshape(b, SSM_GROUPS, SSM_GW, D_STATE))
    return y, h_last.reshape(b, SSM_HEADS, SSM_HEADDIM, D_STATE)


def ssd_layer(xp, xs, gain, w_in, layer, conv_w, conv_b, dt_bias, a_log, d_skip, norm_g, w_out,
              state_conv, state_ssm):
    nzx = D_INNER + CONV_DIM
    w_dt = jnp.pad(w_in[layer, :, nzx:], ((0, 0), (0, LANE - SSM_HEADS)))[None]
    w_in_t = jnp.swapaxes(w_in, 1, 2)
    outs = []
    for x, conv_buf, h0 in ((xp, None, None), (xs, state_conv, state_ssm)):
        b, t, d = x.shape
        x2 = x.reshape(b * t, d)
        if conv_buf is None:
            conv_buf = jnp.zeros((b, CONV_W - 1, CONV_DIM), F32)
            h0 = jnp.zeros((b, SSM_HEADS, SSM_HEADDIM, D_STATE), F32)
        zx = dense(x2, w_in_t, layer, gain=gain, n=nzx, w_t=True).reshape(b, t, -1)
        dt_raw = dense(x2, w_dt, 0, gain=gain).reshape(b, t, LANE)
        tp = -(-t // SSM_CHUNK) * SSM_CHUNK
        y, h_last = ssd_core(_pad_rows(zx, tp), _pad_rows(dt_raw, tp), conv_buf, h0, conv_w, conv_b,
                             dt_bias, a_log, d_skip, norm_g, t)
        y = y[:, :t].reshape(b * t, D_INNER)
        tail = min(t, CONV_W - 1)
        conv_out = jnp.concatenate([conv_buf[:, tail:], zx[:, t - tail:, D_INNER:]], axis=1)
        outs.append((dense(y, w_out, layer, res=x2).reshape(b, t, d), h_last, conv_out))
    (yp, hp, cp), (ys, hs, cs) = outs
    return yp, ys, hp, cp, hs, cs


NSA_KV = N_KV * HEAD_W
NSA_SCALE = HEAD_W ** -0.5
CMP_PAIRS = CMP_STRIDE // 2
CMP_PAGES = 16


def _cmp_uv_kernel(pt_ref, *refs):
    del pt_ref
    pages = refs[:CMP_PAGES]
    wab_ref, o_ref = refs[CMP_PAGES:]
    subs = PAGE // CMP_STRIDE
    for g in range(N_KV):
        acc = jnp.zeros((CMP_PAGES * subs, 2 * HEAD_W), F32)
        for lp in range(CMP_PAIRS):
            halves = []
            for li in range(2):
                rows = pl.ds((2 * lp + li) * N_KV + g, subs, stride=CMP_STRIDE * N_KV)
                halves.append(jnp.concatenate([p[0, 0, rows, :] for p in pages], axis=0))
            acc = acc + _dot(jnp.concatenate(halves, axis=1).astype(BF16), wab_ref[lp])
        o_ref[0, :, g * 2 * HEAD_W:(g + 1) * 2 * HEAD_W] = acc


def cmp_uv(rows, layer, page_table, wab):
    b, n_pages = page_table.shape
    subs = PAGE // CMP_STRIDE

    def spec(u):
        return pl.BlockSpec((1, 1, PAGE_ROWS, HEAD_W),
                            lambda i, c, pt: (layer, pt[i, c * CMP_PAGES + u], 0, 0))

    grid_spec = pltpu.PrefetchScalarGridSpec(
        num_scalar_prefetch=1,
        grid=(b, n_pages // CMP_PAGES),
        in_specs=[spec(u) for u in range(CMP_PAGES)]
        + [pl.BlockSpec(wab.shape, lambda i, c, pt: (0, 0, 0))],
        out_specs=pl.BlockSpec((1, CMP_PAGES * subs, 2 * NSA_KV), lambda i, c, pt: (i, c, 0)),
    )
    return pl.pallas_call(
        _cmp_uv_kernel,
        grid_spec=grid_spec,
        out_shape=jax.ShapeDtypeStruct((b, n_pages * subs, 2 * NSA_KV), F32),
        compiler_params=_params(("parallel", "arbitrary")),
        name="cmp_uv",
    )(page_table, *([rows] * CMP_PAGES), wab)


def _cmp_finish_kernel(uv_ref, pe_ref, w1_ref, w2_ref, kn_ref, o_ref, *, norm):
    n = uv_ref.shape[1]
    c = _dot(jnp.broadcast_to(pe_ref[...], (8, pe_ref.shape[1])).astype(BF16), w1_ref[...])[0:1]
    for g in range(N_KV):
        u = uv_ref[0, :, g * 2 * HEAD_W:g * 2 * HEAD_W + HEAD_W]
        v = uv_ref[0, :, g * 2 * HEAD_W + HEAD_W:(g + 1) * 2 * HEAD_W]
        pre = u + pltpu.roll(v, n - 1, axis=0) + c
        out = _dot((pre * jax.nn.sigmoid(pre)).astype(BF16), w2_ref[...])
        if norm:
            out = _rms(out, kn_ref[...])
        o_ref[0, :, g * HEAD_W:(g + 1) * HEAD_W] = out


def cmp_finish(uv, pe, w1, w2, k_norm):
    b, n, _ = uv.shape
    norm = k_norm is not None
    kn = (k_norm if norm else jnp.ones((HEAD_W,), F32)).reshape(1, HEAD_W)
    return pl.pallas_call(
        functools.partial(_cmp_finish_kernel, norm=norm),
        grid=(b,),
        in_specs=[
            pl.BlockSpec((1, n, 2 * NSA_KV), lambda i: (i, 0, 0)),
            pl.BlockSpec((1, pe.size), lambda i: (0, 0)),
            pl.BlockSpec(w1.shape, lambda i: (0, 0)),
            pl.BlockSpec(w2.shape, lambda i: (0, 0)),
            pl.BlockSpec((1, HEAD_W), lambda i: (0, 0)),
        ],
        out_specs=pl.BlockSpec((1, n, NSA_KV), lambda i: (i, 0, 0)),
        out_shape=jax.ShapeDtypeStruct((b, n, NSA_KV), F32),
        compiler_params=_params(("parallel",)),
        name="cmp_finish",
    )(uv, pe.reshape(1, -1), w1, w2, kn)


def compress(rows, layer, page_table, pe, w1, w2, k_norm):
    w1r = w1.reshape(2, CMP_STRIDE, HEAD_W, HEAD_W)
    wab = jnp.transpose(w1r, (1, 2, 0, 3)).reshape(CMP_PAIRS, 2 * HEAD_W, 2 * HEAD_W).astype(BF16)
    uv = cmp_uv(rows, layer, page_table, wab)
    return cmp_finish(uv, pe, w1.astype(BF16), w2.astype(BF16), k_norm)


def _importance_matrix(n_cmp, n_rows, n_slc, n_cols):
    w = np.zeros((n_rows, n_cols), np.float32)
    w_imp = [1.0] + [2.0] * (SLC_RATIO - 1) + [1.0]
    for s in range(n_slc):
        for m, wm in enumerate(w_imp):
            j = SLC_RATIO * s + m - 1
            if 0 <= j < n_cmp:
                w[j, s] += wm
    return w


def _select_blocks(s_slc, q_pos0, n_slc):
    t, w = s_slc.shape
    blk = lax.broadcasted_iota(jnp.int32, (t, w), 1)
    qpos = q_pos0 + lax.broadcasted_iota(jnp.int32, (t, w), 0)
    qb = qpos // SLC_BLOCK
    forced = (blk == 0) | (blk == qb) | (blk == qb - 1)
    score = jnp.where(forced, FORCE_SCORE, jnp.where(blk * SLC_BLOCK <= qpos, s_slc, -1.0))
    score = jnp.where(blk < n_slc, score, -2.0)
    cnt = jnp.zeros((t, w), F32)
    for sp in range(n_slc):
        col = score[:, sp:sp + 1]
        tie = jnp.where(blk > sp, 1.0, 0.0)
        cnt = cnt + jnp.where(col > score, 1.0, jnp.where(col == score, tie, 0.0))
    return jnp.where(cnt < N_SELECT, 1.0, 0.0)


def _masked_softmax(s, valid, axis=-1, base2=False):
    m = jnp.max(s, axis=axis, keepdims=True)
    e = jnp.where(valid, (jnp.exp2 if base2 else jnp.exp)(s - m), 0.0)
    return e / jnp.maximum(jnp.sum(e, axis=axis, keepdims=True), 1e-30)


def _reset_flash(m_ref, l_ref, acc_ref):
    m_ref[...] = jnp.full(m_ref.shape, NEG_INF, F32)
    l_ref[...] = jnp.zeros(l_ref.shape, F32)
    acc_ref[...] = jnp.zeros(acc_ref.shape, F32)


def _select_blocks_t(s_slc, q_pos0, n_slc):
    w, t = s_slc.shape
    n8 = -(-n_slc // 8) * 8
    blk = lax.broadcasted_iota(jnp.int32, (n8, t), 0)
    qpos = q_pos0 + lax.broadcasted_iota(jnp.int32, (n8, t), 1)
    qb = qpos // SLC_BLOCK
    forced = (blk == 0) | (blk == qb) | (blk == qb - 1)
    score = jnp.where(forced, FORCE_SCORE, jnp.where(blk * SLC_BLOCK <= qpos, s_slc[:n8], -1.0))
    score = jnp.where(blk < n_slc, score, -2.0)
    cnt = jnp.zeros((n8, t), F32)
    for sp in range(n_slc):
        row = score[sp:sp + 1, :]
        tie = jnp.where(blk > sp, 1.0, 0.0)
        cnt = cnt + jnp.where(row > score, 1.0, jnp.where(row == score, tie, 0.0))
    sel = jnp.where(cnt < N_SELECT, 1.0, 0.0)
    if n8 < w:
        sel = jnp.concatenate([sel, jnp.zeros((w - n8, t), F32)], axis=0)
    return sel


def _nsa_prompt_kernel(q_ref, gt_ref, kc_ref, vc_ref, ks_ref, vs_ref, kw_ref, vw_ref, bias_ref, u_ref, wimp_ref,
                       o_ref, qs_ref, ksb_ref, vst_ref, kwb_ref, vwt_ref, sel_ref, m_ref, l_ref, acc_ref,
                       *, tq, n_slc):
    qi = pl.program_id(2)

    @pl.when(qi == 0)
    def _():
        _stage_kv(ks_ref, vs_ref, ksb_ref, vst_ref, tq)
        _stage_kv(kw_ref, vw_ref, kwb_ref, vwt_ref, tq)

    qs_ref[...] = jnp.concatenate(
        [q_ref[0, :, r * HEAD_W:(r + 1) * HEAD_W] for r in range(HPG)], axis=0).astype(BF16)

    n_pad = kc_ref.shape[1]
    start = pl.multiple_of(n_pad - (tq // CMP_STRIDE) * (qi + 1), 8)
    scale = NSA_SCALE * LOG2E
    bias_c = u_ref[0, pl.ds(start, n_pad), :]
    s = _dot_t(kc_ref[0].astype(BF16), qs_ref[...]) * scale + bias_c
    pc = _masked_softmax(s, bias_c > 0.5 * NEG_INF, axis=0, base2=True)
    o_cmp = _dot(vc_ref[0].T.astype(BF16), pc.astype(BF16))
    psum = pc[:, 0:tq] + pc[:, tq:2 * tq] + pc[:, 2 * tq:3 * tq] + pc[:, 3 * tq:4 * tq]
    hi, mid, lo = _split3(psum)
    wimp = wimp_ref[...]
    s_slc = _dot(wimp, hi) + _dot(wimp, mid) + _dot(wimp, lo)
    sel_ref[...] = _select_blocks_t(s_slc, qi * tq, n_slc).astype(BF16)

    _reset_flash(m_ref, l_ref, acc_ref)

    far = bias_ref[2, 0:1, :]

    def key_mask(kt, width):
        key = lax.broadcasted_iota(jnp.int32, (width, LANE), 0)
        blk = lax.broadcasted_iota(jnp.int32, (width, LANE), 1)
        onehot = jnp.where(blk == kt * (width // SLC_BLOCK) + key // SLC_BLOCK, 1.0, 0.0).astype(BF16)
        return jnp.concatenate([_dot(onehot, sel_ref[...])] * HPG, axis=1)

    def slc_far_body(width):
        def body(kt, c):
            rows = pl.ds(pl.multiple_of(kt * width, width), width)
            _flash_tile_t(ksb_ref[rows, :], vst_ref[:, rows], qs_ref, m_ref, l_ref, acc_ref, scale=scale,
                          far_bias=far, mask=key_mask(kt, width))
            return c
        return body

    def slc_near_body(kt, c):
        rows = pl.ds(pl.multiple_of(kt * tq, tq), tq)
        typ = qi - kt
        _flash_tile_t(ksb_ref[rows, :], vst_ref[:, rows], qs_ref, m_ref, l_ref, acc_ref, scale=scale,
                      bias=lambda cols: bias_ref[typ, :, cols], mask=key_mask(kt, tq))
        return c

    n_far = jnp.maximum(qi - 1, 0)
    lax.fori_loop(0, n_far // FAR_TILES, slc_far_body(FAR_TILES * tq), 0)
    lax.fori_loop((n_far // FAR_TILES) * FAR_TILES, n_far, slc_far_body(tq), 0)
    lax.fori_loop(n_far, qi + 1, slc_near_body, 0)
    o_slc = acc_ref[...] / l_ref[...]

    _reset_flash(m_ref, l_ref, acc_ref)
    nw = WINDOW // tq

    def win_body(kt, c):
        rows = pl.ds(pl.multiple_of(kt * tq, tq), tq)
        t = qi - kt
        typ = jnp.where(t == nw, 3, jnp.minimum(t, 2))
        _flash_tile_t(kwb_ref[rows, :], vwt_ref[:, rows], qs_ref, m_ref, l_ref, acc_ref, scale=scale,
                      bias=lambda cols: bias_ref[typ, :, cols])
        return c

    lax.fori_loop(jnp.maximum(qi - nw, 0), qi + 1, win_body, 0)
    o_win = acc_ref[...] / l_ref[...]

    sig = jax.nn.sigmoid(gt_ref[0]).T
    for r in range(HPG):
        cs = slice(r * tq, (r + 1) * tq)
        o_t = (sig[r:r + 1, :] * o_cmp[:, cs] + sig[HPG + r:HPG + r + 1, :] * o_slc[:, cs]
               + sig[2 * HPG + r:2 * HPG + r + 1, :] * o_win[:, cs])
        o_ref[0, :, r * HEAD_W:(r + 1) * HEAD_W] = o_t.T


def _prompt_cmp_buckets(tq, n_pad):
    jp = np.arange(2 * n_pad)[:, None] - (n_pad - tq // CMP_STRIDE)
    i = np.arange(tq)[None, :]
    return _bucket_np(i - CMP_STRIDE * jp - (2 * CMP_STRIDE - 1))[None].astype(np.int32)


def nsa_prompt(proj, gates, kc, vc, bias, u_bias, tq=PROMPT_TQ):
    b, t, _ = proj.shape
    gw = HPG * HEAD_W
    n_pad = kc.shape[1]
    n_slc = t // SLC_BLOCK
    wimp = jnp.asarray(_importance_matrix(n_pad - 1, n_pad, n_slc, LANE).T, BF16)
    rows = HPG * tq
    seq = lambda n: pl.BlockSpec((1, n, HEAD_W), lambda i, g, j: (i, 0, g))
    kv = lambda a: pl.BlockSpec((1, t, HEAD_W), lambda i, g, j: (i, 0, N_HEADS + a * N_KV + g))
    return pl.pallas_call(
        functools.partial(_nsa_prompt_kernel, tq=tq, n_slc=n_slc),
        grid=(b, N_KV, t // tq),
        in_specs=[
            pl.BlockSpec((1, tq, gw), lambda i, g, j: (i, j, g)),
            pl.BlockSpec((1, tq, LANE), lambda i, g, j: (i, j, g)),
            seq(n_pad), seq(n_pad), kv(2), kv(3), kv(4), kv(5),
            pl.BlockSpec((4, tq, rows), lambda i, g, j: (0, 0, g)),
            pl.BlockSpec((1, 2 * n_pad, rows), lambda i, g, j: (0, 0, g)),
            pl.BlockSpec(wimp.shape, lambda i, g, j: (0, 0)),
        ],
        out_specs=pl.BlockSpec((1, tq, gw), lambda i, g, j: (i, j, g)),
        out_shape=jax.ShapeDtypeStruct((b, t, N_HEADS * HEAD_W), F32),
        scratch_shapes=[
            pltpu.VMEM((rows, HEAD_W), BF16),
            pltpu.VMEM((t, HEAD_W), BF16),
            pltpu.VMEM((HEAD_W, t), BF16),
            pltpu.VMEM((t, HEAD_W), BF16),
            pltpu.VMEM((HEAD_W, t), BF16),
            pltpu.VMEM((LANE, tq), BF16),
            pltpu.VMEM((1, rows), F32),
            pltpu.VMEM((1, rows), F32),
            pltpu.VMEM((HEAD_W, rows), F32),
        ],
        compiler_params=_params(("parallel", "parallel", "arbitrary")),
        name="nsa_prompt",
    )(proj, gates, kc, vc, proj, proj, proj, proj, bias, u_bias, wimp)


def _decode_cmp_buckets(t_new, past_len):
    n_pad = past_len // CMP_STRIDE
    i = np.arange(t_new)[:, None]
    j = n_pad - LANE + np.arange(LANE)[None, :]
    last = _bucket_np(past_len + i - CMP_STRIDE * j - (2 * CMP_STRIDE - 1))
    last = np.where(j < n_pad - 1, last, -1)
    return np.stack([np.full((t_new, LANE), N_BUCKETS - 1, np.int32), last]).astype(np.int32)


def _decode_win_buckets(t_new, n_tiles):
    i = np.arange(t_new)[:, None]
    idx = np.arange(n_tiles * LANE)[None, :]
    dw = WINDOW + i - idx
    ok = (dw >= 0) & (dw < WINDOW) & (idx < WINDOW + t_new)
    b = np.where(ok, _bucket_np(dw), -1)
    return np.stack([b[:, k * LANE:(k + 1) * LANE] for k in range(n_tiles)]).astype(np.int32)


def _nsa_decode_kernel(pt_ref, q_ref, gt_ref, kc_ref, vc_ref, kn_ref, vn_ref, kw_ref, vw_ref, *rest,
                       n_pages, past_len):
    del pt_ref
    kp = rest[:PAGES_PER_STEP]
    vp = rest[PAGES_PER_STEP:2 * PAGES_PER_STEP]
    (bias_ref, bias_c_ref, bias_w_ref, wimp_ref, o_ref,
     qr_ref, selt_ref, ocmp_ref, owin_ref, m_ref, l_ref, acc_ref) = rest[2 * PAGES_PER_STEP:]
    s = pl.program_id(1)
    t_new = q_ref.shape[1]
    rows = HPG * t_new
    n_slc = -(-(past_len + t_new) // SLC_BLOCK)

    @pl.when(s == 0)
    def _():
        _reset_flash(m_ref, l_ref, acc_ref)
        qr_ref[...] = _query_rows(q_ref, 1, NSA_SCALE)
        n_pad = kc_ref.shape[1]
        sel_t = jnp.zeros(selt_ref.shape, F32)
        for g in range(N_KV):
            sl = slice(g * HEAD_W, (g + 1) * HEAD_W)
            gr = slice(g * rows, (g + 1) * rows)
            qx = _stack_group_heads(q_ref, g).astype(BF16)
            bias_c = jnp.concatenate([bias_c_ref[0, gr, :]] * (n_pad // LANE - 1) + [bias_c_ref[1, gr, :]], axis=1)
            sc = _dot_t(qx, kc_ref[0, :, sl].astype(BF16)) * NSA_SCALE + bias_c
            pc = _masked_softmax(sc, bias_c > 0.5 * NEG_INF)
            ocmp_ref[g] = _dot(pc.astype(BF16), vc_ref[0, :, sl].astype(BF16))
            psum = sum(pc[r * t_new:(r + 1) * t_new] for r in range(1, HPG)) + pc[0:t_new]
            s_slc = _dot_exact_rhs(psum, wimp_ref[...])
            sel = _select_blocks(s_slc, past_len, n_slc)
            sel_pad = jnp.concatenate([sel, jnp.zeros((LANE - t_new, sel.shape[1]), F32)], axis=0).T
            tok = lax.broadcasted_iota(jnp.int32, (LANE, LANE), 0)
            lane = lax.broadcasted_iota(jnp.int32, (LANE, LANE), 1)
            spread = jnp.where((lane % t_new == tok) & (lane // rows == g), 1.0, 0.0).astype(BF16)
            sel_t = sel_t + _dot(sel_pad.astype(BF16), spread)
            n_wt = bias_w_ref.shape[0]
            bias_w = jnp.concatenate([bias_w_ref[k, gr, :] for k in range(n_wt)], axis=1)
            sw = _dot_t(qx, kw_ref[0, :, sl].astype(BF16)) * NSA_SCALE + bias_w
            pw = _masked_softmax(sw, bias_w > 0.5 * NEG_INF)
            owin_ref[g] = _dot(pw.astype(BF16), vw_ref[0, :, sl].astype(BF16))
        selt_ref[...] = sel_t
        _decode_step_t(kn_ref[0], vn_ref[0], qr_ref, bias_ref[2], m_ref, l_ref, acc_ref)

    block_rows = SLC_BLOCK * N_KV
    first_block = s * (PAGES_PER_STEP * PAGE // SLC_BLOCK)
    keep = jnp.concatenate(
        [jnp.broadcast_to(selt_ref[pl.ds(first_block + b, 1), :], (block_rows, selt_ref.shape[1]))
         for b in range(PAGES_PER_STEP * PAGE // SLC_BLOCK)], axis=0)
    _decode_step_t(_page_rows(kp), _page_rows(vp), qr_ref,
                   _page_bias_t(bias_ref, s * PAGES_PER_STEP, n_pages, 1), m_ref, l_ref, acc_ref, keep=keep)

    @pl.when(s == pl.num_programs(1) - 1)
    def _():
        sig = jax.nn.sigmoid(gt_ref[0])
        o_slc_all = (acc_ref[...] / l_ref[...]).T
        for g in range(N_KV):
            o_cmp = ocmp_ref[g]
            o_win = owin_ref[g]
            for r in range(HPG):
                h = g * HPG + r
                rs = slice(r * t_new, (r + 1) * t_new)
                c0 = g * LANE + r
                o_ref[0, :, h * HEAD_W:(h + 1) * HEAD_W] = (
                    sig[:, c0:c0 + 1] * o_cmp[rs]
                    + sig[:, c0 + HPG:c0 + HPG + 1] * o_slc_all[h * t_new:(h + 1) * t_new]
                    + sig[:, c0 + 2 * HPG:c0 + 2 * HPG + 1] * o_win[rs])


def nsa_decode(q, gates, kc, vc, k_new, v_new, kw_src, vw_src, cache_k, cache_v, layer, page_table,
               bias, bias_c, bias_w, past_len):
    b, t_new, _ = q.shape
    n_pages = page_table.shape[1]
    rows = HPG * t_new
    lanes = N_HEADS * t_new
    assert lanes == LANE
    n_pad = kc.shape[1]
    n_slc = -(-(past_len + t_new) // SLC_BLOCK)
    n_cols = -(-n_slc // LANE) * LANE
    wimp = jnp.asarray(_importance_matrix(n_pad - 1, n_pad, n_slc, n_cols), BF16)
    fixed = lambda *shape: pl.BlockSpec(shape, lambda i, s, pt: (0,) * len(shape))
    per_b = lambda *shape: pl.BlockSpec((1,) + shape, lambda i, s, pt: (i,) + (0,) * len(shape))
    grid_spec = pltpu.PrefetchScalarGridSpec(
        num_scalar_prefetch=1,
        grid=(b, n_pages // PAGES_PER_STEP),
        in_specs=[per_b(t_new, N_HEADS * HEAD_W), per_b(t_new, N_KV * LANE),
                  per_b(n_pad, NSA_KV), per_b(n_pad, NSA_KV), per_b(PAGE_ROWS, HEAD_W), per_b(PAGE_ROWS, HEAD_W),
                  per_b(kw_src.shape[1], NSA_KV), per_b(kw_src.shape[1], NSA_KV)]
        + _page_specs(layer, 2)
        + [fixed(*bias.shape), fixed(*bias_c.shape), fixed(*bias_w.shape), fixed(*wimp.shape)],
        out_specs=per_b(t_new, N_HEADS * HEAD_W),
        scratch_shapes=[
            pltpu.VMEM((lanes, HEAD_W), BF16),
            pltpu.VMEM((n_cols, lanes), F32),
            pltpu.VMEM((N_KV, rows, HEAD_W), F32),
            pltpu.VMEM((N_KV, rows, HEAD_W), F32),
            pltpu.VMEM((1, lanes), F32),
            pltpu.VMEM((1, lanes), F32),
            pltpu.VMEM((HEAD_W, lanes), F32),
        ],
    )
    return pl.pallas_call(
        functools.partial(_nsa_decode_kernel, n_pages=n_pages, past_len=past_len),
        grid_spec=grid_spec,
        out_shape=jax.ShapeDtypeStruct(q.shape[:2] + (N_HEADS * HEAD_W,), F32),
        compiler_params=_params(("parallel", "arbitrary")),
        name="nsa_decode",
    )(page_table, q, gates, kc, vc, k_new, v_new, kw_src, vw_src,
      *([cache_k] * PAGES_PER_STEP), *([cache_v] * PAGES_PER_STEP), bias, bias_c, bias_w, wimp)


def _gate_weights(w_g):
    d = w_g.shape[0]
    w = jnp.transpose(w_g.reshape(d, 3, N_KV, HPG), (0, 2, 1, 3)).reshape(d, N_KV, 3 * HPG)
    return jnp.pad(w, ((0, 0), (0, 0), (0, LANE - 3 * HPG))).reshape(d, N_KV * LANE)


def nsa_layer(xp, xs, gain, w_in, q_norm, k_norm, pe, w1, w2, w_out, caches, layer, page_table,
              bias_p, u_bias, bias_s, bias_sc, bias_sw, past_len):
    cmp_k, cmp_v, slc_k, slc_v, win_k, win_v = caches
    nqkv = N_HEADS * HEAD_W + 6 * NSA_KV
    w_g = _gate_weights(w_in[layer, :, nqkv:])[None]
    w_in_t = jnp.swapaxes(w_in, 1, 2)
    nq = N_HEADS * HEAD_W
    tile = 4 * LANE
    q_gain = jnp.tile(q_norm, tile // HEAD_W)
    head_norm = (HEAD_W, {**{c: q_gain for c in range(nq // tile)},
                          nq // tile + 2: jnp.tile(k_norm[1], tile // HEAD_W),
                          nq // tile + 4: jnp.tile(k_norm[2], tile // HEAD_W)})
    outs = []
    for x, paged in ((xp, False), (xs, True)):
        b, t, d = x.shape
        x2 = x.reshape(b * t, d)
        proj = dense(x2, w_in_t, layer, gain=gain, n=nqkv, w_t=True, head_norm=head_norm).reshape(b, t, -1)
        gates = dense(x2, w_g, 0, gain=gain).reshape(b, t, -1)
        kcr, vcr, ks, vs, kw, vw = (proj[:, :, nq + a * NSA_KV:nq + (a + 1) * NSA_KV] for a in range(6))
        q = proj
        if paged:
            kc = compress(_as_page_rows(cmp_k), layer, page_table, pe[0], w1[0], w2[0], k_norm[0])
            vc = compress(_as_page_rows(cmp_v), layer, page_table, pe[1], w1[1], w2[1], None)
            n_wt = bias_sw.shape[0]
            kw_src = jnp.concatenate([win_k[layer], kw], axis=1)
            vw_src = jnp.concatenate([win_v[layer], vw], axis=1)
            o = nsa_decode(q, gates, kc, vc, _new_page_rows(ks), _new_page_rows(vs),
                           _pad_rows(kw_src, n_wt * LANE), _pad_rows(vw_src, n_wt * LANE),
                           _as_page_rows(slc_k), _as_page_rows(slc_v), layer, page_table,
                           bias_s, bias_sc, bias_sw, past_len)
            kw_out, vw_out = kw_src[:, -WINDOW:], vw_src[:, -WINDOW:]
        else:
            n_pg = t // PAGE
            ident = jnp.arange(b * n_pg, dtype=jnp.int32).reshape(b, n_pg)
            as_pages = lambda a: a.reshape(1, b * n_pg, PAGE_ROWS, HEAD_W)
            kc = compress(as_pages(kcr), 0, ident, pe[0], w1[0], w2[0], k_norm[0])
            vc = compress(as_pages(vcr), 0, ident, pe[1], w1[1], w2[1], None)
            o = nsa_prompt(proj, gates, kc, vc, bias_p, u_bias)
            kw_out, vw_out = kw[:, -WINDOW:], vw[:, -WINDOW:]
        y = dense(o.reshape(b * t, -1), w_out, layer, res=x2).reshape(b, t, d)
        st = tuple(a.reshape(b, -1, N_KV, HEAD_W) for a in (kcr, vcr, ks, vs, kw_out, vw_out))
        outs.append((y, st))
    (yp, stp), (ys, sts) = outs
    return yp, ys, stp, sts


def kernel(x_prompt, x_sample, cache_diff_k, cache_diff_v, state_ssm, state_conv, cache_nsa_cmp_k, cache_nsa_cmp_v, cache_nsa_slc_k, cache_nsa_slc_v, cache_nsa_win_k, cache_nsa_win_v, cache_mem_k, cache_mem_v, page_table, mem_prompt, rel_bias_table, norm_mix, norm_xattn, norm_mem, norm_ffn, diff_w_in, diff_q_norm, diff_k_norm, diff_lambda, diff_sub_norm, diff_w_out, ssm_w_in, ssm_conv_w, ssm_conv_b, ssm_dt_bias, ssm_a_log, ssm_d, ssm_norm, ssm_w_out, nsa_w_in, nsa_q_norm, nsa_k_norm, nsa_cmp_pe, nsa_cmp_w1, nsa_cmp_w2, nsa_w_out, xattn_w_q, xattn_w_k, xattn_w_v, xattn_q_norm, xattn_k_norm, xattn_w_o, ffn_w1, ffn_w3, ffn_w2):
    xp, xs = x_prompt, x_sample
    bp, t, d = xp.shape
    bs, t_new, _ = xs.shape
    past_len = page_table.shape[1] * PAGE
    depth = norm_mix.shape[0]

    bias_p = bias_tiles(_prompt_attn_buckets(PROMPT_TQ), rel_bias_table, heads_on_lanes=True, scale=LOG2E)
    u_bias = bias_tiles(_prompt_cmp_buckets(PROMPT_TQ, t // CMP_STRIDE), rel_bias_table, heads_on_lanes=True,
                        scale=LOG2E)
    bias_s = bias_tiles_indexed(_decode_page_index(t_new, past_len), rel_bias_table)
    bias_sc = bias_tiles(_decode_cmp_buckets(t_new, past_len), rel_bias_table)
    n_wt = -(-(WINDOW + t_new) // LANE)
    bias_sw = bias_tiles(_decode_win_buckets(t_new, n_wt), rel_bias_table)

    mem_k, mem_v = mem_kv(mem_prompt, norm_mem, xattn_w_k, xattn_w_v, xattn_k_norm)
    xw = X_HEADS * X_DH
    win_k = cache_nsa_win_k.reshape(*cache_nsa_win_k.shape[:3], NSA_KV)
    win_v = cache_nsa_win_v.reshape(*cache_nsa_win_v.shape[:3], NSA_KV)

    dkp, dvp, dks, dvs = [], [], [], []
    ssp, cvp, sss, cvs = [], [], [], []
    nsp, nss = [], []
    for i in range(depth):
        kind, j = i % 3, i // 3
        if kind == 0:
            lam_init = 0.8 - 0.6 * math.exp(-0.3 * i)
            xp, xs, kp_, vp_, ks_, vs_ = diff_layer(
                xp, xs, norm_mix[i], diff_w_in, diff_q_norm[j], diff_k_norm[j], diff_lambda[j],
                diff_sub_norm[j], diff_w_out, cache_diff_k, cache_diff_v, j, page_table,
                bias_p, bias_s, lam_init)
            dkp.append(kp_)
            dvp.append(vp_)
            dks.append(ks_)
            dvs.append(vs_)
        elif kind == 1:
            xp, xs, hp_, cp_, hs_, cs_ = ssd_layer(
                xp, xs, norm_mix[i], ssm_w_in, j, ssm_conv_w[j], ssm_conv_b[j], ssm_dt_bias[j], ssm_a_log[j],
                ssm_d[j], ssm_norm[j], ssm_w_out, state_conv[j], state_ssm[j])
            ssp.append(hp_)
            cvp.append(cp_)
            sss.append(hs_)
            cvs.append(cs_)
        else:
            caches = (cache_nsa_cmp_k, cache_nsa_cmp_v, cache_nsa_slc_k, cache_nsa_slc_v, win_k, win_v)
            xp, xs, stp, sts = nsa_layer(
                xp, xs, norm_mix[i], nsa_w_in, nsa_q_norm[j], nsa_k_norm[j], nsa_cmp_pe[j], nsa_cmp_w1[j],
                nsa_cmp_w2[j], nsa_w_out, caches, j, page_table,
                bias_p, u_bias, bias_s, bias_sc, bias_sw, past_len)
            nsp.append(stp)
            nss.append(sts)
        xp = xattn(xp, norm_xattn[i], xattn_w_q, xattn_q_norm[i], mem_k[i], mem_v[i], xattn_w_o, i)
        xs = xattn(xs, norm_xattn[i], xattn_w_q, xattn_q_norm[i], cache_mem_k[i].reshape(bs, N_MEM, xw),
                   cache_mem_v[i].reshape(bs, N_MEM, xw), xattn_w_o, i)
        xp = ffn(xp.reshape(bp * t, d), norm_ffn[i], ffn_w1, ffn_w3, ffn_w2, i).reshape(bp, t, d)
        xs = ffn(xs.reshape(bs * t_new, d), norm_ffn[i], ffn_w1, ffn_w3, ffn_w2, i).reshape(bs, t_new, d)

    st = lambda xs_: jnp.stack(xs_, axis=0)
    nsp_t = [st([s[a] for s in nsp]) for a in range(6)]
    nss_t = [st([s[a] for s in nss]) for a in range(6)]
    mem_shape = (depth, bp, N_MEM, X_HEADS, X_DH)
    return (xp, xs,
            st(dkp), st(dvp), st(dks), st(dvs),
            st(ssp), st(cvp), st(sss), st(cvs),
            *nsp_t, *nss_t,
            mem_k.reshape(mem_shape), mem_v.reshape(mem_shape))
```

```python
import functools
import math

import numpy as np
import jax
import jax.numpy as jnp
from jax import lax
from jax.experimental import pallas as pl
from jax.experimental.pallas import tpu as pltpu

F32 = jnp.float32
BF16 = jnp.bfloat16

D_MODEL = 2048
DEPTH = 4
PAGE = 128
N_HEADS = 16
N_KV = 4
HPG = N_HEADS // N_KV
DIFF_DH = 64
HEAD_W = 128
N_BUCKETS = 32
MAX_EXACT = 16
MAX_DIST = 128
D_INNER = 2 * D_MODEL
SSM_HEADDIM = 64
SSM_HEADS = D_INNER // SSM_HEADDIM
SSM_GROUPS = 8
SSM_HPG = SSM_HEADS // SSM_GROUPS
D_STATE = 128
CONV_W = 4
CONV_DIM = D_INNER + 2 * SSM_GROUPS * D_STATE
SSM_CHUNK = 128
CMP_STRIDE = 16
SLC_BLOCK = 64
SLC_RATIO = SLC_BLOCK // CMP_STRIDE
N_SELECT = 16
WINDOW = 512
N_MEM = 256
X_HEADS = 4
X_DH = 128
EPS = 1e-6
NEG_INF = -1e30
FORCE_SCORE = 1e4

LANE = 128
VMEM_LIMIT = 56 * 1024 * 1024
DENSE_TALL_ROWS = 2048


def _params(sem):
    return pltpu.CompilerParams(dimension_semantics=sem, vmem_limit_bytes=VMEM_LIMIT)


def _rms(x, gain):
    return x * lax.rsqrt(jnp.mean(x * x, axis=-1, keepdims=True) + EPS) * gain


def _dot(a, b):
    return jnp.dot(a, b, preferred_element_type=F32)


def _dot_t(a, b):
    return lax.dot_general(a, b, (((1,), (1,)), ((), ())), preferred_element_type=F32)


def _split3(x):
    hi = x.astype(BF16)
    r1 = x - hi.astype(F32)
    mid = r1.astype(BF16)
    lo = (r1 - mid.astype(F32)).astype(BF16)
    return hi, mid, lo


def _dot_exact_rhs(x, m_bf16):
    hi, mid, lo = _split3(x)
    return _dot(hi, m_bf16) + _dot(mid, m_bf16) + _dot(lo, m_bf16)


def _norm64(blk, gain, lo):
    sq = blk * blk
    s_lo = jnp.sum(jnp.where(lo, sq, 0.0), axis=-1, keepdims=True)
    s_hi = jnp.sum(jnp.where(lo, 0.0, sq), axis=-1, keepdims=True)
    ms = jnp.where(lo, s_lo, s_hi) * (1.0 / DIFF_DH)
    return blk * lax.rsqrt(ms + EPS) * gain


def _head_norm_tile(y, gain, seg):
    lo = lax.broadcasted_iota(jnp.int32, (y.shape[0], LANE), 1) < DIFF_DH
    blocks = []
    for c in range(y.shape[1] // LANE):
        sl = slice(c * LANE, (c + 1) * LANE)
        blocks.append(_norm64(y[:, sl], gain[:, sl], lo) if seg == DIFF_DH else _rms(y[:, sl], gain[:, sl]))
    return jnp.concatenate(blocks, axis=1)


def _dense_kernel(*refs, norm, residual, w_t, head_seg, norm_tiles):
    refs = list(refs)
    x_ref = refs.pop(0)
    g_ref = refs.pop(0) if norm else None
    w_ref = refs.pop(0)
    r_ref = refs.pop(0) if residual else None
    hg_ref = refs.pop(0) if head_seg else None
    o_ref, xb_ref = refs
    j = pl.program_id(1)

    @pl.when(j == 0)
    def _():
        x = x_ref[...]
        if norm:
            x = _rms(x, g_ref[...])
        xb_ref[...] = x.astype(BF16)

    w = w_ref[0].astype(BF16)
    y = _dot_t(xb_ref[...], w) if w_t else _dot(xb_ref[...], w)
    if residual:
        y = y + r_ref[...]
    if not head_seg:
        o_ref[...] = y
        return
    is_norm = functools.reduce(jnp.logical_or, [j == t for t in norm_tiles])

    @pl.when(is_norm)
    def _():
        o_ref[...] = _head_norm_tile(y, hg_ref[0], head_seg)

    @pl.when(jnp.logical_not(is_norm))
    def _():
        o_ref[...] = y


def dense(x, w, layer, gain=None, res=None, n=None, w_t=False, head_norm=None):
    m, k = x.shape
    n = w.shape[1 if w_t else 2] if n is None else n
    tm = m if m <= 512 else (DENSE_TALL_ROWS if k <= D_MODEL and m % DENSE_TALL_ROWS == 0 else 512)
    tn = LANE if n % 256 else (512 if n % 512 == 0 else 256)
    assert m % tm == 0 and n % tn == 0
    norm, residual = gain is not None, res is not None
    x_mode = dict(pipeline_mode=pl.Buffered(1)) if tm == DENSE_TALL_ROWS else {}
    in_specs = [pl.BlockSpec((tm, k), lambda i, j: (i, 0), **x_mode)]
    args = [x]
    if norm:
        in_specs.append(pl.BlockSpec((1, k), lambda i, j: (0, 0)))
        args.append(gain.reshape(1, k))
    if w_t:
        in_specs.append(pl.BlockSpec((1, tn, k), lambda i, j: (layer, j, 0)))
    else:
        in_specs.append(pl.BlockSpec((1, k, tn), lambda i, j: (layer, 0, j)))
    args.append(w)
    if residual:
        in_specs.append(pl.BlockSpec((tm, tn), lambda i, j: (i, j)))
        args.append(res)
    head_seg, norm_tiles = 0, ()
    if head_norm is not None:
        head_seg, tile_gains = head_norm
        norm_tiles = tuple(sorted(tile_gains))
        ones = jnp.ones((tn,), F32)
        gains = jnp.stack([tile_gains.get(t, ones) for t in range(n // tn)]).reshape(n // tn, 1, tn)
        in_specs.append(pl.BlockSpec((1, 1, tn), lambda i, j: (j, 0, 0)))
        args.append(gains)
    return pl.pallas_call(
        functools.partial(_dense_kernel, norm=norm, residual=residual, w_t=w_t, head_seg=head_seg,
                          norm_tiles=norm_tiles),
        grid=(m // tm, n // tn),
        in_specs=in_specs,
        out_specs=pl.BlockSpec((tm, tn), lambda i, j: (i, j)),
        out_shape=jax.ShapeDtypeStruct((m, n), F32),
        scratch_shapes=[pltpu.VMEM((tm, k), BF16)],
        compiler_params=_params(("parallel", "arbitrary")),
        name="dense",
    )(*args)


def _ffn_kernel(x_ref, g_ref, w1_ref, w3_ref, w2_ref, o_ref, xb_ref):
    @pl.when(pl.program_id(1) == 0)
    def _():
        x = x_ref[...]
        xb_ref[...] = _rms(x, g_ref[...]).astype(BF16)
        o_ref[...] = x

    xb = xb_ref[...]
    h1 = _dot(xb, w1_ref[0].astype(BF16))
    h3 = _dot(xb, w3_ref[0].astype(BF16))
    a = (h1 * jax.nn.sigmoid(h1) * h3).astype(BF16)
    o_ref[...] += _dot(a, w2_ref[0].astype(BF16))


def ffn(x, gain, w1, w3, w2, layer):
    m, d = x.shape
    f = w1.shape[2]
    tm = min(m, 1024)
    tf = 256
    assert m % tm == 0 and f % tf == 0
    return pl.pallas_call(
        _ffn_kernel,
        grid=(m // tm, f // tf),
        in_specs=[
            pl.BlockSpec((tm, d), lambda i, j: (i, 0), pipeline_mode=pl.Buffered(1)),
            pl.BlockSpec((1, d), lambda i, j: (0, 0)),
            pl.BlockSpec((1, d, tf), lambda i, j: (layer, 0, j)),
            pl.BlockSpec((1, d, tf), lambda i, j: (layer, 0, j)),
            pl.BlockSpec((1, tf, d), lambda i, j: (layer, j, 0)),
        ],
        out_specs=pl.BlockSpec((tm, d), lambda i, j: (i, 0)),
        out_shape=jax.ShapeDtypeStruct((m, d), F32),
        scratch_shapes=[pltpu.VMEM((tm, d), BF16)],
        compiler_params=_params(("parallel", "arbitrary")),
        name="ffn",
    )(x, gain.reshape(1, d), w1, w3, w2)


def _mem_kv_kernel(mem_ref, g_ref, wk_ref, wv_ref, kn_ref, k_ref, v_ref):
    m = _rms(mem_ref[0], g_ref[0]).astype(BF16)
    k = _dot(m, wk_ref[0].astype(BF16))
    v_ref[0, 0] = _dot(m, wv_ref[0].astype(BF16))
    for h in range(X_HEADS):
        sl = slice(h * X_DH, (h + 1) * X_DH)
        k_ref[0, 0, :, sl] = _rms(k[:, sl], kn_ref[0])


def mem_kv(mem, g_mem, wk, wv, k_norm):
    b = mem.shape[0]
    nl = wk.shape[0]
    hw = X_HEADS * X_DH
    shape = jax.ShapeDtypeStruct((nl, b, N_MEM, hw), F32)
    return pl.pallas_call(
        _mem_kv_kernel,
        grid=(nl, b),
        in_specs=[
            pl.BlockSpec((1, N_MEM, D_MODEL), lambda l, i: (i, 0, 0)),
            pl.BlockSpec((1, 1, D_MODEL), lambda l, i: (l, 0, 0)),
            pl.BlockSpec((1, D_MODEL, hw), lambda l, i: (l, 0, 0)),
            pl.BlockSpec((1, D_MODEL, hw), lambda l, i: (l, 0, 0)),
            pl.BlockSpec((1, 1, X_DH), lambda l, i: (l, 0, 0)),
        ],
        out_specs=[pl.BlockSpec((1, 1, N_MEM, hw), lambda l, i: (l, i, 0, 0))] * 2,
        out_shape=[shape, shape],
        compiler_params=_params(("parallel", "parallel")),
        name="mem_kv",
    )(mem, g_mem.reshape(nl, 1, D_MODEL), wk, wv, k_norm.reshape(nl, 1, X_DH))


def _xattn_kernel(x_ref, g_ref, wq_ref, qn_ref, k_ref, v_ref, wo_ref, o_ref):
    x = x_ref[0]
    q = _dot(_rms(x, g_ref[...]).astype(BF16), wq_ref[0].astype(BF16))
    outs = []
    for h in range(X_HEADS):
        sl = slice(h * X_DH, (h + 1) * X_DH)
        qh = _rms(q[:, sl], qn_ref[...]).astype(BF16)
        s = _dot_t(qh, k_ref[0, :, sl].astype(BF16)) * (X_DH ** -0.5)
        e = jnp.exp(s - jnp.max(s, axis=-1, keepdims=True))
        p = e / jnp.sum(e, axis=-1, keepdims=True)
        outs.append(_dot(p.astype(BF16), v_ref[0, :, sl].astype(BF16)))
    o = jnp.concatenate(outs, axis=1).astype(BF16)
    o_ref[0] = x + _dot(o, wo_ref[0].astype(BF16))


def xattn(x, gain, wq, q_norm, k, v, wo, layer):
    b, t, d = x.shape
    hw = X_HEADS * X_DH
    tm = min(t, 512)
    return pl.pallas_call(
        _xattn_kernel,
        grid=(b, t // tm),
        in_specs=[
            pl.BlockSpec((1, tm, d), lambda i, j: (i, j, 0)),
            pl.BlockSpec((1, d), lambda i, j: (0, 0)),
            pl.BlockSpec((1, d, hw), lambda i, j: (layer, 0, 0)),
            pl.BlockSpec((1, X_DH), lambda i, j: (0, 0)),
            pl.BlockSpec((1, N_MEM, hw), lambda i, j: (i, 0, 0)),
            pl.BlockSpec((1, N_MEM, hw), lambda i, j: (i, 0, 0)),
            pl.BlockSpec((1, hw, d), lambda i, j: (layer, 0, 0)),
        ],
        out_specs=pl.BlockSpec((1, tm, d), lambda i, j: (i, j, 0)),
        out_shape=jax.ShapeDtypeStruct((b, t, d), F32),
        compiler_params=_params(("parallel", "parallel")),
        name="xattn",
    )(x, gain.reshape(1, d), wq, q_norm.reshape(1, X_DH), k, v, wo)


def _bucket_np(dist):
    n = np.maximum(dist, 0)
    nf = np.maximum(n, 1).astype(np.float64)
    large = MAX_EXACT + (np.log(nf / MAX_EXACT) / math.log(MAX_DIST / MAX_EXACT)
                         * (N_BUCKETS - MAX_EXACT)).astype(np.int64)
    b = np.where(n < MAX_EXACT, n, np.minimum(large, N_BUCKETS - 1))
    return np.where(dist < 0, -1, b).astype(np.int32)


def _bias_kernel(bkt_ref, tab_ref, o_ref, *, scale):
    h = pl.program_id(1)
    b = bkt_ref[0]
    acc = jnp.full(b.shape, NEG_INF, F32)
    for k in range(N_BUCKETS):
        acc = jnp.where(b == k, tab_ref[k, h] * scale, acc)
    o_ref[0] = acc


def _bias_packed_kernel(idx_ref, tab_ref, o_ref, *, scale):
    idx = idx_ref[0]
    acc = jnp.full(idx.shape, NEG_INF, F32)
    for k in range(N_BUCKETS):
        for h in range(N_HEADS):
            acc = jnp.where(idx == k * N_HEADS + h, tab_ref[k, h] * scale, acc)
    o_ref[0] = acc


def bias_tiles_indexed(idx, table, scale=1.0):
    nt, r, w = idx.shape
    return pl.pallas_call(
        functools.partial(_bias_packed_kernel, scale=scale),
        grid=(nt,),
        in_specs=[
            pl.BlockSpec((1, r, w), lambda t: (t, 0, 0)),
            pl.BlockSpec(memory_space=pltpu.SMEM),
        ],
        out_specs=pl.BlockSpec((1, r, w), lambda t: (t, 0, 0)),
        out_shape=jax.ShapeDtypeStruct((nt, r, w), F32),
        compiler_params=_params(("parallel",)),
        name="bias_tiles_indexed",
    )(jnp.asarray(idx.astype(np.int32)), table)


def bias_tiles(buckets, table, heads_on_lanes=False, scale=1.0):
    nt, r, w = buckets.shape
    if heads_on_lanes:
        out_spec = pl.BlockSpec((1, r, w), lambda t, h: (t, 0, h))
        out_shape = (nt, r, N_HEADS * w)
    else:
        out_spec = pl.BlockSpec((1, r, w), lambda t, h: (t, h, 0))
        out_shape = (nt, N_HEADS * r, w)
    return pl.pallas_call(
        functools.partial(_bias_kernel, scale=scale),
        grid=(nt, N_HEADS),
        in_specs=[
            pl.BlockSpec((1, r, w), lambda t, h: (t, 0, 0)),
            pl.BlockSpec(memory_space=pltpu.SMEM),
        ],
        out_specs=out_spec,
        out_shape=jax.ShapeDtypeStruct(out_shape, F32),
        compiler_params=_params(("parallel", "parallel")),
        name="bias_tiles",
    )(jnp.asarray(buckets), table)


def _prompt_attn_buckets(tq):
    assert WINDOW % tq == 0 and tq >= MAX_DIST
    j = np.arange(tq)[:, None]
    i = np.arange(tq)[None, :]
    far = np.full((tq, tq), N_BUCKETS - 1, np.int32)
    return np.stack([
        _bucket_np(i - j),
        _bucket_np(i - j + tq),
        far,
        np.where(j > i, far, -1),
    ]).astype(np.int32)


def _diff_lambda(lam_ref, lam_init):
    lf = lam_ref[...]
    s01 = jnp.sum(lf[0:1] * lf[1:2], axis=-1, keepdims=True)
    s23 = jnp.sum(lf[2:3] * lf[3:4], axis=-1, keepdims=True)
    return jnp.exp(s01) - jnp.exp(s23) + lam_init


def _split_maps(qs):
    lo = lax.broadcasted_iota(jnp.int32, qs.shape, 1) < DIFF_DH
    return jnp.concatenate([jnp.where(lo, qs, 0.0), jnp.where(lo, 0.0, qs)], axis=0)


def _diff_finish_rows(o, lam, sub_norm, lam_init):
    r = o.shape[0] // 2
    a = o[:r] - lam * o[r:]
    return _rms(a, sub_norm) * (1.0 - lam_init)


PROMPT_TQ = 256


LOG2E = math.log2(math.e)
FLASH_CHUNKS = 4
FAR_TILES = 2


def _flash_tile_t(k_tile, vt_tile, qs_ref, m_ref, l_ref, acc_ref, scale=None, bias=None, far_bias=None,
                  mask=None, n_chunks=FLASH_CHUNKS):
    chunk = qs_ref.shape[0] // n_chunks
    logits = [_dot_t(k_tile, qs_ref[c * chunk:(c + 1) * chunk, :]) for c in range(n_chunks)]
    for c in range(n_chunks):
        cols = slice(c * chunk, (c + 1) * chunk)
        s = logits[c]
        if scale is not None:
            s = s * scale
        if bias is not None:
            s = s + bias(cols)
        if mask is not None:
            s = jnp.where(mask[:, cols] > 0.5, s, NEG_INF)
        m_old = m_ref[:, cols]
        s_max = jnp.max(s, axis=0, keepdims=True)
        if far_bias is None:
            m_new = jnp.maximum(m_old, s_max)
            p = jnp.exp2(s - m_new)
        else:
            fb = far_bias[:, cols]
            m_new = jnp.maximum(m_old, s_max + fb)
            p = jnp.exp2(s - (m_new - fb))
        alpha = jnp.exp2(m_old - m_new)
        l_ref[:, cols] = alpha * l_ref[:, cols] + jnp.sum(p, axis=0, keepdims=True)
        acc_ref[:, cols] = alpha * acc_ref[:, cols] + _dot(vt_tile, p.astype(BF16))
        m_ref[:, cols] = m_new


def _stage_kv(k_ref, v_ref, kb_ref, vt_ref, chunk):
    for c in range(k_ref.shape[1] // chunk):
        rows = slice(c * chunk, (c + 1) * chunk)
        kb_ref[rows, :] = k_ref[0, rows, :].astype(BF16)
        vt_ref[:, rows] = v_ref[0, rows, :].T.astype(BF16)


def _diff_flash_kernel(lam_ref, q_ref, k_ref, v_ref, bias_ref, sn_ref, o_ref,
                       qs_ref, kb_ref, vt_ref, m_ref, l_ref, acc_ref, *, tq, lam_init):
    qi = pl.program_id(2)

    @pl.when(qi == 0)
    def _():
        _stage_kv(k_ref, v_ref, kb_ref, vt_ref, tq)

    qb = q_ref[0]
    qs = jnp.concatenate([qb[:, r * HEAD_W:(r + 1) * HEAD_W] for r in range(HPG)], axis=0)
    qs_ref[...] = _split_maps(qs * (DIFF_DH ** -0.5 * LOG2E)).astype(BF16)
    _reset_flash(m_ref, l_ref, acc_ref)
    far = bias_ref[2, 0:1, :]
    far = jnp.concatenate([far, far], axis=1)

    def far_body(width):
        def body(kt, c):
            rows = pl.ds(pl.multiple_of(kt * width, width), width)
            _flash_tile_t(kb_ref[rows, :], vt_ref[:, rows], qs_ref, m_ref, l_ref, acc_ref, far_bias=far)
            return c
        return body

    def near_body(kt, c):
        rows = pl.ds(pl.multiple_of(kt * tq, tq), tq)
        b = bias_ref[qi - kt]
        _flash_tile_t(kb_ref[rows, :], vt_ref[:, rows], qs_ref, m_ref, l_ref, acc_ref,
                      bias=lambda cols: jnp.concatenate([b, b], axis=1), n_chunks=1)
        return c

    n_far = jnp.maximum(qi - 1, 0)
    lax.fori_loop(0, n_far // FAR_TILES, far_body(FAR_TILES * tq), 0)
    lax.fori_loop((n_far // FAR_TILES) * FAR_TILES, n_far, far_body(tq), 0)
    lax.fori_loop(n_far, qi + 1, near_body, 0)
    o = (acc_ref[...] / l_ref[...]).T
    a = _diff_finish_rows(o, _diff_lambda(lam_ref, lam_init), sn_ref[...], lam_init)
    for r in range(HPG):
        o_ref[0, :, r * HEAD_W:(r + 1) * HEAD_W] = a[r * tq:(r + 1) * tq]


def diff_flash(qkv, bias, lam_p, sub_norm, lam_init, tq=PROMPT_TQ):
    b, t, _ = qkv.shape
    gw = HPG * HEAD_W
    rows = 2 * HPG * tq
    k0 = N_HEADS
    return pl.pallas_call(
        functools.partial(_diff_flash_kernel, tq=tq, lam_init=lam_init),
        grid=(b, N_KV, t // tq),
        in_specs=[
            pl.BlockSpec((4, DIFF_DH), lambda i, g, j: (0, 0)),
            pl.BlockSpec((1, tq, gw), lambda i, g, j: (i, j, g)),
            pl.BlockSpec((1, t, HEAD_W), lambda i, g, j: (i, 0, k0 + g)),
            pl.BlockSpec((1, t, HEAD_W), lambda i, g, j: (i, 0, k0 + N_KV + g)),
            pl.BlockSpec((4, tq, HPG * tq), lambda i, g, j: (0, 0, g)),
            pl.BlockSpec((1, HEAD_W), lambda i, g, j: (0, 0)),
        ],
        out_specs=pl.BlockSpec((1, tq, gw), lambda i, g, j: (i, j, g)),
        out_shape=jax.ShapeDtypeStruct((b, t, N_HEADS * HEAD_W), F32),
        scratch_shapes=[
            pltpu.VMEM((rows, HEAD_W), BF16),
            pltpu.VMEM((t, HEAD_W), BF16),
            pltpu.VMEM((HEAD_W, t), BF16),
            pltpu.VMEM((1, rows), F32),
            pltpu.VMEM((1, rows), F32),
            pltpu.VMEM((HEAD_W, rows), F32),
        ],
        compiler_params=_params(("parallel", "parallel", "arbitrary")),
        name="diff_flash",
    )(lam_p, qkv, qkv, qkv, bias, sub_norm.reshape(1, HEAD_W))


PAGES_PER_STEP = 16


def _decode_page_index(t_new, past_len):
    j = np.arange(PAGE)[:, None]
    i = np.arange(t_new)[None, :]
    far = np.full((PAGE, t_new), N_BUCKETS - 1, np.int32)
    new = np.where(j < t_new, _bucket_np(i - j), -1)
    g = np.arange(N_KV)[None, :, None, None]
    h = np.arange(N_HEADS)[None, None, :, None]
    tiles = []
    for bkt in (far, _bucket_np(PAGE + i - j), new):
        b = bkt[:, None, None, :]
        idx = np.where((b >= 0) & (h // HPG == g), b * N_HEADS + h, -1)
        tiles.append(idx.reshape(PAGE * N_KV, N_HEADS * t_new))
    return np.stack(tiles).astype(np.int32)


def _stack_group_heads(q_ref, g):
    return jnp.concatenate(
        [q_ref[0, :, (g * HPG + r) * HEAD_W:(g * HPG + r + 1) * HEAD_W] for r in range(HPG)], axis=0)


def _query_rows(q_ref, maps, scale):
    pieces = []
    for m in range(maps):
        for h in range(N_HEADS):
            qh = q_ref[0, :, h * HEAD_W:(h + 1) * HEAD_W] * scale
            if maps == 2:
                lo = lax.broadcasted_iota(jnp.int32, qh.shape, 1) < DIFF_DH
                qh = jnp.where(lo, qh, 0.0) if m == 0 else jnp.where(lo, 0.0, qh)
            pieces.append(qh)
    return jnp.concatenate(pieces, axis=0).astype(BF16)


def _page_bias_t(bias_ref, first_page, n_pages, reps):
    tiles = [bias_ref[jnp.where(first_page + u == n_pages - 1, 1, 0)] for u in range(PAGES_PER_STEP)]
    b = jnp.concatenate(tiles, axis=0)
    return jnp.concatenate([b] * reps, axis=1)


def _decode_step_t(k_rows, v_rows, q_ref, bias, m_ref, l_ref, acc_ref, keep=None):
    s = _dot_t(k_rows.astype(BF16), q_ref[...]) + bias
    if keep is not None:
        s = jnp.where(keep > 0.5, s, NEG_INF)
    m_old = m_ref[...]
    m_new = jnp.maximum(m_old, jnp.max(s, axis=0, keepdims=True))
    alpha = jnp.exp(m_old - m_new)
    p = jnp.exp(s - m_new)
    l_ref[...] = alpha * l_ref[...] + jnp.sum(p, axis=0, keepdims=True)
    acc_ref[...] = alpha * acc_ref[...] + _dot(v_rows.T.astype(BF16), p.astype(BF16))
    m_ref[...] = m_new


def _page_rows(refs):
    return jnp.concatenate([r[0, 0] for r in refs], axis=0)


def _diff_decode_kernel(pt_ref, lam_ref, q_ref, kn_ref, vn_ref, *rest, n_pages, lam_init):
    del pt_ref
    kp = rest[:PAGES_PER_STEP]
    vp = rest[PAGES_PER_STEP:2 * PAGES_PER_STEP]
    bias_ref, sn_ref, o_ref, qr_ref, m_ref, l_ref, acc_ref = rest[2 * PAGES_PER_STEP:]
    s = pl.program_id(1)
    t_new = q_ref.shape[1]

    @pl.when(s == 0)
    def _():
        _reset_flash(m_ref, l_ref, acc_ref)
        qr_ref[...] = _query_rows(q_ref, 2, DIFF_DH ** -0.5)
        b = bias_ref[2]
        _decode_step_t(kn_ref[0], vn_ref[0], qr_ref, jnp.concatenate([b, b], axis=1), m_ref, l_ref, acc_ref)

    _decode_step_t(_page_rows(kp), _page_rows(vp), qr_ref,
                   _page_bias_t(bias_ref, s * PAGES_PER_STEP, n_pages, 2), m_ref, l_ref, acc_ref)

    @pl.when(s == pl.num_programs(1) - 1)
    def _():
        o = (acc_ref[...] / l_ref[...]).T
        a = _diff_finish_rows(o, _diff_lambda(lam_ref, lam_init), sn_ref[...], lam_init)
        for h in range(N_HEADS):
            o_ref[0, :, h * HEAD_W:(h + 1) * HEAD_W] = a[h * t_new:(h + 1) * t_new]


PAGE_ROWS = PAGE * N_KV


def _as_page_rows(cache):
    return cache.reshape(*cache.shape[:2], PAGE_ROWS, HEAD_W)


def _page_specs(layer, n):
    def spec(u):
        return pl.BlockSpec((1, 1, PAGE_ROWS, HEAD_W),
                            lambda i, s, pt: (layer, pt[i, s * PAGES_PER_STEP + u], 0, 0))
    return [spec(u) for u in range(PAGES_PER_STEP)] * n


def diff_decode(q, k_new, v_new, cache_k, cache_v, layer, page_table, bias, lam_p, sub_norm, lam_init):
    b, t_new, _ = q.shape
    n_pages = page_table.shape[1]
    lanes = 2 * N_HEADS * t_new
    assert N_HEADS * t_new == LANE
    fixed = lambda *shape: pl.BlockSpec(shape, lambda i, s, pt: (0,) * len(shape))
    per_b = lambda *shape: pl.BlockSpec((1,) + shape, lambda i, s, pt: (i,) + (0,) * len(shape))
    grid_spec = pltpu.PrefetchScalarGridSpec(
        num_scalar_prefetch=1,
        grid=(b, n_pages // PAGES_PER_STEP),
        in_specs=[fixed(4, DIFF_DH), per_b(t_new, N_HEADS * HEAD_W), per_b(PAGE_ROWS, HEAD_W),
                  per_b(PAGE_ROWS, HEAD_W)]
        + _page_specs(layer, 2)
        + [fixed(*bias.shape), fixed(1, HEAD_W)],
        out_specs=per_b(t_new, N_HEADS * HEAD_W),
        scratch_shapes=[
            pltpu.VMEM((lanes, HEAD_W), BF16),
            pltpu.VMEM((1, lanes), F32),
            pltpu.VMEM((1, lanes), F32),
            pltpu.VMEM((HEAD_W, lanes), F32),
        ],
    )
    return pl.pallas_call(
        functools.partial(_diff_decode_kernel, n_pages=n_pages, lam_init=lam_init),
        grid_spec=grid_spec,
        out_shape=jax.ShapeDtypeStruct(q.shape[:2] + (N_HEADS * HEAD_W,), F32),
        compiler_params=_params(("parallel", "arbitrary")),
        name="diff_decode",
    )(page_table, lam_p, q, k_new, v_new, *([cache_k] * PAGES_PER_STEP), *([cache_v] * PAGES_PER_STEP),
      bias, sub_norm.reshape(1, HEAD_W))


def _pad_rows(a, n):
    return jnp.pad(a, ((0, 0), (0, n - a.shape[1]), (0, 0)))


def _new_page_rows(a):
    return _pad_rows(a, PAGE).reshape(a.shape[0], PAGE_ROWS, HEAD_W)


def diff_layer(xp, xs, gain, w_in, q_norm, k_norm, lam_p, sub_norm, w_out, cache_k, cache_v, layer,
               page_table, bias_p, bias_s, lam_init):
    nq, nk = N_HEADS * HEAD_W, N_KV * HEAD_W
    tile = 4 * LANE
    q_gain, k_gain = jnp.tile(q_norm, tile // DIFF_DH), jnp.tile(k_norm, tile // DIFF_DH)
    head_norm = (DIFF_DH, {**{c: q_gain for c in range(nq // tile)}, nq // tile: k_gain})
    outs = []
    for x, paged in ((xp, False), (xs, True)):
        b, t, d = x.shape
        x2 = x.reshape(b * t, d)
        qkv = dense(x2, w_in, layer, gain=gain, head_norm=head_norm).reshape(b, t, -1)
        k3, v3 = qkv[:, :, nq:nq + nk], qkv[:, :, nq + nk:]
        if paged:
            o = diff_decode(qkv, _new_page_rows(k3), _new_page_rows(v3), _as_page_rows(cache_k),
                            _as_page_rows(cache_v), layer, page_table, bias_s, lam_p, sub_norm, lam_init)
        else:
            o = diff_flash(qkv, bias_p, lam_p, sub_norm, lam_init)
        y = dense(o.reshape(b * t, -1), w_out, layer, res=x2).reshape(b, t, d)
        outs.append((y, k3.reshape(b, t, N_KV, HEAD_W), v3.reshape(b, t, N_KV, HEAD_W)))
    (yp, kp, vp), (ys, ks, vs) = outs
    return yp, ys, kp, vp, ks, vs


SSM_GW = SSM_HPG * SSM_HEADDIM
SSM_BC = 2 * SSM_GROUPS * D_STATE
CONV_PAD = 8


def _conv_silu(buf_ref, w_ref, b_ref, n):
    acc = b_ref[...]
    for k in range(CONV_W):
        acc = acc + buf_ref[pl.ds(CONV_PAD - (CONV_W - 1) + k, n), :] * w_ref[k:k + 1, :]
    return acc * jax.nn.sigmoid(acc)


def _ssd_kernel(z_ref, x_ref, bc_ref, dt_ref, cbx_ref, cbbc_ref, wx_ref, wbc_ref, bx_ref, bbc_ref,
                dtb_ref, alog_ref, dsk_ref, ng_ref, e_ref, tri_ref, h0_ref,
                y_ref, hout_ref, ht_ref, xbuf_ref, bcbuf_ref, xa_ref, bca_ref, *, t_valid):
    c = pl.program_id(1)
    n = x_ref.shape[1]

    @pl.when(c == 0)
    def _():
        for g in range(SSM_GROUPS):
            ht_ref[g] = h0_ref[0, g].T
        xbuf_ref[0:CONV_PAD] = cbx_ref[0]
        bcbuf_ref[0:CONV_PAD] = cbbc_ref[0]

    xbuf_ref[CONV_PAD:CONV_PAD + n] = x_ref[0]
    bcbuf_ref[CONV_PAD:CONV_PAD + n] = bc_ref[0]
    xa_ref[...] = _conv_silu(xbuf_ref, wx_ref, bx_ref, n)
    bca_ref[...] = _conv_silu(bcbuf_ref, wbc_ref, bbc_ref, n)
    xbuf_ref[0:CONV_PAD] = xbuf_ref[n:n + CONV_PAD]
    bcbuf_ref[0:CONV_PAD] = bcbuf_ref[n:n + CONV_PAD]

    row = lax.broadcasted_iota(jnp.int32, (n, LANE), 0) + c * n
    dtr = dt_ref[0] + dtb_ref[...]
    dt = jnp.maximum(dtr, 0.0) + jnp.log1p(jnp.exp(-jnp.abs(dtr)))
    dt = jnp.where(row < t_valid, dt, 0.0)
    dta = dt * (-jnp.exp(alog_ref[...]))
    hi, mid, lo = _split3(dta)
    tri = tri_ref[...]
    cs = _dot(tri, hi) + _dot(tri, mid) + _dot(tri, lo)
    cs_last = cs[n - 1:n, :]
    cs_t = cs.T
    dt_t = dt.T
    stacked = jnp.concatenate(
        [jnp.exp(cs), jnp.exp(cs_last - cs) * dt, jnp.broadcast_to(jnp.exp(cs_last), (8, LANE))], axis=0)
    ex = _dot_exact_rhs(stacked, e_ref[...])
    causal = (lax.broadcasted_iota(jnp.int32, (n, n), 0) >= lax.broadcasted_iota(jnp.int32, (n, n), 1))

    for g in range(SSM_GROUPS):
        gs = slice(g * SSM_GW, (g + 1) * SSM_GW)
        bm = bca_ref[:, g * D_STATE:(g + 1) * D_STATE]
        cm = bca_ref[:, (SSM_GROUPS + g) * D_STATE:(SSM_GROUPS + g + 1) * D_STATE].astype(BF16)
        cb = _dot_t(cm, bm.astype(BF16))
        xg = xa_ref[:, gs]
        ys = []
        for r in range(SSM_HPG):
            h = g * SSM_HPG + r
            seg = cs[:, h:h + 1] - cs_t[h:h + 1, :]
            dec = jnp.where(causal, jnp.exp(jnp.where(causal, seg, 0.0)), 0.0)
            mm = (cb * dec * dt_t[h:h + 1, :]).astype(BF16)
            ys.append(_dot(mm, xg[:, r * SSM_HEADDIM:(r + 1) * SSM_HEADDIM].astype(BF16)))
        ht = ht_ref[g]
        y = jnp.concatenate(ys, axis=1) + _dot(cm, ht.astype(BF16)) * ex[0:n, gs]
        y = y + dsk_ref[:, gs] * xg
        zg = z_ref[0, :, gs]
        y = y * (zg * jax.nn.sigmoid(zg))
        y_ref[0, :, gs] = _rms(y, ng_ref[:, gs])
        xw = (xg * ex[n:2 * n, gs]).astype(BF16)
        ht_ref[g] = ht * ex[2 * n:2 * n + 1, gs] + _dot(bm.T.astype(BF16), xw)

    @pl.when(c == pl.num_programs(1) - 1)
    def _():
        for g in range(SSM_GROUPS):
            hout_ref[0, g] = ht_ref[g].T


def _head_expand_matrix():
    e = np.zeros((LANE, D_INNER), np.float32)
    for h in range(SSM_HEADS):
        e[h, h * SSM_HEADDIM:(h + 1) * SSM_HEADDIM] = 1.0
    return e


def ssd_core(zx, dt_raw, conv_buf, h0, conv_w, conv_b, dt_bias, a_log, d_skip, norm_g, t_valid):
    b, t, _ = zx.shape
    n = SSM_CHUNK
    pad_h = LANE - SSM_HEADS
    cb = jnp.pad(conv_buf, ((0, 0), (CONV_PAD - (CONV_W - 1), 0), (0, 0)))
    tri = jnp.asarray(np.tril(np.ones((n, n), np.float32)), BF16)
    e = jnp.asarray(_head_expand_matrix(), BF16)
    d_exp = jnp.repeat(d_skip, SSM_HEADDIM).reshape(1, D_INNER)
    fixed = lambda *shape: pl.BlockSpec(shape, lambda i, c: (0,) * len(shape))
    per_b = lambda *shape: pl.BlockSpec((1,) + shape, lambda i, c: (i,) + (0,) * len(shape))
    y, h_last = pl.pallas_call(
        functools.partial(_ssd_kernel, t_valid=t_valid),
        grid=(b, t // n),
        in_specs=[
            pl.BlockSpec((1, n, D_INNER), lambda i, c: (i, c, 0)),
            pl.BlockSpec((1, n, D_INNER), lambda i, c: (i, c, 1)),
            pl.BlockSpec((1, n, SSM_BC), lambda i, c: (i, c, 2 * D_INNER // SSM_BC)),
            pl.BlockSpec((1, n, LANE), lambda i, c: (i, c, 0)),
            per_b(CONV_PAD, D_INNER), per_b(CONV_PAD, SSM_BC),
            fixed(CONV_W, D_INNER), fixed(CONV_W, SSM_BC), fixed(1, D_INNER), fixed(1, SSM_BC),
            fixed(1, LANE), fixed(1, LANE), fixed(1, D_INNER), fixed(1, D_INNER),
            fixed(LANE, D_INNER), fixed(n, n),
            per_b(SSM_GROUPS, SSM_GW, D_STATE),
        ],
        out_specs=[
            pl.BlockSpec((1, n, D_INNER), lambda i, c: (i, c, 0)),
            per_b(SSM_GROUPS, SSM_GW, D_STATE),
        ],
        out_shape=[jax.ShapeDtypeStruct((b, t, D_INNER), F32),
                   jax.ShapeDtypeStruct((b, SSM_GROUPS, SSM_GW, D_STATE), F32)],
        scratch_shapes=[
            pltpu.VMEM((SSM_GROUPS, D_STATE, SSM_GW), F32),
            pltpu.VMEM((n + CONV_PAD, D_INNER), F32),
            pltpu.VMEM((n + CONV_PAD, SSM_BC), F32),
            pltpu.VMEM((n, D_INNER), F32),
            pltpu.VMEM((n, SSM_BC), F32),
        ],
        compiler_params=_params(("parallel", "arbitrary")),
        name="ssd_core",
    )(zx, zx, zx, dt_raw, cb[:, :, :D_INNER], cb[:, :, D_INNER:],
      conv_w[:, :D_INNER], conv_w[:, D_INNER:], conv_b[:D_INNER].reshape(1, -1), conv_b[D_INNER:].reshape(1, -1),
      jnp.pad(dt_bias, (0, pad_h)).reshape(1, LANE), jnp.pad(a_log, (0, pad_h)).reshape(1, LANE),
      d_exp, norm_g.reshape(1, D_INNER), e, tri,
      h0.reshape(b, SSM_GROUPS, SSM_GW, D_STATE))
    return y, h_last.reshape(b, SSM_HEADS, SSM_HEADDIM, D_STATE)


def ssd_layer(xp, xs, gain, w_in, layer, conv_w, conv_b, dt_bias, a_log, d_skip, norm_g, w_out,
              state_conv, state_ssm):
    nzx = D_INNER + CONV_DIM
    w_dt = jnp.pad(w_in[layer, :, nzx:], ((0, 0), (0, LANE - SSM_HEADS)))[None]
    w_in_t = jnp.swapaxes(w_in, 1, 2)
    outs = []
    for x, conv_buf, h0 in ((xp, None, None), (xs, state_conv, state_ssm)):
        b, t, d = x.shape
        x2 = x.reshape(b * t, d)
        if conv_buf is None:
            conv_buf = jnp.zeros((b, CONV_W - 1, CONV_DIM), F32)
            h0 = jnp.zeros((b, SSM_HEADS, SSM_HEADDIM, D_STATE), F32)
        zx = dense(x2, w_in_t, layer, gain=gain, n=nzx, w_t=True).reshape(b, t, -1)
        dt_raw = dense(x2, w_dt, 0, gain=gain).reshape(b, t, LANE)
        tp = -(-t // SSM_CHUNK) * SSM_CHUNK
        y, h_last = ssd_core(_pad_rows(zx, tp), _pad_rows(dt_raw, tp), conv_buf, h0, conv_w, conv_b,
                             dt_bias, a_log, d_skip, norm_g, t)
        y = y[:, :t].reshape(b * t, D_INNER)
        tail = min(t, CONV_W - 1)
        conv_out = jnp.concatenate([conv_buf[:, tail:], zx[:, t - tail:, D_INNER:]], axis=1)
        outs.append((dense(y, w_out, layer, res=x2).reshape(b, t, d), h_last, conv_out))
    (yp, hp, cp), (ys, hs, cs) = outs
    return yp, ys, hp, cp, hs, cs


NSA_KV = N_KV * HEAD_W
NSA_SCALE = HEAD_W ** -0.5
CMP_PAIRS = CMP_STRIDE // 2
CMP_PAGES = 32


def _cmp_uv_kernel(pt_ref, *refs):
    del pt_ref
    pages = refs[:-2]
    wab_ref, o_ref = refs[-2:]
    subs = PAGE // CMP_STRIDE
    for g in range(N_KV):
        acc = jnp.zeros((len(pages) * subs, 2 * HEAD_W), F32)
        for lp in range(CMP_PAIRS):
            halves = []
            for li in range(2):
                rows = pl.ds((2 * lp + li) * N_KV + g, subs, stride=CMP_STRIDE * N_KV)
                halves.append(jnp.concatenate([p[0, 0, rows, :] for p in pages], axis=0))
            acc = acc + _dot(jnp.concatenate(halves, axis=1).astype(BF16), wab_ref[lp])
        o_ref[0, :, g * 2 * HEAD_W:(g + 1) * 2 * HEAD_W] = acc


def cmp_uv(rows, layer, page_table, wab):
    b, n_pages = page_table.shape
    subs = PAGE // CMP_STRIDE
    step_pages = min(CMP_PAGES, n_pages)
    assert n_pages % step_pages == 0

    def spec(u):
        return pl.BlockSpec((1, 1, PAGE_ROWS, HEAD_W),
                            lambda i, c, pt: (layer, pt[i, c * step_pages + u], 0, 0))

    grid_spec = pltpu.PrefetchScalarGridSpec(
        num_scalar_prefetch=1,
        grid=(b, n_pages // step_pages),
        in_specs=[spec(u) for u in range(step_pages)]
        + [pl.BlockSpec(wab.shape, lambda i, c, pt: (0, 0, 0))],
        out_specs=pl.BlockSpec((1, step_pages * subs, 2 * NSA_KV), lambda i, c, pt: (i, c, 0)),
    )
    return pl.pallas_call(
        _cmp_uv_kernel,
        grid_spec=grid_spec,
        out_shape=jax.ShapeDtypeStruct((b, n_pages * subs, 2 * NSA_KV), F32),
        compiler_params=_params(("parallel", "arbitrary")),
        name="cmp_uv",
    )(page_table, *([rows] * step_pages), wab)


def _cmp_finish_kernel(uv_ref, pe_ref, w1_ref, w2_ref, kn_ref, o_ref, *, norm):
    n = uv_ref.shape[1]
    c = _dot(jnp.broadcast_to(pe_ref[...], (8, pe_ref.shape[1])).astype(BF16), w1_ref[...])[0:1]
    for g in range(N_KV):
        u = uv_ref[0, :, g * 2 * HEAD_W:g * 2 * HEAD_W + HEAD_W]
        v = uv_ref[0, :, g * 2 * HEAD_W + HEAD_W:(g + 1) * 2 * HEAD_W]
        pre = u + pltpu.roll(v, n - 1, axis=0) + c
        out = _dot((pre * jax.nn.sigmoid(pre)).astype(BF16), w2_ref[...])
        if norm:
            out = _rms(out, kn_ref[...])
        o_ref[0, :, g * HEAD_W:(g + 1) * HEAD_W] = out


def cmp_finish(uv, pe, w1, w2, k_norm):
    b, n, _ = uv.shape
    norm = k_norm is not None
    kn = (k_norm if norm else jnp.ones((HEAD_W,), F32)).reshape(1, HEAD_W)
    return pl.pallas_call(
        functools.partial(_cmp_finish_kernel, norm=norm),
        grid=(b,),
        in_specs=[
            pl.BlockSpec((1, n, 2 * NSA_KV), lambda i: (i, 0, 0)),
            pl.BlockSpec((1, pe.size), lambda i: (0, 0)),
            pl.BlockSpec(w1.shape, lambda i: (0, 0)),
            pl.BlockSpec(w2.shape, lambda i: (0, 0)),
            pl.BlockSpec((1, HEAD_W), lambda i: (0, 0)),
        ],
        out_specs=pl.BlockSpec((1, n, NSA_KV), lambda i: (i, 0, 0)),
        out_shape=jax.ShapeDtypeStruct((b, n, NSA_KV), F32),
        compiler_params=_params(("parallel",)),
        name="cmp_finish",
    )(uv, pe.reshape(1, -1), w1, w2, kn)


def compress(rows, layer, page_table, pe, w1, w2, k_norm):
    w1r = w1.reshape(2, CMP_STRIDE, HEAD_W, HEAD_W)
    wab = jnp.transpose(w1r, (1, 2, 0, 3)).reshape(CMP_PAIRS, 2 * HEAD_W, 2 * HEAD_W).astype(BF16)
    uv = cmp_uv(rows, layer, page_table, wab)
    return cmp_finish(uv, pe, w1.astype(BF16), w2.astype(BF16), k_norm)


def _importance_matrix(n_cmp, n_rows, n_slc, n_cols):
    w = np.zeros((n_rows, n_cols), np.float32)
    w_imp = [1.0] + [2.0] * (SLC_RATIO - 1) + [1.0]
    for s in range(n_slc):
        for m, wm in enumerate(w_imp):
            j = SLC_RATIO * s + m - 1
            if 0 <= j < n_cmp:
                w[j, s] += wm
    return w


def _select_blocks(s_slc, q_pos0, n_slc):
    t, w = s_slc.shape
    blk = lax.broadcasted_iota(jnp.int32, (t, w), 1)
    qpos = q_pos0 + lax.broadcasted_iota(jnp.int32, (t, w), 0)
    qb = qpos // SLC_BLOCK
    forced = (blk == 0) | (blk == qb) | (blk == qb - 1)
    score = jnp.where(forced, FORCE_SCORE, jnp.where(blk * SLC_BLOCK <= qpos, s_slc, -1.0))
    score = jnp.where(blk < n_slc, score, -2.0)
    cnt = jnp.zeros((t, w), F32)
    for sp in range(n_slc):
        col = score[:, sp:sp + 1]
        tie = jnp.where(blk > sp, 1.0, 0.0)
        cnt = cnt + jnp.where(col > score, 1.0, jnp.where(col == score, tie, 0.0))
    return jnp.where(cnt < N_SELECT, 1.0, 0.0)


def _masked_softmax(s, valid, axis=-1, base2=False):
    m = jnp.max(s, axis=axis, keepdims=True)
    e = jnp.where(valid, (jnp.exp2 if base2 else jnp.exp)(s - m), 0.0)
    return e / jnp.maximum(jnp.sum(e, axis=axis, keepdims=True), 1e-30)


def _reset_flash(m_ref, l_ref, acc_ref):
    m_ref[...] = jnp.full(m_ref.shape, NEG_INF, F32)
    l_ref[...] = jnp.zeros(l_ref.shape, F32)
    acc_ref[...] = jnp.zeros(acc_ref.shape, F32)


def _select_blocks_t(s_slc, q_pos0, n_slc):
    w, t = s_slc.shape
    n8 = -(-n_slc // 8) * 8
    blk = lax.broadcasted_iota(jnp.int32, (n8, t), 0)
    qpos = q_pos0 + lax.broadcasted_iota(jnp.int32, (n8, t), 1)
    qb = qpos // SLC_BLOCK
    forced = (blk == 0) | (blk == qb) | (blk == qb - 1)
    score = jnp.where(forced, FORCE_SCORE, jnp.where(blk * SLC_BLOCK <= qpos, s_slc[:n8], -1.0))
    score = jnp.where(blk < n_slc, score, -2.0)
    cnt = jnp.zeros((n8, t), F32)
    for sp in range(n_slc):
        row = score[sp:sp + 1, :]
        tie = jnp.where(blk > sp, 1.0, 0.0)
        cnt = cnt + jnp.where(row > score, 1.0, jnp.where(row == score, tie, 0.0))
    sel = jnp.where(cnt < N_SELECT, 1.0, 0.0)
    if n8 < w:
        sel = jnp.concatenate([sel, jnp.zeros((w - n8, t), F32)], axis=0)
    return sel


def _nsa_prompt_kernel(q_ref, gt_ref, kc_ref, vc_ref, ks_ref, vs_ref, kw_ref, vw_ref, bias_ref, u_ref, wimp_ref,
                       o_ref, qs_ref, ksb_ref, vst_ref, kwb_ref, vwt_ref, sel_ref, m_ref, l_ref, acc_ref,
                       *, tq, n_slc):
    qi = pl.program_id(2)

    @pl.when(qi == 0)
    def _():
        _stage_kv(ks_ref, vs_ref, ksb_ref, vst_ref, tq)
        _stage_kv(kw_ref, vw_ref, kwb_ref, vwt_ref, tq)

    qs_ref[...] = jnp.concatenate(
        [q_ref[0, :, r * HEAD_W:(r + 1) * HEAD_W] for r in range(HPG)], axis=0).astype(BF16)

    n_pad = kc_ref.shape[1]
    start = pl.multiple_of(n_pad - (tq // CMP_STRIDE) * (qi + 1), 8)
    scale = NSA_SCALE * LOG2E
    bias_c = u_ref[0, pl.ds(start, n_pad), :]
    s = _dot_t(kc_ref[0].astype(BF16), qs_ref[...]) * scale + bias_c
    pc = _masked_softmax(s, bias_c > 0.5 * NEG_INF, axis=0, base2=True)
    o_cmp = _dot(vc_ref[0].T.astype(BF16), pc.astype(BF16))
    psum = pc[:, 0:tq] + pc[:, tq:2 * tq] + pc[:, 2 * tq:3 * tq] + pc[:, 3 * tq:4 * tq]
    hi, mid, lo = _split3(psum)
    wimp = wimp_ref[...]
    s_slc = _dot(wimp, hi) + _dot(wimp, mid) + _dot(wimp, lo)
    sel_ref[...] = _select_blocks_t(s_slc, qi * tq, n_slc).astype(BF16)

    _reset_flash(m_ref, l_ref, acc_ref)

    far = bias_ref[2, 0:1, :]

    def key_mask(kt, width):
        key = lax.broadcasted_iota(jnp.int32, (width, LANE), 0)
        blk = lax.broadcasted_iota(jnp.int32, (width, LANE), 1)
        onehot = jnp.where(blk == kt * (width // SLC_BLOCK) + key // SLC_BLOCK, 1.0, 0.0).astype(BF16)
        return jnp.concatenate([_dot(onehot, sel_ref[...])] * HPG, axis=1)

    def slc_far_body(width):
        def body(kt, c):
            rows = pl.ds(pl.multiple_of(kt * width, width), width)
            _flash_tile_t(ksb_ref[rows, :], vst_ref[:, rows], qs_ref, m_ref, l_ref, acc_ref, scale=scale,
                          far_bias=far, mask=key_mask(kt, width))
            return c
        return body

    def slc_near_body(kt, c):
        rows = pl.ds(pl.multiple_of(kt * tq, tq), tq)
        typ = qi - kt
        _flash_tile_t(ksb_ref[rows, :], vst_ref[:, rows], qs_ref, m_ref, l_ref, acc_ref, scale=scale,
                      bias=lambda cols: bias_ref[typ, :, cols], mask=key_mask(kt, tq))
        return c

    n_far = jnp.maximum(qi - 1, 0)
    lax.fori_loop(0, n_far // FAR_TILES, slc_far_body(FAR_TILES * tq), 0)
    lax.fori_loop((n_far // FAR_TILES) * FAR_TILES, n_far, slc_far_body(tq), 0)
    lax.fori_loop(n_far, qi + 1, slc_near_body, 0)
    o_slc = acc_ref[...] / l_ref[...]

    _reset_flash(m_ref, l_ref, acc_ref)
    nw = WINDOW // tq

    def win_body(kt, c):
        rows = pl.ds(pl.multiple_of(kt * tq, tq), tq)
        t = qi - kt
        typ = jnp.where(t == nw, 3, jnp.minimum(t, 2))
        _flash_tile_t(kwb_ref[rows, :], vwt_ref[:, rows], qs_ref, m_ref, l_ref, acc_ref, scale=scale,
                      bias=lambda cols: bias_ref[typ, :, cols])
        return c

    lax.fori_loop(jnp.maximum(qi - nw, 0), qi + 1, win_body, 0)
    o_win = acc_ref[...] / l_ref[...]

    sig = jax.nn.sigmoid(gt_ref[0]).T
    for r in range(HPG):
        cs = slice(r * tq, (r + 1) * tq)
        o_t = (sig[r:r + 1, :] * o_cmp[:, cs] + sig[HPG + r:HPG + r + 1, :] * o_slc[:, cs]
               + sig[2 * HPG + r:2 * HPG + r + 1, :] * o_win[:, cs])
        o_ref[0, :, r * HEAD_W:(r + 1) * HEAD_W] = o_t.T


def _prompt_cmp_buckets(tq, n_pad):
    jp = np.arange(2 * n_pad)[:, None] - (n_pad - tq // CMP_STRIDE)
    i = np.arange(tq)[None, :]
    return _bucket_np(i - CMP_STRIDE * jp - (2 * CMP_STRIDE - 1))[None].astype(np.int32)


def nsa_prompt(proj, gates, kc, vc, bias, u_bias, tq=PROMPT_TQ):
    b, t, _ = proj.shape
    gw = HPG * HEAD_W
    n_pad = kc.shape[1]
    n_slc = t // SLC_BLOCK
    wimp = jnp.asarray(_importance_matrix(n_pad - 1, n_pad, n_slc, LANE).T, BF16)
    rows = HPG * tq
    seq = lambda n: pl.BlockSpec((1, n, HEAD_W), lambda i, g, j: (i, 0, g))
    kv = lambda a: pl.BlockSpec((1, t, HEAD_W), lambda i, g, j: (i, 0, N_HEADS + a * N_KV + g))
    return pl.pallas_call(
        functools.partial(_nsa_prompt_kernel, tq=tq, n_slc=n_slc),
        grid=(b, N_KV, t // tq),
        in_specs=[
            pl.BlockSpec((1, tq, gw), lambda i, g, j: (i, j, g)),
            pl.BlockSpec((1, tq, LANE), lambda i, g, j: (i, j, g)),
            seq(n_pad), seq(n_pad), kv(2), kv(3), kv(4), kv(5),
            pl.BlockSpec((4, tq, rows), lambda i, g, j: (0, 0, g)),
            pl.BlockSpec((1, 2 * n_pad, rows), lambda i, g, j: (0, 0, g)),
            pl.BlockSpec(wimp.shape, lambda i, g, j: (0, 0)),
        ],
        out_specs=pl.BlockSpec((1, tq, gw), lambda i, g, j: (i, j, g)),
        out_shape=jax.ShapeDtypeStruct((b, t, N_HEADS * HEAD_W), F32),
        scratch_shapes=[
            pltpu.VMEM((rows, HEAD_W), BF16),
            pltpu.VMEM((t, HEAD_W), BF16),
            pltpu.VMEM((HEAD_W, t), BF16),
            pltpu.VMEM((t, HEAD_W), BF16),
            pltpu.VMEM((HEAD_W, t), BF16),
            pltpu.VMEM((LANE, tq), BF16),
            pltpu.VMEM((1, rows), F32),
            pltpu.VMEM((1, rows), F32),
            pltpu.VMEM((HEAD_W, rows), F32),
        ],
        compiler_params=_params(("parallel", "parallel", "arbitrary")),
        name="nsa_prompt",
    )(proj, gates, kc, vc, proj, proj, proj, proj, bias, u_bias, wimp)


def _decode_cmp_buckets(t_new, past_len):
    n_pad = past_len // CMP_STRIDE
    i = np.arange(t_new)[:, None]
    j = n_pad - LANE + np.arange(LANE)[None, :]
    last = _bucket_np(past_len + i - CMP_STRIDE * j - (2 * CMP_STRIDE - 1))
    last = np.where(j < n_pad - 1, last, -1)
    return np.stack([np.full((t_new, LANE), N_BUCKETS - 1, np.int32), last]).astype(np.int32)


def _decode_win_buckets(t_new, n_tiles):
    i = np.arange(t_new)[:, None]
    idx = np.arange(n_tiles * LANE)[None, :]
    dw = WINDOW + i - idx
    ok = (dw >= 0) & (dw < WINDOW) & (idx < WINDOW + t_new)
    b = np.where(ok, _bucket_np(dw), -1)
    return np.stack([b[:, k * LANE:(k + 1) * LANE] for k in range(n_tiles)]).astype(np.int32)


def _nsa_decode_kernel(pt_ref, q_ref, gt_ref, kc_ref, vc_ref, kn_ref, vn_ref, kw_ref, vw_ref, *rest,
                       n_pages, past_len):
    del pt_ref
    kp = rest[:PAGES_PER_STEP]
    vp = rest[PAGES_PER_STEP:2 * PAGES_PER_STEP]
    (bias_ref, bias_c_ref, bias_w_ref, wimp_ref, o_ref,
     qr_ref, selt_ref, ocmp_ref, owin_ref, m_ref, l_ref, acc_ref) = rest[2 * PAGES_PER_STEP:]
    s = pl.program_id(1)
    t_new = q_ref.shape[1]
    rows = HPG * t_new
    n_slc = -(-(past_len + t_new) // SLC_BLOCK)

    @pl.when(s == 0)
    def _():
        _reset_flash(m_ref, l_ref, acc_ref)
        qr_ref[...] = _query_rows(q_ref, 1, NSA_SCALE)
        n_pad = kc_ref.shape[1]
        sel_t = jnp.zeros(selt_ref.shape, F32)
        for g in range(N_KV):
            sl = slice(g * HEAD_W, (g + 1) * HEAD_W)
            gr = slice(g * rows, (g + 1) * rows)
            qx = _stack_group_heads(q_ref, g).astype(BF16)
            bias_c = jnp.concatenate([bias_c_ref[0, gr, :]] * (n_pad // LANE - 1) + [bias_c_ref[1, gr, :]], axis=1)
            sc = _dot_t(qx, kc_ref[0, :, sl].astype(BF16)) * NSA_SCALE + bias_c
            pc = _masked_softmax(sc, bias_c > 0.5 * NEG_INF)
            ocmp_ref[g] = _dot(pc.astype(BF16), vc_ref[0, :, sl].astype(BF16))
            psum = sum(pc[r * t_new:(r + 1) * t_new] for r in range(1, HPG)) + pc[0:t_new]
            s_slc = _dot_exact_rhs(psum, wimp_ref[...])
            sel = _select_blocks(s_slc, past_len, n_slc)
            sel_pad = jnp.concatenate([sel, jnp.zeros((LANE - t_new, sel.shape[1]), F32)], axis=0).T
            tok = lax.broadcasted_iota(jnp.int32, (LANE, LANE), 0)
            lane = lax.broadcasted_iota(jnp.int32, (LANE, LANE), 1)
            spread = jnp.where((lane % t_new == tok) & (lane // rows == g), 1.0, 0.0).astype(BF16)
            sel_t = sel_t + _dot(sel_pad.astype(BF16), spread)
            n_wt = bias_w_ref.shape[0]
            bias_w = jnp.concatenate([bias_w_ref[k, gr, :] for k in range(n_wt)], axis=1)
            sw = _dot_t(qx, kw_ref[0, :, sl].astype(BF16)) * NSA_SCALE + bias_w
            pw = _masked_softmax(sw, bias_w > 0.5 * NEG_INF)
            owin_ref[g] = _dot(pw.astype(BF16), vw_ref[0, :, sl].astype(BF16))
        selt_ref[...] = sel_t
        _decode_step_t(kn_ref[0], vn_ref[0], qr_ref, bias_ref[2], m_ref, l_ref, acc_ref)

    block_rows = SLC_BLOCK * N_KV
    first_block = s * (PAGES_PER_STEP * PAGE // SLC_BLOCK)
    keep = jnp.concatenate(
        [jnp.broadcast_to(selt_ref[pl.ds(first_block + b, 1), :], (block_rows, selt_ref.shape[1]))
         for b in range(PAGES_PER_STEP * PAGE // SLC_BLOCK)], axis=0)
    _decode_step_t(_page_rows(kp), _page_rows(vp), qr_ref,
                   _page_bias_t(bias_ref, s * PAGES_PER_STEP, n_pages, 1), m_ref, l_ref, acc_ref, keep=keep)

    @pl.when(s == pl.num_programs(1) - 1)
    def _():
        sig = jax.nn.sigmoid(gt_ref[0])
        o_slc_all = (acc_ref[...] / l_ref[...]).T
        for g in range(N_KV):
            o_cmp = ocmp_ref[g]
            o_win = owin_ref[g]
            for r in range(HPG):
                h = g * HPG + r
                rs = slice(r * t_new, (r + 1) * t_new)
                c0 = g * LANE + r
                o_ref[0, :, h * HEAD_W:(h + 1) * HEAD_W] = (
                    sig[:, c0:c0 + 1] * o_cmp[rs]
                    + sig[:, c0 + HPG:c0 + HPG + 1] * o_slc_all[h * t_new:(h + 1) * t_new]
                    + sig[:, c0 + 2 * HPG:c0 + 2 * HPG + 1] * o_win[rs])


def nsa_decode(q, gates, kc, vc, k_new, v_new, kw_src, vw_src, cache_k, cache_v, layer, page_table,
               bias, bias_c, bias_w, past_len):
    b, t_new, _ = q.shape
    n_pages = page_table.shape[1]
    rows = HPG * t_new
    lanes = N_HEADS * t_new
    assert lanes == LANE
    n_pad = kc.shape[1]
    n_slc = -(-(past_len + t_new) // SLC_BLOCK)
    n_cols = -(-n_slc // LANE) * LANE
    wimp = jnp.asarray(_importance_matrix(n_pad - 1, n_pad, n_slc, n_cols), BF16)
    fixed = lambda *shape: pl.BlockSpec(shape, lambda i, s, pt: (0,) * len(shape))
    per_b = lambda *shape: pl.BlockSpec((1,) + shape, lambda i, s, pt: (i,) + (0,) * len(shape))
    grid_spec = pltpu.PrefetchScalarGridSpec(
        num_scalar_prefetch=1,
        grid=(b, n_pages // PAGES_PER_STEP),
        in_specs=[per_b(t_new, N_HEADS * HEAD_W), per_b(t_new, N_KV * LANE),
                  per_b(n_pad, NSA_KV), per_b(n_pad, NSA_KV), per_b(PAGE_ROWS, HEAD_W), per_b(PAGE_ROWS, HEAD_W),
                  per_b(kw_src.shape[1], NSA_KV), per_b(kw_src.shape[1], NSA_KV)]
        + _page_specs(layer, 2)
        + [fixed(*bias.shape), fixed(*bias_c.shape), fixed(*bias_w.shape), fixed(*wimp.shape)],
        out_specs=per_b(t_new, N_HEADS * HEAD_W),
        scratch_shapes=[
            pltpu.VMEM((lanes, HEAD_W), BF16),
            pltpu.VMEM((n_cols, lanes), F32),
            pltpu.VMEM((N_KV, rows, HEAD_W), F32),
            pltpu.VMEM((N_KV, rows, HEAD_W), F32),
            pltpu.VMEM((1, lanes), F32),
            pltpu.VMEM((1, lanes), F32),
            pltpu.VMEM((HEAD_W, lanes), F32),
        ],
    )
    return pl.pallas_call(
        functools.partial(_nsa_decode_kernel, n_pages=n_pages, past_len=past_len),
        grid_spec=grid_spec,
        out_shape=jax.ShapeDtypeStruct(q.shape[:2] + (N_HEADS * HEAD_W,), F32),
        compiler_params=_params(("parallel", "arbitrary")),
        name="nsa_decode",
    )(page_table, q, gates, kc, vc, k_new, v_new, kw_src, vw_src,
      *([cache_k] * PAGES_PER_STEP), *([cache_v] * PAGES_PER_STEP), bias, bias_c, bias_w, wimp)


def _gate_weights(w_g):
    d = w_g.shape[0]
    w = jnp.transpose(w_g.reshape(d, 3, N_KV, HPG), (0, 2, 1, 3)).reshape(d, N_KV, 3 * HPG)
    return jnp.pad(w, ((0, 0), (0, 0), (0, LANE - 3 * HPG))).reshape(d, N_KV * LANE)


def nsa_layer(xp, xs, gain, w_in, q_norm, k_norm, pe, w1, w2, w_out, caches, layer, page_table,
              bias_p, u_bias, bias_s, bias_sc, bias_sw, past_len):
    cmp_k, cmp_v, slc_k, slc_v, win_k, win_v = caches
    nqkv = N_HEADS * HEAD_W + 6 * NSA_KV
    w_g = _gate_weights(w_in[layer, :, nqkv:])[None]
    w_in_t = jnp.swapaxes(w_in, 1, 2)
    nq = N_HEADS * HEAD_W
    tile = 4 * LANE
    q_gain = jnp.tile(q_norm, tile // HEAD_W)
    head_norm = (HEAD_W, {**{c: q_gain for c in range(nq // tile)},
                          nq // tile + 2: jnp.tile(k_norm[1], tile // HEAD_W),
                          nq // tile + 4: jnp.tile(k_norm[2], tile // HEAD_W)})
    outs = []
    for x, paged in ((xp, False), (xs, True)):
        b, t, d = x.shape
        x2 = x.reshape(b * t, d)
        proj = dense(x2, w_in_t, layer, gain=gain, n=nqkv, w_t=True, head_norm=head_norm).reshape(b, t, -1)
        gates = dense(x2, w_g, 0, gain=gain).reshape(b, t, -1)
        kcr, vcr, ks, vs, kw, vw = (proj[:, :, nq + a * NSA_KV:nq + (a + 1) * NSA_KV] for a in range(6))
        q = proj
        if paged:
            kc = compress(_as_page_rows(cmp_k), layer, page_table, pe[0], w1[0], w2[0], k_norm[0])
            vc = compress(_as_page_rows(cmp_v), layer, page_table, pe[1], w1[1], w2[1], None)
            n_wt = bias_sw.shape[0]
            kw_src = jnp.concatenate([win_k[layer], kw], axis=1)
            vw_src = jnp.concatenate([win_v[layer], vw], axis=1)
            o = nsa_decode(q, gates, kc, vc, _new_page_rows(ks), _new_page_rows(vs),
                           _pad_rows(kw_src, n_wt * LANE), _pad_rows(vw_src, n_wt * LANE),
                           _as_page_rows(slc_k), _as_page_rows(slc_v), layer, page_table,
                           bias_s, bias_sc, bias_sw, past_len)
            kw_out, vw_out = kw_src[:, -WINDOW:], vw_src[:, -WINDOW:]
        else:
            n_pg = t // PAGE
            ident = jnp.arange(b * n_pg, dtype=jnp.int32).reshape(b, n_pg)
            as_pages = lambda a: a.reshape(1, b * n_pg, PAGE_ROWS, HEAD_W)
            kc = compress(as_pages(kcr), 0, ident, pe[0], w1[0], w2[0], k_norm[0])
            vc = compress(as_pages(vcr), 0, ident, pe[1], w1[1], w2[1], None)
            o = nsa_prompt(proj, gates, kc, vc, bias_p, u_bias)
            kw_out, vw_out = kw[:, -WINDOW:], vw[:, -WINDOW:]
        y = dense(o.reshape(b * t, -1), w_out, layer, res=x2).reshape(b, t, d)
        st = tuple(a.reshape(b, -1, N_KV, HEAD_W) for a in (kcr, vcr, ks, vs, kw_out, vw_out))
        outs.append((y, st))
    (yp, stp), (ys, sts) = outs
    return yp, ys, stp, sts


def kernel(x_prompt, x_sample, cache_diff_k, cache_diff_v, state_ssm, state_conv, cache_nsa_cmp_k, cache_nsa_cmp_v, cache_nsa_slc_k, cache_nsa_slc_v, cache_nsa_win_k, cache_nsa_win_v, cache_mem_k, cache_mem_v, page_table, mem_prompt, rel_bias_table, norm_mix, norm_xattn, norm_mem, norm_ffn, diff_w_in, diff_q_norm, diff_k_norm, diff_lambda, diff_sub_norm, diff_w_out, ssm_w_in, ssm_conv_w, ssm_conv_b, ssm_dt_bias, ssm_a_log, ssm_d, ssm_norm, ssm_w_out, nsa_w_in, nsa_q_norm, nsa_k_norm, nsa_cmp_pe, nsa_cmp_w1, nsa_cmp_w2, nsa_w_out, xattn_w_q, xattn_w_k, xattn_w_v, xattn_q_norm, xattn_k_norm, xattn_w_o, ffn_w1, ffn_w3, ffn_w2):
    xp, xs = x_prompt, x_sample
    bp, t, d = xp.shape
    bs, t_new, _ = xs.shape
    past_len = page_table.shape[1] * PAGE
    depth = norm_mix.shape[0]

    bias_p = bias_tiles(_prompt_attn_buckets(PROMPT_TQ), rel_bias_table, heads_on_lanes=True, scale=LOG2E)
    u_bias = bias_tiles(_prompt_cmp_buckets(PROMPT_TQ, t // CMP_STRIDE), rel_bias_table, heads_on_lanes=True,
                        scale=LOG2E)
    bias_s = bias_tiles_indexed(_decode_page_index(t_new, past_len), rel_bias_table)
    bias_sc = bias_tiles(_decode_cmp_buckets(t_new, past_len), rel_bias_table)
    n_wt = -(-(WINDOW + t_new) // LANE)
    bias_sw = bias_tiles(_decode_win_buckets(t_new, n_wt), rel_bias_table)

    mem_k, mem_v = mem_kv(mem_prompt, norm_mem, xattn_w_k, xattn_w_v, xattn_k_norm)
    xw = X_HEADS * X_DH
    win_k = cache_nsa_win_k.reshape(*cache_nsa_win_k.shape[:3], NSA_KV)
    win_v = cache_nsa_win_v.reshape(*cache_nsa_win_v.shape[:3], NSA_KV)

    dkp, dvp, dks, dvs = [], [], [], []
    ssp, cvp, sss, cvs = [], [], [], []
    nsp, nss = [], []
    for i in range(depth):
        kind, j = i % 3, i // 3
        if kind == 0:
            lam_init = 0.8 - 0.6 * math.exp(-0.3 * i)
            xp, xs, kp_, vp_, ks_, vs_ = diff_layer(
                xp, xs, norm_mix[i], diff_w_in, diff_q_norm[j], diff_k_norm[j], diff_lambda[j],
                diff_sub_norm[j], diff_w_out, cache_diff_k, cache_diff_v, j, page_table,
                bias_p, bias_s, lam_init)
            dkp.append(kp_)
            dvp.append(vp_)
            dks.append(ks_)
            dvs.append(vs_)
        elif kind == 1:
            xp, xs, hp_, cp_, hs_, cs_ = ssd_layer(
                xp, xs, norm_mix[i], ssm_w_in, j, ssm_conv_w[j], ssm_conv_b[j], ssm_dt_bias[j], ssm_a_log[j],
                ssm_d[j], ssm_norm[j], ssm_w_out, state_conv[j], state_ssm[j])
            ssp.append(hp_)
            cvp.append(cp_)
            sss.append(hs_)
            cvs.append(cs_)
        else:
            caches = (cache_nsa_cmp_k, cache_nsa_cmp_v, cache_nsa_slc_k, cache_nsa_slc_v, win_k, win_v)
            xp, xs, stp, sts = nsa_layer(
                xp, xs, norm_mix[i], nsa_w_in, nsa_q_norm[j], nsa_k_norm[j], nsa_cmp_pe[j], nsa_cmp_w1[j],
                nsa_cmp_w2[j], nsa_w_out, caches, j, page_table,
                bias_p, u_bias, bias_s, bias_sc, bias_sw, past_len)
            nsp.append(stp)
            nss.append(sts)
        xp = xattn(xp, norm_xattn[i], xattn_w_q, xattn_q_norm[i], mem_k[i], mem_v[i], xattn_w_o, i)
        xs = xattn(xs, norm_xattn[i], xattn_w_q, xattn_q_norm[i], cache_mem_k[i].reshape(bs, N_MEM, xw),
                   cache_mem_v[i].reshape(bs, N_MEM, xw), xattn_w_o, i)
        xp = ffn(xp.reshape(bp * t, d), norm_ffn[i], ffn_w1, ffn_w3, ffn_w2, i).reshape(bp, t, d)
        xs = ffn(xs.reshape(bs * t_new, d), norm_ffn[i], ffn_w1, ffn_w3, ffn_w2, i).reshape(bs, t_new, d)

    st = lambda xs_: jnp.stack(xs_, axis=0)
    nsp_t = [st([s[a] for s in nsp]) for a in range(6)]
    nss_t = [st([s[a] for s in nss]) for a in range(6)]
    mem_shape = (depth, bp, N_MEM, X_HEADS, X_DH)
    return (xp, xs,
            st(dkp), st(dvp), st(dks), st(dvs),
            st(ssp), st(cvp), st(sss), st(cvs),
            *nsp_t, *nss_t,
            mem_k.reshape(mem_shape), mem_v.reshape(mem_shape))
```

```python
import functools
import math

import numpy as np
import jax
import jax.numpy as jnp
from jax import lax
from jax.experimental import pallas as pl
from jax.experimental.pallas import tpu as pltpu

F32 = jnp.float32
BF16 = jnp.bfloat16

D_MODEL = 2048
DEPTH = 4
PAGE = 128
N_HEADS = 16
N_KV = 4
HPG = N_HEADS // N_KV
DIFF_DH = 64
HEAD_W = 128
N_BUCKETS = 32
MAX_EXACT = 16
MAX_DIST = 128
D_INNER = 2 * D_MODEL
SSM_HEADDIM = 64
SSM_HEADS = D_INNER // SSM_HEADDIM
SSM_GROUPS = 8
SSM_HPG = SSM_HEADS // SSM_GROUPS
D_STATE = 128
CONV_W = 4
CONV_DIM = D_INNER + 2 * SSM_GROUPS * D_STATE
SSM_CHUNK = 128
CMP_STRIDE = 16
SLC_BLOCK = 64
SLC_RATIO = SLC_BLOCK // CMP_STRIDE
N_SELECT = 16
WINDOW = 512
N_MEM = 256
X_HEADS = 4
X_DH = 128
EPS = 1e-6
NEG_INF = -1e30
FORCE_SCORE = 1e4

LANE = 128
VMEM_LIMIT = 56 * 1024 * 1024
DENSE_TALL_ROWS = 2048


def _params(sem):
    return pltpu.CompilerParams(dimension_semantics=sem, vmem_limit_bytes=VMEM_LIMIT)


def _rms(x, gain):
    return x * lax.rsqrt(jnp.mean(x * x, axis=-1, keepdims=True) + EPS) * gain


def _dot(a, b):
    return jnp.dot(a, b, preferred_element_type=F32)


def _dot_t(a, b):
    return lax.dot_general(a, b, (((1,), (1,)), ((), ())), preferred_element_type=F32)


def _split3(x):
    hi = x.astype(BF16)
    r1 = x - hi.astype(F32)
    mid = r1.astype(BF16)
    lo = (r1 - mid.astype(F32)).astype(BF16)
    return hi, mid, lo


def _dot_exact_rhs(x, m_bf16):
    hi, mid, lo = _split3(x)
    return _dot(hi, m_bf16) + _dot(mid, m_bf16) + _dot(lo, m_bf16)


def _norm64(blk, gain, lo):
    sq = blk * blk
    s_lo = jnp.sum(jnp.where(lo, sq, 0.0), axis=-1, keepdims=True)
    s_hi = jnp.sum(jnp.where(lo, 0.0, sq), axis=-1, keepdims=True)
    ms = jnp.where(lo, s_lo, s_hi) * (1.0 / DIFF_DH)
    return blk * lax.rsqrt(ms + EPS) * gain


def _head_norm_tile(y, gain, seg):
    lo = lax.broadcasted_iota(jnp.int32, (y.shape[0], LANE), 1) < DIFF_DH
    blocks = []
    for c in range(y.shape[1] // LANE):
        sl = slice(c * LANE, (c + 1) * LANE)
        blocks.append(_norm64(y[:, sl], gain[:, sl], lo) if seg == DIFF_DH else _rms(y[:, sl], gain[:, sl]))
    return jnp.concatenate(blocks, axis=1)


def _dense_kernel(*refs, norm, residual, w_t, head_seg, norm_tiles):
    refs = list(refs)
    x_ref = refs.pop(0)
    g_ref = refs.pop(0) if norm else None
    w_ref = refs.pop(0)
    r_ref = refs.pop(0) if residual else None
    hg_ref = refs.pop(0) if head_seg else None
    o_ref, xb_ref = refs
    j = pl.program_id(1)

    @pl.when(j == 0)
    def _():
        x = x_ref[...]
        if norm:
            x = _rms(x, g_ref[...])
        xb_ref[...] = x.astype(BF16)

    w = w_ref[0].astype(BF16)
    y = _dot_t(xb_ref[...], w) if w_t else _dot(xb_ref[...], w)
    if residual:
        y = y + r_ref[...]
    if not head_seg:
        o_ref[...] = y
        return
    is_norm = functools.reduce(jnp.logical_or, [j == t for t in norm_tiles])

    @pl.when(is_norm)
    def _():
        o_ref[...] = _head_norm_tile(y, hg_ref[0], head_seg)

    @pl.when(jnp.logical_not(is_norm))
    def _():
        o_ref[...] = y


def dense(x, w, layer, gain=None, res=None, n=None, w_t=False, head_norm=None):
    m, k = x.shape
    n = w.shape[1 if w_t else 2] if n is None else n
    tm = m if m <= 512 else (DENSE_TALL_ROWS if k <= D_MODEL and m % DENSE_TALL_ROWS == 0 else 512)
    tn = LANE if n % 256 else (512 if n % 512 == 0 else 256)
    assert m % tm == 0 and n % tn == 0
    norm, residual = gain is not None, res is not None
    x_mode = dict(pipeline_mode=pl.Buffered(1)) if tm == DENSE_TALL_ROWS else {}
    in_specs = [pl.BlockSpec((tm, k), lambda i, j: (i, 0), **x_mode)]
    args = [x]
    if norm:
        in_specs.append(pl.BlockSpec((1, k), lambda i, j: (0, 0)))
        args.append(gain.reshape(1, k))
    if w_t:
        in_specs.append(pl.BlockSpec((1, tn, k), lambda i, j: (layer, j, 0)))
    else:
        in_specs.append(pl.BlockSpec((1, k, tn), lambda i, j: (layer, 0, j)))
    args.append(w)
    if residual:
        in_specs.append(pl.BlockSpec((tm, tn), lambda i, j: (i, j)))
        args.append(res)
    head_seg, norm_tiles = 0, ()
    if head_norm is not None:
        head_seg, tile_gains = head_norm
        norm_tiles = tuple(sorted(tile_gains))
        ones = jnp.ones((tn,), F32)
        gains = jnp.stack([tile_gains.get(t, ones) for t in range(n // tn)]).reshape(n // tn, 1, tn)
        in_specs.append(pl.BlockSpec((1, 1, tn), lambda i, j: (j, 0, 0)))
        args.append(gains)
    return pl.pallas_call(
        functools.partial(_dense_kernel, norm=norm, residual=residual, w_t=w_t, head_seg=head_seg,
                          norm_tiles=norm_tiles),
        grid=(m // tm, n // tn),
        in_specs=in_specs,
        out_specs=pl.BlockSpec((tm, tn), lambda i, j: (i, j)),
        out_shape=jax.ShapeDtypeStruct((m, n), F32),
        scratch_shapes=[pltpu.VMEM((tm, k), BF16)],
        compiler_params=_params(("parallel", "arbitrary")),
        name="dense",
    )(*args)


def _ffn_kernel(x_ref, g_ref, w1_ref, w3_ref, w2_ref, o_ref, xb_ref):
    @pl.when(pl.program_id(1) == 0)
    def _():
        x = x_ref[...]
        xb_ref[...] = _rms(x, g_ref[...]).astype(BF16)
        o_ref[...] = x

    xb = xb_ref[...]
    h1 = _dot(xb, w1_ref[0].astype(BF16))
    h3 = _dot(xb, w3_ref[0].astype(BF16))
    a = (h1 * jax.nn.sigmoid(h1) * h3).astype(BF16)
    o_ref[...] += _dot(a, w2_ref[0].astype(BF16))


def ffn(x, gain, w1, w3, w2, layer):
    m, d = x.shape
    f = w1.shape[2]
    tm = min(m, 1024)
    tf = 256
    assert m % tm == 0 and f % tf == 0
    return pl.pallas_call(
        _ffn_kernel,
        grid=(m // tm, f // tf),
        in_specs=[
            pl.BlockSpec((tm, d), lambda i, j: (i, 0), pipeline_mode=pl.Buffered(1)),
            pl.BlockSpec((1, d), lambda i, j: (0, 0)),
            pl.BlockSpec((1, d, tf), lambda i, j: (layer, 0, j)),
            pl.BlockSpec((1, d, tf), lambda i, j: (layer, 0, j)),
            pl.BlockSpec((1, tf, d), lambda i, j: (layer, j, 0)),
        ],
        out_specs=pl.BlockSpec((tm, d), lambda i, j: (i, 0)),
        out_shape=jax.ShapeDtypeStruct((m, d), F32),
        scratch_shapes=[pltpu.VMEM((tm, d), BF16)],
        compiler_params=_params(("parallel", "arbitrary")),
        name="ffn",
    )(x, gain.reshape(1, d), w1, w3, w2)


def _mem_kv_kernel(mem_ref, g_ref, wk_ref, wv_ref, kn_ref, k_ref, v_ref):
    m = _rms(mem_ref[0], g_ref[0]).astype(BF16)
    k = _dot(m, wk_ref[0].astype(BF16))
    v_ref[0, 0] = _dot(m, wv_ref[0].astype(BF16))
    for h in range(X_HEADS):
        sl = slice(h * X_DH, (h + 1) * X_DH)
        k_ref[0, 0, :, sl] = _rms(k[:, sl], kn_ref[0])


def mem_kv(mem, g_mem, wk, wv, k_norm):
    b = mem.shape[0]
    nl = wk.shape[0]
    hw = X_HEADS * X_DH
    shape = jax.ShapeDtypeStruct((nl, b, N_MEM, hw), F32)
    return pl.pallas_call(
        _mem_kv_kernel,
        grid=(nl, b),
        in_specs=[
            pl.BlockSpec((1, N_MEM, D_MODEL), lambda l, i: (i, 0, 0)),
            pl.BlockSpec((1, 1, D_MODEL), lambda l, i: (l, 0, 0)),
            pl.BlockSpec((1, D_MODEL, hw), lambda l, i: (l, 0, 0)),
            pl.BlockSpec((1, D_MODEL, hw), lambda l, i: (l, 0, 0)),
            pl.BlockSpec((1, 1, X_DH), lambda l, i: (l, 0, 0)),
        ],
        out_specs=[pl.BlockSpec((1, 1, N_MEM, hw), lambda l, i: (l, i, 0, 0))] * 2,
        out_shape=[shape, shape],
        compiler_params=_params(("parallel", "parallel")),
        name="mem_kv",
    )(mem, g_mem.reshape(nl, 1, D_MODEL), wk, wv, k_norm.reshape(nl, 1, X_DH))


def _xattn_kernel(x_ref, g_ref, wq_ref, qn_ref, k_ref, v_ref, wo_ref, o_ref):
    x = x_ref[0]
    q = _dot(_rms(x, g_ref[...]).astype(BF16), wq_ref[0].astype(BF16))
    outs = []
    for h in range(X_HEADS):
        sl = slice(h * X_DH, (h + 1) * X_DH)
        qh = _rms(q[:, sl], qn_ref[...]).astype(BF16)
        s = _dot_t(qh, k_ref[0, :, sl].astype(BF16)) * (X_DH ** -0.5)
        e = jnp.exp(s - jnp.max(s, axis=-1, keepdims=True))
        p = e / jnp.sum(e, axis=-1, keepdims=True)
        outs.append(_dot(p.astype(BF16), v_ref[0, :, sl].astype(BF16)))
    o = jnp.concatenate(outs, axis=1).astype(BF16)
    o_ref[0] = x + _dot(o, wo_ref[0].astype(BF16))


def xattn(x, gain, wq, q_norm, k, v, wo, layer):
    b, t, d = x.shape
    hw = X_HEADS * X_DH
    tm = min(t, 512)
    return pl.pallas_call(
        _xattn_kernel,
        grid=(b, t // tm),
        in_specs=[
            pl.BlockSpec((1, tm, d), lambda i, j: (i, j, 0)),
            pl.BlockSpec((1, d), lambda i, j: (0, 0)),
            pl.BlockSpec((1, d, hw), lambda i, j: (layer, 0, 0)),
            pl.BlockSpec((1, X_DH), lambda i, j: (0, 0)),
            pl.BlockSpec((1, N_MEM, hw), lambda i, j: (i, 0, 0)),
            pl.BlockSpec((1, N_MEM, hw), lambda i, j: (i, 0, 0)),
            pl.BlockSpec((1, hw, d), lambda i, j: (layer, 0, 0)),
        ],
        out_specs=pl.BlockSpec((1, tm, d), lambda i, j: (i, j, 0)),
        out_shape=jax.ShapeDtypeStruct((b, t, d), F32),
        compiler_params=_params(("parallel", "parallel")),
        name="xattn",
    )(x, gain.reshape(1, d), wq, q_norm.reshape(1, X_DH), k, v, wo)


def _bucket_np(dist):
    n = np.maximum(dist, 0)
    nf = np.maximum(n, 1).astype(np.float64)
    large = MAX_EXACT + (np.log(nf / MAX_EXACT) / math.log(MAX_DIST / MAX_EXACT)
                         * (N_BUCKETS - MAX_EXACT)).astype(np.int64)
    b = np.where(n < MAX_EXACT, n, np.minimum(large, N_BUCKETS - 1))
    return np.where(dist < 0, -1, b).astype(np.int32)


def _bias_kernel(bkt_ref, tab_ref, o_ref, *, scale):
    h = pl.program_id(1)
    b = bkt_ref[0]
    acc = jnp.full(b.shape, NEG_INF, F32)
    for k in range(N_BUCKETS):
        acc = jnp.where(b == k, tab_ref[k, h] * scale, acc)
    o_ref[0] = acc


def _bias_packed_kernel(idx_ref, tab_ref, o_ref, *, scale):
    idx = idx_ref[0]
    acc = jnp.full(idx.shape, NEG_INF, F32)
    for k in range(N_BUCKETS):
        for h in range(N_HEADS):
            acc = jnp.where(idx == k * N_HEADS + h, tab_ref[k, h] * scale, acc)
    o_ref[0] = acc


def bias_tiles_indexed(idx, table, scale=1.0):
    nt, r, w = idx.shape
    return pl.pallas_call(
        functools.partial(_bias_packed_kernel, scale=scale),
        grid=(nt,),
        in_specs=[
            pl.BlockSpec((1, r, w), lambda t: (t, 0, 0)),
            pl.BlockSpec(memory_space=pltpu.SMEM),
        ],
        out_specs=pl.BlockSpec((1, r, w), lambda t: (t, 0, 0)),
        out_shape=jax.ShapeDtypeStruct((nt, r, w), F32),
        compiler_params=_params(("parallel",)),
        name="bias_tiles_indexed",
    )(jnp.asarray(idx.astype(np.int32)), table)


def bias_tiles(buckets, table, heads_on_lanes=False, scale=1.0):
    nt, r, w = buckets.shape
    if heads_on_lanes:
        out_spec = pl.BlockSpec((1, r, w), lambda t, h: (t, 0, h))
        out_shape = (nt, r, N_HEADS * w)
    else:
        out_spec = pl.BlockSpec((1, r, w), lambda t, h: (t, h, 0))
        out_shape = (nt, N_HEADS * r, w)
    return pl.pallas_call(
        functools.partial(_bias_kernel, scale=scale),
        grid=(nt, N_HEADS),
        in_specs=[
            pl.BlockSpec((1, r, w), lambda t, h: (t, 0, 0)),
            pl.BlockSpec(memory_space=pltpu.SMEM),
        ],
        out_specs=out_spec,
        out_shape=jax.ShapeDtypeStruct(out_shape, F32),
        compiler_params=_params(("parallel", "parallel")),
        name="bias_tiles",
    )(jnp.asarray(buckets), table)


def _prompt_attn_buckets(tq):
    assert WINDOW % tq == 0 and tq >= MAX_DIST
    j = np.arange(tq)[:, None]
    i = np.arange(tq)[None, :]
    far = np.full((tq, tq), N_BUCKETS - 1, np.int32)
    return np.stack([
        _bucket_np(i - j),
        _bucket_np(i - j + tq),
        far,
        np.where(j > i, far, -1),
    ]).astype(np.int32)


def _diff_lambda(lam_ref, lam_init):
    lf = lam_ref[...]
    s01 = jnp.sum(lf[0:1] * lf[1:2], axis=-1, keepdims=True)
    s23 = jnp.sum(lf[2:3] * lf[3:4], axis=-1, keepdims=True)
    return jnp.exp(s01) - jnp.exp(s23) + lam_init


def _split_maps(qs):
    lo = lax.broadcasted_iota(jnp.int32, qs.shape, 1) < DIFF_DH
    return jnp.concatenate([jnp.where(lo, qs, 0.0), jnp.where(lo, 0.0, qs)], axis=0)


def _diff_finish_rows(o, lam, sub_norm, lam_init):
    r = o.shape[0] // 2
    a = o[:r] - lam * o[r:]
    return _rms(a, sub_norm) * (1.0 - lam_init)


PROMPT_TQ = 256


LOG2E = math.log2(math.e)
FLASH_CHUNKS = 4
FAR_TILES = 2


def _flash_tile_t(k_tile, vt_tile, qs_ref, m_ref, l_ref, acc_ref, scale=None, bias=None, far_bias=None,
                  mask=None, n_chunks=FLASH_CHUNKS):
    chunk = qs_ref.shape[0] // n_chunks
    logits = [_dot_t(k_tile, qs_ref[c * chunk:(c + 1) * chunk, :]) for c in range(n_chunks)]
    for c in range(n_chunks):
        cols = slice(c * chunk, (c + 1) * chunk)
        s = logits[c]
        if scale is not None:
            s = s * scale
        if bias is not None:
            s = s + bias(cols)
        if mask is not None:
            s = jnp.where(mask[:, cols] > 0.5, s, NEG_INF)
        m_old = m_ref[:, cols]
        s_max = jnp.max(s, axis=0, keepdims=True)
        if far_bias is None:
            m_new = jnp.maximum(m_old, s_max)
            p = jnp.exp2(s - m_new)
        else:
            fb = far_bias[:, cols]
            m_new = jnp.maximum(m_old, s_max + fb)
            p = jnp.exp2(s - (m_new - fb))
        alpha = jnp.exp2(m_old - m_new)
        l_ref[:, cols] = alpha * l_ref[:, cols] + jnp.sum(p, axis=0, keepdims=True)
        acc_ref[:, cols] = alpha * acc_ref[:, cols] + _dot(vt_tile, p.astype(BF16))
        m_ref[:, cols] = m_new


def _stage_kv(k_ref, v_ref, kb_ref, vt_ref, chunk):
    for c in range(k_ref.shape[1] // chunk):
        rows = slice(c * chunk, (c + 1) * chunk)
        kb_ref[rows, :] = k_ref[0, rows, :].astype(BF16)
        vt_ref[:, rows] = v_ref[0, rows, :].T.astype(BF16)


def _diff_flash_kernel(lam_ref, q_ref, k_ref, v_ref, bias_ref, sn_ref, o_ref,
                       qs_ref, kb_ref, vt_ref, m_ref, l_ref, acc_ref, *, tq, lam_init):
    qi = pl.program_id(2)

    @pl.when(qi == 0)
    def _():
        _stage_kv(k_ref, v_ref, kb_ref, vt_ref, tq)

    qb = q_ref[0]
    qs = jnp.concatenate([qb[:, r * HEAD_W:(r + 1) * HEAD_W] for r in range(HPG)], axis=0)
    qs_ref[...] = _split_maps(qs * (DIFF_DH ** -0.5 * LOG2E)).astype(BF16)
    _reset_flash(m_ref, l_ref, acc_ref)
    far = bias_ref[2, 0:1, :]
    far = jnp.concatenate([far, far], axis=1)

    def far_body(width):
        def body(kt, c):
            rows = pl.ds(pl.multiple_of(kt * width, width), width)
            _flash_tile_t(kb_ref[rows, :], vt_ref[:, rows], qs_ref, m_ref, l_ref, acc_ref, far_bias=far)
            return c
        return body

    def near_body(kt, c):
        rows = pl.ds(pl.multiple_of(kt * tq, tq), tq)
        b = bias_ref[qi - kt]
        _flash_tile_t(kb_ref[rows, :], vt_ref[:, rows], qs_ref, m_ref, l_ref, acc_ref,
                      bias=lambda cols: jnp.concatenate([b, b], axis=1), n_chunks=1)
        return c

    n_far = jnp.maximum(qi - 1, 0)
    lax.fori_loop(0, n_far // FAR_TILES, far_body(FAR_TILES * tq), 0)
    lax.fori_loop((n_far // FAR_TILES) * FAR_TILES, n_far, far_body(tq), 0)
    lax.fori_loop(n_far, qi + 1, near_body, 0)
    o = (acc_ref[...] / l_ref[...]).T
    a = _diff_finish_rows(o, _diff_lambda(lam_ref, lam_init), sn_ref[...], lam_init)
    for r in range(HPG):
        o_ref[0, :, r * HEAD_W:(r + 1) * HEAD_W] = a[r * tq:(r + 1) * tq]


def diff_flash(qkv, bias, lam_p, sub_norm, lam_init, tq=PROMPT_TQ):
    b, t, _ = qkv.shape
    gw = HPG * HEAD_W
    rows = 2 * HPG * tq
    k0 = N_HEADS
    return pl.pallas_call(
        functools.partial(_diff_flash_kernel, tq=tq, lam_init=lam_init),
        grid=(b, N_KV, t // tq),
        in_specs=[
            pl.BlockSpec((4, DIFF_DH), lambda i, g, j: (0, 0)),
            pl.BlockSpec((1, tq, gw), lambda i, g, j: (i, j, g)),
            pl.BlockSpec((1, t, HEAD_W), lambda i, g, j: (i, 0, k0 + g)),
            pl.BlockSpec((1, t, HEAD_W), lambda i, g, j: (i, 0, k0 + N_KV + g)),
            pl.BlockSpec((4, tq, HPG * tq), lambda i, g, j: (0, 0, g)),
            pl.BlockSpec((1, HEAD_W), lambda i, g, j: (0, 0)),
        ],
        out_specs=pl.BlockSpec((1, tq, gw), lambda i, g, j: (i, j, g)),
        out_shape=jax.ShapeDtypeStruct((b, t, N_HEADS * HEAD_W), F32),
        scratch_shapes=[
            pltpu.VMEM((rows, HEAD_W), BF16),
            pltpu.VMEM((t, HEAD_W), BF16),
            pltpu.VMEM((HEAD_W, t), BF16),
            pltpu.VMEM((1, rows), F32),
            pltpu.VMEM((1, rows), F32),
            pltpu.VMEM((HEAD_W, rows), F32),
        ],
        compiler_params=_params(("parallel", "parallel", "arbitrary")),
        name="diff_flash",
    )(lam_p, qkv, qkv, qkv, bias, sub_norm.reshape(1, HEAD_W))


PAGES_PER_STEP = 16


def _decode_page_index(t_new, past_len):
    j = np.arange(PAGE)[:, None]
    i = np.arange(t_new)[None, :]
    far = np.full((PAGE, t_new), N_BUCKETS - 1, np.int32)
    new = np.where(j < t_new, _bucket_np(i - j), -1)
    g = np.arange(N_KV)[None, :, None, None]
    h = np.arange(N_HEADS)[None, None, :, None]
    tiles = []
    for bkt in (far, _bucket_np(PAGE + i - j), new):
        b = bkt[:, None, None, :]
        idx = np.where((b >= 0) & (h // HPG == g), b * N_HEADS + h, -1)
        tiles.append(idx.reshape(PAGE * N_KV, N_HEADS * t_new))
    return np.stack(tiles).astype(np.int32)


def _stack_group_heads(q_ref, g):
    return jnp.concatenate(
        [q_ref[0, :, (g * HPG + r) * HEAD_W:(g * HPG + r + 1) * HEAD_W] for r in range(HPG)], axis=0)


def _query_rows(q_ref, maps, scale):
    pieces = []
    for m in range(maps):
        for h in range(N_HEADS):
            qh = q_ref[0, :, h * HEAD_W:(h + 1) * HEAD_W] * scale
            if maps == 2:
                lo = lax.broadcasted_iota(jnp.int32, qh.shape, 1) < DIFF_DH
                qh = jnp.where(lo, qh, 0.0) if m == 0 else jnp.where(lo, 0.0, qh)
            pieces.append(qh)
    return jnp.concatenate(pieces, axis=0).astype(BF16)


def _page_bias_t(bias_ref, first_page, n_pages, reps):
    tiles = [bias_ref[jnp.where(first_page + u == n_pages - 1, 1, 0)] for u in range(PAGES_PER_STEP)]
    b = jnp.concatenate(tiles, axis=0)
    return jnp.concatenate([b] * reps, axis=1)


def _decode_step_t(k_rows, v_rows, q_ref, bias, m_ref, l_ref, acc_ref, keep=None):
    s = _dot_t(k_rows.astype(BF16), q_ref[...]) + bias
    if keep is not None:
        s = jnp.where(keep > 0.5, s, NEG_INF)
    m_old = m_ref[...]
    m_new = jnp.maximum(m_old, jnp.max(s, axis=0, keepdims=True))
    alpha = jnp.exp(m_old - m_new)
    p = jnp.exp(s - m_new)
    l_ref[...] = alpha * l_ref[...] + jnp.sum(p, axis=0, keepdims=True)
    acc_ref[...] = alpha * acc_ref[...] + _dot(v_rows.T.astype(BF16), p.astype(BF16))
    m_ref[...] = m_new


def _page_rows(refs):
    return jnp.concatenate([r[0, 0] for r in refs], axis=0)


def _diff_decode_kernel(pt_ref, lam_ref, q_ref, kn_ref, vn_ref, *rest, n_pages, lam_init):
    del pt_ref
    kp = rest[:PAGES_PER_STEP]
    vp = rest[PAGES_PER_STEP:2 * PAGES_PER_STEP]
    bias_ref, sn_ref, o_ref, qr_ref, m_ref, l_ref, acc_ref = rest[2 * PAGES_PER_STEP:]
    s = pl.program_id(1)
    t_new = q_ref.shape[1]

    @pl.when(s == 0)
    def _():
        _reset_flash(m_ref, l_ref, acc_ref)
        qr_ref[...] = _query_rows(q_ref, 2, DIFF_DH ** -0.5)
        b = bias_ref[2]
        _decode_step_t(kn_ref[0], vn_ref[0], qr_ref, jnp.concatenate([b, b], axis=1), m_ref, l_ref, acc_ref)

    _decode_step_t(_page_rows(kp), _page_rows(vp), qr_ref,
                   _page_bias_t(bias_ref, s * PAGES_PER_STEP, n_pages, 2), m_ref, l_ref, acc_ref)

    @pl.when(s == pl.num_programs(1) - 1)
    def _():
        o = (acc_ref[...] / l_ref[...]).T
        a = _diff_finish_rows(o, _diff_lambda(lam_ref, lam_init), sn_ref[...], lam_init)
        for h in range(N_HEADS):
            o_ref[0, :, h * HEAD_W:(h + 1) * HEAD_W] = a[h * t_new:(h + 1) * t_new]


PAGE_ROWS = PAGE * N_KV


def _as_page_rows(cache):
    return cache.reshape(*cache.shape[:2], PAGE_ROWS, HEAD_W)


def _page_specs(layer, n):
    def spec(u):
        return pl.BlockSpec((1, 1, PAGE_ROWS, HEAD_W),
                            lambda i, s, pt: (layer, pt[i, s * PAGES_PER_STEP + u], 0, 0))
    return [spec(u) for u in range(PAGES_PER_STEP)] * n


def diff_decode(q, k_new, v_new, cache_k, cache_v, layer, page_table, bias, lam_p, sub_norm, lam_init):
    b, t_new, _ = q.shape
    n_pages = page_table.shape[1]
    lanes = 2 * N_HEADS * t_new
    assert N_HEADS * t_new == LANE
    fixed = lambda *shape: pl.BlockSpec(shape, lambda i, s, pt: (0,) * len(shape))
    per_b = lambda *shape: pl.BlockSpec((1,) + shape, lambda i, s, pt: (i,) + (0,) * len(shape))
    grid_spec = pltpu.PrefetchScalarGridSpec(
        num_scalar_prefetch=1,
        grid=(b, n_pages // PAGES_PER_STEP),
        in_specs=[fixed(4, DIFF_DH), per_b(t_new, N_HEADS * HEAD_W), per_b(PAGE_ROWS, HEAD_W),
                  per_b(PAGE_ROWS, HEAD_W)]
        + _page_specs(layer, 2)
        + [fixed(*bias.shape), fixed(1, HEAD_W)],
        out_specs=per_b(t_new, N_HEADS * HEAD_W),
        scratch_shapes=[
            pltpu.VMEM((lanes, HEAD_W), BF16),
            pltpu.VMEM((1, lanes), F32),
            pltpu.VMEM((1, lanes), F32),
            pltpu.VMEM((HEAD_W, lanes), F32),
        ],
    )
    return pl.pallas_call(
        functools.partial(_diff_decode_kernel, n_pages=n_pages, lam_init=lam_init),
        grid_spec=grid_spec,
        out_shape=jax.ShapeDtypeStruct(q.shape[:2] + (N_HEADS * HEAD_W,), F32),
        compiler_params=_params(("parallel", "arbitrary")),
        name="diff_decode",
    )(page_table, lam_p, q, k_new, v_new, *([cache_k] * PAGES_PER_STEP), *([cache_v] * PAGES_PER_STEP),
      bias, sub_norm.reshape(1, HEAD_W))


def _pad_rows(a, n):
    return jnp.pad(a, ((0, 0), (0, n - a.shape[1]), (0, 0)))


def _group_rows_kernel(x_ref, o_ref):
    tm = x_ref.shape[0]
    for g in range(N_KV):
        o_ref[pl.ds(g, tm, stride=N_KV), :] = x_ref[:, g * HEAD_W:(g + 1) * HEAD_W]


def group_rows(x, col_block):
    m = x.shape[0]
    tm = min(m, 512)
    assert m % tm == 0
    return pl.pallas_call(
        _group_rows_kernel,
        grid=(m // tm,),
        in_specs=[pl.BlockSpec((tm, N_KV * HEAD_W), lambda i: (i, col_block))],
        out_specs=pl.BlockSpec((tm * N_KV, HEAD_W), lambda i: (i, 0)),
        out_shape=jax.ShapeDtypeStruct((m * N_KV, HEAD_W), F32),
        compiler_params=_params(("parallel",)),
        name="group_rows",
    )(x)


def _new_page_rows(rows, b):
    return _pad_rows(rows.reshape(b, -1, HEAD_W), PAGE_ROWS)


def diff_layer(xp, xs, gain, w_in, q_norm, k_norm, lam_p, sub_norm, w_out, cache_k, cache_v, layer,
               page_table, bias_p, bias_s, lam_init):
    nq, nk = N_HEADS * HEAD_W, N_KV * HEAD_W
    tile = 4 * LANE
    q_gain, k_gain = jnp.tile(q_norm, tile // DIFF_DH), jnp.tile(k_norm, tile // DIFF_DH)
    head_norm = (DIFF_DH, {**{c: q_gain for c in range(nq // tile)}, nq // tile: k_gain})
    outs = []
    for x, paged in ((xp, False), (xs, True)):
        b, t, d = x.shape
        x2 = x.reshape(b * t, d)
        qkv2 = dense(x2, w_in, layer, gain=gain, head_norm=head_norm)
        qkv = qkv2.reshape(b, t, -1)
        k_rows, v_rows = group_rows(qkv2, nq // nk), group_rows(qkv2, nq // nk + 1)
        if paged:
            o = diff_decode(qkv, _new_page_rows(k_rows, b), _new_page_rows(v_rows, b), _as_page_rows(cache_k),
                            _as_page_rows(cache_v), layer, page_table, bias_s, lam_p, sub_norm, lam_init)
        else:
            o = diff_flash(qkv, bias_p, lam_p, sub_norm, lam_init)
        y = dense(o.reshape(b * t, -1), w_out, layer, res=x2).reshape(b, t, d)
        outs.append((y, k_rows.reshape(b, t, N_KV, HEAD_W), v_rows.reshape(b, t, N_KV, HEAD_W)))
    (yp, kp, vp), (ys, ks, vs) = outs
    return yp, ys, kp, vp, ks, vs


SSM_GW = SSM_HPG * SSM_HEADDIM
SSM_BC = 2 * SSM_GROUPS * D_STATE
CONV_PAD = 8


def _conv_silu(buf_ref, w_ref, b_ref, n):
    acc = b_ref[...]
    for k in range(CONV_W):
        acc = acc + buf_ref[pl.ds(CONV_PAD - (CONV_W - 1) + k, n), :] * w_ref[k:k + 1, :]
    return acc * jax.nn.sigmoid(acc)


def _ssd_kernel(z_ref, x_ref, bc_ref, dt_ref, cbx_ref, cbbc_ref, wx_ref, wbc_ref, bx_ref, bbc_ref,
                dtb_ref, alog_ref, dsk_ref, ng_ref, e_ref, tri_ref, h0_ref,
                y_ref, hout_ref, ht_ref, xbuf_ref, bcbuf_ref, xa_ref, bca_ref, *, t_valid):
    c = pl.program_id(1)
    n = x_ref.shape[1]

    @pl.when(c == 0)
    def _():
        for g in range(SSM_GROUPS):
            ht_ref[g] = h0_ref[0, g].T
        xbuf_ref[0:CONV_PAD] = cbx_ref[0]
        bcbuf_ref[0:CONV_PAD] = cbbc_ref[0]

    xbuf_ref[CONV_PAD:CONV_PAD + n] = x_ref[0]
    bcbuf_ref[CONV_PAD:CONV_PAD + n] = bc_ref[0]
    xa_ref[...] = _conv_silu(xbuf_ref, wx_ref, bx_ref, n)
    bca_ref[...] = _conv_silu(bcbuf_ref, wbc_ref, bbc_ref, n)
    xbuf_ref[0:CONV_PAD] = xbuf_ref[n:n + CONV_PAD]
    bcbuf_ref[0:CONV_PAD] = bcbuf_ref[n:n + CONV_PAD]

    row = lax.broadcasted_iota(jnp.int32, (n, LANE), 0) + c * n
    dtr = dt_ref[0] + dtb_ref[...]
    dt = jnp.maximum(dtr, 0.0) + jnp.log1p(jnp.exp(-jnp.abs(dtr)))
    dt = jnp.where(row < t_valid, dt, 0.0)
    dta = dt * (-jnp.exp(alog_ref[...]))
    hi, mid, lo = _split3(dta)
    tri = tri_ref[...]
    cs = _dot(tri, hi) + _dot(tri, mid) + _dot(tri, lo)
    cs_last = cs[n - 1:n, :]
    cs_t = cs.T
    dt_t = dt.T
    stacked = jnp.concatenate(
        [jnp.exp(cs), jnp.exp(cs_last - cs) * dt, jnp.broadcast_to(jnp.exp(cs_last), (8, LANE))], axis=0)
    ex = _dot_exact_rhs(stacked, e_ref[...])
    causal = (lax.broadcasted_iota(jnp.int32, (n, n), 0) >= lax.broadcasted_iota(jnp.int32, (n, n), 1))

    for g in range(SSM_GROUPS):
        gs = slice(g * SSM_GW, (g + 1) * SSM_GW)
        bm = bca_ref[:, g * D_STATE:(g + 1) * D_STATE]
        cm = bca_ref[:, (SSM_GROUPS + g) * D_STATE:(SSM_GROUPS + g + 1) * D_STATE].astype(BF16)
        cb = _dot_t(cm, bm.astype(BF16))
        xg = xa_ref[:, gs]
        ys = []
        for r in range(SSM_HPG):
            h = g * SSM_HPG + r
            seg = cs[:, h:h + 1] - cs_t[h:h + 1, :]
            dec = jnp.where(causal, jnp.exp(jnp.where(causal, seg, 0.0)), 0.0)
            mm = (cb * dec * dt_t[h:h + 1, :]).astype(BF16)
            ys.append(_dot(mm, xg[:, r * SSM_HEADDIM:(r + 1) * SSM_HEADDIM].astype(BF16)))
        ht = ht_ref[g]
        y = jnp.concatenate(ys, axis=1) + _dot(cm, ht.astype(BF16)) * ex[0:n, gs]
        y = y + dsk_ref[:, gs] * xg
        zg = z_ref[0, :, gs]
        y = y * (zg * jax.nn.sigmoid(zg))
        y_ref[0, :, gs] = _rms(y, ng_ref[:, gs])
        xw = (xg * ex[n:2 * n, gs]).astype(BF16)
        ht_ref[g] = ht * ex[2 * n:2 * n + 1, gs] + _dot(bm.T.astype(BF16), xw)

    @pl.when(c == pl.num_programs(1) - 1)
    def _():
        for g in range(SSM_GROUPS):
            hout_ref[0, g] = ht_ref[g].T


def _head_expand_matrix():
    e = np.zeros((LANE, D_INNER), np.float32)
    for h in range(SSM_HEADS):
        e[h, h * SSM_HEADDIM:(h + 1) * SSM_HEADDIM] = 1.0
    return e


def ssd_core(zx, dt_raw, conv_buf, h0, conv_w, conv_b, dt_bias, a_log, d_skip, norm_g, t_valid):
    b, t, _ = zx.shape
    n = SSM_CHUNK
    pad_h = LANE - SSM_HEADS
    cb = jnp.pad(conv_buf, ((0, 0), (CONV_PAD - (CONV_W - 1), 0), (0, 0)))
    tri = jnp.asarray(np.tril(np.ones((n, n), np.float32)), BF16)
    e = jnp.asarray(_head_expand_matrix(), BF16)
    d_exp = jnp.repeat(d_skip, SSM_HEADDIM).reshape(1, D_INNER)
    fixed = lambda *shape: pl.BlockSpec(shape, lambda i, c: (0,) * len(shape))
    per_b = lambda *shape: pl.BlockSpec((1,) + shape, lambda i, c: (i,) + (0,) * len(shape))
    y, h_last = pl.pallas_call(
        functools.partial(_ssd_kernel, t_valid=t_valid),
        grid=(b, t // n),
        in_specs=[
            pl.BlockSpec((1, n, D_INNER), lambda i, c: (i, c, 0)),
            pl.BlockSpec((1, n, D_INNER), lambda i, c: (i, c, 1)),
            pl.BlockSpec((1, n, SSM_BC), lambda i, c: (i, c, 2 * D_INNER // SSM_BC)),
            pl.BlockSpec((1, n, LANE), lambda i, c: (i, c, 0)),
            per_b(CONV_PAD, D_INNER), per_b(CONV_PAD, SSM_BC),
            fixed(CONV_W, D_INNER), fixed(CONV_W, SSM_BC), fixed(1, D_INNER), fixed(1, SSM_BC),
            fixed(1, LANE), fixed(1, LANE), fixed(1, D_INNER), fixed(1, D_INNER),
            fixed(LANE, D_INNER), fixed(n, n),
            per_b(SSM_GROUPS, SSM_GW, D_STATE),
        ],
        out_specs=[
            pl.BlockSpec((1, n, D_INNER), lambda i, c: (i, c, 0)),
            per_b(SSM_GROUPS, SSM_GW, D_STATE),
        ],
        out_shape=[jax.ShapeDtypeStruct((b, t, D_INNER), F32),
                   jax.ShapeDtypeStruct((b, SSM_GROUPS, SSM_GW, D_STATE), F32)],
        scratch_shapes=[
            pltpu.VMEM((SSM_GROUPS, D_STATE, SSM_GW), F32),
            pltpu.VMEM((n + CONV_PAD, D_INNER), F32),
            pltpu.VMEM((n + CONV_PAD, SSM_BC), F32),
            pltpu.VMEM((n, D_INNER), F32),
            pltpu.VMEM((n, SSM_BC), F32),
        ],
        compiler_params=_params(("parallel", "arbitrary")),
        name="ssd_core",
    )(zx, zx, zx, dt_raw, cb[:, :, :D_INNER], cb[:, :, D_INNER:],
      conv_w[:, :D_INNER], conv_w[:, D_INNER:], conv_b[:D_INNER].reshape(1, -1), conv_b[D_INNER:].reshape(1, -1),
      jnp.pad(dt_bias, (0, pad_h)).reshape(1, LANE), jnp.pad(a_log, (0, pad_h)).reshape(1, LANE),
      d_exp, norm_g.reshape(1, D_INNER), e, tri,
      h0.reshape(b, SSM_GROUPS, SSM_GW, D_STATE))
    return y, h_last.reshape(b, SSM_HEADS, SSM_HEADDIM, D_STATE)


def ssd_layer(xp, xs, gain, w_in, layer, conv_w, conv_b, dt_bias, a_log, d_skip, norm_g, w_out,
              state_conv, state_ssm):
    nzx = D_INNER + CONV_DIM
    w_dt = jnp.pad(w_in[layer, :, nzx:], ((0, 0), (0, LANE - SSM_HEADS)))[None]
    w_in_t = jnp.swapaxes(w_in, 1, 2)
    outs = []
    for x, conv_buf, h0 in ((xp, None, None), (xs, state_conv, state_ssm)):
        b, t, d = x.shape
        x2 = x.reshape(b * t, d)
        if conv_buf is None:
            conv_buf = jnp.zeros((b, CONV_W - 1, CONV_DIM), F32)
            h0 = jnp.zeros((b, SSM_HEADS, SSM_HEADDIM, D_STATE), F32)
        zx = dense(x2, w_in_t, layer, gain=gain, n=nzx, w_t=True).reshape(b, t, -1)
        dt_raw = dense(x2, w_dt, 0, gain=gain).reshape(b, t, LANE)
        tp = -(-t // SSM_CHUNK) * SSM_CHUNK
        y, h_last = ssd_core(_pad_rows(zx, tp), _pad_rows(dt_raw, tp), conv_buf, h0, conv_w, conv_b,
                             dt_bias, a_log, d_skip, norm_g, t)
        y = y[:, :t].reshape(b * t, D_INNER)
        tail = min(t, CONV_W - 1)
        conv_out = jnp.concatenate([conv_buf[:, tail:], zx[:, t - tail:, D_INNER:]], axis=1)
        outs.append((dense(y, w_out, layer, res=x2).reshape(b, t, d), h_last, conv_out))
    (yp, hp, cp), (ys, hs, cs) = outs
    return yp, ys, hp, cp, hs, cs


NSA_KV = N_KV * HEAD_W
NSA_SCALE = HEAD_W ** -0.5
CMP_PAIRS = CMP_STRIDE // 2
CMP_PAGES = 32


def _cmp_uv_kernel(pt_ref, *refs):
    del pt_ref
    pages = refs[:-2]
    wab_ref, o_ref = refs[-2:]
    subs = PAGE // CMP_STRIDE
    for g in range(N_KV):
        acc = jnp.zeros((len(pages) * subs, 2 * HEAD_W), F32)
        for lp in range(CMP_PAIRS):
            halves = []
            for li in range(2):
                rows = pl.ds((2 * lp + li) * N_KV + g, subs, stride=CMP_STRIDE * N_KV)
                halves.append(jnp.concatenate([p[0, 0, rows, :] for p in pages], axis=0))
            acc = acc + _dot(jnp.concatenate(halves, axis=1).astype(BF16), wab_ref[lp])
        o_ref[0, :, g * 2 * HEAD_W:(g + 1) * 2 * HEAD_W] = acc


def cmp_uv(rows, layer, page_table, wab):
    b, n_pages = page_table.shape
    subs = PAGE // CMP_STRIDE
    step_pages = min(CMP_PAGES, n_pages)
    assert n_pages % step_pages == 0

    def spec(u):
        return pl.BlockSpec((1, 1, PAGE_ROWS, HEAD_W),
                            lambda i, c, pt: (layer, pt[i, c * step_pages + u], 0, 0))

    grid_spec = pltpu.PrefetchScalarGridSpec(
        num_scalar_prefetch=1,
        grid=(b, n_pages // step_pages),
        in_specs=[spec(u) for u in range(step_pages)]
        + [pl.BlockSpec(wab.shape, lambda i, c, pt: (0, 0, 0))],
        out_specs=pl.BlockSpec((1, step_pages * subs, 2 * NSA_KV), lambda i, c, pt: (i, c, 0)),
    )
    return pl.pallas_call(
        _cmp_uv_kernel,
        grid_spec=grid_spec,
        out_shape=jax.ShapeDtypeStruct((b, n_pages * subs, 2 * NSA_KV), F32),
        compiler_params=_params(("parallel", "arbitrary")),
        name="cmp_uv",
    )(page_table, *([rows] * step_pages), wab)


def _cmp_finish_kernel(uv_ref, pe_ref, w1_ref, w2_ref, kn_ref, o_ref, *, norm):
    n = uv_ref.shape[1]
    c = _dot(jnp.broadcast_to(pe_ref[...], (8, pe_ref.shape[1])).astype(BF16), w1_ref[...])[0:1]
    for g in range(N_KV):
        u = uv_ref[0, :, g * 2 * HEAD_W:g * 2 * HEAD_W + HEAD_W]
        v = uv_ref[0, :, g * 2 * HEAD_W + HEAD_W:(g + 1) * 2 * HEAD_W]
        pre = u + pltpu.roll(v, n - 1, axis=0) + c
        out = _dot((pre * jax.nn.sigmoid(pre)).astype(BF16), w2_ref[...])
        if norm:
            out = _rms(out, kn_ref[...])
        o_ref[0, :, g * HEAD_W:(g + 1) * HEAD_W] = out


def cmp_finish(uv, pe, w1, w2, k_norm):
    b, n, _ = uv.shape
    norm = k_norm is not None
    kn = (k_norm if norm else jnp.ones((HEAD_W,), F32)).reshape(1, HEAD_W)
    return pl.pallas_call(
        functools.partial(_cmp_finish_kernel, norm=norm),
        grid=(b,),
        in_specs=[
            pl.BlockSpec((1, n, 2 * NSA_KV), lambda i: (i, 0, 0)),
            pl.BlockSpec((1, pe.size), lambda i: (0, 0)),
            pl.BlockSpec(w1.shape, lambda i: (0, 0)),
            pl.BlockSpec(w2.shape, lambda i: (0, 0)),
            pl.BlockSpec((1, HEAD_W), lambda i: (0, 0)),
        ],
        out_specs=pl.BlockSpec((1, n, NSA_KV), lambda i: (i, 0, 0)),
        out_shape=jax.ShapeDtypeStruct((b, n, NSA_KV), F32),
        compiler_params=_params(("parallel",)),
        name="cmp_finish",
    )(uv, pe.reshape(1, -1), w1, w2, kn)


def compress(rows, layer, page_table, pe, w1, w2, k_norm):
    w1r = w1.reshape(2, CMP_STRIDE, HEAD_W, HEAD_W)
    wab = jnp.transpose(w1r, (1, 2, 0, 3)).reshape(CMP_PAIRS, 2 * HEAD_W, 2 * HEAD_W).astype(BF16)
    uv = cmp_uv(rows, layer, page_table, wab)
    return cmp_finish(uv, pe, w1.astype(BF16), w2.astype(BF16), k_norm)


def _importance_matrix(n_cmp, n_rows, n_slc, n_cols):
    w = np.zeros((n_rows, n_cols), np.float32)
    w_imp = [1.0] + [2.0] * (SLC_RATIO - 1) + [1.0]
    for s in range(n_slc):
        for m, wm in enumerate(w_imp):
            j = SLC_RATIO * s + m - 1
            if 0 <= j < n_cmp:
                w[j, s] += wm
    return w


def _select_blocks(s_slc, q_pos0, n_slc):
    t, w = s_slc.shape
    blk = lax.broadcasted_iota(jnp.int32, (t, w), 1)
    qpos = q_pos0 + lax.broadcasted_iota(jnp.int32, (t, w), 0)
    qb = qpos // SLC_BLOCK
    forced = (blk == 0) | (blk == qb) | (blk == qb - 1)
    score = jnp.where(forced, FORCE_SCORE, jnp.where(blk * SLC_BLOCK <= qpos, s_slc, -1.0))
    score = jnp.where(blk < n_slc, score, -2.0)
    cnt = jnp.zeros((t, w), F32)
    for sp in range(n_slc):
        col = score[:, sp:sp + 1]
        tie = jnp.where(blk > sp, 1.0, 0.0)
        cnt = cnt + jnp.where(col > score, 1.0, jnp.where(col == score, tie, 0.0))
    return jnp.where(cnt < N_SELECT, 1.0, 0.0)


def _masked_softmax(s, valid, axis=-1, base2=False):
    m = jnp.max(s, axis=axis, keepdims=True)
    e = jnp.where(valid, (jnp.exp2 if base2 else jnp.exp)(s - m), 0.0)
    return e / jnp.maximum(jnp.sum(e, axis=axis, keepdims=True), 1e-30)


def _reset_flash(m_ref, l_ref, acc_ref):
    m_ref[...] = jnp.full(m_ref.shape, NEG_INF, F32)
    l_ref[...] = jnp.zeros(l_ref.shape, F32)
    acc_ref[...] = jnp.zeros(acc_ref.shape, F32)


def _select_blocks_t(s_slc, q_pos0, n_slc):
    w, t = s_slc.shape
    n8 = -(-n_slc // 8) * 8
    blk = lax.broadcasted_iota(jnp.int32, (n8, t), 0)
    qpos = q_pos0 + lax.broadcasted_iota(jnp.int32, (n8, t), 1)
    qb = qpos // SLC_BLOCK
    forced = (blk == 0) | (blk == qb) | (blk == qb - 1)
    score = jnp.where(forced, FORCE_SCORE, jnp.where(blk * SLC_BLOCK <= qpos, s_slc[:n8], -1.0))
    score = jnp.where(blk < n_slc, score, -2.0)
    cnt = jnp.zeros((n8, t), F32)
    for sp in range(n_slc):
        row = score[sp:sp + 1, :]
        tie = jnp.where(blk > sp, 1.0, 0.0)
        cnt = cnt + jnp.where(row > score, 1.0, jnp.where(row == score, tie, 0.0))
    sel = jnp.where(cnt < N_SELECT, 1.0, 0.0)
    if n8 < w:
        sel = jnp.concatenate([sel, jnp.zeros((w - n8, t), F32)], axis=0)
    return sel


def _nsa_prompt_kernel(q_ref, gt_ref, kc_ref, vc_ref, ks_ref, vs_ref, kw_ref, vw_ref, bias_ref, u_ref, wimp_ref,
                       o_ref, qs_ref, ksb_ref, vst_ref, kwb_ref, vwt_ref, sel_ref, m_ref, l_ref, acc_ref,
                       *, tq, n_slc):
    qi = pl.program_id(2)

    @pl.when(qi == 0)
    def _():
        _stage_kv(ks_ref, vs_ref, ksb_ref, vst_ref, tq)
        _stage_kv(kw_ref, vw_ref, kwb_ref, vwt_ref, tq)

    qs_ref[...] = jnp.concatenate(
        [q_ref[0, :, r * HEAD_W:(r + 1) * HEAD_W] for r in range(HPG)], axis=0).astype(BF16)

    n_pad = kc_ref.shape[1]
    start = pl.multiple_of(n_pad - (tq // CMP_STRIDE) * (qi + 1), 8)
    scale = NSA_SCALE * LOG2E
    bias_c = u_ref[0, pl.ds(start, n_pad), :]
    s = _dot_t(kc_ref[0].astype(BF16), qs_ref[...]) * scale + bias_c
    pc = _masked_softmax(s, bias_c > 0.5 * NEG_INF, axis=0, base2=True)
    o_cmp = _dot(vc_ref[0].T.astype(BF16), pc.astype(BF16))
    psum = pc[:, 0:tq] + pc[:, tq:2 * tq] + pc[:, 2 * tq:3 * tq] + pc[:, 3 * tq:4 * tq]
    hi, mid, lo = _split3(psum)
    wimp = wimp_ref[...]
    s_slc = _dot(wimp, hi) + _dot(wimp, mid) + _dot(wimp, lo)
    sel_ref[...] = _select_blocks_t(s_slc, qi * tq, n_slc).astype(BF16)

    _reset_flash(m_ref, l_ref, acc_ref)

    far = bias_ref[2, 0:1, :]

    def key_mask(kt, width):
        key = lax.broadcasted_iota(jnp.int32, (width, LANE), 0)
        blk = lax.broadcasted_iota(jnp.int32, (width, LANE), 1)
        onehot = jnp.where(blk == kt * (width // SLC_BLOCK) + key // SLC_BLOCK, 1.0, 0.0).astype(BF16)
        return jnp.concatenate([_dot(onehot, sel_ref[...])] * HPG, axis=1)

    def slc_far_body(width):
        def body(kt, c):
            rows = pl.ds(pl.multiple_of(kt * width, width), width)
            _flash_tile_t(ksb_ref[rows, :], vst_ref[:, rows], qs_ref, m_ref, l_ref, acc_ref, scale=scale,
                          far_bias=far, mask=key_mask(kt, width))
            return c
        return body

    def slc_near_body(kt, c):
        rows = pl.ds(pl.multiple_of(kt * tq, tq), tq)
        typ = qi - kt
        _flash_tile_t(ksb_ref[rows, :], vst_ref[:, rows], qs_ref, m_ref, l_ref, acc_ref, scale=scale,
                      bias=lambda cols: bias_ref[typ, :, cols], mask=key_mask(kt, tq))
        return c

    n_far = jnp.maximum(qi - 1, 0)
    lax.fori_loop(0, n_far // FAR_TILES, slc_far_body(FAR_TILES * tq), 0)
    lax.fori_loop((n_far // FAR_TILES) * FAR_TILES, n_far, slc_far_body(tq), 0)
    lax.fori_loop(n_far, qi + 1, slc_near_body, 0)
    o_slc = acc_ref[...] / l_ref[...]

    _reset_flash(m_ref, l_ref, acc_ref)
    nw = WINDOW // tq

    def win_body(kt, c):
        rows = pl.ds(pl.multiple_of(kt * tq, tq), tq)
        t = qi - kt
        typ = jnp.where(t == nw, 3, jnp.minimum(t, 2))
        _flash_tile_t(kwb_ref[rows, :], vwt_ref[:, rows], qs_ref, m_ref, l_ref, acc_ref, scale=scale,
                      bias=lambda cols: bias_ref[typ, :, cols])
        return c

    lax.fori_loop(jnp.maximum(qi - nw, 0), qi + 1, win_body, 0)
    o_win = acc_ref[...] / l_ref[...]

    sig = jax.nn.sigmoid(gt_ref[0]).T
    for r in range(HPG):
        cs = slice(r * tq, (r + 1) * tq)
        o_t = (sig[r:r + 1, :] * o_cmp[:, cs] + sig[HPG + r:HPG + r + 1, :] * o_slc[:, cs]
               + sig[2 * HPG + r:2 * HPG + r + 1, :] * o_win[:, cs])
        o_ref[0, :, r * HEAD_W:(r + 1) * HEAD_W] = o_t.T


def _prompt_cmp_buckets(tq, n_pad):
    jp = np.arange(2 * n_pad)[:, None] - (n_pad - tq // CMP_STRIDE)
    i = np.arange(tq)[None, :]
    return _bucket_np(i - CMP_STRIDE * jp - (2 * CMP_STRIDE - 1))[None].astype(np.int32)


def nsa_prompt(proj, gates, kc, vc, bias, u_bias, tq=PROMPT_TQ):
    b, t, _ = proj.shape
    gw = HPG * HEAD_W
    n_pad = kc.shape[1]
    n_slc = t // SLC_BLOCK
    wimp = jnp.asarray(_importance_matrix(n_pad - 1, n_pad, n_slc, LANE).T, BF16)
    rows = HPG * tq
    seq = lambda n: pl.BlockSpec((1, n, HEAD_W), lambda i, g, j: (i, 0, g))
    kv = lambda a: pl.BlockSpec((1, t, HEAD_W), lambda i, g, j: (i, 0, N_HEADS + a * N_KV + g))
    return pl.pallas_call(
        functools.partial(_nsa_prompt_kernel, tq=tq, n_slc=n_slc),
        grid=(b, N_KV, t // tq),
        in_specs=[
            pl.BlockSpec((1, tq, gw), lambda i, g, j: (i, j, g)),
            pl.BlockSpec((1, tq, LANE), lambda i, g, j: (i, j, g)),
            seq(n_pad), seq(n_pad), kv(2), kv(3), kv(4), kv(5),
            pl.BlockSpec((4, tq, rows), lambda i, g, j: (0, 0, g)),
            pl.BlockSpec((1, 2 * n_pad, rows), lambda i, g, j: (0, 0, g)),
            pl.BlockSpec(wimp.shape, lambda i, g, j: (0, 0)),
        ],
        out_specs=pl.BlockSpec((1, tq, gw), lambda i, g, j: (i, j, g)),
        out_shape=jax.ShapeDtypeStruct((b, t, N_HEADS * HEAD_W), F32),
        scratch_shapes=[
            pltpu.VMEM((rows, HEAD_W), BF16),
            pltpu.VMEM((t, HEAD_W), BF16),
            pltpu.VMEM((HEAD_W, t), BF16),
            pltpu.VMEM((t, HEAD_W), BF16),
            pltpu.VMEM((HEAD_W, t), BF16),
            pltpu.VMEM((LANE, tq), BF16),
            pltpu.VMEM((1, rows), F32),
            pltpu.VMEM((1, rows), F32),
            pltpu.VMEM((HEAD_W, rows), F32),
        ],
        compiler_params=_params(("parallel", "parallel", "arbitrary")),
        name="nsa_prompt",
    )(proj, gates, kc, vc, proj, proj, proj, proj, bias, u_bias, wimp)


def _decode_cmp_buckets(t_new, past_len):
    n_pad = past_len // CMP_STRIDE
    i = np.arange(t_new)[:, None]
    j = n_pad - LANE + np.arange(LANE)[None, :]
    last = _bucket_np(past_len + i - CMP_STRIDE * j - (2 * CMP_STRIDE - 1))
    last = np.where(j < n_pad - 1, last, -1)
    return np.stack([np.full((t_new, LANE), N_BUCKETS - 1, np.int32), last]).astype(np.int32)


def _decode_win_buckets(t_new, n_tiles):
    i = np.arange(t_new)[:, None]
    idx = np.arange(n_tiles * LANE)[None, :]
    dw = WINDOW + i - idx
    ok = (dw >= 0) & (dw < WINDOW) & (idx < WINDOW + t_new)
    b = np.where(ok, _bucket_np(dw), -1)
    return np.stack([b[:, k * LANE:(k + 1) * LANE] for k in range(n_tiles)]).astype(np.int32)


def _nsa_decode_kernel(pt_ref, q_ref, gt_ref, kc_ref, vc_ref, kn_ref, vn_ref, kw_ref, vw_ref, *rest,
                       n_pages, past_len):
    del pt_ref
    kp = rest[:PAGES_PER_STEP]
    vp = rest[PAGES_PER_STEP:2 * PAGES_PER_STEP]
    (bias_ref, bias_c_ref, bias_w_ref, wimp_ref, o_ref,
     qr_ref, selt_ref, ocmp_ref, owin_ref, m_ref, l_ref, acc_ref) = rest[2 * PAGES_PER_STEP:]
    s = pl.program_id(1)
    t_new = q_ref.shape[1]
    rows = HPG * t_new
    n_slc = -(-(past_len + t_new) // SLC_BLOCK)

    @pl.when(s == 0)
    def _():
        _reset_flash(m_ref, l_ref, acc_ref)
        qr_ref[...] = _query_rows(q_ref, 1, NSA_SCALE)
        n_pad = kc_ref.shape[1]
        sel_t = jnp.zeros(selt_ref.shape, F32)
        for g in range(N_KV):
            sl = slice(g * HEAD_W, (g + 1) * HEAD_W)
            gr = slice(g * rows, (g + 1) * rows)
            qx = _stack_group_heads(q_ref, g).astype(BF16)
            bias_c = jnp.concatenate([bias_c_ref[0, gr, :]] * (n_pad // LANE - 1) + [bias_c_ref[1, gr, :]], axis=1)
            sc = _dot_t(qx, kc_ref[0, :, sl].astype(BF16)) * NSA_SCALE + bias_c
            pc = _masked_softmax(sc, bias_c > 0.5 * NEG_INF)
            ocmp_ref[g] = _dot(pc.astype(BF16), vc_ref[0, :, sl].astype(BF16))
            psum = sum(pc[r * t_new:(r + 1) * t_new] for r in range(1, HPG)) + pc[0:t_new]
            s_slc = _dot_exact_rhs(psum, wimp_ref[...])
            sel = _select_blocks(s_slc, past_len, n_slc)
            sel_pad = jnp.concatenate([sel, jnp.zeros((LANE - t_new, sel.shape[1]), F32)], axis=0).T
            tok = lax.broadcasted_iota(jnp.int32, (LANE, LANE), 0)
            lane = lax.broadcasted_iota(jnp.int32, (LANE, LANE), 1)
            spread = jnp.where((lane % t_new == tok) & (lane // rows == g), 1.0, 0.0).astype(BF16)
            sel_t = sel_t + _dot(sel_pad.astype(BF16), spread)
            n_wt = bias_w_ref.shape[0]
            bias_w = jnp.concatenate([bias_w_ref[k, gr, :] for k in range(n_wt)], axis=1)
            sw = _dot_t(qx, kw_ref[0, :, sl].astype(BF16)) * NSA_SCALE + bias_w
            pw = _masked_softmax(sw, bias_w > 0.5 * NEG_INF)
            owin_ref[g] = _dot(pw.astype(BF16), vw_ref[0, :, sl].astype(BF16))
        selt_ref[...] = sel_t
        _decode_step_t(kn_ref[0], vn_ref[0], qr_ref, bias_ref[2], m_ref, l_ref, acc_ref)

    block_rows = SLC_BLOCK * N_KV
    first_block = s * (PAGES_PER_STEP * PAGE // SLC_BLOCK)
    keep = jnp.concatenate(
        [jnp.broadcast_to(selt_ref[pl.ds(first_block + b, 1), :], (block_rows, selt_ref.shape[1]))
         for b in range(PAGES_PER_STEP * PAGE // SLC_BLOCK)], axis=0)
    _decode_step_t(_page_rows(kp), _page_rows(vp), qr_ref,
                   _page_bias_t(bias_ref, s * PAGES_PER_STEP, n_pages, 1), m_ref, l_ref, acc_ref, keep=keep)

    @pl.when(s == pl.num_programs(1) - 1)
    def _():
        sig = jax.nn.sigmoid(gt_ref[0])
        o_slc_all = (acc_ref[...] / l_ref[...]).T
        for g in range(N_KV):
            o_cmp = ocmp_ref[g]
            o_win = owin_ref[g]
            for r in range(HPG):
                h = g * HPG + r
                rs = slice(r * t_new, (r + 1) * t_new)
                c0 = g * LANE + r
                o_ref[0, :, h * HEAD_W:(h + 1) * HEAD_W] = (
                    sig[:, c0:c0 + 1] * o_cmp[rs]
                    + sig[:, c0 + HPG:c0 + HPG + 1] * o_slc_all[h * t_new:(h + 1) * t_new]
                    + sig[:, c0 + 2 * HPG:c0 + 2 * HPG + 1] * o_win[rs])


def nsa_decode(q, gates, kc, vc, k_new, v_new, kw_src, vw_src, cache_k, cache_v, layer, page_table,
               bias, bias_c, bias_w, past_len):
    b, t_new, _ = q.shape
    n_pages = page_table.shape[1]
    rows = HPG * t_new
    lanes = N_HEADS * t_new
    assert lanes == LANE
    n_pad = kc.shape[1]
    n_slc = -(-(past_len + t_new) // SLC_BLOCK)
    n_cols = -(-n_slc // LANE) * LANE
    wimp = jnp.asarray(_importance_matrix(n_pad - 1, n_pad, n_slc, n_cols), BF16)
    fixed = lambda *shape: pl.BlockSpec(shape, lambda i, s, pt: (0,) * len(shape))
    per_b = lambda *shape: pl.BlockSpec((1,) + shape, lambda i, s, pt: (i,) + (0,) * len(shape))
    grid_spec = pltpu.PrefetchScalarGridSpec(
        num_scalar_prefetch=1,
        grid=(b, n_pages // PAGES_PER_STEP),
        in_specs=[per_b(t_new, N_HEADS * HEAD_W), per_b(t_new, N_KV * LANE),
                  per_b(n_pad, NSA_KV), per_b(n_pad, NSA_KV), per_b(PAGE_ROWS, HEAD_W), per_b(PAGE_ROWS, HEAD_W),
                  per_b(kw_src.shape[1], NSA_KV), per_b(kw_src.shape[1], NSA_KV)]
        + _page_specs(layer, 2)
        + [fixed(*bias.shape), fixed(*bias_c.shape), fixed(*bias_w.shape), fixed(*wimp.shape)],
        out_specs=per_b(t_new, N_HEADS * HEAD_W),
        scratch_shapes=[
            pltpu.VMEM((lanes, HEAD_W), BF16),
            pltpu.VMEM((n_cols, lanes), F32),
            pltpu.VMEM((N_KV, rows, HEAD_W), F32),
            pltpu.VMEM((N_KV, rows, HEAD_W), F32),
            pltpu.VMEM((1, lanes), F32),
            pltpu.VMEM((1, lanes), F32),
            pltpu.VMEM((HEAD_W, lanes), F32),
        ],
    )
    return pl.pallas_call(
        functools.partial(_nsa_decode_kernel, n_pages=n_pages, past_len=past_len),
        grid_spec=grid_spec,
        out_shape=jax.ShapeDtypeStruct(q.shape[:2] + (N_HEADS * HEAD_W,), F32),
        compiler_params=_params(("parallel", "arbitrary")),
        name="nsa_decode",
    )(page_table, q, gates, kc, vc, k_new, v_new, kw_src, vw_src,
      *([cache_k] * PAGES_PER_STEP), *([cache_v] * PAGES_PER_STEP), bias, bias_c, bias_w, wimp)


def _gate_weights(w_g):
    d = w_g.shape[0]
    w = jnp.transpose(w_g.reshape(d, 3, N_KV, HPG), (0, 2, 1, 3)).reshape(d, N_KV, 3 * HPG)
    return jnp.pad(w, ((0, 0), (0, 0), (0, LANE - 3 * HPG))).reshape(d, N_KV * LANE)


def nsa_layer(xp, xs, gain, w_in, q_norm, k_norm, pe, w1, w2, w_out, caches, layer, page_table,
              bias_p, u_bias, bias_s, bias_sc, bias_sw, past_len):
    cmp_k, cmp_v, slc_k, slc_v, win_k, win_v = caches
    nqkv = N_HEADS * HEAD_W + 6 * NSA_KV
    w_g = _gate_weights(w_in[layer, :, nqkv:])[None]
    w_in_t = jnp.swapaxes(w_in, 1, 2)
    nq = N_HEADS * HEAD_W
    tile = 4 * LANE
    q_gain = jnp.tile(q_norm, tile // HEAD_W)
    head_norm = (HEAD_W, {**{c: q_gain for c in range(nq // tile)},
                          nq // tile + 2: jnp.tile(k_norm[1], tile // HEAD_W),
                          nq // tile + 4: jnp.tile(k_norm[2], tile // HEAD_W)})
    outs = []
    for x, paged in ((xp, False), (xs, True)):
        b, t, d = x.shape
        x2 = x.reshape(b * t, d)
        proj2 = dense(x2, w_in_t, layer, gain=gain, n=nqkv, w_t=True, head_norm=head_norm)
        proj = proj2.reshape(b, t, -1)
        gates = dense(x2, w_g, 0, gain=gain).reshape(b, t, -1)
        rows = [group_rows(proj2, nq // NSA_KV + a) for a in range(4)]
        as_state = lambda r: r.reshape(b, t, N_KV, HEAD_W)
        kw, vw = proj[:, :, nq + 4 * NSA_KV:nq + 5 * NSA_KV], proj[:, :, nq + 5 * NSA_KV:]
        if paged:
            kc = compress(_as_page_rows(cmp_k), layer, page_table, pe[0], w1[0], w2[0], k_norm[0])
            vc = compress(_as_page_rows(cmp_v), layer, page_table, pe[1], w1[1], w2[1], None)
            n_wt = bias_sw.shape[0]
            kw_src = jnp.concatenate([win_k[layer], kw], axis=1)
            vw_src = jnp.concatenate([win_v[layer], vw], axis=1)
            o = nsa_decode(proj, gates, kc, vc, _new_page_rows(rows[2], b), _new_page_rows(rows[3], b),
                           _pad_rows(kw_src, n_wt * LANE), _pad_rows(vw_src, n_wt * LANE),
                           _as_page_rows(slc_k), _as_page_rows(slc_v), layer, page_table,
                           bias_s, bias_sc, bias_sw, past_len)
            kw_win, vw_win = kw_src, vw_src
        else:
            n_pg = t // PAGE
            ident = jnp.arange(b * n_pg, dtype=jnp.int32).reshape(b, n_pg)
            as_pages = lambda r: r.reshape(1, b * n_pg, PAGE_ROWS, HEAD_W)
            kc = compress(as_pages(rows[0]), 0, ident, pe[0], w1[0], w2[0], k_norm[0])
            vc = compress(as_pages(rows[1]), 0, ident, pe[1], w1[1], w2[1], None)
            o = nsa_prompt(proj, gates, kc, vc, bias_p, u_bias)
            kw_win, vw_win = kw, vw
        y = dense(o.reshape(b * t, -1), w_out, layer, res=x2).reshape(b, t, d)
        win = tuple(a[:, -WINDOW:].reshape(b, WINDOW, N_KV, HEAD_W) for a in (kw_win, vw_win))
        st = tuple(as_state(r) for r in rows) + win
        outs.append((y, st))
    (yp, stp), (ys, sts) = outs
    return yp, ys, stp, sts


def kernel(x_prompt, x_sample, cache_diff_k, cache_diff_v, state_ssm, state_conv, cache_nsa_cmp_k, cache_nsa_cmp_v, cache_nsa_slc_k, cache_nsa_slc_v, cache_nsa_win_k, cache_nsa_win_v, cache_mem_k, cache_mem_v, page_table, mem_prompt, rel_bias_table, norm_mix, norm_xattn, norm_mem, norm_ffn, diff_w_in, diff_q_norm, diff_k_norm, diff_lambda, diff_sub_norm, diff_w_out, ssm_w_in, ssm_conv_w, ssm_conv_b, ssm_dt_bias, ssm_a_log, ssm_d, ssm_norm, ssm_w_out, nsa_w_in, nsa_q_norm, nsa_k_norm, nsa_cmp_pe, nsa_cmp_w1, nsa_cmp_w2, nsa_w_out, xattn_w_q, xattn_w_k, xattn_w_v, xattn_q_norm, xattn_k_norm, xattn_w_o, ffn_w1, ffn_w3, ffn_w2):
    xp, xs = x_prompt, x_sample
    bp, t, d = xp.shape
    bs, t_new, _ = xs.shape
    past_len = page_table.shape[1] * PAGE
    depth = norm_mix.shape[0]

    bias_p = bias_tiles(_prompt_attn_buckets(PROMPT_TQ), rel_bias_table, heads_on_lanes=True, scale=LOG2E)
    u_bias = bias_tiles(_prompt_cmp_buckets(PROMPT_TQ, t // CMP_STRIDE), rel_bias_table, heads_on_lanes=True,
                        scale=LOG2E)
    bias_s = bias_tiles_indexed(_decode_page_index(t_new, past_len), rel_bias_table)
    bias_sc = bias_tiles(_decode_cmp_buckets(t_new, past_len), rel_bias_table)
    n_wt = -(-(WINDOW + t_new) // LANE)
    bias_sw = bias_tiles(_decode_win_buckets(t_new, n_wt), rel_bias_table)

    mem_k, mem_v = mem_kv(mem_prompt, norm_mem, xattn_w_k, xattn_w_v, xattn_k_norm)
    xw = X_HEADS * X_DH
    win_k = cache_nsa_win_k.reshape(*cache_nsa_win_k.shape[:3], NSA_KV)
    win_v = cache_nsa_win_v.reshape(*cache_nsa_win_v.shape[:3], NSA_KV)

    dkp, dvp, dks, dvs = [], [], [], []
    ssp, cvp, sss, cvs = [], [], [], []
    nsp, nss = [], []
    for i in range(depth):
        kind, j = i % 3, i // 3
        if kind == 0:
            lam_init = 0.8 - 0.6 * math.exp(-0.3 * i)
            xp, xs, kp_, vp_, ks_, vs_ = diff_layer(
                xp, xs, norm_mix[i], diff_w_in, diff_q_norm[j], diff_k_norm[j], diff_lambda[j],
                diff_sub_norm[j], diff_w_out, cache_diff_k, cache_diff_v, j, page_table,
                bias_p, bias_s, lam_init)
            dkp.append(kp_)
            dvp.append(vp_)
            dks.append(ks_)
            dvs.append(vs_)
        elif kind == 1:
            xp, xs, hp_, cp_, hs_, cs_ = ssd_layer(
                xp, xs, norm_mix[i], ssm_w_in, j, ssm_conv_w[j], ssm_conv_b[j], ssm_dt_bias[j], ssm_a_log[j],
                ssm_d[j], ssm_norm[j], ssm_w_out, state_conv[j], state_ssm[j])
            ssp.append(hp_)
            cvp.append(cp_)
            sss.append(hs_)
            cvs.append(cs_)
        else:
            caches = (cache_nsa_cmp_k, cache_nsa_cmp_v, cache_nsa_slc_k, cache_nsa_slc_v, win_k, win_v)
            xp, xs, stp, sts = nsa_layer(
                xp, xs, norm_mix[i], nsa_w_in, nsa_q_norm[j], nsa_k_norm[j], nsa_cmp_pe[j], nsa_cmp_w1[j],
                nsa_cmp_w2[j], nsa_w_out, caches, j, page_table,
                bias_p, u_bias, bias_s, bias_sc, bias_sw, past_len)
            nsp.append(stp)
            nss.append(sts)
        xp = xattn(xp, norm_xattn[i], xattn_w_q, xattn_q_norm[i], mem_k[i], mem_v[i], xattn_w_o, i)
        xs = xattn(xs, norm_xattn[i], xattn_w_q, xattn_q_norm[i], cache_mem_k[i].reshape(bs, N_MEM, xw),
                   cache_mem_v[i].reshape(bs, N_MEM, xw), xattn_w_o, i)
        xp = ffn(xp.reshape(bp * t, d), norm_ffn[i], ffn_w1, ffn_w3, ffn_w2, i).reshape(bp, t, d)
        xs = ffn(xs.reshape(bs * t_new, d), norm_ffn[i], ffn_w1, ffn_w3, ffn_w2, i).reshape(bs, t_new, d)

    st = lambda xs_: jnp.stack(xs_, axis=0)
    nsp_t = [st([s[a] for s in nsp]) for a in range(6)]
    nss_t = [st([s[a] for s in nss]) for a in range(6)]
    mem_shape = (depth, bp, N_MEM, X_HEADS, X_DH)
    return (xp, xs,
            st(dkp), st(dvp), st(dks), st(dvs),
            st(ssp), st(cvp), st(sss), st(cvs),
            *nsp_t, *nss_t,
            mem_k.reshape(mem_shape), mem_v.reshape(mem_shape))
```

```python
import functools
import math

import numpy as np
import jax
import jax.numpy as jnp
from jax import lax
from jax.experimental import pallas as pl
from jax.experimental.pallas import tpu as pltpu

F32 = jnp.float32
BF16 = jnp.bfloat16

D_MODEL = 2048
DEPTH = 4
PAGE = 128
N_HEADS = 16
N_KV = 4
HPG = N_HEADS // N_KV
DIFF_DH = 64
HEAD_W = 128
N_BUCKETS = 32
MAX_EXACT = 16
MAX_DIST = 128
D_INNER = 2 * D_MODEL
SSM_HEADDIM = 64
SSM_HEADS = D_INNER // SSM_HEADDIM
SSM_GROUPS = 8
SSM_HPG = SSM_HEADS // SSM_GROUPS
D_STATE = 128
CONV_W = 4
CONV_DIM = D_INNER + 2 * SSM_GROUPS * D_STATE
SSM_CHUNK = 128
CMP_STRIDE = 16
SLC_BLOCK = 64
SLC_RATIO = SLC_BLOCK // CMP_STRIDE
N_SELECT = 16
WINDOW = 512
N_MEM = 256
X_HEADS = 4
X_DH = 128
EPS = 1e-6
NEG_INF = -1e30
FORCE_SCORE = 1e4

LANE = 128
VMEM_LIMIT = 56 * 1024 * 1024
DENSE_TALL_ROWS = 2048


def _params(sem):
    return pltpu.CompilerParams(dimension_semantics=sem, vmem_limit_bytes=VMEM_LIMIT)


def _rms(x, gain):
    return x * lax.rsqrt(jnp.mean(x * x, axis=-1, keepdims=True) + EPS) * gain


def _dot(a, b):
    return jnp.dot(a, b, preferred_element_type=F32)


def _dot_t(a, b):
    return lax.dot_general(a, b, (((1,), (1,)), ((), ())), preferred_element_type=F32)


def _split3(x):
    hi = x.astype(BF16)
    r1 = x - hi.astype(F32)
    mid = r1.astype(BF16)
    lo = (r1 - mid.astype(F32)).astype(BF16)
    return hi, mid, lo


def _dot_exact_rhs(x, m_bf16):
    hi, mid, lo = _split3(x)
    return _dot(hi, m_bf16) + _dot(mid, m_bf16) + _dot(lo, m_bf16)


def _norm64(blk, gain, lo):
    sq = blk * blk
    s_lo = jnp.sum(jnp.where(lo, sq, 0.0), axis=-1, keepdims=True)
    s_hi = jnp.sum(jnp.where(lo, 0.0, sq), axis=-1, keepdims=True)
    ms = jnp.where(lo, s_lo, s_hi) * (1.0 / DIFF_DH)
    return blk * lax.rsqrt(ms + EPS) * gain


def _head_norm_tile(y, gain, seg):
    lo = lax.broadcasted_iota(jnp.int32, (y.shape[0], LANE), 1) < DIFF_DH
    blocks = []
    for c in range(y.shape[1] // LANE):
        sl = slice(c * LANE, (c + 1) * LANE)
        blocks.append(_norm64(y[:, sl], gain[:, sl], lo) if seg == DIFF_DH else _rms(y[:, sl], gain[:, sl]))
    return jnp.concatenate(blocks, axis=1)


def _dense_kernel(*refs, norm, residual, w_t, head_seg, norm_tiles):
    refs = list(refs)
    x_ref = refs.pop(0)
    g_ref = refs.pop(0) if norm else None
    w_ref = refs.pop(0)
    r_ref = refs.pop(0) if residual else None
    hg_ref = refs.pop(0) if head_seg else None
    o_ref, xb_ref = refs
    j = pl.program_id(1)

    @pl.when(j == 0)
    def _():
        x = x_ref[...]
        if norm:
            x = _rms(x, g_ref[...])
        xb_ref[...] = x.astype(BF16)

    w = w_ref[0].astype(BF16)
    y = _dot_t(xb_ref[...], w) if w_t else _dot(xb_ref[...], w)
    if residual:
        y = y + r_ref[...]
    if not head_seg:
        o_ref[...] = y
        return
    is_norm = functools.reduce(jnp.logical_or, [j == t for t in norm_tiles])

    @pl.when(is_norm)
    def _():
        o_ref[...] = _head_norm_tile(y, hg_ref[0], head_seg)

    @pl.when(jnp.logical_not(is_norm))
    def _():
        o_ref[...] = y


def dense(x, w, layer, gain=None, res=None, n=None, w_t=False, head_norm=None):
    m, k = x.shape
    n = w.shape[1 if w_t else 2] if n is None else n
    tm = m if m <= 512 else (DENSE_TALL_ROWS if k <= D_MODEL and m % DENSE_TALL_ROWS == 0 else 512)
    tn = LANE if n % 256 else (512 if n % 512 == 0 else 256)
    assert m % tm == 0 and n % tn == 0
    norm, residual = gain is not None, res is not None
    x_mode = dict(pipeline_mode=pl.Buffered(1)) if tm == DENSE_TALL_ROWS else {}
    in_specs = [pl.BlockSpec((tm, k), lambda i, j: (i, 0), **x_mode)]
    args = [x]
    if norm:
        in_specs.append(pl.BlockSpec((1, k), lambda i, j: (0, 0)))
        args.append(gain.reshape(1, k))
    if w_t:
        in_specs.append(pl.BlockSpec((1, tn, k), lambda i, j: (layer, j, 0)))
    else:
        in_specs.append(pl.BlockSpec((1, k, tn), lambda i, j: (layer, 0, j)))
    args.append(w)
    if residual:
        in_specs.append(pl.BlockSpec((tm, tn), lambda i, j: (i, j)))
        args.append(res)
    head_seg, norm_tiles = 0, ()
    if head_norm is not None:
        head_seg, tile_gains = head_norm
        norm_tiles = tuple(sorted(tile_gains))
        ones = jnp.ones((tn,), F32)
        gains = jnp.stack([tile_gains.get(t, ones) for t in range(n // tn)]).reshape(n // tn, 1, tn)
        in_specs.append(pl.BlockSpec((1, 1, tn), lambda i, j: (j, 0, 0)))
        args.append(gains)
    return pl.pallas_call(
        functools.partial(_dense_kernel, norm=norm, residual=residual, w_t=w_t, head_seg=head_seg,
                          norm_tiles=norm_tiles),
        grid=(m // tm, n // tn),
        in_specs=in_specs,
        out_specs=pl.BlockSpec((tm, tn), lambda i, j: (i, j)),
        out_shape=jax.ShapeDtypeStruct((m, n), F32),
        scratch_shapes=[pltpu.VMEM((tm, k), BF16)],
        compiler_params=_params(("parallel", "arbitrary")),
        name="dense",
    )(*args)


def _ffn_kernel(x_ref, g_ref, w1_ref, w3_ref, w2_ref, o_ref, xb_ref):
    @pl.when(pl.program_id(1) == 0)
    def _():
        x = x_ref[...]
        xb_ref[...] = _rms(x, g_ref[...]).astype(BF16)
        o_ref[...] = x

    xb = xb_ref[...]
    h1 = _dot(xb, w1_ref[0].astype(BF16))
    h3 = _dot(xb, w3_ref[0].astype(BF16))
    a = (h1 * jax.nn.sigmoid(h1) * h3).astype(BF16)
    o_ref[...] += _dot(a, w2_ref[0].astype(BF16))


def ffn(x, gain, w1, w3, w2, layer):
    m, d = x.shape
    f = w1.shape[2]
    tm = min(m, 1024)
    tf = 256
    assert m % tm == 0 and f % tf == 0
    return pl.pallas_call(
        _ffn_kernel,
        grid=(m // tm, f // tf),
        in_specs=[
            pl.BlockSpec((tm, d), lambda i, j: (i, 0), pipeline_mode=pl.Buffered(1)),
            pl.BlockSpec((1, d), lambda i, j: (0, 0)),
            pl.BlockSpec((1, d, tf), lambda i, j: (layer, 0, j)),
            pl.BlockSpec((1, d, tf), lambda i, j: (layer, 0, j)),
            pl.BlockSpec((1, tf, d), lambda i, j: (layer, j, 0)),
        ],
        out_specs=pl.BlockSpec((tm, d), lambda i, j: (i, 0)),
        out_shape=jax.ShapeDtypeStruct((m, d), F32),
        scratch_shapes=[pltpu.VMEM((tm, d), BF16)],
        compiler_params=_params(("parallel", "arbitrary")),
        name="ffn",
    )(x, gain.reshape(1, d), w1, w3, w2)


def _mem_kv_kernel(mem_ref, g_ref, wk_ref, wv_ref, kn_ref, k_ref, v_ref):
    m = _rms(mem_ref[0], g_ref[0]).astype(BF16)
    k = _dot(m, wk_ref[0].astype(BF16))
    v_ref[0, 0] = _dot(m, wv_ref[0].astype(BF16))
    for h in range(X_HEADS):
        sl = slice(h * X_DH, (h + 1) * X_DH)
        k_ref[0, 0, :, sl] = _rms(k[:, sl], kn_ref[0])


def mem_kv(mem, g_mem, wk, wv, k_norm):
    b = mem.shape[0]
    nl = wk.shape[0]
    hw = X_HEADS * X_DH
    shape = jax.ShapeDtypeStruct((nl, b, N_MEM, hw), F32)
    return pl.pallas_call(
        _mem_kv_kernel,
        grid=(nl, b),
        in_specs=[
            pl.BlockSpec((1, N_MEM, D_MODEL), lambda l, i: (i, 0, 0)),
            pl.BlockSpec((1, 1, D_MODEL), lambda l, i: (l, 0, 0)),
            pl.BlockSpec((1, D_MODEL, hw), lambda l, i: (l, 0, 0)),
            pl.BlockSpec((1, D_MODEL, hw), lambda l, i: (l, 0, 0)),
            pl.BlockSpec((1, 1, X_DH), lambda l, i: (l, 0, 0)),
        ],
        out_specs=[pl.BlockSpec((1, 1, N_MEM, hw), lambda l, i: (l, i, 0, 0))] * 2,
        out_shape=[shape, shape],
        compiler_params=_params(("parallel", "parallel")),
        name="mem_kv",
    )(mem, g_mem.reshape(nl, 1, D_MODEL), wk, wv, k_norm.reshape(nl, 1, X_DH))


def _xattn_kernel(x_ref, g_ref, wq_ref, qn_ref, k_ref, v_ref, wo_ref, o_ref):
    nb, tm, d = x_ref.shape
    x = x_ref[...].reshape(nb * tm, d)
    q = _dot(_rms(x, g_ref[...]).astype(BF16), wq_ref[0].astype(BF16))
    rows = []
    for bb in range(nb):
        outs = []
        for h in range(X_HEADS):
            sl = slice(h * X_DH, (h + 1) * X_DH)
            qh = _rms(q[bb * tm:(bb + 1) * tm, sl], qn_ref[...]).astype(BF16)
            s = _dot_t(qh, k_ref[bb, :, sl].astype(BF16)) * (X_DH ** -0.5)
            e = jnp.exp(s - jnp.max(s, axis=-1, keepdims=True))
            p = e / jnp.sum(e, axis=-1, keepdims=True)
            outs.append(_dot(p.astype(BF16), v_ref[bb, :, sl].astype(BF16)))
        rows.append(jnp.concatenate(outs, axis=1))
    o = jnp.concatenate(rows, axis=0).astype(BF16)
    o_ref[...] = (x + _dot(o, wo_ref[0].astype(BF16))).reshape(nb, tm, d)


def xattn(x, gain, wq, q_norm, k, v, wo, layer):
    b, t, d = x.shape
    hw = X_HEADS * X_DH
    tm = min(t, 512)
    nb = b if t < LANE else 1
    return pl.pallas_call(
        _xattn_kernel,
        grid=(b // nb, t // tm),
        in_specs=[
            pl.BlockSpec((nb, tm, d), lambda i, j: (i, j, 0)),
            pl.BlockSpec((1, d), lambda i, j: (0, 0)),
            pl.BlockSpec((1, d, hw), lambda i, j: (layer, 0, 0)),
            pl.BlockSpec((1, X_DH), lambda i, j: (0, 0)),
            pl.BlockSpec((nb, N_MEM, hw), lambda i, j: (i, 0, 0)),
            pl.BlockSpec((nb, N_MEM, hw), lambda i, j: (i, 0, 0)),
            pl.BlockSpec((1, hw, d), lambda i, j: (layer, 0, 0)),
        ],
        out_specs=pl.BlockSpec((nb, tm, d), lambda i, j: (i, j, 0)),
        out_shape=jax.ShapeDtypeStruct((b, t, d), F32),
        compiler_params=_params(("parallel", "parallel")),
        name="xattn",
    )(x, gain.reshape(1, d), wq, q_norm.reshape(1, X_DH), k, v, wo)


def _bucket_np(dist):
    n = np.maximum(dist, 0)
    nf = np.maximum(n, 1).astype(np.float64)
    large = MAX_EXACT + (np.log(nf / MAX_EXACT) / math.log(MAX_DIST / MAX_EXACT)
                         * (N_BUCKETS - MAX_EXACT)).astype(np.int64)
    b = np.where(n < MAX_EXACT, n, np.minimum(large, N_BUCKETS - 1))
    return np.where(dist < 0, -1, b).astype(np.int32)


def _bias_kernel(bkt_ref, tab_ref, o_ref, *, scale):
    h = pl.program_id(1)
    b = bkt_ref[0]
    acc = jnp.full(b.shape, NEG_INF, F32)
    for k in range(N_BUCKETS):
        acc = jnp.where(b == k, tab_ref[k, h] * scale, acc)
    o_ref[0] = acc


def _bias_packed_kernel(idx_ref, tab_ref, o_ref, *, scale):
    idx = idx_ref[0]
    acc = jnp.full(idx.shape, NEG_INF, F32)
    for k in range(N_BUCKETS):
        for h in range(N_HEADS):
            acc = jnp.where(idx == k * N_HEADS + h, tab_ref[k, h] * scale, acc)
    o_ref[0] = acc


def bias_tiles_indexed(idx, table, scale=1.0):
    nt, r, w = idx.shape
    return pl.pallas_call(
        functools.partial(_bias_packed_kernel, scale=scale),
        grid=(nt,),
        in_specs=[
            pl.BlockSpec((1, r, w), lambda t: (t, 0, 0)),
            pl.BlockSpec(memory_space=pltpu.SMEM),
        ],
        out_specs=pl.BlockSpec((1, r, w), lambda t: (t, 0, 0)),
        out_shape=jax.ShapeDtypeStruct((nt, r, w), F32),
        compiler_params=_params(("parallel",)),
        name="bias_tiles_indexed",
    )(jnp.asarray(idx.astype(np.int32)), table)


def bias_tiles(buckets, table, heads_on_lanes=False, scale=1.0):
    nt, r, w = buckets.shape
    if heads_on_lanes:
        out_spec = pl.BlockSpec((1, r, w), lambda t, h: (t, 0, h))
        out_shape = (nt, r, N_HEADS * w)
    else:
        out_spec = pl.BlockSpec((1, r, w), lambda t, h: (t, h, 0))
        out_shape = (nt, N_HEADS * r, w)
    return pl.pallas_call(
        functools.partial(_bias_kernel, scale=scale),
        grid=(nt, N_HEADS),
        in_specs=[
            pl.BlockSpec((1, r, w), lambda t, h: (t, 0, 0)),
            pl.BlockSpec(memory_space=pltpu.SMEM),
        ],
        out_specs=out_spec,
        out_shape=jax.ShapeDtypeStruct(out_shape, F32),
        compiler_params=_params(("parallel", "parallel")),
        name="bias_tiles",
    )(jnp.asarray(buckets), table)


def _prompt_attn_buckets(tq):
    assert WINDOW % tq == 0 and tq >= MAX_DIST
    j = np.arange(tq)[:, None]
    i = np.arange(tq)[None, :]
    far = np.full((tq, tq), N_BUCKETS - 1, np.int32)
    return np.stack([
        _bucket_np(i - j),
        _bucket_np(i - j + tq),
        far,
        np.where(j > i, far, -1),
    ]).astype(np.int32)


def _diff_lambda(lam_ref, lam_init):
    lf = lam_ref[...]
    s01 = jnp.sum(lf[0:1] * lf[1:2], axis=-1, keepdims=True)
    s23 = jnp.sum(lf[2:3] * lf[3:4], axis=-1, keepdims=True)
    return jnp.exp(s01) - jnp.exp(s23) + lam_init


def _split_maps(qs):
    lo = lax.broadcasted_iota(jnp.int32, qs.shape, 1) < DIFF_DH
    return jnp.concatenate([jnp.where(lo, qs, 0.0), jnp.where(lo, 0.0, qs)], axis=0)


def _diff_finish_rows(o, lam, sub_norm, lam_init):
    r = o.shape[0] // 2
    a = o[:r] - lam * o[r:]
    return _rms(a, sub_norm) * (1.0 - lam_init)


PROMPT_TQ = 256


LOG2E = math.log2(math.e)
FLASH_CHUNKS = 4
FAR_TILES = 2


def _flash_tile_t(k_tile, vt_tile, qs_ref, m_ref, l_ref, acc_ref, scale=None, bias=None, far_bias=None,
                  mask=None, n_chunks=FLASH_CHUNKS):
    chunk = qs_ref.shape[0] // n_chunks
    logits = [_dot_t(k_tile, qs_ref[c * chunk:(c + 1) * chunk, :]) for c in range(n_chunks)]
    for c in range(n_chunks):
        cols = slice(c * chunk, (c + 1) * chunk)
        s = logits[c]
        if scale is not None:
            s = s * scale
        if bias is not None:
            s = s + bias(cols)
        if mask is not None:
            s = jnp.where(mask[:, cols] > 0.5, s, NEG_INF)
        m_old = m_ref[:, cols]
        s_max = jnp.max(s, axis=0, keepdims=True)
        if far_bias is None:
            m_new = jnp.maximum(m_old, s_max)
            p = jnp.exp2(s - m_new)
        else:
            fb = far_bias[:, cols]
            m_new = jnp.maximum(m_old, s_max + fb)
            p = jnp.exp2(s - (m_new - fb))
        alpha = jnp.exp2(m_old - m_new)
        l_ref[:, cols] = alpha * l_ref[:, cols] + jnp.sum(p, axis=0, keepdims=True)
        acc_ref[:, cols] = alpha * acc_ref[:, cols] + _dot(vt_tile, p.astype(BF16))
        m_ref[:, cols] = m_new


def _stage_kv(k_ref, v_ref, kb_ref, vt_ref, chunk):
    for c in range(k_ref.shape[1] // chunk):
        rows = slice(c * chunk, (c + 1) * chunk)
        kb_ref[rows, :] = k_ref[0, rows, :].astype(BF16)
        vt_ref[:, rows] = v_ref[0, rows, :].T.astype(BF16)


def _diff_flash_kernel(lam_ref, q_ref, k_ref, v_ref, bias_ref, sn_ref, o_ref,
                       qs_ref, kb_ref, vt_ref, m_ref, l_ref, acc_ref, *, tq, lam_init):
    qi = pl.program_id(2)

    @pl.when(qi == 0)
    def _():
        _stage_kv(k_ref, v_ref, kb_ref, vt_ref, tq)

    qb = q_ref[0]
    qs = jnp.concatenate([qb[:, r * HEAD_W:(r + 1) * HEAD_W] for r in range(HPG)], axis=0)
    qs_ref[...] = _split_maps(qs * (DIFF_DH ** -0.5 * LOG2E)).astype(BF16)
    _reset_flash(m_ref, l_ref, acc_ref)
    far = bias_ref[2, 0:1, :]
    far = jnp.concatenate([far, far], axis=1)

    def far_body(width):
        def body(kt, c):
            rows = pl.ds(pl.multiple_of(kt * width, width), width)
            _flash_tile_t(kb_ref[rows, :], vt_ref[:, rows], qs_ref, m_ref, l_ref, acc_ref, far_bias=far)
            return c
        return body

    def near_body(kt, c):
        rows = pl.ds(pl.multiple_of(kt * tq, tq), tq)
        b = bias_ref[qi - kt]
        _flash_tile_t(kb_ref[rows, :], vt_ref[:, rows], qs_ref, m_ref, l_ref, acc_ref,
                      bias=lambda cols: jnp.concatenate([b, b], axis=1), n_chunks=1)
        return c

    n_far = jnp.maximum(qi - 1, 0)
    lax.fori_loop(0, n_far // FAR_TILES, far_body(FAR_TILES * tq), 0)
    lax.fori_loop((n_far // FAR_TILES) * FAR_TILES, n_far, far_body(tq), 0)
    lax.fori_loop(n_far, qi + 1, near_body, 0)
    o = (acc_ref[...] / l_ref[...]).T
    a = _diff_finish_rows(o, _diff_lambda(lam_ref, lam_init), sn_ref[...], lam_init)
    for r in range(HPG):
        o_ref[0, :, r * HEAD_W:(r + 1) * HEAD_W] = a[r * tq:(r + 1) * tq]


def diff_flash(qkv, bias, lam_p, sub_norm, lam_init, tq=PROMPT_TQ):
    b, t, _ = qkv.shape
    gw = HPG * HEAD_W
    rows = 2 * HPG * tq
    k0 = N_HEADS
    return pl.pallas_call(
        functools.partial(_diff_flash_kernel, tq=tq, lam_init=lam_init),
        grid=(b, N_KV, t // tq),
        in_specs=[
            pl.BlockSpec((4, DIFF_DH), lambda i, g, j: (0, 0)),
            pl.BlockSpec((1, tq, gw), lambda i, g, j: (i, j, g)),
            pl.BlockSpec((1, t, HEAD_W), lambda i, g, j: (i, 0, k0 + g)),
            pl.BlockSpec((1, t, HEAD_W), lambda i, g, j: (i, 0, k0 + N_KV + g)),
            pl.BlockSpec((4, tq, HPG * tq), lambda i, g, j: (0, 0, g)),
            pl.BlockSpec((1, HEAD_W), lambda i, g, j: (0, 0)),
        ],
        out_specs=pl.BlockSpec((1, tq, gw), lambda i, g, j: (i, j, g)),
        out_shape=jax.ShapeDtypeStruct((b, t, N_HEADS * HEAD_W), F32),
        scratch_shapes=[
            pltpu.VMEM((rows, HEAD_W), BF16),
            pltpu.VMEM((t, HEAD_W), BF16),
            pltpu.VMEM((HEAD_W, t), BF16),
            pltpu.VMEM((1, rows), F32),
            pltpu.VMEM((1, rows), F32),
            pltpu.VMEM((HEAD_W, rows), F32),
        ],
        compiler_params=_params(("parallel", "parallel", "arbitrary")),
        name="diff_flash",
    )(lam_p, qkv, qkv, qkv, bias, sub_norm.reshape(1, HEAD_W))


PAGES_PER_STEP = 16


def _decode_page_index(t_new, past_len):
    j = np.arange(PAGE)[:, None]
    i = np.arange(t_new)[None, :]
    far = np.full((PAGE, t_new), N_BUCKETS - 1, np.int32)
    new = np.where(j < t_new, _bucket_np(i - j), -1)
    g = np.arange(N_KV)[None, :, None, None]
    h = np.arange(N_HEADS)[None, None, :, None]
    tiles = []
    for bkt in (far, _bucket_np(PAGE + i - j), new):
        b = bkt[:, None, None, :]
        idx = np.where((b >= 0) & (h // HPG == g), b * N_HEADS + h, -1)
        tiles.append(idx.reshape(PAGE * N_KV, N_HEADS * t_new))
    return np.stack(tiles).astype(np.int32)


def _stack_group_heads(q_ref, g):
    return jnp.concatenate(
        [q_ref[0, :, (g * HPG + r) * HEAD_W:(g * HPG + r + 1) * HEAD_W] for r in range(HPG)], axis=0)


def _query_rows(q_ref, maps, scale):
    pieces = []
    for m in range(maps):
        for h in range(N_HEADS):
            qh = q_ref[0, :, h * HEAD_W:(h + 1) * HEAD_W] * scale
            if maps == 2:
                lo = lax.broadcasted_iota(jnp.int32, qh.shape, 1) < DIFF_DH
                qh = jnp.where(lo, qh, 0.0) if m == 0 else jnp.where(lo, 0.0, qh)
            pieces.append(qh)
    return jnp.concatenate(pieces, axis=0).astype(BF16)


def _page_bias_t(bias_ref, first_page, n_pages, reps):
    tiles = [bias_ref[jnp.where(first_page + u == n_pages - 1, 1, 0)] for u in range(PAGES_PER_STEP)]
    b = jnp.concatenate(tiles, axis=0)
    return jnp.concatenate([b] * reps, axis=1)


def _decode_step_t(k_rows, v_rows, q_ref, bias, m_ref, l_ref, acc_ref, keep=None):
    s = _dot_t(k_rows.astype(BF16), q_ref[...]) + bias
    if keep is not None:
        s = jnp.where(keep > 0.5, s, NEG_INF)
    m_old = m_ref[...]
    m_new = jnp.maximum(m_old, jnp.max(s, axis=0, keepdims=True))
    alpha = jnp.exp(m_old - m_new)
    p = jnp.exp(s - m_new)
    l_ref[...] = alpha * l_ref[...] + jnp.sum(p, axis=0, keepdims=True)
    acc_ref[...] = alpha * acc_ref[...] + _dot(v_rows.T.astype(BF16), p.astype(BF16))
    m_ref[...] = m_new


def _page_rows(refs):
    return jnp.concatenate([r[0, 0] for r in refs], axis=0)


def _diff_decode_kernel(pt_ref, lam_ref, q_ref, kn_ref, vn_ref, *rest, n_pages, lam_init):
    del pt_ref
    kp = rest[:PAGES_PER_STEP]
    vp = rest[PAGES_PER_STEP:2 * PAGES_PER_STEP]
    bias_ref, sn_ref, o_ref, qr_ref, m_ref, l_ref, acc_ref = rest[2 * PAGES_PER_STEP:]
    s = pl.program_id(1)
    t_new = q_ref.shape[1]

    @pl.when(s == 0)
    def _():
        _reset_flash(m_ref, l_ref, acc_ref)
        qr_ref[...] = _query_rows(q_ref, 2, DIFF_DH ** -0.5)
        b = bias_ref[2]
        _decode_step_t(kn_ref[0], vn_ref[0], qr_ref, jnp.concatenate([b, b], axis=1), m_ref, l_ref, acc_ref)

    _decode_step_t(_page_rows(kp), _page_rows(vp), qr_ref,
                   _page_bias_t(bias_ref, s * PAGES_PER_STEP, n_pages, 2), m_ref, l_ref, acc_ref)

    @pl.when(s == pl.num_programs(1) - 1)
    def _():
        o = (acc_ref[...] / l_ref[...]).T
        a = _diff_finish_rows(o, _diff_lambda(lam_ref, lam_init), sn_ref[...], lam_init)
        for h in range(N_HEADS):
            o_ref[0, :, h * HEAD_W:(h + 1) * HEAD_W] = a[h * t_new:(h + 1) * t_new]


PAGE_ROWS = PAGE * N_KV


def _as_page_rows(cache):
    return cache.reshape(*cache.shape[:2], PAGE_ROWS, HEAD_W)


def _page_specs(layer, n):
    def spec(u):
        return pl.BlockSpec((1, 1, PAGE_ROWS, HEAD_W),
                            lambda i, s, pt: (layer, pt[i, s * PAGES_PER_STEP + u], 0, 0))
    return [spec(u) for u in range(PAGES_PER_STEP)] * n


def diff_decode(q, k_new, v_new, cache_k, cache_v, layer, page_table, bias, lam_p, sub_norm, lam_init):
    b, t_new, _ = q.shape
    n_pages = page_table.shape[1]
    lanes = 2 * N_HEADS * t_new
    assert N_HEADS * t_new == LANE
    fixed = lambda *shape: pl.BlockSpec(shape, lambda i, s, pt: (0,) * len(shape))
    per_b = lambda *shape: pl.BlockSpec((1,) + shape, lambda i, s, pt: (i,) + (0,) * len(shape))
    grid_spec = pltpu.PrefetchScalarGridSpec(
        num_scalar_prefetch=1,
        grid=(b, n_pages // PAGES_PER_STEP),
        in_specs=[fixed(4, DIFF_DH), per_b(t_new, N_HEADS * HEAD_W), per_b(PAGE_ROWS, HEAD_W),
                  per_b(PAGE_ROWS, HEAD_W)]
        + _page_specs(layer, 2)
        + [fixed(*bias.shape), fixed(1, HEAD_W)],
        out_specs=per_b(t_new, N_HEADS * HEAD_W),
        scratch_shapes=[
            pltpu.VMEM((lanes, HEAD_W), BF16),
            pltpu.VMEM((1, lanes), F32),
            pltpu.VMEM((1, lanes), F32),
            pltpu.VMEM((HEAD_W, lanes), F32),
        ],
    )
    return pl.pallas_call(
        functools.partial(_diff_decode_kernel, n_pages=n_pages, lam_init=lam_init),
        grid_spec=grid_spec,
        out_shape=jax.ShapeDtypeStruct(q.shape[:2] + (N_HEADS * HEAD_W,), F32),
        compiler_params=_params(("parallel", "arbitrary")),
        name="diff_decode",
    )(page_table, lam_p, q, k_new, v_new, *([cache_k] * PAGES_PER_STEP), *([cache_v] * PAGES_PER_STEP),
      bias, sub_norm.reshape(1, HEAD_W))


def _pad_rows(a, n):
    return jnp.pad(a, ((0, 0), (0, n - a.shape[1]), (0, 0)))


def _group_rows_kernel(x_ref, o_ref):
    tm = x_ref.shape[0]
    for g in range(N_KV):
        o_ref[pl.ds(g, tm, stride=N_KV), :] = x_ref[:, g * HEAD_W:(g + 1) * HEAD_W]


def group_rows(x, col_block):
    m = x.shape[0]
    tm = min(m, 512)
    assert m % tm == 0
    return pl.pallas_call(
        _group_rows_kernel,
        grid=(m // tm,),
        in_specs=[pl.BlockSpec((tm, N_KV * HEAD_W), lambda i: (i, col_block))],
        out_specs=pl.BlockSpec((tm * N_KV, HEAD_W), lambda i: (i, 0)),
        out_shape=jax.ShapeDtypeStruct((m * N_KV, HEAD_W), F32),
        compiler_params=_params(("parallel",)),
        name="group_rows",
    )(x)


def _new_page_rows(rows, b):
    return _pad_rows(rows.reshape(b, -1, HEAD_W), PAGE_ROWS)


def diff_layer(xp, xs, gain, w_in, q_norm, k_norm, lam_p, sub_norm, w_out, cache_k, cache_v, layer,
               page_table, bias_p, bias_s, lam_init):
    nq, nk = N_HEADS * HEAD_W, N_KV * HEAD_W
    tile = 4 * LANE
    q_gain, k_gain = jnp.tile(q_norm, tile // DIFF_DH), jnp.tile(k_norm, tile // DIFF_DH)
    head_norm = (DIFF_DH, {**{c: q_gain for c in range(nq // tile)}, nq // tile: k_gain})
    outs = []
    for x, paged in ((xp, False), (xs, True)):
        b, t, d = x.shape
        x2 = x.reshape(b * t, d)
        qkv2 = dense(x2, w_in, layer, gain=gain, head_norm=head_norm)
        qkv = qkv2.reshape(b, t, -1)
        k_rows, v_rows = group_rows(qkv2, nq // nk), group_rows(qkv2, nq // nk + 1)
        if paged:
            o = diff_decode(qkv, _new_page_rows(k_rows, b), _new_page_rows(v_rows, b), _as_page_rows(cache_k),
                            _as_page_rows(cache_v), layer, page_table, bias_s, lam_p, sub_norm, lam_init)
        else:
            o = diff_flash(qkv, bias_p, lam_p, sub_norm, lam_init)
        y = dense(o.reshape(b * t, -1), w_out, layer, res=x2).reshape(b, t, d)
        outs.append((y, k_rows.reshape(b, t, N_KV, HEAD_W), v_rows.reshape(b, t, N_KV, HEAD_W)))
    (yp, kp, vp), (ys, ks, vs) = outs
    return yp, ys, kp, vp, ks, vs


SSM_GW = SSM_HPG * SSM_HEADDIM
SSM_BC = 2 * SSM_GROUPS * D_STATE
CONV_PAD = 8


def _conv_silu(buf_ref, w_ref, b_ref, n):
    acc = b_ref[...]
    for k in range(CONV_W):
        acc = acc + buf_ref[pl.ds(CONV_PAD - (CONV_W - 1) + k, n), :] * w_ref[k:k + 1, :]
    return acc * jax.nn.sigmoid(acc)


def _ssd_kernel(z_ref, x_ref, bc_ref, dt_ref, cbx_ref, cbbc_ref, wx_ref, wbc_ref, bx_ref, bbc_ref,
                dtb_ref, alog_ref, dsk_ref, ng_ref, e_ref, tri_ref, h0_ref,
                y_ref, hout_ref, ht_ref, xbuf_ref, bcbuf_ref, xa_ref, bca_ref, *, t_valid):
    c = pl.program_id(1)
    n = x_ref.shape[1]

    @pl.when(c == 0)
    def _():
        for g in range(SSM_GROUPS):
            ht_ref[g] = h0_ref[0, g].T
        xbuf_ref[0:CONV_PAD] = cbx_ref[0]
        bcbuf_ref[0:CONV_PAD] = cbbc_ref[0]

    xbuf_ref[CONV_PAD:CONV_PAD + n] = x_ref[0]
    bcbuf_ref[CONV_PAD:CONV_PAD + n] = bc_ref[0]
    xa_ref[...] = _conv_silu(xbuf_ref, wx_ref, bx_ref, n)
    bca_ref[...] = _conv_silu(bcbuf_ref, wbc_ref, bbc_ref, n)
    xbuf_ref[0:CONV_PAD] = xbuf_ref[n:n + CONV_PAD]
    bcbuf_ref[0:CONV_PAD] = bcbuf_ref[n:n + CONV_PAD]

    row = lax.broadcasted_iota(jnp.int32, (n, LANE), 0) + c * n
    dtr = dt_ref[0] + dtb_ref[...]
    dt = jnp.maximum(dtr, 0.0) + jnp.log1p(jnp.exp(-jnp.abs(dtr)))
    dt = jnp.where(row < t_valid, dt, 0.0)
    dta = dt * (-jnp.exp(alog_ref[...]))
    hi, mid, lo = _split3(dta)
    tri = tri_ref[...]
    cs = _dot(tri, hi) + _dot(tri, mid) + _dot(tri, lo)
    cs_last = cs[n - 1:n, :]
    cs_t = cs.T
    dt_t = dt.T
    stacked = jnp.concatenate(
        [jnp.exp(cs), jnp.exp(cs_last - cs) * dt, jnp.broadcast_to(jnp.exp(cs_last), (8, LANE))], axis=0)
    ex = _dot_exact_rhs(stacked, e_ref[...])
    causal = (lax.broadcasted_iota(jnp.int32, (n, n), 0) >= lax.broadcasted_iota(jnp.int32, (n, n), 1))

    for g in range(SSM_GROUPS):
        gs = slice(g * SSM_GW, (g + 1) * SSM_GW)
        bm = bca_ref[:, g * D_STATE:(g + 1) * D_STATE]
        cm = bca_ref[:, (SSM_GROUPS + g) * D_STATE:(SSM_GROUPS + g + 1) * D_STATE].astype(BF16)
        cb = _dot_t(cm, bm.astype(BF16))
        xg = xa_ref[:, gs]
        ys = []
        for r in range(SSM_HPG):
            h = g * SSM_HPG + r
            seg = cs[:, h:h + 1] - cs_t[h:h + 1, :]
            dec = jnp.where(causal, jnp.exp(jnp.where(causal, seg, 0.0)), 0.0)
            mm = (cb * dec * dt_t[h:h + 1, :]).astype(BF16)
            ys.append(_dot(mm, xg[:, r * SSM_HEADDIM:(r + 1) * SSM_HEADDIM].astype(BF16)))
        ht = ht_ref[g]
        y = jnp.concatenate(ys, axis=1) + _dot(cm, ht.astype(BF16)) * ex[0:n, gs]
        y = y + dsk_ref[:, gs] * xg
        zg = z_ref[0, :, gs]
        y = y * (zg * jax.nn.sigmoid(zg))
        y_ref[0, :, gs] = _rms(y, ng_ref[:, gs])
        xw = (xg * ex[n:2 * n, gs]).astype(BF16)
        ht_ref[g] = ht * ex[2 * n:2 * n + 1, gs] + _dot(bm.T.astype(BF16), xw)

    @pl.when(c == pl.num_programs(1) - 1)
    def _():
        for g in range(SSM_GROUPS):
            hout_ref[0, g] = ht_ref[g].T


def _head_expand_matrix():
    e = np.zeros((LANE, D_INNER), np.float32)
    for h in range(SSM_HEADS):
        e[h, h * SSM_HEADDIM:(h + 1) * SSM_HEADDIM] = 1.0
    return e


def ssd_core(zx, dt_raw, conv_buf, h0, conv_w, conv_b, dt_bias, a_log, d_skip, norm_g, t_valid):
    b, t, _ = zx.shape
    n = SSM_CHUNK
    pad_h = LANE - SSM_HEADS
    cb = jnp.pad(conv_buf, ((0, 0), (CONV_PAD - (CONV_W - 1), 0), (0, 0)))
    tri = jnp.asarray(np.tril(np.ones((n, n), np.float32)), BF16)
    e = jnp.asarray(_head_expand_matrix(), BF16)
    d_exp = jnp.repeat(d_skip, SSM_HEADDIM).reshape(1, D_INNER)
    fixed = lambda *shape: pl.BlockSpec(shape, lambda i, c: (0,) * len(shape))
    per_b = lambda *shape: pl.BlockSpec((1,) + shape, lambda i, c: (i,) + (0,) * len(shape))
    y, h_last = pl.pallas_call(
        functools.partial(_ssd_kernel, t_valid=t_valid),
        grid=(b, t // n),
        in_specs=[
            pl.BlockSpec((1, n, D_INNER), lambda i, c: (i, c, 0)),
            pl.BlockSpec((1, n, D_INNER), lambda i, c: (i, c, 1)),
            pl.BlockSpec((1, n, SSM_BC), lambda i, c: (i, c, 2 * D_INNER // SSM_BC)),
            pl.BlockSpec((1, n, LANE), lambda i, c: (i, c, 0)),
            per_b(CONV_PAD, D_INNER), per_b(CONV_PAD, SSM_BC),
            fixed(CONV_W, D_INNER), fixed(CONV_W, SSM_BC), fixed(1, D_INNER), fixed(1, SSM_BC),
            fixed(1, LANE), fixed(1, LANE), fixed(1, D_INNER), fixed(1, D_INNER),
            fixed(LANE, D_INNER), fixed(n, n),
            per_b(SSM_GROUPS, SSM_GW, D_STATE),
        ],
        out_specs=[
            pl.BlockSpec((1, n, D_INNER), lambda i, c: (i, c, 0)),
            per_b(SSM_GROUPS, SSM_GW, D_STATE),
        ],
        out_shape=[jax.ShapeDtypeStruct((b, t, D_INNER), F32),
                   jax.ShapeDtypeStruct((b, SSM_GROUPS, SSM_GW, D_STATE), F32)],
        scratch_shapes=[
            pltpu.VMEM((SSM_GROUPS, D_STATE, SSM_GW), F32),
            pltpu.VMEM((n + CONV_PAD, D_INNER), F32),
            pltpu.VMEM((n + CONV_PAD, SSM_BC), F32),
            pltpu.VMEM((n, D_INNER), F32),
            pltpu.VMEM((n, SSM_BC), F32),
        ],
        compiler_params=_params(("parallel", "arbitrary")),
        name="ssd_core",
    )(zx, zx, zx, dt_raw, cb[:, :, :D_INNER], cb[:, :, D_INNER:],
      conv_w[:, :D_INNER], conv_w[:, D_INNER:], conv_b[:D_INNER].reshape(1, -1), conv_b[D_INNER:].reshape(1, -1),
      jnp.pad(dt_bias, (0, pad_h)).reshape(1, LANE), jnp.pad(a_log, (0, pad_h)).reshape(1, LANE),
      d_exp, norm_g.reshape(1, D_INNER), e, tri,
      h0.reshape(b, SSM_GROUPS, SSM_GW, D_STATE))
    return y, h_last.reshape(b, SSM_HEADS, SSM_HEADDIM, D_STATE)


def ssd_layer(xp, xs, gain, w_in, layer, conv_w, conv_b, dt_bias, a_log, d_skip, norm_g, w_out,
              state_conv, state_ssm):
    nzx = D_INNER + CONV_DIM
    w_dt = jnp.pad(w_in[layer, :, nzx:], ((0, 0), (0, LANE - SSM_HEADS)))[None]
    w_in_t = jnp.swapaxes(w_in, 1, 2)
    outs = []
    for x, conv_buf, h0 in ((xp, None, None), (xs, state_conv, state_ssm)):
        b, t, d = x.shape
        x2 = x.reshape(b * t, d)
        if conv_buf is None:
            conv_buf = jnp.zeros((b, CONV_W - 1, CONV_DIM), F32)
            h0 = jnp.zeros((b, SSM_HEADS, SSM_HEADDIM, D_STATE), F32)
        zx = dense(x2, w_in_t, layer, gain=gain, n=nzx, w_t=True).reshape(b, t, -1)
        dt_raw = dense(x2, w_dt, 0, gain=gain).reshape(b, t, LANE)
        tp = -(-t // SSM_CHUNK) * SSM_CHUNK
        y, h_last = ssd_core(_pad_rows(zx, tp), _pad_rows(dt_raw, tp), conv_buf, h0, conv_w, conv_b,
                             dt_bias, a_log, d_skip, norm_g, t)
        y = y[:, :t].reshape(b * t, D_INNER)
        tail = min(t, CONV_W - 1)
        conv_out = jnp.concatenate([conv_buf[:, tail:], zx[:, t - tail:, D_INNER:]], axis=1)
        outs.append((dense(y, w_out, layer, res=x2).reshape(b, t, d), h_last, conv_out))
    (yp, hp, cp), (ys, hs, cs) = outs
    return yp, ys, hp, cp, hs, cs


NSA_KV = N_KV * HEAD_W
NSA_SCALE = HEAD_W ** -0.5
CMP_PAIRS = CMP_STRIDE // 2
CMP_PAGES = 32


def _cmp_uv_kernel(pt_ref, *refs):
    del pt_ref
    pages = refs[:-2]
    wab_ref, o_ref = refs[-2:]
    subs = PAGE // CMP_STRIDE
    for g in range(N_KV):
        acc = jnp.zeros((len(pages) * subs, 2 * HEAD_W), F32)
        for lp in range(CMP_PAIRS):
            halves = []
            for li in range(2):
                rows = pl.ds((2 * lp + li) * N_KV + g, subs, stride=CMP_STRIDE * N_KV)
                halves.append(jnp.concatenate([p[0, 0, rows, :] for p in pages], axis=0))
            acc = acc + _dot(jnp.concatenate(halves, axis=1).astype(BF16), wab_ref[lp])
        o_ref[0, :, g * 2 * HEAD_W:(g + 1) * 2 * HEAD_W] = acc


def cmp_uv(rows, layer, page_table, wab):
    b, n_pages = page_table.shape
    subs = PAGE // CMP_STRIDE
    step_pages = min(CMP_PAGES, n_pages)
    assert n_pages % step_pages == 0

    def spec(u):
        return pl.BlockSpec((1, 1, PAGE_ROWS, HEAD_W),
                            lambda i, c, pt: (layer, pt[i, c * step_pages + u], 0, 0))

    grid_spec = pltpu.PrefetchScalarGridSpec(
        num_scalar_prefetch=1,
        grid=(b, n_pages // step_pages),
        in_specs=[spec(u) for u in range(step_pages)]
        + [pl.BlockSpec(wab.shape, lambda i, c, pt: (0, 0, 0))],
        out_specs=pl.BlockSpec((1, step_pages * subs, 2 * NSA_KV), lambda i, c, pt: (i, c, 0)),
    )
    return pl.pallas_call(
        _cmp_uv_kernel,
        grid_spec=grid_spec,
        out_shape=jax.ShapeDtypeStruct((b, n_pages * subs, 2 * NSA_KV), F32),
        compiler_params=_params(("parallel", "arbitrary")),
        name="cmp_uv",
    )(page_table, *([rows] * step_pages), wab)


def _cmp_finish_kernel(uv_ref, pe_ref, w1_ref, w2_ref, kn_ref, o_ref, *, norm):
    n = uv_ref.shape[1]
    c = _dot(jnp.broadcast_to(pe_ref[...], (8, pe_ref.shape[1])).astype(BF16), w1_ref[...])[0:1]
    for g in range(N_KV):
        u = uv_ref[0, :, g * 2 * HEAD_W:g * 2 * HEAD_W + HEAD_W]
        v = uv_ref[0, :, g * 2 * HEAD_W + HEAD_W:(g + 1) * 2 * HEAD_W]
        pre = u + pltpu.roll(v, n - 1, axis=0) + c
        out = _dot((pre * jax.nn.sigmoid(pre)).astype(BF16), w2_ref[...])
        if norm:
            out = _rms(out, kn_ref[...])
        o_ref[0, :, g * HEAD_W:(g + 1) * HEAD_W] = out


def cmp_finish(uv, pe, w1, w2, k_norm):
    b, n, _ = uv.shape
    norm = k_norm is not None
    kn = (k_norm if norm else jnp.ones((HEAD_W,), F32)).reshape(1, HEAD_W)
    return pl.pallas_call(
        functools.partial(_cmp_finish_kernel, norm=norm),
        grid=(b,),
        in_specs=[
            pl.BlockSpec((1, n, 2 * NSA_KV), lambda i: (i, 0, 0)),
            pl.BlockSpec((1, pe.size), lambda i: (0, 0)),
            pl.BlockSpec(w1.shape, lambda i: (0, 0)),
            pl.BlockSpec(w2.shape, lambda i: (0, 0)),
            pl.BlockSpec((1, HEAD_W), lambda i: (0, 0)),
        ],
        out_specs=pl.BlockSpec((1, n, NSA_KV), lambda i: (i, 0, 0)),
        out_shape=jax.ShapeDtypeStruct((b, n, NSA_KV), F32),
        compiler_params=_params(("parallel",)),
        name="cmp_finish",
    )(uv, pe.reshape(1, -1), w1, w2, kn)


def compress(rows, layer, page_table, pe, w1, w2, k_norm):
    w1r = w1.reshape(2, CMP_STRIDE, HEAD_W, HEAD_W)
    wab = jnp.transpose(w1r, (1, 2, 0, 3)).reshape(CMP_PAIRS, 2 * HEAD_W, 2 * HEAD_W).astype(BF16)
    uv = cmp_uv(rows, layer, page_table, wab)
    return cmp_finish(uv, pe, w1.astype(BF16), w2.astype(BF16), k_norm)


def _importance_matrix(n_cmp, n_rows, n_slc, n_cols):
    w = np.zeros((n_rows, n_cols), np.float32)
    w_imp = [1.0] + [2.0] * (SLC_RATIO - 1) + [1.0]
    for s in range(n_slc):
        for m, wm in enumerate(w_imp):
            j = SLC_RATIO * s + m - 1
            if 0 <= j < n_cmp:
                w[j, s] += wm
    return w


def _select_blocks(s_slc, q_pos0, n_slc):
    t, w = s_slc.shape
    blk = lax.broadcasted_iota(jnp.int32, (t, w), 1)
    qpos = q_pos0 + lax.broadcasted_iota(jnp.int32, (t, w), 0)
    qb = qpos // SLC_BLOCK
    forced = (blk == 0) | (blk == qb) | (blk == qb - 1)
    score = jnp.where(forced, FORCE_SCORE, jnp.where(blk * SLC_BLOCK <= qpos, s_slc, -1.0))
    score = jnp.where(blk < n_slc, score, -2.0)
    cnt = jnp.zeros((t, w), F32)
    for sp in range(n_slc):
        col = score[:, sp:sp + 1]
        tie = jnp.where(blk > sp, 1.0, 0.0)
        cnt = cnt + jnp.where(col > score, 1.0, jnp.where(col == score, tie, 0.0))
    return jnp.where(cnt < N_SELECT, 1.0, 0.0)


def _masked_softmax(s, valid, axis=-1, base2=False):
    m = jnp.max(s, axis=axis, keepdims=True)
    e = jnp.where(valid, (jnp.exp2 if base2 else jnp.exp)(s - m), 0.0)
    return e / jnp.maximum(jnp.sum(e, axis=axis, keepdims=True), 1e-30)


def _reset_flash(m_ref, l_ref, acc_ref):
    m_ref[...] = jnp.full(m_ref.shape, NEG_INF, F32)
    l_ref[...] = jnp.zeros(l_ref.shape, F32)
    acc_ref[...] = jnp.zeros(acc_ref.shape, F32)


def _select_blocks_t(s_slc, q_pos0, n_slc):
    w, t = s_slc.shape
    n8 = -(-n_slc // 8) * 8
    blk = lax.broadcasted_iota(jnp.int32, (n8, t), 0)
    qpos = q_pos0 + lax.broadcasted_iota(jnp.int32, (n8, t), 1)
    qb = qpos // SLC_BLOCK
    forced = (blk == 0) | (blk == qb) | (blk == qb - 1)
    score = jnp.where(forced, FORCE_SCORE, jnp.where(blk * SLC_BLOCK <= qpos, s_slc[:n8], -1.0))
    score = jnp.where(blk < n_slc, score, -2.0)
    cnt = jnp.zeros((n8, t), F32)
    for sp in range(n_slc):
        row = score[sp:sp + 1, :]
        tie = jnp.where(blk > sp, 1.0, 0.0)
        cnt = cnt + jnp.where(row > score, 1.0, jnp.where(row == score, tie, 0.0))
    sel = jnp.where(cnt < N_SELECT, 1.0, 0.0)
    if n8 < w:
        sel = jnp.concatenate([sel, jnp.zeros((w - n8, t), F32)], axis=0)
    return sel


def _nsa_prompt_kernel(q_ref, gt_ref, kc_ref, vc_ref, ks_ref, vs_ref, kw_ref, vw_ref, bias_ref, u_ref, wimp_ref,
                       o_ref, qs_ref, ksb_ref, vst_ref, kwb_ref, vwt_ref, sel_ref, m_ref, l_ref, acc_ref,
                       *, tq, n_slc):
    qi = pl.program_id(2)

    @pl.when(qi == 0)
    def _():
        _stage_kv(ks_ref, vs_ref, ksb_ref, vst_ref, tq)
        _stage_kv(kw_ref, vw_ref, kwb_ref, vwt_ref, tq)

    qs_ref[...] = jnp.concatenate(
        [q_ref[0, :, r * HEAD_W:(r + 1) * HEAD_W] for r in range(HPG)], axis=0).astype(BF16)

    n_pad = kc_ref.shape[1]
    start = pl.multiple_of(n_pad - (tq // CMP_STRIDE) * (qi + 1), 8)
    scale = NSA_SCALE * LOG2E
    bias_c = u_ref[0, pl.ds(start, n_pad), :]
    s = _dot_t(kc_ref[0].astype(BF16), qs_ref[...]) * scale + bias_c
    pc = _masked_softmax(s, bias_c > 0.5 * NEG_INF, axis=0, base2=True)
    o_cmp = _dot(vc_ref[0].T.astype(BF16), pc.astype(BF16))
    psum = pc[:, 0:tq] + pc[:, tq:2 * tq] + pc[:, 2 * tq:3 * tq] + pc[:, 3 * tq:4 * tq]
    hi, mid, lo = _split3(psum)
    wimp = wimp_ref[...]
    s_slc = _dot(wimp, hi) + _dot(wimp, mid) + _dot(wimp, lo)
    sel_ref[...] = _select_blocks_t(s_slc, qi * tq, n_slc).astype(BF16)

    _reset_flash(m_ref, l_ref, acc_ref)

    far = bias_ref[2, 0:1, :]

    def key_mask(kt, width):
        key = lax.broadcasted_iota(jnp.int32, (width, LANE), 0)
        blk = lax.broadcasted_iota(jnp.int32, (width, LANE), 1)
        onehot = jnp.where(blk == kt * (width // SLC_BLOCK) + key // SLC_BLOCK, 1.0, 0.0).astype(BF16)
        return jnp.concatenate([_dot(onehot, sel_ref[...])] * HPG, axis=1)

    def slc_far_body(width):
        def body(kt, c):
            rows = pl.ds(pl.multiple_of(kt * width, width), width)
            _flash_tile_t(ksb_ref[rows, :], vst_ref[:, rows], qs_ref, m_ref, l_ref, acc_ref, scale=scale,
                          far_bias=far, mask=key_mask(kt, width))
            return c
        return body

    def slc_near_body(kt, c):
        rows = pl.ds(pl.multiple_of(kt * tq, tq), tq)
        typ = qi - kt
        _flash_tile_t(ksb_ref[rows, :], vst_ref[:, rows], qs_ref, m_ref, l_ref, acc_ref, scale=scale,
                      bias=lambda cols: bias_ref[typ, :, cols], mask=key_mask(kt, tq))
        return c

    n_far = jnp.maximum(qi - 1, 0)
    lax.fori_loop(0, n_far // FAR_TILES, slc_far_body(FAR_TILES * tq), 0)
    lax.fori_loop((n_far // FAR_TILES) * FAR_TILES, n_far, slc_far_body(tq), 0)
    lax.fori_loop(n_far, qi + 1, slc_near_body, 0)
    o_slc = acc_ref[...] / l_ref[...]

    _reset_flash(m_ref, l_ref, acc_ref)
    nw = WINDOW // tq

    def win_body(kt, c):
        rows = pl.ds(pl.multiple_of(kt * tq, tq), tq)
        t = qi - kt
        typ = jnp.where(t == nw, 3, jnp.minimum(t, 2))
        _flash_tile_t(kwb_ref[rows, :], vwt_ref[:, rows], qs_ref, m_ref, l_ref, acc_ref, scale=scale,
                      bias=lambda cols: bias_ref[typ, :, cols])
        return c

    lax.fori_loop(jnp.maximum(qi - nw, 0), qi + 1, win_body, 0)
    o_win = acc_ref[...] / l_ref[...]

    sig = jax.nn.sigmoid(gt_ref[0]).T
    for r in range(HPG):
        cs = slice(r * tq, (r + 1) * tq)
        o_t = (sig[r:r + 1, :] * o_cmp[:, cs] + sig[HPG + r:HPG + r + 1, :] * o_slc[:, cs]
               + sig[2 * HPG + r:2 * HPG + r + 1, :] * o_win[:, cs])
        o_ref[0, :, r * HEAD_W:(r + 1) * HEAD_W] = o_t.T


def _prompt_cmp_buckets(tq, n_pad):
    jp = np.arange(2 * n_pad)[:, None] - (n_pad - tq // CMP_STRIDE)
    i = np.arange(tq)[None, :]
    return _bucket_np(i - CMP_STRIDE * jp - (2 * CMP_STRIDE - 1))[None].astype(np.int32)


def nsa_prompt(proj, gates, kc, vc, bias, u_bias, tq=PROMPT_TQ):
    b, t, _ = proj.shape
    gw = HPG * HEAD_W
    n_pad = kc.shape[1]
    n_slc = t // SLC_BLOCK
    wimp = jnp.asarray(_importance_matrix(n_pad - 1, n_pad, n_slc, LANE).T, BF16)
    rows = HPG * tq
    seq = lambda n: pl.BlockSpec((1, n, HEAD_W), lambda i, g, j: (i, 0, g))
    kv = lambda a: pl.BlockSpec((1, t, HEAD_W), lambda i, g, j: (i, 0, N_HEADS + a * N_KV + g))
    return pl.pallas_call(
        functools.partial(_nsa_prompt_kernel, tq=tq, n_slc=n_slc),
        grid=(b, N_KV, t // tq),
        in_specs=[
            pl.BlockSpec((1, tq, gw), lambda i, g, j: (i, j, g)),
            pl.BlockSpec((1, tq, LANE), lambda i, g, j: (i, j, g)),
            seq(n_pad), seq(n_pad), kv(2), kv(3), kv(4), kv(5),
            pl.BlockSpec((4, tq, rows), lambda i, g, j: (0, 0, g)),
            pl.BlockSpec((1, 2 * n_pad, rows), lambda i, g, j: (0, 0, g)),
            pl.BlockSpec(wimp.shape, lambda i, g, j: (0, 0)),
        ],
        out_specs=pl.BlockSpec((1, tq, gw), lambda i, g, j: (i, j, g)),
        out_shape=jax.ShapeDtypeStruct((b, t, N_HEADS * HEAD_W), F32),
        scratch_shapes=[
            pltpu.VMEM((rows, HEAD_W), BF16),
            pltpu.VMEM((t, HEAD_W), BF16),
            pltpu.VMEM((HEAD_W, t), BF16),
            pltpu.VMEM((t, HEAD_W), BF16),
            pltpu.VMEM((HEAD_W, t), BF16),
            pltpu.VMEM((LANE, tq), BF16),
            pltpu.VMEM((1, rows), F32),
            pltpu.VMEM((1, rows), F32),
            pltpu.VMEM((HEAD_W, rows), F32),
        ],
        compiler_params=_params(("parallel", "parallel", "arbitrary")),
        name="nsa_prompt",
    )(proj, gates, kc, vc, proj, proj, proj, proj, bias, u_bias, wimp)


def _decode_cmp_buckets(t_new, past_len):
    n_pad = past_len // CMP_STRIDE
    i = np.arange(t_new)[:, None]
    j = n_pad - LANE + np.arange(LANE)[None, :]
    last = _bucket_np(past_len + i - CMP_STRIDE * j - (2 * CMP_STRIDE - 1))
    last = np.where(j < n_pad - 1, last, -1)
    return np.stack([np.full((t_new, LANE), N_BUCKETS - 1, np.int32), last]).astype(np.int32)


def _decode_win_buckets(t_new, n_tiles):
    i = np.arange(t_new)[:, None]
    idx = np.arange(n_tiles * LANE)[None, :]
    dw = WINDOW + i - idx
    ok = (dw >= 0) & (dw < WINDOW) & (idx < WINDOW + t_new)
    b = np.where(ok, _bucket_np(dw), -1)
    return np.stack([b[:, k * LANE:(k + 1) * LANE] for k in range(n_tiles)]).astype(np.int32)


def _nsa_decode_kernel(pt_ref, q_ref, gt_ref, kc_ref, vc_ref, kn_ref, vn_ref, kw_ref, vw_ref, *rest,
                       n_pages, past_len):
    del pt_ref
    kp = rest[:PAGES_PER_STEP]
    vp = rest[PAGES_PER_STEP:2 * PAGES_PER_STEP]
    (bias_ref, bias_c_ref, bias_w_ref, wimp_ref, o_ref,
     qr_ref, selt_ref, ocmp_ref, owin_ref, m_ref, l_ref, acc_ref) = rest[2 * PAGES_PER_STEP:]
    s = pl.program_id(1)
    t_new = q_ref.shape[1]
    rows = HPG * t_new
    n_slc = -(-(past_len + t_new) // SLC_BLOCK)

    @pl.when(s == 0)
    def _():
        _reset_flash(m_ref, l_ref, acc_ref)
        qr_ref[...] = _query_rows(q_ref, 1, NSA_SCALE)
        n_pad = kc_ref.shape[1]
        sel_t = jnp.zeros(selt_ref.shape, F32)
        for g in range(N_KV):
            sl = slice(g * HEAD_W, (g + 1) * HEAD_W)
            gr = slice(g * rows, (g + 1) * rows)
            qx = _stack_group_heads(q_ref, g).astype(BF16)
            bias_c = jnp.concatenate([bias_c_ref[0, gr, :]] * (n_pad // LANE - 1) + [bias_c_ref[1, gr, :]], axis=1)
            sc = _dot_t(qx, kc_ref[0, :, sl].astype(BF16)) * NSA_SCALE + bias_c
            pc = _masked_softmax(sc, bias_c > 0.5 * NEG_INF)
            ocmp_ref[g] = _dot(pc.astype(BF16), vc_ref[0, :, sl].astype(BF16))
            psum = sum(pc[r * t_new:(r + 1) * t_new] for r in range(1, HPG)) + pc[0:t_new]
            s_slc = _dot_exact_rhs(psum, wimp_ref[...])
            sel = _select_blocks(s_slc, past_len, n_slc)
            sel_pad = jnp.concatenate([sel, jnp.zeros((LANE - t_new, sel.shape[1]), F32)], axis=0).T
            tok = lax.broadcasted_iota(jnp.int32, (LANE, LANE), 0)
            lane = lax.broadcasted_iota(jnp.int32, (LANE, LANE), 1)
            spread = jnp.where((lane % t_new == tok) & (lane // rows == g), 1.0, 0.0).astype(BF16)
            sel_t = sel_t + _dot(sel_pad.astype(BF16), spread)
            n_wt = bias_w_ref.shape[0]
            bias_w = jnp.concatenate([bias_w_ref[k, gr, :] for k in range(n_wt)], axis=1)
            sw = _dot_t(qx, kw_ref[0, :, sl].astype(BF16)) * NSA_SCALE + bias_w
            pw = _masked_softmax(sw, bias_w > 0.5 * NEG_INF)
            owin_ref[g] = _dot(pw.astype(BF16), vw_ref[0, :, sl].astype(BF16))
        selt_ref[...] = sel_t
        _decode_step_t(kn_ref[0], vn_ref[0], qr_ref, bias_ref[2], m_ref, l_ref, acc_ref)

    block_rows = SLC_BLOCK * N_KV
    first_block = s * (PAGES_PER_STEP * PAGE // SLC_BLOCK)
    keep = jnp.concatenate(
        [jnp.broadcast_to(selt_ref[pl.ds(first_block + b, 1), :], (block_rows, selt_ref.shape[1]))
         for b in range(PAGES_PER_STEP * PAGE // SLC_BLOCK)], axis=0)
    _decode_step_t(_page_rows(kp), _page_rows(vp), qr_ref,
                   _page_bias_t(bias_ref, s * PAGES_PER_STEP, n_pages, 1), m_ref, l_ref, acc_ref, keep=keep)

    @pl.when(s == pl.num_programs(1) - 1)
    def _():
        sig = jax.nn.sigmoid(gt_ref[0])
        o_slc_all = (acc_ref[...] / l_ref[...]).T
        for g in range(N_KV):
            o_cmp = ocmp_ref[g]
            o_win = owin_ref[g]
            for r in range(HPG):
                h = g * HPG + r
                rs = slice(r * t_new, (r + 1) * t_new)
                c0 = g * LANE + r
                o_ref[0, :, h * HEAD_W:(h + 1) * HEAD_W] = (
                    sig[:, c0:c0 + 1] * o_cmp[rs]
                    + sig[:, c0 + HPG:c0 + HPG + 1] * o_slc_all[h * t_new:(h + 1) * t_new]
                    + sig[:, c0 + 2 * HPG:c0 + 2 * HPG + 1] * o_win[rs])


def nsa_decode(q, gates, kc, vc, k_new, v_new, kw_src, vw_src, cache_k, cache_v, layer, page_table,
               bias, bias_c, bias_w, past_len):
    b, t_new, _ = q.shape
    n_pages = page_table.shape[1]
    rows = HPG * t_new
    lanes = N_HEADS * t_new
    assert lanes == LANE
    n_pad = kc.shape[1]
    n_slc = -(-(past_len + t_new) // SLC_BLOCK)
    n_cols = -(-n_slc // LANE) * LANE
    wimp = jnp.asarray(_importance_matrix(n_pad - 1, n_pad, n_slc, n_cols), BF16)
    fixed = lambda *shape: pl.BlockSpec(shape, lambda i, s, pt: (0,) * len(shape))
    per_b = lambda *shape: pl.BlockSpec((1,) + shape, lambda i, s, pt: (i,) + (0,) * len(shape))
    grid_spec = pltpu.PrefetchScalarGridSpec(
        num_scalar_prefetch=1,
        grid=(b, n_pages // PAGES_PER_STEP),
        in_specs=[per_b(t_new, N_HEADS * HEAD_W), per_b(t_new, N_KV * LANE),
                  per_b(n_pad, NSA_KV), per_b(n_pad, NSA_KV), per_b(PAGE_ROWS, HEAD_W), per_b(PAGE_ROWS, HEAD_W),
                  per_b(kw_src.shape[1], NSA_KV), per_b(kw_src.shape[1], NSA_KV)]
        + _page_specs(layer, 2)
        + [fixed(*bias.shape), fixed(*bias_c.shape), fixed(*bias_w.shape), fixed(*wimp.shape)],
        out_specs=per_b(t_new, N_HEADS * HEAD_W),
        scratch_shapes=[
            pltpu.VMEM((lanes, HEAD_W), BF16),
            pltpu.VMEM((n_cols, lanes), F32),
            pltpu.VMEM((N_KV, rows, HEAD_W), F32),
            pltpu.VMEM((N_KV, rows, HEAD_W), F32),
            pltpu.VMEM((1, lanes), F32),
            pltpu.VMEM((1, lanes), F32),
            pltpu.VMEM((HEAD_W, lanes), F32),
        ],
    )
    return pl.pallas_call(
        functools.partial(_nsa_decode_kernel, n_pages=n_pages, past_len=past_len),
        grid_spec=grid_spec,
        out_shape=jax.ShapeDtypeStruct(q.shape[:2] + (N_HEADS * HEAD_W,), F32),
        compiler_params=_params(("parallel", "arbitrary")),
        name="nsa_decode",
    )(page_table, q, gates, kc, vc, k_new, v_new, kw_src, vw_src,
      *([cache_k] * PAGES_PER_STEP), *([cache_v] * PAGES_PER_STEP), bias, bias_c, bias_w, wimp)


def _gate_weights(w_g):
    d = w_g.shape[0]
    w = jnp.transpose(w_g.reshape(d, 3, N_KV, HPG), (0, 2, 1, 3)).reshape(d, N_KV, 3 * HPG)
    return jnp.pad(w, ((0, 0), (0, 0), (0, LANE - 3 * HPG))).reshape(d, N_KV * LANE)


def nsa_layer(xp, xs, gain, w_in, q_norm, k_norm, pe, w1, w2, w_out, caches, layer, page_table,
              bias_p, u_bias, bias_s, bias_sc, bias_sw, past_len):
    cmp_k, cmp_v, slc_k, slc_v, win_k, win_v = caches
    nqkv = N_HEADS * HEAD_W + 6 * NSA_KV
    w_g = _gate_weights(w_in[layer, :, nqkv:])[None]
    w_in_t = jnp.swapaxes(w_in, 1, 2)
    nq = N_HEADS * HEAD_W
    tile = 4 * LANE
    q_gain = jnp.tile(q_norm, tile // HEAD_W)
    head_norm = (HEAD_W, {**{c: q_gain for c in range(nq // tile)},
                          nq // tile + 2: jnp.tile(k_norm[1], tile // HEAD_W),
                          nq // tile + 4: jnp.tile(k_norm[2], tile // HEAD_W)})
    outs = []
    for x, paged in ((xp, False), (xs, True)):
        b, t, d = x.shape
        x2 = x.reshape(b * t, d)
        proj2 = dense(x2, w_in_t, layer, gain=gain, n=nqkv, w_t=True, head_norm=head_norm)
        proj = proj2.reshape(b, t, -1)
        gates = dense(x2, w_g, 0, gain=gain).reshape(b, t, -1)
        rows = [group_rows(proj2, nq // NSA_KV + a) for a in range(4)]
        as_state = lambda r: r.reshape(b, t, N_KV, HEAD_W)
        kw, vw = proj[:, :, nq + 4 * NSA_KV:nq + 5 * NSA_KV], proj[:, :, nq + 5 * NSA_KV:]
        if paged:
            kc = compress(_as_page_rows(cmp_k), layer, page_table, pe[0], w1[0], w2[0], k_norm[0])
            vc = compress(_as_page_rows(cmp_v), layer, page_table, pe[1], w1[1], w2[1], None)
            n_wt = bias_sw.shape[0]
            kw_src = jnp.concatenate([win_k[layer], kw], axis=1)
            vw_src = jnp.concatenate([win_v[layer], vw], axis=1)
            o = nsa_decode(proj, gates, kc, vc, _new_page_rows(rows[2], b), _new_page_rows(rows[3], b),
                           _pad_rows(kw_src, n_wt * LANE), _pad_rows(vw_src, n_wt * LANE),
                           _as_page_rows(slc_k), _as_page_rows(slc_v), layer, page_table,
                           bias_s, bias_sc, bias_sw, past_len)
            kw_win, vw_win = kw_src, vw_src
        else:
            n_pg = t // PAGE
            ident = jnp.arange(b * n_pg, dtype=jnp.int32).reshape(b, n_pg)
            as_pages = lambda r: r.reshape(1, b * n_pg, PAGE_ROWS, HEAD_W)
            kc = compress(as_pages(rows[0]), 0, ident, pe[0], w1[0], w2[0], k_norm[0])
            vc = compress(as_pages(rows[1]), 0, ident, pe[1], w1[1], w2[1], None)
            o = nsa_prompt(proj, gates, kc, vc, bias_p, u_bias)
            kw_win, vw_win = kw, vw
        y = dense(o.reshape(b * t, -1), w_out, layer, res=x2).reshape(b, t, d)
        win = tuple(a[:, -WINDOW:].reshape(b, WINDOW, N_KV, HEAD_W) for a in (kw_win, vw_win))
        st = tuple(as_state(r) for r in rows) + win
        outs.append((y, st))
    (yp, stp), (ys, sts) = outs
    return yp, ys, stp, sts


def kernel(x_prompt, x_sample, cache_diff_k, cache_diff_v, state_ssm, state_conv, cache_nsa_cmp_k, cache_nsa_cmp_v, cache_nsa_slc_k, cache_nsa_slc_v, cache_nsa_win_k, cache_nsa_win_v, cache_mem_k, cache_mem_v, page_table, mem_prompt, rel_bias_table, norm_mix, norm_xattn, norm_mem, norm_ffn, diff_w_in, diff_q_norm, diff_k_norm, diff_lambda, diff_sub_norm, diff_w_out, ssm_w_in, ssm_conv_w, ssm_conv_b, ssm_dt_bias, ssm_a_log, ssm_d, ssm_norm, ssm_w_out, nsa_w_in, nsa_q_norm, nsa_k_norm, nsa_cmp_pe, nsa_cmp_w1, nsa_cmp_w2, nsa_w_out, xattn_w_q, xattn_w_k, xattn_w_v, xattn_q_norm, xattn_k_norm, xattn_w_o, ffn_w1, ffn_w3, ffn_w2):
    xp, xs = x_prompt, x_sample
    bp, t, d = xp.shape
    bs, t_new, _ = xs.shape
    past_len = page_table.shape[1] * PAGE
    depth = norm_mix.shape[0]

    bias_p = bias_tiles(_prompt_attn_buckets(PROMPT_TQ), rel_bias_table, heads_on_lanes=True, scale=LOG2E)
    u_bias = bias_tiles(_prompt_cmp_buckets(PROMPT_TQ, t // CMP_STRIDE), rel_bias_table, heads_on_lanes=True,
                        scale=LOG2E)
    bias_s = bias_tiles_indexed(_decode_page_index(t_new, past_len), rel_bias_table)
    bias_sc = bias_tiles(_decode_cmp_buckets(t_new, past_len), rel_bias_table)
    n_wt = -(-(WINDOW + t_new) // LANE)
    bias_sw = bias_tiles(_decode_win_buckets(t_new, n_wt), rel_bias_table)

    mem_k, mem_v = mem_kv(mem_prompt, norm_mem, xattn_w_k, xattn_w_v, xattn_k_norm)
    xw = X_HEADS * X_DH
    win_k = cache_nsa_win_k.reshape(*cache_nsa_win_k.shape[:3], NSA_KV)
    win_v = cache_nsa_win_v.reshape(*cache_nsa_win_v.shape[:3], NSA_KV)

    dkp, dvp, dks, dvs = [], [], [], []
    ssp, cvp, sss, cvs = [], [], [], []
    nsp, nss = [], []
    for i in range(depth):
        kind, j = i % 3, i // 3
        if kind == 0:
            lam_init = 0.8 - 0.6 * math.exp(-0.3 * i)
            xp, xs, kp_, vp_, ks_, vs_ = diff_layer(
                xp, xs, norm_mix[i], diff_w_in, diff_q_norm[j], diff_k_norm[j], diff_lambda[j],
                diff_sub_norm[j], diff_w_out, cache_diff_k, cache_diff_v, j, page_table,
                bias_p, bias_s, lam_init)
            dkp.append(kp_)
            dvp.append(vp_)
            dks.append(ks_)
            dvs.append(vs_)
        elif kind == 1:
            xp, xs, hp_, cp_, hs_, cs_ = ssd_layer(
                xp, xs, norm_mix[i], ssm_w_in, j, ssm_conv_w[j], ssm_conv_b[j], ssm_dt_bias[j], ssm_a_log[j],
                ssm_d[j], ssm_norm[j], ssm_w_out, state_conv[j], state_ssm[j])
            ssp.append(hp_)
            cvp.append(cp_)
            sss.append(hs_)
            cvs.append(cs_)
        else:
            caches = (cache_nsa_cmp_k, cache_nsa_cmp_v, cache_nsa_slc_k, cache_nsa_slc_v, win_k, win_v)
            xp, xs, stp, sts = nsa_layer(
                xp, xs, norm_mix[i], nsa_w_in, nsa_q_norm[j], nsa_k_norm[j], nsa_cmp_pe[j], nsa_cmp_w1[j],
                nsa_cmp_w2[j], nsa_w_out, caches, j, page_table,
                bias_p, u_bias, bias_s, bias_sc, bias_sw, past_len)
            nsp.append(stp)
            nss.append(sts)
        xp = xattn(xp, norm_xattn[i], xattn_w_q, xattn_q_norm[i], mem_k[i], mem_v[i], xattn_w_o, i)
        xs = xattn(xs, norm_xattn[i], xattn_w_q, xattn_q_norm[i], cache_mem_k[i].reshape(bs, N_MEM, xw),
                   cache_mem_v[i].reshape(bs, N_MEM, xw), xattn_w_o, i)
        xp = ffn(xp.reshape(bp * t, d), norm_ffn[i], ffn_w1, ffn_w3, ffn_w2, i).reshape(bp, t, d)
        xs = ffn(xs.reshape(bs * t_new, d), norm_ffn[i], ffn_w1, ffn_w3, ffn_w2, i).reshape(bs, t_new, d)

    st = lambda xs_: jnp.stack(xs_, axis=0)
    nsp_t = [st([s[a] for s in nsp]) for a in range(6)]
    nss_t = [st([s[a] for s in nss]) for a in range(6)]
    mem_shape = (depth, bp, N_MEM, X_HEADS, X_DH)
    return (xp, xs,
            st(dkp), st(dvp), st(dks), st(dvs),
            st(ssp), st(cvp), st(sss), st(cvs),
            *nsp_t, *nss_t,
            mem_k.reshape(mem_shape), mem_v.reshape(mem_shape))
```

```python
import functools
import math

import numpy as np
import jax
import jax.numpy as jnp
from jax import lax
from jax.experimental import pallas as pl
from jax.experimental.pallas import tpu as pltpu

F32 = jnp.float32
BF16 = jnp.bfloat16

D_MODEL = 2048
DEPTH = 4
PAGE = 128
N_HEADS = 16
N_KV = 4
HPG = N_HEADS // N_KV
DIFF_DH = 64
HEAD_W = 128
N_BUCKETS = 32
MAX_EXACT = 16
MAX_DIST = 128
D_INNER = 2 * D_MODEL
SSM_HEADDIM = 64
SSM_HEADS = D_INNER // SSM_HEADDIM
SSM_GROUPS = 8
SSM_HPG = SSM_HEADS // SSM_GROUPS
D_STATE = 128
CONV_W = 4
CONV_DIM = D_INNER + 2 * SSM_GROUPS * D_STATE
SSM_CHUNK = 128
CMP_STRIDE = 16
SLC_BLOCK = 64
SLC_RATIO = SLC_BLOCK // CMP_STRIDE
N_SELECT = 16
WINDOW = 512
N_MEM = 256
X_HEADS = 4
X_DH = 128
EPS = 1e-6
NEG_INF = -1e30
FORCE_SCORE = 1e4

LANE = 128
VMEM_LIMIT = 56 * 1024 * 1024
DENSE_TALL_ROWS = 2048


def _params(sem):
    return pltpu.CompilerParams(dimension_semantics=sem, vmem_limit_bytes=VMEM_LIMIT)


def _rms(x, gain):
    return x * lax.rsqrt(jnp.mean(x * x, axis=-1, keepdims=True) + EPS) * gain


def _dot(a, b):
    return jnp.dot(a, b, preferred_element_type=F32)


def _dot_t(a, b):
    return lax.dot_general(a, b, (((1,), (1,)), ((), ())), preferred_element_type=F32)


def _split3(x):
    hi = x.astype(BF16)
    r1 = x - hi.astype(F32)
    mid = r1.astype(BF16)
    lo = (r1 - mid.astype(F32)).astype(BF16)
    return hi, mid, lo


def _dot_exact_rhs(x, m_bf16):
    hi, mid, lo = _split3(x)
    return _dot(hi, m_bf16) + _dot(mid, m_bf16) + _dot(lo, m_bf16)


def _norm64(blk, gain, lo):
    sq = blk * blk
    s_lo = jnp.sum(jnp.where(lo, sq, 0.0), axis=-1, keepdims=True)
    s_hi = jnp.sum(jnp.where(lo, 0.0, sq), axis=-1, keepdims=True)
    ms = jnp.where(lo, s_lo, s_hi) * (1.0 / DIFF_DH)
    return blk * lax.rsqrt(ms + EPS) * gain


def _head_norm_tile(y, gain, seg):
    lo = lax.broadcasted_iota(jnp.int32, (y.shape[0], LANE), 1) < DIFF_DH
    blocks = []
    for c in range(y.shape[1] // LANE):
        sl = slice(c * LANE, (c + 1) * LANE)
        blocks.append(_norm64(y[:, sl], gain[:, sl], lo) if seg == DIFF_DH else _rms(y[:, sl], gain[:, sl]))
    return jnp.concatenate(blocks, axis=1)


def _dense_kernel(*refs, norm, residual, w_t, head_seg, norm_tiles):
    refs = list(refs)
    x_ref = refs.pop(0)
    g_ref = refs.pop(0) if norm else None
    w_ref = refs.pop(0)
    r_ref = refs.pop(0) if residual else None
    hg_ref = refs.pop(0) if head_seg else None
    o_ref, xb_ref = refs
    j = pl.program_id(1)

    @pl.when(j == 0)
    def _():
        x = x_ref[...]
        if norm:
            x = _rms(x, g_ref[...])
        xb_ref[...] = x.astype(BF16)

    w = w_ref[0].astype(BF16)
    y = _dot_t(xb_ref[...], w) if w_t else _dot(xb_ref[...], w)
    if residual:
        y = y + r_ref[...]
    if not head_seg:
        o_ref[...] = y
        return
    is_norm = functools.reduce(jnp.logical_or, [j == t for t in norm_tiles])

    @pl.when(is_norm)
    def _():
        o_ref[...] = _head_norm_tile(y, hg_ref[0], head_seg)

    @pl.when(jnp.logical_not(is_norm))
    def _():
        o_ref[...] = y


def dense(x, w, layer, gain=None, res=None, n=None, w_t=False, head_norm=None):
    m, k = x.shape
    n = w.shape[1 if w_t else 2] if n is None else n
    tm = m if m <= 512 else (DENSE_TALL_ROWS if k <= D_MODEL and m % DENSE_TALL_ROWS == 0 else 512)
    tn = LANE if n % 256 else (512 if n % 512 == 0 else 256)
    assert m % tm == 0 and n % tn == 0
    norm, residual = gain is not None, res is not None
    x_mode = dict(pipeline_mode=pl.Buffered(1)) if tm == DENSE_TALL_ROWS else {}
    in_specs = [pl.BlockSpec((tm, k), lambda i, j: (i, 0), **x_mode)]
    args = [x]
    if norm:
        in_specs.append(pl.BlockSpec((1, k), lambda i, j: (0, 0)))
        args.append(gain.reshape(1, k))
    if w_t:
        in_specs.append(pl.BlockSpec((1, tn, k), lambda i, j: (layer, j, 0)))
    else:
        in_specs.append(pl.BlockSpec((1, k, tn), lambda i, j: (layer, 0, j)))
    args.append(w)
    if residual:
        in_specs.append(pl.BlockSpec((tm, tn), lambda i, j: (i, j)))
        args.append(res)
    head_seg, norm_tiles = 0, ()
    if head_norm is not None:
        head_seg, tile_gains = head_norm
        norm_tiles = tuple(sorted(tile_gains))
        ones = jnp.ones((tn,), F32)
        gains = jnp.stack([tile_gains.get(t, ones) for t in range(n // tn)]).reshape(n // tn, 1, tn)
        in_specs.append(pl.BlockSpec((1, 1, tn), lambda i, j: (j, 0, 0)))
        args.append(gains)
    return pl.pallas_call(
        functools.partial(_dense_kernel, norm=norm, residual=residual, w_t=w_t, head_seg=head_seg,
                          norm_tiles=norm_tiles),
        grid=(m // tm, n // tn),
        in_specs=in_specs,
        out_specs=pl.BlockSpec((tm, tn), lambda i, j: (i, j)),
        out_shape=jax.ShapeDtypeStruct((m, n), F32),
        scratch_shapes=[pltpu.VMEM((tm, k), BF16)],
        compiler_params=_params(("parallel", "arbitrary")),
        name="dense",
    )(*args)


def _ffn_kernel(x_ref, g_ref, w1_ref, w3_ref, w2_ref, o_ref, xb_ref):
    @pl.when(pl.program_id(1) == 0)
    def _():
        x = x_ref[...]
        xb_ref[...] = _rms(x, g_ref[...]).astype(BF16)
        o_ref[...] = x

    xb = xb_ref[...]
    h1 = _dot(xb, w1_ref[0].astype(BF16))
    h3 = _dot(xb, w3_ref[0].astype(BF16))
    a = (h1 * jax.nn.sigmoid(h1) * h3).astype(BF16)
    o_ref[...] += _dot(a, w2_ref[0].astype(BF16))


def ffn(x, gain, w1, w3, w2, layer):
    m, d = x.shape
    f = w1.shape[2]
    tm = min(m, 1024)
    tf = 256
    assert m % tm == 0 and f % tf == 0
    return pl.pallas_call(
        _ffn_kernel,
        grid=(m // tm, f // tf),
        in_specs=[
            pl.BlockSpec((tm, d), lambda i, j: (i, 0), pipeline_mode=pl.Buffered(1)),
            pl.BlockSpec((1, d), lambda i, j: (0, 0)),
            pl.BlockSpec((1, d, tf), lambda i, j: (layer, 0, j)),
            pl.BlockSpec((1, d, tf), lambda i, j: (layer, 0, j)),
            pl.BlockSpec((1, tf, d), lambda i, j: (layer, j, 0)),
        ],
        out_specs=pl.BlockSpec((tm, d), lambda i, j: (i, 0)),
        out_shape=jax.ShapeDtypeStruct((m, d), F32),
        scratch_shapes=[pltpu.VMEM((tm, d), BF16)],
        compiler_params=_params(("parallel", "arbitrary")),
        name="ffn",
    )(x, gain.reshape(1, d), w1, w3, w2)


def _mem_kv_kernel(mem_ref, g_ref, wk_ref, wv_ref, kn_ref, k_ref, v_ref):
    m = _rms(mem_ref[0], g_ref[0]).astype(BF16)
    k = _dot(m, wk_ref[0].astype(BF16))
    v_ref[0, 0] = _dot(m, wv_ref[0].astype(BF16))
    for h in range(X_HEADS):
        sl = slice(h * X_DH, (h + 1) * X_DH)
        k_ref[0, 0, :, sl] = _rms(k[:, sl], kn_ref[0])


def mem_kv(mem, g_mem, wk, wv, k_norm):
    b = mem.shape[0]
    nl = wk.shape[0]
    hw = X_HEADS * X_DH
    shape = jax.ShapeDtypeStruct((nl, b, N_MEM, hw), F32)
    return pl.pallas_call(
        _mem_kv_kernel,
        grid=(nl, b),
        in_specs=[
            pl.BlockSpec((1, N_MEM, D_MODEL), lambda l, i: (i, 0, 0)),
            pl.BlockSpec((1, 1, D_MODEL), lambda l, i: (l, 0, 0)),
            pl.BlockSpec((1, D_MODEL, hw), lambda l, i: (l, 0, 0)),
            pl.BlockSpec((1, D_MODEL, hw), lambda l, i: (l, 0, 0)),
            pl.BlockSpec((1, 1, X_DH), lambda l, i: (l, 0, 0)),
        ],
        out_specs=[pl.BlockSpec((1, 1, N_MEM, hw), lambda l, i: (l, i, 0, 0))] * 2,
        out_shape=[shape, shape],
        compiler_params=_params(("parallel", "parallel")),
        name="mem_kv",
    )(mem, g_mem.reshape(nl, 1, D_MODEL), wk, wv, k_norm.reshape(nl, 1, X_DH))


def _xattn_kernel(x_ref, g_ref, wq_ref, qn_ref, k_ref, v_ref, wo_ref, o_ref):
    nb, tm, d = x_ref.shape
    x = x_ref[...].reshape(nb * tm, d)
    q = _dot(_rms(x, g_ref[...]).astype(BF16), wq_ref[0].astype(BF16))
    rows = []
    for bb in range(nb):
        outs = []
        for h in range(X_HEADS):
            sl = slice(h * X_DH, (h + 1) * X_DH)
            qh = _rms(q[bb * tm:(bb + 1) * tm, sl], qn_ref[...]).astype(BF16)
            s = _dot_t(qh, k_ref[bb, :, sl].astype(BF16)) * (X_DH ** -0.5)
            e = jnp.exp(s - jnp.max(s, axis=-1, keepdims=True))
            p = e / jnp.sum(e, axis=-1, keepdims=True)
            outs.append(_dot(p.astype(BF16), v_ref[bb, :, sl].astype(BF16)))
        rows.append(jnp.concatenate(outs, axis=1))
    o = jnp.concatenate(rows, axis=0).astype(BF16)
    o_ref[...] = (x + _dot(o, wo_ref[0].astype(BF16))).reshape(nb, tm, d)


def xattn(x, gain, wq, q_norm, k, v, wo, layer):
    b, t, d = x.shape
    hw = X_HEADS * X_DH
    tm = min(t, 512)
    nb = b if t < LANE else 1
    return pl.pallas_call(
        _xattn_kernel,
        grid=(b // nb, t // tm),
        in_specs=[
            pl.BlockSpec((nb, tm, d), lambda i, j: (i, j, 0)),
            pl.BlockSpec((1, d), lambda i, j: (0, 0)),
            pl.BlockSpec((1, d, hw), lambda i, j: (layer, 0, 0)),
            pl.BlockSpec((1, X_DH), lambda i, j: (0, 0)),
            pl.BlockSpec((nb, N_MEM, hw), lambda i, j: (i, 0, 0)),
            pl.BlockSpec((nb, N_MEM, hw), lambda i, j: (i, 0, 0)),
            pl.BlockSpec((1, hw, d), lambda i, j: (layer, 0, 0)),
        ],
        out_specs=pl.BlockSpec((nb, tm, d), lambda i, j: (i, j, 0)),
        out_shape=jax.ShapeDtypeStruct((b, t, d), F32),
        compiler_params=_params(("parallel", "parallel")),
        name="xattn",
    )(x, gain.reshape(1, d), wq, q_norm.reshape(1, X_DH), k, v, wo)


def _bucket_np(dist):
    n = np.maximum(dist, 0)
    nf = np.maximum(n, 1).astype(np.float64)
    large = MAX_EXACT + (np.log(nf / MAX_EXACT) / math.log(MAX_DIST / MAX_EXACT)
                         * (N_BUCKETS - MAX_EXACT)).astype(np.int64)
    b = np.where(n < MAX_EXACT, n, np.minimum(large, N_BUCKETS - 1))
    return np.where(dist < 0, -1, b).astype(np.int32)


def _bias_kernel(bkt_ref, tab_ref, o_ref, *, scale, present):
    t, h = pl.program_id(0), pl.program_id(1)
    for tt, buckets in enumerate(present):
        @pl.when(t == tt)
        def _():
            b = bkt_ref[0]
            acc = jnp.full(b.shape, NEG_INF, F32)
            for k in buckets:
                acc = jnp.where(b == k, tab_ref[k, h] * scale, acc)
            o_ref[0] = acc


def _bias_packed_kernel(idx_ref, tab_ref, o_ref, *, scale):
    idx = idx_ref[0]
    acc = jnp.full(idx.shape, NEG_INF, F32)
    for k in range(N_BUCKETS):
        for h in range(N_HEADS):
            acc = jnp.where(idx == k * N_HEADS + h, tab_ref[k, h] * scale, acc)
    o_ref[0] = acc


def bias_tiles_indexed(idx, table, scale=1.0):
    nt, r, w = idx.shape
    return pl.pallas_call(
        functools.partial(_bias_packed_kernel, scale=scale),
        grid=(nt,),
        in_specs=[
            pl.BlockSpec((1, r, w), lambda t: (t, 0, 0)),
            pl.BlockSpec(memory_space=pltpu.SMEM),
        ],
        out_specs=pl.BlockSpec((1, r, w), lambda t: (t, 0, 0)),
        out_shape=jax.ShapeDtypeStruct((nt, r, w), F32),
        compiler_params=_params(("parallel",)),
        name="bias_tiles_indexed",
    )(jnp.asarray(idx.astype(np.int32)), table)


def bias_tiles(buckets, table, heads_on_lanes=False, scale=1.0):
    nt, r, w = buckets.shape
    if heads_on_lanes:
        out_spec = pl.BlockSpec((1, r, w), lambda t, h: (t, 0, h))
        out_shape = (nt, r, N_HEADS * w)
    else:
        out_spec = pl.BlockSpec((1, r, w), lambda t, h: (t, h, 0))
        out_shape = (nt, N_HEADS * r, w)
    return pl.pallas_call(
        functools.partial(_bias_kernel, scale=scale,
                          present=tuple(tuple(int(k) for k in np.unique(tile) if k >= 0) for tile in buckets)),
        grid=(nt, N_HEADS),
        in_specs=[
            pl.BlockSpec((1, r, w), lambda t, h: (t, 0, 0)),
            pl.BlockSpec(memory_space=pltpu.SMEM),
        ],
        out_specs=out_spec,
        out_shape=jax.ShapeDtypeStruct(out_shape, F32),
        compiler_params=_params(("parallel", "parallel")),
        name="bias_tiles",
    )(jnp.asarray(buckets), table)


def _prompt_attn_buckets(tq):
    assert WINDOW % tq == 0 and tq >= MAX_DIST
    j = np.arange(tq)[:, None]
    i = np.arange(tq)[None, :]
    far = np.full((tq, tq), N_BUCKETS - 1, np.int32)
    return np.stack([
        _bucket_np(i - j),
        _bucket_np(i - j + tq),
        far,
        np.where(j > i, far, -1),
    ]).astype(np.int32)


def _diff_lambda(lam_ref, lam_init):
    lf = lam_ref[...]
    s01 = jnp.sum(lf[0:1] * lf[1:2], axis=-1, keepdims=True)
    s23 = jnp.sum(lf[2:3] * lf[3:4], axis=-1, keepdims=True)
    return jnp.exp(s01) - jnp.exp(s23) + lam_init


def _split_maps(qs):
    lo = lax.broadcasted_iota(jnp.int32, qs.shape, 1) < DIFF_DH
    return jnp.concatenate([jnp.where(lo, qs, 0.0), jnp.where(lo, 0.0, qs)], axis=0)


def _diff_finish_rows(o, lam, sub_norm, lam_init):
    r = o.shape[0] // 2
    a = o[:r] - lam * o[r:]
    return _rms(a, sub_norm) * (1.0 - lam_init)


PROMPT_TQ = 256


LOG2E = math.log2(math.e)
FLASH_CHUNKS = 4
FAR_TILES = 2


def _flash_tile_t(k_tile, vt_tile, qs_ref, m_ref, l_ref, acc_ref, scale=None, bias=None, far_bias=None,
                  mask=None, n_chunks=FLASH_CHUNKS):
    chunk = qs_ref.shape[0] // n_chunks
    logits = [_dot_t(k_tile, qs_ref[c * chunk:(c + 1) * chunk, :]) for c in range(n_chunks)]
    for c in range(n_chunks):
        cols = slice(c * chunk, (c + 1) * chunk)
        s = logits[c]
        if scale is not None:
            s = s * scale
        if bias is not None:
            s = s + bias(cols)
        if mask is not None:
            s = jnp.where(mask[:, cols] > 0.5, s, NEG_INF)
        m_old = m_ref[:, cols]
        s_max = jnp.max(s, axis=0, keepdims=True)
        if far_bias is None:
            m_new = jnp.maximum(m_old, s_max)
            p = jnp.exp2(s - m_new)
        else:
            fb = far_bias[:, cols]
            m_new = jnp.maximum(m_old, s_max + fb)
            p = jnp.exp2(s - (m_new - fb))
        alpha = jnp.exp2(m_old - m_new)
        l_ref[:, cols] = alpha * l_ref[:, cols] + jnp.sum(p, axis=0, keepdims=True)
        acc_ref[:, cols] = alpha * acc_ref[:, cols] + _dot(vt_tile, p.astype(BF16))
        m_ref[:, cols] = m_new


def _stage_kv(k_ref, v_ref, kb_ref, vt_ref, chunk):
    for c in range(k_ref.shape[1] // chunk):
        rows = slice(c * chunk, (c + 1) * chunk)
        kb_ref[rows, :] = k_ref[0, rows, :].astype(BF16)
        vt_ref[:, rows] = v_ref[0, rows, :].T.astype(BF16)


def _diff_flash_kernel(lam_ref, q_ref, k_ref, v_ref, bias_ref, sn_ref, o_ref,
                       qs_ref, kb_ref, vt_ref, m_ref, l_ref, acc_ref, *, tq, lam_init):
    qi = pl.program_id(2)

    @pl.when(qi == 0)
    def _():
        _stage_kv(k_ref, v_ref, kb_ref, vt_ref, tq)

    qb = q_ref[0]
    qs = jnp.concatenate([qb[:, r * HEAD_W:(r + 1) * HEAD_W] for r in range(HPG)], axis=0)
    qs_ref[...] = _split_maps(qs * (DIFF_DH ** -0.5 * LOG2E)).astype(BF16)
    _reset_flash(m_ref, l_ref, acc_ref)
    far = bias_ref[2, 0:1, :]
    far = jnp.concatenate([far, far], axis=1)

    def far_body(width):
        def body(kt, c):
            rows = pl.ds(pl.multiple_of(kt * width, width), width)
            _flash_tile_t(kb_ref[rows, :], vt_ref[:, rows], qs_ref, m_ref, l_ref, acc_ref, far_bias=far)
            return c
        return body

    def near_body(kt, c):
        rows = pl.ds(pl.multiple_of(kt * tq, tq), tq)
        b = bias_ref[qi - kt]
        _flash_tile_t(kb_ref[rows, :], vt_ref[:, rows], qs_ref, m_ref, l_ref, acc_ref,
                      bias=lambda cols: jnp.concatenate([b, b], axis=1), n_chunks=1)
        return c

    n_far = jnp.maximum(qi - 1, 0)
    lax.fori_loop(0, n_far // FAR_TILES, far_body(FAR_TILES * tq), 0)
    lax.fori_loop((n_far // FAR_TILES) * FAR_TILES, n_far, far_body(tq), 0)
    lax.fori_loop(n_far, qi + 1, near_body, 0)
    o = (acc_ref[...] / l_ref[...]).T
    a = _diff_finish_rows(o, _diff_lambda(lam_ref, lam_init), sn_ref[...], lam_init)
    for r in range(HPG):
        o_ref[0, :, r * HEAD_W:(r + 1) * HEAD_W] = a[r * tq:(r + 1) * tq]


def diff_flash(qkv, bias, lam_p, sub_norm, lam_init, tq=PROMPT_TQ):
    b, t, _ = qkv.shape
    gw = HPG * HEAD_W
    rows = 2 * HPG * tq
    k0 = N_HEADS
    return pl.pallas_call(
        functools.partial(_diff_flash_kernel, tq=tq, lam_init=lam_init),
        grid=(b, N_KV, t // tq),
        in_specs=[
            pl.BlockSpec((4, DIFF_DH), lambda i, g, j: (0, 0)),
            pl.BlockSpec((1, tq, gw), lambda i, g, j: (i, j, g)),
            pl.BlockSpec((1, t, HEAD_W), lambda i, g, j: (i, 0, k0 + g)),
            pl.BlockSpec((1, t, HEAD_W), lambda i, g, j: (i, 0, k0 + N_KV + g)),
            pl.BlockSpec((4, tq, HPG * tq), lambda i, g, j: (0, 0, g)),
            pl.BlockSpec((1, HEAD_W), lambda i, g, j: (0, 0)),
        ],
        out_specs=pl.BlockSpec((1, tq, gw), lambda i, g, j: (i, j, g)),
        out_shape=jax.ShapeDtypeStruct((b, t, N_HEADS * HEAD_W), F32),
        scratch_shapes=[
            pltpu.VMEM((rows, HEAD_W), BF16),
            pltpu.VMEM((t, HEAD_W), BF16),
            pltpu.VMEM((HEAD_W, t), BF16),
            pltpu.VMEM((1, rows), F32),
            pltpu.VMEM((1, rows), F32),
            pltpu.VMEM((HEAD_W, rows), F32),
        ],
        compiler_params=_params(("parallel", "parallel", "arbitrary")),
        name="diff_flash",
    )(lam_p, qkv, qkv, qkv, bias, sub_norm.reshape(1, HEAD_W))


PAGES_PER_STEP = 16


def _decode_page_index(t_new, past_len):
    j = np.arange(PAGE)[:, None]
    i = np.arange(t_new)[None, :]
    far = np.full((PAGE, t_new), N_BUCKETS - 1, np.int32)
    new = np.where(j < t_new, _bucket_np(i - j), -1)
    g = np.arange(N_KV)[None, :, None, None]
    h = np.arange(N_HEADS)[None, None, :, None]
    tiles = []
    for bkt in (far, _bucket_np(PAGE + i - j), new):
        b = bkt[:, None, None, :]
        idx = np.where((b >= 0) & (h // HPG == g), b * N_HEADS + h, -1)
        tiles.append(idx.reshape(PAGE * N_KV, N_HEADS * t_new))
    return np.stack(tiles).astype(np.int32)


def _stack_group_heads(q_ref, g):
    return jnp.concatenate(
        [q_ref[0, :, (g * HPG + r) * HEAD_W:(g * HPG + r + 1) * HEAD_W] for r in range(HPG)], axis=0)


def _query_rows(q_ref, maps, scale):
    pieces = []
    for m in range(maps):
        for h in range(N_HEADS):
            qh = q_ref[0, :, h * HEAD_W:(h + 1) * HEAD_W] * scale
            if maps == 2:
                lo = lax.broadcasted_iota(jnp.int32, qh.shape, 1) < DIFF_DH
                qh = jnp.where(lo, qh, 0.0) if m == 0 else jnp.where(lo, 0.0, qh)
            pieces.append(qh)
    return jnp.concatenate(pieces, axis=0).astype(BF16)


def _page_bias_t(bias_ref, first_page, n_pages, reps):
    tiles = [bias_ref[jnp.where(first_page + u == n_pages - 1, 1, 0)] for u in range(PAGES_PER_STEP)]
    b = jnp.concatenate(tiles, axis=0)
    return jnp.concatenate([b] * reps, axis=1)


def _decode_step_t(k_rows, v_rows, q_ref, bias, m_ref, l_ref, acc_ref, keep=None):
    s = _dot_t(k_rows.astype(BF16), q_ref[...]) + bias
    if keep is not None:
        s = jnp.where(keep > 0.5, s, NEG_INF)
    m_old = m_ref[...]
    m_new = jnp.maximum(m_old, jnp.max(s, axis=0, keepdims=True))
    alpha = jnp.exp(m_old - m_new)
    p = jnp.exp(s - m_new)
    l_ref[...] = alpha * l_ref[...] + jnp.sum(p, axis=0, keepdims=True)
    acc_ref[...] = alpha * acc_ref[...] + _dot(v_rows.T.astype(BF16), p.astype(BF16))
    m_ref[...] = m_new


def _page_rows(refs):
    return jnp.concatenate([r[0, 0] for r in refs], axis=0)


def _diff_decode_kernel(pt_ref, lam_ref, q_ref, kn_ref, vn_ref, *rest, n_pages, lam_init):
    del pt_ref
    kp = rest[:PAGES_PER_STEP]
    vp = rest[PAGES_PER_STEP:2 * PAGES_PER_STEP]
    bias_ref, sn_ref, o_ref, qr_ref, m_ref, l_ref, acc_ref = rest[2 * PAGES_PER_STEP:]
    s = pl.program_id(1)
    t_new = q_ref.shape[1]

    @pl.when(s == 0)
    def _():
        _reset_flash(m_ref, l_ref, acc_ref)
        qr_ref[...] = _query_rows(q_ref, 2, DIFF_DH ** -0.5)
        b = bias_ref[2]
        _decode_step_t(kn_ref[0], vn_ref[0], qr_ref, jnp.concatenate([b, b], axis=1), m_ref, l_ref, acc_ref)

    _decode_step_t(_page_rows(kp), _page_rows(vp), qr_ref,
                   _page_bias_t(bias_ref, s * PAGES_PER_STEP, n_pages, 2), m_ref, l_ref, acc_ref)

    @pl.when(s == pl.num_programs(1) - 1)
    def _():
        o = (acc_ref[...] / l_ref[...]).T
        a = _diff_finish_rows(o, _diff_lambda(lam_ref, lam_init), sn_ref[...], lam_init)
        for h in range(N_HEADS):
            o_ref[0, :, h * HEAD_W:(h + 1) * HEAD_W] = a[h * t_new:(h + 1) * t_new]


PAGE_ROWS = PAGE * N_KV


def _as_page_rows(cache):
    return cache.reshape(*cache.shape[:2], PAGE_ROWS, HEAD_W)


def _page_specs(layer, n):
    def spec(u):
        return pl.BlockSpec((1, 1, PAGE_ROWS, HEAD_W),
                            lambda i, s, pt: (layer, pt[i, s * PAGES_PER_STEP + u], 0, 0))
    return [spec(u) for u in range(PAGES_PER_STEP)] * n


def diff_decode(q, k_new, v_new, cache_k, cache_v, layer, page_table, bias, lam_p, sub_norm, lam_init):
    b, t_new, _ = q.shape
    n_pages = page_table.shape[1]
    lanes = 2 * N_HEADS * t_new
    assert N_HEADS * t_new == LANE
    fixed = lambda *shape: pl.BlockSpec(shape, lambda i, s, pt: (0,) * len(shape))
    per_b = lambda *shape: pl.BlockSpec((1,) + shape, lambda i, s, pt: (i,) + (0,) * len(shape))
    grid_spec = pltpu.PrefetchScalarGridSpec(
        num_scalar_prefetch=1,
        grid=(b, n_pages // PAGES_PER_STEP),
        in_specs=[fixed(4, DIFF_DH), per_b(t_new, N_HEADS * HEAD_W), per_b(PAGE_ROWS, HEAD_W),
                  per_b(PAGE_ROWS, HEAD_W)]
        + _page_specs(layer, 2)
        + [fixed(*bias.shape), fixed(1, HEAD_W)],
        out_specs=per_b(t_new, N_HEADS * HEAD_W),
        scratch_shapes=[
            pltpu.VMEM((lanes, HEAD_W), BF16),
            pltpu.VMEM((1, lanes), F32),
            pltpu.VMEM((1, lanes), F32),
            pltpu.VMEM((HEAD_W, lanes), F32),
        ],
    )
    return pl.pallas_call(
        functools.partial(_diff_decode_kernel, n_pages=n_pages, lam_init=lam_init),
        grid_spec=grid_spec,
        out_shape=jax.ShapeDtypeStruct(q.shape[:2] + (N_HEADS * HEAD_W,), F32),
        compiler_params=_params(("parallel", "arbitrary")),
        name="diff_decode",
    )(page_table, lam_p, q, k_new, v_new, *([cache_k] * PAGES_PER_STEP), *([cache_v] * PAGES_PER_STEP),
      bias, sub_norm.reshape(1, HEAD_W))


def _pad_rows(a, n):
    return jnp.pad(a, ((0, 0), (0, n - a.shape[1]), (0, 0)))


def _group_rows_kernel(x_ref, o_ref):
    tm = x_ref.shape[0]
    for g in range(N_KV):
        o_ref[pl.ds(g, tm, stride=N_KV), :] = x_ref[:, g * HEAD_W:(g + 1) * HEAD_W]


def group_rows(x, col_block):
    m = x.shape[0]
    tm = min(m, 512)
    assert m % tm == 0
    return pl.pallas_call(
        _group_rows_kernel,
        grid=(m // tm,),
        in_specs=[pl.BlockSpec((tm, N_KV * HEAD_W), lambda i: (i, col_block))],
        out_specs=pl.BlockSpec((tm * N_KV, HEAD_W), lambda i: (i, 0)),
        out_shape=jax.ShapeDtypeStruct((m * N_KV, HEAD_W), F32),
        compiler_params=_params(("parallel",)),
        name="group_rows",
    )(x)


def _new_page_rows(rows, b):
    return _pad_rows(rows.reshape(b, -1, HEAD_W), PAGE_ROWS)


def diff_layer(xp, xs, gain, w_in, q_norm, k_norm, lam_p, sub_norm, w_out, cache_k, cache_v, layer,
               page_table, bias_p, bias_s, lam_init):
    nq, nk = N_HEADS * HEAD_W, N_KV * HEAD_W
    tile = 4 * LANE
    q_gain, k_gain = jnp.tile(q_norm, tile // DIFF_DH), jnp.tile(k_norm, tile // DIFF_DH)
    head_norm = (DIFF_DH, {**{c: q_gain for c in range(nq // tile)}, nq // tile: k_gain})
    outs = []
    for x, paged in ((xp, False), (xs, True)):
        b, t, d = x.shape
        x2 = x.reshape(b * t, d)
        qkv2 = dense(x2, w_in, layer, gain=gain, head_norm=head_norm)
        qkv = qkv2.reshape(b, t, -1)
        k_rows, v_rows = group_rows(qkv2, nq // nk), group_rows(qkv2, nq // nk + 1)
        if paged:
            o = diff_decode(qkv, _new_page_rows(k_rows, b), _new_page_rows(v_rows, b), _as_page_rows(cache_k),
                            _as_page_rows(cache_v), layer, page_table, bias_s, lam_p, sub_norm, lam_init)
        else:
            o = diff_flash(qkv, bias_p, lam_p, sub_norm, lam_init)
        y = dense(o.reshape(b * t, -1), w_out, layer, res=x2).reshape(b, t, d)
        outs.append((y, k_rows.reshape(b, t, N_KV, HEAD_W), v_rows.reshape(b, t, N_KV, HEAD_W)))
    (yp, kp, vp), (ys, ks, vs) = outs
    return yp, ys, kp, vp, ks, vs


SSM_GW = SSM_HPG * SSM_HEADDIM
SSM_BC = 2 * SSM_GROUPS * D_STATE
CONV_PAD = 8


def _conv_silu(buf_ref, w_ref, b_ref, n):
    acc = b_ref[...]
    for k in range(CONV_W):
        acc = acc + buf_ref[pl.ds(CONV_PAD - (CONV_W - 1) + k, n), :] * w_ref[k:k + 1, :]
    return acc * jax.nn.sigmoid(acc)


def _ssd_kernel(z_ref, x_ref, bc_ref, dt_ref, cbx_ref, cbbc_ref, wx_ref, wbc_ref, bx_ref, bbc_ref,
                dtb_ref, alog_ref, dsk_ref, ng_ref, e_ref, tri_ref, h0_ref,
                y_ref, hout_ref, ht_ref, xbuf_ref, bcbuf_ref, xa_ref, bca_ref, *, t_valid):
    c = pl.program_id(1)
    n = x_ref.shape[1]

    @pl.when(c == 0)
    def _():
        for g in range(SSM_GROUPS):
            ht_ref[g] = h0_ref[0, g].T
        xbuf_ref[0:CONV_PAD] = cbx_ref[0]
        bcbuf_ref[0:CONV_PAD] = cbbc_ref[0]

    xbuf_ref[CONV_PAD:CONV_PAD + n] = x_ref[0]
    bcbuf_ref[CONV_PAD:CONV_PAD + n] = bc_ref[0]
    xa_ref[...] = _conv_silu(xbuf_ref, wx_ref, bx_ref, n)
    bca_ref[...] = _conv_silu(bcbuf_ref, wbc_ref, bbc_ref, n)
    xbuf_ref[0:CONV_PAD] = xbuf_ref[n:n + CONV_PAD]
    bcbuf_ref[0:CONV_PAD] = bcbuf_ref[n:n + CONV_PAD]

    row = lax.broadcasted_iota(jnp.int32, (n, LANE), 0) + c * n
    dtr = dt_ref[0] + dtb_ref[...]
    dt = jnp.maximum(dtr, 0.0) + jnp.log1p(jnp.exp(-jnp.abs(dtr)))
    dt = jnp.where(row < t_valid, dt, 0.0)
    dta = dt * (-jnp.exp(alog_ref[...]))
    hi, mid, lo = _split3(dta)
    tri = tri_ref[...]
    cs = _dot(tri, hi) + _dot(tri, mid) + _dot(tri, lo)
    cs_last = cs[n - 1:n, :]
    cs_t = cs.T
    dt_t = dt.T
    stacked = jnp.concatenate(
        [jnp.exp(cs), jnp.exp(cs_last - cs) * dt, jnp.broadcast_to(jnp.exp(cs_last), (8, LANE))], axis=0)
    ex = _dot_exact_rhs(stacked, e_ref[...])
    causal = (lax.broadcasted_iota(jnp.int32, (n, n), 0) >= lax.broadcasted_iota(jnp.int32, (n, n), 1))

    for g in range(SSM_GROUPS):
        gs = slice(g * SSM_GW, (g + 1) * SSM_GW)
        bm = bca_ref[:, g * D_STATE:(g + 1) * D_STATE]
        cm = bca_ref[:, (SSM_GROUPS + g) * D_STATE:(SSM_GROUPS + g + 1) * D_STATE].astype(BF16)
        cb = _dot_t(cm, bm.astype(BF16))
        xg = xa_ref[:, gs]
        ys = []
        for r in range(SSM_HPG):
            h = g * SSM_HPG + r
            seg = cs[:, h:h + 1] - cs_t[h:h + 1, :]
            dec = jnp.where(causal, jnp.exp(jnp.where(causal, seg, 0.0)), 0.0)
            mm = (cb * dec * dt_t[h:h + 1, :]).astype(BF16)
            ys.append(_dot(mm, xg[:, r * SSM_HEADDIM:(r + 1) * SSM_HEADDIM].astype(BF16)))
        ht = ht_ref[g]
        y = jnp.concatenate(ys, axis=1) + _dot(cm, ht.astype(BF16)) * ex[0:n, gs]
        y = y + dsk_ref[:, gs] * xg
        zg = z_ref[0, :, gs]
        y = y * (zg * jax.nn.sigmoid(zg))
        y_ref[0, :, gs] = _rms(y, ng_ref[:, gs])
        xw = (xg * ex[n:2 * n, gs]).astype(BF16)
        ht_ref[g] = ht * ex[2 * n:2 * n + 1, gs] + _dot(bm.T.astype(BF16), xw)

    @pl.when(c == pl.num_programs(1) - 1)
    def _():
        for g in range(SSM_GROUPS):
            hout_ref[0, g] = ht_ref[g].T


def _head_expand_matrix():
    e = np.zeros((LANE, D_INNER), np.float32)
    for h in range(SSM_HEADS):
        e[h, h * SSM_HEADDIM:(h + 1) * SSM_HEADDIM] = 1.0
    return e


def ssd_core(zx, dt_raw, conv_buf, h0, conv_w, conv_b, dt_bias, a_log, d_skip, norm_g, t_valid):
    b, t, _ = zx.shape
    n = SSM_CHUNK
    pad_h = LANE - SSM_HEADS
    cb = jnp.pad(conv_buf, ((0, 0), (CONV_PAD - (CONV_W - 1), 0), (0, 0)))
    tri = jnp.asarray(np.tril(np.ones((n, n), np.float32)), BF16)
    e = jnp.asarray(_head_expand_matrix(), BF16)
    d_exp = jnp.repeat(d_skip, SSM_HEADDIM).reshape(1, D_INNER)
    fixed = lambda *shape: pl.BlockSpec(shape, lambda i, c: (0,) * len(shape))
    per_b = lambda *shape: pl.BlockSpec((1,) + shape, lambda i, c: (i,) + (0,) * len(shape))
    y, h_last = pl.pallas_call(
        functools.partial(_ssd_kernel, t_valid=t_valid),
        grid=(b, t // n),
        in_specs=[
            pl.BlockSpec((1, n, D_INNER), lambda i, c: (i, c, 0)),
            pl.BlockSpec((1, n, D_INNER), lambda i, c: (i, c, 1)),
            pl.BlockSpec((1, n, SSM_BC), lambda i, c: (i, c, 2 * D_INNER // SSM_BC)),
            pl.BlockSpec((1, n, LANE), lambda i, c: (i, c, 0)),
            per_b(CONV_PAD, D_INNER), per_b(CONV_PAD, SSM_BC),
            fixed(CONV_W, D_INNER), fixed(CONV_W, SSM_BC), fixed(1, D_INNER), fixed(1, SSM_BC),
            fixed(1, LANE), fixed(1, LANE), fixed(1, D_INNER), fixed(1, D_INNER),
            fixed(LANE, D_INNER), fixed(n, n),
            per_b(SSM_GROUPS, SSM_GW, D_STATE),
        ],
        out_specs=[
            pl.BlockSpec((1, n, D_INNER), lambda i, c: (i, c, 0)),
            per_b(SSM_GROUPS, SSM_GW, D_STATE),
        ],
        out_shape=[jax.ShapeDtypeStruct((b, t, D_INNER), F32),
                   jax.ShapeDtypeStruct((b, SSM_GROUPS, SSM_GW, D_STATE), F32)],
        scratch_shapes=[
            pltpu.VMEM((SSM_GROUPS, D_STATE, SSM_GW), F32),
            pltpu.VMEM((n + CONV_PAD, D_INNER), F32),
            pltpu.VMEM((n + CONV_PAD, SSM_BC), F32),
            pltpu.VMEM((n, D_INNER), F32),
            pltpu.VMEM((n, SSM_BC), F32),
        ],
        compiler_params=_params(("parallel", "arbitrary")),
        name="ssd_core",
    )(zx, zx, zx, dt_raw, cb[:, :, :D_INNER], cb[:, :, D_INNER:],
      conv_w[:, :D_INNER], conv_w[:, D_INNER:], conv_b[:D_INNER].reshape(1, -1), conv_b[D_INNER:].reshape(1, -1),
      jnp.pad(dt_bias, (0, pad_h)).reshape(1, LANE), jnp.pad(a_log, (0, pad_h)).reshape(1, LANE),
      d_exp, norm_g.reshape(1, D_INNER), e, tri,
      h0.reshape(b, SSM_GROUPS, SSM_GW, D_STATE))
    return y, h_last.reshape(b, SSM_HEADS, SSM_HEADDIM, D_STATE)


def ssd_layer(xp, xs, gain, w_in, layer, conv_w, conv_b, dt_bias, a_log, d_skip, norm_g, w_out,
              state_conv, state_ssm):
    nzx = D_INNER + CONV_DIM
    w_dt = jnp.pad(w_in[layer, :, nzx:], ((0, 0), (0, LANE - SSM_HEADS)))[None]
    w_in_t = jnp.swapaxes(w_in, 1, 2)
    outs = []
    for x, conv_buf, h0 in ((xp, None, None), (xs, state_conv, state_ssm)):
        b, t, d = x.shape
        x2 = x.reshape(b * t, d)
        if conv_buf is None:
            conv_buf = jnp.zeros((b, CONV_W - 1, CONV_DIM), F32)
            h0 = jnp.zeros((b, SSM_HEADS, SSM_HEADDIM, D_STATE), F32)
        zx = dense(x2, w_in_t, layer, gain=gain, n=nzx, w_t=True).reshape(b, t, -1)
        dt_raw = dense(x2, w_dt, 0, gain=gain).reshape(b, t, LANE)
        tp = -(-t // SSM_CHUNK) * SSM_CHUNK
        y, h_last = ssd_core(_pad_rows(zx, tp), _pad_rows(dt_raw, tp), conv_buf, h0, conv_w, conv_b,
                             dt_bias, a_log, d_skip, norm_g, t)
        y = y[:, :t].reshape(b * t, D_INNER)
        tail = min(t, CONV_W - 1)
        conv_out = jnp.concatenate([conv_buf[:, tail:], zx[:, t - tail:, D_INNER:]], axis=1)
        outs.append((dense(y, w_out, layer, res=x2).reshape(b, t, d), h_last, conv_out))
    (yp, hp, cp), (ys, hs, cs) = outs
    return yp, ys, hp, cp, hs, cs


NSA_KV = N_KV * HEAD_W
NSA_SCALE = HEAD_W ** -0.5
CMP_PAIRS = CMP_STRIDE // 2
CMP_PAGES = 32


def _cmp_uv_kernel(pt_ref, *refs):
    del pt_ref
    pages = refs[:-2]
    wab_ref, o_ref = refs[-2:]
    subs = PAGE // CMP_STRIDE
    for g in range(N_KV):
        acc = jnp.zeros((len(pages) * subs, 2 * HEAD_W), F32)
        for lp in range(CMP_PAIRS):
            halves = []
            for li in range(2):
                rows = pl.ds((2 * lp + li) * N_KV + g, subs, stride=CMP_STRIDE * N_KV)
                halves.append(jnp.concatenate([p[0, 0, rows, :] for p in pages], axis=0))
            acc = acc + _dot(jnp.concatenate(halves, axis=1).astype(BF16), wab_ref[lp])
        o_ref[0, :, g * 2 * HEAD_W:(g + 1) * 2 * HEAD_W] = acc


def cmp_uv(rows, layer, page_table, wab):
    b, n_pages = page_table.shape
    subs = PAGE // CMP_STRIDE
    step_pages = min(CMP_PAGES, n_pages)
    assert n_pages % step_pages == 0

    def spec(u):
        return pl.BlockSpec((1, 1, PAGE_ROWS, HEAD_W),
                            lambda i, c, pt: (layer, pt[i, c * step_pages + u], 0, 0))

    grid_spec = pltpu.PrefetchScalarGridSpec(
        num_scalar_prefetch=1,
        grid=(b, n_pages // step_pages),
        in_specs=[spec(u) for u in range(step_pages)]
        + [pl.BlockSpec(wab.shape, lambda i, c, pt: (0, 0, 0))],
        out_specs=pl.BlockSpec((1, step_pages * subs, 2 * NSA_KV), lambda i, c, pt: (i, c, 0)),
    )
    return pl.pallas_call(
        _cmp_uv_kernel,
        grid_spec=grid_spec,
        out_shape=jax.ShapeDtypeStruct((b, n_pages * subs, 2 * NSA_KV), F32),
        compiler_params=_params(("parallel", "arbitrary")),
        name="cmp_uv",
    )(page_table, *([rows] * step_pages), wab)


def _cmp_finish_kernel(uv_ref, pe_ref, w1_ref, w2_ref, kn_ref, o_ref, *, norm):
    n = uv_ref.shape[1]
    c = _dot(jnp.broadcast_to(pe_ref[...], (8, pe_ref.shape[1])).astype(BF16), w1_ref[...])[0:1]
    for g in range(N_KV):
        u = uv_ref[0, :, g * 2 * HEAD_W:g * 2 * HEAD_W + HEAD_W]
        v = uv_ref[0, :, g * 2 * HEAD_W + HEAD_W:(g + 1) * 2 * HEAD_W]
        pre = u + pltpu.roll(v, n - 1, axis=0) + c
        out = _dot((pre * jax.nn.sigmoid(pre)).astype(BF16), w2_ref[...])
        if norm:
            out = _rms(out, kn_ref[...])
        o_ref[0, :, g * HEAD_W:(g + 1) * HEAD_W] = out


def cmp_finish(uv, pe, w1, w2, k_norm):
    b, n, _ = uv.shape
    norm = k_norm is not None
    kn = (k_norm if norm else jnp.ones((HEAD_W,), F32)).reshape(1, HEAD_W)
    return pl.pallas_call(
        functools.partial(_cmp_finish_kernel, norm=norm),
        grid=(b,),
        in_specs=[
            pl.BlockSpec((1, n, 2 * NSA_KV), lambda i: (i, 0, 0)),
            pl.BlockSpec((1, pe.size), lambda i: (0, 0)),
            pl.BlockSpec(w1.shape, lambda i: (0, 0)),
            pl.BlockSpec(w2.shape, lambda i: (0, 0)),
            pl.BlockSpec((1, HEAD_W), lambda i: (0, 0)),
        ],
        out_specs=pl.BlockSpec((1, n, NSA_KV), lambda i: (i, 0, 0)),
        out_shape=jax.ShapeDtypeStruct((b, n, NSA_KV), F32),
        compiler_params=_params(("parallel",)),
        name="cmp_finish",
    )(uv, pe.reshape(1, -1), w1, w2, kn)


def compress(rows, layer, page_table, pe, w1, w2, k_norm):
    w1r = w1.reshape(2, CMP_STRIDE, HEAD_W, HEAD_W)
    wab = jnp.transpose(w1r, (1, 2, 0, 3)).reshape(CMP_PAIRS, 2 * HEAD_W, 2 * HEAD_W).astype(BF16)
    uv = cmp_uv(rows, layer, page_table, wab)
    return cmp_finish(uv, pe, w1.astype(BF16), w2.astype(BF16), k_norm)


def _importance_matrix(n_cmp, n_rows, n_slc, n_cols):
    w = np.zeros((n_rows, n_cols), np.float32)
    w_imp = [1.0] + [2.0] * (SLC_RATIO - 1) + [1.0]
    for s in range(n_slc):
        for m, wm in enumerate(w_imp):
            j = SLC_RATIO * s + m - 1
            if 0 <= j < n_cmp:
                w[j, s] += wm
    return w


def _select_blocks(s_slc, q_pos0, n_slc):
    t, w = s_slc.shape
    blk = lax.broadcasted_iota(jnp.int32, (t, w), 1)
    qpos = q_pos0 + lax.broadcasted_iota(jnp.int32, (t, w), 0)
    qb = qpos // SLC_BLOCK
    forced = (blk == 0) | (blk == qb) | (blk == qb - 1)
    score = jnp.where(forced, FORCE_SCORE, jnp.where(blk * SLC_BLOCK <= qpos, s_slc, -1.0))
    score = jnp.where(blk < n_slc, score, -2.0)
    cnt = jnp.zeros((t, w), F32)
    for sp in range(n_slc):
        col = score[:, sp:sp + 1]
        tie = jnp.where(blk > sp, 1.0, 0.0)
        cnt = cnt + jnp.where(col > score, 1.0, jnp.where(col == score, tie, 0.0))
    return jnp.where(cnt < N_SELECT, 1.0, 0.0)


def _masked_softmax(s, valid, axis=-1, base2=False):
    m = jnp.max(s, axis=axis, keepdims=True)
    e = jnp.where(valid, (jnp.exp2 if base2 else jnp.exp)(s - m), 0.0)
    return e / jnp.maximum(jnp.sum(e, axis=axis, keepdims=True), 1e-30)


def _reset_flash(m_ref, l_ref, acc_ref):
    m_ref[...] = jnp.full(m_ref.shape, NEG_INF, F32)
    l_ref[...] = jnp.zeros(l_ref.shape, F32)
    acc_ref[...] = jnp.zeros(acc_ref.shape, F32)


def _select_blocks_t(s_slc, q_pos0, n_slc):
    w, t = s_slc.shape
    n8 = -(-n_slc // 8) * 8
    blk = lax.broadcasted_iota(jnp.int32, (n8, t), 0)
    qpos = q_pos0 + lax.broadcasted_iota(jnp.int32, (n8, t), 1)
    qb = qpos // SLC_BLOCK
    forced = (blk == 0) | (blk == qb) | (blk == qb - 1)
    score = jnp.where(forced, FORCE_SCORE, jnp.where(blk * SLC_BLOCK <= qpos, s_slc[:n8], -1.0))
    score = jnp.where(blk < n_slc, score, -2.0)
    cnt = jnp.zeros((n8, t), F32)
    for sp in range(n_slc):
        row = score[sp:sp + 1, :]
        tie = jnp.where(blk > sp, 1.0, 0.0)
        cnt = cnt + jnp.where(row > score, 1.0, jnp.where(row == score, tie, 0.0))
    sel = jnp.where(cnt < N_SELECT, 1.0, 0.0)
    if n8 < w:
        sel = jnp.concatenate([sel, jnp.zeros((w - n8, t), F32)], axis=0)
    return sel


def _nsa_prompt_kernel(q_ref, gt_ref, kc_ref, vc_ref, ks_ref, vs_ref, kw_ref, vw_ref, bias_ref, u_ref, wimp_ref,
                       o_ref, qs_ref, ksb_ref, vst_ref, kwb_ref, vwt_ref, kcb_ref, vct_ref, sel_ref, m_ref, l_ref,
                       acc_ref, *, tq, n_slc):
    qi = pl.program_id(2)

    @pl.when(qi == 0)
    def _():
        _stage_kv(ks_ref, vs_ref, ksb_ref, vst_ref, tq)
        _stage_kv(kw_ref, vw_ref, kwb_ref, vwt_ref, tq)
        _stage_kv(kc_ref, vc_ref, kcb_ref, vct_ref, kc_ref.shape[1])

    qs_ref[...] = jnp.concatenate(
        [q_ref[0, :, r * HEAD_W:(r + 1) * HEAD_W] for r in range(HPG)], axis=0).astype(BF16)

    n_pad = kc_ref.shape[1]
    start = pl.multiple_of(n_pad - (tq // CMP_STRIDE) * (qi + 1), 8)
    scale = NSA_SCALE * LOG2E
    bias_c = u_ref[0, pl.ds(start, n_pad), :]
    s = _dot_t(kcb_ref[...], qs_ref[...]) * scale + bias_c
    pc = _masked_softmax(s, bias_c > 0.5 * NEG_INF, axis=0, base2=True)
    o_cmp = _dot(vct_ref[...], pc.astype(BF16))
    psum = pc[:, 0:tq] + pc[:, tq:2 * tq] + pc[:, 2 * tq:3 * tq] + pc[:, 3 * tq:4 * tq]
    hi, mid, lo = _split3(psum)
    wimp = wimp_ref[...]
    s_slc = _dot(wimp, hi) + _dot(wimp, mid) + _dot(wimp, lo)
    sel_ref[...] = _select_blocks_t(s_slc, qi * tq, n_slc).astype(BF16)

    _reset_flash(m_ref, l_ref, acc_ref)

    far = bias_ref[2, 0:1, :]

    def key_mask(kt, width):
        key = lax.broadcasted_iota(jnp.int32, (width, LANE), 0)
        blk = lax.broadcasted_iota(jnp.int32, (width, LANE), 1)
        onehot = jnp.where(blk == kt * (width // SLC_BLOCK) + key // SLC_BLOCK, 1.0, 0.0).astype(BF16)
        return jnp.concatenate([_dot(onehot, sel_ref[...])] * HPG, axis=1)

    def slc_far_body(width):
        def body(kt, c):
            rows = pl.ds(pl.multiple_of(kt * width, width), width)
            _flash_tile_t(ksb_ref[rows, :], vst_ref[:, rows], qs_ref, m_ref, l_ref, acc_ref, scale=scale,
                          far_bias=far, mask=key_mask(kt, width))
            return c
        return body

    def slc_near_body(kt, c):
        rows = pl.ds(pl.multiple_of(kt * tq, tq), tq)
        typ = qi - kt
        _flash_tile_t(ksb_ref[rows, :], vst_ref[:, rows], qs_ref, m_ref, l_ref, acc_ref, scale=scale,
                      bias=lambda cols: bias_ref[typ, :, cols], mask=key_mask(kt, tq))
        return c

    n_far = jnp.maximum(qi - 1, 0)
    lax.fori_loop(0, n_far // FAR_TILES, slc_far_body(FAR_TILES * tq), 0)
    lax.fori_loop((n_far // FAR_TILES) * FAR_TILES, n_far, slc_far_body(tq), 0)
    lax.fori_loop(n_far, qi + 1, slc_near_body, 0)
    o_slc = acc_ref[...] / l_ref[...]

    _reset_flash(m_ref, l_ref, acc_ref)
    nw = WINDOW // tq

    def win_body(kt, c):
        rows = pl.ds(pl.multiple_of(kt * tq, tq), tq)
        t = qi - kt
        typ = jnp.where(t == nw, 3, jnp.minimum(t, 2))
        _flash_tile_t(kwb_ref[rows, :], vwt_ref[:, rows], qs_ref, m_ref, l_ref, acc_ref, scale=scale,
                      bias=lambda cols: bias_ref[typ, :, cols])
        return c

    lax.fori_loop(jnp.maximum(qi - nw, 0), qi + 1, win_body, 0)
    o_win = acc_ref[...] / l_ref[...]

    sig = jax.nn.sigmoid(gt_ref[0]).T
    for r in range(HPG):
        cs = slice(r * tq, (r + 1) * tq)
        o_t = (sig[r:r + 1, :] * o_cmp[:, cs] + sig[HPG + r:HPG + r + 1, :] * o_slc[:, cs]
               + sig[2 * HPG + r:2 * HPG + r + 1, :] * o_win[:, cs])
        o_ref[0, :, r * HEAD_W:(r + 1) * HEAD_W] = o_t.T


def _prompt_cmp_buckets(tq, n_pad):
    jp = np.arange(2 * n_pad)[:, None] - (n_pad - tq // CMP_STRIDE)
    i = np.arange(tq)[None, :]
    return _bucket_np(i - CMP_STRIDE * jp - (2 * CMP_STRIDE - 1))[None].astype(np.int32)


def nsa_prompt(proj, gates, kc, vc, bias, u_bias, tq=PROMPT_TQ):
    b, t, _ = proj.shape
    gw = HPG * HEAD_W
    n_pad = kc.shape[1]
    n_slc = t // SLC_BLOCK
    wimp = jnp.asarray(_importance_matrix(n_pad - 1, n_pad, n_slc, LANE).T, BF16)
    rows = HPG * tq
    seq = lambda n: pl.BlockSpec((1, n, HEAD_W), lambda i, g, j: (i, 0, g))
    kv = lambda a: pl.BlockSpec((1, t, HEAD_W), lambda i, g, j: (i, 0, N_HEADS + a * N_KV + g))
    return pl.pallas_call(
        functools.partial(_nsa_prompt_kernel, tq=tq, n_slc=n_slc),
        grid=(b, N_KV, t // tq),
        in_specs=[
            pl.BlockSpec((1, tq, gw), lambda i, g, j: (i, j, g)),
            pl.BlockSpec((1, tq, LANE), lambda i, g, j: (i, j, g)),
            seq(n_pad), seq(n_pad), kv(2), kv(3), kv(4), kv(5),
            pl.BlockSpec((4, tq, rows), lambda i, g, j: (0, 0, g)),
            pl.BlockSpec((1, 2 * n_pad, rows), lambda i, g, j: (0, 0, g)),
            pl.BlockSpec(wimp.shape, lambda i, g, j: (0, 0)),
        ],
        out_specs=pl.BlockSpec((1, tq, gw), lambda i, g, j: (i, j, g)),
        out_shape=jax.ShapeDtypeStruct((b, t, N_HEADS * HEAD_W), F32),
        scratch_shapes=[
            pltpu.VMEM((rows, HEAD_W), BF16),
            pltpu.VMEM((t, HEAD_W), BF16),
            pltpu.VMEM((HEAD_W, t), BF16),
            pltpu.VMEM((t, HEAD_W), BF16),
            pltpu.VMEM((HEAD_W, t), BF16),
            pltpu.VMEM((n_pad, HEAD_W), BF16),
            pltpu.VMEM((HEAD_W, n_pad), BF16),
            pltpu.VMEM((LANE, tq), BF16),
            pltpu.VMEM((1, rows), F32),
            pltpu.VMEM((1, rows), F32),
            pltpu.VMEM((HEAD_W, rows), F32),
        ],
        compiler_params=_params(("parallel", "parallel", "arbitrary")),
        name="nsa_prompt",
    )(proj, gates, kc, vc, proj, proj, proj, proj, bias, u_bias, wimp)


def _decode_cmp_buckets(t_new, past_len):
    n_pad = past_len // CMP_STRIDE
    i = np.arange(t_new)[:, None]
    j = n_pad - LANE + np.arange(LANE)[None, :]
    last = _bucket_np(past_len + i - CMP_STRIDE * j - (2 * CMP_STRIDE - 1))
    last = np.where(j < n_pad - 1, last, -1)
    return np.stack([np.full((t_new, LANE), N_BUCKETS - 1, np.int32), last]).astype(np.int32)


def _decode_win_buckets(t_new, n_tiles):
    i = np.arange(t_new)[:, None]
    idx = np.arange(n_tiles * LANE)[None, :]
    dw = WINDOW + i - idx
    ok = (dw >= 0) & (dw < WINDOW) & (idx < WINDOW + t_new)
    b = np.where(ok, _bucket_np(dw), -1)
    return np.stack([b[:, k * LANE:(k + 1) * LANE] for k in range(n_tiles)]).astype(np.int32)


def _nsa_decode_kernel(pt_ref, q_ref, gt_ref, kc_ref, vc_ref, kn_ref, vn_ref, kw_ref, vw_ref, *rest,
                       n_pages, past_len):
    del pt_ref
    kp = rest[:PAGES_PER_STEP]
    vp = rest[PAGES_PER_STEP:2 * PAGES_PER_STEP]
    (bias_ref, bias_c_ref, bias_w_ref, wimp_ref, o_ref,
     qr_ref, selt_ref, ocmp_ref, owin_ref, m_ref, l_ref, acc_ref) = rest[2 * PAGES_PER_STEP:]
    s = pl.program_id(1)
    t_new = q_ref.shape[1]
    rows = HPG * t_new
    n_slc = -(-(past_len + t_new) // SLC_BLOCK)

    @pl.when(s == 0)
    def _():
        _reset_flash(m_ref, l_ref, acc_ref)
        qr_ref[...] = _query_rows(q_ref, 1, NSA_SCALE)
        n_pad = kc_ref.shape[1]
        sel_t = jnp.zeros(selt_ref.shape, F32)
        for g in range(N_KV):
            sl = slice(g * HEAD_W, (g + 1) * HEAD_W)
            gr = slice(g * rows, (g + 1) * rows)
            qx = _stack_group_heads(q_ref, g).astype(BF16)
            bias_c = jnp.concatenate([bias_c_ref[0, gr, :]] * (n_pad // LANE - 1) + [bias_c_ref[1, gr, :]], axis=1)
            sc = _dot_t(qx, kc_ref[0, :, sl].astype(BF16)) * NSA_SCALE + bias_c
            pc = _masked_softmax(sc, bias_c > 0.5 * NEG_INF)
            ocmp_ref[g] = _dot(pc.astype(BF16), vc_ref[0, :, sl].astype(BF16))
            psum = sum(pc[r * t_new:(r + 1) * t_new] for r in range(1, HPG)) + pc[0:t_new]
            s_slc = _dot_exact_rhs(psum, wimp_ref[...])
            sel = _select_blocks(s_slc, past_len, n_slc)
            sel_pad = jnp.concatenate([sel, jnp.zeros((LANE - t_new, sel.shape[1]), F32)], axis=0).T
            tok = lax.broadcasted_iota(jnp.int32, (LANE, LANE), 0)
            lane = lax.broadcasted_iota(jnp.int32, (LANE, LANE), 1)
            spread = jnp.where((lane % t_new == tok) & (lane // rows == g), 1.0, 0.0).astype(BF16)
            sel_t = sel_t + _dot(sel_pad.astype(BF16), spread)
            n_wt = bias_w_ref.shape[0]
            bias_w = jnp.concatenate([bias_w_ref[k, gr, :] for k in range(n_wt)], axis=1)
            sw = _dot_t(qx, kw_ref[0, :, sl].astype(BF16)) * NSA_SCALE + bias_w
            pw = _masked_softmax(sw, bias_w > 0.5 * NEG_INF)
            owin_ref[g] = _dot(pw.astype(BF16), vw_ref[0, :, sl].astype(BF16))
        selt_ref[...] = sel_t
        _decode_step_t(kn_ref[0], vn_ref[0], qr_ref, bias_ref[2], m_ref, l_ref, acc_ref)

    block_rows = SLC_BLOCK * N_KV
    first_block = s * (PAGES_PER_STEP * PAGE // SLC_BLOCK)
    keep = jnp.concatenate(
        [jnp.broadcast_to(selt_ref[pl.ds(first_block + b, 1), :], (block_rows, selt_ref.shape[1]))
         for b in range(PAGES_PER_STEP * PAGE // SLC_BLOCK)], axis=0)
    _decode_step_t(_page_rows(kp), _page_rows(vp), qr_ref,
                   _page_bias_t(bias_ref, s * PAGES_PER_STEP, n_pages, 1), m_ref, l_ref, acc_ref, keep=keep)

    @pl.when(s == pl.num_programs(1) - 1)
    def _():
        sig = jax.nn.sigmoid(gt_ref[0])
        o_slc_all = (acc_ref[...] / l_ref[...]).T
        for g in range(N_KV):
            o_cmp = ocmp_ref[g]
            o_win = owin_ref[g]
            for r in range(HPG):
                h = g * HPG + r
                rs = slice(r * t_new, (r + 1) * t_new)
                c0 = g * LANE + r
                o_ref[0, :, h * HEAD_W:(h + 1) * HEAD_W] = (
                    sig[:, c0:c0 + 1] * o_cmp[rs]
                    + sig[:, c0 + HPG:c0 + HPG + 1] * o_slc_all[h * t_new:(h + 1) * t_new]
                    + sig[:, c0 + 2 * HPG:c0 + 2 * HPG + 1] * o_win[rs])


def nsa_decode(q, gates, kc, vc, k_new, v_new, kw_src, vw_src, cache_k, cache_v, layer, page_table,
               bias, bias_c, bias_w, past_len):
    b, t_new, _ = q.shape
    n_pages = page_table.shape[1]
    rows = HPG * t_new
    lanes = N_HEADS * t_new
    assert lanes == LANE
    n_pad = kc.shape[1]
    n_slc = -(-(past_len + t_new) // SLC_BLOCK)
    n_cols = -(-n_slc // LANE) * LANE
    wimp = jnp.asarray(_importance_matrix(n_pad - 1, n_pad, n_slc, n_cols), BF16)
    fixed = lambda *shape: pl.BlockSpec(shape, lambda i, s, pt: (0,) * len(shape))
    per_b = lambda *shape: pl.BlockSpec((1,) + shape, lambda i, s, pt: (i,) + (0,) * len(shape))
    grid_spec = pltpu.PrefetchScalarGridSpec(
        num_scalar_prefetch=1,
        grid=(b, n_pages // PAGES_PER_STEP),
        in_specs=[per_b(t_new, N_HEADS * HEAD_W), per_b(t_new, N_KV * LANE),
                  per_b(n_pad, NSA_KV), per_b(n_pad, NSA_KV), per_b(PAGE_ROWS, HEAD_W), per_b(PAGE_ROWS, HEAD_W),
                  per_b(kw_src.shape[1], NSA_KV), per_b(kw_src.shape[1], NSA_KV)]
        + _page_specs(layer, 2)
        + [fixed(*bias.shape), fixed(*bias_c.shape), fixed(*bias_w.shape), fixed(*wimp.shape)],
        out_specs=per_b(t_new, N_HEADS * HEAD_W),
        scratch_shapes=[
            pltpu.VMEM((lanes, HEAD_W), BF16),
            pltpu.VMEM((n_cols, lanes), F32),
            pltpu.VMEM((N_KV, rows, HEAD_W), F32),
            pltpu.VMEM((N_KV, rows, HEAD_W), F32),
            pltpu.VMEM((1, lanes), F32),
            pltpu.VMEM((1, lanes), F32),
            pltpu.VMEM((HEAD_W, lanes), F32),
        ],
    )
    return pl.pallas_call(
        functools.partial(_nsa_decode_kernel, n_pages=n_pages, past_len=past_len),
        grid_spec=grid_spec,
        out_shape=jax.ShapeDtypeStruct(q.shape[:2] + (N_HEADS * HEAD_W,), F32),
        compiler_params=_params(("parallel", "arbitrary")),
        name="nsa_decode",
    )(page_table, q, gates, kc, vc, k_new, v_new, kw_src, vw_src,
      *([cache_k] * PAGES_PER_STEP), *([cache_v] * PAGES_PER_STEP), bias, bias_c, bias_w, wimp)


def _gate_weights(w_g):
    d = w_g.shape[0]
    w = jnp.transpose(w_g.reshape(d, 3, N_KV, HPG), (0, 2, 1, 3)).reshape(d, N_KV, 3 * HPG)
    return jnp.pad(w, ((0, 0), (0, 0), (0, LANE - 3 * HPG))).reshape(d, N_KV * LANE)


def nsa_layer(xp, xs, gain, w_in, q_norm, k_norm, pe, w1, w2, w_out, caches, layer, page_table,
              bias_p, u_bias, bias_s, bias_sc, bias_sw, past_len):
    cmp_k, cmp_v, slc_k, slc_v, win_k, win_v = caches
    nqkv = N_HEADS * HEAD_W + 6 * NSA_KV
    w_g = _gate_weights(w_in[layer, :, nqkv:])[None]
    w_in_t = jnp.swapaxes(w_in, 1, 2)
    nq = N_HEADS * HEAD_W
    tile = 4 * LANE
    q_gain = jnp.tile(q_norm, tile // HEAD_W)
    head_norm = (HEAD_W, {**{c: q_gain for c in range(nq // tile)},
                          nq // tile + 2: jnp.tile(k_norm[1], tile // HEAD_W),
                          nq // tile + 4: jnp.tile(k_norm[2], tile // HEAD_W)})
    outs = []
    for x, paged in ((xp, False), (xs, True)):
        b, t, d = x.shape
        x2 = x.reshape(b * t, d)
        proj2 = dense(x2, w_in_t, layer, gain=gain, n=nqkv, w_t=True, head_norm=head_norm)
        proj = proj2.reshape(b, t, -1)
        gates = dense(x2, w_g, 0, gain=gain).reshape(b, t, -1)
        rows = [group_rows(proj2, nq // NSA_KV + a) for a in range(4)]
        as_state = lambda r: r.reshape(b, t, N_KV, HEAD_W)
        kw, vw = proj[:, :, nq + 4 * NSA_KV:nq + 5 * NSA_KV], proj[:, :, nq + 5 * NSA_KV:]
        if paged:
            kc = compress(_as_page_rows(cmp_k), layer, page_table, pe[0], w1[0], w2[0], k_norm[0])
            vc = compress(_as_page_rows(cmp_v), layer, page_table, pe[1], w1[1], w2[1], None)
            n_wt = bias_sw.shape[0]
            kw_src = jnp.concatenate([win_k[layer], kw], axis=1)
            vw_src = jnp.concatenate([win_v[layer], vw], axis=1)
            o = nsa_decode(proj, gates, kc, vc, _new_page_rows(rows[2], b), _new_page_rows(rows[3], b),
                           _pad_rows(kw_src, n_wt * LANE), _pad_rows(vw_src, n_wt * LANE),
                           _as_page_rows(slc_k), _as_page_rows(slc_v), layer, page_table,
                           bias_s, bias_sc, bias_sw, past_len)
            kw_win, vw_win = kw_src, vw_src
        else:
            n_pg = t // PAGE
            ident = jnp.arange(b * n_pg, dtype=jnp.int32).reshape(b, n_pg)
            as_pages = lambda r: r.reshape(1, b * n_pg, PAGE_ROWS, HEAD_W)
            kc = compress(as_pages(rows[0]), 0, ident, pe[0], w1[0], w2[0], k_norm[0])
            vc = compress(as_pages(rows[1]), 0, ident, pe[1], w1[1], w2[1], None)
            o = nsa_prompt(proj, gates, kc, vc, bias_p, u_bias)
            kw_win, vw_win = kw, vw
        y = dense(o.reshape(b * t, -1), w_out, layer, res=x2).reshape(b, t, d)
        win = tuple(a[:, -WINDOW:].reshape(b, WINDOW, N_KV, HEAD_W) for a in (kw_win, vw_win))
        st = tuple(as_state(r) for r in rows) + win
        outs.append((y, st))
    (yp, stp), (ys, sts) = outs
    return yp, ys, stp, sts


def kernel(x_prompt, x_sample, cache_diff_k, cache_diff_v, state_ssm, state_conv, cache_nsa_cmp_k, cache_nsa_cmp_v, cache_nsa_slc_k, cache_nsa_slc_v, cache_nsa_win_k, cache_nsa_win_v, cache_mem_k, cache_mem_v, page_table, mem_prompt, rel_bias_table, norm_mix, norm_xattn, norm_mem, norm_ffn, diff_w_in, diff_q_norm, diff_k_norm, diff_lambda, diff_sub_norm, diff_w_out, ssm_w_in, ssm_conv_w, ssm_conv_b, ssm_dt_bias, ssm_a_log, ssm_d, ssm_norm, ssm_w_out, nsa_w_in, nsa_q_norm, nsa_k_norm, nsa_cmp_pe, nsa_cmp_w1, nsa_cmp_w2, nsa_w_out, xattn_w_q, xattn_w_k, xattn_w_v, xattn_q_norm, xattn_k_norm, xattn_w_o, ffn_w1, ffn_w3, ffn_w2):
    xp, xs = x_prompt, x_sample
    bp, t, d = xp.shape
    bs, t_new, _ = xs.shape
    past_len = page_table.shape[1] * PAGE
    depth = norm_mix.shape[0]

    bias_p = bias_tiles(_prompt_attn_buckets(PROMPT_TQ), rel_bias_table, heads_on_lanes=True, scale=LOG2E)
    u_bias = bias_tiles(_prompt_cmp_buckets(PROMPT_TQ, t // CMP_STRIDE), rel_bias_table, heads_on_lanes=True,
                        scale=LOG2E)
    bias_s = bias_tiles_indexed(_decode_page_index(t_new, past_len), rel_bias_table)
    bias_sc = bias_tiles(_decode_cmp_buckets(t_new, past_len), rel_bias_table)
    n_wt = -(-(WINDOW + t_new) // LANE)
    bias_sw = bias_tiles(_decode_win_buckets(t_new, n_wt), rel_bias_table)

    mem_k, mem_v = mem_kv(mem_prompt, norm_mem, xattn_w_k, xattn_w_v, xattn_k_norm)
    xw = X_HEADS * X_DH
    win_k = cache_nsa_win_k.reshape(*cache_nsa_win_k.shape[:3], NSA_KV)
    win_v = cache_nsa_win_v.reshape(*cache_nsa_win_v.shape[:3], NSA_KV)

    dkp, dvp, dks, dvs = [], [], [], []
    ssp, cvp, sss, cvs = [], [], [], []
    nsp, nss = [], []
    for i in range(depth):
        kind, j = i % 3, i // 3
        if kind == 0:
            lam_init = 0.8 - 0.6 * math.exp(-0.3 * i)
            xp, xs, kp_, vp_, ks_, vs_ = diff_layer(
                xp, xs, norm_mix[i], diff_w_in, diff_q_norm[j], diff_k_norm[j], diff_lambda[j],
                diff_sub_norm[j], diff_w_out, cache_diff_k, cache_diff_v, j, page_table,
                bias_p, bias_s, lam_init)
            dkp.append(kp_)
            dvp.append(vp_)
            dks.append(ks_)
            dvs.append(vs_)
        elif kind == 1:
            xp, xs, hp_, cp_, hs_, cs_ = ssd_layer(
                xp, xs, norm_mix[i], ssm_w_in, j, ssm_conv_w[j], ssm_conv_b[j], ssm_dt_bias[j], ssm_a_log[j],
                ssm_d[j], ssm_norm[j], ssm_w_out, state_conv[j], state_ssm[j])
            ssp.append(hp_)
            cvp.append(cp_)
            sss.append(hs_)
            cvs.append(cs_)
        else:
            caches = (cache_nsa_cmp_k, cache_nsa_cmp_v, cache_nsa_slc_k, cache_nsa_slc_v, win_k, win_v)
            xp, xs, stp, sts = nsa_layer(
                xp, xs, norm_mix[i], nsa_w_in, nsa_q_norm[j], nsa_k_norm[j], nsa_cmp_pe[j], nsa_cmp_w1[j],
                nsa_cmp_w2[j], nsa_w_out, caches, j, page_table,
                bias_p, u_bias, bias_s, bias_sc, bias_sw, past_len)
            nsp.append(stp)
            nss.append(sts)
        xp = xattn(xp, norm_xattn[i], xattn_w_q, xattn_q_norm[i], mem_k[i], mem_v[i], xattn_w_o, i)
        xs = xattn(xs, norm_xattn[i], xattn_w_q, xattn_q_norm[i], cache_mem_k[i].reshape(bs, N_MEM, xw),
                   cache_mem_v[i].reshape(bs, N_MEM, xw), xattn_w_o, i)
        xp = ffn(xp.reshape(bp * t, d), norm_ffn[i], ffn_w1, ffn_w3, ffn_w2, i).reshape(bp, t, d)
        xs = ffn(xs.reshape(bs * t_new, d), norm_ffn[i], ffn_w1, ffn_w3, ffn_w2, i).reshape(bs, t_new, d)

    st = lambda xs_: jnp.stack(xs_, axis=0)
    nsp_t = [st([s[a] for s in nsp]) for a in range(6)]
    nss_t = [st([s[a] for s in nss]) for a in range(6)]
    mem_shape = (depth, bp, N_MEM, X_HEADS, X_DH)
    return (xp, xs,
            st(dkp), st(dvp), st(dks), st(dvs),
            st(ssp), st(cvp), st(sss), st(cvs),
            *nsp_t, *nss_t,
            mem_k.reshape(mem_shape), mem_v.reshape(mem_shape))
```

```python
import functools
import math

import numpy as np
import jax
import jax.numpy as jnp
from jax import lax
from jax.experimental import pallas as pl
from jax.experimental.pallas import tpu as pltpu

F32 = jnp.float32
BF16 = jnp.bfloat16

D_MODEL = 2048
DEPTH = 4
PAGE = 128
N_HEADS = 16
N_KV = 4
HPG = N_HEADS // N_KV
DIFF_DH = 64
HEAD_W = 128
N_BUCKETS = 32
MAX_EXACT = 16
MAX_DIST = 128
D_INNER = 2 * D_MODEL
SSM_HEADDIM = 64
SSM_HEADS = D_INNER // SSM_HEADDIM
SSM_GROUPS = 8
SSM_HPG = SSM_HEADS // SSM_GROUPS
D_STATE = 128
CONV_W = 4
CONV_DIM = D_INNER + 2 * SSM_GROUPS * D_STATE
SSM_CHUNK = 128
CMP_STRIDE = 16
SLC_BLOCK = 64
SLC_RATIO = SLC_BLOCK // CMP_STRIDE
N_SELECT = 16
WINDOW = 512
N_MEM = 256
X_HEADS = 4
X_DH = 128
EPS = 1e-6
NEG_INF = -1e30
FORCE_SCORE = 1e4

LANE = 128
VMEM_LIMIT = 56 * 1024 * 1024
DENSE_TALL_ROWS = 2048


def _params(sem):
    return pltpu.CompilerParams(dimension_semantics=sem, vmem_limit_bytes=VMEM_LIMIT)


def _rms(x, gain):
    return x * lax.rsqrt(jnp.mean(x * x, axis=-1, keepdims=True) + EPS) * gain


def _dot(a, b):
    return jnp.dot(a, b, preferred_element_type=F32)


def _dot_t(a, b):
    return lax.dot_general(a, b, (((1,), (1,)), ((), ())), preferred_element_type=F32)


def _split3(x):
    hi = x.astype(BF16)
    r1 = x - hi.astype(F32)
    mid = r1.astype(BF16)
    lo = (r1 - mid.astype(F32)).astype(BF16)
    return hi, mid, lo


def _dot_exact_rhs(x, m_bf16):
    hi, mid, lo = _split3(x)
    return _dot(hi, m_bf16) + _dot(mid, m_bf16) + _dot(lo, m_bf16)


def _norm64(blk, gain, lo):
    sq = blk * blk
    s_lo = jnp.sum(jnp.where(lo, sq, 0.0), axis=-1, keepdims=True)
    s_hi = jnp.sum(jnp.where(lo, 0.0, sq), axis=-1, keepdims=True)
    ms = jnp.where(lo, s_lo, s_hi) * (1.0 / DIFF_DH)
    return blk * lax.rsqrt(ms + EPS) * gain


def _head_norm_tile(y, gain, seg):
    lo = lax.broadcasted_iota(jnp.int32, (y.shape[0], LANE), 1) < DIFF_DH
    blocks = []
    for c in range(y.shape[1] // LANE):
        sl = slice(c * LANE, (c + 1) * LANE)
        blocks.append(_norm64(y[:, sl], gain[:, sl], lo) if seg == DIFF_DH else _rms(y[:, sl], gain[:, sl]))
    return jnp.concatenate(blocks, axis=1)


def _dense_kernel(*refs, norm, residual, w_t, head_seg, norm_tiles):
    refs = list(refs)
    x_ref = refs.pop(0)
    g_ref = refs.pop(0) if norm else None
    w_ref = refs.pop(0)
    r_ref = refs.pop(0) if residual else None
    hg_ref = refs.pop(0) if head_seg else None
    o_ref, xb_ref = refs
    j = pl.program_id(1)

    @pl.when(j == 0)
    def _():
        x = x_ref[...]
        if norm:
            x = _rms(x, g_ref[...])
        xb_ref[...] = x.astype(BF16)

    w = w_ref[0].astype(BF16)
    y = _dot_t(xb_ref[...], w) if w_t else _dot(xb_ref[...], w)
    if residual:
        y = y + r_ref[...]
    if not head_seg:
        o_ref[...] = y
        return
    is_norm = functools.reduce(jnp.logical_or, [j == t for t in norm_tiles])

    @pl.when(is_norm)
    def _():
        o_ref[...] = _head_norm_tile(y, hg_ref[0], head_seg)

    @pl.when(jnp.logical_not(is_norm))
    def _():
        o_ref[...] = y


def dense(x, w, layer, gain=None, res=None, n=None, w_t=False, head_norm=None):
    m, k = x.shape
    n = w.shape[1 if w_t else 2] if n is None else n
    tm = m if m <= 512 else (DENSE_TALL_ROWS if k <= D_MODEL and m % DENSE_TALL_ROWS == 0 else 512)
    tn = LANE if n % 256 else (512 if n % 512 == 0 else 256)
    assert m % tm == 0 and n % tn == 0
    norm, residual = gain is not None, res is not None
    x_mode = dict(pipeline_mode=pl.Buffered(1)) if tm == DENSE_TALL_ROWS else {}
    in_specs = [pl.BlockSpec((tm, k), lambda i, j: (i, 0), **x_mode)]
    args = [x]
    if norm:
        in_specs.append(pl.BlockSpec((1, k), lambda i, j: (0, 0)))
        args.append(gain.reshape(1, k))
    if w_t:
        in_specs.append(pl.BlockSpec((1, tn, k), lambda i, j: (layer, j, 0)))
    else:
        in_specs.append(pl.BlockSpec((1, k, tn), lambda i, j: (layer, 0, j)))
    args.append(w)
    if residual:
        in_specs.append(pl.BlockSpec((tm, tn), lambda i, j: (i, j)))
        args.append(res)
    head_seg, norm_tiles = 0, ()
    if head_norm is not None:
        head_seg, tile_gains = head_norm
        norm_tiles = tuple(sorted(tile_gains))
        ones = jnp.ones((tn,), F32)
        gains = jnp.stack([tile_gains.get(t, ones) for t in range(n // tn)]).reshape(n // tn, 1, tn)
        in_specs.append(pl.BlockSpec((1, 1, tn), lambda i, j: (j, 0, 0)))
        args.append(gains)
    return pl.pallas_call(
        functools.partial(_dense_kernel, norm=norm, residual=residual, w_t=w_t, head_seg=head_seg,
                          norm_tiles=norm_tiles),
        grid=(m // tm, n // tn),
        in_specs=in_specs,
        out_specs=pl.BlockSpec((tm, tn), lambda i, j: (i, j)),
        out_shape=jax.ShapeDtypeStruct((m, n), F32),
        scratch_shapes=[pltpu.VMEM((tm, k), BF16)],
        compiler_params=_params(("parallel", "arbitrary")),
        name="dense",
    )(*args)


def _ffn_kernel(x_ref, g_ref, w1_ref, w3_ref, w2_ref, o_ref, xb_ref):
    @pl.when(pl.program_id(1) == 0)
    def _():
        x = x_ref[...]
        xb_ref[...] = _rms(x, g_ref[...]).astype(BF16)
        o_ref[...] = x

    xb = xb_ref[...]
    h1 = _dot(xb, w1_ref[0].astype(BF16))
    h3 = _dot(xb, w3_ref[0].astype(BF16))
    a = (h1 * jax.nn.sigmoid(h1) * h3).astype(BF16)
    o_ref[...] += _dot(a, w2_ref[0].astype(BF16))


def ffn(x, gain, w1, w3, w2, layer):
    m, d = x.shape
    f = w1.shape[2]
    tm = min(m, 1024)
    tf = 256
    assert m % tm == 0 and f % tf == 0
    return pl.pallas_call(
        _ffn_kernel,
        grid=(m // tm, f // tf),
        in_specs=[
            pl.BlockSpec((tm, d), lambda i, j: (i, 0), pipeline_mode=pl.Buffered(1)),
            pl.BlockSpec((1, d), lambda i, j: (0, 0)),
            pl.BlockSpec((1, d, tf), lambda i, j: (layer, 0, j)),
            pl.BlockSpec((1, d, tf), lambda i, j: (layer, 0, j)),
            pl.BlockSpec((1, tf, d), lambda i, j: (layer, j, 0)),
        ],
        out_specs=pl.BlockSpec((tm, d), lambda i, j: (i, 0)),
        out_shape=jax.ShapeDtypeStruct((m, d), F32),
        scratch_shapes=[pltpu.VMEM((tm, d), BF16)],
        compiler_params=_params(("parallel", "arbitrary")),
        name="ffn",
    )(x, gain.reshape(1, d), w1, w3, w2)


def _mem_kv_kernel(mem_ref, g_ref, wk_ref, wv_ref, kn_ref, k_ref, v_ref):
    m = _rms(mem_ref[0], g_ref[0]).astype(BF16)
    k = _dot(m, wk_ref[0].astype(BF16))
    v_ref[0, 0] = _dot(m, wv_ref[0].astype(BF16))
    for h in range(X_HEADS):
        sl = slice(h * X_DH, (h + 1) * X_DH)
        k_ref[0, 0, :, sl] = _rms(k[:, sl], kn_ref[0])


def mem_kv(mem, g_mem, wk, wv, k_norm):
    b = mem.shape[0]
    nl = wk.shape[0]
    hw = X_HEADS * X_DH
    shape = jax.ShapeDtypeStruct((nl, b, N_MEM, hw), F32)
    return pl.pallas_call(
        _mem_kv_kernel,
        grid=(nl, b),
        in_specs=[
            pl.BlockSpec((1, N_MEM, D_MODEL), lambda l, i: (i, 0, 0)),
            pl.BlockSpec((1, 1, D_MODEL), lambda l, i: (l, 0, 0)),
            pl.BlockSpec((1, D_MODEL, hw), lambda l, i: (l, 0, 0)),
            pl.BlockSpec((1, D_MODEL, hw), lambda l, i: (l, 0, 0)),
            pl.BlockSpec((1, 1, X_DH), lambda l, i: (l, 0, 0)),
        ],
        out_specs=[pl.BlockSpec((1, 1, N_MEM, hw), lambda l, i: (l, i, 0, 0))] * 2,
        out_shape=[shape, shape],
        compiler_params=_params(("parallel", "parallel")),
        name="mem_kv",
    )(mem, g_mem.reshape(nl, 1, D_MODEL), wk, wv, k_norm.reshape(nl, 1, X_DH))


def _xattn_kernel(x_ref, g_ref, wq_ref, qn_ref, k_ref, v_ref, wo_ref, o_ref):
    nb, tm, d = x_ref.shape
    x = x_ref[...].reshape(nb * tm, d)
    q = _dot(_rms(x, g_ref[...]).astype(BF16), wq_ref[0].astype(BF16))
    rows = []
    for bb in range(nb):
        outs = []
        for h in range(X_HEADS):
            sl = slice(h * X_DH, (h + 1) * X_DH)
            qh = _rms(q[bb * tm:(bb + 1) * tm, sl], qn_ref[...]).astype(BF16)
            s = _dot_t(qh, k_ref[bb, :, sl].astype(BF16)) * (X_DH ** -0.5)
            e = jnp.exp(s - jnp.max(s, axis=-1, keepdims=True))
            p = e / jnp.sum(e, axis=-1, keepdims=True)
            outs.append(_dot(p.astype(BF16), v_ref[bb, :, sl].astype(BF16)))
        rows.append(jnp.concatenate(outs, axis=1))
    o = jnp.concatenate(rows, axis=0).astype(BF16)
    o_ref[...] = (x + _dot(o, wo_ref[0].astype(BF16))).reshape(nb, tm, d)


def xattn(x, gain, wq, q_norm, k, v, wo, layer):
    b, t, d = x.shape
    hw = X_HEADS * X_DH
    tm = min(t, 512)
    nb = b if t < LANE else 1
    return pl.pallas_call(
        _xattn_kernel,
        grid=(b // nb, t // tm),
        in_specs=[
            pl.BlockSpec((nb, tm, d), lambda i, j: (i, j, 0)),
            pl.BlockSpec((1, d), lambda i, j: (0, 0)),
            pl.BlockSpec((1, d, hw), lambda i, j: (layer, 0, 0)),
            pl.BlockSpec((1, X_DH), lambda i, j: (0, 0)),
            pl.BlockSpec((nb, N_MEM, hw), lambda i, j: (i, 0, 0)),
            pl.BlockSpec((nb, N_MEM, hw), lambda i, j: (i, 0, 0)),
            pl.BlockSpec((1, hw, d), lambda i, j: (layer, 0, 0)),
        ],
        out_specs=pl.BlockSpec((nb, tm, d), lambda i, j: (i, j, 0)),
        out_shape=jax.ShapeDtypeStruct((b, t, d), F32),
        compiler_params=_params(("parallel", "parallel")),
        name="xattn",
    )(x, gain.reshape(1, d), wq, q_norm.reshape(1, X_DH), k, v, wo)


def _bucket_np(dist):
    n = np.maximum(dist, 0)
    nf = np.maximum(n, 1).astype(np.float64)
    large = MAX_EXACT + (np.log(nf / MAX_EXACT) / math.log(MAX_DIST / MAX_EXACT)
                         * (N_BUCKETS - MAX_EXACT)).astype(np.int64)
    b = np.where(n < MAX_EXACT, n, np.minimum(large, N_BUCKETS - 1))
    return np.where(dist < 0, -1, b).astype(np.int32)


def _bias_kernel(bkt_ref, tab_ref, o_ref, *, scale):
    h = pl.program_id(1)
    b = bkt_ref[0]
    acc = jnp.full(b.shape, NEG_INF, F32)
    for k in range(N_BUCKETS):
        acc = jnp.where(b == k, tab_ref[k, h] * scale, acc)
    o_ref[0] = acc


def _bias_packed_kernel(idx_ref, tab_ref, o_ref, *, scale):
    idx = idx_ref[0]
    acc = jnp.full(idx.shape, NEG_INF, F32)
    for k in range(N_BUCKETS):
        for h in range(N_HEADS):
            acc = jnp.where(idx == k * N_HEADS + h, tab_ref[k, h] * scale, acc)
    o_ref[0] = acc


def bias_tiles_indexed(idx, table, scale=1.0):
    nt, r, w = idx.shape
    return pl.pallas_call(
        functools.partial(_bias_packed_kernel, scale=scale),
        grid=(nt,),
        in_specs=[
            pl.BlockSpec((1, r, w), lambda t: (t, 0, 0)),
            pl.BlockSpec(memory_space=pltpu.SMEM),
        ],
        out_specs=pl.BlockSpec((1, r, w), lambda t: (t, 0, 0)),
        out_shape=jax.ShapeDtypeStruct((nt, r, w), F32),
        compiler_params=_params(("parallel",)),
        name="bias_tiles_indexed",
    )(jnp.asarray(idx.astype(np.int32)), table)


def bias_tiles(buckets, table, heads_on_lanes=False, scale=1.0):
    nt, r, w = buckets.shape
    if heads_on_lanes:
        out_spec = pl.BlockSpec((1, r, w), lambda t, h: (t, 0, h))
        out_shape = (nt, r, N_HEADS * w)
    else:
        out_spec = pl.BlockSpec((1, r, w), lambda t, h: (t, h, 0))
        out_shape = (nt, N_HEADS * r, w)
    return pl.pallas_call(
        functools.partial(_bias_kernel, scale=scale),
        grid=(nt, N_HEADS),
        in_specs=[
            pl.BlockSpec((1, r, w), lambda t, h: (t, 0, 0)),
            pl.BlockSpec(memory_space=pltpu.SMEM),
        ],
        out_specs=out_spec,
        out_shape=jax.ShapeDtypeStruct(out_shape, F32),
        compiler_params=_params(("parallel", "parallel")),
        name="bias_tiles",
    )(jnp.asarray(buckets), table)


def _prompt_attn_buckets(tq):
    assert WINDOW % tq == 0 and tq >= MAX_DIST
    j = np.arange(tq)[:, None]
    i = np.arange(tq)[None, :]
    far = np.full((tq, tq), N_BUCKETS - 1, np.int32)
    return np.stack([
        _bucket_np(i - j),
        _bucket_np(i - j + tq),
        far,
        np.where(j > i, far, -1),
    ]).astype(np.int32)


def _diff_lambda(lam_ref, lam_init):
    lf = lam_ref[...]
    s01 = jnp.sum(lf[0:1] * lf[1:2], axis=-1, keepdims=True)
    s23 = jnp.sum(lf[2:3] * lf[3:4], axis=-1, keepdims=True)
    return jnp.exp(s01) - jnp.exp(s23) + lam_init


def _split_maps(qs):
    lo = lax.broadcasted_iota(jnp.int32, qs.shape, 1) < DIFF_DH
    return jnp.concatenate([jnp.where(lo, qs, 0.0), jnp.where(lo, 0.0, qs)], axis=0)


def _diff_finish_rows(o, lam, sub_norm, lam_init):
    r = o.shape[0] // 2
    a = o[:r] - lam * o[r:]
    return _rms(a, sub_norm) * (1.0 - lam_init)


PROMPT_TQ = 256


LOG2E = math.log2(math.e)
FLASH_CHUNKS = 4
FAR_WIDTHS = (4, 2, 1)


def _far_loops(n_far, body_of_width, tq):
    start = 0
    for w in FAR_WIDTHS:
        n = (n_far - start) // w
        lax.fori_loop(start // w, start // w + n, body_of_width(w * tq), 0)
        start = start + n * w


def _flash_tile_t(k_tile, vt_tile, qs_ref, m_ref, l_ref, acc_ref, scale=None, bias=None, far_bias=None,
                  mask=None, n_chunks=FLASH_CHUNKS):
    chunk = qs_ref.shape[0] // n_chunks
    logits = [_dot_t(k_tile, qs_ref[c * chunk:(c + 1) * chunk, :]) for c in range(n_chunks)]
    for c in range(n_chunks):
        cols = slice(c * chunk, (c + 1) * chunk)
        s = logits[c]
        if scale is not None:
            s = s * scale
        if bias is not None:
            s = s + bias(cols)
        if mask is not None:
            s = jnp.where(mask[:, cols] > 0.5, s, NEG_INF)
        m_old = m_ref[:, cols]
        s_max = jnp.max(s, axis=0, keepdims=True)
        if far_bias is None:
            m_new = jnp.maximum(m_old, s_max)
            p = jnp.exp2(s - m_new)
        else:
            fb = far_bias[:, cols]
            m_new = jnp.maximum(m_old, s_max + fb)
            p = jnp.exp2(s - (m_new - fb))
        alpha = jnp.exp2(m_old - m_new)
        l_ref[:, cols] = alpha * l_ref[:, cols] + jnp.sum(p, axis=0, keepdims=True)
        acc_ref[:, cols] = alpha * acc_ref[:, cols] + _dot(vt_tile, p.astype(BF16))
        m_ref[:, cols] = m_new


def _stage_kv(k_ref, v_ref, kb_ref, vt_ref, chunk):
    for c in range(k_ref.shape[1] // chunk):
        rows = slice(c * chunk, (c + 1) * chunk)
        kb_ref[rows, :] = k_ref[0, rows, :].astype(BF16)
        vt_ref[:, rows] = v_ref[0, rows, :].T.astype(BF16)


def _diff_flash_kernel(lam_ref, q_ref, k_ref, v_ref, bias_ref, sn_ref, o_ref,
                       qs_ref, kb_ref, vt_ref, m_ref, l_ref, acc_ref, *, tq, lam_init):
    qi = pl.program_id(2)

    @pl.when(qi == 0)
    def _():
        _stage_kv(k_ref, v_ref, kb_ref, vt_ref, tq)

    qb = q_ref[0]
    qs = jnp.concatenate([qb[:, r * HEAD_W:(r + 1) * HEAD_W] for r in range(HPG)], axis=0)
    qs_ref[...] = _split_maps(qs * (DIFF_DH ** -0.5 * LOG2E)).astype(BF16)
    _reset_flash(m_ref, l_ref, acc_ref)
    far = bias_ref[2, 0:1, :]
    far = jnp.concatenate([far, far], axis=1)

    def far_body(width):
        def body(kt, c):
            rows = pl.ds(pl.multiple_of(kt * width, width), width)
            _flash_tile_t(kb_ref[rows, :], vt_ref[:, rows], qs_ref, m_ref, l_ref, acc_ref, far_bias=far)
            return c
        return body

    def near_body(kt, c):
        rows = pl.ds(pl.multiple_of(kt * tq, tq), tq)
        b = bias_ref[qi - kt]
        _flash_tile_t(kb_ref[rows, :], vt_ref[:, rows], qs_ref, m_ref, l_ref, acc_ref,
                      bias=lambda cols: jnp.concatenate([b, b], axis=1), n_chunks=1)
        return c

    n_far = jnp.maximum(qi - 1, 0)
    _far_loops(n_far, far_body, tq)
    lax.fori_loop(n_far, qi + 1, near_body, 0)
    o = (acc_ref[...] / l_ref[...]).T
    a = _diff_finish_rows(o, _diff_lambda(lam_ref, lam_init), sn_ref[...], lam_init)
    for r in range(HPG):
        o_ref[0, :, r * HEAD_W:(r + 1) * HEAD_W] = a[r * tq:(r + 1) * tq]


def diff_flash(qkv, bias, lam_p, sub_norm, lam_init, tq=PROMPT_TQ):
    b, t, _ = qkv.shape
    gw = HPG * HEAD_W
    rows = 2 * HPG * tq
    k0 = N_HEADS
    return pl.pallas_call(
        functools.partial(_diff_flash_kernel, tq=tq, lam_init=lam_init),
        grid=(b, N_KV, t // tq),
        in_specs=[
            pl.BlockSpec((4, DIFF_DH), lambda i, g, j: (0, 0)),
            pl.BlockSpec((1, tq, gw), lambda i, g, j: (i, j, g)),
            pl.BlockSpec((1, t, HEAD_W), lambda i, g, j: (i, 0, k0 + g)),
            pl.BlockSpec((1, t, HEAD_W), lambda i, g, j: (i, 0, k0 + N_KV + g)),
            pl.BlockSpec((4, tq, HPG * tq), lambda i, g, j: (0, 0, g)),
            pl.BlockSpec((1, HEAD_W), lambda i, g, j: (0, 0)),
        ],
        out_specs=pl.BlockSpec((1, tq, gw), lambda i, g, j: (i, j, g)),
        out_shape=jax.ShapeDtypeStruct((b, t, N_HEADS * HEAD_W), F32),
        scratch_shapes=[
            pltpu.VMEM((rows, HEAD_W), BF16),
            pltpu.VMEM((t, HEAD_W), BF16),
            pltpu.VMEM((HEAD_W, t), BF16),
            pltpu.VMEM((1, rows), F32),
            pltpu.VMEM((1, rows), F32),
            pltpu.VMEM((HEAD_W, rows), F32),
        ],
        compiler_params=_params(("parallel", "parallel", "arbitrary")),
        name="diff_flash",
    )(lam_p, qkv, qkv, qkv, bias, sub_norm.reshape(1, HEAD_W))


PAGES_PER_STEP = 16


def _decode_page_index(t_new, past_len):
    j = np.arange(PAGE)[:, None]
    i = np.arange(t_new)[None, :]
    far = np.full((PAGE, t_new), N_BUCKETS - 1, np.int32)
    new = np.where(j < t_new, _bucket_np(i - j), -1)
    g = np.arange(N_KV)[None, :, None, None]
    h = np.arange(N_HEADS)[None, None, :, None]
    tiles = []
    for bkt in (far, _bucket_np(PAGE + i - j), new):
        b = bkt[:, None, None, :]
        idx = np.where((b >= 0) & (h // HPG == g), b * N_HEADS + h, -1)
        tiles.append(idx.reshape(PAGE * N_KV, N_HEADS * t_new))
    return np.stack(tiles).astype(np.int32)


def _stack_group_heads(q_ref, g):
    return jnp.concatenate(
        [q_ref[0, :, (g * HPG + r) * HEAD_W:(g * HPG + r + 1) * HEAD_W] for r in range(HPG)], axis=0)


def _query_rows(q_ref, maps, scale):
    pieces = []
    for m in range(maps):
        for h in range(N_HEADS):
            qh = q_ref[0, :, h * HEAD_W:(h + 1) * HEAD_W] * scale
            if maps == 2:
                lo = lax.broadcasted_iota(jnp.int32, qh.shape, 1) < DIFF_DH
                qh = jnp.where(lo, qh, 0.0) if m == 0 else jnp.where(lo, 0.0, qh)
            pieces.append(qh)
    return jnp.concatenate(pieces, axis=0).astype(BF16)


def _page_bias_t(bias_ref, first_page, n_pages, reps):
    tiles = [bias_ref[jnp.where(first_page + u == n_pages - 1, 1, 0)] for u in range(PAGES_PER_STEP)]
    b = jnp.concatenate(tiles, axis=0)
    return jnp.concatenate([b] * reps, axis=1)


def _decode_step_t(k_rows, v_rows, q_ref, bias, m_ref, l_ref, acc_ref, keep=None):
    s = _dot_t(k_rows.astype(BF16), q_ref[...]) + bias
    if keep is not None:
        s = jnp.where(keep > 0.5, s, NEG_INF)
    m_old = m_ref[...]
    m_new = jnp.maximum(m_old, jnp.max(s, axis=0, keepdims=True))
    alpha = jnp.exp(m_old - m_new)
    p = jnp.exp(s - m_new)
    l_ref[...] = alpha * l_ref[...] + jnp.sum(p, axis=0, keepdims=True)
    acc_ref[...] = alpha * acc_ref[...] + _dot(v_rows.T.astype(BF16), p.astype(BF16))
    m_ref[...] = m_new


def _page_rows(refs):
    return jnp.concatenate([r[0, 0] for r in refs], axis=0)


def _diff_decode_kernel(pt_ref, lam_ref, q_ref, kn_ref, vn_ref, *rest, n_pages, lam_init):
    del pt_ref
    kp = rest[:PAGES_PER_STEP]
    vp = rest[PAGES_PER_STEP:2 * PAGES_PER_STEP]
    bias_ref, sn_ref, o_ref, qr_ref, m_ref, l_ref, acc_ref = rest[2 * PAGES_PER_STEP:]
    s = pl.program_id(1)
    t_new = q_ref.shape[1]

    @pl.when(s == 0)
    def _():
        _reset_flash(m_ref, l_ref, acc_ref)
        qr_ref[...] = _query_rows(q_ref, 2, DIFF_DH ** -0.5)
        b = bias_ref[2]
        _decode_step_t(kn_ref[0], vn_ref[0], qr_ref, jnp.concatenate([b, b], axis=1), m_ref, l_ref, acc_ref)

    _decode_step_t(_page_rows(kp), _page_rows(vp), qr_ref,
                   _page_bias_t(bias_ref, s * PAGES_PER_STEP, n_pages, 2), m_ref, l_ref, acc_ref)

    @pl.when(s == pl.num_programs(1) - 1)
    def _():
        o = (acc_ref[...] / l_ref[...]).T
        a = _diff_finish_rows(o, _diff_lambda(lam_ref, lam_init), sn_ref[...], lam_init)
        for h in range(N_HEADS):
            o_ref[0, :, h * HEAD_W:(h + 1) * HEAD_W] = a[h * t_new:(h + 1) * t_new]


PAGE_ROWS = PAGE * N_KV


def _as_page_rows(cache):
    return cache.reshape(*cache.shape[:2], PAGE_ROWS, HEAD_W)


def _page_specs(layer, n):
    def spec(u):
        return pl.BlockSpec((1, 1, PAGE_ROWS, HEAD_W),
                            lambda i, s, pt: (layer, pt[i, s * PAGES_PER_STEP + u], 0, 0))
    return [spec(u) for u in range(PAGES_PER_STEP)] * n


def diff_decode(q, k_new, v_new, cache_k, cache_v, layer, page_table, bias, lam_p, sub_norm, lam_init):
    b, t_new, _ = q.shape
    n_pages = page_table.shape[1]
    lanes = 2 * N_HEADS * t_new
    assert N_HEADS * t_new == LANE
    fixed = lambda *shape: pl.BlockSpec(shape, lambda i, s, pt: (0,) * len(shape))
    per_b = lambda *shape: pl.BlockSpec((1,) + shape, lambda i, s, pt: (i,) + (0,) * len(shape))
    grid_spec = pltpu.PrefetchScalarGridSpec(
        num_scalar_prefetch=1,
        grid=(b, n_pages // PAGES_PER_STEP),
        in_specs=[fixed(4, DIFF_DH), per_b(t_new, N_HEADS * HEAD_W), per_b(PAGE_ROWS, HEAD_W),
                  per_b(PAGE_ROWS, HEAD_W)]
        + _page_specs(layer, 2)
        + [fixed(*bias.shape), fixed(1, HEAD_W)],
        out_specs=per_b(t_new, N_HEADS * HEAD_W),
        scratch_shapes=[
            pltpu.VMEM((lanes, HEAD_W), BF16),
            pltpu.VMEM((1, lanes), F32),
            pltpu.VMEM((1, lanes), F32),
            pltpu.VMEM((HEAD_W, lanes), F32),
        ],
    )
    return pl.pallas_call(
        functools.partial(_diff_decode_kernel, n_pages=n_pages, lam_init=lam_init),
        grid_spec=grid_spec,
        out_shape=jax.ShapeDtypeStruct(q.shape[:2] + (N_HEADS * HEAD_W,), F32),
        compiler_params=_params(("parallel", "arbitrary")),
        name="diff_decode",
    )(page_table, lam_p, q, k_new, v_new, *([cache_k] * PAGES_PER_STEP), *([cache_v] * PAGES_PER_STEP),
      bias, sub_norm.reshape(1, HEAD_W))


def _pad_rows(a, n):
    return jnp.pad(a, ((0, 0), (0, n - a.shape[1]), (0, 0)))


def _group_rows_kernel(x_ref, o_ref):
    tm = x_ref.shape[0]
    for g in range(N_KV):
        o_ref[pl.ds(g, tm, stride=N_KV), :] = x_ref[:, g * HEAD_W:(g + 1) * HEAD_W]


def group_rows(x, col_block):
    m = x.shape[0]
    tm = min(m, 512)
    assert m % tm == 0
    return pl.pallas_call(
        _group_rows_kernel,
        grid=(m // tm,),
        in_specs=[pl.BlockSpec((tm, N_KV * HEAD_W), lambda i: (i, col_block))],
        out_specs=pl.BlockSpec((tm * N_KV, HEAD_W), lambda i: (i, 0)),
        out_shape=jax.ShapeDtypeStruct((m * N_KV, HEAD_W), F32),
        compiler_params=_params(("parallel",)),
        name="group_rows",
    )(x)


def _new_page_rows(rows, b):
    return _pad_rows(rows.reshape(b, -1, HEAD_W), PAGE_ROWS)


def diff_layer(xp, xs, gain, w_in, q_norm, k_norm, lam_p, sub_norm, w_out, cache_k, cache_v, layer,
               page_table, bias_p, bias_s, lam_init):
    nq, nk = N_HEADS * HEAD_W, N_KV * HEAD_W
    tile = 4 * LANE
    q_gain, k_gain = jnp.tile(q_norm, tile // DIFF_DH), jnp.tile(k_norm, tile // DIFF_DH)
    head_norm = (DIFF_DH, {**{c: q_gain for c in range(nq // tile)}, nq // tile: k_gain})
    outs = []
    for x, paged in ((xp, False), (xs, True)):
        b, t, d = x.shape
        x2 = x.reshape(b * t, d)
        qkv2 = dense(x2, w_in, layer, gain=gain, head_norm=head_norm)
        qkv = qkv2.reshape(b, t, -1)
        k_rows, v_rows = group_rows(qkv2, nq // nk), group_rows(qkv2, nq // nk + 1)
        if paged:
            o = diff_decode(qkv, _new_page_rows(k_rows, b), _new_page_rows(v_rows, b), _as_page_rows(cache_k),
                            _as_page_rows(cache_v), layer, page_table, bias_s, lam_p, sub_norm, lam_init)
        else:
            o = diff_flash(qkv, bias_p, lam_p, sub_norm, lam_init)
        y = dense(o.reshape(b * t, -1), w_out, layer, res=x2).reshape(b, t, d)
        outs.append((y, k_rows.reshape(b, t, N_KV, HEAD_W), v_rows.reshape(b, t, N_KV, HEAD_W)))
    (yp, kp, vp), (ys, ks, vs) = outs
    return yp, ys, kp, vp, ks, vs


SSM_GW = SSM_HPG * SSM_HEADDIM
SSM_BC = 2 * SSM_GROUPS * D_STATE
CONV_PAD = 8


def _conv_silu(buf_ref, w_ref, b_ref, n):
    acc = b_ref[...]
    for k in range(CONV_W):
        acc = acc + buf_ref[pl.ds(CONV_PAD - (CONV_W - 1) + k, n), :] * w_ref[k:k + 1, :]
    return acc * jax.nn.sigmoid(acc)


def _ssd_kernel(z_ref, x_ref, bc_ref, dt_ref, cbx_ref, cbbc_ref, wx_ref, wbc_ref, bx_ref, bbc_ref,
                dtb_ref, alog_ref, dsk_ref, ng_ref, e_ref, tri_ref, h0_ref,
                y_ref, hout_ref, ht_ref, xbuf_ref, bcbuf_ref, xa_ref, bca_ref, *, t_valid):
    c = pl.program_id(1)
    n = x_ref.shape[1]

    @pl.when(c == 0)
    def _():
        for g in range(SSM_GROUPS):
            ht_ref[g] = h0_ref[0, g].T
        xbuf_ref[0:CONV_PAD] = cbx_ref[0]
        bcbuf_ref[0:CONV_PAD] = cbbc_ref[0]

    xbuf_ref[CONV_PAD:CONV_PAD + n] = x_ref[0]
    bcbuf_ref[CONV_PAD:CONV_PAD + n] = bc_ref[0]
    xa_ref[...] = _conv_silu(xbuf_ref, wx_ref, bx_ref, n)
    bca_ref[...] = _conv_silu(bcbuf_ref, wbc_ref, bbc_ref, n)
    xbuf_ref[0:CONV_PAD] = xbuf_ref[n:n + CONV_PAD]
    bcbuf_ref[0:CONV_PAD] = bcbuf_ref[n:n + CONV_PAD]

    row = lax.broadcasted_iota(jnp.int32, (n, LANE), 0) + c * n
    dtr = dt_ref[0] + dtb_ref[...]
    dt = jnp.maximum(dtr, 0.0) + jnp.log1p(jnp.exp(-jnp.abs(dtr)))
    dt = jnp.where(row < t_valid, dt, 0.0)
    dta = dt * (-jnp.exp(alog_ref[...]))
    hi, mid, lo = _split3(dta)
    tri = tri_ref[...]
    cs = _dot(tri, hi) + _dot(tri, mid) + _dot(tri, lo)
    cs_last = cs[n - 1:n, :]
    cs_t = cs.T
    dt_t = dt.T
    stacked = jnp.concatenate(
        [jnp.exp(cs), jnp.exp(cs_last - cs) * dt, jnp.broadcast_to(jnp.exp(cs_last), (8, LANE))], axis=0)
    ex = _dot_exact_rhs(stacked, e_ref[...])
    causal = (lax.broadcasted_iota(jnp.int32, (n, n), 0) >= lax.broadcasted_iota(jnp.int32, (n, n), 1))

    for g in range(SSM_GROUPS):
        gs = slice(g * SSM_GW, (g + 1) * SSM_GW)
        bm = bca_ref[:, g * D_STATE:(g + 1) * D_STATE]
        cm = bca_ref[:, (SSM_GROUPS + g) * D_STATE:(SSM_GROUPS + g + 1) * D_STATE].astype(BF16)
        cb = _dot_t(cm, bm.astype(BF16))
        xg = xa_ref[:, gs]
        ys = []
        for r in range(SSM_HPG):
            h = g * SSM_HPG + r
            seg = cs[:, h:h + 1] - cs_t[h:h + 1, :]
            dec = jnp.where(causal, jnp.exp(jnp.where(causal, seg, 0.0)), 0.0)
            mm = (cb * dec * dt_t[h:h + 1, :]).astype(BF16)
            ys.append(_dot(mm, xg[:, r * SSM_HEADDIM:(r + 1) * SSM_HEADDIM].astype(BF16)))
        ht = ht_ref[g]
        y = jnp.concatenate(ys, axis=1) + _dot(cm, ht.astype(BF16)) * ex[0:n, gs]
        y = y + dsk_ref[:, gs] * xg
        zg = z_ref[0, :, gs]
        y = y * (zg * jax.nn.sigmoid(zg))
        y_ref[0, :, gs] = _rms(y, ng_ref[:, gs])
        xw = (xg * ex[n:2 * n, gs]).astype(BF16)
        ht_ref[g] = ht * ex[2 * n:2 * n + 1, gs] + _dot(bm.T.astype(BF16), xw)

    @pl.when(c == pl.num_programs(1) - 1)
    def _():
        for g in range(SSM_GROUPS):
            hout_ref[0, g] = ht_ref[g].T


def _head_expand_matrix():
    e = np.zeros((LANE, D_INNER), np.float32)
    for h in range(SSM_HEADS):
        e[h, h * SSM_HEADDIM:(h + 1) * SSM_HEADDIM] = 1.0
    return e


def ssd_core(zx, dt_raw, conv_buf, h0, conv_w, conv_b, dt_bias, a_log, d_skip, norm_g, t_valid):
    b, t, _ = zx.shape
    n = SSM_CHUNK
    pad_h = LANE - SSM_HEADS
    cb = jnp.pad(conv_buf, ((0, 0), (CONV_PAD - (CONV_W - 1), 0), (0, 0)))
    tri = jnp.asarray(np.tril(np.ones((n, n), np.float32)), BF16)
    e = jnp.asarray(_head_expand_matrix(), BF16)
    d_exp = jnp.repeat(d_skip, SSM_HEADDIM).reshape(1, D_INNER)
    fixed = lambda *shape: pl.BlockSpec(shape, lambda i, c: (0,) * len(shape))
    per_b = lambda *shape: pl.BlockSpec((1,) + shape, lambda i, c: (i,) + (0,) * len(shape))
    y, h_last = pl.pallas_call(
        functools.partial(_ssd_kernel, t_valid=t_valid),
        grid=(b, t // n),
        in_specs=[
            pl.BlockSpec((1, n, D_INNER), lambda i, c: (i, c, 0)),
            pl.BlockSpec((1, n, D_INNER), lambda i, c: (i, c, 1)),
            pl.BlockSpec((1, n, SSM_BC), lambda i, c: (i, c, 2 * D_INNER // SSM_BC)),
            pl.BlockSpec((1, n, LANE), lambda i, c: (i, c, 0)),
            per_b(CONV_PAD, D_INNER), per_b(CONV_PAD, SSM_BC),
            fixed(CONV_W, D_INNER), fixed(CONV_W, SSM_BC), fixed(1, D_INNER), fixed(1, SSM_BC),
            fixed(1, LANE), fixed(1, LANE), fixed(1, D_INNER), fixed(1, D_INNER),
            fixed(LANE, D_INNER), fixed(n, n),
            per_b(SSM_GROUPS, SSM_GW, D_STATE),
        ],
        out_specs=[
            pl.BlockSpec((1, n, D_INNER), lambda i, c: (i, c, 0)),
            per_b(SSM_GROUPS, SSM_GW, D_STATE),
        ],
        out_shape=[jax.ShapeDtypeStruct((b, t, D_INNER), F32),
                   jax.ShapeDtypeStruct((b, SSM_GROUPS, SSM_GW, D_STATE), F32)],
        scratch_shapes=[
            pltpu.VMEM((SSM_GROUPS, D_STATE, SSM_GW), F32),
            pltpu.VMEM((n + CONV_PAD, D_INNER), F32),
            pltpu.VMEM((n + CONV_PAD, SSM_BC), F32),
            pltpu.VMEM((n, D_INNER), F32),
            pltpu.VMEM((n, SSM_BC), F32),
        ],
        compiler_params=_params(("parallel", "arbitrary")),
        name="ssd_core",
    )(zx, zx, zx, dt_raw, cb[:, :, :D_INNER], cb[:, :, D_INNER:],
      conv_w[:, :D_INNER], conv_w[:, D_INNER:], conv_b[:D_INNER].reshape(1, -1), conv_b[D_INNER:].reshape(1, -1),
      jnp.pad(dt_bias, (0, pad_h)).reshape(1, LANE), jnp.pad(a_log, (0, pad_h)).reshape(1, LANE),
      d_exp, norm_g.reshape(1, D_INNER), e, tri,
      h0.reshape(b, SSM_GROUPS, SSM_GW, D_STATE))
    return y, h_last.reshape(b, SSM_HEADS, SSM_HEADDIM, D_STATE)


def ssd_layer(xp, xs, gain, w_in, layer, conv_w, conv_b, dt_bias, a_log, d_skip, norm_g, w_out,
              state_conv, state_ssm):
    nzx = D_INNER + CONV_DIM
    w_dt = jnp.pad(w_in[layer, :, nzx:], ((0, 0), (0, LANE - SSM_HEADS)))[None]
    w_in_t = jnp.swapaxes(w_in, 1, 2)
    outs = []
    for x, conv_buf, h0 in ((xp, None, None), (xs, state_conv, state_ssm)):
        b, t, d = x.shape
        x2 = x.reshape(b * t, d)
        if conv_buf is None:
            conv_buf = jnp.zeros((b, CONV_W - 1, CONV_DIM), F32)
            h0 = jnp.zeros((b, SSM_HEADS, SSM_HEADDIM, D_STATE), F32)
        zx = dense(x2, w_in_t, layer, gain=gain, n=nzx, w_t=True).reshape(b, t, -1)
        dt_raw = dense(x2, w_dt, 0, gain=gain).reshape(b, t, LANE)
        tp = -(-t // SSM_CHUNK) * SSM_CHUNK
        y, h_last = ssd_core(_pad_rows(zx, tp), _pad_rows(dt_raw, tp), conv_buf, h0, conv_w, conv_b,
                             dt_bias, a_log, d_skip, norm_g, t)
        y = y[:, :t].reshape(b * t, D_INNER)
        tail = min(t, CONV_W - 1)
        conv_out = jnp.concatenate([conv_buf[:, tail:], zx[:, t - tail:, D_INNER:]], axis=1)
        outs.append((dense(y, w_out, layer, res=x2).reshape(b, t, d), h_last, conv_out))
    (yp, hp, cp), (ys, hs, cs) = outs
    return yp, ys, hp, cp, hs, cs


NSA_KV = N_KV * HEAD_W
NSA_SCALE = HEAD_W ** -0.5
CMP_PAIRS = CMP_STRIDE // 2
CMP_PAGES = 32


def _cmp_uv_kernel(pt_ref, *refs):
    del pt_ref
    pages = refs[:-2]
    wab_ref, o_ref = refs[-2:]
    subs = PAGE // CMP_STRIDE
    for g in range(N_KV):
        acc = jnp.zeros((len(pages) * subs, 2 * HEAD_W), F32)
        for lp in range(CMP_PAIRS):
            halves = []
            for li in range(2):
                rows = pl.ds((2 * lp + li) * N_KV + g, subs, stride=CMP_STRIDE * N_KV)
                halves.append(jnp.concatenate([p[0, 0, rows, :] for p in pages], axis=0))
            acc = acc + _dot(jnp.concatenate(halves, axis=1).astype(BF16), wab_ref[lp])
        o_ref[0, :, g * 2 * HEAD_W:(g + 1) * 2 * HEAD_W] = acc


def cmp_uv(rows, layer, page_table, wab):
    b, n_pages = page_table.shape
    subs = PAGE // CMP_STRIDE
    step_pages = min(CMP_PAGES, n_pages)
    assert n_pages % step_pages == 0

    def spec(u):
        return pl.BlockSpec((1, 1, PAGE_ROWS, HEAD_W),
                            lambda i, c, pt: (layer, pt[i, c * step_pages + u], 0, 0))

    grid_spec = pltpu.PrefetchScalarGridSpec(
        num_scalar_prefetch=1,
        grid=(b, n_pages // step_pages),
        in_specs=[spec(u) for u in range(step_pages)]
        + [pl.BlockSpec(wab.shape, lambda i, c, pt: (0, 0, 0))],
        out_specs=pl.BlockSpec((1, step_pages * subs, 2 * NSA_KV), lambda i, c, pt: (i, c, 0)),
    )
    return pl.pallas_call(
        _cmp_uv_kernel,
        grid_spec=grid_spec,
        out_shape=jax.ShapeDtypeStruct((b, n_pages * subs, 2 * NSA_KV), F32),
        compiler_params=_params(("parallel", "arbitrary")),
        name="cmp_uv",
    )(page_table, *([rows] * step_pages), wab)


def _cmp_finish_kernel(uv_ref, pe_ref, w1_ref, w2_ref, kn_ref, o_ref, *, norm):
    n = uv_ref.shape[1]
    c = _dot(jnp.broadcast_to(pe_ref[...], (8, pe_ref.shape[1])).astype(BF16), w1_ref[...])[0:1]
    for g in range(N_KV):
        u = uv_ref[0, :, g * 2 * HEAD_W:g * 2 * HEAD_W + HEAD_W]
        v = uv_ref[0, :, g * 2 * HEAD_W + HEAD_W:(g + 1) * 2 * HEAD_W]
        pre = u + pltpu.roll(v, n - 1, axis=0) + c
        out = _dot((pre * jax.nn.sigmoid(pre)).astype(BF16), w2_ref[...])
        if norm:
            out = _rms(out, kn_ref[...])
        o_ref[0, :, g * HEAD_W:(g + 1) * HEAD_W] = out


def cmp_finish(uv, pe, w1, w2, k_norm):
    b, n, _ = uv.shape
    norm = k_norm is not None
    kn = (k_norm if norm else jnp.ones((HEAD_W,), F32)).reshape(1, HEAD_W)
    return pl.pallas_call(
        functools.partial(_cmp_finish_kernel, norm=norm),
        grid=(b,),
        in_specs=[
            pl.BlockSpec((1, n, 2 * NSA_KV), lambda i: (i, 0, 0)),
            pl.BlockSpec((1, pe.size), lambda i: (0, 0)),
            pl.BlockSpec(w1.shape, lambda i: (0, 0)),
            pl.BlockSpec(w2.shape, lambda i: (0, 0)),
            pl.BlockSpec((1, HEAD_W), lambda i: (0, 0)),
        ],
        out_specs=pl.BlockSpec((1, n, NSA_KV), lambda i: (i, 0, 0)),
        out_shape=jax.ShapeDtypeStruct((b, n, NSA_KV), F32),
        compiler_params=_params(("parallel",)),
        name="cmp_finish",
    )(uv, pe.reshape(1, -1), w1, w2, kn)


def compress(rows, layer, page_table, pe, w1, w2, k_norm):
    w1r = w1.reshape(2, CMP_STRIDE, HEAD_W, HEAD_W)
    wab = jnp.transpose(w1r, (1, 2, 0, 3)).reshape(CMP_PAIRS, 2 * HEAD_W, 2 * HEAD_W).astype(BF16)
    uv = cmp_uv(rows, layer, page_table, wab)
    return cmp_finish(uv, pe, w1.astype(BF16), w2.astype(BF16), k_norm)


def _importance_matrix(n_cmp, n_rows, n_slc, n_cols):
    w = np.zeros((n_rows, n_cols), np.float32)
    w_imp = [1.0] + [2.0] * (SLC_RATIO - 1) + [1.0]
    for s in range(n_slc):
        for m, wm in enumerate(w_imp):
            j = SLC_RATIO * s + m - 1
            if 0 <= j < n_cmp:
                w[j, s] += wm
    return w


def _select_blocks(s_slc, q_pos0, n_slc):
    t, w = s_slc.shape
    blk = lax.broadcasted_iota(jnp.int32, (t, w), 1)
    qpos = q_pos0 + lax.broadcasted_iota(jnp.int32, (t, w), 0)
    qb = qpos // SLC_BLOCK
    forced = (blk == 0) | (blk == qb) | (blk == qb - 1)
    score = jnp.where(forced, FORCE_SCORE, jnp.where(blk * SLC_BLOCK <= qpos, s_slc, -1.0))
    score = jnp.where(blk < n_slc, score, -2.0)
    cnt = jnp.zeros((t, w), F32)
    for sp in range(n_slc):
        col = score[:, sp:sp + 1]
        tie = jnp.where(blk > sp, 1.0, 0.0)
        cnt = cnt + jnp.where(col > score, 1.0, jnp.where(col == score, tie, 0.0))
    return jnp.where(cnt < N_SELECT, 1.0, 0.0)


def _masked_softmax(s, valid, axis=-1, base2=False):
    m = jnp.max(s, axis=axis, keepdims=True)
    e = jnp.where(valid, (jnp.exp2 if base2 else jnp.exp)(s - m), 0.0)
    return e / jnp.maximum(jnp.sum(e, axis=axis, keepdims=True), 1e-30)


def _reset_flash(m_ref, l_ref, acc_ref):
    m_ref[...] = jnp.full(m_ref.shape, NEG_INF, F32)
    l_ref[...] = jnp.zeros(l_ref.shape, F32)
    acc_ref[...] = jnp.zeros(acc_ref.shape, F32)


def _select_blocks_t(s_slc, q_pos0, n_slc):
    w, t = s_slc.shape
    n8 = -(-n_slc // 8) * 8
    blk = lax.broadcasted_iota(jnp.int32, (n8, t), 0)
    qpos = q_pos0 + lax.broadcasted_iota(jnp.int32, (n8, t), 1)
    qb = qpos // SLC_BLOCK
    forced = (blk == 0) | (blk == qb) | (blk == qb - 1)
    score = jnp.where(forced, FORCE_SCORE, jnp.where(blk * SLC_BLOCK <= qpos, s_slc[:n8], -1.0))
    score = jnp.where(blk < n_slc, score, -2.0)
    cnt = jnp.zeros((n8, t), F32)
    for sp in range(n_slc):
        row = score[sp:sp + 1, :]
        tie = jnp.where(blk > sp, 1.0, 0.0)
        cnt = cnt + jnp.where(row > score, 1.0, jnp.where(row == score, tie, 0.0))
    sel = jnp.where(cnt < N_SELECT, 1.0, 0.0)
    if n8 < w:
        sel = jnp.concatenate([sel, jnp.zeros((w - n8, t), F32)], axis=0)
    return sel


def _nsa_prompt_kernel(q_ref, gt_ref, kc_ref, vc_ref, ks_ref, vs_ref, kw_ref, vw_ref, bias_ref, u_ref, wimp_ref,
                       o_ref, qs_ref, ksb_ref, vst_ref, kwb_ref, vwt_ref, sel_ref, m_ref, l_ref, acc_ref,
                       *, tq, n_slc):
    qi = pl.program_id(2)

    @pl.when(qi == 0)
    def _():
        _stage_kv(ks_ref, vs_ref, ksb_ref, vst_ref, tq)
        _stage_kv(kw_ref, vw_ref, kwb_ref, vwt_ref, tq)

    qs_ref[...] = jnp.concatenate(
        [q_ref[0, :, r * HEAD_W:(r + 1) * HEAD_W] for r in range(HPG)], axis=0).astype(BF16)

    n_pad = kc_ref.shape[1]
    start = pl.multiple_of(n_pad - (tq // CMP_STRIDE) * (qi + 1), 8)
    scale = NSA_SCALE * LOG2E
    bias_c = u_ref[0, pl.ds(start, n_pad), :]
    s = _dot_t(kc_ref[0].astype(BF16), qs_ref[...]) * scale + bias_c
    pc = _masked_softmax(s, bias_c > 0.5 * NEG_INF, axis=0, base2=True)
    o_cmp = _dot(vc_ref[0].T.astype(BF16), pc.astype(BF16))
    psum = pc[:, 0:tq] + pc[:, tq:2 * tq] + pc[:, 2 * tq:3 * tq] + pc[:, 3 * tq:4 * tq]
    hi, mid, lo = _split3(psum)
    wimp = wimp_ref[...]
    s_slc = _dot(wimp, hi) + _dot(wimp, mid) + _dot(wimp, lo)
    sel_ref[...] = _select_blocks_t(s_slc, qi * tq, n_slc).astype(BF16)

    _reset_flash(m_ref, l_ref, acc_ref)

    far = bias_ref[2, 0:1, :]

    def key_mask(kt, width):
        key = lax.broadcasted_iota(jnp.int32, (width, LANE), 0)
        blk = lax.broadcasted_iota(jnp.int32, (width, LANE), 1)
        onehot = jnp.where(blk == kt * (width // SLC_BLOCK) + key // SLC_BLOCK, 1.0, 0.0).astype(BF16)
        return jnp.concatenate([_dot(onehot, sel_ref[...])] * HPG, axis=1)

    def slc_far_body(width):
        def body(kt, c):
            rows = pl.ds(pl.multiple_of(kt * width, width), width)
            _flash_tile_t(ksb_ref[rows, :], vst_ref[:, rows], qs_ref, m_ref, l_ref, acc_ref, scale=scale,
                          far_bias=far, mask=key_mask(kt, width))
            return c
        return body

    def slc_near_body(kt, c):
        rows = pl.ds(pl.multiple_of(kt * tq, tq), tq)
        typ = qi - kt
        _flash_tile_t(ksb_ref[rows, :], vst_ref[:, rows], qs_ref, m_ref, l_ref, acc_ref, scale=scale,
                      bias=lambda cols: bias_ref[typ, :, cols], mask=key_mask(kt, tq))
        return c

    n_far = jnp.maximum(qi - 1, 0)
    _far_loops(n_far, slc_far_body, tq)
    lax.fori_loop(n_far, qi + 1, slc_near_body, 0)
    o_slc = acc_ref[...] / l_ref[...]

    _reset_flash(m_ref, l_ref, acc_ref)
    nw = WINDOW // tq

    def win_body(kt, c):
        rows = pl.ds(pl.multiple_of(kt * tq, tq), tq)
        t = qi - kt
        typ = jnp.where(t == nw, 3, jnp.minimum(t, 2))
        _flash_tile_t(kwb_ref[rows, :], vwt_ref[:, rows], qs_ref, m_ref, l_ref, acc_ref, scale=scale,
                      bias=lambda cols: bias_ref[typ, :, cols])
        return c

    lax.fori_loop(jnp.maximum(qi - nw, 0), qi + 1, win_body, 0)
    o_win = acc_ref[...] / l_ref[...]

    sig = jax.nn.sigmoid(gt_ref[0]).T
    for r in range(HPG):
        cs = slice(r * tq, (r + 1) * tq)
        o_t = (sig[r:r + 1, :] * o_cmp[:, cs] + sig[HPG + r:HPG + r + 1, :] * o_slc[:, cs]
               + sig[2 * HPG + r:2 * HPG + r + 1, :] * o_win[:, cs])
        o_ref[0, :, r * HEAD_W:(r + 1) * HEAD_W] = o_t.T


def _prompt_cmp_buckets(tq, n_pad):
    jp = np.arange(2 * n_pad)[:, None] - (n_pad - tq // CMP_STRIDE)
    i = np.arange(tq)[None, :]
    return _bucket_np(i - CMP_STRIDE * jp - (2 * CMP_STRIDE - 1))[None].astype(np.int32)


def nsa_prompt(proj, gates, kc, vc, bias, u_bias, tq=PROMPT_TQ):
    b, t, _ = proj.shape
    gw = HPG * HEAD_W
    n_pad = kc.shape[1]
    n_slc = t // SLC_BLOCK
    wimp = jnp.asarray(_importance_matrix(n_pad - 1, n_pad, n_slc, LANE).T, BF16)
    rows = HPG * tq
    seq = lambda n: pl.BlockSpec((1, n, HEAD_W), lambda i, g, j: (i, 0, g))
    kv = lambda a: pl.BlockSpec((1, t, HEAD_W), lambda i, g, j: (i, 0, N_HEADS + a * N_KV + g))
    return pl.pallas_call(
        functools.partial(_nsa_prompt_kernel, tq=tq, n_slc=n_slc),
        grid=(b, N_KV, t // tq),
        in_specs=[
            pl.BlockSpec((1, tq, gw), lambda i, g, j: (i, j, g)),
            pl.BlockSpec((1, tq, LANE), lambda i, g, j: (i, j, g)),
            seq(n_pad), seq(n_pad), kv(2), kv(3), kv(4), kv(5),
            pl.BlockSpec((4, tq, rows), lambda i, g, j: (0, 0, g)),
            pl.BlockSpec((1, 2 * n_pad, rows), lambda i, g, j: (0, 0, g)),
            pl.BlockSpec(wimp.shape, lambda i, g, j: (0, 0)),
        ],
        out_specs=pl.BlockSpec((1, tq, gw), lambda i, g, j: (i, j, g)),
        out_shape=jax.ShapeDtypeStruct((b, t, N_HEADS * HEAD_W), F32),
        scratch_shapes=[
            pltpu.VMEM((rows, HEAD_W), BF16),
            pltpu.VMEM((t, HEAD_W), BF16),
            pltpu.VMEM((HEAD_W, t), BF16),
            pltpu.VMEM((t, HEAD_W), BF16),
            pltpu.VMEM((HEAD_W, t), BF16),
            pltpu.VMEM((LANE, tq), BF16),
            pltpu.VMEM((1, rows), F32),
            pltpu.VMEM((1, rows), F32),
            pltpu.VMEM((HEAD_W, rows), F32),
        ],
        compiler_params=_params(("parallel", "parallel", "arbitrary")),
        name="nsa_prompt",
    )(proj, gates, kc, vc, proj, proj, proj, proj, bias, u_bias, wimp)


def _decode_cmp_buckets(t_new, past_len):
    n_pad = past_len // CMP_STRIDE
    i = np.arange(t_new)[:, None]
    j = n_pad - LANE + np.arange(LANE)[None, :]
    last = _bucket_np(past_len + i - CMP_STRIDE * j - (2 * CMP_STRIDE - 1))
    last = np.where(j < n_pad - 1, last, -1)
    return np.stack([np.full((t_new, LANE), N_BUCKETS - 1, np.int32), last]).astype(np.int32)


def _decode_win_buckets(t_new, n_tiles):
    i = np.arange(t_new)[:, None]
    idx = np.arange(n_tiles * LANE)[None, :]
    dw = WINDOW + i - idx
    ok = (dw >= 0) & (dw < WINDOW) & (idx < WINDOW + t_new)
    b = np.where(ok, _bucket_np(dw), -1)
    return np.stack([b[:, k * LANE:(k + 1) * LANE] for k in range(n_tiles)]).astype(np.int32)


def _nsa_decode_kernel(pt_ref, q_ref, gt_ref, kc_ref, vc_ref, kn_ref, vn_ref, kw_ref, vw_ref, *rest,
                       n_pages, past_len):
    del pt_ref
    kp = rest[:PAGES_PER_STEP]
    vp = rest[PAGES_PER_STEP:2 * PAGES_PER_STEP]
    (bias_ref, bias_c_ref, bias_w_ref, wimp_ref, o_ref,
     qr_ref, selt_ref, ocmp_ref, owin_ref, m_ref, l_ref, acc_ref) = rest[2 * PAGES_PER_STEP:]
    s = pl.program_id(1)
    t_new = q_ref.shape[1]
    rows = HPG * t_new
    n_slc = -(-(past_len + t_new) // SLC_BLOCK)

    @pl.when(s == 0)
    def _():
        _reset_flash(m_ref, l_ref, acc_ref)
        qr_ref[...] = _query_rows(q_ref, 1, NSA_SCALE)
        n_pad = kc_ref.shape[1]
        sel_t = jnp.zeros(selt_ref.shape, F32)
        for g in range(N_KV):
            sl = slice(g * HEAD_W, (g + 1) * HEAD_W)
            gr = slice(g * rows, (g + 1) * rows)
            qx = _stack_group_heads(q_ref, g).astype(BF16)
            bias_c = jnp.concatenate([bias_c_ref[0, gr, :]] * (n_pad // LANE - 1) + [bias_c_ref[1, gr, :]], axis=1)
            sc = _dot_t(qx, kc_ref[0, :, sl].astype(BF16)) * NSA_SCALE + bias_c
            pc = _masked_softmax(sc, bias_c > 0.5 * NEG_INF)
            ocmp_ref[g] = _dot(pc.astype(BF16), vc_ref[0, :, sl].astype(BF16))
            psum = sum(pc[r * t_new:(r + 1) * t_new] for r in range(1, HPG)) + pc[0:t_new]
            s_slc = _dot_exact_rhs(psum, wimp_ref[...])
            sel = _select_blocks(s_slc, past_len, n_slc)
            sel_pad = jnp.concatenate([sel, jnp.zeros((LANE - t_new, sel.shape[1]), F32)], axis=0).T
            tok = lax.broadcasted_iota(jnp.int32, (LANE, LANE), 0)
            lane = lax.broadcasted_iota(jnp.int32, (LANE, LANE), 1)
            spread = jnp.where((lane % t_new == tok) & (lane // rows == g), 1.0, 0.0).astype(BF16)
            sel_t = sel_t + _dot(sel_pad.astype(BF16), spread)
            n_wt = bias_w_ref.shape[0]
            bias_w = jnp.concatenate([bias_w_ref[k, gr, :] for k in range(n_wt)], axis=1)
            sw = _dot_t(qx, kw_ref[0, :, sl].astype(BF16)) * NSA_SCALE + bias_w
            pw = _masked_softmax(sw, bias_w > 0.5 * NEG_INF)
            owin_ref[g] = _dot(pw.astype(BF16), vw_ref[0, :, sl].astype(BF16))
        selt_ref[...] = sel_t
        _decode_step_t(kn_ref[0], vn_ref[0], qr_ref, bias_ref[2], m_ref, l_ref, acc_ref)

    block_rows = SLC_BLOCK * N_KV
    first_block = s * (PAGES_PER_STEP * PAGE // SLC_BLOCK)
    keep = jnp.concatenate(
        [jnp.broadcast_to(selt_ref[pl.ds(first_block + b, 1), :], (block_rows, selt_ref.shape[1]))
         for b in range(PAGES_PER_STEP * PAGE // SLC_BLOCK)], axis=0)
    _decode_step_t(_page_rows(kp), _page_rows(vp), qr_ref,
                   _page_bias_t(bias_ref, s * PAGES_PER_STEP, n_pages, 1), m_ref, l_ref, acc_ref, keep=keep)

    @pl.when(s == pl.num_programs(1) - 1)
    def _():
        sig = jax.nn.sigmoid(gt_ref[0])
        o_slc_all = (acc_ref[...] / l_ref[...]).T
        for g in range(N_KV):
            o_cmp = ocmp_ref[g]
            o_win = owin_ref[g]
            for r in range(HPG):
                h = g * HPG + r
                rs = slice(r * t_new, (r + 1) * t_new)
                c0 = g * LANE + r
                o_ref[0, :, h * HEAD_W:(h + 1) * HEAD_W] = (
                    sig[:, c0:c0 + 1] * o_cmp[rs]
                    + sig[:, c0 + HPG:c0 + HPG + 1] * o_slc_all[h * t_new:(h + 1) * t_new]
                    + sig[:, c0 + 2 * HPG:c0 + 2 * HPG + 1] * o_win[rs])


def nsa_decode(q, gates, kc, vc, k_new, v_new, kw_src, vw_src, cache_k, cache_v, layer, page_table,
               bias, bias_c, bias_w, past_len):
    b, t_new, _ = q.shape
    n_pages = page_table.shape[1]
    rows = HPG * t_new
    lanes = N_HEADS * t_new
    assert lanes == LANE
    n_pad = kc.shape[1]
    n_slc = -(-(past_len + t_new) // SLC_BLOCK)
    n_cols = -(-n_slc // LANE) * LANE
    wimp = jnp.asarray(_importance_matrix(n_pad - 1, n_pad, n_slc, n_cols), BF16)
    fixed = lambda *shape: pl.BlockSpec(shape, lambda i, s, pt: (0,) * len(shape))
    per_b = lambda *shape: pl.BlockSpec((1,) + shape, lambda i, s, pt: (i,) + (0,) * len(shape))
    grid_spec = pltpu.PrefetchScalarGridSpec(
        num_scalar_prefetch=1,
        grid=(b, n_pages // PAGES_PER_STEP),
        in_specs=[per_b(t_new, N_HEADS * HEAD_W), per_b(t_new, N_KV * LANE),
                  per_b(n_pad, NSA_KV), per_b(n_pad, NSA_KV), per_b(PAGE_ROWS, HEAD_W), per_b(PAGE_ROWS, HEAD_W),
                  per_b(kw_src.shape[1], NSA_KV), per_b(kw_src.shape[1], NSA_KV)]
        + _page_specs(layer, 2)
        + [fixed(*bias.shape), fixed(*bias_c.shape), fixed(*bias_w.shape), fixed(*wimp.shape)],
        out_specs=per_b(t_new, N_HEADS * HEAD_W),
        scratch_shapes=[
            pltpu.VMEM((lanes, HEAD_W), BF16),
            pltpu.VMEM((n_cols, lanes), F32),
            pltpu.VMEM((N_KV, rows, HEAD_W), F32),
            pltpu.VMEM((N_KV, rows, HEAD_W), F32),
            pltpu.VMEM((1, lanes), F32),
            pltpu.VMEM((1, lanes), F32),
            pltpu.VMEM((HEAD_W, lanes), F32),
        ],
    )
    return pl.pallas_call(
        functools.partial(_nsa_decode_kernel, n_pages=n_pages, past_len=past_len),
        grid_spec=grid_spec,
        out_shape=jax.ShapeDtypeStruct(q.shape[:2] + (N_HEADS * HEAD_W,), F32),
        compiler_params=_params(("parallel", "arbitrary")),
        name="nsa_decode",
    )(page_table, q, gates, kc, vc, k_new, v_new, kw_src, vw_src,
      *([cache_k] * PAGES_PER_STEP), *([cache_v] * PAGES_PER_STEP), bias, bias_c, bias_w, wimp)


def _gate_weights(w_g):
    d = w_g.shape[0]
    w = jnp.transpose(w_g.reshape(d, 3, N_KV, HPG), (0, 2, 1, 3)).reshape(d, N_KV, 3 * HPG)
    return jnp.pad(w, ((0, 0), (0, 0), (0, LANE - 3 * HPG))).reshape(d, N_KV * LANE)


def nsa_layer(xp, xs, gain, w_in, q_norm, k_norm, pe, w1, w2, w_out, caches, layer, page_table,
              bias_p, u_bias, bias_s, bias_sc, bias_sw, past_len):
    cmp_k, cmp_v, slc_k, slc_v, win_k, win_v = caches
    nqkv = N_HEADS * HEAD_W + 6 * NSA_KV
    w_g = _gate_weights(w_in[layer, :, nqkv:])[None]
    w_in_t = jnp.swapaxes(w_in, 1, 2)
    nq = N_HEADS * HEAD_W
    tile = 4 * LANE
    q_gain = jnp.tile(q_norm, tile // HEAD_W)
    head_norm = (HEAD_W, {**{c: q_gain for c in range(nq // tile)},
                          nq // tile + 2: jnp.tile(k_norm[1], tile // HEAD_W),
                          nq // tile + 4: jnp.tile(k_norm[2], tile // HEAD_W)})
    outs = []
    for x, paged in ((xp, False), (xs, True)):
        b, t, d = x.shape
        x2 = x.reshape(b * t, d)
        proj2 = dense(x2, w_in_t, layer, gain=gain, n=nqkv, w_t=True, head_norm=head_norm)
        proj = proj2.reshape(b, t, -1)
        gates = dense(x2, w_g, 0, gain=gain).reshape(b, t, -1)
        rows = [group_rows(proj2, nq // NSA_KV + a) for a in range(4)]
        as_state = lambda r: r.reshape(b, t, N_KV, HEAD_W)
        kw, vw = proj[:, :, nq + 4 * NSA_KV:nq + 5 * NSA_KV], proj[:, :, nq + 5 * NSA_KV:]
        if paged:
            kc = compress(_as_page_rows(cmp_k), layer, page_table, pe[0], w1[0], w2[0], k_norm[0])
            vc = compress(_as_page_rows(cmp_v), layer, page_table, pe[1], w1[1], w2[1], None)
            n_wt = bias_sw.shape[0]
            kw_src = jnp.concatenate([win_k[layer], kw], axis=1)
            vw_src = jnp.concatenate([win_v[layer], vw], axis=1)
            o = nsa_decode(proj, gates, kc, vc, _new_page_rows(rows[2], b), _new_page_rows(rows[3], b),
                           _pad_rows(kw_src, n_wt * LANE), _pad_rows(vw_src, n_wt * LANE),
                           _as_page_rows(slc_k), _as_page_rows(slc_v), layer, page_table,
                           bias_s, bias_sc, bias_sw, past_len)
            kw_win, vw_win = kw_src, vw_src
        else:
            n_pg = t // PAGE
            ident = jnp.arange(b * n_pg, dtype=jnp.int32).reshape(b, n_pg)
            as_pages = lambda r: r.reshape(1, b * n_pg, PAGE_ROWS, HEAD_W)
            kc = compress(as_pages(rows[0]), 0, ident, pe[0], w1[0], w2[0], k_norm[0])
            vc = compress(as_pages(rows[1]), 0, ident, pe[1], w1[1], w2[1], None)
            o = nsa_prompt(proj, gates, kc, vc, bias_p, u_bias)
            kw_win, vw_win = kw, vw
        y = dense(o.reshape(b * t, -1), w_out, layer, res=x2).reshape(b, t, d)
        win = tuple(a[:, -WINDOW:].reshape(b, WINDOW, N_KV, HEAD_W) for a in (kw_win, vw_win))
        st = tuple(as_state(r) for r in rows) + win
        outs.append((y, st))
    (yp, stp), (ys, sts) = outs
    return yp, ys, stp, sts


def kernel(x_prompt, x_sample, cache_diff_k, cache_diff_v, state_ssm, state_conv, cache_nsa_cmp_k, cache_nsa_cmp_v, cache_nsa_slc_k, cache_nsa_slc_v, cache_nsa_win_k, cache_nsa_win_v, cache_mem_k, cache_mem_v, page_table, mem_prompt, rel_bias_table, norm_mix, norm_xattn, norm_mem, norm_ffn, diff_w_in, diff_q_norm, diff_k_norm, diff_lambda, diff_sub_norm, diff_w_out, ssm_w_in, ssm_conv_w, ssm_conv_b, ssm_dt_bias, ssm_a_log, ssm_d, ssm_norm, ssm_w_out, nsa_w_in, nsa_q_norm, nsa_k_norm, nsa_cmp_pe, nsa_cmp_w1, nsa_cmp_w2, nsa_w_out, xattn_w_q, xattn_w_k, xattn_w_v, xattn_q_norm, xattn_k_norm, xattn_w_o, ffn_w1, ffn_w3, ffn_w2):
    xp, xs = x_prompt, x_sample
    bp, t, d = xp.shape
    bs, t_new, _ = xs.shape
    past_len = page_table.shape[1] * PAGE
    depth = norm_mix.shape[0]

    bias_p = bias_tiles(_prompt_attn_buckets(PROMPT_TQ), rel_bias_table, heads_on_lanes=True, scale=LOG2E)
    u_bias = bias_tiles(_prompt_cmp_buckets(PROMPT_TQ, t // CMP_STRIDE), rel_bias_table, heads_on_lanes=True,
                        scale=LOG2E)
    bias_s = bias_tiles_indexed(_decode_page_index(t_new, past_len), rel_bias_table)
    bias_sc = bias_tiles(_decode_cmp_buckets(t_new, past_len), rel_bias_table)
    n_wt = -(-(WINDOW + t_new) // LANE)
    bias_sw = bias_tiles(_decode_win_buckets(t_new, n_wt), rel_bias_table)

    mem_k, mem_v = mem_kv(mem_prompt, norm_mem, xattn_w_k, xattn_w_v, xattn_k_norm)
    xw = X_HEADS * X_DH
    win_k = cache_nsa_win_k.reshape(*cache_nsa_win_k.shape[:3], NSA_KV)
    win_v = cache_nsa_win_v.reshape(*cache_nsa_win_v.shape[:3], NSA_KV)

    dkp, dvp, dks, dvs = [], [], [], []
    ssp, cvp, sss, cvs = [], [], [], []
    nsp, nss = [], []
    for i in range(depth):
        kind, j = i % 3, i // 3
        if kind == 0:
            lam_init = 0.8 - 0.6 * math.exp(-0.3 * i)
            xp, xs, kp_, vp_, ks_, vs_ = diff_layer(
                xp, xs, norm_mix[i], diff_w_in, diff_q_norm[j], diff_k_norm[j], diff_lambda[j],
                diff_sub_norm[j], diff_w_out, cache_diff_k, cache_diff_v, j, page_table,
                bias_p, bias_s, lam_init)
            dkp.append(kp_)
            dvp.append(vp_)
            dks.append(ks_)
            dvs.append(vs_)
        elif kind == 1:
            xp, xs, hp_, cp_, hs_, cs_ = ssd_layer(
                xp, xs, norm_mix[i], ssm_w_in, j, ssm_conv_w[j], ssm_conv_b[j], ssm_dt_bias[j], ssm_a_log[j],
                ssm_d[j], ssm_norm[j], ssm_w_out, state_conv[j], state_ssm[j])
            ssp.append(hp_)
            cvp.append(cp_)
            sss.append(hs_)
            cvs.append(cs_)
        else:
            caches = (cache_nsa_cmp_k, cache_nsa_cmp_v, cache_nsa_slc_k, cache_nsa_slc_v, win_k, win_v)
            xp, xs, stp, sts = nsa_layer(
                xp, xs, norm_mix[i], nsa_w_in, nsa_q_norm[j], nsa_k_norm[j], nsa_cmp_pe[j], nsa_cmp_w1[j],
                nsa_cmp_w2[j], nsa_w_out, caches, j, page_table,
                bias_p, u_bias, bias_s, bias_sc, bias_sw, past_len)
            nsp.append(stp)
            nss.append(sts)
        xp = xattn(xp, norm_xattn[i], xattn_w_q, xattn_q_norm[i], mem_k[i], mem_v[i], xattn_w_o, i)
        xs = xattn(xs, norm_xattn[i], xattn_w_q, xattn_q_norm[i], cache_mem_k[i].reshape(bs, N_MEM, xw),
                   cache_mem_v[i].reshape(bs, N_MEM, xw), xattn_w_o, i)
        xp = ffn(xp.reshape(bp * t, d), norm_ffn[i], ffn_w1, ffn_w3, ffn_w2, i).reshape(bp, t, d)
        xs = ffn(xs.reshape(bs * t_new, d), norm_ffn[i], ffn_w1, ffn_w3, ffn_w2, i).reshape(bs, t_new, d)

    st = lambda xs_: jnp.stack(xs_, axis=0)
    nsp_t = [st([s[a] for s in nsp]) for a in range(6)]
    nss_t = [st([s[a] for s in nss]) for a in range(6)]
    mem_shape = (depth, bp, N_MEM, X_HEADS, X_DH)
    return (xp, xs,
            st(dkp), st(dvp), st(dks), st(dvs),
            st(ssp), st(cvp), st(sss), st(cvs),
            *nsp_t, *nss_t,
            mem_k.reshape(mem_shape), mem_v.reshape(mem_shape))
```

```python
import functools
import math

import numpy as np
import jax
import jax.numpy as jnp
from jax import lax
from jax.experimental import pallas as pl
from jax.experimental.pallas import tpu as pltpu

F32 = jnp.float32
BF16 = jnp.bfloat16

D_MODEL = 2048
DEPTH = 4
PAGE = 128
N_HEADS = 16
N_KV = 4
HPG = N_HEADS // N_KV
DIFF_DH = 64
HEAD_W = 128
N_BUCKETS = 32
MAX_EXACT = 16
MAX_DIST = 128
D_INNER = 2 * D_MODEL
SSM_HEADDIM = 64
SSM_HEADS = D_INNER // SSM_HEADDIM
SSM_GROUPS = 8
SSM_HPG = SSM_HEADS // SSM_GROUPS
D_STATE = 128
CONV_W = 4
CONV_DIM = D_INNER + 2 * SSM_GROUPS * D_STATE
SSM_CHUNK = 128
CMP_STRIDE = 16
SLC_BLOCK = 64
SLC_RATIO = SLC_BLOCK // CMP_STRIDE
N_SELECT = 16
WINDOW = 512
N_MEM = 256
X_HEADS = 4
X_DH = 128
EPS = 1e-6
NEG_INF = -1e30
FORCE_SCORE = 1e4

LANE = 128
VMEM_LIMIT = 56 * 1024 * 1024
DENSE_TALL_ROWS = 2048


def _params(sem):
    return pltpu.CompilerParams(dimension_semantics=sem, vmem_limit_bytes=VMEM_LIMIT)


def _rms(x, gain):
    return x * lax.rsqrt(jnp.mean(x * x, axis=-1, keepdims=True) + EPS) * gain


def _dot(a, b):
    return jnp.dot(a, b, preferred_element_type=F32)


def _dot_t(a, b):
    return lax.dot_general(a, b, (((1,), (1,)), ((), ())), preferred_element_type=F32)


def _split3(x):
    hi = x.astype(BF16)
    r1 = x - hi.astype(F32)
    mid = r1.astype(BF16)
    lo = (r1 - mid.astype(F32)).astype(BF16)
    return hi, mid, lo


def _dot_exact_rhs(x, m_bf16):
    hi, mid, lo = _split3(x)
    return _dot(hi, m_bf16) + _dot(mid, m_bf16) + _dot(lo, m_bf16)


def _norm64(blk, gain, lo):
    sq = blk * blk
    s_lo = jnp.sum(jnp.where(lo, sq, 0.0), axis=-1, keepdims=True)
    s_hi = jnp.sum(jnp.where(lo, 0.0, sq), axis=-1, keepdims=True)
    ms = jnp.where(lo, s_lo, s_hi) * (1.0 / DIFF_DH)
    return blk * lax.rsqrt(ms + EPS) * gain


def _head_norm_tile(y, gain, seg):
    lo = lax.broadcasted_iota(jnp.int32, (y.shape[0], LANE), 1) < DIFF_DH
    blocks = []
    for c in range(y.shape[1] // LANE):
        sl = slice(c * LANE, (c + 1) * LANE)
        blocks.append(_norm64(y[:, sl], gain[:, sl], lo) if seg == DIFF_DH else _rms(y[:, sl], gain[:, sl]))
    return jnp.concatenate(blocks, axis=1)


def _dense_kernel(*refs, norm, residual, w_t, head_seg, norm_tiles):
    refs = list(refs)
    x_ref = refs.pop(0)
    g_ref = refs.pop(0) if norm else None
    w_ref = refs.pop(0)
    r_ref = refs.pop(0) if residual else None
    hg_ref = refs.pop(0) if head_seg else None
    o_ref, xb_ref = refs
    j = pl.program_id(1)

    @pl.when(j == 0)
    def _():
        x = x_ref[...]
        if norm:
            x = _rms(x, g_ref[...])
        xb_ref[...] = x.astype(BF16)

    w = w_ref[0].astype(BF16)
    y = _dot_t(xb_ref[...], w) if w_t else _dot(xb_ref[...], w)
    if residual:
        y = y + r_ref[...]
    if not head_seg:
        o_ref[...] = y
        return
    is_norm = functools.reduce(jnp.logical_or, [j == t for t in norm_tiles])

    @pl.when(is_norm)
    def _():
        o_ref[...] = _head_norm_tile(y, hg_ref[0], head_seg)

    @pl.when(jnp.logical_not(is_norm))
    def _():
        o_ref[...] = y


def dense(x, w, layer, gain=None, res=None, n=None, w_t=False, head_norm=None):
    m, k = x.shape
    n = w.shape[1 if w_t else 2] if n is None else n
    tm = m if m <= 512 else (DENSE_TALL_ROWS if k <= D_MODEL and m % DENSE_TALL_ROWS == 0 else 512)
    tn = LANE if n % 256 else (512 if n % 512 == 0 else 256)
    assert m % tm == 0 and n % tn == 0
    norm, residual = gain is not None, res is not None
    x_mode = dict(pipeline_mode=pl.Buffered(1)) if tm == DENSE_TALL_ROWS else {}
    in_specs = [pl.BlockSpec((tm, k), lambda i, j: (i, 0), **x_mode)]
    args = [x]
    if norm:
        in_specs.append(pl.BlockSpec((1, k), lambda i, j: (0, 0)))
        args.append(gain.reshape(1, k))
    if w_t:
        in_specs.append(pl.BlockSpec((1, tn, k), lambda i, j: (layer, j, 0)))
    else:
        in_specs.append(pl.BlockSpec((1, k, tn), lambda i, j: (layer, 0, j)))
    args.append(w)
    if residual:
        in_specs.append(pl.BlockSpec((tm, tn), lambda i, j: (i, j)))
        args.append(res)
    head_seg, norm_tiles = 0, ()
    if head_norm is not None:
        head_seg, tile_gains = head_norm
        norm_tiles = tuple(sorted(tile_gains))
        ones = jnp.ones((tn,), F32)
        gains = jnp.stack([tile_gains.get(t, ones) for t in range(n // tn)]).reshape(n // tn, 1, tn)
        in_specs.append(pl.BlockSpec((1, 1, tn), lambda i, j: (j, 0, 0)))
        args.append(gains)
    return pl.pallas_call(
        functools.partial(_dense_kernel, norm=norm, residual=residual, w_t=w_t, head_seg=head_seg,
                          norm_tiles=norm_tiles),
        grid=(m // tm, n // tn),
        in_specs=in_specs,
        out_specs=pl.BlockSpec((tm, tn), lambda i, j: (i, j)),
        out_shape=jax.ShapeDtypeStruct((m, n), F32),
        scratch_shapes=[pltpu.VMEM((tm, k), BF16)],
        compiler_params=_params(("parallel", "arbitrary")),
        name="dense",
    )(*args)


def _ffn_kernel(x_ref, g_ref, w1_ref, w3_ref, w2_ref, o_ref, xb_ref):
    @pl.when(pl.program_id(1) == 0)
    def _():
        x = x_ref[...]
        xb_ref[...] = _rms(x, g_ref[...]).astype(BF16)
        o_ref[...] = x

    xb = xb_ref[...]
    h1 = _dot(xb, w1_ref[0].astype(BF16))
    h3 = _dot(xb, w3_ref[0].astype(BF16))
    a = (h1 * jax.nn.sigmoid(h1) * h3).astype(BF16)
    o_ref[...] += _dot(a, w2_ref[0].astype(BF16))


def ffn(x, gain, w1, w3, w2, layer):
    m, d = x.shape
    f = w1.shape[2]
    tm = min(m, 1024)
    tf = 256
    assert m % tm == 0 and f % tf == 0
    return pl.pallas_call(
        _ffn_kernel,
        grid=(m // tm, f // tf),
        in_specs=[
            pl.BlockSpec((tm, d), lambda i, j: (i, 0), pipeline_mode=pl.Buffered(1)),
            pl.BlockSpec((1, d), lambda i, j: (0, 0)),
            pl.BlockSpec((1, d, tf), lambda i, j: (layer, 0, j)),
            pl.BlockSpec((1, d, tf), lambda i, j: (layer, 0, j)),
            pl.BlockSpec((1, tf, d), lambda i, j: (layer, j, 0)),
        ],
        out_specs=pl.BlockSpec((tm, d), lambda i, j: (i, 0)),
        out_shape=jax.ShapeDtypeStruct((m, d), F32),
        scratch_shapes=[pltpu.VMEM((tm, d), BF16)],
        compiler_params=_params(("parallel", "arbitrary")),
        name="ffn",
    )(x, gain.reshape(1, d), w1, w3, w2)


def _mem_kv_kernel(mem_ref, g_ref, wk_ref, wv_ref, kn_ref, k_ref, v_ref):
    m = _rms(mem_ref[0], g_ref[0]).astype(BF16)
    k = _dot(m, wk_ref[0].astype(BF16))
    v_ref[0, 0] = _dot(m, wv_ref[0].astype(BF16))
    for h in range(X_HEADS):
        sl = slice(h * X_DH, (h + 1) * X_DH)
        k_ref[0, 0, :, sl] = _rms(k[:, sl], kn_ref[0])


def mem_kv(mem, g_mem, wk, wv, k_norm):
    b = mem.shape[0]
    nl = wk.shape[0]
    hw = X_HEADS * X_DH
    shape = jax.ShapeDtypeStruct((nl, b, N_MEM, hw), F32)
    return pl.pallas_call(
        _mem_kv_kernel,
        grid=(nl, b),
        in_specs=[
            pl.BlockSpec((1, N_MEM, D_MODEL), lambda l, i: (i, 0, 0)),
            pl.BlockSpec((1, 1, D_MODEL), lambda l, i: (l, 0, 0)),
            pl.BlockSpec((1, D_MODEL, hw), lambda l, i: (l, 0, 0)),
            pl.BlockSpec((1, D_MODEL, hw), lambda l, i: (l, 0, 0)),
            pl.BlockSpec((1, 1, X_DH), lambda l, i: (l, 0, 0)),
        ],
        out_specs=[pl.BlockSpec((1, 1, N_MEM, hw), lambda l, i: (l, i, 0, 0))] * 2,
        out_shape=[shape, shape],
        compiler_params=_params(("parallel", "parallel")),
        name="mem_kv",
    )(mem, g_mem.reshape(nl, 1, D_MODEL), wk, wv, k_norm.reshape(nl, 1, X_DH))


def _xattn_kernel(x_ref, g_ref, wq_ref, qn_ref, k_ref, v_ref, wo_ref, o_ref):
    nb, tm, d = x_ref.shape
    x = x_ref[...].reshape(nb * tm, d)
    q = _dot(_rms(x, g_ref[...]).astype(BF16), wq_ref[0].astype(BF16))
    rows = []
    for bb in range(nb):
        outs = []
        for h in range(X_HEADS):
            sl = slice(h * X_DH, (h + 1) * X_DH)
            qh = _rms(q[bb * tm:(bb + 1) * tm, sl], qn_ref[...]).astype(BF16)
            s = _dot_t(qh, k_ref[bb, :, sl].astype(BF16)) * (X_DH ** -0.5)
            e = jnp.exp(s - jnp.max(s, axis=-1, keepdims=True))
            p = e / jnp.sum(e, axis=-1, keepdims=True)
            outs.append(_dot(p.astype(BF16), v_ref[bb, :, sl].astype(BF16)))
        rows.append(jnp.concatenate(outs, axis=1))
    o = jnp.concatenate(rows, axis=0).astype(BF16)
    o_ref[...] = (x + _dot(o, wo_ref[0].astype(BF16))).reshape(nb, tm, d)


def xattn(x, gain, wq, q_norm, k, v, wo, layer):
    b, t, d = x.shape
    hw = X_HEADS * X_DH
    tm = min(t, 512)
    nb = b if t < LANE else 1
    return pl.pallas_call(
        _xattn_kernel,
        grid=(b // nb, t // tm),
        in_specs=[
            pl.BlockSpec((nb, tm, d), lambda i, j: (i, j, 0)),
            pl.BlockSpec((1, d), lambda i, j: (0, 0)),
            pl.BlockSpec((1, d, hw), lambda i, j: (layer, 0, 0)),
            pl.BlockSpec((1, X_DH), lambda i, j: (0, 0)),
            pl.BlockSpec((nb, N_MEM, hw), lambda i, j: (i, 0, 0)),
            pl.BlockSpec((nb, N_MEM, hw), lambda i, j: (i, 0, 0)),
            pl.BlockSpec((1, hw, d), lambda i, j: (layer, 0, 0)),
        ],
        out_specs=pl.BlockSpec((nb, tm, d), lambda i, j: (i, j, 0)),
        out_shape=jax.ShapeDtypeStruct((b, t, d), F32),
        compiler_params=_params(("parallel", "parallel")),
        name="xattn",
    )(x, gain.reshape(1, d), wq, q_norm.reshape(1, X_DH), k, v, wo)


def _bucket_np(dist):
    n = np.maximum(dist, 0)
    nf = np.maximum(n, 1).astype(np.float64)
    large = MAX_EXACT + (np.log(nf / MAX_EXACT) / math.log(MAX_DIST / MAX_EXACT)
                         * (N_BUCKETS - MAX_EXACT)).astype(np.int64)
    b = np.where(n < MAX_EXACT, n, np.minimum(large, N_BUCKETS - 1))
    return np.where(dist < 0, -1, b).astype(np.int32)


def _bias_kernel(bkt_ref, tab_ref, o_ref, *, scale):
    h = pl.program_id(1)
    b = bkt_ref[0]
    acc = jnp.full(b.shape, NEG_INF, F32)
    for k in range(N_BUCKETS):
        acc = jnp.where(b == k, tab_ref[k, h] * scale, acc)
    o_ref[0] = acc


def _bias_packed_kernel(idx_ref, tab_ref, o_ref, *, scale):
    idx = idx_ref[0]
    acc = jnp.full(idx.shape, NEG_INF, F32)
    for k in range(N_BUCKETS):
        for h in range(N_HEADS):
            acc = jnp.where(idx == k * N_HEADS + h, tab_ref[k, h] * scale, acc)
    o_ref[0] = acc


def bias_tiles_indexed(idx, table, scale=1.0):
    nt, r, w = idx.shape
    return pl.pallas_call(
        functools.partial(_bias_packed_kernel, scale=scale),
        grid=(nt,),
        in_specs=[
            pl.BlockSpec((1, r, w), lambda t: (t, 0, 0)),
            pl.BlockSpec(memory_space=pltpu.SMEM),
        ],
        out_specs=pl.BlockSpec((1, r, w), lambda t: (t, 0, 0)),
        out_shape=jax.ShapeDtypeStruct((nt, r, w), F32),
        compiler_params=_params(("parallel",)),
        name="bias_tiles_indexed",
    )(jnp.asarray(idx.astype(np.int32)), table)


def bias_tiles(buckets, table, heads_on_lanes=False, scale=1.0):
    nt, r, w = buckets.shape
    if heads_on_lanes:
        out_spec = pl.BlockSpec((1, r, w), lambda t, h: (t, 0, h))
        out_shape = (nt, r, N_HEADS * w)
    else:
        out_spec = pl.BlockSpec((1, r, w), lambda t, h: (t, h, 0))
        out_shape = (nt, N_HEADS * r, w)
    return pl.pallas_call(
        functools.partial(_bias_kernel, scale=scale),
        grid=(nt, N_HEADS),
        in_specs=[
            pl.BlockSpec((1, r, w), lambda t, h: (t, 0, 0)),
            pl.BlockSpec(memory_space=pltpu.SMEM),
        ],
        out_specs=out_spec,
        out_shape=jax.ShapeDtypeStruct(out_shape, F32),
        compiler_params=_params(("parallel", "parallel")),
        name="bias_tiles",
    )(jnp.asarray(buckets), table)


def _prompt_attn_buckets(tq):
    assert WINDOW % tq == 0 and tq >= MAX_DIST
    j = np.arange(tq)[:, None]
    i = np.arange(tq)[None, :]
    far = np.full((tq, tq), N_BUCKETS - 1, np.int32)
    return np.stack([
        _bucket_np(i - j),
        _bucket_np(i - j + tq),
        far,
        np.where(j > i, far, -1),
    ]).astype(np.int32)


def _diff_lambda(lam_ref, lam_init):
    lf = lam_ref[...]
    s01 = jnp.sum(lf[0:1] * lf[1:2], axis=-1, keepdims=True)
    s23 = jnp.sum(lf[2:3] * lf[3:4], axis=-1, keepdims=True)
    return jnp.exp(s01) - jnp.exp(s23) + lam_init


def _split_maps(qs):
    lo = lax.broadcasted_iota(jnp.int32, qs.shape, 1) < DIFF_DH
    return jnp.concatenate([jnp.where(lo, qs, 0.0), jnp.where(lo, 0.0, qs)], axis=0)


def _diff_finish_rows(o, lam, sub_norm, lam_init):
    r = o.shape[0] // 2
    a = o[:r] - lam * o[r:]
    return _rms(a, sub_norm) * (1.0 - lam_init)


PROMPT_TQ = 256


LOG2E = math.log2(math.e)
FLASH_CHUNKS = 4
FAR_WIDTHS = (4, 2, 1)


def _far_loops(n_far, body_of_width, tq):
    start = 0
    for w in FAR_WIDTHS:
        n = (n_far - start) // w
        lax.fori_loop(start // w, start // w + n, body_of_width(w * tq), 0)
        start = start + n * w


def _flash_tile_t(k_tile, vt_tile, qs_ref, m_ref, l_ref, acc_ref, scale=None, bias=None, far_bias=None,
                  mask=None, n_chunks=FLASH_CHUNKS):
    chunk = qs_ref.shape[0] // n_chunks
    vt_ones = jnp.concatenate([vt_tile, jnp.ones((16, vt_tile.shape[1]), BF16)], axis=0)
    logits = [_dot_t(k_tile, qs_ref[c * chunk:(c + 1) * chunk, :]) for c in range(n_chunks)]
    for c in range(n_chunks):
        cols = slice(c * chunk, (c + 1) * chunk)
        s = logits[c]
        if scale is not None:
            s = s * scale
        if bias is not None:
            s = s + bias(cols)
        if mask is not None:
            s = jnp.where(mask[:, cols] > 0.5, s, NEG_INF)
        m_old = m_ref[:, cols]
        s_max = jnp.max(s, axis=0, keepdims=True)
        if far_bias is None:
            m_new = jnp.maximum(m_old, s_max)
            p = jnp.exp2(s - m_new)
        else:
            fb = far_bias[:, cols]
            m_new = jnp.maximum(m_old, s_max + fb)
            p = jnp.exp2(s - (m_new - fb))
        alpha = jnp.exp2(m_old - m_new)
        pv = _dot(vt_ones, p.astype(BF16))
        d = vt_tile.shape[0]
        l_ref[:, cols] = alpha * l_ref[:, cols] + pv[d:d + 1]
        acc_ref[:, cols] = alpha * acc_ref[:, cols] + pv[:d]
        m_ref[:, cols] = m_new


def _stage_kv(k_ref, v_ref, kb_ref, vt_ref, chunk):
    for c in range(k_ref.shape[1] // chunk):
        rows = slice(c * chunk, (c + 1) * chunk)
        kb_ref[rows, :] = k_ref[0, rows, :].astype(BF16)
        vt_ref[:, rows] = v_ref[0, rows, :].T.astype(BF16)


def _diff_flash_kernel(lam_ref, q_ref, k_ref, v_ref, bias_ref, sn_ref, o_ref,
                       qs_ref, kb_ref, vt_ref, m_ref, l_ref, acc_ref, *, tq, lam_init):
    qi = pl.program_id(2)

    @pl.when(qi == 0)
    def _():
        _stage_kv(k_ref, v_ref, kb_ref, vt_ref, tq)

    qb = q_ref[0]
    qs = jnp.concatenate([qb[:, r * HEAD_W:(r + 1) * HEAD_W] for r in range(HPG)], axis=0)
    qs_ref[...] = _split_maps(qs * (DIFF_DH ** -0.5 * LOG2E)).astype(BF16)
    _reset_flash(m_ref, l_ref, acc_ref)
    far = bias_ref[2, 0:1, :]
    far = jnp.concatenate([far, far], axis=1)

    def far_body(width):
        def body(kt, c):
            rows = pl.ds(pl.multiple_of(kt * width, width), width)
            _flash_tile_t(kb_ref[rows, :], vt_ref[:, rows], qs_ref, m_ref, l_ref, acc_ref, far_bias=far)
            return c
        return body

    def near_body(kt, c):
        rows = pl.ds(pl.multiple_of(kt * tq, tq), tq)
        b = bias_ref[qi - kt]
        _flash_tile_t(kb_ref[rows, :], vt_ref[:, rows], qs_ref, m_ref, l_ref, acc_ref,
                      bias=lambda cols: jnp.concatenate([b, b], axis=1), n_chunks=1)
        return c

    n_far = jnp.maximum(qi - 1, 0)
    _far_loops(n_far, far_body, tq)
    lax.fori_loop(n_far, qi + 1, near_body, 0)
    o = (acc_ref[...] / l_ref[...]).T
    a = _diff_finish_rows(o, _diff_lambda(lam_ref, lam_init), sn_ref[...], lam_init)
    for r in range(HPG):
        o_ref[0, :, r * HEAD_W:(r + 1) * HEAD_W] = a[r * tq:(r + 1) * tq]


def diff_flash(qkv, bias, lam_p, sub_norm, lam_init, tq=PROMPT_TQ):
    b, t, _ = qkv.shape
    gw = HPG * HEAD_W
    rows = 2 * HPG * tq
    k0 = N_HEADS
    return pl.pallas_call(
        functools.partial(_diff_flash_kernel, tq=tq, lam_init=lam_init),
        grid=(b, N_KV, t // tq),
        in_specs=[
            pl.BlockSpec((4, DIFF_DH), lambda i, g, j: (0, 0)),
            pl.BlockSpec((1, tq, gw), lambda i, g, j: (i, j, g)),
            pl.BlockSpec((1, t, HEAD_W), lambda i, g, j: (i, 0, k0 + g)),
            pl.BlockSpec((1, t, HEAD_W), lambda i, g, j: (i, 0, k0 + N_KV + g)),
            pl.BlockSpec((4, tq, HPG * tq), lambda i, g, j: (0, 0, g)),
            pl.BlockSpec((1, HEAD_W), lambda i, g, j: (0, 0)),
        ],
        out_specs=pl.BlockSpec((1, tq, gw), lambda i, g, j: (i, j, g)),
        out_shape=jax.ShapeDtypeStruct((b, t, N_HEADS * HEAD_W), F32),
        scratch_shapes=[
            pltpu.VMEM((rows, HEAD_W), BF16),
            pltpu.VMEM((t, HEAD_W), BF16),
            pltpu.VMEM((HEAD_W, t), BF16),
            pltpu.VMEM((1, rows), F32),
            pltpu.VMEM((1, rows), F32),
            pltpu.VMEM((HEAD_W, rows), F32),
        ],
        compiler_params=_params(("parallel", "parallel", "arbitrary")),
        name="diff_flash",
    )(lam_p, qkv, qkv, qkv, bias, sub_norm.reshape(1, HEAD_W))


PAGES_PER_STEP = 16


def _decode_page_index(t_new, past_len):
    j = np.arange(PAGE)[:, None]
    i = np.arange(t_new)[None, :]
    far = np.full((PAGE, t_new), N_BUCKETS - 1, np.int32)
    new = np.where(j < t_new, _bucket_np(i - j), -1)
    g = np.arange(N_KV)[None, :, None, None]
    h = np.arange(N_HEADS)[None, None, :, None]
    tiles = []
    for bkt in (far, _bucket_np(PAGE + i - j), new):
        b = bkt[:, None, None, :]
        idx = np.where((b >= 0) & (h // HPG == g), b * N_HEADS + h, -1)
        tiles.append(idx.reshape(PAGE * N_KV, N_HEADS * t_new))
    return np.stack(tiles).astype(np.int32)


def _stack_group_heads(q_ref, g):
    return jnp.concatenate(
        [q_ref[0, :, (g * HPG + r) * HEAD_W:(g * HPG + r + 1) * HEAD_W] for r in range(HPG)], axis=0)


def _query_rows(q_ref, maps, scale):
    pieces = []
    for m in range(maps):
        for h in range(N_HEADS):
            qh = q_ref[0, :, h * HEAD_W:(h + 1) * HEAD_W] * scale
            if maps == 2:
                lo = lax.broadcasted_iota(jnp.int32, qh.shape, 1) < DIFF_DH
                qh = jnp.where(lo, qh, 0.0) if m == 0 else jnp.where(lo, 0.0, qh)
            pieces.append(qh)
    return jnp.concatenate(pieces, axis=0).astype(BF16)


def _page_bias_t(bias_ref, first_page, n_pages, reps):
    tiles = [bias_ref[jnp.where(first_page + u == n_pages - 1, 1, 0)] for u in range(PAGES_PER_STEP)]
    b = jnp.concatenate(tiles, axis=0)
    return jnp.concatenate([b] * reps, axis=1)


def _decode_step_t(k_rows, v_rows, q_ref, bias, m_ref, l_ref, acc_ref, keep=None):
    s = _dot_t(k_rows.astype(BF16), q_ref[...]) + bias
    if keep is not None:
        s = jnp.where(keep > 0.5, s, NEG_INF)
    m_old = m_ref[...]
    m_new = jnp.maximum(m_old, jnp.max(s, axis=0, keepdims=True))
    alpha = jnp.exp(m_old - m_new)
    p = jnp.exp(s - m_new)
    l_ref[...] = alpha * l_ref[...] + jnp.sum(p, axis=0, keepdims=True)
    acc_ref[...] = alpha * acc_ref[...] + _dot(v_rows.T.astype(BF16), p.astype(BF16))
    m_ref[...] = m_new


def _page_rows(refs):
    return jnp.concatenate([r[0, 0] for r in refs], axis=0)


def _diff_decode_kernel(pt_ref, lam_ref, q_ref, kn_ref, vn_ref, *rest, n_pages, lam_init):
    del pt_ref
    kp = rest[:PAGES_PER_STEP]
    vp = rest[PAGES_PER_STEP:2 * PAGES_PER_STEP]
    bias_ref, sn_ref, o_ref, qr_ref, m_ref, l_ref, acc_ref = rest[2 * PAGES_PER_STEP:]
    s = pl.program_id(1)
    t_new = q_ref.shape[1]

    @pl.when(s == 0)
    def _():
        _reset_flash(m_ref, l_ref, acc_ref)
        qr_ref[...] = _query_rows(q_ref, 2, DIFF_DH ** -0.5)
        b = bias_ref[2]
        _decode_step_t(kn_ref[0], vn_ref[0], qr_ref, jnp.concatenate([b, b], axis=1), m_ref, l_ref, acc_ref)

    _decode_step_t(_page_rows(kp), _page_rows(vp), qr_ref,
                   _page_bias_t(bias_ref, s * PAGES_PER_STEP, n_pages, 2), m_ref, l_ref, acc_ref)

    @pl.when(s == pl.num_programs(1) - 1)
    def _():
        o = (acc_ref[...] / l_ref[...]).T
        a = _diff_finish_rows(o, _diff_lambda(lam_ref, lam_init), sn_ref[...], lam_init)
        for h in range(N_HEADS):
            o_ref[0, :, h * HEAD_W:(h + 1) * HEAD_W] = a[h * t_new:(h + 1) * t_new]


PAGE_ROWS = PAGE * N_KV


def _as_page_rows(cache):
    return cache.reshape(*cache.shape[:2], PAGE_ROWS, HEAD_W)


def _page_specs(layer, n):
    def spec(u):
        return pl.BlockSpec((1, 1, PAGE_ROWS, HEAD_W),
                            lambda i, s, pt: (layer, pt[i, s * PAGES_PER_STEP + u], 0, 0))
    return [spec(u) for u in range(PAGES_PER_STEP)] * n


def diff_decode(q, k_new, v_new, cache_k, cache_v, layer, page_table, bias, lam_p, sub_norm, lam_init):
    b, t_new, _ = q.shape
    n_pages = page_table.shape[1]
    lanes = 2 * N_HEADS * t_new
    assert N_HEADS * t_new == LANE
    fixed = lambda *shape: pl.BlockSpec(shape, lambda i, s, pt: (0,) * len(shape))
    per_b = lambda *shape: pl.BlockSpec((1,) + shape, lambda i, s, pt: (i,) + (0,) * len(shape))
    grid_spec = pltpu.PrefetchScalarGridSpec(
        num_scalar_prefetch=1,
        grid=(b, n_pages // PAGES_PER_STEP),
        in_specs=[fixed(4, DIFF_DH), per_b(t_new, N_HEADS * HEAD_W), per_b(PAGE_ROWS, HEAD_W),
                  per_b(PAGE_ROWS, HEAD_W)]
        + _page_specs(layer, 2)
        + [fixed(*bias.shape), fixed(1, HEAD_W)],
        out_specs=per_b(t_new, N_HEADS * HEAD_W),
        scratch_shapes=[
            pltpu.VMEM((lanes, HEAD_W), BF16),
            pltpu.VMEM((1, lanes), F32),
            pltpu.VMEM((1, lanes), F32),
            pltpu.VMEM((HEAD_W, lanes), F32),
        ],
    )
    return pl.pallas_call(
        functools.partial(_diff_decode_kernel, n_pages=n_pages, lam_init=lam_init),
        grid_spec=grid_spec,
        out_shape=jax.ShapeDtypeStruct(q.shape[:2] + (N_HEADS * HEAD_W,), F32),
        compiler_params=_params(("parallel", "arbitrary")),
        name="diff_decode",
    )(page_table, lam_p, q, k_new, v_new, *([cache_k] * PAGES_PER_STEP), *([cache_v] * PAGES_PER_STEP),
      bias, sub_norm.reshape(1, HEAD_W))


def _pad_rows(a, n):
    return jnp.pad(a, ((0, 0), (0, n - a.shape[1]), (0, 0)))


def _group_rows_kernel(x_ref, o_ref):
    tm = x_ref.shape[0]
    for g in range(N_KV):
        o_ref[pl.ds(g, tm, stride=N_KV), :] = x_ref[:, g * HEAD_W:(g + 1) * HEAD_W]


def group_rows(x, col_block):
    m = x.shape[0]
    tm = min(m, 512)
    assert m % tm == 0
    return pl.pallas_call(
        _group_rows_kernel,
        grid=(m // tm,),
        in_specs=[pl.BlockSpec((tm, N_KV * HEAD_W), lambda i: (i, col_block))],
        out_specs=pl.BlockSpec((tm * N_KV, HEAD_W), lambda i: (i, 0)),
        out_shape=jax.ShapeDtypeStruct((m * N_KV, HEAD_W), F32),
        compiler_params=_params(("parallel",)),
        name="group_rows",
    )(x)


def _new_page_rows(rows, b):
    return _pad_rows(rows.reshape(b, -1, HEAD_W), PAGE_ROWS)


def diff_layer(xp, xs, gain, w_in, q_norm, k_norm, lam_p, sub_norm, w_out, cache_k, cache_v, layer,
               page_table, bias_p, bias_s, lam_init):
    nq, nk = N_HEADS * HEAD_W, N_KV * HEAD_W
    tile = 4 * LANE
    q_gain, k_gain = jnp.tile(q_norm, tile // DIFF_DH), jnp.tile(k_norm, tile // DIFF_DH)
    head_norm = (DIFF_DH, {**{c: q_gain for c in range(nq // tile)}, nq // tile: k_gain})
    outs = []
    for x, paged in ((xp, False), (xs, True)):
        b, t, d = x.shape
        x2 = x.reshape(b * t, d)
        qkv2 = dense(x2, w_in, layer, gain=gain, head_norm=head_norm)
        qkv = qkv2.reshape(b, t, -1)
        k_rows, v_rows = group_rows(qkv2, nq // nk), group_rows(qkv2, nq // nk + 1)
        if paged:
            o = diff_decode(qkv, _new_page_rows(k_rows, b), _new_page_rows(v_rows, b), _as_page_rows(cache_k),
                            _as_page_rows(cache_v), layer, page_table, bias_s, lam_p, sub_norm, lam_init)
        else:
            o = diff_flash(qkv, bias_p, lam_p, sub_norm, lam_init)
        y = dense(o.reshape(b * t, -1), w_out, layer, res=x2).reshape(b, t, d)
        outs.append((y, k_rows.reshape(b, t, N_KV, HEAD_W), v_rows.reshape(b, t, N_KV, HEAD_W)))
    (yp, kp, vp), (ys, ks, vs) = outs
    return yp, ys, kp, vp, ks, vs


SSM_GW = SSM_HPG * SSM_HEADDIM
SSM_BC = 2 * SSM_GROUPS * D_STATE
CONV_PAD = 8


def _conv_silu(buf_ref, w_ref, b_ref, n):
    acc = b_ref[...]
    for k in range(CONV_W):
        acc = acc + buf_ref[pl.ds(CONV_PAD - (CONV_W - 1) + k, n), :] * w_ref[k:k + 1, :]
    return acc * jax.nn.sigmoid(acc)


def _ssd_kernel(z_ref, x_ref, bc_ref, dt_ref, cbx_ref, cbbc_ref, wx_ref, wbc_ref, bx_ref, bbc_ref,
                dtb_ref, alog_ref, dsk_ref, ng_ref, e_ref, tri_ref, h0_ref,
                y_ref, hout_ref, ht_ref, xbuf_ref, bcbuf_ref, xa_ref, bca_ref, *, t_valid):
    c = pl.program_id(1)
    n = x_ref.shape[1]

    @pl.when(c == 0)
    def _():
        for g in range(SSM_GROUPS):
            ht_ref[g] = h0_ref[0, g].T
        xbuf_ref[0:CONV_PAD] = cbx_ref[0]
        bcbuf_ref[0:CONV_PAD] = cbbc_ref[0]

    xbuf_ref[CONV_PAD:CONV_PAD + n] = x_ref[0]
    bcbuf_ref[CONV_PAD:CONV_PAD + n] = bc_ref[0]
    xa_ref[...] = _conv_silu(xbuf_ref, wx_ref, bx_ref, n)
    bca_ref[...] = _conv_silu(bcbuf_ref, wbc_ref, bbc_ref, n)
    xbuf_ref[0:CONV_PAD] = xbuf_ref[n:n + CONV_PAD]
    bcbuf_ref[0:CONV_PAD] = bcbuf_ref[n:n + CONV_PAD]

    row = lax.broadcasted_iota(jnp.int32, (n, LANE), 0) + c * n
    dtr = dt_ref[0] + dtb_ref[...]
    dt = jnp.maximum(dtr, 0.0) + jnp.log1p(jnp.exp(-jnp.abs(dtr)))
    dt = jnp.where(row < t_valid, dt, 0.0)
    dta = dt * (-jnp.exp(alog_ref[...]))
    hi, mid, lo = _split3(dta)
    tri = tri_ref[...]
    cs = _dot(tri, hi) + _dot(tri, mid) + _dot(tri, lo)
    cs_last = cs[n - 1:n, :]
    cs_t = cs.T
    dt_t = dt.T
    stacked = jnp.concatenate(
        [jnp.exp(cs), jnp.exp(cs_last - cs) * dt, jnp.broadcast_to(jnp.exp(cs_last), (8, LANE))], axis=0)
    ex = _dot_exact_rhs(stacked, e_ref[...])
    causal = (lax.broadcasted_iota(jnp.int32, (n, n), 0) >= lax.broadcasted_iota(jnp.int32, (n, n), 1))

    for g in range(SSM_GROUPS):
        gs = slice(g * SSM_GW, (g + 1) * SSM_GW)
        bm = bca_ref[:, g * D_STATE:(g + 1) * D_STATE]
        cm = bca_ref[:, (SSM_GROUPS + g) * D_STATE:(SSM_GROUPS + g + 1) * D_STATE].astype(BF16)
        cb = _dot_t(cm, bm.astype(BF16))
        xg = xa_ref[:, gs]
        ys = []
        for r in range(SSM_HPG):
            h = g * SSM_HPG + r
            seg = cs[:, h:h + 1] - cs_t[h:h + 1, :]
            dec = jnp.where(causal, jnp.exp(jnp.where(causal, seg, 0.0)), 0.0)
            mm = (cb * dec * dt_t[h:h + 1, :]).astype(BF16)
            ys.append(_dot(mm, xg[:, r * SSM_HEADDIM:(r + 1) * SSM_HEADDIM].astype(BF16)))
        ht = ht_ref[g]
        y = jnp.concatenate(ys, axis=1) + _dot(cm, ht.astype(BF16)) * ex[0:n, gs]
        y = y + dsk_ref[:, gs] * xg
        zg = z_ref[0, :, gs]
        y = y * (zg * jax.nn.sigmoid(zg))
        y_ref[0, :, gs] = _rms(y, ng_ref[:, gs])
        xw = (xg * ex[n:2 * n, gs]).astype(BF16)
        ht_ref[g] = ht * ex[2 * n:2 * n + 1, gs] + _dot(bm.T.astype(BF16), xw)

    @pl.when(c == pl.num_programs(1) - 1)
    def _():
        for g in range(SSM_GROUPS):
            hout_ref[0, g] = ht_ref[g].T


def _head_expand_matrix():
    e = np.zeros((LANE, D_INNER), np.float32)
    for h in range(SSM_HEADS):
        e[h, h * SSM_HEADDIM:(h + 1) * SSM_HEADDIM] = 1.0
    return e


def ssd_core(zx, dt_raw, conv_buf, h0, conv_w, conv_b, dt_bias, a_log, d_skip, norm_g, t_valid):
    b, t, _ = zx.shape
    n = SSM_CHUNK
    pad_h = LANE - SSM_HEADS
    cb = jnp.pad(conv_buf, ((0, 0), (CONV_PAD - (CONV_W - 1), 0), (0, 0)))
    tri = jnp.asarray(np.tril(np.ones((n, n), np.float32)), BF16)
    e = jnp.asarray(_head_expand_matrix(), BF16)
    d_exp = jnp.repeat(d_skip, SSM_HEADDIM).reshape(1, D_INNER)
    fixed = lambda *shape: pl.BlockSpec(shape, lambda i, c: (0,) * len(shape))
    per_b = lambda *shape: pl.BlockSpec((1,) + shape, lambda i, c: (i,) + (0,) * len(shape))
    y, h_last = pl.pallas_call(
        functools.partial(_ssd_kernel, t_valid=t_valid),
        grid=(b, t // n),
        in_specs=[
            pl.BlockSpec((1, n, D_INNER), lambda i, c: (i, c, 0)),
            pl.BlockSpec((1, n, D_INNER), lambda i, c: (i, c, 1)),
            pl.BlockSpec((1, n, SSM_BC), lambda i, c: (i, c, 2 * D_INNER // SSM_BC)),
            pl.BlockSpec((1, n, LANE), lambda i, c: (i, c, 0)),
            per_b(CONV_PAD, D_INNER), per_b(CONV_PAD, SSM_BC),
            fixed(CONV_W, D_INNER), fixed(CONV_W, SSM_BC), fixed(1, D_INNER), fixed(1, SSM_BC),
            fixed(1, LANE), fixed(1, LANE), fixed(1, D_INNER), fixed(1, D_INNER),
            fixed(LANE, D_INNER), fixed(n, n),
            per_b(SSM_GROUPS, SSM_GW, D_STATE),
        ],
        out_specs=[
            pl.BlockSpec((1, n, D_INNER), lambda i, c: (i, c, 0)),
            per_b(SSM_GROUPS, SSM_GW, D_STATE),
        ],
        out_shape=[jax.ShapeDtypeStruct((b, t, D_INNER), F32),
                   jax.ShapeDtypeStruct((b, SSM_GROUPS, SSM_GW, D_STATE), F32)],
        scratch_shapes=[
            pltpu.VMEM((SSM_GROUPS, D_STATE, SSM_GW), F32),
            pltpu.VMEM((n + CONV_PAD, D_INNER), F32),
            pltpu.VMEM((n + CONV_PAD, SSM_BC), F32),
            pltpu.VMEM((n, D_INNER), F32),
            pltpu.VMEM((n, SSM_BC), F32),
        ],
        compiler_params=_params(("parallel", "arbitrary")),
        name="ssd_core",
    )(zx, zx, zx, dt_raw, cb[:, :, :D_INNER], cb[:, :, D_INNER:],
      conv_w[:, :D_INNER], conv_w[:, D_INNER:], conv_b[:D_INNER].reshape(1, -1), conv_b[D_INNER:].reshape(1, -1),
      jnp.pad(dt_bias, (0, pad_h)).reshape(1, LANE), jnp.pad(a_log, (0, pad_h)).reshape(1, LANE),
      d_exp, norm_g.reshape(1, D_INNER), e, tri,
      h0.reshape(b, SSM_GROUPS, SSM_GW, D_STATE))
    return y, h_last.reshape(b, SSM_HEADS, SSM_HEADDIM, D_STATE)


def ssd_layer(xp, xs, gain, w_in, layer, conv_w, conv_b, dt_bias, a_log, d_skip, norm_g, w_out,
              state_conv, state_ssm):
    nzx = D_INNER + CONV_DIM
    w_dt = jnp.pad(w_in[layer, :, nzx:], ((0, 0), (0, LANE - SSM_HEADS)))[None]
    w_in_t = jnp.swapaxes(w_in, 1, 2)
    outs = []
    for x, conv_buf, h0 in ((xp, None, None), (xs, state_conv, state_ssm)):
        b, t, d = x.shape
        x2 = x.reshape(b * t, d)
        if conv_buf is None:
            conv_buf = jnp.zeros((b, CONV_W - 1, CONV_DIM), F32)
            h0 = jnp.zeros((b, SSM_HEADS, SSM_HEADDIM, D_STATE), F32)
        zx = dense(x2, w_in_t, layer, gain=gain, n=nzx, w_t=True).reshape(b, t, -1)
        dt_raw = dense(x2, w_dt, 0, gain=gain).reshape(b, t, LANE)
        tp = -(-t // SSM_CHUNK) * SSM_CHUNK
        y, h_last = ssd_core(_pad_rows(zx, tp), _pad_rows(dt_raw, tp), conv_buf, h0, conv_w, conv_b,
                             dt_bias, a_log, d_skip, norm_g, t)
        y = y[:, :t].reshape(b * t, D_INNER)
        tail = min(t, CONV_W - 1)
        conv_out = jnp.concatenate([conv_buf[:, tail:], zx[:, t - tail:, D_INNER:]], axis=1)
        outs.append((dense(y, w_out, layer, res=x2).reshape(b, t, d), h_last, conv_out))
    (yp, hp, cp), (ys, hs, cs) = outs
    return yp, ys, hp, cp, hs, cs


NSA_KV = N_KV * HEAD_W
NSA_SCALE = HEAD_W ** -0.5
CMP_PAIRS = CMP_STRIDE // 2
CMP_PAGES = 32


def _cmp_uv_kernel(pt_ref, *refs):
    del pt_ref
    pages = refs[:-2]
    wab_ref, o_ref = refs[-2:]
    subs = PAGE // CMP_STRIDE
    for g in range(N_KV):
        acc = jnp.zeros((len(pages) * subs, 2 * HEAD_W), F32)
        for lp in range(CMP_PAIRS):
            halves = []
            for li in range(2):
                rows = pl.ds((2 * lp + li) * N_KV + g, subs, stride=CMP_STRIDE * N_KV)
                halves.append(jnp.concatenate([p[0, 0, rows, :] for p in pages], axis=0))
            acc = acc + _dot(jnp.concatenate(halves, axis=1).astype(BF16), wab_ref[lp])
        o_ref[0, :, g * 2 * HEAD_W:(g + 1) * 2 * HEAD_W] = acc


def cmp_uv(rows, layer, page_table, wab):
    b, n_pages = page_table.shape
    subs = PAGE // CMP_STRIDE
    step_pages = min(CMP_PAGES, n_pages)
    assert n_pages % step_pages == 0

    def spec(u):
        return pl.BlockSpec((1, 1, PAGE_ROWS, HEAD_W),
                            lambda i, c, pt: (layer, pt[i, c * step_pages + u], 0, 0))

    grid_spec = pltpu.PrefetchScalarGridSpec(
        num_scalar_prefetch=1,
        grid=(b, n_pages // step_pages),
        in_specs=[spec(u) for u in range(step_pages)]
        + [pl.BlockSpec(wab.shape, lambda i, c, pt: (0, 0, 0))],
        out_specs=pl.BlockSpec((1, step_pages * subs, 2 * NSA_KV), lambda i, c, pt: (i, c, 0)),
    )
    return pl.pallas_call(
        _cmp_uv_kernel,
        grid_spec=grid_spec,
        out_shape=jax.ShapeDtypeStruct((b, n_pages * subs, 2 * NSA_KV), F32),
        compiler_params=_params(("parallel", "arbitrary")),
        name="cmp_uv",
    )(page_table, *([rows] * step_pages), wab)


def _cmp_finish_kernel(uv_ref, pe_ref, w1_ref, w2_ref, kn_ref, o_ref, *, norm):
    n = uv_ref.shape[1]
    c = _dot(jnp.broadcast_to(pe_ref[...], (8, pe_ref.shape[1])).astype(BF16), w1_ref[...])[0:1]
    for g in range(N_KV):
        u = uv_ref[0, :, g * 2 * HEAD_W:g * 2 * HEAD_W + HEAD_W]
        v = uv_ref[0, :, g * 2 * HEAD_W + HEAD_W:(g + 1) * 2 * HEAD_W]
        pre = u + pltpu.roll(v, n - 1, axis=0) + c
        out = _dot((pre * jax.nn.sigmoid(pre)).astype(BF16), w2_ref[...])
        if norm:
            out = _rms(out, kn_ref[...])
        o_ref[0, :, g * HEAD_W:(g + 1) * HEAD_W] = out


def cmp_finish(uv, pe, w1, w2, k_norm):
    b, n, _ = uv.shape
    norm = k_norm is not None
    kn = (k_norm if norm else jnp.ones((HEAD_W,), F32)).reshape(1, HEAD_W)
    return pl.pallas_call(
        functools.partial(_cmp_finish_kernel, norm=norm),
        grid=(b,),
        in_specs=[
            pl.BlockSpec((1, n, 2 * NSA_KV), lambda i: (i, 0, 0)),
            pl.BlockSpec((1, pe.size), lambda i: (0, 0)),
            pl.BlockSpec(w1.shape, lambda i: (0, 0)),
            pl.BlockSpec(w2.shape, lambda i: (0, 0)),
            pl.BlockSpec((1, HEAD_W), lambda i: (0, 0)),
        ],
        out_specs=pl.BlockSpec((1, n, NSA_KV), lambda i: (i, 0, 0)),
        out_shape=jax.ShapeDtypeStruct((b, n, NSA_KV), F32),
        compiler_params=_params(("parallel",)),
        name="cmp_finish",
    )(uv, pe.reshape(1, -1), w1, w2, kn)


def compress(rows, layer, page_table, pe, w1, w2, k_norm):
    w1r = w1.reshape(2, CMP_STRIDE, HEAD_W, HEAD_W)
    wab = jnp.transpose(w1r, (1, 2, 0, 3)).reshape(CMP_PAIRS, 2 * HEAD_W, 2 * HEAD_W).astype(BF16)
    uv = cmp_uv(rows, layer, page_table, wab)
    return cmp_finish(uv, pe, w1.astype(BF16), w2.astype(BF16), k_norm)


def _importance_matrix(n_cmp, n_rows, n_slc, n_cols):
    w = np.zeros((n_rows, n_cols), np.float32)
    w_imp = [1.0] + [2.0] * (SLC_RATIO - 1) + [1.0]
    for s in range(n_slc):
        for m, wm in enumerate(w_imp):
            j = SLC_RATIO * s + m - 1
            if 0 <= j < n_cmp:
                w[j, s] += wm
    return w


def _select_blocks(s_slc, q_pos0, n_slc):
    t, w = s_slc.shape
    blk = lax.broadcasted_iota(jnp.int32, (t, w), 1)
    qpos = q_pos0 + lax.broadcasted_iota(jnp.int32, (t, w), 0)
    qb = qpos // SLC_BLOCK
    forced = (blk == 0) | (blk == qb) | (blk == qb - 1)
    score = jnp.where(forced, FORCE_SCORE, jnp.where(blk * SLC_BLOCK <= qpos, s_slc, -1.0))
    score = jnp.where(blk < n_slc, score, -2.0)
    cnt = jnp.zeros((t, w), F32)
    for sp in range(n_slc):
        col = score[:, sp:sp + 1]
        tie = jnp.where(blk > sp, 1.0, 0.0)
        cnt = cnt + jnp.where(col > score, 1.0, jnp.where(col == score, tie, 0.0))
    return jnp.where(cnt < N_SELECT, 1.0, 0.0)


def _masked_softmax(s, valid, axis=-1, base2=False):
    m = jnp.max(s, axis=axis, keepdims=True)
    e = jnp.where(valid, (jnp.exp2 if base2 else jnp.exp)(s - m), 0.0)
    return e / jnp.maximum(jnp.sum(e, axis=axis, keepdims=True), 1e-30)


def _reset_flash(m_ref, l_ref, acc_ref):
    m_ref[...] = jnp.full(m_ref.shape, NEG_INF, F32)
    l_ref[...] = jnp.zeros(l_ref.shape, F32)
    acc_ref[...] = jnp.zeros(acc_ref.shape, F32)


def _select_blocks_t(s_slc, q_pos0, n_slc):
    w, t = s_slc.shape
    n8 = -(-n_slc // 8) * 8
    blk = lax.broadcasted_iota(jnp.int32, (n8, t), 0)
    qpos = q_pos0 + lax.broadcasted_iota(jnp.int32, (n8, t), 1)
    qb = qpos // SLC_BLOCK
    forced = (blk == 0) | (blk == qb) | (blk == qb - 1)
    score = jnp.where(forced, FORCE_SCORE, jnp.where(blk * SLC_BLOCK <= qpos, s_slc[:n8], -1.0))
    score = jnp.where(blk < n_slc, score, -2.0)
    cnt = jnp.zeros((n8, t), F32)
    for sp in range(n_slc):
        row = score[sp:sp + 1, :]
        tie = jnp.where(blk > sp, 1.0, 0.0)
        cnt = cnt + jnp.where(row > score, 1.0, jnp.where(row == score, tie, 0.0))
    sel = jnp.where(cnt < N_SELECT, 1.0, 0.0)
    if n8 < w:
        sel = jnp.concatenate([sel, jnp.zeros((w - n8, t), F32)], axis=0)
    return sel


def _nsa_prompt_kernel(q_ref, gt_ref, kc_ref, vc_ref, ks_ref, vs_ref, kw_ref, vw_ref, bias_ref, u_ref, wimp_ref,
                       o_ref, qs_ref, ksb_ref, vst_ref, kwb_ref, vwt_ref, sel_ref, m_ref, l_ref, acc_ref,
                       *, tq, n_slc):
    qi = pl.program_id(2)

    @pl.when(qi == 0)
    def _():
        _stage_kv(ks_ref, vs_ref, ksb_ref, vst_ref, tq)
        _stage_kv(kw_ref, vw_ref, kwb_ref, vwt_ref, tq)

    qs_ref[...] = jnp.concatenate(
        [q_ref[0, :, r * HEAD_W:(r + 1) * HEAD_W] for r in range(HPG)], axis=0).astype(BF16)

    n_pad = kc_ref.shape[1]
    start = pl.multiple_of(n_pad - (tq // CMP_STRIDE) * (qi + 1), 8)
    scale = NSA_SCALE * LOG2E
    bias_c = u_ref[0, pl.ds(start, n_pad), :]
    s = _dot_t(kc_ref[0].astype(BF16), qs_ref[...]) * scale + bias_c
    pc = _masked_softmax(s, bias_c > 0.5 * NEG_INF, axis=0, base2=True)
    o_cmp = _dot(vc_ref[0].T.astype(BF16), pc.astype(BF16))
    psum = pc[:, 0:tq] + pc[:, tq:2 * tq] + pc[:, 2 * tq:3 * tq] + pc[:, 3 * tq:4 * tq]
    hi, mid, lo = _split3(psum)
    wimp = wimp_ref[...]
    s_slc = _dot(wimp, hi) + _dot(wimp, mid) + _dot(wimp, lo)
    sel_ref[...] = _select_blocks_t(s_slc, qi * tq, n_slc).astype(BF16)

    _reset_flash(m_ref, l_ref, acc_ref)

    far = bias_ref[2, 0:1, :]

    def key_mask(kt, width):
        key = lax.broadcasted_iota(jnp.int32, (width, LANE), 0)
        blk = lax.broadcasted_iota(jnp.int32, (width, LANE), 1)
        onehot = jnp.where(blk == kt * (width // SLC_BLOCK) + key // SLC_BLOCK, 1.0, 0.0).astype(BF16)
        return jnp.concatenate([_dot(onehot, sel_ref[...])] * HPG, axis=1)

    def slc_far_body(width):
        def body(kt, c):
            rows = pl.ds(pl.multiple_of(kt * width, width), width)
            _flash_tile_t(ksb_ref[rows, :], vst_ref[:, rows], qs_ref, m_ref, l_ref, acc_ref, scale=scale,
                          far_bias=far, mask=key_mask(kt, width))
            return c
        return body

    def slc_near_body(kt, c):
        rows = pl.ds(pl.multiple_of(kt * tq, tq), tq)
        typ = qi - kt
        _flash_tile_t(ksb_ref[rows, :], vst_ref[:, rows], qs_ref, m_ref, l_ref, acc_ref, scale=scale,
                      bias=lambda cols: bias_ref[typ, :, cols], mask=key_mask(kt, tq))
        return c

    n_far = jnp.maximum(qi - 1, 0)
    _far_loops(n_far, slc_far_body, tq)
    lax.fori_loop(n_far, qi + 1, slc_near_body, 0)
    o_slc = acc_ref[...] / l_ref[...]

    _reset_flash(m_ref, l_ref, acc_ref)
    nw = WINDOW // tq

    def win_body(kt, c):
        rows = pl.ds(pl.multiple_of(kt * tq, tq), tq)
        t = qi - kt
        typ = jnp.where(t == nw, 3, jnp.minimum(t, 2))
        _flash_tile_t(kwb_ref[rows, :], vwt_ref[:, rows], qs_ref, m_ref, l_ref, acc_ref, scale=scale,
                      bias=lambda cols: bias_ref[typ, :, cols])
        return c

    lax.fori_loop(jnp.maximum(qi - nw, 0), qi + 1, win_body, 0)
    o_win = acc_ref[...] / l_ref[...]

    sig = jax.nn.sigmoid(gt_ref[0]).T
    for r in range(HPG):
        cs = slice(r * tq, (r + 1) * tq)
        o_t = (sig[r:r + 1, :] * o_cmp[:, cs] + sig[HPG + r:HPG + r + 1, :] * o_slc[:, cs]
               + sig[2 * HPG + r:2 * HPG + r + 1, :] * o_win[:, cs])
        o_ref[0, :, r * HEAD_W:(r + 1) * HEAD_W] = o_t.T


def _prompt_cmp_buckets(tq, n_pad):
    jp = np.arange(2 * n_pad)[:, None] - (n_pad - tq // CMP_STRIDE)
    i = np.arange(tq)[None, :]
    return _bucket_np(i - CMP_STRIDE * jp - (2 * CMP_STRIDE - 1))[None].astype(np.int32)


def nsa_prompt(proj, gates, kc, vc, bias, u_bias, tq=PROMPT_TQ):
    b, t, _ = proj.shape
    gw = HPG * HEAD_W
    n_pad = kc.shape[1]
    n_slc = t // SLC_BLOCK
    wimp = jnp.asarray(_importance_matrix(n_pad - 1, n_pad, n_slc, LANE).T, BF16)
    rows = HPG * tq
    seq = lambda n: pl.BlockSpec((1, n, HEAD_W), lambda i, g, j: (i, 0, g))
    kv = lambda a: pl.BlockSpec((1, t, HEAD_W), lambda i, g, j: (i, 0, N_HEADS + a * N_KV + g))
    return pl.pallas_call(
        functools.partial(_nsa_prompt_kernel, tq=tq, n_slc=n_slc),
        grid=(b, N_KV, t // tq),
        in_specs=[
            pl.BlockSpec((1, tq, gw), lambda i, g, j: (i, j, g)),
            pl.BlockSpec((1, tq, LANE), lambda i, g, j: (i, j, g)),
            seq(n_pad), seq(n_pad), kv(2), kv(3), kv(4), kv(5),
            pl.BlockSpec((4, tq, rows), lambda i, g, j: (0, 0, g)),
            pl.BlockSpec((1, 2 * n_pad, rows), lambda i, g, j: (0, 0, g)),
            pl.BlockSpec(wimp.shape, lambda i, g, j: (0, 0)),
        ],
        out_specs=pl.BlockSpec((1, tq, gw), lambda i, g, j: (i, j, g)),
        out_shape=jax.ShapeDtypeStruct((b, t, N_HEADS * HEAD_W), F32),
        scratch_shapes=[
            pltpu.VMEM((rows, HEAD_W), BF16),
            pltpu.VMEM((t, HEAD_W), BF16),
            pltpu.VMEM((HEAD_W, t), BF16),
            pltpu.VMEM((t, HEAD_W), BF16),
            pltpu.VMEM((HEAD_W, t), BF16),
            pltpu.VMEM((LANE, tq), BF16),
            pltpu.VMEM((1, rows), F32),
            pltpu.VMEM((1, rows), F32),
            pltpu.VMEM((HEAD_W, rows), F32),
        ],
        compiler_params=_params(("parallel", "parallel", "arbitrary")),
        name="nsa_prompt",
    )(proj, gates, kc, vc, proj, proj, proj, proj, bias, u_bias, wimp)


def _decode_cmp_buckets(t_new, past_len):
    n_pad = past_len // CMP_STRIDE
    i = np.arange(t_new)[:, None]
    j = n_pad - LANE + np.arange(LANE)[None, :]
    last = _bucket_np(past_len + i - CMP_STRIDE * j - (2 * CMP_STRIDE - 1))
    last = np.where(j < n_pad - 1, last, -1)
    return np.stack([np.full((t_new, LANE), N_BUCKETS - 1, np.int32), last]).astype(np.int32)


def _decode_win_buckets(t_new, n_tiles):
    i = np.arange(t_new)[:, None]
    idx = np.arange(n_tiles * LANE)[None, :]
    dw = WINDOW + i - idx
    ok = (dw >= 0) & (dw < WINDOW) & (idx < WINDOW + t_new)
    b = np.where(ok, _bucket_np(dw), -1)
    return np.stack([b[:, k * LANE:(k + 1) * LANE] for k in range(n_tiles)]).astype(np.int32)


def _nsa_decode_kernel(pt_ref, q_ref, gt_ref, kc_ref, vc_ref, kn_ref, vn_ref, kw_ref, vw_ref, *rest,
                       n_pages, past_len):
    del pt_ref
    kp = rest[:PAGES_PER_STEP]
    vp = rest[PAGES_PER_STEP:2 * PAGES_PER_STEP]
    (bias_ref, bias_c_ref, bias_w_ref, wimp_ref, o_ref,
     qr_ref, selt_ref, ocmp_ref, owin_ref, m_ref, l_ref, acc_ref) = rest[2 * PAGES_PER_STEP:]
    s = pl.program_id(1)
    t_new = q_ref.shape[1]
    rows = HPG * t_new
    n_slc = -(-(past_len + t_new) // SLC_BLOCK)

    @pl.when(s == 0)
    def _():
        _reset_flash(m_ref, l_ref, acc_ref)
        qr_ref[...] = _query_rows(q_ref, 1, NSA_SCALE)
        n_pad = kc_ref.shape[1]
        sel_t = jnp.zeros(selt_ref.shape, F32)
        for g in range(N_KV):
            sl = slice(g * HEAD_W, (g + 1) * HEAD_W)
            gr = slice(g * rows, (g + 1) * rows)
            qx = _stack_group_heads(q_ref, g).astype(BF16)
            bias_c = jnp.concatenate([bias_c_ref[0, gr, :]] * (n_pad // LANE - 1) + [bias_c_ref[1, gr, :]], axis=1)
            sc = _dot_t(qx, kc_ref[0, :, sl].astype(BF16)) * NSA_SCALE + bias_c
            pc = _masked_softmax(sc, bias_c > 0.5 * NEG_INF)
            ocmp_ref[g] = _dot(pc.astype(BF16), vc_ref[0, :, sl].astype(BF16))
            psum = sum(pc[r * t_new:(r + 1) * t_new] for r in range(1, HPG)) + pc[0:t_new]
            s_slc = _dot_exact_rhs(psum, wimp_ref[...])
            sel = _select_blocks(s_slc, past_len, n_slc)
            sel_pad = jnp.concatenate([sel, jnp.zeros((LANE - t_new, sel.shape[1]), F32)], axis=0).T
            tok = lax.broadcasted_iota(jnp.int32, (LANE, LANE), 0)
            lane = lax.broadcasted_iota(jnp.int32, (LANE, LANE), 1)
            spread = jnp.where((lane % t_new == tok) & (lane // rows == g), 1.0, 0.0).astype(BF16)
            sel_t = sel_t + _dot(sel_pad.astype(BF16), spread)
            n_wt = bias_w_ref.shape[0]
            bias_w = jnp.concatenate([bias_w_ref[k, gr, :] for k in range(n_wt)], axis=1)
            sw = _dot_t(qx, kw_ref[0, :, sl].astype(BF16)) * NSA_SCALE + bias_w
            pw = _masked_softmax(sw, bias_w > 0.5 * NEG_INF)
            owin_ref[g] = _dot(pw.astype(BF16), vw_ref[0, :, sl].astype(BF16))
        selt_ref[...] = sel_t
        _decode_step_t(kn_ref[0], vn_ref[0], qr_ref, bias_ref[2], m_ref, l_ref, acc_ref)

    block_rows = SLC_BLOCK * N_KV
    first_block = s * (PAGES_PER_STEP * PAGE // SLC_BLOCK)
    keep = jnp.concatenate(
        [jnp.broadcast_to(selt_ref[pl.ds(first_block + b, 1), :], (block_rows, selt_ref.shape[1]))
         for b in range(PAGES_PER_STEP * PAGE // SLC_BLOCK)], axis=0)
    _decode_step_t(_page_rows(kp), _page_rows(vp), qr_ref,
                   _page_bias_t(bias_ref, s * PAGES_PER_STEP, n_pages, 1), m_ref, l_ref, acc_ref, keep=keep)

    @pl.when(s == pl.num_programs(1) - 1)
    def _():
        sig = jax.nn.sigmoid(gt_ref[0])
        o_slc_all = (acc_ref[...] / l_ref[...]).T
        for g in range(N_KV):
            o_cmp = ocmp_ref[g]
            o_win = owin_ref[g]
            for r in range(HPG):
                h = g * HPG + r
                rs = slice(r * t_new, (r + 1) * t_new)
                c0 = g * LANE + r
                o_ref[0, :, h * HEAD_W:(h + 1) * HEAD_W] = (
                    sig[:, c0:c0 + 1] * o_cmp[rs]
                    + sig[:, c0 + HPG:c0 + HPG + 1] * o_slc_all[h * t_new:(h + 1) * t_new]
                    + sig[:, c0 + 2 * HPG:c0 + 2 * HPG + 1] * o_win[rs])


def nsa_decode(q, gates, kc, vc, k_new, v_new, kw_src, vw_src, cache_k, cache_v, layer, page_table,
               bias, bias_c, bias_w, past_len):
    b, t_new, _ = q.shape
    n_pages = page_table.shape[1]
    rows = HPG * t_new
    lanes = N_HEADS * t_new
    assert lanes == LANE
    n_pad = kc.shape[1]
    n_slc = -(-(past_len + t_new) // SLC_BLOCK)
    n_cols = -(-n_slc // LANE) * LANE
    wimp = jnp.asarray(_importance_matrix(n_pad - 1, n_pad, n_slc, n_cols), BF16)
    fixed = lambda *shape: pl.BlockSpec(shape, lambda i, s, pt: (0,) * len(shape))
    per_b = lambda *shape: pl.BlockSpec((1,) + shape, lambda i, s, pt: (i,) + (0,) * len(shape))
    grid_spec = pltpu.PrefetchScalarGridSpec(
        num_scalar_prefetch=1,
        grid=(b, n_pages // PAGES_PER_STEP),
        in_specs=[per_b(t_new, N_HEADS * HEAD_W), per_b(t_new, N_KV * LANE),
                  per_b(n_pad, NSA_KV), per_b(n_pad, NSA_KV), per_b(PAGE_ROWS, HEAD_W), per_b(PAGE_ROWS, HEAD_W),
                  per_b(kw_src.shape[1], NSA_KV), per_b(kw_src.shape[1], NSA_KV)]
        + _page_specs(layer, 2)
        + [fixed(*bias.shape), fixed(*bias_c.shape), fixed(*bias_w.shape), fixed(*wimp.shape)],
        out_specs=per_b(t_new, N_HEADS * HEAD_W),
        scratch_shapes=[
            pltpu.VMEM((lanes, HEAD_W), BF16),
            pltpu.VMEM((n_cols, lanes), F32),
            pltpu.VMEM((N_KV, rows, HEAD_W), F32),
            pltpu.VMEM((N_KV, rows, HEAD_W), F32),
            pltpu.VMEM((1, lanes), F32),
            pltpu.VMEM((1, lanes), F32),
            pltpu.VMEM((HEAD_W, lanes), F32),
        ],
    )
    return pl.pallas_call(
        functools.partial(_nsa_decode_kernel, n_pages=n_pages, past_len=past_len),
        grid_spec=grid_spec,
        out_shape=jax.ShapeDtypeStruct(q.shape[:2] + (N_HEADS * HEAD_W,), F32),
        compiler_params=_params(("parallel", "arbitrary")),
        name="nsa_decode",
    )(page_table, q, gates, kc, vc, k_new, v_new, kw_src, vw_src,
      *([cache_k] * PAGES_PER_STEP), *([cache_v] * PAGES_PER_STEP), bias, bias_c, bias_w, wimp)


def _gate_weights(w_g):
    d = w_g.shape[0]
    w = jnp.transpose(w_g.reshape(d, 3, N_KV, HPG), (0, 2, 1, 3)).reshape(d, N_KV, 3 * HPG)
    return jnp.pad(w, ((0, 0), (0, 0), (0, LANE - 3 * HPG))).reshape(d, N_KV * LANE)


def nsa_layer(xp, xs, gain, w_in, q_norm, k_norm, pe, w1, w2, w_out, caches, layer, page_table,
              bias_p, u_bias, bias_s, bias_sc, bias_sw, past_len):
    cmp_k, cmp_v, slc_k, slc_v, win_k, win_v = caches
    nqkv = N_HEADS * HEAD_W + 6 * NSA_KV
    w_g = _gate_weights(w_in[layer, :, nqkv:])[None]
    w_in_t = jnp.swapaxes(w_in, 1, 2)
    nq = N_HEADS * HEAD_W
    tile = 4 * LANE
    q_gain = jnp.tile(q_norm, tile // HEAD_W)
    head_norm = (HEAD_W, {**{c: q_gain for c in range(nq // tile)},
                          nq // tile + 2: jnp.tile(k_norm[1], tile // HEAD_W),
                          nq // tile + 4: jnp.tile(k_norm[2], tile // HEAD_W)})
    outs = []
    for x, paged in ((xp, False), (xs, True)):
        b, t, d = x.shape
        x2 = x.reshape(b * t, d)
        proj2 = dense(x2, w_in_t, layer, gain=gain, n=nqkv, w_t=True, head_norm=head_norm)
        proj = proj2.reshape(b, t, -1)
        gates = dense(x2, w_g, 0, gain=gain).reshape(b, t, -1)
        rows = [group_rows(proj2, nq // NSA_KV + a) for a in range(4)]
        as_state = lambda r: r.reshape(b, t, N_KV, HEAD_W)
        kw, vw = proj[:, :, nq + 4 * NSA_KV:nq + 5 * NSA_KV], proj[:, :, nq + 5 * NSA_KV:]
        if paged:
            kc = compress(_as_page_rows(cmp_k), layer, page_table, pe[0], w1[0], w2[0], k_norm[0])
            vc = compress(_as_page_rows(cmp_v), layer, page_table, pe[1], w1[1], w2[1], None)
            n_wt = bias_sw.shape[0]
            kw_src = jnp.concatenate([win_k[layer], kw], axis=1)
            vw_src = jnp.concatenate([win_v[layer], vw], axis=1)
            o = nsa_decode(proj, gates, kc, vc, _new_page_rows(rows[2], b), _new_page_rows(rows[3], b),
                           _pad_rows(kw_src, n_wt * LANE), _pad_rows(vw_src, n_wt * LANE),
                           _as_page_rows(slc_k), _as_page_rows(slc_v), layer, page_table,
                           bias_s, bias_sc, bias_sw, past_len)
            kw_win, vw_win = kw_src, vw_src
        else:
            n_pg = t // PAGE
            ident = jnp.arange(b * n_pg, dtype=jnp.int32).reshape(b, n_pg)
            as_pages = lambda r: r.reshape(1, b * n_pg, PAGE_ROWS, HEAD_W)
            kc = compress(as_pages(rows[0]), 0, ident, pe[0], w1[0], w2[0], k_norm[0])
            vc = compress(as_pages(rows[1]), 0, ident, pe[1], w1[1], w2[1], None)
            o = nsa_prompt(proj, gates, kc, vc, bias_p, u_bias)
            kw_win, vw_win = kw, vw
        y = dense(o.reshape(b * t, -1), w_out, layer, res=x2).reshape(b, t, d)
        win = tuple(a[:, -WINDOW:].reshape(b, WINDOW, N_KV, HEAD_W) for a in (kw_win, vw_win))
        st = tuple(as_state(r) for r in rows) + win
        outs.append((y, st))
    (yp, stp), (ys, sts) = outs
    return yp, ys, stp, sts


def kernel(x_prompt, x_sample, cache_diff_k, cache_diff_v, state_ssm, state_conv, cache_nsa_cmp_k, cache_nsa_cmp_v, cache_nsa_slc_k, cache_nsa_slc_v, cache_nsa_win_k, cache_nsa_win_v, cache_mem_k, cache_mem_v, page_table, mem_prompt, rel_bias_table, norm_mix, norm_xattn, norm_mem, norm_ffn, diff_w_in, diff_q_norm, diff_k_norm, diff_lambda, diff_sub_norm, diff_w_out, ssm_w_in, ssm_conv_w, ssm_conv_b, ssm_dt_bias, ssm_a_log, ssm_d, ssm_norm, ssm_w_out, nsa_w_in, nsa_q_norm, nsa_k_norm, nsa_cmp_pe, nsa_cmp_w1, nsa_cmp_w2, nsa_w_out, xattn_w_q, xattn_w_k, xattn_w_v, xattn_q_norm, xattn_k_norm, xattn_w_o, ffn_w1, ffn_w3, ffn_w2):
    xp, xs = x_prompt, x_sample
    bp, t, d = xp.shape
    bs, t_new, _ = xs.shape
    past_len = page_table.shape[1] * PAGE
    depth = norm_mix.shape[0]

    bias_p = bias_tiles(_prompt_attn_buckets(PROMPT_TQ), rel_bias_table, heads_on_lanes=True, scale=LOG2E)
    u_bias = bias_tiles(_prompt_cmp_buckets(PROMPT_TQ, t // CMP_STRIDE), rel_bias_table, heads_on_lanes=True,
                        scale=LOG2E)
    bias_s = bias_tiles_indexed(_decode_page_index(t_new, past_len), rel_bias_table)
    bias_sc = bias_tiles(_decode_cmp_buckets(t_new, past_len), rel_bias_table)
    n_wt = -(-(WINDOW + t_new) // LANE)
    bias_sw = bias_tiles(_decode_win_buckets(t_new, n_wt), rel_bias_table)

    mem_k, mem_v = mem_kv(mem_prompt, norm_mem, xattn_w_k, xattn_w_v, xattn_k_norm)
    xw = X_HEADS * X_DH
    win_k = cache_nsa_win_k.reshape(*cache_nsa_win_k.shape[:3], NSA_KV)
    win_v = cache_nsa_win_v.reshape(*cache_nsa_win_v.shape[:3], NSA_KV)

    dkp, dvp, dks, dvs = [], [], [], []
    ssp, cvp, sss, cvs = [], [], [], []
    nsp, nss = [], []
    for i in range(depth):
        kind, j = i % 3, i // 3
        if kind == 0:
            lam_init = 0.8 - 0.6 * math.exp(-0.3 * i)
            xp, xs, kp_, vp_, ks_, vs_ = diff_layer(
                xp, xs, norm_mix[i], diff_w_in, diff_q_norm[j], diff_k_norm[j], diff_lambda[j],
                diff_sub_norm[j], diff_w_out, cache_diff_k, cache_diff_v, j, page_table,
                bias_p, bias_s, lam_init)
            dkp.append(kp_)
            dvp.append(vp_)
            dks.append(ks_)
            dvs.append(vs_)
        elif kind == 1:
            xp, xs, hp_, cp_, hs_, cs_ = ssd_layer(
                xp, xs, norm_mix[i], ssm_w_in, j, ssm_conv_w[j], ssm_conv_b[j], ssm_dt_bias[j], ssm_a_log[j],
                ssm_d[j], ssm_norm[j], ssm_w_out, state_conv[j], state_ssm[j])
            ssp.append(hp_)
            cvp.append(cp_)
            sss.append(hs_)
            cvs.append(cs_)
        else:
            caches = (cache_nsa_cmp_k, cache_nsa_cmp_v, cache_nsa_slc_k, cache_nsa_slc_v, win_k, win_v)
            xp, xs, stp, sts = nsa_layer(
                xp, xs, norm_mix[i], nsa_w_in, nsa_q_norm[j], nsa_k_norm[j], nsa_cmp_pe[j], nsa_cmp_w1[j],
                nsa_cmp_w2[j], nsa_w_out, caches, j, page_table,
                bias_p, u_bias, bias_s, bias_sc, bias_sw, past_len)
            nsp.append(stp)
            nss.append(sts)
        xp = xattn(xp, norm_xattn[i], xattn_w_q, xattn_q_norm[i], mem_k[i], mem_v[i], xattn_w_o, i)
        xs = xattn(xs, norm_xattn[i], xattn_w_q, xattn_q_norm[i], cache_mem_k[i].reshape(bs, N_MEM, xw),
                   cache_mem_v[i].reshape(bs, N_MEM, xw), xattn_w_o, i)
        xp = ffn(xp.reshape(bp * t, d), norm_ffn[i], ffn_w1, ffn_w3, ffn_w2, i).reshape(bp, t, d)
        xs = ffn(xs.reshape(bs * t_new, d), norm_ffn[i], ffn_w1, ffn_w3, ffn_w2, i).reshape(bs, t_new, d)

    st = lambda xs_: jnp.stack(xs_, axis=0)
    nsp_t = [st([s[a] for s in nsp]) for a in range(6)]
    nss_t = [st([s[a] for s in nss]) for a in range(6)]
    mem_shape = (depth, bp, N_MEM, X_HEADS, X_DH)
    return (xp, xs,
            st(dkp), st(dvp), st(dks), st(dvs),
            st(ssp), st(cvp), st(sss), st(cvs),
            *nsp_t, *nss_t,
            mem_k.reshape(mem_shape), mem_v.reshape(mem_shape))
```
